```python
import math
import jax
import jax.numpy as jnp
from jax import lax
import numpy as np

D_MODEL = 1024
BATCH = 1
SEQ = 16384
DEPTH = 4

GRID_W = 64
CTX_LEN = 256
HEAD_DIM = 64
ATT_HEADS = 8
ATT_KV_HEADS = 2
ATT_GROUP = ATT_HEADS // ATT_KV_HEADS
WINDOW = 128
ATT_BLOCK = 128
ROPE_BASE = 10000.0
RET_HEADS = 4
RET_DK = 64
RET_DV = 64
RET_CHUNK = 128
RET_DECAY_EXP0 = 5
S5_WIDTH = 256
S5_CH = 16
S5_GROUPS = S5_WIDTH // S5_CH
S5_STATE = 64
ATT_WIDTH = ATT_HEADS * HEAD_DIM
RET_WIDTH = RET_HEADS * RET_DV
MIX_WIDTH = ATT_WIDTH + RET_WIDTH + S5_WIDTH
IN_SPLIT = (ATT_WIDTH, ATT_KV_HEADS * HEAD_DIM, ATT_KV_HEADS * HEAD_DIM,
            RET_HEADS * RET_DK, RET_HEADS * RET_DK, RET_WIDTH, RET_WIDTH, S5_WIDTH)
IN_WIDTH = ATT_WIDTH + 2 * ATT_KV_HEADS * HEAD_DIM + 2 * RET_HEADS * RET_DK + 2 * RET_WIDTH + S5_WIDTH
D_FF = 4 * D_MODEL
N_ADA = 6
LN_EPS = 1e-5
GN_EPS = 1e-5
DEEPNORM_ALPHA = (2 * DEPTH) ** 0.25
DEEPNORM_BETA = (8 * DEPTH) ** -0.25

kernel_name = 'hybrid_diffusion_attn_retention_s5_trunk'


def layer_norm(x, g, b):
    xf = x.astype(jnp.float32)
    mu = xf.mean(-1, keepdims=True)
    var = jnp.square(xf - mu).mean(-1, keepdims=True)
    y = (xf - mu) * lax.rsqrt(var + LN_EPS)
    return (y * g.astype(jnp.float32) + b.astype(jnp.float32)).astype(x.dtype)


def split_projection(p):
    out, start = [], 0
    for size in IN_SPLIT:
        out.append(p[..., start:start + size])
        start += size
    return out


def to_heads(t, nh):
    return t.reshape(t.shape[0], t.shape[1], nh, -1).transpose(0, 2, 1, 3)


def merge_heads(t):
    return t.transpose(0, 2, 1, 3).reshape(t.shape[0], t.shape[2], -1)


def to_gqa(t):
    return t.reshape(t.shape[0], t.shape[1], ATT_KV_HEADS, ATT_GROUP, HEAD_DIM).transpose(0, 2, 3, 1, 4)


def merge_gqa(t):
    return t.transpose(0, 3, 1, 2, 4).reshape(t.shape[0], t.shape[3], -1)


def rope_angles(pos, dim):
    half = dim // 2
    inv = ROPE_BASE ** (-jnp.arange(half, dtype=jnp.float32) / half)
    ang = pos.astype(jnp.float32)[:, None] * inv[None, :]
    return jnp.cos(ang), jnp.sin(ang)


def apply_rotary(x, cos, sin):
    x1, x2 = jnp.split(x, 2, axis=-1)
    cos = cos.astype(x.dtype)
    sin = sin.astype(x.dtype)
    return jnp.concatenate([x1 * cos - x2 * sin, x1 * sin + x2 * cos], axis=-1)


def axial_rope_2d(x, rows, cols):
    half = x.shape[-1] // 2
    cr, sr = rope_angles(rows, half)
    cc, sc = rope_angles(cols, half)
    return jnp.concatenate([apply_rotary(x[..., :half], cr, sr), apply_rotary(x[..., half:], cc, sc)], axis=-1)


def softmax_with_sink(logits, sink):
    col = jnp.broadcast_to(sink, logits.shape[:-1] + (1,))
    return jax.nn.softmax(jnp.concatenate([col, logits], axis=-1), axis=-1)[..., 1:]


def banded_attention(q, k, v, k_ctx, v_ctx, sink):
    bsz, kvh, grp, L, d = q.shape
    nb = L // ATT_BLOCK
    scale = d ** -0.5
    qb = q.reshape(bsz, kvh, grp, nb, ATT_BLOCK, d)
    pad = ((0, 0), (0, 0), (ATT_BLOCK, ATT_BLOCK), (0, 0))

    def band(t):
        tp = jnp.pad(t, pad)
        return jnp.concatenate(
            [tp[:, :, s * ATT_BLOCK:s * ATT_BLOCK + L].reshape(bsz, kvh, nb, ATT_BLOCK, d) for s in range(3)],
            axis=3)

    kb, vb = band(k), band(v)
    qi = jnp.arange(ATT_BLOCK)[:, None]
    kj = jnp.arange(3 * ATT_BLOCK)[None, :]
    kpos = (jnp.arange(nb) * ATT_BLOCK)[:, None, None] + kj[None] - ATT_BLOCK
    valid = (jnp.abs(kj - ATT_BLOCK - qi)[None] <= WINDOW) & (kpos >= 0) & (kpos < L)
    s_loc = jnp.einsum('bkgnqd,bknjd->bkgnqj', qb, kb).astype(jnp.float32) * scale
    s_loc = jnp.where(valid, s_loc, -jnp.inf)
    s_ctx = jnp.einsum('bkgnqd,bkjd->bkgnqj', qb, k_ctx).astype(jnp.float32) * scale
    p = softmax_with_sink(jnp.concatenate([s_loc, s_ctx], axis=-1), sink[None, :, :, None, None, None])
    p_loc = p[..., :3 * ATT_BLOCK].astype(v.dtype)
    p_ctx = p[..., 3 * ATT_BLOCK:].astype(v.dtype)
    o = (jnp.einsum('bkgnqj,bknjd->bkgnqd', p_loc, vb)
         + jnp.einsum('bkgnqj,bkjd->bkgnqd', p_ctx, v_ctx))
    return o.reshape(bsz, kvh, grp, L, d)


def context_attention(q, k_ctx, v_ctx, sink):
    s = jnp.einsum('bkgqd,bkjd->bkgqj', q, k_ctx).astype(jnp.float32) * (q.shape[-1] ** -0.5)
    p = softmax_with_sink(s, sink[None, :, :, None, None]).astype(v_ctx.dtype)
    return jnp.einsum('bkgqj,bkjd->bkgqd', p, v_ctx)


def attention_mixer(q_x, k_x, v_x, q_c, k_c, v_c, sink, rows, cols, ctx_out):
    sink = sink.astype(jnp.float32).reshape(ATT_KV_HEADS, ATT_GROUP)
    q = axial_rope_2d(to_gqa(q_x), rows, cols)
    k = axial_rope_2d(to_heads(k_x, ATT_KV_HEADS), rows, cols)
    v = to_heads(v_x, ATT_KV_HEADS)
    kc = to_heads(k_c, ATT_KV_HEADS)
    vc = to_heads(v_c, ATT_KV_HEADS)
    y_x = merge_gqa(banded_attention(q, k, v, kc, vc, sink))
    y_c = merge_gqa(context_attention(to_gqa(q_c), kc, vc, sink)) if ctx_out else None
    return y_x, y_c


def retention_chunked(q, k, v, log_gamma, state0, inclusive):
    f32 = jnp.float32
    bsz, nh, L, dk = q.shape
    dv = v.shape[-1]
    nc = L // RET_CHUNK
    qc = q.astype(f32).reshape(bsz, nh, nc, RET_CHUNK, dk)
    kc = k.astype(f32).reshape(bsz, nh, nc, RET_CHUNK, dk)
    vc = v.astype(f32).reshape(bsz, nh, nc, RET_CHUNK, dv)
    pos = jnp.arange(RET_CHUNK, dtype=f32)
    diff = pos[:, None] - pos[None, :]
    mask = (diff >= 0) if inclusive else (diff > 0)
    decay = jnp.where(mask[None], jnp.exp(jnp.where(mask, diff, 0.0)[None] * log_gamma[:, None, None]), 0.0)
    scores = jnp.einsum('bhnid,bhnjd->bhnij', qc, kc) * decay[None, :, None]
    o_intra = jnp.einsum('bhnij,bhnje->bhnie', scores, vc)
    k_w = jnp.exp((RET_CHUNK - 1 - pos)[None, :] * log_gamma[:, None])
    u = jnp.einsum('bhnjd,bhnje->nbhde', kc * k_w[None, :, None, :, None], vc)
    g_chunk = jnp.exp(RET_CHUNK * log_gamma)[None, :, None, None]

    def step(s, u_n):
        return g_chunk * s + u_n, s

    s_final, s_prev = lax.scan(step, state0.astype(f32), u)
    q_w = jnp.exp((pos + 1.0)[None, :] * log_gamma[:, None])
    o_cross = jnp.einsum('bhnid,nbhde->bhnie', qc * q_w[None, :, None, :, None], s_prev)
    return (o_intra + o_cross).reshape(bsz, nh, L, dv), s_final


def retention_final_state(k, v, log_gamma):
    L = k.shape[2]
    w = jnp.exp((L - 1 - jnp.arange(L, dtype=jnp.float32))[None, :] * log_gamma[:, None])
    return jnp.einsum('bhld,bhle->bhde', k.astype(jnp.float32) * w[None, :, :, None], v.astype(jnp.float32))


def retention_output(o, g):
    mu = o.mean(-1, keepdims=True)
    var = jnp.square(o - mu).mean(-1, keepdims=True)
    o = (o - mu) * lax.rsqrt(var + GN_EPS)
    return merge_heads(o).astype(g.dtype) * jax.nn.silu(g)


def retention_mixer(q_x, k_x, v_x, g_x, q_c, k_c, v_c, g_c, decay_logit, t_pos, ctx_out):
    log_gamma = jax.nn.log_sigmoid(decay_logit.astype(jnp.float32))
    scale = RET_DK ** -0.5
    cos, sin = rope_angles(t_pos, RET_DK)
    q = apply_rotary(to_heads(q_x, RET_HEADS), cos, sin) * scale
    k = apply_rotary(to_heads(k_x, RET_HEADS), cos, sin)
    v = to_heads(v_x, RET_HEADS)
    qc = to_heads(q_c, RET_HEADS) * scale
    kc = to_heads(k_c, RET_HEADS)
    vc = to_heads(v_c, RET_HEADS)
    flip = lambda t: jnp.flip(t, axis=2)
    zero = jnp.zeros((q.shape[0], RET_HEADS, RET_DK, RET_DV), jnp.float32)
    if ctx_out:
        oc_f, s_f = retention_chunked(qc, kc, vc, log_gamma[0], zero, True)
        oc_b, s_b = retention_chunked(flip(qc), flip(kc), flip(vc), log_gamma[1], zero, False)
        y_c = retention_output(oc_f + flip(oc_b), g_c)
    else:
        s_f = retention_final_state(kc, vc, log_gamma[0])
        s_b = retention_final_state(flip(kc), flip(vc), log_gamma[1])
        y_c = None
    ox_f, _ = retention_chunked(q, k, v, log_gamma[0], s_f, True)
    ox_b, _ = retention_chunked(flip(q), flip(k), flip(v), log_gamma[1], s_b, False)
    y_x = retention_output(ox_f + flip(ox_b), g_x)
    return y_x, y_c


def s5_discretize(lam_re, lam_im, b_re, b_im, log_dt):
    lam = lax.complex(lam_re.astype(jnp.float32), lam_im.astype(jnp.float32))
    dt = jnp.exp(log_dt.astype(jnp.float32))[:, None]
    lam_bar = jnp.exp(lam * dt)
    b = lax.complex(b_re.astype(jnp.float32), b_im.astype(jnp.float32))
    return lam_bar, ((lam_bar - 1.0) / lam)[..., None] * b


def s5_drive(b_bar, u):
    return jnp.einsum('gnp,blgp->blgn', b_bar, u)


def s5_scan(lam_bar, bu, s0):
    bu = bu.at[:, 0].add(lam_bar[None] * s0)
    a = jnp.broadcast_to(lam_bar, bu.shape)

    def combine(e1, e2):
        a1, b1 = e1
        a2, b2 = e2
        return a1 * a2, a2 * b1 + b2

    _, states = lax.associative_scan(combine, (a, bu), axis=1)
    return states


def s5_read(cmat, s):
    return jnp.real(jnp.einsum('gpn,blgn->blgp', cmat, s))


def s5_mixer(u_x, u_c, lam_re, lam_im, b_re, b_im, c_re, c_im, log_dt, d_skip, w_glu, b_glu, ctx_out):
    f32 = jnp.float32
    bsz = u_x.shape[0]
    ux = u_x.astype(f32).reshape(bsz, u_x.shape[1], S5_GROUPS, S5_CH)
    uc = u_c.astype(f32).reshape(bsz, u_c.shape[1], S5_GROUPS, S5_CH)
    lam_f, bbar_f = s5_discretize(lam_re[0], lam_im[0], b_re[0], b_im[0], log_dt[0])
    lam_b, bbar_b = s5_discretize(lam_re[1], lam_im[1], b_re[1], b_im[1], log_dt[1])
    c_f = lax.complex(c_re[0].astype(f32), c_im[0].astype(f32))
    c_b = lax.complex(c_re[1].astype(f32), c_im[1].astype(f32))
    s0 = jnp.zeros((bsz, S5_GROUPS, S5_STATE), jnp.complex64)
    sc_f = s5_scan(lam_f, s5_drive(bbar_f, uc), s0)
    sc_b = s5_scan(lam_b, s5_drive(bbar_b, jnp.flip(uc, 1)), s0)
    sx_f = s5_scan(lam_f, s5_drive(bbar_f, ux), sc_f[:, -1])
    sx_b = jnp.flip(s5_scan(lam_b, s5_drive(bbar_b, jnp.flip(ux, 1)), sc_b[:, -1]), 1)
    d_g = d_skip.astype(f32).reshape(S5_GROUPS, S5_CH)

    def readout(u, s_f, s_b):
        y = s5_read(c_f, s_f) + s5_read(c_b, s_b) + d_g * u
        h = jax.nn.gelu(y.reshape(u.shape[0], u.shape[1], S5_WIDTH))
        gate = jax.nn.sigmoid(h @ w_glu.astype(f32) + b_glu.astype(f32))
        return (h * gate).astype(u_x.dtype)

    y_x = readout(ux, sx_f, sx_b)
    y_c = readout(uc, sc_f, jnp.flip(sc_b, 1)) if ctx_out else None
    return y_x, y_c


def squared_relu_mlp(h, w1, w2):
    return jnp.square(jax.nn.relu(h @ w1)) @ w2


def setup_inputs(seed: int = 0) -> dict:
    key = jax.random.key(seed)
    ks = jax.random.split(key, 26)
    f32 = jnp.float32
    nrm = lambda k, shape, s: jax.random.normal(k, shape, f32) * s
    G, N, P = S5_GROUPS, S5_STATE, S5_CH
    x = nrm(ks[0], (BATCH, SEQ, D_MODEL), 1.0)
    c = nrm(ks[1], (BATCH, D_MODEL), 1.0)
    ctx = nrm(ks[2], (BATCH, CTX_LEN, D_MODEL), 1.0)
    c_ctx = nrm(ks[3], (D_MODEL,), 1.0)
    w_ada = nrm(ks[4], (DEPTH, D_MODEL, N_ADA * D_MODEL), 0.5 * D_MODEL ** -0.5)
    b_ada = nrm(ks[5], (DEPTH, N_ADA * D_MODEL), 0.02)
    w_in = nrm(ks[6], (DEPTH, D_MODEL, IN_WIDTH), D_MODEL ** -0.5)
    att_sink = nrm(ks[7], (DEPTH, ATT_HEADS), 0.5)
    ret_base = jnp.log(2.0 ** (RET_DECAY_EXP0 + jnp.arange(RET_HEADS, dtype=f32)) - 1.0)
    ret_decay_logit = ret_base + nrm(ks[8], (DEPTH, 2, RET_HEADS), 0.05)
    s5_lambda_re = -0.5 + nrm(ks[9], (DEPTH, 2, G, N), 0.01)
    s5_lambda_im = math.pi * jnp.arange(N, dtype=f32) + nrm(ks[10], (DEPTH, 2, G, N), 0.01)
    s5_b_re = nrm(ks[11], (DEPTH, 2, G, N, P), (2 * P) ** -0.5)
    s5_b_im = nrm(ks[12], (DEPTH, 2, G, N, P), (2 * P) ** -0.5)
    s5_c_re = nrm(ks[13], (DEPTH, 2, G, P, N), (2 * N) ** -0.5)
    s5_c_im = nrm(ks[14], (DEPTH, 2, G, P, N), (2 * N) ** -0.5)
    s5_log_dt = jax.random.uniform(ks[15], (DEPTH, 2, G), f32, math.log(1e-3), math.log(1e-1))
    s5_d = nrm(ks[16], (DEPTH, S5_WIDTH), 1.0)
    w_glu = nrm(ks[17], (DEPTH, S5_WIDTH, S5_WIDTH), S5_WIDTH ** -0.5)
    b_glu = nrm(ks[18], (DEPTH, S5_WIDTH), 0.02)
    w_out = nrm(ks[19], (DEPTH, MIX_WIDTH, D_MODEL), DEEPNORM_BETA * MIX_WIDTH ** -0.5)
    ln1_g = 1.0 + nrm(ks[20], (DEPTH, D_MODEL), 0.02)
    ln1_b = nrm(ks[21], (DEPTH, D_MODEL), 0.02)
    w_ff1 = nrm(ks[22], (DEPTH, D_MODEL, D_FF), D_MODEL ** -0.5)
    w_ff2 = nrm(ks[23], (DEPTH, D_FF, D_MODEL), DEEPNORM_BETA * D_FF ** -0.5)
    ln2_g = 1.0 + nrm(ks[24], (DEPTH, D_MODEL), 0.02)
    ln2_b = nrm(ks[25], (DEPTH, D_MODEL), 0.02)
    return {'x': x, 'c': c, 'ctx': ctx, 'c_ctx': c_ctx, 'w_ada': w_ada, 'b_ada': b_ada, 'w_in': w_in,
            'att_sink': att_sink, 'ret_decay_logit': ret_decay_logit,
            's5_lambda_re': s5_lambda_re, 's5_lambda_im': s5_lambda_im, 's5_b_re': s5_b_re, 's5_b_im': s5_b_im,
            's5_c_re': s5_c_re, 's5_c_im': s5_c_im, 's5_log_dt': s5_log_dt, 's5_d': s5_d,
            'w_glu': w_glu, 'b_glu': b_glu, 'w_out': w_out, 'ln1_g': ln1_g, 'ln1_b': ln1_b,
            'w_ff1': w_ff1, 'w_ff2': w_ff2, 'ln2_g': ln2_g, 'ln2_b': ln2_b}


def reference(x, c, ctx, c_ctx, w_ada, b_ada, w_in, att_sink, ret_decay_logit,
              s5_lambda_re, s5_lambda_im, s5_b_re, s5_b_im, s5_c_re, s5_c_im, s5_log_dt, s5_d,
              w_glu, b_glu, w_out, ln1_g, ln1_b, w_ff1, w_ff2, ln2_g, ln2_b):
    L = x.shape[1]
    ROWS = L // GRID_W
    t_pos = jnp.arange(L)
    rows = jnp.broadcast_to(jnp.arange(ROWS)[:, None], (ROWS, GRID_W)).reshape(-1)
    cols = jnp.broadcast_to(jnp.arange(GRID_W)[None, :], (ROWS, GRID_W)).reshape(-1)
    cond_x = jax.nn.silu(c)
    cond_c = jax.nn.silu(c_ctx)
    for l in range(DEPTH):
        ctx_out = l < DEPTH - 1
        mx = jnp.split((cond_x @ w_ada[l] + b_ada[l])[:, None, :], N_ADA, axis=-1)
        mc = jnp.split(cond_c @ w_ada[l] + b_ada[l], N_ADA, axis=-1)
        hx = x * (1.0 + mx[1]) + mx[0]
        hc = ctx * (1.0 + mc[1]) + mc[0]
        px = split_projection(hx @ w_in[l])
        pc = split_projection(hc @ w_in[l])
        ax, ac = attention_mixer(px[0], px[1], px[2], pc[0], pc[1], pc[2], att_sink[l], rows, cols, ctx_out)
        rx, rc = retention_mixer(px[3], px[4], px[5], px[6], pc[3], pc[4], pc[5], pc[6],
                                 ret_decay_logit[l], t_pos, ctx_out)
        sx, sc = s5_mixer(px[7], pc[7], s5_lambda_re[l], s5_lambda_im[l], s5_b_re[l], s5_b_im[l],
                          s5_c_re[l], s5_c_im[l], s5_log_dt[l], s5_d[l], w_glu[l], b_glu[l], ctx_out)
        ox = jnp.concatenate([ax, rx, sx], axis=-1) @ w_out[l]
        x = layer_norm(DEEPNORM_ALPHA * x + mx[2] * ox, ln1_g[l], ln1_b[l])
        fx = squared_relu_mlp(x * (1.0 + mx[4]) + mx[3], w_ff1[l], w_ff2[l])
        x = layer_norm(DEEPNORM_ALPHA * x + mx[5] * fx, ln2_g[l], ln2_b[l])
        if ctx_out:
            oc = jnp.concatenate([ac, rc, sc], axis=-1) @ w_out[l]
            ctx = layer_norm(DEEPNORM_ALPHA * ctx + mc[2] * oc, ln1_g[l], ln1_b[l])
            fc = squared_relu_mlp(ctx * (1.0 + mc[4]) + mc[3], w_ff1[l], w_ff2[l])
            ctx = layer_norm(DEEPNORM_ALPHA * ctx + mc[5] * fc, ln2_g[l], ln2_b[l])
    return x
```

```python
import functools
import math

import jax
import jax.numpy as jnp
from jax import lax
from jax.experimental import pallas as pl
from jax.experimental.pallas import tpu as pltpu

F32 = jnp.float32
MXU_DTYPE = jnp.bfloat16

D_MODEL = 1024
DEPTH = 4
GRID_W = 64
CTX_LEN = 256
CTX_PAD = 512
HEAD_DIM = 64
ATT_HEADS = 8
ATT_KV_HEADS = 2
ATT_BLOCK = 128
ROPE_BASE = 10000.0
RET_HEADS = 4
RET_CHUNK = 256
S5_CH = 16
S5_GROUPS = 16
S5_STATE = 64
S5_T = 8
S5_PAIRS = S5_GROUPS // 2
S5_TILE = 64
ATT_WIDTH = ATT_HEADS * HEAD_DIM
KV_WIDTH = ATT_KV_HEADS * HEAD_DIM
RET_WIDTH = RET_HEADS * HEAD_DIM
S5_WIDTH = S5_GROUPS * S5_CH
IN_WIDTH = ATT_WIDTH + 2 * KV_WIDTH + 4 * RET_WIDTH + S5_WIDTH
D_FF = 4 * D_MODEL
FF_CHUNK = 1024
N_ADA = 6
LN_EPS = 1e-5
GN_EPS = 1e-5
DEEPNORM_ALPHA = (2 * DEPTH) ** 0.25
ROW_TILE = 512
NEG_BIG = -1e30
LANES = 128
VMEM_LIMIT = 56 * 1024 * 1024

COL_AQ, COL_AK, COL_AV = 0, ATT_WIDTH, ATT_WIDTH + KV_WIDTH
COL_RQ = ATT_WIDTH + 2 * KV_WIDTH
COL_RK, COL_RV, COL_RG = COL_RQ + RET_WIDTH, COL_RQ + 2 * RET_WIDTH, COL_RQ + 3 * RET_WIDTH
COL_S5 = COL_RQ + 4 * RET_WIDTH


def _mm(a, b):
    return jnp.dot(a, b, preferred_element_type=F32)


def _mm_nt(a, b):
    return lax.dot_general(a, b, (((1,), (1,)), ((), ())), preferred_element_type=F32)


def _mm_tn(a, b):
    return lax.dot_general(a, b, (((0,), (0,)), ((), ())), preferred_element_type=F32)


def _params(sem):
    return pltpu.CompilerParams(dimension_semantics=sem, vmem_limit_bytes=VMEM_LIMIT)


def _mod_kernel(cond_ref, w_ref, b_ref, o_ref):
    c = cond_ref[...]
    s = c * jax.nn.sigmoid(c)
    o_ref[0] = jnp.dot(s, w_ref[0], preferred_element_type=F32, precision=lax.Precision.HIGHEST) + b_ref[0]


def _modulation(cond, w_ada, b_ada):
    tn = 1536
    n = N_ADA * D_MODEL
    return pl.pallas_call(
        _mod_kernel,
        grid=(DEPTH, n // tn),
        in_specs=[
            pl.BlockSpec((8, D_MODEL), lambda l, j: (0, 0)),
            pl.BlockSpec((1, D_MODEL, tn), lambda l, j: (l, 0, j)),
            pl.BlockSpec((1, 1, tn), lambda l, j: (l, 0, j)),
        ],
        out_specs=pl.BlockSpec((1, 8, tn), lambda l, j: (l, 0, j)),
        out_shape=jax.ShapeDtypeStruct((DEPTH, 8, n), F32),
        compiler_params=_params(("arbitrary", "arbitrary")),
        name="modulation",
    )(cond, w_ada, b_ada.reshape(DEPTH, 1, n))


def _in_proj_kernel(x_ref, mod_ref, w_ref, ca_ref, sa_ref, cr_ref, sr_ref, o_ref):
    x = x_ref[...]
    h = (x * (1.0 + mod_ref[0, 1:2, :]) + mod_ref[0, 0:1, :]).astype(w_ref.dtype)
    lane = lax.broadcasted_iota(jnp.int32, (x.shape[0], LANES), 1)
    first_att = (lane & 31) < 16
    first_ret = (lane & 63) < 32

    def proj(c0, c1):
        return _mm(h, w_ref[:, c0:c1])

    def rope_store(c0, width, cos, sin, first, half):
        p = proj(c0, c0 + width)
        for b in range(width // LANES):
            blk = p[:, LANES * b:LANES * (b + 1)]
            rot = jnp.where(first, pltpu.roll(blk, LANES - half, 1), pltpu.roll(blk, half, 1))
            o_ref[:, c0 + LANES * b:c0 + LANES * (b + 1)] = (blk * cos + rot * sin).astype(o_ref.dtype)

    ca, sa, cr, sr = ca_ref[...], sa_ref[...], cr_ref[...], sr_ref[...]
    rope_store(COL_AQ, ATT_WIDTH, ca, sa, first_att, 16)
    rope_store(COL_AK, KV_WIDTH, ca, sa, first_att, 16)
    o_ref[:, COL_AV:COL_RQ] = proj(COL_AV, COL_RQ).astype(o_ref.dtype)
    rope_store(COL_RQ, RET_WIDTH, cr, sr, first_ret, 32)
    rope_store(COL_RK, RET_WIDTH, cr, sr, first_ret, 32)
    o_ref[:, COL_RV:IN_WIDTH] = proj(COL_RV, IN_WIDTH).astype(o_ref.dtype)


def _in_proj(stream, mod, w_in, tabs):
    rows = stream.shape[0]
    tm = ROW_TILE
    tab_spec = pl.BlockSpec((tm, LANES), lambda i: (i, 0))
    return pl.pallas_call(
        _in_proj_kernel,
        grid=(rows // tm,),
        in_specs=[
            pl.BlockSpec((tm, D_MODEL), lambda i: (i, 0)),
            pl.BlockSpec((1, N_ADA, D_MODEL), lambda i: (jnp.where(i == 0, 1, 0), 0, 0)),
            pl.BlockSpec((D_MODEL, IN_WIDTH), lambda i: (0, 0)),
            tab_spec, tab_spec, tab_spec, tab_spec,
        ],
        out_specs=pl.BlockSpec((tm, IN_WIDTH), lambda i: (i, 0)),
        out_shape=jax.ShapeDtypeStruct((rows, IN_WIDTH), MXU_DTYPE),
        compiler_params=_params(("arbitrary",)),
        name="in_proj",
    )(stream, mod, w_in, *tabs)


def _rope_tables(seq):
    half_a = HEAD_DIM // 4
    half_r = HEAD_DIM // 2
    t = jnp.arange(seq)
    rows_pos = (t // GRID_W).astype(F32)
    cols_pos = (t % GRID_W).astype(F32)
    inv_a = ROPE_BASE ** (-jnp.arange(half_a, dtype=F32) / half_a)
    inv_r = ROPE_BASE ** (-jnp.arange(half_r, dtype=F32) / half_r)
    ang_r = rows_pos[:, None] * inv_a[None, :]
    ang_c = cols_pos[:, None] * inv_a[None, :]
    ang_t = t.astype(F32)[:, None] * inv_r[None, :]
    cos_a = jnp.concatenate([jnp.cos(ang_r)] * 2 + [jnp.cos(ang_c)] * 2, axis=-1)
    sin_a = jnp.concatenate([-jnp.sin(ang_r), jnp.sin(ang_r), -jnp.sin(ang_c), jnp.sin(ang_c)], axis=-1)
    cos_r = jnp.concatenate([jnp.cos(ang_t)] * 2, axis=-1)
    sin_r = jnp.concatenate([-jnp.sin(ang_t), jnp.sin(ang_t)], axis=-1)

    def full(tab, ident):
        tab = jnp.concatenate([tab, tab], axis=-1)
        head = jnp.full((CTX_PAD, LANES), ident, F32)
        return jnp.concatenate([head, tab], axis=0)

    return full(cos_a, 1.0), full(sin_a, 0.0), full(cos_r, 1.0), full(sin_r, 0.0)


def _swap_halves(x):
    if x.dtype.itemsize == 4:
        return pltpu.roll(x, 64, 1)
    packed = pltpu.bitcast(x, jnp.uint32)
    return pltpu.bitcast(pltpu.roll(packed, 64, 1), x.dtype)


def _dup_heads(x):
    sw = _swap_halves(x)
    lo = lax.broadcasted_iota(jnp.int32, x.shape, 1) < HEAD_DIM
    return jnp.where(lo, x, sw), jnp.where(lo, sw, x)


def _attn_kernel(sink_ref, q_ref, km_ref, kp_ref, kn_ref, vm_ref, vp_ref, vn_ref, kc_ref, vc_ref, mask_ref,
                 o_ref, k2_ref, v2_ref, kc2_ref, vc2_ref):
    i = pl.program_id(0)
    last_blk = pl.num_programs(0) * (ROW_TILE // ATT_BLOCK) - 1
    blk = ATT_BLOCK
    for dst, parts in ((k2_ref, (kp_ref, km_ref, kn_ref)), (v2_ref, (vp_ref, vm_ref, vn_ref))):
        row = 0
        for part in parts:
            a, b = _dup_heads(part[...])
            n = part.shape[0]
            dst[0, row:row + n, :] = a
            dst[1, row:row + n, :] = b
            row += n
    for dst, src in ((kc2_ref, kc_ref), (vc2_ref, vc_ref)):
        a, b = _dup_heads(src[...])
        dst[0] = a
        dst[1] = b

    lo = lax.broadcasted_iota(jnp.int32, (blk, LANES), 1) < HEAD_DIM
    group = ATT_HEADS // ATT_KV_HEADS

    def body(j, carry):
        r0 = pl.multiple_of(j * blk, blk)
        gblk = i * (ROW_TILE // blk) + j
        sel = jnp.where(i == 0, 3, jnp.where(gblk == CTX_PAD // blk, 0, jnp.where(gblk == last_blk, 2, 1)))
        bias = mask_ref[sel]
        for kv in range(ATT_KV_HEADS):
            qt = q_ref[pl.ds(r0, blk), group * HEAD_DIM * kv:group * HEAD_DIM * (kv + 1)]
            parts = []
            for g in range(group):
                qc = qt[:, LANES * (g // 2):LANES * (g // 2 + 1)]
                keep = lo if g % 2 == 0 else jnp.logical_not(lo)
                parts.append(jnp.where(keep, qc, jnp.zeros_like(qc)))
            qs = jnp.concatenate(parts, axis=0)
            kl = k2_ref[kv, pl.ds(r0, 3 * blk), :]
            vl = v2_ref[kv, pl.ds(r0, 3 * blk), :]
            s_loc = _mm_nt(qs, kl)
            s_ctx = _mm_nt(qs, kc2_ref[kv])
            p_loc, p_ctx, inv = [], [], []
            for g in range(group):
                sl = s_loc[blk * g:blk * (g + 1)] + bias
                sc = s_ctx[blk * g:blk * (g + 1)]
                sk = sink_ref[group * kv + g]
                m = jnp.maximum(jnp.maximum(jnp.max(sl, axis=-1, keepdims=True),
                                            jnp.max(sc, axis=-1, keepdims=True)), sk)
                el = jnp.exp(sl - m)
                ec = jnp.exp(sc - m)
                den = jnp.sum(el, axis=-1, keepdims=True) + jnp.sum(ec, axis=-1, keepdims=True) + jnp.exp(sk - m)
                p_loc.append(el.astype(vl.dtype))
                p_ctx.append(ec.astype(vl.dtype))
                inv.append(1.0 / den)
            o = _mm(jnp.concatenate(p_loc, axis=0), vl) + _mm(jnp.concatenate(p_ctx, axis=0), vc2_ref[kv])
            outs = [o[blk * g:blk * (g + 1)] * inv[g] for g in range(group)]
            for half in range(group // 2):
                y = jnp.where(lo, outs[2 * half], outs[2 * half + 1])
                c0 = group * HEAD_DIM * kv + LANES * half
                o_ref[pl.ds(r0, blk), c0:c0 + LANES] = y.astype(o_ref.dtype)
        return carry

    lax.fori_loop(0, ROW_TILE // blk, body, 0)


def _attention_masks():
    qi = jnp.arange(ATT_BLOCK)[:, None]
    kj = jnp.arange(3 * ATT_BLOCK)[None, :]
    band = jnp.abs(kj - ATT_BLOCK - qi) <= ATT_BLOCK
    first = band & (kj >= ATT_BLOCK)
    last = band & (kj < 2 * ATT_BLOCK)
    none = jnp.zeros_like(band)
    masks = jnp.stack([first, band, last, none])
    return jnp.where(masks, 0.0, NEG_BIG).astype(F32)


def _attention(proj, sink, masks):
    rows = proj.shape[0]
    tm, blk = ROW_TILE, ATT_BLOCK
    per = tm // blk
    nblk = rows // blk
    ck, cv = COL_AK // KV_WIDTH, COL_AV // KV_WIDTH
    dt = proj.dtype
    return pl.pallas_call(
        _attn_kernel,
        grid=(rows // tm,),
        in_specs=[
            pl.BlockSpec(memory_space=pltpu.SMEM),
            pl.BlockSpec((tm, ATT_WIDTH), lambda i: (i, 0)),
            pl.BlockSpec((tm, KV_WIDTH), lambda i: (i, ck)),
            pl.BlockSpec((blk, KV_WIDTH), lambda i: (jnp.maximum(i * per - 1, 0), ck)),
            pl.BlockSpec((blk, KV_WIDTH), lambda i: (jnp.minimum((i + 1) * per, nblk - 1), ck)),
            pl.BlockSpec((tm, KV_WIDTH), lambda i: (i, cv)),
            pl.BlockSpec((blk, KV_WIDTH), lambda i: (jnp.maximum(i * per - 1, 0), cv)),
            pl.BlockSpec((blk, KV_WIDTH), lambda i: (jnp.minimum((i + 1) * per, nblk - 1), cv)),
            pl.BlockSpec((CTX_LEN, KV_WIDTH), lambda i: (0, ck)),
            pl.BlockSpec((CTX_LEN, KV_WIDTH), lambda i: (0, cv)),
            pl.BlockSpec((4, blk, 3 * blk), lambda i: (0, 0, 0)),
        ],
        out_specs=pl.BlockSpec((tm, ATT_WIDTH), lambda i: (i, 0)),
        out_shape=jax.ShapeDtypeStruct((rows, ATT_WIDTH), dt),
        scratch_shapes=[
            pltpu.VMEM((2, tm + 2 * blk, KV_WIDTH), dt),
            pltpu.VMEM((2, tm + 2 * blk, KV_WIDTH), dt),
            pltpu.VMEM((2, CTX_LEN, KV_WIDTH), dt),
            pltpu.VMEM((2, CTX_LEN, KV_WIDTH), dt),
        ],
        compiler_params=_params(("arbitrary",)),
        name="attention",
    )(sink, proj, proj, proj, proj, proj, proj, proj, proj, proj, masks)


def _ret_kernel(lg_ref, q_ref, k_ref, v_ref, g_ref, o_ref, sb_ref, s_ref, dm_ref, tab_ref, gbd_ref):
    ph = pl.program_id(0)
    t = pl.program_id(1)
    nlat = pl.num_programs(1) - 2
    c = RET_CHUNK
    w = RET_WIDTH
    mdt = sb_ref.dtype

    def lane_vec(direction, shape, axis):
        head = lax.broadcasted_iota(jnp.int32, shape, axis) // HEAD_DIM
        out = jnp.full(shape, lg_ref[direction, RET_HEADS - 1], F32)
        for h in range(RET_HEADS - 2, -1, -1):
            out = jnp.where(head == h, lg_ref[direction, h], out)
        return out

    @pl.when(jnp.logical_and(ph == 0, t == 0))
    def _init_tables():
        diff = (lax.broadcasted_iota(jnp.int32, (c, c), 0) - lax.broadcasted_iota(jnp.int32, (c, c), 1)).astype(F32)
        for h in range(RET_HEADS):
            dm_ref[h] = jnp.exp(jnp.where(diff >= 0, diff * lg_ref[0, h], -diff * lg_ref[1, h]))
        pos = lax.broadcasted_iota(jnp.int32, (c, w), 0).astype(F32)
        lgf = lane_vec(0, (c, w), 1)
        lgb = lane_vec(1, (c, w), 1)
        tab_ref[0] = jnp.exp((c - 1.0 - pos) * lgf)
        tab_ref[1] = jnp.exp((pos + 1.0) * lgf)
        tab_ref[2] = jnp.exp(pos * lgb)
        tab_ref[3] = jnp.exp((c - pos) * lgb)
        same = (lax.broadcasted_iota(jnp.int32, (w, w), 0) // HEAD_DIM
                == lax.broadcasted_iota(jnp.int32, (w, w), 1) // HEAD_DIM)
        bd = jnp.where(same, 1.0, 0.0)
        gbd_ref[0] = bd * jnp.exp(c * lane_vec(0, (w, w), 0))
        gbd_ref[1] = bd * jnp.exp(c * lane_vec(1, (w, w), 0))
        gbd_ref[2] = bd

    @pl.when(t == 0)
    def _reset_state():
        s_ref[...] = jnp.zeros_like(s_ref)

    def state_update(direction, key_tab):
        kw = (k_ref[...].astype(F32) * tab_ref[key_tab]).astype(mdt)
        u = _mm_tn(kw, v_ref[...])
        s_ref[...] = gbd_ref[direction] * s_ref[...] + gbd_ref[2] * u

    @pl.when(jnp.logical_and(ph == 0, t <= nlat))
    def _backward_states():
        idx = jnp.where(t == 0, 0, nlat + 1 - t)
        sb_ref[idx] = s_ref[...].astype(mdt)
        state_update(1, 2)

    @pl.when(jnp.logical_and(ph == 1, t == 1))
    def _pad_rows():
        o_ref[...] = jnp.zeros_like(o_ref)

    @pl.when(jnp.logical_and(ph == 1, t != 1))
    def _outputs():
        idx = jnp.where(t == 0, 0, t - 1)
        q = q_ref[...]
        k = k_ref[...]
        v = v_ref[...]
        head = lax.broadcasted_iota(jnp.int32, (c, w), 1) // HEAD_DIM
        qs = jnp.concatenate([jnp.where(head == h, q, jnp.zeros_like(q)) for h in range(RET_HEADS)], axis=0)
        sc = _mm_nt(qs, k)
        scd = jnp.concatenate([sc[c * h:c * (h + 1)] * dm_ref[h] for h in range(RET_HEADS)], axis=0).astype(mdt)
        oi = _mm(scd, v)
        o = jnp.zeros((c, w), F32)
        for h in range(RET_HEADS):
            o = o + jnp.where(head == h, oi[c * h:c * (h + 1)], 0.0)
        qf = q.astype(F32)
        o = o + _mm((qf * tab_ref[1]).astype(mdt), s_ref[...].astype(mdt))
        o = o + _mm((qf * tab_ref[3]).astype(mdt), sb_ref[idx])
        state_update(0, 0)
        avg = (gbd_ref[2] * (1.0 / HEAD_DIM)).astype(mdt)

        def head_mean(val):
            hi = val.astype(mdt)
            lo = (val - hi.astype(F32)).astype(mdt)
            return _mm(hi, avg) + _mm(lo, avg)

        d = o - head_mean(o)
        var = head_mean(d * d)
        gate = g_ref[...].astype(F32)
        y = d * lax.rsqrt(var + GN_EPS) * (gate * jax.nn.sigmoid(gate))
        o_ref[...] = y.astype(o_ref.dtype)


def _retention(proj, log_gamma):
    rows = proj.shape[0]
    c = RET_CHUNK
    nlat = (rows - CTX_PAD) // c
    dt = proj.dtype
    cq, ckk, cvv, cg = (COL_RQ // RET_WIDTH, COL_RK // RET_WIDTH, COL_RV // RET_WIDTH, COL_RG // RET_WIDTH)

    def kv_blk(ph, t):
        back = jnp.where(t == 0, 0, jnp.maximum(nlat + 2 - t, 2))
        return jnp.where(ph == 0, back, t)

    def fw_blk(ph, t):
        return jnp.where(ph == 0, 0, t)

    return pl.pallas_call(
        _ret_kernel,
        grid=(2, nlat + 2),
        in_specs=[
            pl.BlockSpec(memory_space=pltpu.SMEM),
            pl.BlockSpec((c, RET_WIDTH), lambda ph, t: (fw_blk(ph, t), cq)),
            pl.BlockSpec((c, RET_WIDTH), lambda ph, t: (kv_blk(ph, t), ckk)),
            pl.BlockSpec((c, RET_WIDTH), lambda ph, t: (kv_blk(ph, t), cvv)),
            pl.BlockSpec((c, RET_WIDTH), lambda ph, t: (fw_blk(ph, t), cg)),
        ],
        out_specs=pl.BlockSpec((c, RET_WIDTH), lambda ph, t: (fw_blk(ph, t), 0)),
        out_shape=jax.ShapeDtypeStruct((rows, RET_WIDTH), dt),
        scratch_shapes=[
            pltpu.VMEM((nlat + 1, RET_WIDTH, RET_WIDTH), dt),
            pltpu.VMEM((RET_WIDTH, RET_WIDTH), F32),
            pltpu.VMEM((RET_HEADS, c, c), F32),
            pltpu.VMEM((4, c, RET_WIDTH), F32),
            pltpu.VMEM((3, RET_WIDTH, RET_WIDTH), F32),
        ],
        compiler_params=_params(("arbitrary", "arbitrary")),
        name="retention",
    )(log_gamma, proj, proj, proj, proj)


def _s5_weights(lam_re, lam_im, b_re, b_im, c_re, c_im, log_dt, d_skip):
    hp = lax.Precision.HIGHEST
    tt, g, n, p, a = S5_T, S5_GROUPS, S5_STATE, S5_CH, S5_PAIRS
    lam = lax.complex(lam_re.astype(F32), lam_im.astype(F32))
    dtv = jnp.exp(log_dt.astype(F32))[..., None]
    lam_bar = jnp.exp(lam * dtv)
    bbar = ((lam_bar - 1.0) / lam)[..., None] * lax.complex(b_re.astype(F32), b_im.astype(F32))
    cmat = lax.complex(c_re.astype(F32), c_im.astype(F32))
    pw = [jnp.ones_like(lam_bar)]
    for _ in range(tt):
        pw.append(pw[-1] * lam_bar)
    pw = jnp.stack(pw, axis=1)
    eye2 = jnp.eye(2, dtype=F32)
    j = jnp.arange(tt)

    def drive(direction, exps):
        m = pw[direction][exps][..., None] * bbar[direction][None]
        m = jnp.stack([jnp.real(m), jnp.imag(m)], axis=0)
        m = m.reshape(2, tt, a, 2, n, p)
        full = jnp.einsum('rjagnp,gh->ajgprhn', m, eye2)
        return full.reshape(a, tt * 2 * p, 2 * 2 * n)

    w_drive = jnp.concatenate([drive(0, tt - 1 - j), drive(1, j)], axis=-1)

    def read(direction, exps):
        y = cmat[direction][None] * pw[direction][exps][:, :, None, :]
        y = jnp.stack([jnp.real(y), -jnp.imag(y)], axis=0)
        y = y.reshape(2, tt, a, 2, p, n)
        full = jnp.einsum('riagpn,gh->arhnigp', y, eye2)
        return full.reshape(a, 2 * 2 * n, tt * 2 * p)

    w_read = jnp.concatenate([read(0, j + 1), read(1, tt - j)], axis=1)

    def lag_kernels(direction):
        return jnp.real(jnp.einsum('gqn,dgn,gnp->dgpq', cmat[direction], pw[direction][:tt], bbar[direction],
                                   precision=hp))

    kf, kb = lag_kernels(0), lag_kernels(1)
    lag = j[None, :] - j[:, None]
    toep = (jnp.where((lag >= 0)[:, :, None, None, None], kf[jnp.clip(lag, 0, tt - 1)], 0.0)
            + jnp.where((lag <= 0)[:, :, None, None, None], kb[jnp.clip(-lag, 0, tt - 1)], 0.0))
    toep = toep.reshape(tt, tt, a, 2, p, p)
    w_intra = jnp.einsum('jiagpq,gh->ajgpihq', toep, eye2).reshape(a, tt * 2 * p, tt * 2 * p)

    lam_t = pw[:, tt].reshape(2, a, 2 * n)
    lam_t = jnp.concatenate([jnp.real(lam_t[0]), jnp.imag(lam_t[0]), jnp.real(lam_t[1]), jnp.imag(lam_t[1])],
                            axis=-1).reshape(a, 1, 4 * 2 * n)
    skip = jnp.tile(d_skip.astype(F32).reshape(a, 1, 2 * p), (1, tt, 1)).reshape(a, 1, tt * 2 * p)
    return w_drive.astype(MXU_DTYPE), w_read.astype(MXU_DTYPE), w_intra.astype(MXU_DTYPE), lam_t, skip


def _s5_drive_kernel(u_ref, w_ref, o_ref):
    o_ref[0] = _mm(u_ref[0], w_ref[0])


def _s5_drive(u_pairs, w_drive):
    a, nch, wd = u_pairs.shape
    return pl.pallas_call(
        _s5_drive_kernel,
        grid=(a,),
        in_specs=[pl.BlockSpec((1, nch, wd), lambda i: (i, 0, 0)),
                  pl.BlockSpec((1, wd, 2 * wd), lambda i: (i, 0, 0))],
        out_specs=pl.BlockSpec((1, nch, 2 * wd), lambda i: (i, 0, 0)),
        out_shape=jax.ShapeDtypeStruct((a, nch, 2 * wd), F32),
        compiler_params=_params(("arbitrary",)),
        name="s5_drive",
    )(u_pairs, w_drive)


def _s5_scan_kernel(ef_ref, eb_ref, lam_ref, sf_ref, sb_ref, stf_ref, stb_ref):
    t = pl.program_id(0)
    hw = LANES
    sub = 8
    npair = lam_ref.shape[0]

    @pl.when(t == 0)
    def _reset():
        stf_ref[...] = jnp.zeros_like(stf_ref)
        stb_ref[...] = jnp.zeros_like(stb_ref)

    def step(lr, li, sr, si, e):
        return lr * sr - li * si + e[:, 0:hw], lr * si + li * sr + e[:, hw:2 * hw]

    def run(nrows):
        ngroups = nrows // sub

        def body(gi, carry):
            r0 = pl.multiple_of(gi * sub, sub)
            rb0 = pl.multiple_of((ngroups - 1 - gi) * sub, sub)
            new = []
            for a in range(npair):
                lam = lam_ref[a]
                lfr, lfi, lbr, lbi = (lam[:, k * hw:(k + 1) * hw] for k in range(4))
                fr, fi, br, bi = carry[a]
                ef = ef_ref[a, pl.ds(r0, sub), :]
                eb = eb_ref[a, pl.ds(rb0, sub), :]
                before_f = []
                for k in range(sub):
                    before_f.append(jnp.concatenate([fr, fi], axis=1))
                    fr, fi = step(lfr, lfi, fr, fi, ef[k:k + 1])
                sf_ref[a, pl.ds(r0, sub), :] = jnp.concatenate(before_f, axis=0)
                before_b = [None] * sub
                for k in range(sub - 1, -1, -1):
                    before_b[k] = jnp.concatenate([br, bi], axis=1)
                    br, bi = step(lbr, lbi, br, bi, eb[k:k + 1])
                sb_ref[a, pl.ds(rb0, sub), :] = jnp.concatenate(before_b, axis=0)
                new.append((fr, fi, br, bi))
            return tuple(new)

        init = tuple((stf_ref[a, :, 0:hw], stf_ref[a, :, hw:2 * hw], stb_ref[a, :, 0:hw], stb_ref[a, :, hw:2 * hw])
                     for a in range(npair))
        final = lax.fori_loop(0, ngroups, body, init)
        for a in range(npair):
            fr, fi, br, bi = final[a]
            stf_ref[a, :, 0:hw] = fr
            stf_ref[a, :, hw:2 * hw] = fi
            stb_ref[a, :, 0:hw] = br
            stb_ref[a, :, hw:2 * hw] = bi

    @pl.when(t == 0)
    def _context():
        sf_ref[...] = jnp.zeros_like(sf_ref)
        sb_ref[...] = jnp.zeros_like(sb_ref)
        run(CTX_LEN // S5_T)

    @pl.when(t > 0)
    def _latent():
        run(S5_TILE)


def _s5_scan(drive, lam_t):
    a, nch, wd2 = drive.shape
    wd = wd2 // 2
    nt = nch // S5_TILE

    def bwd(t):
        return jnp.where(t == 0, 0, nt - t)

    return pl.pallas_call(
        _s5_scan_kernel,
        grid=(nt,),
        in_specs=[pl.BlockSpec((a, S5_TILE, wd), lambda t: (0, t, 0)),
                  pl.BlockSpec((a, S5_TILE, wd), lambda t: (0, bwd(t), 1)),
                  pl.BlockSpec((a, 1, wd2), lambda t: (0, 0, 0))],
        out_specs=[pl.BlockSpec((a, S5_TILE, wd), lambda t: (0, t, 0)),
                   pl.BlockSpec((a, S5_TILE, wd), lambda t: (0, bwd(t), 0))],
        out_shape=[jax.ShapeDtypeStruct((a, nch, wd), F32), jax.ShapeDtypeStruct((a, nch, wd), F32)],
        scratch_shapes=[pltpu.VMEM((a, 1, wd), F32), pltpu.VMEM((a, 1, wd), F32)],
        compiler_params=_params(("arbitrary",)),
        name="s5_scan",
    )(drive, drive, lam_t)


def _s5_read_kernel(u_ref, sf_ref, sb_ref, wi_ref, wr_ref, skip_ref, o_ref):
    u = u_ref[0]
    wd = u.shape[-1]
    mdt = u.dtype
    y = _mm(u, wi_ref[0])
    y = y + _mm(sf_ref[0].astype(mdt), wr_ref[0, 0:wd, :])
    y = y + _mm(sb_ref[0].astype(mdt), wr_ref[0, wd:2 * wd, :])
    o_ref[0] = y + u.astype(F32) * skip_ref[0]


def _s5_read(u_pairs, sf, sb, w_intra, w_read, skip):
    a, nch, wd = u_pairs.shape
    blk = lambda *shape: pl.BlockSpec((1,) + shape, lambda i: (i, 0, 0))
    return pl.pallas_call(
        _s5_read_kernel,
        grid=(a,),
        in_specs=[blk(nch, wd), blk(nch, wd), blk(nch, wd), blk(wd, wd), blk(2 * wd, wd), blk(1, wd)],
        out_specs=blk(nch, wd),
        out_shape=jax.ShapeDtypeStruct((a, nch, wd), F32),
        compiler_params=_params(("arbitrary",)),
        name="s5_read",
    )(u_pairs, sf, sb, w_intra, w_read, skip)


def _s5_mixer(proj, weights):
    rows = proj.shape[0]
    nch = rows // S5_T
    pw = 2 * S5_CH
    w_drive, w_read, w_intra, lam_t, skip = weights
    u = proj[:, COL_S5:COL_S5 + S5_WIDTH]
    u_pairs = u.reshape(nch, S5_T, S5_PAIRS, pw).transpose(2, 0, 1, 3).reshape(S5_PAIRS, nch, S5_T * pw)
    drive = _s5_drive(u_pairs, w_drive)
    sf, sb = _s5_scan(drive, lam_t)
    y = _s5_read(u_pairs, sf, sb, w_intra, w_read, skip)
    return y.reshape(S5_PAIRS, nch, S5_T, pw).transpose(1, 2, 0, 3).reshape(rows, S5_WIDTH)


def _layer_norm(x, g, b):
    mu = jnp.mean(x, axis=-1, keepdims=True)
    d = x - mu
    var = jnp.mean(d * d, axis=-1, keepdims=True)
    return d * lax.rsqrt(var + LN_EPS) * g + b


def _post_kernel(x_ref, att_ref, ret_ref, s5_ref, mod_ref, wglu_ref, bglu_ref, woa_ref, wor_ref, wos_ref,
                 g1_ref, b1_ref, w1_ref, w2_ref, g2_ref, b2_ref, o_ref):
    mdt = w1_ref.dtype
    x = x_ref[...]
    hs = jax.nn.gelu(s5_ref[...])
    gate = jax.nn.sigmoid(_mm(hs.astype(mdt), wglu_ref[...]) + bglu_ref[...])
    s5 = (hs * gate).astype(mdt)
    ox = _mm(att_ref[...], woa_ref[...]) + _mm(ret_ref[...], wor_ref[...]) + _mm(s5, wos_ref[...])
    x1 = _layer_norm(DEEPNORM_ALPHA * x + mod_ref[0, 2:3, :] * ox, g1_ref[...], b1_ref[...])
    h = (x1 * (1.0 + mod_ref[0, 4:5, :]) + mod_ref[0, 3:4, :]).astype(mdt)
    acc = jnp.zeros(x.shape, F32)
    for c in range(D_FF // FF_CHUNK):
        a = _mm(h, w1_ref[:, FF_CHUNK * c:FF_CHUNK * (c + 1)])
        a = jnp.square(jnp.maximum(a, 0.0)).astype(mdt)
        acc = acc + _mm(a, w2_ref[FF_CHUNK * c:FF_CHUNK * (c + 1), :])
    o_ref[...] = _layer_norm(DEEPNORM_ALPHA * x1 + mod_ref[0, 5:6, :] * acc, g2_ref[...], b2_ref[...])


def _post(stream, att, ret, s5, mod, wglu, bglu, wo, g1, b1, w1, w2, g2, b2, skip_context):
    rows = stream.shape[0]
    tm = ROW_TILE
    off = CTX_PAD // tm if skip_context else 0
    row_blk = lambda width: pl.BlockSpec((tm, width), lambda i: (i + off, 0))
    full = lambda arr: pl.BlockSpec(arr.shape, lambda i: (0,) * arr.ndim)
    woa, wor, wos = wo[:ATT_WIDTH], wo[ATT_WIDTH:ATT_WIDTH + RET_WIDTH], wo[ATT_WIDTH + RET_WIDTH:]
    vec = lambda v: v.reshape(1, -1).astype(F32)
    small = [wglu, vec(bglu), woa, wor, wos, vec(g1), vec(b1), w1, w2, vec(g2), vec(b2)]
    return pl.pallas_call(
        _post_kernel,
        grid=(rows // tm - off,),
        in_specs=[row_blk(D_MODEL), row_blk(ATT_WIDTH), row_blk(RET_WIDTH), row_blk(S5_WIDTH),
                  pl.BlockSpec((1, N_ADA, D_MODEL), lambda i: (jnp.where(i + off == 0, 1, 0), 0, 0))]
                 + [full(arr) for arr in small],
        out_specs=pl.BlockSpec((tm, D_MODEL), lambda i: (i, 0)),
        out_shape=jax.ShapeDtypeStruct((rows - off * tm, D_MODEL), F32),
        compiler_params=_params(("arbitrary",)),
        name="post",
    )(stream, att, ret, s5, mod, *small)


def kernel(x, c, ctx, c_ctx, w_ada, b_ada, w_in, att_sink, ret_decay_logit, s5_lambda_re, s5_lambda_im, s5_b_re,
           s5_b_im, s5_c_re, s5_c_im, s5_log_dt, s5_d, w_glu, b_glu, w_out, ln1_g, ln1_b, w_ff1, w_ff2, ln2_g,
           ln2_b):
    assert x.shape[0] == 1 and x.shape[2] == D_MODEL and ctx.shape[1] == CTX_LEN
    seq = x.shape[1]
    assert seq % ROW_TILE == 0
    stream = jnp.concatenate([ctx[0], jnp.zeros((CTX_PAD - CTX_LEN, D_MODEL), F32), x[0]], axis=0)
    cond = jnp.zeros((8, D_MODEL), F32).at[0].set(c[0]).at[1].set(c_ctx)
    mods = _modulation(cond, w_ada, b_ada)
    tabs = _rope_tables(seq)
    masks = _attention_masks()
    col_scale = jnp.ones((IN_WIDTH,), F32).at[COL_AQ:COL_AK].set(HEAD_DIM ** -0.5)
    col_scale = col_scale.at[COL_RQ:COL_RK].set(HEAD_DIM ** -0.5)
    for l in range(DEPTH):
        mod = mods[l, :2].reshape(2, N_ADA, D_MODEL)
        proj = _in_proj(stream, mod, (w_in[l] * col_scale).astype(MXU_DTYPE), tabs)
        att = _attention(proj, att_sink[l].astype(F32), masks)
        ret = _retention(proj, jax.nn.log_sigmoid(ret_decay_logit[l].astype(F32)))
        s5w = _s5_weights(s5_lambda_re[l], s5_lambda_im[l], s5_b_re[l], s5_b_im[l], s5_c_re[l], s5_c_im[l],
                          s5_log_dt[l], s5_d[l])
        s5 = _s5_mixer(proj, s5w)
        stream = _post(stream, att, ret, s5, mod, w_glu[l].astype(MXU_DTYPE), b_glu[l],
                       w_out[l].astype(MXU_DTYPE), ln1_g[l], ln1_b[l], w_ff1[l].astype(MXU_DTYPE),
                       w_ff2[l].astype(MXU_DTYPE), ln2_g[l], ln2_b[l], skip_context=(l == DEPTH - 1))
    return stream[None]
```

```python
import functools
import math

import jax
import jax.numpy as jnp
from jax import lax
from jax.experimental import pallas as pl
from jax.experimental.pallas import tpu as pltpu

F32 = jnp.float32
MXU_DTYPE = jnp.bfloat16

D_MODEL = 1024
DEPTH = 4
GRID_W = 64
CTX_LEN = 256
CTX_PAD = 512
HEAD_DIM = 64
ATT_HEADS = 8
ATT_KV_HEADS = 2
ATT_BLOCK = 128
ROPE_BASE = 10000.0
RET_HEADS = 4
RET_CHUNK = 256
S5_CH = 16
S5_GROUPS = 16
S5_STATE = 64
S5_T = 8
S5_PAIRS = S5_GROUPS // 2
S5_PAIR_W = 2 * S5_CH
S5_TILE = 64
ATT_WIDTH = ATT_HEADS * HEAD_DIM
KV_WIDTH = ATT_KV_HEADS * HEAD_DIM
RET_WIDTH = RET_HEADS * HEAD_DIM
S5_WIDTH = S5_GROUPS * S5_CH
IN_WIDTH = ATT_WIDTH + 2 * KV_WIDTH + 4 * RET_WIDTH + S5_WIDTH
D_FF = 4 * D_MODEL
FF_CHUNK = 1024
N_ADA = 6
LN_EPS = 1e-5
GN_EPS = 1e-5
DEEPNORM_ALPHA = (2 * DEPTH) ** 0.25
ROW_TILE = 512
NEG_BIG = -1e30
LANES = 128
VMEM_LIMIT = 56 * 1024 * 1024

COL_AQ, COL_AK, COL_AV = 0, ATT_WIDTH, ATT_WIDTH + KV_WIDTH
COL_RQ = ATT_WIDTH + 2 * KV_WIDTH
COL_RK, COL_RV, COL_RG = COL_RQ + RET_WIDTH, COL_RQ + 2 * RET_WIDTH, COL_RQ + 3 * RET_WIDTH
COL_S5 = COL_RQ + 4 * RET_WIDTH


def _mm(a, b):
    return jnp.dot(a, b, preferred_element_type=F32)


def _mm_nt(a, b):
    return lax.dot_general(a, b, (((1,), (1,)), ((), ())), preferred_element_type=F32)


def _mm_tn(a, b):
    return lax.dot_general(a, b, (((0,), (0,)), ((), ())), preferred_element_type=F32)


def _params(sem):
    return pltpu.CompilerParams(dimension_semantics=sem, vmem_limit_bytes=VMEM_LIMIT)


def _mod_kernel(cond_ref, w_ref, b_ref, o_ref):
    c = cond_ref[...]
    s = c * jax.nn.sigmoid(c)
    o_ref[0] = jnp.dot(s, w_ref[0], preferred_element_type=F32, precision=lax.Precision.HIGHEST) + b_ref[0]


def _modulation(cond, w_ada, b_ada):
    tn = 1536
    n = N_ADA * D_MODEL
    return pl.pallas_call(
        _mod_kernel,
        grid=(DEPTH, n // tn),
        in_specs=[
            pl.BlockSpec((8, D_MODEL), lambda l, j: (0, 0)),
            pl.BlockSpec((1, D_MODEL, tn), lambda l, j: (l, 0, j)),
            pl.BlockSpec((1, 1, tn), lambda l, j: (l, 0, j)),
        ],
        out_specs=pl.BlockSpec((1, 8, tn), lambda l, j: (l, 0, j)),
        out_shape=jax.ShapeDtypeStruct((DEPTH, 8, n), F32),
        compiler_params=_params(("arbitrary", "arbitrary")),
        name="modulation",
    )(cond, w_ada, b_ada.reshape(DEPTH, 1, n))


def _lane_block_shuffle(src_rows, src_lane_blk, out_vreg):
    acc = None
    for q in range(LANES // S5_PAIR_W):
        piece = src_rows(out_vreg * (LANES // S5_PAIR_W) + q)
        shift = (S5_PAIR_W * (q - src_lane_blk)) % LANES
        if shift:
            piece = pltpu.roll(piece, shift, 1)
        if acc is None:
            acc = piece
        else:
            lane_blk = lax.broadcasted_iota(jnp.int32, piece.shape, 1) // S5_PAIR_W
            acc = jnp.where(lane_blk == q, piece, acc)
    return acc


def _in_proj_kernel(x_ref, mod_ref, w_ref, ca_ref, sa_ref, cr_ref, sr_ref, perm_ref, o_ref, u_ref):
    x = x_ref[...]
    h = (x * (1.0 + mod_ref[1:2, :]) + mod_ref[0:1, :]).astype(w_ref.dtype)
    lane = lax.broadcasted_iota(jnp.int32, (x.shape[0], LANES), 1)
    first_att = (lane & 31) < 16
    first_ret = (lane & 63) < 32

    def proj(c0, c1):
        return _mm(h, w_ref[:, c0:c1])

    def rope_store(c0, width, cos, sin, first, half):
        p = proj(c0, c0 + width)
        for b in range(width // LANES):
            blk = p[:, LANES * b:LANES * (b + 1)]
            rot = jnp.where(first, pltpu.roll(blk, LANES - half, 1), pltpu.roll(blk, half, 1))
            o_ref[:, c0 + LANES * b:c0 + LANES * (b + 1)] = (blk * cos + rot * sin).astype(o_ref.dtype)

    ca, sa, cr, sr = ca_ref[...], sa_ref[...], cr_ref[...], sr_ref[...]
    rope_store(COL_AQ, ATT_WIDTH, ca, sa, first_att, 16)
    rope_store(COL_AK, KV_WIDTH, ca, sa, first_att, 16)
    o_ref[:, COL_AV:COL_RQ] = proj(COL_AV, COL_RQ).astype(o_ref.dtype)
    rope_store(COL_RQ, RET_WIDTH, cr, sr, first_ret, 32)
    rope_store(COL_RK, RET_WIDTH, cr, sr, first_ret, 32)
    o_ref[:, COL_RV:COL_S5] = proj(COL_RV, COL_S5).astype(o_ref.dtype)
    u = proj(COL_S5, IN_WIDTH).astype(w_ref.dtype)
    g = _mm(perm_ref[...], u)
    nchunk = x.shape[0] // S5_T
    for a in range(S5_PAIRS):
        vreg_col, lane_blk = divmod(a * S5_PAIR_W, LANES)
        lane_blk //= S5_PAIR_W
        for v in range(S5_T * S5_PAIR_W // LANES):
            slab = _lane_block_shuffle(
                lambda j: g[nchunk * j:nchunk * (j + 1), vreg_col * LANES:(vreg_col + 1) * LANES], lane_blk, v)
            u_ref[a, :, LANES * v:LANES * (v + 1)] = slab.astype(u_ref.dtype)


def _chunk_perm(tile_rows, dtype):
    nchunk = tile_rows // S5_T
    r = jnp.arange(tile_rows)
    src = S5_T * (r % nchunk) + r // nchunk
    return (src[:, None] == jnp.arange(tile_rows)[None, :]).astype(dtype)


def _in_proj(stream, mods, layer, w_in, tabs, perm):
    rows = stream.shape[0]
    tm = ROW_TILE
    tab_spec = pl.BlockSpec((tm, LANES), lambda i: (i, 0))
    nch = rows // S5_T
    return pl.pallas_call(
        _in_proj_kernel,
        grid=(rows // tm,),
        in_specs=[
            pl.BlockSpec((tm, D_MODEL), lambda i: (i, 0)),
            pl.BlockSpec((None, None, N_ADA, D_MODEL), lambda i: (layer, jnp.where(i == 0, 1, 0), 0, 0)),
            pl.BlockSpec((D_MODEL, IN_WIDTH), lambda i: (0, 0)),
            tab_spec, tab_spec, tab_spec, tab_spec,
            pl.BlockSpec((tm, tm), lambda i: (0, 0)),
        ],
        out_specs=[pl.BlockSpec((tm, COL_S5), lambda i: (i, 0)),
                   pl.BlockSpec((S5_PAIRS, tm // S5_T, S5_T * S5_PAIR_W), lambda i: (0, i, 0))],
        out_shape=[jax.ShapeDtypeStruct((rows, COL_S5), MXU_DTYPE),
                   jax.ShapeDtypeStruct((S5_PAIRS, nch, S5_T * S5_PAIR_W), MXU_DTYPE)],
        compiler_params=_params(("arbitrary",)),
        name="in_proj",
    )(stream, mods, w_in, *tabs, perm)


def _rope_tables(seq):
    half_a = HEAD_DIM // 4
    half_r = HEAD_DIM // 2
    nrow = seq // GRID_W
    inv_a = ROPE_BASE ** (-jnp.arange(half_a, dtype=F32) / half_a)
    inv_r = ROPE_BASE ** (-jnp.arange(half_r, dtype=F32) / half_r)
    ang_r = jnp.arange(nrow, dtype=F32)[:, None] * inv_a[None, :]
    ang_c = jnp.arange(GRID_W, dtype=F32)[:, None] * inv_a[None, :]
    ang_t = jnp.arange(seq, dtype=F32)[:, None] * inv_r[None, :]
    cr, sr, cc, sc, ct, st = lax.optimization_barrier(
        (jnp.cos(ang_r), jnp.sin(ang_r), jnp.cos(ang_c), jnp.sin(ang_c), jnp.cos(ang_t), jnp.sin(ang_t)))
    by_row = lambda tab: jnp.broadcast_to(tab[:, None, :], (nrow, GRID_W, half_a)).reshape(seq, half_a)
    by_col = lambda tab: jnp.broadcast_to(tab[None, :, :], (nrow, GRID_W, half_a)).reshape(seq, half_a)
    cr, sr, cc, sc = by_row(cr), by_row(sr), by_col(cc), by_col(sc)
    cos_a = jnp.concatenate([cr, cr, cc, cc] * 2, axis=-1)
    sin_a = jnp.concatenate([-sr, sr, -sc, sc] * 2, axis=-1)
    cos_r = jnp.concatenate([ct, ct] * 2, axis=-1)
    sin_r = jnp.concatenate([-st, st] * 2, axis=-1)
    pad = lambda tab, ident: jnp.pad(tab, ((CTX_PAD, 0), (0, 0)), constant_values=ident)
    return pad(cos_a, 1.0), pad(sin_a, 0.0), pad(cos_r, 1.0), pad(sin_r, 0.0)


def _swap_halves(x):
    if x.dtype.itemsize == 4:
        return pltpu.roll(x, 64, 1)
    packed = pltpu.bitcast(x, jnp.uint32)
    return pltpu.bitcast(pltpu.roll(packed, 64, 1), x.dtype)


def _dup_heads(x):
    sw = _swap_halves(x)
    lo = lax.broadcasted_iota(jnp.int32, x.shape, 1) < HEAD_DIM
    return jnp.where(lo, x, sw), jnp.where(lo, sw, x)


def _attn_kernel(sink_ref, q_ref, km_ref, kp_ref, kn_ref, vm_ref, vp_ref, vn_ref, kc_ref, vc_ref, mask_ref,
                 o_ref, k2_ref, v2_ref, kc2_ref, vc2_ref):
    i = pl.program_id(0)
    last_blk = pl.num_programs(0) * (ROW_TILE // ATT_BLOCK) - 1
    blk = ATT_BLOCK
    for dst, parts in ((k2_ref, (kp_ref, km_ref, kn_ref)), (v2_ref, (vp_ref, vm_ref, vn_ref))):
        row = 0
        for part in parts:
            a, b = _dup_heads(part[...])
            n = part.shape[0]
            dst[0, row:row + n, :] = a
            dst[1, row:row + n, :] = b
            row += n
    for dst, src in ((kc2_ref, kc_ref), (vc2_ref, vc_ref)):
        a, b = _dup_heads(src[...])
        dst[0] = a
        dst[1] = b

    lo = lax.broadcasted_iota(jnp.int32, (blk, LANES), 1) < HEAD_DIM
    group = ATT_HEADS // ATT_KV_HEADS

    def body(j, carry):
        r0 = pl.multiple_of(j * blk, blk)
        gblk = i * (ROW_TILE // blk) + j
        sel = jnp.where(i == 0, 3, jnp.where(gblk == CTX_PAD // blk, 0, jnp.where(gblk == last_blk, 2, 1)))
        bias = mask_ref[sel]
        for kv in range(ATT_KV_HEADS):
            qt = q_ref[pl.ds(r0, blk), group * HEAD_DIM * kv:group * HEAD_DIM * (kv + 1)]
            parts = []
            for g in range(group):
                qc = qt[:, LANES * (g // 2):LANES * (g // 2 + 1)]
                keep = lo if g % 2 == 0 else jnp.logical_not(lo)
                parts.append(jnp.where(keep, qc, jnp.zeros_like(qc)))
            qs = jnp.concatenate(parts, axis=0)
            kl = k2_ref[kv, pl.ds(r0, 3 * blk), :]
            vl = v2_ref[kv, pl.ds(r0, 3 * blk), :]
            s_loc = _mm_nt(qs, kl)
            s_ctx = _mm_nt(qs, kc2_ref[kv])
            p_loc, p_ctx, inv = [], [], []
            for g in range(group):
                sl = s_loc[blk * g:blk * (g + 1)] + bias
                sc = s_ctx[blk * g:blk * (g + 1)]
                sk = sink_ref[group * kv + g]
                m = jnp.maximum(jnp.maximum(jnp.max(sl, axis=-1, keepdims=True),
                                            jnp.max(sc, axis=-1, keepdims=True)), sk)
                el = jnp.exp(sl - m)
                ec = jnp.exp(sc - m)
                den = jnp.sum(el, axis=-1, keepdims=True) + jnp.sum(ec, axis=-1, keepdims=True) + jnp.exp(sk - m)
                p_loc.append(el.astype(vl.dtype))
                p_ctx.append(ec.astype(vl.dtype))
                inv.append(1.0 / den)
            o = _mm(jnp.concatenate(p_loc, axis=0), vl) + _mm(jnp.concatenate(p_ctx, axis=0), vc2_ref[kv])
            outs = [o[blk * g:blk * (g + 1)] * inv[g] for g in range(group)]
            for half in range(group // 2):
                y = jnp.where(lo, outs[2 * half], outs[2 * half + 1])
                c0 = group * HEAD_DIM * kv + LANES * half
                o_ref[pl.ds(r0, blk), c0:c0 + LANES] = y.astype(o_ref.dtype)
        return carry

    lax.fori_loop(0, ROW_TILE // blk, body, 0)


def _attention_masks():
    qi = jnp.arange(ATT_BLOCK)[:, None]
    kj = jnp.arange(3 * ATT_BLOCK)[None, :]
    band = jnp.abs(kj - ATT_BLOCK - qi) <= ATT_BLOCK
    first = band & (kj >= ATT_BLOCK)
    last = band & (kj < 2 * ATT_BLOCK)
    none = jnp.zeros_like(band)
    masks = jnp.stack([first, band, last, none])
    return jnp.where(masks, 0.0, NEG_BIG).astype(F32)


def _attention(proj, sink, masks):
    rows = proj.shape[0]
    tm, blk = ROW_TILE, ATT_BLOCK
    per = tm // blk
    nblk = rows // blk
    ck, cv = COL_AK // KV_WIDTH, COL_AV // KV_WIDTH
    dt = proj.dtype
    return pl.pallas_call(
        _attn_kernel,
        grid=(rows // tm,),
        in_specs=[
            pl.BlockSpec(memory_space=pltpu.SMEM),
            pl.BlockSpec((tm, ATT_WIDTH), lambda i: (i, 0)),
            pl.BlockSpec((tm, KV_WIDTH), lambda i: (i, ck)),
            pl.BlockSpec((blk, KV_WIDTH), lambda i: (jnp.maximum(i * per - 1, 0), ck)),
            pl.BlockSpec((blk, KV_WIDTH), lambda i: (jnp.minimum((i + 1) * per, nblk - 1), ck)),
            pl.BlockSpec((tm, KV_WIDTH), lambda i: (i, cv)),
            pl.BlockSpec((blk, KV_WIDTH), lambda i: (jnp.maximum(i * per - 1, 0), cv)),
            pl.BlockSpec((blk, KV_WIDTH), lambda i: (jnp.minimum((i + 1) * per, nblk - 1), cv)),
            pl.BlockSpec((CTX_LEN, KV_WIDTH), lambda i: (0, ck)),
            pl.BlockSpec((CTX_LEN, KV_WIDTH), lambda i: (0, cv)),
            pl.BlockSpec((4, blk, 3 * blk), lambda i: (0, 0, 0)),
        ],
        out_specs=pl.BlockSpec((tm, ATT_WIDTH), lambda i: (i, 0)),
        out_shape=jax.ShapeDtypeStruct((rows, ATT_WIDTH), dt),
        scratch_shapes=[
            pltpu.VMEM((2, tm + 2 * blk, KV_WIDTH), dt),
            pltpu.VMEM((2, tm + 2 * blk, KV_WIDTH), dt),
            pltpu.VMEM((2, CTX_LEN, KV_WIDTH), dt),
            pltpu.VMEM((2, CTX_LEN, KV_WIDTH), dt),
        ],
        compiler_params=_params(("arbitrary",)),
        name="attention",
    )(sink, proj, proj, proj, proj, proj, proj, proj, proj, proj, masks)


def _ret_kernel(lg_ref, q_ref, k_ref, v_ref, g_ref, o_ref, sb_ref, s_ref, dm_ref, tab_ref, gbd_ref):
    ph = pl.program_id(0)
    t = pl.program_id(1)
    nlat = pl.num_programs(1) - 2
    c = RET_CHUNK
    w = RET_WIDTH
    mdt = sb_ref.dtype

    def lane_vec(direction, shape, axis):
        head = lax.broadcasted_iota(jnp.int32, shape, axis) // HEAD_DIM
        out = jnp.full(shape, lg_ref[direction, RET_HEADS - 1], F32)
        for h in range(RET_HEADS - 2, -1, -1):
            out = jnp.where(head == h, lg_ref[direction, h], out)
        return out

    @pl.when(jnp.logical_and(ph == 0, t == 0))
    def _init_tables():
        diff = (lax.broadcasted_iota(jnp.int32, (c, c), 0) - lax.broadcasted_iota(jnp.int32, (c, c), 1)).astype(F32)
        for h in range(RET_HEADS):
            dm_ref[h] = jnp.exp(jnp.where(diff >= 0, diff * lg_ref[0, h], -diff * lg_ref[1, h]))
        pos = lax.broadcasted_iota(jnp.int32, (c, w), 0).astype(F32)
        lgf = lane_vec(0, (c, w), 1)
        lgb = lane_vec(1, (c, w), 1)
        tab_ref[0] = jnp.exp((c - 1.0 - pos) * lgf)
        tab_ref[1] = jnp.exp((pos + 1.0) * lgf)
        tab_ref[2] = jnp.exp(pos * lgb)
        tab_ref[3] = jnp.exp((c - pos) * lgb)
        same = (lax.broadcasted_iota(jnp.int32, (w, w), 0) // HEAD_DIM
                == lax.broadcasted_iota(jnp.int32, (w, w), 1) // HEAD_DIM)
        bd = jnp.where(same, 1.0, 0.0)
        gbd_ref[0] = bd * jnp.exp(c * lane_vec(0, (w, w), 0))
        gbd_ref[1] = bd * jnp.exp(c * lane_vec(1, (w, w), 0))
        gbd_ref[2] = bd

    @pl.when(t == 0)
    def _reset_state():
        s_ref[...] = jnp.zeros_like(s_ref)

    def state_update(direction, key_tab):
        kw = (k_ref[...].astype(F32) * tab_ref[key_tab]).astype(mdt)
        u = _mm_tn(kw, v_ref[...])
        s_ref[...] = gbd_ref[direction] * s_ref[...] + gbd_ref[2] * u

    @pl.when(jnp.logical_and(ph == 0, t <= nlat))
    def _backward_states():
        idx = jnp.where(t == 0, 0, nlat + 1 - t)
        sb_ref[idx] = s_ref[...].astype(mdt)
        state_update(1, 2)

    @pl.when(jnp.logical_and(ph == 1, t == 1))
    def _pad_rows():
        o_ref[...] = jnp.zeros_like(o_ref)

    @pl.when(jnp.logical_and(ph == 1, t != 1))
    def _outputs():
        idx = jnp.where(t == 0, 0, t - 1)
        q = q_ref[...]
        k = k_ref[...]
        v = v_ref[...]
        head = lax.broadcasted_iota(jnp.int32, (c, w), 1) // HEAD_DIM
        qs = jnp.concatenate([jnp.where(head == h, q, jnp.zeros_like(q)) for h in range(RET_HEADS)], axis=0)
        sc = _mm_nt(qs, k)
        scd = jnp.concatenate([sc[c * h:c * (h + 1)] * dm_ref[h] for h in range(RET_HEADS)], axis=0).astype(mdt)
        oi = _mm(scd, v)
        o = jnp.zeros((c, w), F32)
        for h in range(RET_HEADS):
            o = o + jnp.where(head == h, oi[c * h:c * (h + 1)], 0.0)
        qf = q.astype(F32)
        o = o + _mm((qf * tab_ref[1]).astype(mdt), s_ref[...].astype(mdt))
        o = o + _mm((qf * tab_ref[3]).astype(mdt), sb_ref[idx])
        state_update(0, 0)
        avg = (gbd_ref[2] * (1.0 / HEAD_DIM)).astype(mdt)

        def head_mean(val):
            hi = val.astype(mdt)
            lo = (val - hi.astype(F32)).astype(mdt)
            return _mm(hi, avg) + _mm(lo, avg)

        d = o - head_mean(o)
        var = head_mean(d * d)
        gate = g_ref[...].astype(F32)
        y = d * lax.rsqrt(var + GN_EPS) * (gate * jax.nn.sigmoid(gate))
        o_ref[...] = y.astype(o_ref.dtype)


def _retention(proj, log_gamma):
    rows = proj.shape[0]
    c = RET_CHUNK
    nlat = (rows - CTX_PAD) // c
    dt = proj.dtype
    cq, ckk, cvv, cg = (COL_RQ // RET_WIDTH, COL_RK // RET_WIDTH, COL_RV // RET_WIDTH, COL_RG // RET_WIDTH)

    def kv_blk(ph, t):
        back = jnp.where(t == 0, 0, jnp.maximum(nlat + 2 - t, 2))
        return jnp.where(ph == 0, back, t)

    def fw_blk(ph, t):
        return jnp.where(ph == 0, 0, t)

    return pl.pallas_call(
        _ret_kernel,
        grid=(2, nlat + 2),
        in_specs=[
            pl.BlockSpec(memory_space=pltpu.SMEM),
            pl.BlockSpec((c, RET_WIDTH), lambda ph, t: (fw_blk(ph, t), cq)),
            pl.BlockSpec((c, RET_WIDTH), lambda ph, t: (kv_blk(ph, t), ckk)),
            pl.BlockSpec((c, RET_WIDTH), lambda ph, t: (kv_blk(ph, t), cvv)),
            pl.BlockSpec((c, RET_WIDTH), lambda ph, t: (fw_blk(ph, t), cg)),
        ],
        out_specs=pl.BlockSpec((c, RET_WIDTH), lambda ph, t: (fw_blk(ph, t), 0)),
        out_shape=jax.ShapeDtypeStruct((rows, RET_WIDTH), dt),
        scratch_shapes=[
            pltpu.VMEM((nlat + 1, RET_WIDTH, RET_WIDTH), dt),
            pltpu.VMEM((RET_WIDTH, RET_WIDTH), F32),
            pltpu.VMEM((RET_HEADS, c, c), F32),
            pltpu.VMEM((4, c, RET_WIDTH), F32),
            pltpu.VMEM((3, RET_WIDTH, RET_WIDTH), F32),
        ],
        compiler_params=_params(("arbitrary", "arbitrary")),
        name="retention",
    )(log_gamma, proj, proj, proj, proj)


def _s5_weights(lam_re, lam_im, b_re, b_im, c_re, c_im, log_dt, d_skip):
    hp = lax.Precision.HIGHEST
    tt, g, n, p, a = S5_T, S5_GROUPS, S5_STATE, S5_CH, S5_PAIRS
    lam = lax.complex(lam_re.astype(F32), lam_im.astype(F32))
    dtv = jnp.exp(log_dt.astype(F32))[..., None]
    lam_bar = jnp.exp(lam * dtv)
    bbar = ((lam_bar - 1.0) / lam)[..., None] * lax.complex(b_re.astype(F32), b_im.astype(F32))
    cmat = lax.complex(c_re.astype(F32), c_im.astype(F32))
    pw = [jnp.ones_like(lam_bar)]
    for _ in range(tt):
        pw.append(pw[-1] * lam_bar)
    pw = jnp.stack(pw, axis=1)
    eye2 = jnp.eye(2, dtype=F32)
    j = jnp.arange(tt)

    def drive(direction, exps):
        m = pw[direction][exps][..., None] * bbar[direction][None]
        m = jnp.stack([jnp.real(m), jnp.imag(m)], axis=0)
        m = m.reshape(2, tt, a, 2, n, p)
        full = jnp.einsum('rjagnp,gh->ajgprhn', m, eye2)
        return full.reshape(a, tt * 2 * p, 2 * 2 * n)

    w_drive = jnp.concatenate([drive(0, tt - 1 - j), drive(1, j)], axis=-1)

    def read(direction, exps):
        y = cmat[direction][None] * pw[direction][exps][:, :, None, :]
        y = jnp.stack([jnp.real(y), -jnp.imag(y)], axis=0)
        y = y.reshape(2, tt, a, 2, p, n)
        full = jnp.einsum('riagpn,gh->arhnigp', y, eye2)
        return full.reshape(a, 2 * 2 * n, tt * 2 * p)

    w_read = jnp.concatenate([read(0, j + 1), read(1, tt - j)], axis=1)

    def lag_kernels(direction):
        return jnp.real(jnp.einsum('gqn,dgn,gnp->dgpq', cmat[direction], pw[direction][:tt], bbar[direction],
                                   precision=hp))

    kf, kb = lag_kernels(0), lag_kernels(1)
    lag = j[None, :] - j[:, None]
    toep = (jnp.where((lag >= 0)[:, :, None, None, None], kf[jnp.clip(lag, 0, tt - 1)], 0.0)
            + jnp.where((lag <= 0)[:, :, None, None, None], kb[jnp.clip(-lag, 0, tt - 1)], 0.0))
    toep = toep.reshape(tt, tt, a, 2, p, p)
    w_intra = jnp.einsum('jiagpq,gh->ajgpihq', toep, eye2).reshape(a, tt * 2 * p, tt * 2 * p)

    lam_t = pw[:, tt].reshape(2, a, 2 * n)
    lam_t = jnp.concatenate([jnp.real(lam_t[0]), jnp.imag(lam_t[0]), jnp.real(lam_t[1]), jnp.imag(lam_t[1])],
                            axis=-1).reshape(a, 1, 4 * 2 * n)
    skip = jnp.tile(d_skip.astype(F32).reshape(a, 1, 2 * p), (1, tt, 1)).reshape(a, 1, tt * 2 * p)
    return w_drive.astype(MXU_DTYPE), w_read.astype(MXU_DTYPE), w_intra.astype(MXU_DTYPE), lam_t, skip


def _s5_drive_kernel(u_ref, w_ref, o_ref):
    o_ref[0] = _mm(u_ref[0], w_ref[0])


def _s5_drive(u_pairs, w_drive, layer):
    a, nch, wd = u_pairs.shape
    return pl.pallas_call(
        _s5_drive_kernel,
        grid=(a,),
        in_specs=[pl.BlockSpec((1, nch, wd), lambda i: (i, 0, 0)),
                  pl.BlockSpec((None, 1, wd, 2 * wd), lambda i: (layer, i, 0, 0))],
        out_specs=pl.BlockSpec((1, nch, 2 * wd), lambda i: (i, 0, 0)),
        out_shape=jax.ShapeDtypeStruct((a, nch, 2 * wd), F32),
        compiler_params=_params(("arbitrary",)),
        name="s5_drive",
    )(u_pairs, w_drive)


def _s5_scan_kernel(ef_ref, eb_ref, lam_ref, sf_ref, sb_ref, stf_ref, stb_ref):
    t = pl.program_id(0)
    hw = LANES
    sub = 8
    npair = lam_ref.shape[0]

    @pl.when(t == 0)
    def _reset():
        stf_ref[...] = jnp.zeros_like(stf_ref)
        stb_ref[...] = jnp.zeros_like(stb_ref)

    def step(lr, li, sr, si, e):
        return lr * sr - li * si + e[:, 0:hw], lr * si + li * sr + e[:, hw:2 * hw]

    def run(nrows):
        ngroups = nrows // sub

        def body(gi, carry):
            r0 = pl.multiple_of(gi * sub, sub)
            rb0 = pl.multiple_of((ngroups - 1 - gi) * sub, sub)
            new = []
            for a in range(npair):
                lam = lam_ref[a]
                lfr, lfi, lbr, lbi = (lam[:, k * hw:(k + 1) * hw] for k in range(4))
                fr, fi, br, bi = carry[a]
                ef = ef_ref[a, pl.ds(r0, sub), :]
                eb = eb_ref[a, pl.ds(rb0, sub), :]
                before_f = []
                for k in range(sub):
                    before_f.append(jnp.concatenate([fr, fi], axis=1))
                    fr, fi = step(lfr, lfi, fr, fi, ef[k:k + 1])
                sf_ref[a, pl.ds(r0, sub), :] = jnp.concatenate(before_f, axis=0)
                before_b = [None] * sub
                for k in range(sub - 1, -1, -1):
                    before_b[k] = jnp.concatenate([br, bi], axis=1)
                    br, bi = step(lbr, lbi, br, bi, eb[k:k + 1])
                sb_ref[a, pl.ds(rb0, sub), :] = jnp.concatenate(before_b, axis=0)
                new.append((fr, fi, br, bi))
            return tuple(new)

        init = tuple((stf_ref[a, :, 0:hw], stf_ref[a, :, hw:2 * hw], stb_ref[a, :, 0:hw], stb_ref[a, :, hw:2 * hw])
                     for a in range(npair))
        final = lax.fori_loop(0, ngroups, body, init)
        for a in range(npair):
            fr, fi, br, bi = final[a]
            stf_ref[a, :, 0:hw] = fr
            stf_ref[a, :, hw:2 * hw] = fi
            stb_ref[a, :, 0:hw] = br
            stb_ref[a, :, hw:2 * hw] = bi

    @pl.when(t == 0)
    def _context():
        sf_ref[...] = jnp.zeros_like(sf_ref)
        sb_ref[...] = jnp.zeros_like(sb_ref)
        run(CTX_LEN // S5_T)

    @pl.when(t > 0)
    def _latent():
        run(S5_TILE)


def _s5_scan(drive, lam_t, layer):
    a, nch, wd2 = drive.shape
    wd = wd2 // 2
    nt = nch // S5_TILE

    def bwd(t):
        return jnp.where(t == 0, 0, nt - t)

    return pl.pallas_call(
        _s5_scan_kernel,
        grid=(nt,),
        in_specs=[pl.BlockSpec((a, S5_TILE, wd), lambda t: (0, t, 0)),
                  pl.BlockSpec((a, S5_TILE, wd), lambda t: (0, bwd(t), 1)),
                  pl.BlockSpec((None, a, 1, wd2), lambda t: (layer, 0, 0, 0))],
        out_specs=[pl.BlockSpec((a, S5_TILE, wd), lambda t: (0, t, 0)),
                   pl.BlockSpec((a, S5_TILE, wd), lambda t: (0, bwd(t), 0))],
        out_shape=[jax.ShapeDtypeStruct((a, nch, wd), F32), jax.ShapeDtypeStruct((a, nch, wd), F32)],
        scratch_shapes=[pltpu.VMEM((a, 1, wd), F32), pltpu.VMEM((a, 1, wd), F32)],
        compiler_params=_params(("arbitrary",)),
        name="s5_scan",
    )(drive, drive, lam_t)


def _s5_read_kernel(u_ref, sf_ref, sb_ref, wi_ref, wr_ref, skip_ref, o_ref):
    u = u_ref[0]
    wd = u.shape[-1]
    mdt = u.dtype
    y = _mm(u, wi_ref[0])
    y = y + _mm(sf_ref[0].astype(mdt), wr_ref[0, 0:wd, :])
    y = y + _mm(sb_ref[0].astype(mdt), wr_ref[0, wd:2 * wd, :])
    o_ref[0] = y + u.astype(F32) * skip_ref[0]


def _s5_read(u_pairs, sf, sb, w_intra, w_read, skip, layer):
    a, nch, wd = u_pairs.shape
    blk = lambda *shape: pl.BlockSpec((1,) + shape, lambda i: (i, 0, 0))
    wblk = lambda *shape: pl.BlockSpec((None, 1) + shape, lambda i: (layer, i, 0, 0))
    return pl.pallas_call(
        _s5_read_kernel,
        grid=(a,),
        in_specs=[blk(nch, wd), blk(nch, wd), blk(nch, wd), wblk(wd, wd), wblk(2 * wd, wd), wblk(1, wd)],
        out_specs=blk(nch, wd),
        out_shape=jax.ShapeDtypeStruct((a, nch, wd), F32),
        compiler_params=_params(("arbitrary",)),
        name="s5_read",
    )(u_pairs, sf, sb, w_intra, w_read, skip)


def _s5_mixer(u_pairs, weights, layer):
    w_drive, w_read, w_intra, lam_t, skip = weights
    drive = _s5_drive(u_pairs, w_drive, layer)
    sf, sb = _s5_scan(drive, lam_t, layer)
    return _s5_read(u_pairs, sf, sb, w_intra, w_read, skip, layer)


def _layer_norm(x, g, b):
    mu = jnp.mean(x, axis=-1, keepdims=True)
    d = x - mu
    var = jnp.mean(d * d, axis=-1, keepdims=True)
    return d * lax.rsqrt(var + LN_EPS) * g + b


def _post_kernel(x_ref, att_ref, ret_ref, s5_ref, mod_ref, permt_ref, wglu_ref, bglu_ref, woa_ref, wor_ref, wos_ref,
                 g1_ref, b1_ref, w1_ref, w2_ref, g2_ref, b2_ref, o_ref):
    mdt = w1_ref.dtype
    x = x_ref[...]
    zrows = []
    for i in range(S5_T):
        src_vreg, src_blk = divmod(i * S5_PAIR_W, LANES)
        src_blk //= S5_PAIR_W
        cols = [_lane_block_shuffle(lambda a: s5_ref[a, :, src_vreg * LANES:(src_vreg + 1) * LANES], src_blk, w)
                for w in range(S5_WIDTH // LANES)]
        zrows.append(jnp.concatenate(cols, axis=1))
    z = jnp.concatenate(zrows, axis=0)
    z_hi = z.astype(mdt)
    z_lo = (z - z_hi.astype(F32)).astype(mdt)
    ys = _mm(permt_ref[...], z_hi) + _mm(permt_ref[...], z_lo)
    hs = jax.nn.gelu(ys)
    gate = jax.nn.sigmoid(_mm(hs.astype(mdt), wglu_ref[...]) + bglu_ref[...])
    s5 = (hs * gate).astype(mdt)
    ox = _mm(att_ref[...], woa_ref[...]) + _mm(ret_ref[...], wor_ref[...]) + _mm(s5, wos_ref[...])
    x1 = _layer_norm(DEEPNORM_ALPHA * x + mod_ref[2:3, :] * ox, g1_ref[...], b1_ref[...])
    h = (x1 * (1.0 + mod_ref[4:5, :]) + mod_ref[3:4, :]).astype(mdt)
    acc = jnp.zeros(x.shape, F32)
    for c in range(D_FF // FF_CHUNK):
        a = _mm(h, w1_ref[:, FF_CHUNK * c:FF_CHUNK * (c + 1)])
        a = jnp.square(jnp.maximum(a, 0.0)).astype(mdt)
        acc = acc + _mm(a, w2_ref[FF_CHUNK * c:FF_CHUNK * (c + 1), :])
    o_ref[...] = _layer_norm(DEEPNORM_ALPHA * x1 + mod_ref[5:6, :] * acc, g2_ref[...], b2_ref[...])


def _post(stream, att, ret, s5_pairs, mods, layer, permt, wglu, bglu, wo, g1, b1, w1, w2, g2, b2, skip_context):
    rows = stream.shape[0]
    tm = ROW_TILE
    off = CTX_PAD // tm if skip_context else 0
    row_blk = lambda width: pl.BlockSpec((tm, width), lambda i: (i + off, 0))
    full = lambda arr: pl.BlockSpec(arr.shape, lambda i: (0,) * arr.ndim)
    woa, wor, wos = wo[:ATT_WIDTH], wo[ATT_WIDTH:ATT_WIDTH + RET_WIDTH], wo[ATT_WIDTH + RET_WIDTH:]
    vec = lambda v: v.reshape(1, -1).astype(F32)
    small = [permt, wglu, vec(bglu), woa, wor, wos, vec(g1), vec(b1), w1, w2, vec(g2), vec(b2)]
    return pl.pallas_call(
        _post_kernel,
        grid=(rows // tm - off,),
        in_specs=[row_blk(D_MODEL), row_blk(ATT_WIDTH), row_blk(RET_WIDTH),
                  pl.BlockSpec((S5_PAIRS, tm // S5_T, S5_T * S5_PAIR_W), lambda i: (0, i + off, 0)),
                  pl.BlockSpec((None, None, N_ADA, D_MODEL),
                               lambda i: (layer, jnp.where(i + off == 0, 1, 0), 0, 0))]
                 + [full(arr) for arr in small],
        out_specs=pl.BlockSpec((tm, D_MODEL), lambda i: (i, 0)),
        out_shape=jax.ShapeDtypeStruct((rows - off * tm, D_MODEL), F32),
        compiler_params=_params(("arbitrary",)),
        name="post",
    )(stream, att, ret, s5_pairs, mods, *small)


def kernel(x, c, ctx, c_ctx, w_ada, b_ada, w_in, att_sink, ret_decay_logit, s5_lambda_re, s5_lambda_im, s5_b_re,
           s5_b_im, s5_c_re, s5_c_im, s5_log_dt, s5_d, w_glu, b_glu, w_out, ln1_g, ln1_b, w_ff1, w_ff2, ln2_g,
           ln2_b):
    assert x.shape[0] == 1 and x.shape[2] == D_MODEL and ctx.shape[1] == CTX_LEN
    seq = x.shape[1]
    assert seq % ROW_TILE == 0
    stream = jnp.concatenate([ctx[0], jnp.zeros((CTX_PAD - CTX_LEN, D_MODEL), F32), x[0]], axis=0)
    cond = jnp.zeros((8, D_MODEL), F32).at[0].set(c[0]).at[1].set(c_ctx)
    mods = _modulation(cond, w_ada, b_ada).reshape(DEPTH, 8, N_ADA, D_MODEL)
    tabs = _rope_tables(seq)
    masks = _attention_masks()
    perm = _chunk_perm(ROW_TILE, MXU_DTYPE)
    permt = perm.T
    col_scale = jnp.ones((IN_WIDTH,), F32).at[COL_AQ:COL_AK].set(HEAD_DIM ** -0.5)
    col_scale = col_scale.at[COL_RQ:COL_RK].set(HEAD_DIM ** -0.5)
    s5w = jax.vmap(_s5_weights)(s5_lambda_re, s5_lambda_im, s5_b_re, s5_b_im, s5_c_re, s5_c_im, s5_log_dt, s5_d)
    log_gamma = jax.nn.log_sigmoid(ret_decay_logit.astype(F32))
    for l in range(DEPTH):
        proj, u_pairs = _in_proj(stream, mods, l, (w_in[l] * col_scale).astype(MXU_DTYPE), tabs, perm)
        att = _attention(proj, att_sink[l].astype(F32), masks)
        ret = _retention(proj, log_gamma[l])
        s5 = _s5_mixer(u_pairs, s5w, l)
        stream = _post(stream, att, ret, s5, mods, l, permt, w_glu[l].astype(MXU_DTYPE), b_glu[l],
                       w_out[l].astype(MXU_DTYPE), ln1_g[l], ln1_b[l], w_ff1[l].astype(MXU_DTYPE),
                       w_ff2[l].astype(MXU_DTYPE), ln2_g[l], ln2_b[l], skip_context=(l == DEPTH - 1))
    return stream[None]
```

```python
import functools
import math

import jax
import jax.numpy as jnp
from jax import lax
from jax.experimental import pallas as pl
from jax.experimental.pallas import tpu as pltpu

F32 = jnp.float32
MXU_DTYPE = jnp.bfloat16

D_MODEL = 1024
DEPTH = 4
GRID_W = 64
CTX_LEN = 256
CTX_PAD = 512
HEAD_DIM = 64
ATT_HEADS = 8
ATT_KV_HEADS = 2
ATT_BLOCK = 128
ROPE_BASE = 10000.0
RET_HEADS = 4
RET_CHUNK = 256
S5_CH = 16
S5_GROUPS = 16
S5_STATE = 64
S5_T = 8
S5_PAIRS = S5_GROUPS // 2
S5_PAIR_W = 2 * S5_CH
S5_TILE = 64
ATT_WIDTH = ATT_HEADS * HEAD_DIM
KV_WIDTH = ATT_KV_HEADS * HEAD_DIM
RET_WIDTH = RET_HEADS * HEAD_DIM
S5_WIDTH = S5_GROUPS * S5_CH
IN_WIDTH = ATT_WIDTH + 2 * KV_WIDTH + 4 * RET_WIDTH + S5_WIDTH
D_FF = 4 * D_MODEL
FF_CHUNK = 1024
N_ADA = 6
LN_EPS = 1e-5
GN_EPS = 1e-5
DEEPNORM_ALPHA = (2 * DEPTH) ** 0.25
ROW_TILE = 512
NEG_BIG = -1e30
LANES = 128
VMEM_LIMIT = 56 * 1024 * 1024

COL_AQ, COL_AK, COL_AV = 0, ATT_WIDTH, ATT_WIDTH + KV_WIDTH
COL_RQ = ATT_WIDTH + 2 * KV_WIDTH
COL_RK, COL_RV, COL_RG = COL_RQ + RET_WIDTH, COL_RQ + 2 * RET_WIDTH, COL_RQ + 3 * RET_WIDTH
COL_S5 = COL_RQ + 4 * RET_WIDTH


def _mm(a, b):
    return jnp.dot(a, b, preferred_element_type=F32)


def _mm_nt(a, b):
    return lax.dot_general(a, b, (((1,), (1,)), ((), ())), preferred_element_type=F32)


def _mm_tn(a, b):
    return lax.dot_general(a, b, (((0,), (0,)), ((), ())), preferred_element_type=F32)


def _params(sem):
    return pltpu.CompilerParams(dimension_semantics=sem, vmem_limit_bytes=VMEM_LIMIT)


def _mod_kernel(cond_ref, w_ref, b_ref, o_ref):
    c = cond_ref[...]
    s = c * jax.nn.sigmoid(c)
    o_ref[0] = jnp.dot(s, w_ref[0], preferred_element_type=F32, precision=lax.Precision.HIGHEST) + b_ref[0]


def _modulation(cond, w_ada, b_ada):
    tn = 1536
    n = N_ADA * D_MODEL
    return pl.pallas_call(
        _mod_kernel,
        grid=(DEPTH, n // tn),
        in_specs=[
            pl.BlockSpec((8, D_MODEL), lambda l, j: (0, 0)),
            pl.BlockSpec((1, D_MODEL, tn), lambda l, j: (l, 0, j)),
            pl.BlockSpec((1, 1, tn), lambda l, j: (l, 0, j)),
        ],
        out_specs=pl.BlockSpec((1, 8, tn), lambda l, j: (l, 0, j)),
        out_shape=jax.ShapeDtypeStruct((DEPTH, 8, n), F32),
        compiler_params=_params(("arbitrary", "arbitrary")),
        name="modulation",
    )(cond, w_ada, b_ada.reshape(DEPTH, 1, n))


def _lane_block_shuffle(src_rows, src_lane_blk, out_vreg):
    acc = None
    for q in range(LANES // S5_PAIR_W):
        piece = src_rows(out_vreg * (LANES // S5_PAIR_W) + q)
        shift = (S5_PAIR_W * (q - src_lane_blk)) % LANES
        if shift:
            piece = pltpu.roll(piece, shift, 1)
        if acc is None:
            acc = piece
        else:
            lane_blk = lax.broadcasted_iota(jnp.int32, piece.shape, 1) // S5_PAIR_W
            acc = jnp.where(lane_blk == q, piece, acc)
    return acc


def _in_proj_kernel(x_ref, mod_ref, w_ref, ca_ref, sa_ref, cr_ref, sr_ref, perm_ref, o_ref, u_ref):
    x = x_ref[...]
    h = (x * (1.0 + mod_ref[1:2, :]) + mod_ref[0:1, :]).astype(w_ref.dtype)
    lane = lax.broadcasted_iota(jnp.int32, (x.shape[0], LANES), 1)
    first_att = (lane & 31) < 16
    first_ret = (lane & 63) < 32

    def proj(c0, c1):
        return _mm(h, w_ref[:, c0:c1])

    def rope_store(c0, width, cos, sin, first, half):
        p = proj(c0, c0 + width)
        for b in range(width // LANES):
            blk = p[:, LANES * b:LANES * (b + 1)]
            rot = jnp.where(first, pltpu.roll(blk, LANES - half, 1), pltpu.roll(blk, half, 1))
            o_ref[:, c0 + LANES * b:c0 + LANES * (b + 1)] = (blk * cos + rot * sin).astype(o_ref.dtype)

    ca, sa, cr, sr = ca_ref[...], sa_ref[...], cr_ref[...], sr_ref[...]
    rope_store(COL_AQ, ATT_WIDTH, ca, sa, first_att, 16)
    rope_store(COL_AK, KV_WIDTH, ca, sa, first_att, 16)
    o_ref[:, COL_AV:COL_RQ] = proj(COL_AV, COL_RQ).astype(o_ref.dtype)
    rope_store(COL_RQ, RET_WIDTH, cr, sr, first_ret, 32)
    rope_store(COL_RK, RET_WIDTH, cr, sr, first_ret, 32)
    o_ref[:, COL_RV:COL_S5] = proj(COL_RV, COL_S5).astype(o_ref.dtype)
    u = proj(COL_S5, IN_WIDTH).astype(w_ref.dtype)
    g = _mm(perm_ref[...], u)
    nchunk = x.shape[0] // S5_T
    for a in range(S5_PAIRS):
        vreg_col, lane_blk = divmod(a * S5_PAIR_W, LANES)
        lane_blk //= S5_PAIR_W
        for v in range(S5_T * S5_PAIR_W // LANES):
            slab = _lane_block_shuffle(
                lambda j: g[nchunk * j:nchunk * (j + 1), vreg_col * LANES:(vreg_col + 1) * LANES], lane_blk, v)
            u_ref[a, :, LANES * v:LANES * (v + 1)] = slab.astype(u_ref.dtype)


def _chunk_perm(tile_rows, dtype):
    nchunk = tile_rows // S5_T
    r = jnp.arange(tile_rows)
    src = S5_T * (r % nchunk) + r // nchunk
    return (src[:, None] == jnp.arange(tile_rows)[None, :]).astype(dtype)


def _in_proj(stream, mods, layer, w_in, tabs, perm):
    rows = stream.shape[0]
    tm = ROW_TILE
    tab_spec = pl.BlockSpec((tm, LANES), lambda i: (i, 0))
    nch = rows // S5_T
    return pl.pallas_call(
        _in_proj_kernel,
        grid=(rows // tm,),
        in_specs=[
            pl.BlockSpec((tm, D_MODEL), lambda i: (i, 0)),
            pl.BlockSpec((None, None, N_ADA, D_MODEL), lambda i: (layer, jnp.where(i == 0, 1, 0), 0, 0)),
            pl.BlockSpec((D_MODEL, IN_WIDTH), lambda i: (0, 0)),
            tab_spec, tab_spec, tab_spec, tab_spec,
            pl.BlockSpec((tm, tm), lambda i: (0, 0)),
        ],
        out_specs=[pl.BlockSpec((tm, COL_S5), lambda i: (i, 0)),
                   pl.BlockSpec((S5_PAIRS, tm // S5_T, S5_T * S5_PAIR_W), lambda i: (0, i, 0))],
        out_shape=[jax.ShapeDtypeStruct((rows, COL_S5), MXU_DTYPE),
                   jax.ShapeDtypeStruct((S5_PAIRS, nch, S5_T * S5_PAIR_W), MXU_DTYPE)],
        compiler_params=_params(("arbitrary",)),
        name="in_proj",
    )(stream, mods, w_in, *tabs, perm)


def _rope_tables(seq):
    half_a = HEAD_DIM // 4
    half_r = HEAD_DIM // 2
    nrow = seq // GRID_W
    inv_a = ROPE_BASE ** (-jnp.arange(half_a, dtype=F32) / half_a)
    inv_r = ROPE_BASE ** (-jnp.arange(half_r, dtype=F32) / half_r)
    ang_r = jnp.arange(nrow, dtype=F32)[:, None] * inv_a[None, :]
    ang_c = jnp.arange(GRID_W, dtype=F32)[:, None] * inv_a[None, :]
    ang_t = jnp.arange(seq, dtype=F32)[:, None] * inv_r[None, :]
    hp = lax.Precision.HIGHEST
    lane = jnp.arange(LANES)
    within = lane % HEAD_DIM
    pick_a = (within % half_a)[None, :] == jnp.arange(half_a)[:, None]
    exp_row = (pick_a & (within < 2 * half_a)[None, :]).astype(F32)
    exp_col = (pick_a & (within >= 2 * half_a)[None, :]).astype(F32)
    exp_t = ((lane % half_r)[None, :] == jnp.arange(half_r)[:, None]).astype(F32)
    sign_a = jnp.where(within % (2 * half_a) < half_a, -1.0, 1.0).astype(F32)
    sign_r = jnp.where(within < half_r, -1.0, 1.0).astype(F32)

    def att_table(fn):
        by_row = jnp.dot(fn(ang_r), exp_row, precision=hp)
        by_col = jnp.dot(fn(ang_c), exp_col, precision=hp)
        return (by_row[:, None, :] + by_col[None, :, :]).reshape(seq, LANES)

    cos_a = att_table(jnp.cos)
    sin_a = att_table(jnp.sin) * sign_a
    cos_r = jnp.dot(jnp.cos(ang_t), exp_t, precision=hp)
    sin_r = jnp.dot(jnp.sin(ang_t), exp_t, precision=hp) * sign_r
    pad = lambda tab, ident: jnp.pad(tab, ((CTX_PAD, 0), (0, 0)), constant_values=ident)
    return pad(cos_a, 1.0), pad(sin_a, 0.0), pad(cos_r, 1.0), pad(sin_r, 0.0)


def _swap_halves(x):
    if x.dtype.itemsize == 4:
        return pltpu.roll(x, 64, 1)
    packed = pltpu.bitcast(x, jnp.uint32)
    return pltpu.bitcast(pltpu.roll(packed, 64, 1), x.dtype)


def _dup_heads(x):
    sw = _swap_halves(x)
    lo = lax.broadcasted_iota(jnp.int32, x.shape, 1) < HEAD_DIM
    return jnp.where(lo, x, sw), jnp.where(lo, sw, x)


def _attn_kernel(sink_ref, q_ref, km_ref, kp_ref, kn_ref, vm_ref, vp_ref, vn_ref, kc_ref, vc_ref, mask_ref,
                 o_ref, k2_ref, v2_ref, kc2_ref, vc2_ref):
    i = pl.program_id(0)
    last_blk = pl.num_programs(0) * (ROW_TILE // ATT_BLOCK) - 1
    blk = ATT_BLOCK
    for dst, parts in ((k2_ref, (kp_ref, km_ref, kn_ref)), (v2_ref, (vp_ref, vm_ref, vn_ref))):
        row = 0
        for part in parts:
            a, b = _dup_heads(part[...])
            n = part.shape[0]
            dst[0, row:row + n, :] = a
            dst[1, row:row + n, :] = b
            row += n
    for dst, src in ((kc2_ref, kc_ref), (vc2_ref, vc_ref)):
        a, b = _dup_heads(src[...])
        dst[0] = a
        dst[1] = b

    lo = lax.broadcasted_iota(jnp.int32, (blk, LANES), 1) < HEAD_DIM
    group = ATT_HEADS // ATT_KV_HEADS

    def body(j, carry):
        r0 = pl.multiple_of(j * blk, blk)
        gblk = i * (ROW_TILE // blk) + j
        sel = jnp.where(i == 0, 3, jnp.where(gblk == CTX_PAD // blk, 0, jnp.where(gblk == last_blk, 2, 1)))
        bias = mask_ref[sel]
        for kv in range(ATT_KV_HEADS):
            qt = q_ref[pl.ds(r0, blk), group * HEAD_DIM * kv:group * HEAD_DIM * (kv + 1)]
            parts = []
            for g in range(group):
                qc = qt[:, LANES * (g // 2):LANES * (g // 2 + 1)]
                keep = lo if g % 2 == 0 else jnp.logical_not(lo)
                parts.append(jnp.where(keep, qc, jnp.zeros_like(qc)))
            qs = jnp.concatenate(parts, axis=0)
            kl = k2_ref[kv, pl.ds(r0, 3 * blk), :]
            vl = v2_ref[kv, pl.ds(r0, 3 * blk), :]
            s_loc = _mm_nt(qs, kl)
            s_ctx = _mm_nt(qs, kc2_ref[kv])
            p_loc, p_ctx, inv = [], [], []
            for g in range(group):
                sl = s_loc[blk * g:blk * (g + 1)] + bias
                sc = s_ctx[blk * g:blk * (g + 1)]
                sk = sink_ref[group * kv + g]
                m = jnp.maximum(jnp.maximum(jnp.max(sl, axis=-1, keepdims=True),
                                            jnp.max(sc, axis=-1, keepdims=True)), sk)
                el = jnp.exp(sl - m)
                ec = jnp.exp(sc - m)
                den = jnp.sum(el, axis=-1, keepdims=True) + jnp.sum(ec, axis=-1, keepdims=True) + jnp.exp(sk - m)
                p_loc.append(el.astype(vl.dtype))
                p_ctx.append(ec.astype(vl.dtype))
                inv.append(1.0 / den)
            o = _mm(jnp.concatenate(p_loc, axis=0), vl) + _mm(jnp.concatenate(p_ctx, axis=0), vc2_ref[kv])
            outs = [o[blk * g:blk * (g + 1)] * inv[g] for g in range(group)]
            for half in range(group // 2):
                y = jnp.where(lo, outs[2 * half], outs[2 * half + 1])
                c0 = group * HEAD_DIM * kv + LANES * half
                o_ref[pl.ds(r0, blk), c0:c0 + LANES] = y.astype(o_ref.dtype)
        return carry

    lax.fori_loop(0, ROW_TILE // blk, body, 0)


def _attention_masks():
    qi = jnp.arange(ATT_BLOCK)[:, None]
    kj = jnp.arange(3 * ATT_BLOCK)[None, :]
    band = jnp.abs(kj - ATT_BLOCK - qi) <= ATT_BLOCK
    first = band & (kj >= ATT_BLOCK)
    last = band & (kj < 2 * ATT_BLOCK)
    none = jnp.zeros_like(band)
    masks = jnp.stack([first, band, last, none])
    return jnp.where(masks, 0.0, NEG_BIG).astype(F32)


def _attention(proj, sink, masks):
    rows = proj.shape[0]
    tm, blk = ROW_TILE, ATT_BLOCK
    per = tm // blk
    nblk = rows // blk
    ck, cv = COL_AK // KV_WIDTH, COL_AV // KV_WIDTH
    dt = proj.dtype
    return pl.pallas_call(
        _attn_kernel,
        grid=(rows // tm,),
        in_specs=[
            pl.BlockSpec(memory_space=pltpu.SMEM),
            pl.BlockSpec((tm, ATT_WIDTH), lambda i: (i, 0)),
            pl.BlockSpec((tm, KV_WIDTH), lambda i: (i, ck)),
            pl.BlockSpec((blk, KV_WIDTH), lambda i: (jnp.maximum(i * per - 1, 0), ck)),
            pl.BlockSpec((blk, KV_WIDTH), lambda i: (jnp.minimum((i + 1) * per, nblk - 1), ck)),
            pl.BlockSpec((tm, KV_WIDTH), lambda i: (i, cv)),
            pl.BlockSpec((blk, KV_WIDTH), lambda i: (jnp.maximum(i * per - 1, 0), cv)),
            pl.BlockSpec((blk, KV_WIDTH), lambda i: (jnp.minimum((i + 1) * per, nblk - 1), cv)),
            pl.BlockSpec((CTX_LEN, KV_WIDTH), lambda i: (0, ck)),
            pl.BlockSpec((CTX_LEN, KV_WIDTH), lambda i: (0, cv)),
            pl.BlockSpec((4, blk, 3 * blk), lambda i: (0, 0, 0)),
        ],
        out_specs=pl.BlockSpec((tm, ATT_WIDTH), lambda i: (i, 0)),
        out_shape=jax.ShapeDtypeStruct((rows, ATT_WIDTH), dt),
        scratch_shapes=[
            pltpu.VMEM((2, tm + 2 * blk, KV_WIDTH), dt),
            pltpu.VMEM((2, tm + 2 * blk, KV_WIDTH), dt),
            pltpu.VMEM((2, CTX_LEN, KV_WIDTH), dt),
            pltpu.VMEM((2, CTX_LEN, KV_WIDTH), dt),
        ],
        compiler_params=_params(("arbitrary",)),
        name="attention",
    )(sink, proj, proj, proj, proj, proj, proj, proj, proj, proj, masks)


def _ret_kernel(lg_ref, q_ref, k_ref, v_ref, g_ref, o_ref, sb_ref, s_ref, dm_ref, tab_ref, gbd_ref):
    ph = pl.program_id(0)
    t = pl.program_id(1)
    nlat = pl.num_programs(1) - 2
    c = RET_CHUNK
    w = RET_WIDTH
    mdt = sb_ref.dtype

    def lane_vec(direction, shape, axis):
        head = lax.broadcasted_iota(jnp.int32, shape, axis) // HEAD_DIM
        out = jnp.full(shape, lg_ref[direction, RET_HEADS - 1], F32)
        for h in range(RET_HEADS - 2, -1, -1):
            out = jnp.where(head == h, lg_ref[direction, h], out)
        return out

    @pl.when(jnp.logical_and(ph == 0, t == 0))
    def _init_tables():
        diff = (lax.broadcasted_iota(jnp.int32, (c, c), 0) - lax.broadcasted_iota(jnp.int32, (c, c), 1)).astype(F32)
        for h in range(RET_HEADS):
            dm_ref[h] = jnp.exp(jnp.where(diff >= 0, diff * lg_ref[0, h], -diff * lg_ref[1, h]))
        pos = lax.broadcasted_iota(jnp.int32, (c, w), 0).astype(F32)
        lgf = lane_vec(0, (c, w), 1)
        lgb = lane_vec(1, (c, w), 1)
        tab_ref[0] = jnp.exp((c - 1.0 - pos) * lgf)
        tab_ref[1] = jnp.exp((pos + 1.0) * lgf)
        tab_ref[2] = jnp.exp(pos * lgb)
        tab_ref[3] = jnp.exp((c - pos) * lgb)
        same = (lax.broadcasted_iota(jnp.int32, (w, w), 0) // HEAD_DIM
                == lax.broadcasted_iota(jnp.int32, (w, w), 1) // HEAD_DIM)
        bd = jnp.where(same, 1.0, 0.0)
        gbd_ref[0] = bd * jnp.exp(c * lane_vec(0, (w, w), 0))
        gbd_ref[1] = bd * jnp.exp(c * lane_vec(1, (w, w), 0))
        gbd_ref[2] = bd

    @pl.when(t == 0)
    def _reset_state():
        s_ref[...] = jnp.zeros_like(s_ref)

    def state_update(direction, key_tab):
        kw = (k_ref[...].astype(F32) * tab_ref[key_tab]).astype(mdt)
        u = _mm_tn(kw, v_ref[...])
        s_ref[...] = gbd_ref[direction] * s_ref[...] + gbd_ref[2] * u

    @pl.when(jnp.logical_and(ph == 0, t <= nlat))
    def _backward_states():
        idx = jnp.where(t == 0, 0, nlat + 1 - t)
        sb_ref[idx] = s_ref[...].astype(mdt)
        state_update(1, 2)

    @pl.when(jnp.logical_and(ph == 1, t == 1))
    def _pad_rows():
        o_ref[...] = jnp.zeros_like(o_ref)

    @pl.when(jnp.logical_and(ph == 1, t != 1))
    def _outputs():
        idx = jnp.where(t == 0, 0, t - 1)
        q = q_ref[...]
        k = k_ref[...]
        v = v_ref[...]
        head = lax.broadcasted_iota(jnp.int32, (c, w), 1) // HEAD_DIM
        qs = jnp.concatenate([jnp.where(head == h, q, jnp.zeros_like(q)) for h in range(RET_HEADS)], axis=0)
        sc = _mm_nt(qs, k)
        scd = jnp.concatenate([sc[c * h:c * (h + 1)] * dm_ref[h] for h in range(RET_HEADS)], axis=0).astype(mdt)
        oi = _mm(scd, v)
        o = jnp.zeros((c, w), F32)
        for h in range(RET_HEADS):
            o = o + jnp.where(head == h, oi[c * h:c * (h + 1)], 0.0)
        qf = q.astype(F32)
        o = o + _mm((qf * tab_ref[1]).astype(mdt), s_ref[...].astype(mdt))
        o = o + _mm((qf * tab_ref[3]).astype(mdt), sb_ref[idx])
        state_update(0, 0)
        avg = (gbd_ref[2] * (1.0 / HEAD_DIM)).astype(mdt)

        def head_mean(val):
            hi = val.astype(mdt)
            lo = (val - hi.astype(F32)).astype(mdt)
            return _mm(hi, avg) + _mm(lo, avg)

        d = o - head_mean(o)
        var = head_mean(d * d)
        gate = g_ref[...].astype(F32)
        y = d * lax.rsqrt(var + GN_EPS) * (gate * jax.nn.sigmoid(gate))
        o_ref[...] = y.astype(o_ref.dtype)


def _retention(proj, log_gamma):
    rows = proj.shape[0]
    c = RET_CHUNK
    nlat = (rows - CTX_PAD) // c
    dt = proj.dtype
    cq, ckk, cvv, cg = (COL_RQ // RET_WIDTH, COL_RK // RET_WIDTH, COL_RV // RET_WIDTH, COL_RG // RET_WIDTH)

    def kv_blk(ph, t):
        back = jnp.where(t == 0, 0, jnp.maximum(nlat + 2 - t, 2))
        return jnp.where(ph == 0, back, t)

    def fw_blk(ph, t):
        return jnp.where(ph == 0, 0, t)

    return pl.pallas_call(
        _ret_kernel,
        grid=(2, nlat + 2),
        in_specs=[
            pl.BlockSpec(memory_space=pltpu.SMEM),
            pl.BlockSpec((c, RET_WIDTH), lambda ph, t: (fw_blk(ph, t), cq)),
            pl.BlockSpec((c, RET_WIDTH), lambda ph, t: (kv_blk(ph, t), ckk)),
            pl.BlockSpec((c, RET_WIDTH), lambda ph, t: (kv_blk(ph, t), cvv)),
            pl.BlockSpec((c, RET_WIDTH), lambda ph, t: (fw_blk(ph, t), cg)),
        ],
        out_specs=pl.BlockSpec((c, RET_WIDTH), lambda ph, t: (fw_blk(ph, t), 0)),
        out_shape=jax.ShapeDtypeStruct((rows, RET_WIDTH), dt),
        scratch_shapes=[
            pltpu.VMEM((nlat + 1, RET_WIDTH, RET_WIDTH), dt),
            pltpu.VMEM((RET_WIDTH, RET_WIDTH), F32),
            pltpu.VMEM((RET_HEADS, c, c), F32),
            pltpu.VMEM((4, c, RET_WIDTH), F32),
            pltpu.VMEM((3, RET_WIDTH, RET_WIDTH), F32),
        ],
        compiler_params=_params(("arbitrary", "arbitrary")),
        name="retention",
    )(log_gamma, proj, proj, proj, proj)


def _s5_weights(lam_re, lam_im, b_re, b_im, c_re, c_im, log_dt, d_skip):
    hp = lax.Precision.HIGHEST
    tt, g, n, p, a = S5_T, S5_GROUPS, S5_STATE, S5_CH, S5_PAIRS
    lam = lax.complex(lam_re.astype(F32), lam_im.astype(F32))
    dtv = jnp.exp(log_dt.astype(F32))[..., None]
    lam_bar = jnp.exp(lam * dtv)
    bbar = ((lam_bar - 1.0) / lam)[..., None] * lax.complex(b_re.astype(F32), b_im.astype(F32))
    cmat = lax.complex(c_re.astype(F32), c_im.astype(F32))
    pw = [jnp.ones_like(lam_bar)]
    for _ in range(tt):
        pw.append(pw[-1] * lam_bar)
    pw = jnp.stack(pw, axis=1)
    eye2 = jnp.eye(2, dtype=F32)
    ri = lambda z, axis: jnp.stack([jnp.real(z), jnp.imag(z)], axis=axis)

    pw_l = pw.reshape(2, tt + 1, a, 2 * n)
    bbt = jnp.einsum('dahpn,gh->dagphn', jnp.swapaxes(bbar, -1, -2).reshape(2, a, 2, p, n), eye2)
    bbt = bbt.reshape(2, a, 2 * p, 2 * n)
    cct = jnp.einsum('dahpn,gh->dagphn', cmat.reshape(2, a, 2, p, n), eye2).reshape(2, a, 2 * p, 2 * n)
    x = bbt[:, None] * pw_l[:, :tt, :, None, :]
    k = jnp.real(jnp.einsum('dlaxk,dayk->dlaxy', x, cct, precision=hp))
    lags = jnp.arange(-(tt - 1), tt)
    kl = (jnp.where((lags >= 0)[:, None, None, None], k[0][jnp.clip(lags, 0, tt - 1)], 0.0)
          + jnp.where((lags <= 0)[:, None, None, None], k[1][jnp.clip(-lags, 0, tt - 1)], 0.0))
    kall = kl.transpose(1, 2, 0, 3).reshape(a, 2 * p, (2 * tt - 1) * 2 * p)
    kall = jnp.pad(kall, ((0, 0), (0, 0), (0, 2 * p)))

    pw_k = ri(pw_l, 1).transpose(3, 0, 1, 2, 4)
    b_k = ri(bbt, 2).transpose(1, 0, 2, 3, 4)
    c_k = ri(cct, 2).transpose(1, 0, 2, 3, 4)
    lam_t = pw_l[:, tt]
    lam_t = jnp.concatenate([jnp.real(lam_t[0]), jnp.imag(lam_t[0]), jnp.real(lam_t[1]), jnp.imag(lam_t[1])],
                            axis=-1).reshape(a, 1, 4 * 2 * n)
    skip = jnp.tile(d_skip.astype(F32).reshape(a, 1, 2 * p), (1, tt, 1)).reshape(a, 1, tt * 2 * p)
    return pw_k, b_k, c_k, kall, lam_t, skip


def _pair_spec(layer, *shape):
    return pl.BlockSpec((None, None) + shape, lambda i: (layer, i) + (0,) * len(shape))


def _s5_drive_kernel(u_ref, pw_ref, b_ref, o_ref, w_ref):
    rows, half = S5_PAIR_W, LANES
    for d in range(2):
        br, bi = b_ref[d, 0], b_ref[d, 1]
        for j in range(S5_T):
            e = S5_T - 1 - j if d == 0 else j
            pr, pi = pw_ref[d, 0, e:e + 1, :], pw_ref[d, 1, e:e + 1, :]
            w_ref[rows * j:rows * (j + 1), 2 * half * d:2 * half * d + half] = (pr * br - pi * bi).astype(w_ref.dtype)
            w_ref[rows * j:rows * (j + 1), 2 * half * d + half:2 * half * (d + 1)] = (
                pr * bi + pi * br).astype(w_ref.dtype)
    o_ref[0] = _mm(u_ref[0], w_ref[...])


def _s5_drive(u_pairs, pw_k, b_k, layer):
    a, nch, wd = u_pairs.shape
    return pl.pallas_call(
        _s5_drive_kernel,
        grid=(a,),
        in_specs=[pl.BlockSpec((1, nch, wd), lambda i: (i, 0, 0)),
                  _pair_spec(layer, *pw_k.shape[2:]), _pair_spec(layer, *b_k.shape[2:])],
        out_specs=pl.BlockSpec((1, nch, 2 * wd), lambda i: (i, 0, 0)),
        out_shape=jax.ShapeDtypeStruct((a, nch, 2 * wd), F32),
        scratch_shapes=[pltpu.VMEM((wd, 2 * wd), u_pairs.dtype)],
        compiler_params=_params(("arbitrary",)),
        name="s5_drive",
    )(u_pairs, pw_k, b_k)


def _s5_scan_kernel(ef_ref, eb_ref, lam_ref, sf_ref, sb_ref, stf_ref, stb_ref):
    t = pl.program_id(0)
    hw = LANES
    sub = 8
    npair = lam_ref.shape[0]

    @pl.when(t == 0)
    def _reset():
        stf_ref[...] = jnp.zeros_like(stf_ref)
        stb_ref[...] = jnp.zeros_like(stb_ref)

    def step(lr, li, sr, si, e):
        return lr * sr - li * si + e[:, 0:hw], lr * si + li * sr + e[:, hw:2 * hw]

    def run(nrows):
        ngroups = nrows // sub

        def body(gi, carry):
            r0 = pl.multiple_of(gi * sub, sub)
            rb0 = pl.multiple_of((ngroups - 1 - gi) * sub, sub)
            new = []
            for a in range(npair):
                lam = lam_ref[a]
                lfr, lfi, lbr, lbi = (lam[:, k * hw:(k + 1) * hw] for k in range(4))
                fr, fi, br, bi = carry[a]
                ef = ef_ref[a, pl.ds(r0, sub), :]
                eb = eb_ref[a, pl.ds(rb0, sub), :]
                before_f = []
                for k in range(sub):
                    before_f.append(jnp.concatenate([fr, fi], axis=1))
                    fr, fi = step(lfr, lfi, fr, fi, ef[k:k + 1])
                sf_ref[a, pl.ds(r0, sub), :] = jnp.concatenate(before_f, axis=0)
                before_b = [None] * sub
                for k in range(sub - 1, -1, -1):
                    before_b[k] = jnp.concatenate([br, bi], axis=1)
                    br, bi = step(lbr, lbi, br, bi, eb[k:k + 1])
                sb_ref[a, pl.ds(rb0, sub), :] = jnp.concatenate(before_b, axis=0)
                new.append((fr, fi, br, bi))
            return tuple(new)

        init = tuple((stf_ref[a, :, 0:hw], stf_ref[a, :, hw:2 * hw], stb_ref[a, :, 0:hw], stb_ref[a, :, hw:2 * hw])
                     for a in range(npair))
        final = lax.fori_loop(0, ngroups, body, init)
        for a in range(npair):
            fr, fi, br, bi = final[a]
            stf_ref[a, :, 0:hw] = fr
            stf_ref[a, :, hw:2 * hw] = fi
            stb_ref[a, :, 0:hw] = br
            stb_ref[a, :, hw:2 * hw] = bi

    @pl.when(t == 0)
    def _context():
        sf_ref[...] = jnp.zeros_like(sf_ref)
        sb_ref[...] = jnp.zeros_like(sb_ref)
        run(CTX_LEN // S5_T)

    @pl.when(t > 0)
    def _latent():
        run(S5_TILE)


def _s5_scan(drive, lam_t, layer):
    a, nch, wd2 = drive.shape
    wd = wd2 // 2
    nt = nch // S5_TILE

    def bwd(t):
        return jnp.where(t == 0, 0, nt - t)

    return pl.pallas_call(
        _s5_scan_kernel,
        grid=(nt,),
        in_specs=[pl.BlockSpec((a, S5_TILE, wd), lambda t: (0, t, 0)),
                  pl.BlockSpec((a, S5_TILE, wd), lambda t: (0, bwd(t), 1)),
                  pl.BlockSpec((None, a, 1, wd2), lambda t: (layer, 0, 0, 0))],
        out_specs=[pl.BlockSpec((a, S5_TILE, wd), lambda t: (0, t, 0)),
                   pl.BlockSpec((a, S5_TILE, wd), lambda t: (0, bwd(t), 0))],
        out_shape=[jax.ShapeDtypeStruct((a, nch, wd), F32), jax.ShapeDtypeStruct((a, nch, wd), F32)],
        scratch_shapes=[pltpu.VMEM((a, 1, wd), F32), pltpu.VMEM((a, 1, wd), F32)],
        compiler_params=_params(("arbitrary",)),
        name="s5_scan",
    )(drive, drive, lam_t)


def _lane_window(x, start, width):
    cols = []
    for v in range(width // LANES):
        k0, off = divmod(start + LANES * v, LANES)
        lo = x[:, LANES * k0:LANES * (k0 + 1)]
        if off:
            hi = x[:, LANES * (k0 + 1):LANES * (k0 + 2)]
            lane = lax.broadcasted_iota(jnp.int32, lo.shape, 1)
            lo = jnp.where(lane < LANES - off, pltpu.roll(lo, LANES - off, 1), pltpu.roll(hi, LANES - off, 1))
        cols.append(lo)
    return jnp.concatenate(cols, axis=1)


def _s5_read_kernel(u_ref, sf_ref, sb_ref, pw_ref, c_ref, kall_ref, skip_ref, o_ref, wt_ref, wi_ref):
    u = u_ref[0]
    mdt = u.dtype
    rows, half = S5_PAIR_W, LANES
    for d in range(2):
        cr, ci = c_ref[d, 0], c_ref[d, 1]
        for i in range(S5_T):
            e = i + 1 if d == 0 else S5_T - i
            pr, pi = pw_ref[d, 0, e:e + 1, :], pw_ref[d, 1, e:e + 1, :]
            wt_ref[d, rows * i:rows * (i + 1), 0:half] = (pr * cr - pi * ci).astype(mdt)
            wt_ref[d, rows * i:rows * (i + 1), half:2 * half] = (-(pr * ci + pi * cr)).astype(mdt)
    kall = kall_ref[...]
    for j in range(S5_T):
        wi_ref[rows * j:rows * (j + 1), :] = _lane_window(kall, rows * (S5_T - 1 - j), S5_T * rows).astype(mdt)
    y = _mm(u, wi_ref[...])
    y = y + _mm_nt(sf_ref[0].astype(mdt), wt_ref[0])
    y = y + _mm_nt(sb_ref[0].astype(mdt), wt_ref[1])
    o_ref[0] = y + u.astype(F32) * skip_ref[...]


def _s5_read(u_pairs, sf, sb, pw_k, c_k, kall, skip, layer):
    a, nch, wd = u_pairs.shape
    blk = lambda *shape: pl.BlockSpec((1,) + shape, lambda i: (i, 0, 0))
    return pl.pallas_call(
        _s5_read_kernel,
        grid=(a,),
        in_specs=[blk(nch, wd), blk(nch, wd), blk(nch, wd), _pair_spec(layer, *pw_k.shape[2:]),
                  _pair_spec(layer, *c_k.shape[2:]), _pair_spec(layer, *kall.shape[2:]),
                  _pair_spec(layer, *skip.shape[2:])],
        out_specs=blk(nch, wd),
        out_shape=jax.ShapeDtypeStruct((a, nch, wd), F32),
        scratch_shapes=[pltpu.VMEM((2, wd, wd), u_pairs.dtype), pltpu.VMEM((wd, wd), u_pairs.dtype)],
        compiler_params=_params(("arbitrary",)),
        name="s5_read",
    )(u_pairs, sf, sb, pw_k, c_k, kall, skip)


def _s5_mixer(u_pairs, weights, layer):
    pw_k, b_k, c_k, kall, lam_t, skip = weights
    drive = _s5_drive(u_pairs, pw_k, b_k, layer)
    sf, sb = _s5_scan(drive, lam_t, layer)
    return _s5_read(u_pairs, sf, sb, pw_k, c_k, kall, skip, layer)


def _layer_norm(x, g, b):
    mu = jnp.mean(x, axis=-1, keepdims=True)
    d = x - mu
    var = jnp.mean(d * d, axis=-1, keepdims=True)
    return d * lax.rsqrt(var + LN_EPS) * g + b


def _post_kernel(x_ref, att_ref, ret_ref, s5_ref, mod_ref, permt_ref, wglu_ref, bglu_ref, woa_ref, wor_ref, wos_ref,
                 g1_ref, b1_ref, w1_ref, w2_ref, g2_ref, b2_ref, o_ref):
    mdt = w1_ref.dtype
    x = x_ref[...]
    zrows = []
    for i in range(S5_T):
        src_vreg, src_blk = divmod(i * S5_PAIR_W, LANES)
        src_blk //= S5_PAIR_W
        cols = [_lane_block_shuffle(lambda a: s5_ref[a, :, src_vreg * LANES:(src_vreg + 1) * LANES], src_blk, w)
                for w in range(S5_WIDTH // LANES)]
        zrows.append(jnp.concatenate(cols, axis=1))
    z = jnp.concatenate(zrows, axis=0)
    z_hi = z.astype(mdt)
    z_lo = (z - z_hi.astype(F32)).astype(mdt)
    ys = _mm(permt_ref[...], z_hi) + _mm(permt_ref[...], z_lo)
    hs = jax.nn.gelu(ys)
    gate = jax.nn.sigmoid(_mm(hs.astype(mdt), wglu_ref[...]) + bglu_ref[...])
    s5 = (hs * gate).astype(mdt)
    ox = _mm(att_ref[...], woa_ref[...]) + _mm(ret_ref[...], wor_ref[...]) + _mm(s5, wos_ref[...])
    x1 = _layer_norm(DEEPNORM_ALPHA * x + mod_ref[2:3, :] * ox, g1_ref[...], b1_ref[...])
    h = (x1 * (1.0 + mod_ref[4:5, :]) + mod_ref[3:4, :]).astype(mdt)
    acc = jnp.zeros(x.shape, F32)
    for c in range(D_FF // FF_CHUNK):
        a = _mm(h, w1_ref[:, FF_CHUNK * c:FF_CHUNK * (c + 1)])
        a = jnp.square(jnp.maximum(a, 0.0)).astype(mdt)
        acc = acc + _mm(a, w2_ref[FF_CHUNK * c:FF_CHUNK * (c + 1), :])
    o_ref[...] = _layer_norm(DEEPNORM_ALPHA * x1 + mod_ref[5:6, :] * acc, g2_ref[...], b2_ref[...])


def _post(stream, att, ret, s5_pairs, mods, layer, permt, wglu, bglu, wo, g1, b1, w1, w2, g2, b2, skip_context):
    rows = stream.shape[0]
    tm = ROW_TILE
    off = CTX_PAD // tm if skip_context else 0
    row_blk = lambda width: pl.BlockSpec((tm, width), lambda i: (i + off, 0))
    full = lambda arr: pl.BlockSpec(arr.shape, lambda i: (0,) * arr.ndim)
    woa, wor, wos = wo[:ATT_WIDTH], wo[ATT_WIDTH:ATT_WIDTH + RET_WIDTH], wo[ATT_WIDTH + RET_WIDTH:]
    vec = lambda v: v.reshape(1, -1).astype(F32)
    small = [permt, wglu, vec(bglu), woa, wor, wos, vec(g1), vec(b1), w1, w2, vec(g2), vec(b2)]
    return pl.pallas_call(
        _post_kernel,
        grid=(rows // tm - off,),
        in_specs=[row_blk(D_MODEL), row_blk(ATT_WIDTH), row_blk(RET_WIDTH),
                  pl.BlockSpec((S5_PAIRS, tm // S5_T, S5_T * S5_PAIR_W), lambda i: (0, i + off, 0)),
                  pl.BlockSpec((None, None, N_ADA, D_MODEL),
                               lambda i: (layer, jnp.where(i + off == 0, 1, 0), 0, 0))]
                 + [full(arr) for arr in small],
        out_specs=pl.BlockSpec((tm, D_MODEL), lambda i: (i, 0)),
        out_shape=jax.ShapeDtypeStruct((rows - off * tm, D_MODEL), F32),
        compiler_params=_params(("arbitrary",)),
        name="post",
    )(stream, att, ret, s5_pairs, mods, *small)


def kernel(x, c, ctx, c_ctx, w_ada, b_ada, w_in, att_sink, ret_decay_logit, s5_lambda_re, s5_lambda_im, s5_b_re,
           s5_b_im, s5_c_re, s5_c_im, s5_log_dt, s5_d, w_glu, b_glu, w_out, ln1_g, ln1_b, w_ff1, w_ff2, ln2_g,
           ln2_b):
    assert x.shape[0] == 1 and x.shape[2] == D_MODEL and ctx.shape[1] == CTX_LEN
    seq = x.shape[1]
    assert seq % ROW_TILE == 0
    stream = jnp.concatenate([ctx[0], jnp.zeros((CTX_PAD - CTX_LEN, D_MODEL), F32), x[0]], axis=0)
    cond = jnp.zeros((8, D_MODEL), F32).at[0].set(c[0]).at[1].set(c_ctx)
    mods = _modulation(cond, w_ada, b_ada).reshape(DEPTH, 8, N_ADA, D_MODEL)
    tabs = _rope_tables(seq)
    masks = _attention_masks()
    perm = _chunk_perm(ROW_TILE, MXU_DTYPE)
    permt = perm.T
    col_scale = jnp.ones((IN_WIDTH,), F32).at[COL_AQ:COL_AK].set(HEAD_DIM ** -0.5)
    col_scale = col_scale.at[COL_RQ:COL_RK].set(HEAD_DIM ** -0.5)
    s5w = jax.vmap(_s5_weights)(s5_lambda_re, s5_lambda_im, s5_b_re, s5_b_im, s5_c_re, s5_c_im, s5_log_dt, s5_d)
    log_gamma = jax.nn.log_sigmoid(ret_decay_logit.astype(F32))
    for l in range(DEPTH):
        proj, u_pairs = _in_proj(stream, mods, l, (w_in[l] * col_scale).astype(MXU_DTYPE), tabs, perm)
        att = _attention(proj, att_sink[l].astype(F32), masks)
        ret = _retention(proj, log_gamma[l])
        s5 = _s5_mixer(u_pairs, s5w, l)
        stream = _post(stream, att, ret, s5, mods, l, permt, w_glu[l].astype(MXU_DTYPE), b_glu[l],
                       w_out[l].astype(MXU_DTYPE), ln1_g[l], ln1_b[l], w_ff1[l].astype(MXU_DTYPE),
                       w_ff2[l].astype(MXU_DTYPE), ln2_g[l], ln2_b[l], skip_context=(l == DEPTH - 1))
    return stream[None]
```

```python
import functools
import math

import jax
import jax.numpy as jnp
from jax import lax
from jax.experimental import pallas as pl
from jax.experimental.pallas import tpu as pltpu

F32 = jnp.float32
MXU_DTYPE = jnp.bfloat16

D_MODEL = 1024
DEPTH = 4
GRID_W = 64
CTX_LEN = 256
CTX_PAD = 512
HEAD_DIM = 64
ATT_HEADS = 8
ATT_KV_HEADS = 2
ATT_BLOCK = 128
ROPE_BASE = 10000.0
RET_HEADS = 4
RET_CHUNK = 256
S5_CH = 16
S5_GROUPS = 16
S5_STATE = 64
S5_T = 8
S5_PAIRS = S5_GROUPS // 2
S5_PAIR_W = 2 * S5_CH
S5_TILE = 64
ATT_WIDTH = ATT_HEADS * HEAD_DIM
KV_WIDTH = ATT_KV_HEADS * HEAD_DIM
RET_WIDTH = RET_HEADS * HEAD_DIM
S5_WIDTH = S5_GROUPS * S5_CH
IN_WIDTH = ATT_WIDTH + 2 * KV_WIDTH + 4 * RET_WIDTH + S5_WIDTH
D_FF = 4 * D_MODEL
FF_CHUNK = 1024
N_ADA = 6
LN_EPS = 1e-5
GN_EPS = 1e-5
DEEPNORM_ALPHA = (2 * DEPTH) ** 0.25
ROW_TILE = 512
POST_SPLIT = 2
POST_PROGRAM = ((0, 0), (0, 1), (1, 0), (0, 2), (1, 1), (0, 3), (0, 4), (0, 5), (1, 2), (0, 6), (1, 3), (1, 4),
                (1, 5), (1, 6))
NEG_BIG = -1e30
LANES = 128
VMEM_LIMIT = 56 * 1024 * 1024

COL_AQ, COL_AK, COL_AV = 0, ATT_WIDTH, ATT_WIDTH + KV_WIDTH
COL_RQ = ATT_WIDTH + 2 * KV_WIDTH
COL_RK, COL_RV, COL_RG = COL_RQ + RET_WIDTH, COL_RQ + 2 * RET_WIDTH, COL_RQ + 3 * RET_WIDTH
COL_S5 = COL_RQ + 4 * RET_WIDTH


def _mm(a, b):
    return jnp.dot(a, b, preferred_element_type=F32)


def _mm_nt(a, b):
    return lax.dot_general(a, b, (((1,), (1,)), ((), ())), preferred_element_type=F32)


def _mm_tn(a, b):
    return lax.dot_general(a, b, (((0,), (0,)), ((), ())), preferred_element_type=F32)


def _params(sem):
    return pltpu.CompilerParams(dimension_semantics=sem, vmem_limit_bytes=VMEM_LIMIT)


def _mod_kernel(cond_ref, w_ref, b_ref, o_ref):
    c = cond_ref[...]
    s = c * jax.nn.sigmoid(c)
    o_ref[0] = jnp.dot(s, w_ref[0], preferred_element_type=F32, precision=lax.Precision.HIGHEST) + b_ref[0]


def _modulation(cond, w_ada, b_ada):
    tn = 1536
    n = N_ADA * D_MODEL
    return pl.pallas_call(
        _mod_kernel,
        grid=(DEPTH, n // tn),
        in_specs=[
            pl.BlockSpec((8, D_MODEL), lambda l, j: (0, 0)),
            pl.BlockSpec((1, D_MODEL, tn), lambda l, j: (l, 0, j)),
            pl.BlockSpec((1, 1, tn), lambda l, j: (l, 0, j)),
        ],
        out_specs=pl.BlockSpec((1, 8, tn), lambda l, j: (l, 0, j)),
        out_shape=jax.ShapeDtypeStruct((DEPTH, 8, n), F32),
        compiler_params=_params(("arbitrary", "arbitrary")),
        name="modulation",
    )(cond, w_ada, b_ada.reshape(DEPTH, 1, n))


def _lane_block_shuffle(src_rows, src_lane_blk, out_vreg):
    acc = None
    for q in range(LANES // S5_PAIR_W):
        piece = src_rows(out_vreg * (LANES // S5_PAIR_W) + q)
        shift = (S5_PAIR_W * (q - src_lane_blk)) % LANES
        if shift:
            piece = pltpu.roll(piece, shift, 1)
        if acc is None:
            acc = piece
        else:
            lane_blk = lax.broadcasted_iota(jnp.int32, piece.shape, 1) // S5_PAIR_W
            acc = jnp.where(lane_blk == q, piece, acc)
    return acc


def _in_proj_kernel(x_ref, mod_ref, w_ref, ca_ref, sa_ref, cr_ref, sr_ref, perm_ref, o_ref, u_ref):
    x = x_ref[...]
    h = (x * (1.0 + mod_ref[1:2, :]) + mod_ref[0:1, :]).astype(w_ref.dtype)
    lane = lax.broadcasted_iota(jnp.int32, (x.shape[0], LANES), 1)
    first_att = (lane & 31) < 16
    first_ret = (lane & 63) < 32

    def proj(c0, c1):
        return _mm(h, w_ref[:, c0:c1])

    def rope_store(c0, width, cos, sin, first, half):
        p = proj(c0, c0 + width)
        for b in range(width // LANES):
            blk = p[:, LANES * b:LANES * (b + 1)]
            rot = jnp.where(first, pltpu.roll(blk, LANES - half, 1), pltpu.roll(blk, half, 1))
            o_ref[:, c0 + LANES * b:c0 + LANES * (b + 1)] = (blk * cos + rot * sin).astype(o_ref.dtype)

    u = proj(COL_S5, IN_WIDTH).astype(w_ref.dtype)
    sub = perm_ref.shape[0]
    nchunk = sub // S5_T
    for part in range(x.shape[0] // sub):
        g = _mm(perm_ref[...], u[sub * part:sub * (part + 1)])
        for a in range(S5_PAIRS):
            vreg_col, lane_blk = divmod(a * S5_PAIR_W, LANES)
            lane_blk //= S5_PAIR_W
            for v in range(S5_T * S5_PAIR_W // LANES):
                slab = _lane_block_shuffle(
                    lambda j: g[nchunk * j:nchunk * (j + 1), vreg_col * LANES:(vreg_col + 1) * LANES], lane_blk, v)
                u_ref[a, nchunk * part:nchunk * (part + 1), LANES * v:LANES * (v + 1)] = slab.astype(u_ref.dtype)

    ca, sa, cr, sr = ca_ref[...], sa_ref[...], cr_ref[...], sr_ref[...]
    rope_store(COL_AQ, ATT_WIDTH, ca, sa, first_att, 16)
    rope_store(COL_AK, KV_WIDTH, ca, sa, first_att, 16)
    rope_store(COL_RQ, RET_WIDTH, cr, sr, first_ret, 32)
    rope_store(COL_RK, RET_WIDTH, cr, sr, first_ret, 32)
    o_ref[:, COL_AV:COL_RQ] = proj(COL_AV, COL_RQ).astype(o_ref.dtype)
    o_ref[:, COL_RV:COL_S5] = proj(COL_RV, COL_S5).astype(o_ref.dtype)


def _chunk_perm(tile_rows, dtype):
    nchunk = tile_rows // S5_T
    r = jnp.arange(tile_rows)
    src = S5_T * (r % nchunk) + r // nchunk
    return (src[:, None] == jnp.arange(tile_rows)[None, :]).astype(dtype)


def _in_proj(stream, mods, layer, w_in, tabs, perm):
    rows = stream.shape[0]
    tm = ROW_TILE
    tab_spec = pl.BlockSpec((tm, LANES), lambda i: (i, 0))
    nch = rows // S5_T
    return pl.pallas_call(
        _in_proj_kernel,
        grid=(rows // tm,),
        in_specs=[
            pl.BlockSpec((tm, D_MODEL), lambda i: (i, 0)),
            pl.BlockSpec((None, None, N_ADA, D_MODEL), lambda i: (layer, jnp.where(i == 0, 1, 0), 0, 0)),
            pl.BlockSpec((D_MODEL, IN_WIDTH), lambda i: (0, 0)),
            tab_spec, tab_spec, tab_spec, tab_spec,
            pl.BlockSpec(perm.shape, lambda i: (0, 0)),
        ],
        out_specs=[pl.BlockSpec((tm, COL_S5), lambda i: (i, 0)),
                   pl.BlockSpec((S5_PAIRS, tm // S5_T, S5_T * S5_PAIR_W), lambda i: (0, i, 0))],
        out_shape=[jax.ShapeDtypeStruct((rows, COL_S5), MXU_DTYPE),
                   jax.ShapeDtypeStruct((S5_PAIRS, nch, S5_T * S5_PAIR_W), MXU_DTYPE)],
        compiler_params=_params(("arbitrary",)),
        name="in_proj",
    )(stream, mods, w_in, *tabs, perm)


def _rope_tables(seq):
    half_a = HEAD_DIM // 4
    half_r = HEAD_DIM // 2
    nrow = seq // GRID_W
    inv_a = ROPE_BASE ** (-jnp.arange(half_a, dtype=F32) / half_a)
    inv_r = ROPE_BASE ** (-jnp.arange(half_r, dtype=F32) / half_r)
    ang_r = jnp.arange(nrow, dtype=F32)[:, None] * inv_a[None, :]
    ang_c = jnp.arange(GRID_W, dtype=F32)[:, None] * inv_a[None, :]
    ang_t = jnp.arange(seq, dtype=F32)[:, None] * inv_r[None, :]
    hp = lax.Precision.HIGHEST
    lane = jnp.arange(LANES)
    within = lane % HEAD_DIM
    pick_a = (within % half_a)[None, :] == jnp.arange(half_a)[:, None]
    exp_row = (pick_a & (within < 2 * half_a)[None, :]).astype(F32)
    exp_col = (pick_a & (within >= 2 * half_a)[None, :]).astype(F32)
    exp_t = ((lane % half_r)[None, :] == jnp.arange(half_r)[:, None]).astype(F32)
    sign_a = jnp.where(within % (2 * half_a) < half_a, -1.0, 1.0).astype(F32)
    sign_r = jnp.where(within < half_r, -1.0, 1.0).astype(F32)

    def att_table(fn):
        by_row = jnp.dot(fn(ang_r), exp_row, precision=hp)
        by_col = jnp.dot(fn(ang_c), exp_col, precision=hp)
        return (by_row[:, None, :] + by_col[None, :, :]).reshape(seq, LANES)

    cos_a = att_table(jnp.cos)
    sin_a = att_table(jnp.sin) * sign_a
    cos_r = jnp.dot(jnp.cos(ang_t), exp_t, precision=hp)
    sin_r = jnp.dot(jnp.sin(ang_t), exp_t, precision=hp) * sign_r
    pad = lambda tab, ident: jnp.pad(tab, ((CTX_PAD, 0), (0, 0)), constant_values=ident)
    return pad(cos_a, 1.0), pad(sin_a, 0.0), pad(cos_r, 1.0), pad(sin_r, 0.0)


def _swap_halves(x):
    if x.dtype.itemsize == 4:
        return pltpu.roll(x, 64, 1)
    packed = pltpu.bitcast(x, jnp.uint32)
    return pltpu.bitcast(pltpu.roll(packed, 64, 1), x.dtype)


def _dup_heads(x):
    sw = _swap_halves(x)
    lo = lax.broadcasted_iota(jnp.int32, x.shape, 1) < HEAD_DIM
    return jnp.where(lo, x, sw), jnp.where(lo, sw, x)


def _attn_kernel(sink_ref, q_ref, km_ref, kp_ref, kn_ref, vm_ref, vp_ref, vn_ref, kc_ref, vc_ref, mask_ref,
                 o_ref, k2_ref, v2_ref, kc2_ref, vc2_ref):
    i = pl.program_id(0)
    last_blk = pl.num_programs(0) * (ROW_TILE // ATT_BLOCK) - 1
    blk = ATT_BLOCK
    for dst, parts in ((k2_ref, (kp_ref, km_ref, kn_ref)), (v2_ref, (vp_ref, vm_ref, vn_ref))):
        row = 0
        for part in parts:
            a, b = _dup_heads(part[...])
            n = part.shape[0]
            dst[0, row:row + n, :] = a
            dst[1, row:row + n, :] = b
            row += n
    for dst, src in ((kc2_ref, kc_ref), (vc2_ref, vc_ref)):
        a, b = _dup_heads(src[...])
        dst[0] = a
        dst[1] = b

    lo = lax.broadcasted_iota(jnp.int32, (blk, LANES), 1) < HEAD_DIM
    group = ATT_HEADS // ATT_KV_HEADS

    def body(j, carry):
        r0 = pl.multiple_of(j * blk, blk)
        gblk = i * (ROW_TILE // blk) + j
        sel = jnp.where(i == 0, 3, jnp.where(gblk == CTX_PAD // blk, 0, jnp.where(gblk == last_blk, 2, 1)))
        bias = mask_ref[sel]
        for kv in range(ATT_KV_HEADS):
            qt = q_ref[pl.ds(r0, blk), group * HEAD_DIM * kv:group * HEAD_DIM * (kv + 1)]
            parts = []
            for g in range(group):
                qc = qt[:, LANES * (g // 2):LANES * (g // 2 + 1)]
                keep = lo if g % 2 == 0 else jnp.logical_not(lo)
                parts.append(jnp.where(keep, qc, jnp.zeros_like(qc)))
            qs = jnp.concatenate(parts, axis=0)
            kl = k2_ref[kv, pl.ds(r0, 3 * blk), :]
            vl = v2_ref[kv, pl.ds(r0, 3 * blk), :]
            s_loc = _mm_nt(qs, kl)
            s_ctx = _mm_nt(qs, kc2_ref[kv])
            p_loc, p_ctx, inv = [], [], []
            for g in range(group):
                sl = s_loc[blk * g:blk * (g + 1)] + bias
                sc = s_ctx[blk * g:blk * (g + 1)]
                sk = sink_ref[group * kv + g]
                m = jnp.maximum(jnp.maximum(jnp.max(sl, axis=-1, keepdims=True),
                                            jnp.max(sc, axis=-1, keepdims=True)), sk)
                el = jnp.exp(sl - m)
                ec = jnp.exp(sc - m)
                den = jnp.sum(el, axis=-1, keepdims=True) + jnp.sum(ec, axis=-1, keepdims=True) + jnp.exp(sk - m)
                p_loc.append(el.astype(vl.dtype))
                p_ctx.append(ec.astype(vl.dtype))
                inv.append(1.0 / den)
            o = _mm(jnp.concatenate(p_loc, axis=0), vl) + _mm(jnp.concatenate(p_ctx, axis=0), vc2_ref[kv])
            outs = [o[blk * g:blk * (g + 1)] * inv[g] for g in range(group)]
            for half in range(group // 2):
                y = jnp.where(lo, outs[2 * half], outs[2 * half + 1])
                c0 = group * HEAD_DIM * kv + LANES * half
                o_ref[pl.ds(r0, blk), c0:c0 + LANES] = y.astype(o_ref.dtype)
        return carry

    lax.fori_loop(0, ROW_TILE // blk, body, 0)


def _attention_masks():
    qi = jnp.arange(ATT_BLOCK)[:, None]
    kj = jnp.arange(3 * ATT_BLOCK)[None, :]
    band = jnp.abs(kj - ATT_BLOCK - qi) <= ATT_BLOCK
    first = band & (kj >= ATT_BLOCK)
    last = band & (kj < 2 * ATT_BLOCK)
    none = jnp.zeros_like(band)
    masks = jnp.stack([first, band, last, none])
    return jnp.where(masks, 0.0, NEG_BIG).astype(F32)


def _attention(proj, sink, masks):
    rows = proj.shape[0]
    tm, blk = ROW_TILE, ATT_BLOCK
    per = tm // blk
    nblk = rows // blk
    ck, cv = COL_AK // KV_WIDTH, COL_AV // KV_WIDTH
    dt = proj.dtype
    return pl.pallas_call(
        _attn_kernel,
        grid=(rows // tm,),
        in_specs=[
            pl.BlockSpec(memory_space=pltpu.SMEM),
            pl.BlockSpec((tm, ATT_WIDTH), lambda i: (i, 0)),
            pl.BlockSpec((tm, KV_WIDTH), lambda i: (i, ck)),
            pl.BlockSpec((blk, KV_WIDTH), lambda i: (jnp.maximum(i * per - 1, 0), ck)),
            pl.BlockSpec((blk, KV_WIDTH), lambda i: (jnp.minimum((i + 1) * per, nblk - 1), ck)),
            pl.BlockSpec((tm, KV_WIDTH), lambda i: (i, cv)),
            pl.BlockSpec((blk, KV_WIDTH), lambda i: (jnp.maximum(i * per - 1, 0), cv)),
            pl.BlockSpec((blk, KV_WIDTH), lambda i: (jnp.minimum((i + 1) * per, nblk - 1), cv)),
            pl.BlockSpec((CTX_LEN, KV_WIDTH), lambda i: (0, ck)),
            pl.BlockSpec((CTX_LEN, KV_WIDTH), lambda i: (0, cv)),
            pl.BlockSpec((4, blk, 3 * blk), lambda i: (0, 0, 0)),
        ],
        out_specs=pl.BlockSpec((tm, ATT_WIDTH), lambda i: (i, 0)),
        out_shape=jax.ShapeDtypeStruct((rows, ATT_WIDTH), dt),
        scratch_shapes=[
            pltpu.VMEM((2, tm + 2 * blk, KV_WIDTH), dt),
            pltpu.VMEM((2, tm + 2 * blk, KV_WIDTH), dt),
            pltpu.VMEM((2, CTX_LEN, KV_WIDTH), dt),
            pltpu.VMEM((2, CTX_LEN, KV_WIDTH), dt),
        ],
        compiler_params=_params(("arbitrary",)),
        name="attention",
    )(sink, proj, proj, proj, proj, proj, proj, proj, proj, proj, masks)


def _ret_kernel(lg_ref, q_ref, k_ref, v_ref, g_ref, o_ref, sb_ref, s_ref, dm_ref, tab_ref, gbd_ref):
    ph = pl.program_id(0)
    t = pl.program_id(1)
    nlat = pl.num_programs(1) - 2
    c = RET_CHUNK
    w = RET_WIDTH
    mdt = sb_ref.dtype

    def lane_vec(direction, shape, axis):
        head = lax.broadcasted_iota(jnp.int32, shape, axis) // HEAD_DIM
        out = jnp.full(shape, lg_ref[direction, RET_HEADS - 1], F32)
        for h in range(RET_HEADS - 2, -1, -1):
            out = jnp.where(head == h, lg_ref[direction, h], out)
        return out

    @pl.when(jnp.logical_and(ph == 0, t == 0))
    def _init_tables():
        diff = (lax.broadcasted_iota(jnp.int32, (c, c), 0) - lax.broadcasted_iota(jnp.int32, (c, c), 1)).astype(F32)
        for h in range(RET_HEADS):
            dm_ref[h] = jnp.exp(jnp.where(diff >= 0, diff * lg_ref[0, h], -diff * lg_ref[1, h]))
        pos = lax.broadcasted_iota(jnp.int32, (c, w), 0).astype(F32)
        lgf = lane_vec(0, (c, w), 1)
        lgb = lane_vec(1, (c, w), 1)
        tab_ref[0] = jnp.exp((c - 1.0 - pos) * lgf)
        tab_ref[1] = jnp.exp((pos + 1.0) * lgf)
        tab_ref[2] = jnp.exp(pos * lgb)
        tab_ref[3] = jnp.exp((c - pos) * lgb)
        same = (lax.broadcasted_iota(jnp.int32, (w, w), 0) // HEAD_DIM
                == lax.broadcasted_iota(jnp.int32, (w, w), 1) // HEAD_DIM)
        bd = jnp.where(same, 1.0, 0.0)
        gbd_ref[0] = bd * jnp.exp(c * lane_vec(0, (w, w), 0))
        gbd_ref[1] = bd * jnp.exp(c * lane_vec(1, (w, w), 0))
        gbd_ref[2] = bd

    @pl.when(t == 0)
    def _reset_state():
        s_ref[...] = jnp.zeros_like(s_ref)

    def state_update(direction, key_tab):
        kw = (k_ref[...].astype(F32) * tab_ref[key_tab]).astype(mdt)
        u = _mm_tn(kw, v_ref[...])
        s_ref[...] = gbd_ref[direction] * s_ref[...] + gbd_ref[2] * u

    @pl.when(jnp.logical_and(ph == 0, t <= nlat))
    def _backward_states():
        idx = jnp.where(t == 0, 0, nlat + 1 - t)
        sb_ref[idx] = s_ref[...].astype(mdt)
        state_update(1, 2)

    @pl.when(jnp.logical_and(ph == 1, t == 1))
    def _pad_rows():
        o_ref[...] = jnp.zeros_like(o_ref)

    @pl.when(jnp.logical_and(ph == 1, t != 1))
    def _outputs():
        idx = jnp.where(t == 0, 0, t - 1)
        q = q_ref[...]
        k = k_ref[...]
        v = v_ref[...]
        head = lax.broadcasted_iota(jnp.int32, (c, w), 1) // HEAD_DIM
        qs = jnp.concatenate([jnp.where(head == h, q, jnp.zeros_like(q)) for h in range(RET_HEADS)], axis=0)
        sc = _mm_nt(qs, k)
        scd = jnp.concatenate([sc[c * h:c * (h + 1)] * dm_ref[h] for h in range(RET_HEADS)], axis=0).astype(mdt)
        oi = _mm(scd, v)
        o = jnp.zeros((c, w), F32)
        for h in range(RET_HEADS):
            o = o + jnp.where(head == h, oi[c * h:c * (h + 1)], 0.0)
        qf = q.astype(F32)
        o = o + _mm((qf * tab_ref[1]).astype(mdt), s_ref[...].astype(mdt))
        o = o + _mm((qf * tab_ref[3]).astype(mdt), sb_ref[idx])
        state_update(0, 0)
        avg = (gbd_ref[2] * (1.0 / HEAD_DIM)).astype(mdt)

        def head_mean(val):
            hi = val.astype(mdt)
            lo = (val - hi.astype(F32)).astype(mdt)
            return _mm(hi, avg) + _mm(lo, avg)

        d = o - head_mean(o)
        var = head_mean(d * d)
        gate = g_ref[...].astype(F32)
        y = d * lax.rsqrt(var + GN_EPS) * (gate * jax.nn.sigmoid(gate))
        o_ref[...] = y.astype(o_ref.dtype)


def _retention(proj, log_gamma):
    rows = proj.shape[0]
    c = RET_CHUNK
    nlat = (rows - CTX_PAD) // c
    dt = proj.dtype
    cq, ckk, cvv, cg = (COL_RQ // RET_WIDTH, COL_RK // RET_WIDTH, COL_RV // RET_WIDTH, COL_RG // RET_WIDTH)

    def kv_blk(ph, t):
        back = jnp.where(t == 0, 0, jnp.maximum(nlat + 2 - t, 2))
        return jnp.where(ph == 0, back, t)

    def fw_blk(ph, t):
        return jnp.where(ph == 0, 0, t)

    return pl.pallas_call(
        _ret_kernel,
        grid=(2, nlat + 2),
        in_specs=[
            pl.BlockSpec(memory_space=pltpu.SMEM),
            pl.BlockSpec((c, RET_WIDTH), lambda ph, t: (fw_blk(ph, t), cq)),
            pl.BlockSpec((c, RET_WIDTH), lambda ph, t: (kv_blk(ph, t), ckk)),
            pl.BlockSpec((c, RET_WIDTH), lambda ph, t: (kv_blk(ph, t), cvv)),
            pl.BlockSpec((c, RET_WIDTH), lambda ph, t: (fw_blk(ph, t), cg)),
        ],
        out_specs=pl.BlockSpec((c, RET_WIDTH), lambda ph, t: (fw_blk(ph, t), 0)),
        out_shape=jax.ShapeDtypeStruct((rows, RET_WIDTH), dt),
        scratch_shapes=[
            pltpu.VMEM((nlat + 1, RET_WIDTH, RET_WIDTH), dt),
            pltpu.VMEM((RET_WIDTH, RET_WIDTH), F32),
            pltpu.VMEM((RET_HEADS, c, c), F32),
            pltpu.VMEM((4, c, RET_WIDTH), F32),
            pltpu.VMEM((3, RET_WIDTH, RET_WIDTH), F32),
        ],
        compiler_params=_params(("arbitrary", "arbitrary")),
        name="retention",
    )(log_gamma, proj, proj, proj, proj)


def _s5_weights(lam_re, lam_im, b_re, b_im, c_re, c_im, log_dt, d_skip):
    hp = lax.Precision.HIGHEST
    tt, g, n, p, a = S5_T, S5_GROUPS, S5_STATE, S5_CH, S5_PAIRS
    lam = lax.complex(lam_re.astype(F32), lam_im.astype(F32))
    dtv = jnp.exp(log_dt.astype(F32))[..., None]
    lam_bar = jnp.exp(lam * dtv)
    bbar = ((lam_bar - 1.0) / lam)[..., None] * lax.complex(b_re.astype(F32), b_im.astype(F32))
    cmat = lax.complex(c_re.astype(F32), c_im.astype(F32))
    pw = [jnp.ones_like(lam_bar)]
    for _ in range(tt):
        pw.append(pw[-1] * lam_bar)
    pw = jnp.stack(pw, axis=1)
    eye2 = jnp.eye(2, dtype=F32)
    ri = lambda z, axis: jnp.stack([jnp.real(z), jnp.imag(z)], axis=axis)

    pw_l = pw.reshape(2, tt + 1, a, 2 * n)
    bbt = jnp.einsum('dahpn,gh->dagphn', jnp.swapaxes(bbar, -1, -2).reshape(2, a, 2, p, n), eye2)
    bbt = bbt.reshape(2, a, 2 * p, 2 * n)
    cct = jnp.einsum('dahpn,gh->dagphn', cmat.reshape(2, a, 2, p, n), eye2).reshape(2, a, 2 * p, 2 * n)
    x = bbt[:, None] * pw_l[:, :tt, :, None, :]
    k = jnp.real(jnp.einsum('dlaxk,dayk->dlaxy', x, cct, precision=hp))
    lags = jnp.arange(-(tt - 1), tt)
    kl = (jnp.where((lags >= 0)[:, None, None, None], k[0][jnp.clip(lags, 0, tt - 1)], 0.0)
          + jnp.where((lags <= 0)[:, None, None, None], k[1][jnp.clip(-lags, 0, tt - 1)], 0.0))
    kall = kl.transpose(1, 2, 0, 3).reshape(a, 2 * p, (2 * tt - 1) * 2 * p)
    kall = jnp.pad(kall, ((0, 0), (0, 0), (0, 2 * p)))

    pw_k = ri(pw_l, 1).transpose(3, 0, 1, 2, 4)
    b_k = ri(bbt, 2).transpose(1, 0, 2, 3, 4)
    c_k = ri(cct, 2).transpose(1, 0, 2, 3, 4)
    lam_t = pw_l[:, tt]
    lam_t = jnp.concatenate([jnp.real(lam_t[0]), jnp.imag(lam_t[0]), jnp.real(lam_t[1]), jnp.imag(lam_t[1])],
                            axis=-1).reshape(a, 1, 4 * 2 * n)
    skip = jnp.tile(d_skip.astype(F32).reshape(a, 1, 2 * p), (1, tt, 1)).reshape(a, 1, tt * 2 * p)
    return pw_k, b_k, c_k, kall, lam_t, skip


def _pair_spec(layer, *shape):
    return pl.BlockSpec((None, None) + shape, lambda i: (layer, i) + (0,) * len(shape))


def _s5_drive_kernel(u_ref, pw_ref, b_ref, o_ref, w_ref):
    rows, half = S5_PAIR_W, LANES
    for d in range(2):
        br, bi = b_ref[d, 0], b_ref[d, 1]
        for j in range(S5_T):
            e = S5_T - 1 - j if d == 0 else j
            pr, pi = pw_ref[d, 0, e:e + 1, :], pw_ref[d, 1, e:e + 1, :]
            w_ref[rows * j:rows * (j + 1), 2 * half * d:2 * half * d + half] = (pr * br - pi * bi).astype(w_ref.dtype)
            w_ref[rows * j:rows * (j + 1), 2 * half * d + half:2 * half * (d + 1)] = (
                pr * bi + pi * br).astype(w_ref.dtype)
    o_ref[0] = _mm(u_ref[0], w_ref[...])


def _s5_drive(u_pairs, pw_k, b_k, layer):
    a, nch, wd = u_pairs.shape
    return pl.pallas_call(
        _s5_drive_kernel,
        grid=(a,),
        in_specs=[pl.BlockSpec((1, nch, wd), lambda i: (i, 0, 0)),
                  _pair_spec(layer, *pw_k.shape[2:]), _pair_spec(layer, *b_k.shape[2:])],
        out_specs=pl.BlockSpec((1, nch, 2 * wd), lambda i: (i, 0, 0)),
        out_shape=jax.ShapeDtypeStruct((a, nch, 2 * wd), F32),
        scratch_shapes=[pltpu.VMEM((wd, 2 * wd), u_pairs.dtype)],
        compiler_params=_params(("arbitrary",)),
        name="s5_drive",
    )(u_pairs, pw_k, b_k)


def _s5_scan_kernel(ef_ref, eb_ref, lam_ref, sf_ref, sb_ref, stf_ref, stb_ref):
    t = pl.program_id(0)
    hw = LANES
    sub = 8
    npair = lam_ref.shape[0]

    @pl.when(t == 0)
    def _reset():
        stf_ref[...] = jnp.zeros_like(stf_ref)
        stb_ref[...] = jnp.zeros_like(stb_ref)

    def step(lr, li, sr, si, e):
        return lr * sr - li * si + e[:, 0:hw], lr * si + li * sr + e[:, hw:2 * hw]

    def run(nrows):
        ngroups = nrows // sub

        def body(gi, carry):
            r0 = pl.multiple_of(gi * sub, sub)
            rb0 = pl.multiple_of((ngroups - 1 - gi) * sub, sub)
            new = []
            for a in range(npair):
                lam = lam_ref[a]
                lfr, lfi, lbr, lbi = (lam[:, k * hw:(k + 1) * hw] for k in range(4))
                fr, fi, br, bi = carry[a]
                ef = ef_ref[a, pl.ds(r0, sub), :]
                eb = eb_ref[a, pl.ds(rb0, sub), :]
                before_f = []
                for k in range(sub):
                    before_f.append(jnp.concatenate([fr, fi], axis=1))
                    fr, fi = step(lfr, lfi, fr, fi, ef[k:k + 1])
                sf_ref[a, pl.ds(r0, sub), :] = jnp.concatenate(before_f, axis=0)
                before_b = [None] * sub
                for k in range(sub - 1, -1, -1):
                    before_b[k] = jnp.concatenate([br, bi], axis=1)
                    br, bi = step(lbr, lbi, br, bi, eb[k:k + 1])
                sb_ref[a, pl.ds(rb0, sub), :] = jnp.concatenate(before_b, axis=0)
                new.append((fr, fi, br, bi))
            return tuple(new)

        init = tuple((stf_ref[a, :, 0:hw], stf_ref[a, :, hw:2 * hw], stb_ref[a, :, 0:hw], stb_ref[a, :, hw:2 * hw])
                     for a in range(npair))
        final = lax.fori_loop(0, ngroups, body, init)
        for a in range(npair):
            fr, fi, br, bi = final[a]
            stf_ref[a, :, 0:hw] = fr
            stf_ref[a, :, hw:2 * hw] = fi
            stb_ref[a, :, 0:hw] = br
            stb_ref[a, :, hw:2 * hw] = bi

    @pl.when(t == 0)
    def _context():
        sf_ref[...] = jnp.zeros_like(sf_ref)
        sb_ref[...] = jnp.zeros_like(sb_ref)
        run(CTX_LEN // S5_T)

    @pl.when(t > 0)
    def _latent():
        run(S5_TILE)


def _s5_scan(drive, lam_t, layer):
    a, nch, wd2 = drive.shape
    wd = wd2 // 2
    nt = nch // S5_TILE

    def bwd(t):
        return jnp.where(t == 0, 0, nt - t)

    return pl.pallas_call(
        _s5_scan_kernel,
        grid=(nt,),
        in_specs=[pl.BlockSpec((a, S5_TILE, wd), lambda t: (0, t, 0)),
                  pl.BlockSpec((a, S5_TILE, wd), lambda t: (0, bwd(t), 1)),
                  pl.BlockSpec((None, a, 1, wd2), lambda t: (layer, 0, 0, 0))],
        out_specs=[pl.BlockSpec((a, S5_TILE, wd), lambda t: (0, t, 0)),
                   pl.BlockSpec((a, S5_TILE, wd), lambda t: (0, bwd(t), 0))],
        out_shape=[jax.ShapeDtypeStruct((a, nch, wd), F32), jax.ShapeDtypeStruct((a, nch, wd), F32)],
        scratch_shapes=[pltpu.VMEM((a, 1, wd), F32), pltpu.VMEM((a, 1, wd), F32)],
        compiler_params=_params(("arbitrary",)),
        name="s5_scan",
    )(drive, drive, lam_t)


def _lane_window(x, start, width):
    cols = []
    for v in range(width // LANES):
        k0, off = divmod(start + LANES * v, LANES)
        lo = x[:, LANES * k0:LANES * (k0 + 1)]
        if off:
            hi = x[:, LANES * (k0 + 1):LANES * (k0 + 2)]
            lane = lax.broadcasted_iota(jnp.int32, lo.shape, 1)
            lo = jnp.where(lane < LANES - off, pltpu.roll(lo, LANES - off, 1), pltpu.roll(hi, LANES - off, 1))
        cols.append(lo)
    return jnp.concatenate(cols, axis=1)


def _s5_read_kernel(u_ref, sf_ref, sb_ref, pw_ref, c_ref, kall_ref, skip_ref, o_ref, wt_ref, wi_ref):
    u = u_ref[0]
    mdt = u.dtype
    rows, half = S5_PAIR_W, LANES
    for d in range(2):
        cr, ci = c_ref[d, 0], c_ref[d, 1]
        for i in range(S5_T):
            e = i + 1 if d == 0 else S5_T - i
            pr, pi = pw_ref[d, 0, e:e + 1, :], pw_ref[d, 1, e:e + 1, :]
            wt_ref[d, rows * i:rows * (i + 1), 0:half] = (pr * cr - pi * ci).astype(mdt)
            wt_ref[d, rows * i:rows * (i + 1), half:2 * half] = (-(pr * ci + pi * cr)).astype(mdt)
    kall = kall_ref[...]
    for j in range(S5_T):
        wi_ref[rows * j:rows * (j + 1), :] = _lane_window(kall, rows * (S5_T - 1 - j), S5_T * rows).astype(mdt)
    y = _mm(u, wi_ref[...])
    y = y + _mm_nt(sf_ref[0].astype(mdt), wt_ref[0])
    y = y + _mm_nt(sb_ref[0].astype(mdt), wt_ref[1])
    o_ref[0] = y + u.astype(F32) * skip_ref[...]


def _s5_read(u_pairs, sf, sb, pw_k, c_k, kall, skip, layer):
    a, nch, wd = u_pairs.shape
    blk = lambda *shape: pl.BlockSpec((1,) + shape, lambda i: (i, 0, 0))
    return pl.pallas_call(
        _s5_read_kernel,
        grid=(a,),
        in_specs=[blk(nch, wd), blk(nch, wd), blk(nch, wd), _pair_spec(layer, *pw_k.shape[2:]),
                  _pair_spec(layer, *c_k.shape[2:]), _pair_spec(layer, *kall.shape[2:]),
                  _pair_spec(layer, *skip.shape[2:])],
        out_specs=blk(nch, wd),
        out_shape=jax.ShapeDtypeStruct((a, nch, wd), F32),
        scratch_shapes=[pltpu.VMEM((2, wd, wd), u_pairs.dtype), pltpu.VMEM((wd, wd), u_pairs.dtype)],
        compiler_params=_params(("arbitrary",)),
        name="s5_read",
    )(u_pairs, sf, sb, pw_k, c_k, kall, skip)


def _s5_mixer(u_pairs, weights, layer):
    pw_k, b_k, c_k, kall, lam_t, skip = weights
    drive = _s5_drive(u_pairs, pw_k, b_k, layer)
    sf, sb = _s5_scan(drive, lam_t, layer)
    return _s5_read(u_pairs, sf, sb, pw_k, c_k, kall, skip, layer)


def _layer_norm(x, g, b):
    mu = jnp.mean(x, axis=-1, keepdims=True)
    d = x - mu
    var = jnp.mean(d * d, axis=-1, keepdims=True)
    return d * lax.rsqrt(var + LN_EPS) * g + b


def _post_kernel(x_ref, att_ref, ret_ref, s5_ref, mod_ref, permt_ref, wglu_ref, bglu_ref, woa_ref, wor_ref, wos_ref,
                 g1_ref, b1_ref, w1_ref, w2_ref, g2_ref, b2_ref, o_ref):
    mdt = w1_ref.dtype
    sub = x_ref.shape[0] // POST_SPLIT
    csub = sub // S5_T
    nff = D_FF // FF_CHUNK

    def mix(part):
        rows = slice(sub * part, sub * (part + 1))
        zrows = []
        for i in range(S5_T):
            src_vreg, src_blk = divmod(i * S5_PAIR_W, LANES)
            src_blk //= S5_PAIR_W
            cols = [_lane_block_shuffle(
                lambda a: s5_ref[a, csub * part:csub * (part + 1), src_vreg * LANES:(src_vreg + 1) * LANES],
                src_blk, w) for w in range(S5_WIDTH // LANES)]
            zrows.append(jnp.concatenate(cols, axis=1))
        z = jnp.concatenate(zrows, axis=0)
        z_hi = z.astype(mdt)
        z_lo = (z - z_hi.astype(F32)).astype(mdt)
        ys = _mm(permt_ref[...], z_hi) + _mm(permt_ref[...], z_lo)
        hs = jax.nn.gelu(ys)
        gate = jax.nn.sigmoid(_mm(hs.astype(mdt), wglu_ref[...]) + bglu_ref[...])
        s5 = (hs * gate).astype(mdt)
        return _mm(att_ref[rows, :], woa_ref[...]) + _mm(ret_ref[rows, :], wor_ref[...]) + _mm(s5, wos_ref[...])

    def norm1(part, ox):
        rows = slice(sub * part, sub * (part + 1))
        x1 = _layer_norm(DEEPNORM_ALPHA * x_ref[rows, :] + mod_ref[2:3, :] * ox, g1_ref[...], b1_ref[...])
        return x1, (x1 * (1.0 + mod_ref[4:5, :]) + mod_ref[3:4, :]).astype(mdt)

    def ff(h, c):
        a = _mm(h, w1_ref[:, FF_CHUNK * c:FF_CHUNK * (c + 1)])
        a = jnp.square(jnp.maximum(a, 0.0)).astype(mdt)
        return _mm(a, w2_ref[FF_CHUNK * c:FF_CHUNK * (c + 1), :])

    def norm2(part, x1, acc):
        rows = slice(sub * part, sub * (part + 1))
        o_ref[rows, :] = _layer_norm(DEEPNORM_ALPHA * x1 + mod_ref[5:6, :] * acc, g2_ref[...], b2_ref[...])

    nstage = nff + 3
    state = [dict() for _ in range(POST_SPLIT)]
    for part, stage in POST_PROGRAM:
        st = state[part]
        if stage == 0:
            st["ox"] = mix(part)
        elif stage == 1:
            st["x1"], st["h"] = norm1(part, st.pop("ox"))
        elif stage < nstage - 1:
            term = ff(st["h"], stage - 2)
            st["acc"] = term if stage == 2 else st["acc"] + term
        else:
            norm2(part, st["x1"], st["acc"])


def _post(stream, att, ret, s5_pairs, mods, layer, permt, wglu, bglu, wo, g1, b1, w1, w2, g2, b2, skip_context):
    rows = stream.shape[0]
    tm = ROW_TILE
    off = CTX_PAD // tm if skip_context else 0
    row_blk = lambda width: pl.BlockSpec((tm, width), lambda i: (i + off, 0))
    full = lambda arr: pl.BlockSpec(arr.shape, lambda i: (0,) * arr.ndim)
    woa, wor, wos = wo[:ATT_WIDTH], wo[ATT_WIDTH:ATT_WIDTH + RET_WIDTH], wo[ATT_WIDTH + RET_WIDTH:]
    vec = lambda v: v.reshape(1, -1).astype(F32)
    small = [permt, wglu, vec(bglu), woa, wor, wos, vec(g1), vec(b1), w1, w2, vec(g2), vec(b2)]
    return pl.pallas_call(
        _post_kernel,
        grid=(rows // tm - off,),
        in_specs=[row_blk(D_MODEL), row_blk(ATT_WIDTH), row_blk(RET_WIDTH),
                  pl.BlockSpec((S5_PAIRS, tm // S5_T, S5_T * S5_PAIR_W), lambda i: (0, i + off, 0)),
                  pl.BlockSpec((None, None, N_ADA, D_MODEL),
                               lambda i: (layer, jnp.where(i + off == 0, 1, 0), 0, 0))]
                 + [full(arr) for arr in small],
        out_specs=pl.BlockSpec((tm, D_MODEL), lambda i: (i, 0)),
        out_shape=jax.ShapeDtypeStruct((rows - off * tm, D_MODEL), F32),
        compiler_params=_params(("arbitrary",)),
        name="post",
    )(stream, att, ret, s5_pairs, mods, *small)


def kernel(x, c, ctx, c_ctx, w_ada, b_ada, w_in, att_sink, ret_decay_logit, s5_lambda_re, s5_lambda_im, s5_b_re,
           s5_b_im, s5_c_re, s5_c_im, s5_log_dt, s5_d, w_glu, b_glu, w_out, ln1_g, ln1_b, w_ff1, w_ff2, ln2_g,
           ln2_b):
    assert x.shape[0] == 1 and x.shape[2] == D_MODEL and ctx.shape[1] == CTX_LEN
    seq = x.shape[1]
    assert seq % ROW_TILE == 0
    stream = jnp.concatenate([ctx[0], jnp.zeros((CTX_PAD - CTX_LEN, D_MODEL), F32), x[0]], axis=0)
    cond = jnp.zeros((8, D_MODEL), F32).at[0].set(c[0]).at[1].set(c_ctx)
    mods = _modulation(cond, w_ada, b_ada).reshape(DEPTH, 8, N_ADA, D_MODEL)
    tabs = _rope_tables(seq)
    masks = _attention_masks()
    perm = _chunk_perm(ROW_TILE // POST_SPLIT, MXU_DTYPE)
    permt = perm.T
    col_scale = jnp.ones((IN_WIDTH,), F32).at[COL_AQ:COL_AK].set(HEAD_DIM ** -0.5)
    col_scale = col_scale.at[COL_RQ:COL_RK].set(HEAD_DIM ** -0.5)
    s5w = jax.vmap(_s5_weights)(s5_lambda_re, s5_lambda_im, s5_b_re, s5_b_im, s5_c_re, s5_c_im, s5_log_dt, s5_d)
    log_gamma = jax.nn.log_sigmoid(ret_decay_logit.astype(F32))
    for l in range(DEPTH):
        proj, u_pairs = _in_proj(stream, mods, l, (w_in[l] * col_scale).astype(MXU_DTYPE), tabs, perm)
        att = _attention(proj, att_sink[l].astype(F32), masks)
        ret = _retention(proj, log_gamma[l])
        s5 = _s5_mixer(u_pairs, s5w, l)
        stream = _post(stream, att, ret, s5, mods, l, permt, w_glu[l].astype(MXU_DTYPE), b_glu[l],
                       w_out[l].astype(MXU_DTYPE), ln1_g[l], ln1_b[l], w_ff1[l].astype(MXU_DTYPE),
                       w_ff2[l].astype(MXU_DTYPE), ln2_g[l], ln2_b[l], skip_context=(l == DEPTH - 1))
    return stream[None]
```

```python
import functools
import math

import jax
import jax.numpy as jnp
from jax import lax
from jax.experimental import pallas as pl
from jax.experimental.pallas import tpu as pltpu

F32 = jnp.float32
MXU_DTYPE = jnp.bfloat16

D_MODEL = 1024
DEPTH = 4
GRID_W = 64
CTX_LEN = 256
CTX_PAD = 512
HEAD_DIM = 64
ATT_HEADS = 8
ATT_KV_HEADS = 2
ATT_BLOCK = 128
ATT_LOOKAHEAD = 2
ROPE_BASE = 10000.0
RET_HEADS = 4
RET_CHUNK = 256
S5_CH = 16
S5_GROUPS = 16
S5_STATE = 64
S5_T = 8
S5_PAIRS = S5_GROUPS // 2
S5_PAIR_W = 2 * S5_CH
S5_TILE = 64
ATT_WIDTH = ATT_HEADS * HEAD_DIM
KV_WIDTH = ATT_KV_HEADS * HEAD_DIM
RET_WIDTH = RET_HEADS * HEAD_DIM
S5_WIDTH = S5_GROUPS * S5_CH
IN_WIDTH = ATT_WIDTH + 2 * KV_WIDTH + 4 * RET_WIDTH + S5_WIDTH
D_FF = 4 * D_MODEL
FF_CHUNK = 1024
N_ADA = 6
LN_EPS = 1e-5
GN_EPS = 1e-5
DEEPNORM_ALPHA = (2 * DEPTH) ** 0.25
ROW_TILE = 512
POST_SPLIT = 2
POST_PROGRAM = ((0, 0), (0, 1), (1, 0), (0, 2), (1, 1), (0, 3), (0, 4), (0, 5), (1, 2), (0, 6), (1, 3), (1, 4),
                (1, 5), (1, 6))
NEG_BIG = -1e30
LOG2E = math.log2(math.e)
LANES = 128
VMEM_LIMIT = 56 * 1024 * 1024

COL_AQ, COL_AK, COL_AV = 0, ATT_WIDTH, ATT_WIDTH + KV_WIDTH
COL_RQ = ATT_WIDTH + 2 * KV_WIDTH
COL_RK, COL_RV, COL_RG = COL_RQ + RET_WIDTH, COL_RQ + 2 * RET_WIDTH, COL_RQ + 3 * RET_WIDTH
COL_S5 = COL_RQ + 4 * RET_WIDTH


def _mm(a, b):
    return jnp.dot(a, b, preferred_element_type=F32)


def _mm_nt(a, b):
    return lax.dot_general(a, b, (((1,), (1,)), ((), ())), preferred_element_type=F32)


def _mm_tn(a, b):
    return lax.dot_general(a, b, (((0,), (0,)), ((), ())), preferred_element_type=F32)


def _params(sem):
    return pltpu.CompilerParams(dimension_semantics=sem, vmem_limit_bytes=VMEM_LIMIT)


def _mod_kernel(cond_ref, w_ref, b_ref, o_ref):
    c = cond_ref[...]
    s = c * jax.nn.sigmoid(c)
    o_ref[0] = jnp.dot(s, w_ref[0], preferred_element_type=F32, precision=lax.Precision.HIGHEST) + b_ref[0]


def _modulation(cond, w_ada, b_ada):
    tn = 1536
    n = N_ADA * D_MODEL
    return pl.pallas_call(
        _mod_kernel,
        grid=(DEPTH, n // tn),
        in_specs=[
            pl.BlockSpec((8, D_MODEL), lambda l, j: (0, 0)),
            pl.BlockSpec((1, D_MODEL, tn), lambda l, j: (l, 0, j)),
            pl.BlockSpec((1, 1, tn), lambda l, j: (l, 0, j)),
        ],
        out_specs=pl.BlockSpec((1, 8, tn), lambda l, j: (l, 0, j)),
        out_shape=jax.ShapeDtypeStruct((DEPTH, 8, n), F32),
        compiler_params=_params(("arbitrary", "arbitrary")),
        name="modulation",
    )(cond, w_ada, b_ada.reshape(DEPTH, 1, n))


def _lane_block_shuffle(src_rows, src_lane_blk, out_vreg):
    acc = None
    for q in range(LANES // S5_PAIR_W):
        piece = src_rows(out_vreg * (LANES // S5_PAIR_W) + q)
        shift = (S5_PAIR_W * (q - src_lane_blk)) % LANES
        if shift:
            piece = pltpu.roll(piece, shift, 1)
        if acc is None:
            acc = piece
        else:
            lane_blk = lax.broadcasted_iota(jnp.int32, piece.shape, 1) // S5_PAIR_W
            acc = jnp.where(lane_blk == q, piece, acc)
    return acc


def _in_proj_kernel(x_ref, mod_ref, w_ref, ca_ref, sa_ref, cr_ref, sr_ref, perm_ref, o_ref, u_ref):
    x = x_ref[...]
    h = (x * (1.0 + mod_ref[1:2, :]) + mod_ref[0:1, :]).astype(w_ref.dtype)
    lane = lax.broadcasted_iota(jnp.int32, (x.shape[0], LANES), 1)
    first_att = (lane & 31) < 16
    first_ret = (lane & 63) < 32

    def proj(c0, c1):
        return _mm(h, w_ref[:, c0:c1])

    def rope_store(c0, width, cos, sin, first, half):
        p = proj(c0, c0 + width)
        for b in range(width // LANES):
            blk = p[:, LANES * b:LANES * (b + 1)]
            rot = jnp.where(first, pltpu.roll(blk, LANES - half, 1), pltpu.roll(blk, half, 1))
            o_ref[:, c0 + LANES * b:c0 + LANES * (b + 1)] = (blk * cos + rot * sin).astype(o_ref.dtype)

    u = proj(COL_S5, IN_WIDTH).astype(w_ref.dtype)
    sub = perm_ref.shape[0]
    nchunk = sub // S5_T
    for part in range(x.shape[0] // sub):
        g = _mm(perm_ref[...], u[sub * part:sub * (part + 1)])
        for a in range(S5_PAIRS):
            vreg_col, lane_blk = divmod(a * S5_PAIR_W, LANES)
            lane_blk //= S5_PAIR_W
            for v in range(S5_T * S5_PAIR_W // LANES):
                slab = _lane_block_shuffle(
                    lambda j: g[nchunk * j:nchunk * (j + 1), vreg_col * LANES:(vreg_col + 1) * LANES], lane_blk, v)
                u_ref[a, nchunk * part:nchunk * (part + 1), LANES * v:LANES * (v + 1)] = slab.astype(u_ref.dtype)

    ca, sa, cr, sr = ca_ref[...], sa_ref[...], cr_ref[...], sr_ref[...]
    rope_store(COL_AQ, ATT_WIDTH, ca, sa, first_att, 16)
    rope_store(COL_AK, KV_WIDTH, ca, sa, first_att, 16)
    rope_store(COL_RQ, RET_WIDTH, cr, sr, first_ret, 32)
    rope_store(COL_RK, RET_WIDTH, cr, sr, first_ret, 32)
    o_ref[:, COL_AV:COL_RQ] = proj(COL_AV, COL_RQ).astype(o_ref.dtype)
    o_ref[:, COL_RV:COL_S5] = proj(COL_RV, COL_S5).astype(o_ref.dtype)


def _chunk_perm(tile_rows, dtype):
    nchunk = tile_rows // S5_T
    r = jnp.arange(tile_rows)
    src = S5_T * (r % nchunk) + r // nchunk
    return (src[:, None] == jnp.arange(tile_rows)[None, :]).astype(dtype)


def _in_proj(stream, mods, layer, w_in, tabs, perm):
    rows = stream.shape[0]
    tm = ROW_TILE
    tab_spec = pl.BlockSpec((tm, LANES), lambda i: (i, 0))
    nch = rows // S5_T
    return pl.pallas_call(
        _in_proj_kernel,
        grid=(rows // tm,),
        in_specs=[
            pl.BlockSpec((tm, D_MODEL), lambda i: (i, 0)),
            pl.BlockSpec((None, None, N_ADA, D_MODEL), lambda i: (layer, jnp.where(i == 0, 1, 0), 0, 0)),
            pl.BlockSpec((D_MODEL, IN_WIDTH), lambda i: (0, 0)),
            tab_spec, tab_spec, tab_spec, tab_spec,
            pl.BlockSpec(perm.shape, lambda i: (0, 0)),
        ],
        out_specs=[pl.BlockSpec((tm, COL_S5), lambda i: (i, 0)),
                   pl.BlockSpec((S5_PAIRS, tm // S5_T, S5_T * S5_PAIR_W), lambda i: (0, i, 0))],
        out_shape=[jax.ShapeDtypeStruct((rows, COL_S5), MXU_DTYPE),
                   jax.ShapeDtypeStruct((S5_PAIRS, nch, S5_T * S5_PAIR_W), MXU_DTYPE)],
        compiler_params=_params(("arbitrary",)),
        name="in_proj",
    )(stream, mods, w_in, *tabs, perm)


def _rope_tables(seq):
    half_a = HEAD_DIM // 4
    half_r = HEAD_DIM // 2
    nrow = seq // GRID_W
    inv_a = ROPE_BASE ** (-jnp.arange(half_a, dtype=F32) / half_a)
    inv_r = ROPE_BASE ** (-jnp.arange(half_r, dtype=F32) / half_r)
    ang_r = jnp.arange(nrow, dtype=F32)[:, None] * inv_a[None, :]
    ang_c = jnp.arange(GRID_W, dtype=F32)[:, None] * inv_a[None, :]
    ang_t = jnp.arange(seq, dtype=F32)[:, None] * inv_r[None, :]
    hp = lax.Precision.HIGHEST
    lane = jnp.arange(LANES)
    within = lane % HEAD_DIM
    pick_a = (within % half_a)[None, :] == jnp.arange(half_a)[:, None]
    exp_row = (pick_a & (within < 2 * half_a)[None, :]).astype(F32)
    exp_col = (pick_a & (within >= 2 * half_a)[None, :]).astype(F32)
    exp_t = ((lane % half_r)[None, :] == jnp.arange(half_r)[:, None]).astype(F32)
    sign_a = jnp.where(within % (2 * half_a) < half_a, -1.0, 1.0).astype(F32)
    sign_r = jnp.where(within < half_r, -1.0, 1.0).astype(F32)

    def att_table(fn):
        by_row = jnp.dot(fn(ang_r), exp_row, precision=hp)
        by_col = jnp.dot(fn(ang_c), exp_col, precision=hp)
        return (by_row[:, None, :] + by_col[None, :, :]).reshape(seq, LANES)

    cos_a = att_table(jnp.cos)
    sin_a = att_table(jnp.sin) * sign_a
    cos_r = jnp.dot(jnp.cos(ang_t), exp_t, precision=hp)
    sin_r = jnp.dot(jnp.sin(ang_t), exp_t, precision=hp) * sign_r
    pad = lambda tab, ident: jnp.pad(tab, ((CTX_PAD, 0), (0, 0)), constant_values=ident)
    return pad(cos_a, 1.0), pad(sin_a, 0.0), pad(cos_r, 1.0), pad(sin_r, 0.0)


def _swap_halves(x):
    if x.dtype.itemsize == 4:
        return pltpu.roll(x, 64, 1)
    packed = pltpu.bitcast(x, jnp.uint32)
    return pltpu.bitcast(pltpu.roll(packed, 64, 1), x.dtype)


def _dup_heads(x):
    sw = _swap_halves(x)
    lo = lax.broadcasted_iota(jnp.int32, x.shape, 1) < HEAD_DIM
    return jnp.where(lo, x, sw), jnp.where(lo, sw, x)


def _attn_kernel(sink_ref, q_ref, km_ref, kp_ref, kn_ref, vm_ref, vp_ref, vn_ref, kc_ref, vc_ref, mask_ref,
                 o_ref, k2_ref, v2_ref, kc2_ref, vc2_ref):
    i = pl.program_id(0)
    last_blk = pl.num_programs(0) * (ROW_TILE // ATT_BLOCK) - 1
    blk = ATT_BLOCK
    def spread(src, ones_upper):
        x = src[...]
        a, b = _dup_heads(x)
        if ones_upper:
            upper = lax.broadcasted_iota(jnp.int32, x.shape, 1) >= HEAD_DIM
            a = jnp.where(upper, jnp.ones_like(a), a)
            b = jnp.where(upper, jnp.ones_like(b), b)
        return a, b

    for dst, parts, is_v in ((k2_ref, (kp_ref, km_ref, kn_ref), False), (v2_ref, (vp_ref, vm_ref, vn_ref), True)):
        row = 0
        for part in parts:
            a, b = spread(part, is_v)
            n = part.shape[0]
            dst[0, row:row + n, :] = a
            dst[1, row:row + n, :] = b
            row += n
    for dst, src, is_v in ((kc2_ref, kc_ref, False), (vc2_ref, vc_ref, True)):
        a, b = spread(src, is_v)
        dst[0] = a
        dst[1] = b

    lo = lax.broadcasted_iota(jnp.int32, (blk, LANES), 1) < HEAD_DIM
    group = ATT_HEADS // ATT_KV_HEADS

    nloc = 3 * blk

    def scores(j, kv):
        r0 = j * blk
        qt = q_ref[r0:r0 + blk, group * HEAD_DIM * kv:group * HEAD_DIM * (kv + 1)]
        parts = []
        for g in range(group):
            qc = qt[:, LANES * (g // 2):LANES * (g // 2 + 1)]
            keep = lo if g % 2 == 0 else jnp.logical_not(lo)
            parts.append(jnp.where(keep, qc, jnp.zeros_like(qc)))
        qs = jnp.concatenate(parts, axis=0)
        return _mm_nt(qs, k2_ref[kv, r0:r0 + nloc, :]), _mm_nt(qs, kc2_ref[kv])

    def finish(j, kv, s_loc, s_ctx):
        r0 = j * blk
        gblk = i * (ROW_TILE // blk) + j
        sel = jnp.where(i == 0, 3, jnp.where(gblk == CTX_PAD // blk, 0, jnp.where(gblk == last_blk, 2, 1)))
        bias = mask_ref[sel]
        probs, sink_w = [], []
        for g in range(group):
            s = jnp.concatenate([s_loc[blk * g:blk * (g + 1)] + bias, s_ctx[blk * g:blk * (g + 1)]], axis=1)
            sk = sink_ref[group * kv + g]
            m = jnp.maximum(jnp.max(s, axis=-1, keepdims=True), sk)
            probs.append(jnp.exp2(s - m).astype(o_ref.dtype))
            sink_w.append(jnp.exp2(sk - m))
        p = jnp.concatenate(probs, axis=0)
        o = _mm(p[:, :nloc], v2_ref[kv, r0:r0 + nloc, :]) + _mm(p[:, nloc:], vc2_ref[kv])
        for half in range(group // 2):
            even, odd = 2 * half, 2 * half + 1
            oe = o[blk * even:blk * (even + 1)]
            oo = o[blk * odd:blk * (odd + 1)]
            y_even = oe * (1.0 / (pltpu.roll(oe, HEAD_DIM, 1) + sink_w[even]))
            y_odd = pltpu.roll(oo, HEAD_DIM, 1) * (1.0 / (oo + sink_w[odd]))
            c0 = group * HEAD_DIM * kv + LANES * half
            o_ref[r0:r0 + blk, c0:c0 + LANES] = jnp.where(lo, y_even, y_odd).astype(o_ref.dtype)

    items = [(j, kv) for j in range(ROW_TILE // blk) for kv in range(ATT_KV_HEADS)]
    pending = {}
    for t in range(len(items) + ATT_LOOKAHEAD):
        if t < len(items):
            pending[t] = scores(*items[t])
        if t >= ATT_LOOKAHEAD:
            finish(*items[t - ATT_LOOKAHEAD], *pending.pop(t - ATT_LOOKAHEAD))


def _attention_masks():
    qi = jnp.arange(ATT_BLOCK)[:, None]
    kj = jnp.arange(3 * ATT_BLOCK)[None, :]
    band = jnp.abs(kj - ATT_BLOCK - qi) <= ATT_BLOCK
    first = band & (kj >= ATT_BLOCK)
    last = band & (kj < 2 * ATT_BLOCK)
    none = jnp.zeros_like(band)
    masks = jnp.stack([first, band, last, none])
    return jnp.where(masks, 0.0, NEG_BIG).astype(F32)


def _attention(proj, sink, masks):
    rows = proj.shape[0]
    tm, blk = ROW_TILE, ATT_BLOCK
    per = tm // blk
    nblk = rows // blk
    ck, cv = COL_AK // KV_WIDTH, COL_AV // KV_WIDTH
    dt = proj.dtype
    return pl.pallas_call(
        _attn_kernel,
        grid=(rows // tm,),
        in_specs=[
            pl.BlockSpec(memory_space=pltpu.SMEM),
            pl.BlockSpec((tm, ATT_WIDTH), lambda i: (i, 0)),
            pl.BlockSpec((tm, KV_WIDTH), lambda i: (i, ck)),
            pl.BlockSpec((blk, KV_WIDTH), lambda i: (jnp.maximum(i * per - 1, 0), ck)),
            pl.BlockSpec((blk, KV_WIDTH), lambda i: (jnp.minimum((i + 1) * per, nblk - 1), ck)),
            pl.BlockSpec((tm, KV_WIDTH), lambda i: (i, cv)),
            pl.BlockSpec((blk, KV_WIDTH), lambda i: (jnp.maximum(i * per - 1, 0), cv)),
            pl.BlockSpec((blk, KV_WIDTH), lambda i: (jnp.minimum((i + 1) * per, nblk - 1), cv)),
            pl.BlockSpec((CTX_LEN, KV_WIDTH), lambda i: (0, ck)),
            pl.BlockSpec((CTX_LEN, KV_WIDTH), lambda i: (0, cv)),
            pl.BlockSpec((4, blk, 3 * blk), lambda i: (0, 0, 0)),
        ],
        out_specs=pl.BlockSpec((tm, ATT_WIDTH), lambda i: (i, 0)),
        out_shape=jax.ShapeDtypeStruct((rows, ATT_WIDTH), dt),
        scratch_shapes=[
            pltpu.VMEM((2, tm + 2 * blk, KV_WIDTH), dt),
            pltpu.VMEM((2, tm + 2 * blk, KV_WIDTH), dt),
            pltpu.VMEM((2, CTX_LEN, KV_WIDTH), dt),
            pltpu.VMEM((2, CTX_LEN, KV_WIDTH), dt),
        ],
        compiler_params=_params(("arbitrary",)),
        name="attention",
    )(sink, proj, proj, proj, proj, proj, proj, proj, proj, proj, masks)


def _ret_kernel(lg_ref, q_ref, k_ref, v_ref, g_ref, o_ref, sb_ref, s_ref, dm_ref, tab_ref, gbd_ref):
    ph = pl.program_id(0)
    t = pl.program_id(1)
    ntile = pl.num_programs(1)
    c = RET_CHUNK
    w = RET_WIDTH
    per = q_ref.shape[0] // c
    mdt = sb_ref.dtype
    rows = lambda ci: slice(c * ci, c * (ci + 1))

    def lane_vec(direction, shape, axis):
        head = lax.broadcasted_iota(jnp.int32, shape, axis) // HEAD_DIM
        out = jnp.full(shape, lg_ref[direction, RET_HEADS - 1], F32)
        for h in range(RET_HEADS - 2, -1, -1):
            out = jnp.where(head == h, lg_ref[direction, h], out)
        return out

    @pl.when(jnp.logical_and(ph == 0, t == 0))
    def _init_tables():
        diff = (lax.broadcasted_iota(jnp.int32, (c, c), 0) - lax.broadcasted_iota(jnp.int32, (c, c), 1)).astype(F32)
        for h in range(RET_HEADS):
            dm_ref[h] = jnp.exp(jnp.where(diff >= 0, diff * lg_ref[0, h], -diff * lg_ref[1, h]))
        pos = lax.broadcasted_iota(jnp.int32, (c, w), 0).astype(F32)
        lgf = lane_vec(0, (c, w), 1)
        lgb = lane_vec(1, (c, w), 1)
        tab_ref[0] = jnp.exp((c - 1.0 - pos) * lgf)
        tab_ref[1] = jnp.exp((pos + 1.0) * lgf)
        tab_ref[2] = jnp.exp(pos * lgb)
        tab_ref[3] = jnp.exp((c - pos) * lgb)
        same = (lax.broadcasted_iota(jnp.int32, (w, w), 0) // HEAD_DIM
                == lax.broadcasted_iota(jnp.int32, (w, w), 1) // HEAD_DIM)
        bd = jnp.where(same, 1.0, 0.0)
        gbd_ref[0] = bd * jnp.exp(c * lane_vec(0, (w, w), 0))
        gbd_ref[1] = bd * jnp.exp(c * lane_vec(1, (w, w), 0))
        gbd_ref[2] = bd

    @pl.when(t == 0)
    def _reset_state():
        s_ref[...] = jnp.zeros_like(s_ref)

    def state_update(direction, key_tab, ci):
        kw = (k_ref[rows(ci), :].astype(F32) * tab_ref[key_tab]).astype(mdt)
        u = _mm_tn(kw, v_ref[rows(ci), :])
        s_ref[...] = gbd_ref[direction] * s_ref[...] + gbd_ref[2] * u

    @pl.when(jnp.logical_and(ph == 0, t == 0))
    def _backward_context():
        sb_ref[0] = s_ref[...].astype(mdt)
        state_update(1, 2, 0)

    @pl.when(jnp.logical_and(ph == 0, t > 0))
    def _backward_latent():
        base = 1 + per * (ntile - t - 1)
        for ci in range(per - 1, -1, -1):
            sb_ref[base + ci] = s_ref[...].astype(mdt)
            state_update(1, 2, ci)

    head = lax.broadcasted_iota(jnp.int32, (c, w), 1) // HEAD_DIM

    def scores(ci):
        q = q_ref[rows(ci), :]
        qs = jnp.concatenate([jnp.where(head == h, q, jnp.zeros_like(q)) for h in range(RET_HEADS)], axis=0)
        return _mm_nt(qs, k_ref[rows(ci), :])

    def intra(ci, sc):
        scd = jnp.concatenate([sc[c * h:c * (h + 1)] * dm_ref[h] for h in range(RET_HEADS)], axis=0).astype(mdt)
        oi = _mm(scd, v_ref[rows(ci), :])
        o = jnp.where(head == 0, oi[0:c], 0.0)
        for h in range(1, RET_HEADS):
            o = o + jnp.where(head == h, oi[c * h:c * (h + 1)], 0.0)
        return o

    def cross(ci, idx):
        qf = q_ref[rows(ci), :].astype(F32)
        return (_mm((qf * tab_ref[1]).astype(mdt), s_ref[...].astype(mdt))
                + _mm((qf * tab_ref[3]).astype(mdt), sb_ref[idx]))

    def finish(ci, o):
        avg = (gbd_ref[2] * (1.0 / HEAD_DIM)).astype(mdt)

        def head_mean(val):
            hi = val.astype(mdt)
            lo = (val - hi.astype(F32)).astype(mdt)
            return _mm(hi, avg) + _mm(lo, avg)

        d = o - head_mean(o)
        var = head_mean(d * d)
        gate = g_ref[rows(ci), :].astype(F32)
        y = d * lax.rsqrt(var + GN_EPS) * (gate * jax.nn.sigmoid(gate))
        o_ref[rows(ci), :] = y.astype(o_ref.dtype)

    @pl.when(jnp.logical_and(ph == 1, t == 0))
    def _forward_context():
        o = intra(0, scores(0)) + cross(0, 0)
        state_update(0, 0, 0)
        finish(0, o)
        for ci in range(1, per):
            o_ref[rows(ci), :] = jnp.zeros((c, w), o_ref.dtype)

    @pl.when(jnp.logical_and(ph == 1, t > 0))
    def _forward_latent():
        base = 1 + per * (t - 1)
        sc = [scores(ci) for ci in range(per)]
        outs = [intra(ci, sc[ci]) for ci in range(per)]
        for ci in range(per):
            outs[ci] = outs[ci] + cross(ci, base + ci)
            state_update(0, 0, ci)
        for ci in range(per):
            finish(ci, outs[ci])


def _retention(proj, log_gamma):
    rows = proj.shape[0]
    c = RET_CHUNK
    tm = ROW_TILE
    ntile = rows // tm
    nchunk = 1 + (rows - CTX_PAD) // c
    dt = proj.dtype
    cq, ckk, cvv, cg = (COL_RQ // RET_WIDTH, COL_RK // RET_WIDTH, COL_RV // RET_WIDTH, COL_RG // RET_WIDTH)

    def kv_blk(ph, t):
        return jnp.where(ph == 0, jnp.where(t == 0, 0, ntile - t), t)

    def fw_blk(ph, t):
        return jnp.where(ph == 0, 0, t)

    return pl.pallas_call(
        _ret_kernel,
        grid=(2, ntile),
        in_specs=[
            pl.BlockSpec(memory_space=pltpu.SMEM),
            pl.BlockSpec((tm, RET_WIDTH), lambda ph, t: (fw_blk(ph, t), cq)),
            pl.BlockSpec((tm, RET_WIDTH), lambda ph, t: (kv_blk(ph, t), ckk)),
            pl.BlockSpec((tm, RET_WIDTH), lambda ph, t: (kv_blk(ph, t), cvv)),
            pl.BlockSpec((tm, RET_WIDTH), lambda ph, t: (fw_blk(ph, t), cg)),
        ],
        out_specs=pl.BlockSpec((tm, RET_WIDTH), lambda ph, t: (fw_blk(ph, t), 0)),
        out_shape=jax.ShapeDtypeStruct((rows, RET_WIDTH), dt),
        scratch_shapes=[
            pltpu.VMEM((nchunk, RET_WIDTH, RET_WIDTH), dt),
            pltpu.VMEM((RET_WIDTH, RET_WIDTH), F32),
            pltpu.VMEM((RET_HEADS, c, c), F32),
            pltpu.VMEM((4, c, RET_WIDTH), F32),
            pltpu.VMEM((3, RET_WIDTH, RET_WIDTH), F32),
        ],
        compiler_params=_params(("arbitrary", "arbitrary")),
        name="retention",
    )(log_gamma, proj, proj, proj, proj)


def _s5_weights(lam_re, lam_im, b_re, b_im, c_re, c_im, log_dt, d_skip):
    hp = lax.Precision.HIGHEST
    tt, g, n, p, a = S5_T, S5_GROUPS, S5_STATE, S5_CH, S5_PAIRS
    lam = lax.complex(lam_re.astype(F32), lam_im.astype(F32))
    dtv = jnp.exp(log_dt.astype(F32))[..., None]
    lam_bar = jnp.exp(lam * dtv)
    bbar = ((lam_bar - 1.0) / lam)[..., None] * lax.complex(b_re.astype(F32), b_im.astype(F32))
    cmat = lax.complex(c_re.astype(F32), c_im.astype(F32))
    pw = [jnp.ones_like(lam_bar)]
    for _ in range(tt):
        pw.append(pw[-1] * lam_bar)
    pw = jnp.stack(pw, axis=1)
    eye2 = jnp.eye(2, dtype=F32)
    ri = lambda z, axis: jnp.stack([jnp.real(z), jnp.imag(z)], axis=axis)

    pw_l = pw.reshape(2, tt + 1, a, 2 * n)
    bbt = jnp.einsum('dahpn,gh->dagphn', jnp.swapaxes(bbar, -1, -2).reshape(2, a, 2, p, n), eye2)
    bbt = bbt.reshape(2, a, 2 * p, 2 * n)
    cct = jnp.einsum('dahpn,gh->dagphn', cmat.reshape(2, a, 2, p, n), eye2).reshape(2, a, 2 * p, 2 * n)
    x = bbt[:, None] * pw_l[:, :tt, :, None, :]
    k = jnp.real(jnp.einsum('dlaxk,dayk->dlaxy', x, cct, precision=hp))
    lags = jnp.arange(-(tt - 1), tt)
    kl = (jnp.where((lags >= 0)[:, None, None, None], k[0][jnp.clip(lags, 0, tt - 1)], 0.0)
          + jnp.where((lags <= 0)[:, None, None, None], k[1][jnp.clip(-lags, 0, tt - 1)], 0.0))
    kall = kl.transpose(1, 2, 0, 3).reshape(a, 2 * p, (2 * tt - 1) * 2 * p)
    kall = jnp.pad(kall, ((0, 0), (0, 0), (0, 2 * p)))

    pw_k = ri(pw_l, 1).transpose(3, 0, 1, 2, 4)
    b_k = ri(bbt, 2).transpose(1, 0, 2, 3, 4)
    c_k = ri(cct, 2).transpose(1, 0, 2, 3, 4)
    lam_t = pw_l[:, tt]
    lam_t = jnp.concatenate([jnp.real(lam_t[0]), jnp.imag(lam_t[0]), jnp.real(lam_t[1]), jnp.imag(lam_t[1])],
                            axis=-1).reshape(a, 1, 4 * 2 * n)
    skip = jnp.tile(d_skip.astype(F32).reshape(a, 1, 2 * p), (1, tt, 1)).reshape(a, 1, tt * 2 * p)
    return pw_k, b_k, c_k, kall, lam_t, skip


def _pair_spec(layer, *shape):
    return pl.BlockSpec((None, None) + shape, lambda i: (layer, i) + (0,) * len(shape))


def _s5_drive_kernel(u_ref, pw_ref, b_ref, o_ref, w_ref):
    rows, half = S5_PAIR_W, LANES
    for d in range(2):
        br, bi = b_ref[d, 0], b_ref[d, 1]
        for j in range(S5_T):
            e = S5_T - 1 - j if d == 0 else j
            pr, pi = pw_ref[d, 0, e:e + 1, :], pw_ref[d, 1, e:e + 1, :]
            w_ref[rows * j:rows * (j + 1), 2 * half * d:2 * half * d + half] = (pr * br - pi * bi).astype(w_ref.dtype)
            w_ref[rows * j:rows * (j + 1), 2 * half * d + half:2 * half * (d + 1)] = (
                pr * bi + pi * br).astype(w_ref.dtype)
    o_ref[0] = _mm(u_ref[0], w_ref[...])


def _s5_drive(u_pairs, pw_k, b_k, layer):
    a, nch, wd = u_pairs.shape
    return pl.pallas_call(
        _s5_drive_kernel,
        grid=(a,),
        in_specs=[pl.BlockSpec((1, nch, wd), lambda i: (i, 0, 0)),
                  _pair_spec(layer, *pw_k.shape[2:]), _pair_spec(layer, *b_k.shape[2:])],
        out_specs=pl.BlockSpec((1, nch, 2 * wd), lambda i: (i, 0, 0)),
        out_shape=jax.ShapeDtypeStruct((a, nch, 2 * wd), F32),
        scratch_shapes=[pltpu.VMEM((wd, 2 * wd), u_pairs.dtype)],
        compiler_params=_params(("arbitrary",)),
        name="s5_drive",
    )(u_pairs, pw_k, b_k)


def _s5_scan_kernel(ef_ref, eb_ref, lam_ref, sf_ref, sb_ref, stf_ref, stb_ref):
    t = pl.program_id(0)
    hw = LANES
    sub = 8
    npair = lam_ref.shape[0]

    @pl.when(t == 0)
    def _reset():
        stf_ref[...] = jnp.zeros_like(stf_ref)
        stb_ref[...] = jnp.zeros_like(stb_ref)

    def step(lr, li, sr, si, e):
        return lr * sr - li * si + e[:, 0:hw], lr * si + li * sr + e[:, hw:2 * hw]

    def run(nrows):
        ngroups = nrows // sub

        def body(gi, carry):
            r0 = pl.multiple_of(gi * sub, sub)
            rb0 = pl.multiple_of((ngroups - 1 - gi) * sub, sub)
            new = []
            for a in range(npair):
                lam = lam_ref[a]
                lfr, lfi, lbr, lbi = (lam[:, k * hw:(k + 1) * hw] for k in range(4))
                fr, fi, br, bi = carry[a]
                ef = ef_ref[a, pl.ds(r0, sub), :]
                eb = eb_ref[a, pl.ds(rb0, sub), :]
                before_f = []
                for k in range(sub):
                    before_f.append(jnp.concatenate([fr, fi], axis=1))
                    fr, fi = step(lfr, lfi, fr, fi, ef[k:k + 1])
                sf_ref[a, pl.ds(r0, sub), :] = jnp.concatenate(before_f, axis=0)
                before_b = [None] * sub
                for k in range(sub - 1, -1, -1):
                    before_b[k] = jnp.concatenate([br, bi], axis=1)
                    br, bi = step(lbr, lbi, br, bi, eb[k:k + 1])
                sb_ref[a, pl.ds(rb0, sub), :] = jnp.concatenate(before_b, axis=0)
                new.append((fr, fi, br, bi))
            return tuple(new)

        init = tuple((stf_ref[a, :, 0:hw], stf_ref[a, :, hw:2 * hw], stb_ref[a, :, 0:hw], stb_ref[a, :, hw:2 * hw])
                     for a in range(npair))
        final = lax.fori_loop(0, ngroups, body, init)
        for a in range(npair):
            fr, fi, br, bi = final[a]
            stf_ref[a, :, 0:hw] = fr
            stf_ref[a, :, hw:2 * hw] = fi
            stb_ref[a, :, 0:hw] = br
            stb_ref[a, :, hw:2 * hw] = bi

    @pl.when(t == 0)
    def _context():
        sf_ref[...] = jnp.zeros_like(sf_ref)
        sb_ref[...] = jnp.zeros_like(sb_ref)
        run(CTX_LEN // S5_T)

    @pl.when(t > 0)
    def _latent():
        run(S5_TILE)


def _s5_scan(drive, lam_t, layer):
    a, nch, wd2 = drive.shape
    wd = wd2 // 2
    nt = nch // S5_TILE

    def bwd(t):
        return jnp.where(t == 0, 0, nt - t)

    return pl.pallas_call(
        _s5_scan_kernel,
        grid=(nt,),
        in_specs=[pl.BlockSpec((a, S5_TILE, wd), lambda t: (0, t, 0)),
                  pl.BlockSpec((a, S5_TILE, wd), lambda t: (0, bwd(t), 1)),
                  pl.BlockSpec((None, a, 1, wd2), lambda t: (layer, 0, 0, 0))],
        out_specs=[pl.BlockSpec((a, S5_TILE, wd), lambda t: (0, t, 0)),
                   pl.BlockSpec((a, S5_TILE, wd), lambda t: (0, bwd(t), 0))],
        out_shape=[jax.ShapeDtypeStruct((a, nch, wd), F32), jax.ShapeDtypeStruct((a, nch, wd), F32)],
        scratch_shapes=[pltpu.VMEM((a, 1, wd), F32), pltpu.VMEM((a, 1, wd), F32)],
        compiler_params=_params(("arbitrary",)),
        name="s5_scan",
    )(drive, drive, lam_t)


def _lane_window(x, start, width):
    cols = []
    for v in range(width // LANES):
        k0, off = divmod(start + LANES * v, LANES)
        lo = x[:, LANES * k0:LANES * (k0 + 1)]
        if off:
            hi = x[:, LANES * (k0 + 1):LANES * (k0 + 2)]
            lane = lax.broadcasted_iota(jnp.int32, lo.shape, 1)
            lo = jnp.where(lane < LANES - off, pltpu.roll(lo, LANES - off, 1), pltpu.roll(hi, LANES - off, 1))
        cols.append(lo)
    return jnp.concatenate(cols, axis=1)


def _s5_read_kernel(u_ref, sf_ref, sb_ref, pw_ref, c_ref, kall_ref, skip_ref, o_ref, wt_ref, wi_ref):
    u = u_ref[0]
    mdt = u.dtype
    rows, half = S5_PAIR_W, LANES
    for d in range(2):
        cr, ci = c_ref[d, 0], c_ref[d, 1]
        for i in range(S5_T):
            e = i + 1 if d == 0 else S5_T - i
            pr, pi = pw_ref[d, 0, e:e + 1, :], pw_ref[d, 1, e:e + 1, :]
            wt_ref[d, rows * i:rows * (i + 1), 0:half] = (pr * cr - pi * ci).astype(mdt)
            wt_ref[d, rows * i:rows * (i + 1), half:2 * half] = (-(pr * ci + pi * cr)).astype(mdt)
    kall = kall_ref[...]
    for j in range(S5_T):
        wi_ref[rows * j:rows * (j + 1), :] = _lane_window(kall, rows * (S5_T - 1 - j), S5_T * rows).astype(mdt)
    y = _mm(u, wi_ref[...])
    y = y + _mm_nt(sf_ref[0].astype(mdt), wt_ref[0])
    y = y + _mm_nt(sb_ref[0].astype(mdt), wt_ref[1])
    o_ref[0] = y + u.astype(F32) * skip_ref[...]


def _s5_read(u_pairs, sf, sb, pw_k, c_k, kall, skip, layer):
    a, nch, wd = u_pairs.shape
    blk = lambda *shape: pl.BlockSpec((1,) + shape, lambda i: (i, 0, 0))
    return pl.pallas_call(
        _s5_read_kernel,
        grid=(a,),
        in_specs=[blk(nch, wd), blk(nch, wd), blk(nch, wd), _pair_spec(layer, *pw_k.shape[2:]),
                  _pair_spec(layer, *c_k.shape[2:]), _pair_spec(layer, *kall.shape[2:]),
                  _pair_spec(layer, *skip.shape[2:])],
        out_specs=blk(nch, wd),
        out_shape=jax.ShapeDtypeStruct((a, nch, wd), F32),
        scratch_shapes=[pltpu.VMEM((2, wd, wd), u_pairs.dtype), pltpu.VMEM((wd, wd), u_pairs.dtype)],
        compiler_params=_params(("arbitrary",)),
        name="s5_read",
    )(u_pairs, sf, sb, pw_k, c_k, kall, skip)


def _s5_mixer(u_pairs, weights, layer):
    pw_k, b_k, c_k, kall, lam_t, skip = weights
    drive = _s5_drive(u_pairs, pw_k, b_k, layer)
    sf, sb = _s5_scan(drive, lam_t, layer)
    return _s5_read(u_pairs, sf, sb, pw_k, c_k, kall, skip, layer)


def _layer_norm(x, g, b):
    mu = jnp.mean(x, axis=-1, keepdims=True)
    d = x - mu
    var = jnp.mean(d * d, axis=-1, keepdims=True)
    return d * lax.rsqrt(var + LN_EPS) * g + b


def _post_kernel(x_ref, att_ref, ret_ref, s5_ref, mod_ref, permt_ref, wglu_ref, bglu_ref, woa_ref, wor_ref, wos_ref,
                 g1_ref, b1_ref, w1_ref, w2_ref, g2_ref, b2_ref, o_ref):
    mdt = w1_ref.dtype
    sub = x_ref.shape[0] // POST_SPLIT
    csub = sub // S5_T
    nff = D_FF // FF_CHUNK

    def mix(part):
        rows = slice(sub * part, sub * (part + 1))
        zrows = []
        for i in range(S5_T):
            src_vreg, src_blk = divmod(i * S5_PAIR_W, LANES)
            src_blk //= S5_PAIR_W
            cols = [_lane_block_shuffle(
                lambda a: s5_ref[a, csub * part:csub * (part + 1), src_vreg * LANES:(src_vreg + 1) * LANES],
                src_blk, w) for w in range(S5_WIDTH // LANES)]
            zrows.append(jnp.concatenate(cols, axis=1))
        z = jnp.concatenate(zrows, axis=0)
        z_hi = z.astype(mdt)
        z_lo = (z - z_hi.astype(F32)).astype(mdt)
        ys = _mm(permt_ref[...], z_hi) + _mm(permt_ref[...], z_lo)
        hs = jax.nn.gelu(ys)
        gate = jax.nn.sigmoid(_mm(hs.astype(mdt), wglu_ref[...]) + bglu_ref[...])
        s5 = (hs * gate).astype(mdt)
        return _mm(att_ref[rows, :], woa_ref[...]) + _mm(ret_ref[rows, :], wor_ref[...]) + _mm(s5, wos_ref[...])

    def norm1(part, ox):
        rows = slice(sub * part, sub * (part + 1))
        x1 = _layer_norm(DEEPNORM_ALPHA * x_ref[rows, :] + mod_ref[2:3, :] * ox, g1_ref[...], b1_ref[...])
        return x1, (x1 * (1.0 + mod_ref[4:5, :]) + mod_ref[3:4, :]).astype(mdt)

    def ff(h, c):
        a = _mm(h, w1_ref[:, FF_CHUNK * c:FF_CHUNK * (c + 1)])
        a = jnp.square(jnp.maximum(a, 0.0)).astype(mdt)
        return _mm(a, w2_ref[FF_CHUNK * c:FF_CHUNK * (c + 1), :])

    def norm2(part, x1, acc):
        rows = slice(sub * part, sub * (part + 1))
        o_ref[rows, :] = _layer_norm(DEEPNORM_ALPHA * x1 + mod_ref[5:6, :] * acc, g2_ref[...], b2_ref[...])

    nstage = nff + 3
    state = [dict() for _ in range(POST_SPLIT)]
    for part, stage in POST_PROGRAM:
        st = state[part]
        if stage == 0:
            st["ox"] = mix(part)
        elif stage == 1:
            st["x1"], st["h"] = norm1(part, st.pop("ox"))
        elif stage < nstage - 1:
            term = ff(st["h"], stage - 2)
            st["acc"] = term if stage == 2 else st["acc"] + term
        else:
            norm2(part, st["x1"], st["acc"])


def _post(stream, att, ret, s5_pairs, mods, layer, permt, wglu, bglu, wo, g1, b1, w1, w2, g2, b2, skip_context):
    rows = stream.shape[0]
    tm = ROW_TILE
    off = CTX_PAD // tm if skip_context else 0
    row_blk = lambda width: pl.BlockSpec((tm, width), lambda i: (i + off, 0))
    full = lambda arr: pl.BlockSpec(arr.shape, lambda i: (0,) * arr.ndim)
    woa, wor, wos = wo[:ATT_WIDTH], wo[ATT_WIDTH:ATT_WIDTH + RET_WIDTH], wo[ATT_WIDTH + RET_WIDTH:]
    vec = lambda v: v.reshape(1, -1).astype(F32)
    small = [permt, wglu, vec(bglu), woa, wor, wos, vec(g1), vec(b1), w1, w2, vec(g2), vec(b2)]
    return pl.pallas_call(
        _post_kernel,
        grid=(rows // tm - off,),
        in_specs=[row_blk(D_MODEL), row_blk(ATT_WIDTH), row_blk(RET_WIDTH),
                  pl.BlockSpec((S5_PAIRS, tm // S5_T, S5_T * S5_PAIR_W), lambda i: (0, i + off, 0)),
                  pl.BlockSpec((None, None, N_ADA, D_MODEL),
                               lambda i: (layer, jnp.where(i + off == 0, 1, 0), 0, 0))]
                 + [full(arr) for arr in small],
        out_specs=pl.BlockSpec((tm, D_MODEL), lambda i: (i, 0)),
        out_shape=jax.ShapeDtypeStruct((rows - off * tm, D_MODEL), F32),
        compiler_params=_params(("arbitrary",)),
        name="post",
    )(stream, att, ret, s5_pairs, mods, *small)


def kernel(x, c, ctx, c_ctx, w_ada, b_ada, w_in, att_sink, ret_decay_logit, s5_lambda_re, s5_lambda_im, s5_b_re,
           s5_b_im, s5_c_re, s5_c_im, s5_log_dt, s5_d, w_glu, b_glu, w_out, ln1_g, ln1_b, w_ff1, w_ff2, ln2_g,
           ln2_b):
    assert x.shape[0] == 1 and x.shape[2] == D_MODEL and ctx.shape[1] == CTX_LEN
    seq = x.shape[1]
    assert seq % ROW_TILE == 0
    stream = jnp.concatenate([ctx[0], jnp.zeros((CTX_PAD - CTX_LEN, D_MODEL), F32), x[0]], axis=0)
    cond = jnp.zeros((8, D_MODEL), F32).at[0].set(c[0]).at[1].set(c_ctx)
    mods = _modulation(cond, w_ada, b_ada).reshape(DEPTH, 8, N_ADA, D_MODEL)
    tabs = _rope_tables(seq)
    masks = _attention_masks()
    perm = _chunk_perm(ROW_TILE // POST_SPLIT, MXU_DTYPE)
    permt = perm.T
    col_scale = jnp.ones((IN_WIDTH,), F32).at[COL_AQ:COL_AK].set(HEAD_DIM ** -0.5 * LOG2E)
    col_scale = col_scale.at[COL_RQ:COL_RK].set(HEAD_DIM ** -0.5)
    s5w = jax.vmap(_s5_weights)(s5_lambda_re, s5_lambda_im, s5_b_re, s5_b_im, s5_c_re, s5_c_im, s5_log_dt, s5_d)
    log_gamma = jax.nn.log_sigmoid(ret_decay_logit.astype(F32))
    for l in range(DEPTH):
        proj, u_pairs = _in_proj(stream, mods, l, (w_in[l] * col_scale).astype(MXU_DTYPE), tabs, perm)
        att = _attention(proj, att_sink[l].astype(F32) * LOG2E, masks)
        ret = _retention(proj, log_gamma[l])
        s5 = _s5_mixer(u_pairs, s5w, l)
        stream = _post(stream, att, ret, s5, mods, l, permt, w_glu[l].astype(MXU_DTYPE), b_glu[l],
                       w_out[l].astype(MXU_DTYPE), ln1_g[l], ln1_b[l], w_ff1[l].astype(MXU_DTYPE),
                       w_ff2[l].astype(MXU_DTYPE), ln2_g[l], ln2_b[l], skip_context=(l == DEPTH - 1))
    return stream[None]
```

```python
import functools
import math

import jax
import jax.numpy as jnp
from jax import lax
from jax.experimental import pallas as pl
from jax.experimental.pallas import tpu as pltpu

F32 = jnp.float32
MXU_DTYPE = jnp.bfloat16

D_MODEL = 1024
DEPTH = 4
GRID_W = 64
CTX_LEN = 256
CTX_PAD = 512
HEAD_DIM = 64
ATT_HEADS = 8
ATT_KV_HEADS = 2
ATT_BLOCK = 128
ATT_LOOKAHEAD = 2
ROPE_BASE = 10000.0
RET_HEADS = 4
RET_CHUNK = 256
S5_CH = 16
S5_GROUPS = 16
S5_STATE = 64
S5_T = 16
S5_PAIRS = S5_GROUPS // 2
S5_PAIR_W = 2 * S5_CH
S5_TILE = CTX_PAD // S5_T
SCAN_ROWS = 8
S5_STATE_W = 2 * 2 * S5_STATE
ATT_WIDTH = ATT_HEADS * HEAD_DIM
KV_WIDTH = ATT_KV_HEADS * HEAD_DIM
RET_WIDTH = RET_HEADS * HEAD_DIM
S5_WIDTH = S5_GROUPS * S5_CH
IN_WIDTH = ATT_WIDTH + 2 * KV_WIDTH + 4 * RET_WIDTH + S5_WIDTH
D_FF = 4 * D_MODEL
FF_CHUNK = 1024
N_ADA = 6
LN_EPS = 1e-5
GN_EPS = 1e-5
DEEPNORM_ALPHA = (2 * DEPTH) ** 0.25
ROW_TILE = 512
POST_SPLIT = 2
POST_PROGRAM = ((0, 0), (0, 1), (1, 0), (0, 2), (1, 1), (0, 3), (0, 4), (0, 5), (1, 2), (0, 6), (1, 3), (1, 4),
                (1, 5), (1, 6))
NEG_BIG = -1e30
LOG2E = math.log2(math.e)
LANES = 128
VMEM_LIMIT = 56 * 1024 * 1024

COL_AQ, COL_AK, COL_AV = 0, ATT_WIDTH, ATT_WIDTH + KV_WIDTH
COL_RQ = ATT_WIDTH + 2 * KV_WIDTH
COL_RK, COL_RV, COL_RG = COL_RQ + RET_WIDTH, COL_RQ + 2 * RET_WIDTH, COL_RQ + 3 * RET_WIDTH
COL_S5 = COL_RQ + 4 * RET_WIDTH


def _mm(a, b):
    return jnp.dot(a, b, preferred_element_type=F32)


def _mm_nt(a, b):
    return lax.dot_general(a, b, (((1,), (1,)), ((), ())), preferred_element_type=F32)


def _mm_tn(a, b):
    return lax.dot_general(a, b, (((0,), (0,)), ((), ())), preferred_element_type=F32)


def _params(sem):
    return pltpu.CompilerParams(dimension_semantics=sem, vmem_limit_bytes=VMEM_LIMIT)


def _mod_kernel(cond_ref, w_ref, b_ref, o_ref):
    c = cond_ref[...]
    s = c * jax.nn.sigmoid(c)
    o_ref[0] = jnp.dot(s, w_ref[0], preferred_element_type=F32, precision=lax.Precision.HIGHEST) + b_ref[0]


def _modulation(cond, w_ada, b_ada):
    tn = 1536
    n = N_ADA * D_MODEL
    return pl.pallas_call(
        _mod_kernel,
        grid=(DEPTH, n // tn),
        in_specs=[
            pl.BlockSpec((8, D_MODEL), lambda l, j: (0, 0)),
            pl.BlockSpec((1, D_MODEL, tn), lambda l, j: (l, 0, j)),
            pl.BlockSpec((1, 1, tn), lambda l, j: (l, 0, j)),
        ],
        out_specs=pl.BlockSpec((1, 8, tn), lambda l, j: (l, 0, j)),
        out_shape=jax.ShapeDtypeStruct((DEPTH, 8, n), F32),
        compiler_params=_params(("arbitrary", "arbitrary")),
        name="modulation",
    )(cond, w_ada, b_ada.reshape(DEPTH, 1, n))


def _lane_block_shuffle(src_rows, src_lane_blk, out_vreg):
    acc = None
    for q in range(LANES // S5_PAIR_W):
        piece = src_rows(out_vreg * (LANES // S5_PAIR_W) + q)
        shift = (S5_PAIR_W * (q - src_lane_blk)) % LANES
        if shift:
            piece = pltpu.roll(piece, shift, 1)
        if acc is None:
            acc = piece
        else:
            lane_blk = lax.broadcasted_iota(jnp.int32, piece.shape, 1) // S5_PAIR_W
            acc = jnp.where(lane_blk == q, piece, acc)
    return acc


def _in_proj_kernel(x_ref, mod_ref, w_ref, ca_ref, sa_ref, cr_ref, sr_ref, perm_ref, o_ref, u_ref):
    x = x_ref[...]
    h = (x * (1.0 + mod_ref[1:2, :]) + mod_ref[0:1, :]).astype(w_ref.dtype)
    lane = lax.broadcasted_iota(jnp.int32, (x.shape[0], LANES), 1)
    first_att = (lane & 31) < 16
    first_ret = (lane & 63) < 32

    def proj(c0, c1):
        return _mm(h, w_ref[:, c0:c1])

    def rope_store(c0, width, cos, sin, first, half):
        p = proj(c0, c0 + width)
        for b in range(width // LANES):
            blk = p[:, LANES * b:LANES * (b + 1)]
            rot = jnp.where(first, pltpu.roll(blk, LANES - half, 1), pltpu.roll(blk, half, 1))
            o_ref[:, c0 + LANES * b:c0 + LANES * (b + 1)] = (blk * cos + rot * sin).astype(o_ref.dtype)

    u = proj(COL_S5, IN_WIDTH).astype(w_ref.dtype)
    sub = perm_ref.shape[0]
    nchunk = sub // S5_T
    for part in range(x.shape[0] // sub):
        g = _mm(perm_ref[...], u[sub * part:sub * (part + 1)])
        for a in range(S5_PAIRS):
            vreg_col, lane_blk = divmod(a * S5_PAIR_W, LANES)
            lane_blk //= S5_PAIR_W
            for v in range(S5_T * S5_PAIR_W // LANES):
                slab = _lane_block_shuffle(
                    lambda j: g[nchunk * j:nchunk * (j + 1), vreg_col * LANES:(vreg_col + 1) * LANES], lane_blk, v)
                u_ref[a, nchunk * part:nchunk * (part + 1), LANES * v:LANES * (v + 1)] = slab.astype(u_ref.dtype)

    ca, sa, cr, sr = ca_ref[...], sa_ref[...], cr_ref[...], sr_ref[...]
    rope_store(COL_AQ, ATT_WIDTH, ca, sa, first_att, 16)
    rope_store(COL_AK, KV_WIDTH, ca, sa, first_att, 16)
    rope_store(COL_RQ, RET_WIDTH, cr, sr, first_ret, 32)
    rope_store(COL_RK, RET_WIDTH, cr, sr, first_ret, 32)
    o_ref[:, COL_AV:COL_RQ] = proj(COL_AV, COL_RQ).astype(o_ref.dtype)
    o_ref[:, COL_RV:COL_S5] = proj(COL_RV, COL_S5).astype(o_ref.dtype)


def _chunk_perm(tile_rows, dtype):
    nchunk = tile_rows // S5_T
    r = jnp.arange(tile_rows)
    src = S5_T * (r % nchunk) + r // nchunk
    return (src[:, None] == jnp.arange(tile_rows)[None, :]).astype(dtype)


def _in_proj(stream, mods, layer, w_in, tabs, perm):
    rows = stream.shape[0]
    tm = ROW_TILE
    tab_spec = pl.BlockSpec((tm, LANES), lambda i: (i, 0))
    nch = rows // S5_T
    return pl.pallas_call(
        _in_proj_kernel,
        grid=(rows // tm,),
        in_specs=[
            pl.BlockSpec((tm, D_MODEL), lambda i: (i, 0)),
            pl.BlockSpec((None, None, N_ADA, D_MODEL), lambda i: (layer, jnp.where(i == 0, 1, 0), 0, 0)),
            pl.BlockSpec((D_MODEL, IN_WIDTH), lambda i: (0, 0)),
            tab_spec, tab_spec, tab_spec, tab_spec,
            pl.BlockSpec(perm.shape, lambda i: (0, 0)),
        ],
        out_specs=[pl.BlockSpec((tm, COL_S5), lambda i: (i, 0)),
                   pl.BlockSpec((S5_PAIRS, tm // S5_T, S5_T * S5_PAIR_W), lambda i: (0, i, 0))],
        out_shape=[jax.ShapeDtypeStruct((rows, COL_S5), MXU_DTYPE),
                   jax.ShapeDtypeStruct((S5_PAIRS, nch, S5_T * S5_PAIR_W), MXU_DTYPE)],
        compiler_params=_params(("arbitrary",)),
        name="in_proj",
    )(stream, mods, w_in, *tabs, perm)


def _rope_tables(seq):
    half_a = HEAD_DIM // 4
    half_r = HEAD_DIM // 2
    nrow = seq // GRID_W
    inv_a = ROPE_BASE ** (-jnp.arange(half_a, dtype=F32) / half_a)
    inv_r = ROPE_BASE ** (-jnp.arange(half_r, dtype=F32) / half_r)
    ang_r = jnp.arange(nrow, dtype=F32)[:, None] * inv_a[None, :]
    ang_c = jnp.arange(GRID_W, dtype=F32)[:, None] * inv_a[None, :]
    ang_t = jnp.arange(seq, dtype=F32)[:, None] * inv_r[None, :]
    hp = lax.Precision.HIGHEST
    lane = jnp.arange(LANES)
    within = lane % HEAD_DIM
    pick_a = (within % half_a)[None, :] == jnp.arange(half_a)[:, None]
    exp_row = (pick_a & (within < 2 * half_a)[None, :]).astype(F32)
    exp_col = (pick_a & (within >= 2 * half_a)[None, :]).astype(F32)
    exp_t = ((lane % half_r)[None, :] == jnp.arange(half_r)[:, None]).astype(F32)
    sign_a = jnp.where(within % (2 * half_a) < half_a, -1.0, 1.0).astype(F32)
    sign_r = jnp.where(within < half_r, -1.0, 1.0).astype(F32)

    def att_table(fn):
        by_row = jnp.dot(fn(ang_r), exp_row, precision=hp)
        by_col = jnp.dot(fn(ang_c), exp_col, precision=hp)
        return (by_row[:, None, :] + by_col[None, :, :]).reshape(seq, LANES)

    cos_a = att_table(jnp.cos)
    sin_a = att_table(jnp.sin) * sign_a
    cos_r = jnp.dot(jnp.cos(ang_t), exp_t, precision=hp)
    sin_r = jnp.dot(jnp.sin(ang_t), exp_t, precision=hp) * sign_r
    pad = lambda tab, ident: jnp.pad(tab, ((CTX_PAD, 0), (0, 0)), constant_values=ident)
    return pad(cos_a, 1.0), pad(sin_a, 0.0), pad(cos_r, 1.0), pad(sin_r, 0.0)


def _swap_halves(x):
    if x.dtype.itemsize == 4:
        return pltpu.roll(x, 64, 1)
    packed = pltpu.bitcast(x, jnp.uint32)
    return pltpu.bitcast(pltpu.roll(packed, 64, 1), x.dtype)


def _dup_heads(x):
    sw = _swap_halves(x)
    lo = lax.broadcasted_iota(jnp.int32, x.shape, 1) < HEAD_DIM
    return jnp.where(lo, x, sw), jnp.where(lo, sw, x)


def _attn_kernel(sink_ref, q_ref, km_ref, kp_ref, kn_ref, vm_ref, vp_ref, vn_ref, kc_ref, vc_ref, mask_ref,
                 o_ref, k2_ref, v2_ref, kc2_ref, vc2_ref):
    i = pl.program_id(0)
    last_blk = pl.num_programs(0) * (ROW_TILE // ATT_BLOCK) - 1
    blk = ATT_BLOCK
    def spread(src, ones_upper):
        x = src[...]
        a, b = _dup_heads(x)
        if ones_upper:
            upper = lax.broadcasted_iota(jnp.int32, x.shape, 1) >= HEAD_DIM
            a = jnp.where(upper, jnp.ones_like(a), a)
            b = jnp.where(upper, jnp.ones_like(b), b)
        return a, b

    for dst, parts, is_v in ((k2_ref, (kp_ref, km_ref, kn_ref), False), (v2_ref, (vp_ref, vm_ref, vn_ref), True)):
        row = 0
        for part in parts:
            a, b = spread(part, is_v)
            n = part.shape[0]
            dst[0, row:row + n, :] = a
            dst[1, row:row + n, :] = b
            row += n
    for dst, src, is_v in ((kc2_ref, kc_ref, False), (vc2_ref, vc_ref, True)):
        a, b = spread(src, is_v)
        dst[0] = a
        dst[1] = b

    lo = lax.broadcasted_iota(jnp.int32, (blk, LANES), 1) < HEAD_DIM
    group = ATT_HEADS // ATT_KV_HEADS

    nloc = 3 * blk

    def scores(j, kv):
        r0 = j * blk
        qt = q_ref[r0:r0 + blk, group * HEAD_DIM * kv:group * HEAD_DIM * (kv + 1)]
        parts = []
        for g in range(group):
            qc = qt[:, LANES * (g // 2):LANES * (g // 2 + 1)]
            keep = lo if g % 2 == 0 else jnp.logical_not(lo)
            parts.append(jnp.where(keep, qc, jnp.zeros_like(qc)))
        qs = jnp.concatenate(parts, axis=0)
        return _mm_nt(qs, k2_ref[kv, r0:r0 + nloc, :]), _mm_nt(qs, kc2_ref[kv])

    def finish(j, kv, s_loc, s_ctx):
        r0 = j * blk
        gblk = i * (ROW_TILE // blk) + j
        sel = jnp.where(i == 0, 3, jnp.where(gblk == CTX_PAD // blk, 0, jnp.where(gblk == last_blk, 2, 1)))
        bias = mask_ref[sel]
        probs, sink_w = [], []
        for g in range(group):
            s = jnp.concatenate([s_loc[blk * g:blk * (g + 1)] + bias, s_ctx[blk * g:blk * (g + 1)]], axis=1)
            sk = sink_ref[group * kv + g]
            m = jnp.maximum(jnp.max(s, axis=-1, keepdims=True), sk)
            probs.append(jnp.exp2(s - m).astype(o_ref.dtype))
            sink_w.append(jnp.exp2(sk - m))
        p = jnp.concatenate(probs, axis=0)
        o = _mm(p[:, :nloc], v2_ref[kv, r0:r0 + nloc, :]) + _mm(p[:, nloc:], vc2_ref[kv])
        for half in range(group // 2):
            even, odd = 2 * half, 2 * half + 1
            oe = o[blk * even:blk * (even + 1)]
            oo = o[blk * odd:blk * (odd + 1)]
            y_even = oe * (1.0 / (pltpu.roll(oe, HEAD_DIM, 1) + sink_w[even]))
            y_odd = pltpu.roll(oo, HEAD_DIM, 1) * (1.0 / (oo + sink_w[odd]))
            c0 = group * HEAD_DIM * kv + LANES * half
            o_ref[r0:r0 + blk, c0:c0 + LANES] = jnp.where(lo, y_even, y_odd).astype(o_ref.dtype)

    items = [(j, kv) for j in range(ROW_TILE // blk) for kv in range(ATT_KV_HEADS)]
    pending = {}
    for t in range(len(items) + ATT_LOOKAHEAD):
        if t < len(items):
            pending[t] = scores(*items[t])
        if t >= ATT_LOOKAHEAD:
            finish(*items[t - ATT_LOOKAHEAD], *pending.pop(t - ATT_LOOKAHEAD))


def _attention_masks():
    qi = jnp.arange(ATT_BLOCK)[:, None]
    kj = jnp.arange(3 * ATT_BLOCK)[None, :]
    band = jnp.abs(kj - ATT_BLOCK - qi) <= ATT_BLOCK
    first = band & (kj >= ATT_BLOCK)
    last = band & (kj < 2 * ATT_BLOCK)
    none = jnp.zeros_like(band)
    masks = jnp.stack([first, band, last, none])
    return jnp.where(masks, 0.0, NEG_BIG).astype(F32)


def _attention(proj, sink, masks):
    rows = proj.shape[0]
    tm, blk = ROW_TILE, ATT_BLOCK
    per = tm // blk
    nblk = rows // blk
    ck, cv = COL_AK // KV_WIDTH, COL_AV // KV_WIDTH
    dt = proj.dtype
    return pl.pallas_call(
        _attn_kernel,
        grid=(rows // tm,),
        in_specs=[
            pl.BlockSpec(memory_space=pltpu.SMEM),
            pl.BlockSpec((tm, ATT_WIDTH), lambda i: (i, 0)),
            pl.BlockSpec((tm, KV_WIDTH), lambda i: (i, ck)),
            pl.BlockSpec((blk, KV_WIDTH), lambda i: (jnp.maximum(i * per - 1, 0), ck)),
            pl.BlockSpec((blk, KV_WIDTH), lambda i: (jnp.minimum((i + 1) * per, nblk - 1), ck)),
            pl.BlockSpec((tm, KV_WIDTH), lambda i: (i, cv)),
            pl.BlockSpec((blk, KV_WIDTH), lambda i: (jnp.maximum(i * per - 1, 0), cv)),
            pl.BlockSpec((blk, KV_WIDTH), lambda i: (jnp.minimum((i + 1) * per, nblk - 1), cv)),
            pl.BlockSpec((CTX_LEN, KV_WIDTH), lambda i: (0, ck)),
            pl.BlockSpec((CTX_LEN, KV_WIDTH), lambda i: (0, cv)),
            pl.BlockSpec((4, blk, 3 * blk), lambda i: (0, 0, 0)),
        ],
        out_specs=pl.BlockSpec((tm, ATT_WIDTH), lambda i: (i, 0)),
        out_shape=jax.ShapeDtypeStruct((rows, ATT_WIDTH), dt),
        scratch_shapes=[
            pltpu.VMEM((2, tm + 2 * blk, KV_WIDTH), dt),
            pltpu.VMEM((2, tm + 2 * blk, KV_WIDTH), dt),
            pltpu.VMEM((2, CTX_LEN, KV_WIDTH), dt),
            pltpu.VMEM((2, CTX_LEN, KV_WIDTH), dt),
        ],
        compiler_params=_params(("arbitrary",)),
        name="attention",
    )(sink, proj, proj, proj, proj, proj, proj, proj, proj, proj, masks)


def _ret_kernel(lg_ref, q_ref, k_ref, v_ref, g_ref, o_ref, sb_ref, s_ref, dm_ref, tab_ref, gbd_ref):
    ph = pl.program_id(0)
    t = pl.program_id(1)
    ntile = pl.num_programs(1)
    c = RET_CHUNK
    w = RET_WIDTH
    per = q_ref.shape[0] // c
    mdt = sb_ref.dtype
    rows = lambda ci: slice(c * ci, c * (ci + 1))

    def lane_vec(direction, shape, axis):
        head = lax.broadcasted_iota(jnp.int32, shape, axis) // HEAD_DIM
        out = jnp.full(shape, lg_ref[direction, RET_HEADS - 1], F32)
        for h in range(RET_HEADS - 2, -1, -1):
            out = jnp.where(head == h, lg_ref[direction, h], out)
        return out

    @pl.when(jnp.logical_and(ph == 0, t == 0))
    def _init_tables():
        diff = (lax.broadcasted_iota(jnp.int32, (c, c), 0) - lax.broadcasted_iota(jnp.int32, (c, c), 1)).astype(F32)
        for h in range(RET_HEADS):
            dm_ref[h] = jnp.exp(jnp.where(diff >= 0, diff * lg_ref[0, h], -diff * lg_ref[1, h]))
        pos = lax.broadcasted_iota(jnp.int32, (c, w), 0).astype(F32)
        lgf = lane_vec(0, (c, w), 1)
        lgb = lane_vec(1, (c, w), 1)
        tab_ref[0] = jnp.exp((c - 1.0 - pos) * lgf)
        tab_ref[1] = jnp.exp((pos + 1.0) * lgf)
        tab_ref[2] = jnp.exp(pos * lgb)
        tab_ref[3] = jnp.exp((c - pos) * lgb)
        same = (lax.broadcasted_iota(jnp.int32, (w, w), 0) // HEAD_DIM
                == lax.broadcasted_iota(jnp.int32, (w, w), 1) // HEAD_DIM)
        bd = jnp.where(same, 1.0, 0.0)
        gbd_ref[0] = bd * jnp.exp(c * lane_vec(0, (w, w), 0))
        gbd_ref[1] = bd * jnp.exp(c * lane_vec(1, (w, w), 0))
        gbd_ref[2] = bd

    @pl.when(t == 0)
    def _reset_state():
        s_ref[...] = jnp.zeros_like(s_ref)

    def state_update(direction, key_tab, ci):
        kw = (k_ref[rows(ci), :].astype(F32) * tab_ref[key_tab]).astype(mdt)
        u = _mm_tn(kw, v_ref[rows(ci), :])
        s_ref[...] = gbd_ref[direction] * s_ref[...] + gbd_ref[2] * u

    @pl.when(jnp.logical_and(ph == 0, t == 0))
    def _backward_context():
        sb_ref[0] = s_ref[...].astype(mdt)
        state_update(1, 2, 0)

    @pl.when(jnp.logical_and(ph == 0, t > 0))
    def _backward_latent():
        base = 1 + per * (ntile - t - 1)
        for ci in range(per - 1, -1, -1):
            sb_ref[base + ci] = s_ref[...].astype(mdt)
            state_update(1, 2, ci)

    head = lax.broadcasted_iota(jnp.int32, (c, w), 1) // HEAD_DIM

    def scores(ci):
        q = q_ref[rows(ci), :]
        qs = jnp.concatenate([jnp.where(head == h, q, jnp.zeros_like(q)) for h in range(RET_HEADS)], axis=0)
        return _mm_nt(qs, k_ref[rows(ci), :])

    def intra(ci, sc):
        scd = jnp.concatenate([sc[c * h:c * (h + 1)] * dm_ref[h] for h in range(RET_HEADS)], axis=0).astype(mdt)
        oi = _mm(scd, v_ref[rows(ci), :])
        o = jnp.where(head == 0, oi[0:c], 0.0)
        for h in range(1, RET_HEADS):
            o = o + jnp.where(head == h, oi[c * h:c * (h + 1)], 0.0)
        return o

    def cross(ci, idx):
        qf = q_ref[rows(ci), :].astype(F32)
        return (_mm((qf * tab_ref[1]).astype(mdt), s_ref[...].astype(mdt))
                + _mm((qf * tab_ref[3]).astype(mdt), sb_ref[idx]))

    def finish(ci, o):
        avg = (gbd_ref[2] * (1.0 / HEAD_DIM)).astype(mdt)

        def head_mean(val):
            hi = val.astype(mdt)
            lo = (val - hi.astype(F32)).astype(mdt)
            return _mm(hi, avg) + _mm(lo, avg)

        d = o - head_mean(o)
        var = head_mean(d * d)
        gate = g_ref[rows(ci), :].astype(F32)
        y = d * lax.rsqrt(var + GN_EPS) * (gate * jax.nn.sigmoid(gate))
        o_ref[rows(ci), :] = y.astype(o_ref.dtype)

    @pl.when(jnp.logical_and(ph == 1, t == 0))
    def _forward_context():
        o = intra(0, scores(0)) + cross(0, 0)
        state_update(0, 0, 0)
        finish(0, o)
        for ci in range(1, per):
            o_ref[rows(ci), :] = jnp.zeros((c, w), o_ref.dtype)

    @pl.when(jnp.logical_and(ph == 1, t > 0))
    def _forward_latent():
        base = 1 + per * (t - 1)
        sc = [scores(ci) for ci in range(per)]
        outs = [intra(ci, sc[ci]) for ci in range(per)]
        for ci in range(per):
            outs[ci] = outs[ci] + cross(ci, base + ci)
            state_update(0, 0, ci)
        for ci in range(per):
            finish(ci, outs[ci])


def _retention(proj, log_gamma):
    rows = proj.shape[0]
    c = RET_CHUNK
    tm = ROW_TILE
    ntile = rows // tm
    nchunk = 1 + (rows - CTX_PAD) // c
    dt = proj.dtype
    cq, ckk, cvv, cg = (COL_RQ // RET_WIDTH, COL_RK // RET_WIDTH, COL_RV // RET_WIDTH, COL_RG // RET_WIDTH)

    def kv_blk(ph, t):
        return jnp.where(ph == 0, jnp.where(t == 0, 0, ntile - t), t)

    def fw_blk(ph, t):
        return jnp.where(ph == 0, 0, t)

    return pl.pallas_call(
        _ret_kernel,
        grid=(2, ntile),
        in_specs=[
            pl.BlockSpec(memory_space=pltpu.SMEM),
            pl.BlockSpec((tm, RET_WIDTH), lambda ph, t: (fw_blk(ph, t), cq)),
            pl.BlockSpec((tm, RET_WIDTH), lambda ph, t: (kv_blk(ph, t), ckk)),
            pl.BlockSpec((tm, RET_WIDTH), lambda ph, t: (kv_blk(ph, t), cvv)),
            pl.BlockSpec((tm, RET_WIDTH), lambda ph, t: (fw_blk(ph, t), cg)),
        ],
        out_specs=pl.BlockSpec((tm, RET_WIDTH), lambda ph, t: (fw_blk(ph, t), 0)),
        out_shape=jax.ShapeDtypeStruct((rows, RET_WIDTH), dt),
        scratch_shapes=[
            pltpu.VMEM((nchunk, RET_WIDTH, RET_WIDTH), dt),
            pltpu.VMEM((RET_WIDTH, RET_WIDTH), F32),
            pltpu.VMEM((RET_HEADS, c, c), F32),
            pltpu.VMEM((4, c, RET_WIDTH), F32),
            pltpu.VMEM((3, RET_WIDTH, RET_WIDTH), F32),
        ],
        compiler_params=_params(("arbitrary", "arbitrary")),
        name="retention",
    )(log_gamma, proj, proj, proj, proj)


def _s5_weights(lam_re, lam_im, b_re, b_im, c_re, c_im, log_dt, d_skip):
    hp = lax.Precision.HIGHEST
    tt, g, n, p, a = S5_T, S5_GROUPS, S5_STATE, S5_CH, S5_PAIRS
    lam = lax.complex(lam_re.astype(F32), lam_im.astype(F32))
    dtv = jnp.exp(log_dt.astype(F32))[..., None]
    lam_bar = jnp.exp(lam * dtv)
    bbar = ((lam_bar - 1.0) / lam)[..., None] * lax.complex(b_re.astype(F32), b_im.astype(F32))
    cmat = lax.complex(c_re.astype(F32), c_im.astype(F32))
    pw = [jnp.ones_like(lam_bar)]
    for _ in range(tt):
        pw.append(pw[-1] * lam_bar)
    pw = jnp.stack(pw, axis=1)
    eye2 = jnp.eye(2, dtype=F32)
    ri = lambda z, axis: jnp.stack([jnp.real(z), jnp.imag(z)], axis=axis)

    pw_l = pw.reshape(2, tt + 1, a, 2 * n)
    bbt = jnp.einsum('dahpn,gh->dagphn', jnp.swapaxes(bbar, -1, -2).reshape(2, a, 2, p, n), eye2)
    bbt = bbt.reshape(2, a, 2 * p, 2 * n)
    cct = jnp.einsum('dahpn,gh->dagphn', cmat.reshape(2, a, 2, p, n), eye2).reshape(2, a, 2 * p, 2 * n)
    x = bbt[:, None] * pw_l[:, :tt, :, None, :]
    k = jnp.real(jnp.einsum('dlaxk,dayk->dlaxy', x, cct, precision=hp))
    lags = jnp.arange(-(tt - 1), tt)
    kl = (jnp.where((lags >= 0)[:, None, None, None], k[0][jnp.clip(lags, 0, tt - 1)], 0.0)
          + jnp.where((lags <= 0)[:, None, None, None], k[1][jnp.clip(-lags, 0, tt - 1)], 0.0))
    kall = kl.transpose(1, 2, 0, 3).reshape(a, 2 * p, (2 * tt - 1) * 2 * p)
    kall = jnp.pad(kall, ((0, 0), (0, 0), (0, 2 * p)))

    pw_k = ri(pw_l, 1).transpose(3, 0, 1, 2, 4)
    b_k = ri(bbt, 2).transpose(1, 0, 2, 3, 4)
    c_k = ri(cct, 2).transpose(1, 0, 2, 3, 4)
    decay = [pw_l[:, tt]]
    for _ in range(SCAN_ROWS - 1):
        decay.append(decay[-1] * decay[0])
    decay = jnp.stack(decay, axis=0)
    rows8 = lambda z: jnp.broadcast_to(z[None], (SCAN_ROWS,) + z.shape)
    carry_w = jnp.stack([decay[:, 0], decay[::-1, 1]], axis=1)
    scan_tab = jnp.stack([rows8(decay[0]), rows8(decay[1]), rows8(decay[3]), carry_w], axis=0)
    scan_tab = ri(scan_tab, 0).transpose(4, 3, 1, 0, 2, 5)
    skip = jnp.tile(d_skip.astype(F32).reshape(a, 1, 2 * p), (1, tt, 1)).reshape(a, 1, tt * 2 * p)
    return pw_k, b_k, c_k, kall, scan_tab, skip


def _pair_spec(layer, *shape):
    return pl.BlockSpec((None, None) + shape, lambda i: (layer, i) + (0,) * len(shape))


def _s5_drive_kernel(u_ref, pw_ref, b_ref, o_ref, w_ref):
    rows, half = S5_PAIR_W, LANES
    for d in range(2):
        br, bi = b_ref[d, 0], b_ref[d, 1]
        for j in range(S5_T):
            e = S5_T - 1 - j if d == 0 else j
            pr, pi = pw_ref[d, 0, e:e + 1, :], pw_ref[d, 1, e:e + 1, :]
            w_ref[rows * j:rows * (j + 1), 2 * half * d:2 * half * d + half] = (pr * br - pi * bi).astype(w_ref.dtype)
            w_ref[rows * j:rows * (j + 1), 2 * half * d + half:2 * half * (d + 1)] = (
                pr * bi + pi * br).astype(w_ref.dtype)
    o_ref[0] = _mm(u_ref[0], w_ref[...])


def _s5_drive(u_pairs, pw_k, b_k, layer):
    a, nch, wd = u_pairs.shape
    return pl.pallas_call(
        _s5_drive_kernel,
        grid=(a,),
        in_specs=[pl.BlockSpec((1, nch, wd), lambda i: (i, 0, 0)),
                  _pair_spec(layer, *pw_k.shape[2:]), _pair_spec(layer, *b_k.shape[2:])],
        out_specs=pl.BlockSpec((1, nch, 2 * S5_STATE_W), lambda i: (i, 0, 0)),
        out_shape=jax.ShapeDtypeStruct((a, nch, 2 * S5_STATE_W), F32),
        scratch_shapes=[pltpu.VMEM((wd, 2 * S5_STATE_W), u_pairs.dtype)],
        compiler_params=_params(("arbitrary",)),
        name="s5_drive",
    )(u_pairs, pw_k, b_k)


def _s5_scan_kernel(ef_ref, eb_ref, tab_ref, sf_ref, sb_ref, st_ref):
    t = pl.program_id(0)
    hw = LANES
    sub = SCAN_ROWS
    npair = tab_ref.shape[0]
    row = lax.broadcasted_iota(jnp.int32, (sub, hw), 0)

    @pl.when(t == 0)
    def _reset():
        st_ref[...] = jnp.zeros_like(st_ref)

    def shift(x, k, reverse):
        if reverse:
            return jnp.where(row < sub - k, pltpu.roll(x, sub - k, 0), 0.0)
        return jnp.where(row >= k, pltpu.roll(x, k, 0), 0.0)

    def scan_group(e_ref, o_ref, a, d, r0, cr, ci):
        reverse = d == 1
        xr = e_ref[a, pl.ds(r0, sub), 0:hw]
        xi = e_ref[a, pl.ds(r0, sub), hw:2 * hw]
        for step, k in enumerate((1, 2, 4)):
            ar, ai = tab_ref[a, d, step, 0], tab_ref[a, d, step, 1]
            sr, si = shift(xr, k, reverse), shift(xi, k, reverse)
            xr, xi = xr + ar * sr - ai * si, xi + ar * si + ai * sr
        wr, wi = tab_ref[a, d, 3, 0], tab_ref[a, d, 3, 1]
        fr = xr + wr * cr - wi * ci
        fi = xi + wr * ci + wi * cr
        edge = sub - 1 if reverse else 0
        o_ref[a, pl.ds(r0, sub), 0:hw] = jnp.where(row == edge, cr, shift(fr, 1, reverse))
        o_ref[a, pl.ds(r0, sub), hw:2 * hw] = jnp.where(row == edge, ci, shift(fi, 1, reverse))
        last = 0 if reverse else sub - 1
        return (jnp.broadcast_to(fr[last:last + 1], (sub, hw)), jnp.broadcast_to(fi[last:last + 1], (sub, hw)))

    def run(nrows):
        ngroups = nrows // sub

        def body(gi, carry):
            r0 = pl.multiple_of(gi * sub, sub)
            rb0 = pl.multiple_of((ngroups - 1 - gi) * sub, sub)
            new = []
            for a in range(npair):
                fr, fi, br, bi = carry[a]
                fr, fi = scan_group(ef_ref, sf_ref, a, 0, r0, fr, fi)
                br, bi = scan_group(eb_ref, sb_ref, a, 1, rb0, br, bi)
                new.append((fr, fi, br, bi))
            return tuple(new)

        init = tuple(tuple(st_ref[a, k] for k in range(4)) for a in range(npair))
        final = lax.fori_loop(0, ngroups, body, init)
        for a in range(npair):
            for k in range(4):
                st_ref[a, k] = final[a][k]

    @pl.when(t == 0)
    def _context():
        sf_ref[...] = jnp.zeros_like(sf_ref)
        sb_ref[...] = jnp.zeros_like(sb_ref)
        run(CTX_LEN // S5_T)

    @pl.when(t > 0)
    def _latent():
        run(S5_TILE)


def _s5_scan(drive, scan_tab, layer):
    a, nch, wd2 = drive.shape
    wd = wd2 // 2
    nt = nch // S5_TILE

    def bwd(t):
        return jnp.where(t == 0, 0, nt - t)

    return pl.pallas_call(
        _s5_scan_kernel,
        grid=(nt,),
        in_specs=[pl.BlockSpec((a, S5_TILE, wd), lambda t: (0, t, 0)),
                  pl.BlockSpec((a, S5_TILE, wd), lambda t: (0, bwd(t), 1)),
                  pl.BlockSpec((None,) + scan_tab.shape[1:], lambda t: (layer,) + (0,) * (scan_tab.ndim - 1))],
        out_specs=[pl.BlockSpec((a, S5_TILE, wd), lambda t: (0, t, 0)),
                   pl.BlockSpec((a, S5_TILE, wd), lambda t: (0, bwd(t), 0))],
        out_shape=[jax.ShapeDtypeStruct((a, nch, wd), F32), jax.ShapeDtypeStruct((a, nch, wd), F32)],
        scratch_shapes=[pltpu.VMEM((a, 4, SCAN_ROWS, LANES), F32)],
        compiler_params=_params(("arbitrary",)),
        name="s5_scan",
    )(drive, drive, scan_tab)


def _lane_window(x, start, width):
    cols = []
    for v in range(width // LANES):
        k0, off = divmod(start + LANES * v, LANES)
        lo = x[:, LANES * k0:LANES * (k0 + 1)]
        if off:
            hi = x[:, LANES * (k0 + 1):LANES * (k0 + 2)]
            lane = lax.broadcasted_iota(jnp.int32, lo.shape, 1)
            lo = jnp.where(lane < LANES - off, pltpu.roll(lo, LANES - off, 1), pltpu.roll(hi, LANES - off, 1))
        cols.append(lo)
    return jnp.concatenate(cols, axis=1)


def _s5_read_kernel(u_ref, sf_ref, sb_ref, pw_ref, c_ref, kall_ref, skip_ref, o_ref, wt_ref, wi_ref):
    u = u_ref[0]
    mdt = u.dtype
    rows, half = S5_PAIR_W, LANES
    for d in range(2):
        cr, ci = c_ref[d, 0], c_ref[d, 1]
        for i in range(S5_T):
            e = i + 1 if d == 0 else S5_T - i
            pr, pi = pw_ref[d, 0, e:e + 1, :], pw_ref[d, 1, e:e + 1, :]
            wt_ref[d, rows * i:rows * (i + 1), 0:half] = (pr * cr - pi * ci).astype(mdt)
            wt_ref[d, rows * i:rows * (i + 1), half:2 * half] = (-(pr * ci + pi * cr)).astype(mdt)
    kall = kall_ref[...]
    for j in range(S5_T):
        wi_ref[rows * j:rows * (j + 1), :] = _lane_window(kall, rows * (S5_T - 1 - j), S5_T * rows).astype(mdt)
    y = _mm(u, wi_ref[...])
    y = y + _mm_nt(sf_ref[0].astype(mdt), wt_ref[0])
    y = y + _mm_nt(sb_ref[0].astype(mdt), wt_ref[1])
    o_ref[0] = y + u.astype(F32) * skip_ref[...]


def _s5_read(u_pairs, sf, sb, pw_k, c_k, kall, skip, layer):
    a, nch, wd = u_pairs.shape
    blk = lambda *shape: pl.BlockSpec((1,) + shape, lambda i: (i, 0, 0))
    return pl.pallas_call(
        _s5_read_kernel,
        grid=(a,),
        in_specs=[blk(nch, wd), blk(nch, S5_STATE_W), blk(nch, S5_STATE_W), _pair_spec(layer, *pw_k.shape[2:]),
                  _pair_spec(layer, *c_k.shape[2:]), _pair_spec(layer, *kall.shape[2:]),
                  _pair_spec(layer, *skip.shape[2:])],
        out_specs=blk(nch, wd),
        out_shape=jax.ShapeDtypeStruct((a, nch, wd), F32),
        scratch_shapes=[pltpu.VMEM((2, wd, S5_STATE_W), u_pairs.dtype), pltpu.VMEM((wd, wd), u_pairs.dtype)],
        compiler_params=_params(("arbitrary",)),
        name="s5_read",
    )(u_pairs, sf, sb, pw_k, c_k, kall, skip)


def _s5_mixer(u_pairs, weights, layer):
    pw_k, b_k, c_k, kall, scan_tab, skip = weights
    drive = _s5_drive(u_pairs, pw_k, b_k, layer)
    sf, sb = _s5_scan(drive, scan_tab, layer)
    return _s5_read(u_pairs, sf, sb, pw_k, c_k, kall, skip, layer)


def _layer_norm(x, g, b):
    mu = jnp.mean(x, axis=-1, keepdims=True)
    d = x - mu
    var = jnp.mean(d * d, axis=-1, keepdims=True)
    return d * lax.rsqrt(var + LN_EPS) * g + b


def _post_kernel(x_ref, att_ref, ret_ref, s5_ref, mod_ref, permt_ref, wglu_ref, bglu_ref, woa_ref, wor_ref, wos_ref,
                 g1_ref, b1_ref, w1_ref, w2_ref, g2_ref, b2_ref, o_ref):
    mdt = w1_ref.dtype
    sub = x_ref.shape[0] // POST_SPLIT
    csub = sub // S5_T
    nff = D_FF // FF_CHUNK

    def mix(part):
        rows = slice(sub * part, sub * (part + 1))
        zrows = []
        for i in range(S5_T):
            src_vreg, src_blk = divmod(i * S5_PAIR_W, LANES)
            src_blk //= S5_PAIR_W
            cols = [_lane_block_shuffle(
                lambda a: s5_ref[a, csub * part:csub * (part + 1), src_vreg * LANES:(src_vreg + 1) * LANES],
                src_blk, w) for w in range(S5_WIDTH // LANES)]
            zrows.append(jnp.concatenate(cols, axis=1))
        z = jnp.concatenate(zrows, axis=0)
        z_hi = z.astype(mdt)
        z_lo = (z - z_hi.astype(F32)).astype(mdt)
        ys = _mm(permt_ref[...], z_hi) + _mm(permt_ref[...], z_lo)
        hs = jax.nn.gelu(ys)
        gate = jax.nn.sigmoid(_mm(hs.astype(mdt), wglu_ref[...]) + bglu_ref[...])
        s5 = (hs * gate).astype(mdt)
        return _mm(att_ref[rows, :], woa_ref[...]) + _mm(ret_ref[rows, :], wor_ref[...]) + _mm(s5, wos_ref[...])

    def norm1(part, ox):
        rows = slice(sub * part, sub * (part + 1))
        x1 = _layer_norm(DEEPNORM_ALPHA * x_ref[rows, :] + mod_ref[2:3, :] * ox, g1_ref[...], b1_ref[...])
        return x1, (x1 * (1.0 + mod_ref[4:5, :]) + mod_ref[3:4, :]).astype(mdt)

    def ff(h, c):
        a = _mm(h, w1_ref[:, FF_CHUNK * c:FF_CHUNK * (c + 1)])
        a = jnp.square(jnp.maximum(a, 0.0)).astype(mdt)
        return _mm(a, w2_ref[FF_CHUNK * c:FF_CHUNK * (c + 1), :])

    def norm2(part, x1, acc):
        rows = slice(sub * part, sub * (part + 1))
        o_ref[rows, :] = _layer_norm(DEEPNORM_ALPHA * x1 + mod_ref[5:6, :] * acc, g2_ref[...], b2_ref[...])

    nstage = nff + 3
    state = [dict() for _ in range(POST_SPLIT)]
    for part, stage in POST_PROGRAM:
        st = state[part]
        if stage == 0:
            st["ox"] = mix(part)
        elif stage == 1:
            st["x1"], st["h"] = norm1(part, st.pop("ox"))
        elif stage < nstage - 1:
            term = ff(st["h"], stage - 2)
            st["acc"] = term if stage == 2 else st["acc"] + term
        else:
            norm2(part, st["x1"], st["acc"])


def _post(stream, att, ret, s5_pairs, mods, layer, permt, wglu, bglu, wo, g1, b1, w1, w2, g2, b2, skip_context):
    rows = stream.shape[0]
    tm = ROW_TILE
    off = CTX_PAD // tm if skip_context else 0
    row_blk = lambda width: pl.BlockSpec((tm, width), lambda i: (i + off, 0))
    full = lambda arr: pl.BlockSpec(arr.shape, lambda i: (0,) * arr.ndim)
    woa, wor, wos = wo[:ATT_WIDTH], wo[ATT_WIDTH:ATT_WIDTH + RET_WIDTH], wo[ATT_WIDTH + RET_WIDTH:]
    vec = lambda v: v.reshape(1, -1).astype(F32)
    small = [permt, wglu, vec(bglu), woa, wor, wos, vec(g1), vec(b1), w1, w2, vec(g2), vec(b2)]
    return pl.pallas_call(
        _post_kernel,
        grid=(rows // tm - off,),
        in_specs=[row_blk(D_MODEL), row_blk(ATT_WIDTH), row_blk(RET_WIDTH),
                  pl.BlockSpec((S5_PAIRS, tm // S5_T, S5_T * S5_PAIR_W), lambda i: (0, i + off, 0)),
                  pl.BlockSpec((None, None, N_ADA, D_MODEL),
                               lambda i: (layer, jnp.where(i + off == 0, 1, 0), 0, 0))]
                 + [full(arr) for arr in small],
        out_specs=pl.BlockSpec((tm, D_MODEL), lambda i: (i, 0)),
        out_shape=jax.ShapeDtypeStruct((rows - off * tm, D_MODEL), F32),
        compiler_params=_params(("arbitrary",)),
        name="post",
    )(stream, att, ret, s5_pairs, mods, *small)


def kernel(x, c, ctx, c_ctx, w_ada, b_ada, w_in, att_sink, ret_decay_logit, s5_lambda_re, s5_lambda_im, s5_b_re,
           s5_b_im, s5_c_re, s5_c_im, s5_log_dt, s5_d, w_glu, b_glu, w_out, ln1_g, ln1_b, w_ff1, w_ff2, ln2_g,
           ln2_b):
    assert x.shape[0] == 1 and x.shape[2] == D_MODEL and ctx.shape[1] == CTX_LEN
    seq = x.shape[1]
    assert seq % ROW_TILE == 0
    stream = jnp.concatenate([ctx[0], jnp.zeros((CTX_PAD - CTX_LEN, D_MODEL), F32), x[0]], axis=0)
    cond = jnp.zeros((8, D_MODEL), F32).at[0].set(c[0]).at[1].set(c_ctx)
    mods = _modulation(cond, w_ada, b_ada).reshape(DEPTH, 8, N_ADA, D_MODEL)
    tabs = _rope_tables(seq)
    masks = _attention_masks()
    perm = _chunk_perm(ROW_TILE // POST_SPLIT, MXU_DTYPE)
    permt = perm.T
    col_scale = jnp.ones((IN_WIDTH,), F32).at[COL_AQ:COL_AK].set(HEAD_DIM ** -0.5 * LOG2E)
    col_scale = col_scale.at[COL_RQ:COL_RK].set(HEAD_DIM ** -0.5)
    s5w = jax.vmap(_s5_weights)(s5_lambda_re, s5_lambda_im, s5_b_re, s5_b_im, s5_c_re, s5_c_im, s5_log_dt, s5_d)
    log_gamma = jax.nn.log_sigmoid(ret_decay_logit.astype(F32))
    for l in range(DEPTH):
        proj, u_pairs = _in_proj(stream, mods, l, (w_in[l] * col_scale).astype(MXU_DTYPE), tabs, perm)
        att = _attention(proj, att_sink[l].astype(F32) * LOG2E, masks)
        ret = _retention(proj, log_gamma[l])
        s5 = _s5_mixer(u_pairs, s5w, l)
        stream = _post(stream, att, ret, s5, mods, l, permt, w_glu[l].astype(MXU_DTYPE), b_glu[l],
                       w_out[l].astype(MXU_DTYPE), ln1_g[l], ln1_b[l], w_ff1[l].astype(MXU_DTYPE),
                       w_ff2[l].astype(MXU_DTYPE), ln2_g[l], ln2_b[l], skip_context=(l == DEPTH - 1))
    return stream[None]
```

```python
import functools
import math

import jax
import jax.numpy as jnp
from jax import lax
from jax.experimental import pallas as pl
from jax.experimental.pallas import tpu as pltpu

F32 = jnp.float32
MXU_DTYPE = jnp.bfloat16

D_MODEL = 1024
DEPTH = 4
GRID_W = 64
CTX_LEN = 256
CTX_PAD = 512
HEAD_DIM = 64
ATT_HEADS = 8
ATT_KV_HEADS = 2
ATT_BLOCK = 128
ATT_LOOKAHEAD = 2
ROPE_BASE = 10000.0
RET_HEADS = 4
RET_CHUNK = 256
S5_CH = 16
S5_GROUPS = 16
S5_STATE = 64
S5_T = 16
S5_PAIRS = S5_GROUPS // 2
S5_PAIR_W = 2 * S5_CH
S5_TILE = CTX_PAD // S5_T
SCAN_ROWS = 8
S5_STATE_W = 2 * 2 * S5_STATE
ATT_WIDTH = ATT_HEADS * HEAD_DIM
KV_WIDTH = ATT_KV_HEADS * HEAD_DIM
RET_WIDTH = RET_HEADS * HEAD_DIM
S5_WIDTH = S5_GROUPS * S5_CH
IN_WIDTH = ATT_WIDTH + 2 * KV_WIDTH + 4 * RET_WIDTH + S5_WIDTH
D_FF = 4 * D_MODEL
FF_CHUNK = 1024
N_ADA = 6
LN_EPS = 1e-5
GN_EPS = 1e-5
DEEPNORM_ALPHA = (2 * DEPTH) ** 0.25
ROW_TILE = 512
POST_SPLIT = 2
POST_PROGRAM = ((0, 0), (0, 1), (1, 0), (0, 2), (1, 1), (0, 3), (0, 4), (0, 5), (1, 2), (0, 6), (1, 3), (1, 4),
                (1, 5), (1, 6))
NEG_BIG = -1e30
LOG2E = math.log2(math.e)
LANES = 128
VMEM_LIMIT = 56 * 1024 * 1024

COL_AQ, COL_AK, COL_AV = 0, ATT_WIDTH, ATT_WIDTH + KV_WIDTH
COL_RQ = ATT_WIDTH + 2 * KV_WIDTH
COL_RK, COL_RV, COL_RG = COL_RQ + RET_WIDTH, COL_RQ + 2 * RET_WIDTH, COL_RQ + 3 * RET_WIDTH
COL_S5 = COL_RQ + 4 * RET_WIDTH


def _mm(a, b):
    return jnp.dot(a, b, preferred_element_type=F32)


def _mm_nt(a, b):
    return lax.dot_general(a, b, (((1,), (1,)), ((), ())), preferred_element_type=F32)


def _mm_tn(a, b):
    return lax.dot_general(a, b, (((0,), (0,)), ((), ())), preferred_element_type=F32)


def _params(sem):
    return pltpu.CompilerParams(dimension_semantics=sem, vmem_limit_bytes=VMEM_LIMIT)


def _mod_kernel(cond_ref, w_ref, b_ref, o_ref):
    c = cond_ref[...]
    s = c * jax.nn.sigmoid(c)
    o_ref[0] = jnp.dot(s, w_ref[0], preferred_element_type=F32, precision=lax.Precision.HIGHEST) + b_ref[0]


def _modulation(cond, w_ada, b_ada):
    tn = 1536
    n = N_ADA * D_MODEL
    return pl.pallas_call(
        _mod_kernel,
        grid=(DEPTH, n // tn),
        in_specs=[
            pl.BlockSpec((8, D_MODEL), lambda l, j: (0, 0)),
            pl.BlockSpec((1, D_MODEL, tn), lambda l, j: (l, 0, j)),
            pl.BlockSpec((1, 1, tn), lambda l, j: (l, 0, j)),
        ],
        out_specs=pl.BlockSpec((1, 8, tn), lambda l, j: (l, 0, j)),
        out_shape=jax.ShapeDtypeStruct((DEPTH, 8, n), F32),
        compiler_params=_params(("arbitrary", "arbitrary")),
        name="modulation",
    )(cond, w_ada, b_ada.reshape(DEPTH, 1, n))


def _lane_block_shuffle(src_rows, src_lane_blk, out_vreg):
    acc = None
    for q in range(LANES // S5_PAIR_W):
        piece = src_rows(out_vreg * (LANES // S5_PAIR_W) + q)
        shift = (S5_PAIR_W * (q - src_lane_blk)) % LANES
        if shift:
            piece = pltpu.roll(piece, shift, 1)
        if acc is None:
            acc = piece
        else:
            lane_blk = lax.broadcasted_iota(jnp.int32, piece.shape, 1) // S5_PAIR_W
            acc = jnp.where(lane_blk == q, piece, acc)
    return acc


def _in_proj_kernel(*refs, split):
    nres = 2 if split else 1
    mod_ref, w_ref, ca_ref, sa_ref, cr_ref, sr_ref, perm_ref, o_ref, u_ref = refs[nres:]
    x = _residual_rows(refs[:nres])(slice(None))
    h = (x * (1.0 + mod_ref[1:2, :]) + mod_ref[0:1, :]).astype(w_ref.dtype)
    lane = lax.broadcasted_iota(jnp.int32, (x.shape[0], LANES), 1)
    first_att = (lane & 31) < 16
    first_ret = (lane & 63) < 32

    def proj(c0, c1):
        return _mm(h, w_ref[:, c0:c1])

    def rope_store(c0, width, cos, sin, first, half):
        p = proj(c0, c0 + width)
        for b in range(width // LANES):
            blk = p[:, LANES * b:LANES * (b + 1)]
            rot = jnp.where(first, pltpu.roll(blk, LANES - half, 1), pltpu.roll(blk, half, 1))
            o_ref[:, c0 + LANES * b:c0 + LANES * (b + 1)] = (blk * cos + rot * sin).astype(o_ref.dtype)

    u = proj(COL_S5, IN_WIDTH).astype(w_ref.dtype)
    sub = perm_ref.shape[0]
    nchunk = sub // S5_T
    for part in range(x.shape[0] // sub):
        g = _mm(perm_ref[...], u[sub * part:sub * (part + 1)])
        for a in range(S5_PAIRS):
            vreg_col, lane_blk = divmod(a * S5_PAIR_W, LANES)
            lane_blk //= S5_PAIR_W
            for v in range(S5_T * S5_PAIR_W // LANES):
                slab = _lane_block_shuffle(
                    lambda j: g[nchunk * j:nchunk * (j + 1), vreg_col * LANES:(vreg_col + 1) * LANES], lane_blk, v)
                u_ref[a, nchunk * part:nchunk * (part + 1), LANES * v:LANES * (v + 1)] = slab.astype(u_ref.dtype)

    ca, sa, cr, sr = ca_ref[...], sa_ref[...], cr_ref[...], sr_ref[...]
    rope_store(COL_AQ, ATT_WIDTH, ca, sa, first_att, 16)
    rope_store(COL_AK, KV_WIDTH, ca, sa, first_att, 16)
    rope_store(COL_RQ, RET_WIDTH, cr, sr, first_ret, 32)
    rope_store(COL_RK, RET_WIDTH, cr, sr, first_ret, 32)
    o_ref[:, COL_AV:COL_RQ] = proj(COL_AV, COL_RQ).astype(o_ref.dtype)
    o_ref[:, COL_RV:COL_S5] = proj(COL_RV, COL_S5).astype(o_ref.dtype)


def _chunk_perm(tile_rows, dtype):
    nchunk = tile_rows // S5_T
    r = jnp.arange(tile_rows)
    src = S5_T * (r % nchunk) + r // nchunk
    return (src[:, None] == jnp.arange(tile_rows)[None, :]).astype(dtype)


def _in_proj(residual, mods, layer, w_in, tabs, perm):
    rows = tabs[0].shape[0]
    tm = ROW_TILE
    tab_spec = pl.BlockSpec((tm, LANES), lambda i: (i, 0))
    nch = rows // S5_T
    return pl.pallas_call(
        functools.partial(_in_proj_kernel, split=len(residual) > 1),
        grid=(rows // tm,),
        in_specs=_residual_specs(residual) + [
            pl.BlockSpec((None, None, N_ADA, D_MODEL), lambda i: (layer, jnp.where(i == 0, 1, 0), 0, 0)),
            pl.BlockSpec((D_MODEL, IN_WIDTH), lambda i: (0, 0)),
            tab_spec, tab_spec, tab_spec, tab_spec,
            pl.BlockSpec(perm.shape, lambda i: (0, 0)),
        ],
        out_specs=[pl.BlockSpec((tm, COL_S5), lambda i: (i, 0)),
                   pl.BlockSpec((S5_PAIRS, tm // S5_T, S5_T * S5_PAIR_W), lambda i: (0, i, 0))],
        out_shape=[jax.ShapeDtypeStruct((rows, COL_S5), MXU_DTYPE),
                   jax.ShapeDtypeStruct((S5_PAIRS, nch, S5_T * S5_PAIR_W), MXU_DTYPE)],
        compiler_params=_params(("arbitrary",)),
        name="in_proj",
    )(*residual, mods, w_in, *tabs, perm)


def _rope_tables(seq):
    half_a = HEAD_DIM // 4
    half_r = HEAD_DIM // 2
    nrow = seq // GRID_W
    inv_a = ROPE_BASE ** (-jnp.arange(half_a, dtype=F32) / half_a)
    inv_r = ROPE_BASE ** (-jnp.arange(half_r, dtype=F32) / half_r)
    ang_r = jnp.arange(nrow, dtype=F32)[:, None] * inv_a[None, :]
    ang_c = jnp.arange(GRID_W, dtype=F32)[:, None] * inv_a[None, :]
    ang_t = jnp.arange(seq, dtype=F32)[:, None] * inv_r[None, :]
    hp = lax.Precision.HIGHEST
    lane = jnp.arange(LANES)
    within = lane % HEAD_DIM
    pick_a = (within % half_a)[None, :] == jnp.arange(half_a)[:, None]
    exp_row = (pick_a & (within < 2 * half_a)[None, :]).astype(F32)
    exp_col = (pick_a & (within >= 2 * half_a)[None, :]).astype(F32)
    exp_t = ((lane % half_r)[None, :] == jnp.arange(half_r)[:, None]).astype(F32)
    sign_a = jnp.where(within % (2 * half_a) < half_a, -1.0, 1.0).astype(F32)
    sign_r = jnp.where(within < half_r, -1.0, 1.0).astype(F32)

    def att_table(fn):
        by_row = jnp.dot(fn(ang_r), exp_row, precision=hp)
        by_col = jnp.dot(fn(ang_c), exp_col, precision=hp)
        return (by_row[:, None, :] + by_col[None, :, :]).reshape(seq, LANES)

    cos_a = att_table(jnp.cos)
    sin_a = att_table(jnp.sin) * sign_a
    cos_r = jnp.dot(jnp.cos(ang_t), exp_t, precision=hp)
    sin_r = jnp.dot(jnp.sin(ang_t), exp_t, precision=hp) * sign_r
    pad = lambda tab, ident: jnp.pad(tab, ((CTX_PAD, 0), (0, 0)), constant_values=ident)
    return pad(cos_a, 1.0), pad(sin_a, 0.0), pad(cos_r, 1.0), pad(sin_r, 0.0)


def _swap_halves(x):
    if x.dtype.itemsize == 4:
        return pltpu.roll(x, 64, 1)
    packed = pltpu.bitcast(x, jnp.uint32)
    return pltpu.bitcast(pltpu.roll(packed, 64, 1), x.dtype)


def _dup_heads(x):
    sw = _swap_halves(x)
    lo = lax.broadcasted_iota(jnp.int32, x.shape, 1) < HEAD_DIM
    return jnp.where(lo, x, sw), jnp.where(lo, sw, x)


def _attn_kernel(sink_ref, q_ref, km_ref, kp_ref, kn_ref, vm_ref, vp_ref, vn_ref, kc_ref, vc_ref, mask_ref,
                 o_ref, k2_ref, v2_ref, kc2_ref, vc2_ref):
    i = pl.program_id(0)
    last_blk = pl.num_programs(0) * (ROW_TILE // ATT_BLOCK) - 1
    blk = ATT_BLOCK
    def spread(src, ones_upper):
        x = src[...]
        a, b = _dup_heads(x)
        if ones_upper:
            upper = lax.broadcasted_iota(jnp.int32, x.shape, 1) >= HEAD_DIM
            a = jnp.where(upper, jnp.ones_like(a), a)
            b = jnp.where(upper, jnp.ones_like(b), b)
        return a, b

    for dst, parts, is_v in ((k2_ref, (kp_ref, km_ref, kn_ref), False), (v2_ref, (vp_ref, vm_ref, vn_ref), True)):
        row = 0
        for part in parts:
            a, b = spread(part, is_v)
            n = part.shape[0]
            dst[0, row:row + n, :] = a
            dst[1, row:row + n, :] = b
            row += n
    for dst, src, is_v in ((kc2_ref, kc_ref, False), (vc2_ref, vc_ref, True)):
        a, b = spread(src, is_v)
        dst[0] = a
        dst[1] = b

    lo = lax.broadcasted_iota(jnp.int32, (blk, LANES), 1) < HEAD_DIM
    group = ATT_HEADS // ATT_KV_HEADS

    nloc = 3 * blk

    def scores(j, kv):
        r0 = j * blk
        qt = q_ref[r0:r0 + blk, group * HEAD_DIM * kv:group * HEAD_DIM * (kv + 1)]
        parts = []
        for g in range(group):
            qc = qt[:, LANES * (g // 2):LANES * (g // 2 + 1)]
            keep = lo if g % 2 == 0 else jnp.logical_not(lo)
            parts.append(jnp.where(keep, qc, jnp.zeros_like(qc)))
        qs = jnp.concatenate(parts, axis=0)
        return _mm_nt(qs, k2_ref[kv, r0:r0 + nloc, :]), _mm_nt(qs, kc2_ref[kv])

    def finish(j, kv, s_loc, s_ctx):
        r0 = j * blk
        gblk = i * (ROW_TILE // blk) + j
        sel = jnp.where(i == 0, 3, jnp.where(gblk == CTX_PAD // blk, 0, jnp.where(gblk == last_blk, 2, 1)))
        bias = mask_ref[sel]
        probs, sink_w = [], []
        for g in range(group):
            s = jnp.concatenate([s_loc[blk * g:blk * (g + 1)] + bias, s_ctx[blk * g:blk * (g + 1)]], axis=1)
            sk = sink_ref[group * kv + g]
            m = jnp.maximum(jnp.max(s, axis=-1, keepdims=True), sk)
            probs.append(jnp.exp2(s - m).astype(o_ref.dtype))
            sink_w.append(jnp.exp2(sk - m))
        p = jnp.concatenate(probs, axis=0)
        o = _mm(p[:, :nloc], v2_ref[kv, r0:r0 + nloc, :]) + _mm(p[:, nloc:], vc2_ref[kv])
        for half in range(group // 2):
            even, odd = 2 * half, 2 * half + 1
            oe = o[blk * even:blk * (even + 1)]
            oo = o[blk * odd:blk * (odd + 1)]
            y_even = oe * (1.0 / (pltpu.roll(oe, HEAD_DIM, 1) + sink_w[even]))
            y_odd = pltpu.roll(oo, HEAD_DIM, 1) * (1.0 / (oo + sink_w[odd]))
            c0 = group * HEAD_DIM * kv + LANES * half
            o_ref[r0:r0 + blk, c0:c0 + LANES] = jnp.where(lo, y_even, y_odd).astype(o_ref.dtype)

    items = [(j, kv) for j in range(ROW_TILE // blk) for kv in range(ATT_KV_HEADS)]
    pending = {}
    for t in range(len(items) + ATT_LOOKAHEAD):
        if t < len(items):
            pending[t] = scores(*items[t])
        if t >= ATT_LOOKAHEAD:
            finish(*items[t - ATT_LOOKAHEAD], *pending.pop(t - ATT_LOOKAHEAD))


def _attention_masks():
    qi = jnp.arange(ATT_BLOCK)[:, None]
    kj = jnp.arange(3 * ATT_BLOCK)[None, :]
    band = jnp.abs(kj - ATT_BLOCK - qi) <= ATT_BLOCK
    first = band & (kj >= ATT_BLOCK)
    last = band & (kj < 2 * ATT_BLOCK)
    none = jnp.zeros_like(band)
    masks = jnp.stack([first, band, last, none])
    return jnp.where(masks, 0.0, NEG_BIG).astype(F32)


def _attention(proj, sink, masks):
    rows = proj.shape[0]
    tm, blk = ROW_TILE, ATT_BLOCK
    per = tm // blk
    nblk = rows // blk
    ck, cv = COL_AK // KV_WIDTH, COL_AV // KV_WIDTH
    dt = proj.dtype
    return pl.pallas_call(
        _attn_kernel,
        grid=(rows // tm,),
        in_specs=[
            pl.BlockSpec(memory_space=pltpu.SMEM),
            pl.BlockSpec((tm, ATT_WIDTH), lambda i: (i, 0)),
            pl.BlockSpec((tm, KV_WIDTH), lambda i: (i, ck)),
            pl.BlockSpec((blk, KV_WIDTH), lambda i: (jnp.maximum(i * per - 1, 0), ck)),
            pl.BlockSpec((blk, KV_WIDTH), lambda i: (jnp.minimum((i + 1) * per, nblk - 1), ck)),
            pl.BlockSpec((tm, KV_WIDTH), lambda i: (i, cv)),
            pl.BlockSpec((blk, KV_WIDTH), lambda i: (jnp.maximum(i * per - 1, 0), cv)),
            pl.BlockSpec((blk, KV_WIDTH), lambda i: (jnp.minimum((i + 1) * per, nblk - 1), cv)),
            pl.BlockSpec((CTX_LEN, KV_WIDTH), lambda i: (0, ck)),
            pl.BlockSpec((CTX_LEN, KV_WIDTH), lambda i: (0, cv)),
            pl.BlockSpec((4, blk, 3 * blk), lambda i: (0, 0, 0)),
        ],
        out_specs=pl.BlockSpec((tm, ATT_WIDTH), lambda i: (i, 0)),
        out_shape=jax.ShapeDtypeStruct((rows, ATT_WIDTH), dt),
        scratch_shapes=[
            pltpu.VMEM((2, tm + 2 * blk, KV_WIDTH), dt),
            pltpu.VMEM((2, tm + 2 * blk, KV_WIDTH), dt),
            pltpu.VMEM((2, CTX_LEN, KV_WIDTH), dt),
            pltpu.VMEM((2, CTX_LEN, KV_WIDTH), dt),
        ],
        compiler_params=_params(("arbitrary",)),
        name="attention",
    )(sink, proj, proj, proj, proj, proj, proj, proj, proj, proj, masks)


def _ret_kernel(lg_ref, q_ref, k_ref, v_ref, g_ref, o_ref, sb_ref, s_ref, dm_ref, tab_ref, gbd_ref):
    ph = pl.program_id(0)
    t = pl.program_id(1)
    ntile = pl.num_programs(1)
    c = RET_CHUNK
    w = RET_WIDTH
    per = q_ref.shape[0] // c
    mdt = sb_ref.dtype
    rows = lambda ci: slice(c * ci, c * (ci + 1))

    def lane_vec(direction, shape, axis):
        head = lax.broadcasted_iota(jnp.int32, shape, axis) // HEAD_DIM
        out = jnp.full(shape, lg_ref[direction, RET_HEADS - 1], F32)
        for h in range(RET_HEADS - 2, -1, -1):
            out = jnp.where(head == h, lg_ref[direction, h], out)
        return out

    @pl.when(jnp.logical_and(ph == 0, t == 0))
    def _init_tables():
        diff = (lax.broadcasted_iota(jnp.int32, (c, c), 0) - lax.broadcasted_iota(jnp.int32, (c, c), 1)).astype(F32)
        for h in range(RET_HEADS):
            dm_ref[h] = jnp.exp(jnp.where(diff >= 0, diff * lg_ref[0, h], -diff * lg_ref[1, h]))
        pos = lax.broadcasted_iota(jnp.int32, (c, w), 0).astype(F32)
        lgf = lane_vec(0, (c, w), 1)
        lgb = lane_vec(1, (c, w), 1)
        tab_ref[0] = jnp.exp((c - 1.0 - pos) * lgf)
        tab_ref[1] = jnp.exp((pos + 1.0) * lgf)
        tab_ref[2] = jnp.exp(pos * lgb)
        tab_ref[3] = jnp.exp((c - pos) * lgb)
        same = (lax.broadcasted_iota(jnp.int32, (w, w), 0) // HEAD_DIM
                == lax.broadcasted_iota(jnp.int32, (w, w), 1) // HEAD_DIM)
        bd = jnp.where(same, 1.0, 0.0)
        gbd_ref[0] = bd * jnp.exp(c * lane_vec(0, (w, w), 0))
        gbd_ref[1] = bd * jnp.exp(c * lane_vec(1, (w, w), 0))
        gbd_ref[2] = bd

    @pl.when(t == 0)
    def _reset_state():
        s_ref[...] = jnp.zeros_like(s_ref)

    def state_update(direction, key_tab, ci):
        kw = (k_ref[rows(ci), :].astype(F32) * tab_ref[key_tab]).astype(mdt)
        u = _mm_tn(kw, v_ref[rows(ci), :])
        s_ref[...] = gbd_ref[direction] * s_ref[...] + gbd_ref[2] * u

    @pl.when(jnp.logical_and(ph == 0, t == 0))
    def _backward_context():
        sb_ref[0] = s_ref[...].astype(mdt)
        state_update(1, 2, 0)

    @pl.when(jnp.logical_and(ph == 0, t > 0))
    def _backward_latent():
        base = 1 + per * (ntile - t - 1)
        for ci in range(per - 1, -1, -1):
            sb_ref[base + ci] = s_ref[...].astype(mdt)
            state_update(1, 2, ci)

    head = lax.broadcasted_iota(jnp.int32, (c, w), 1) // HEAD_DIM

    def scores(ci):
        q = q_ref[rows(ci), :]
        qs = jnp.concatenate([jnp.where(head == h, q, jnp.zeros_like(q)) for h in range(RET_HEADS)], axis=0)
        return _mm_nt(qs, k_ref[rows(ci), :])

    def intra(ci, sc):
        scd = jnp.concatenate([sc[c * h:c * (h + 1)] * dm_ref[h] for h in range(RET_HEADS)], axis=0).astype(mdt)
        oi = _mm(scd, v_ref[rows(ci), :])
        o = jnp.where(head == 0, oi[0:c], 0.0)
        for h in range(1, RET_HEADS):
            o = o + jnp.where(head == h, oi[c * h:c * (h + 1)], 0.0)
        return o

    def cross(ci, idx):
        qf = q_ref[rows(ci), :].astype(F32)
        return (_mm((qf * tab_ref[1]).astype(mdt), s_ref[...].astype(mdt))
                + _mm((qf * tab_ref[3]).astype(mdt), sb_ref[idx]))

    def finish(ci, o):
        avg = (gbd_ref[2] * (1.0 / HEAD_DIM)).astype(mdt)

        def head_mean(val):
            hi = val.astype(mdt)
            lo = (val - hi.astype(F32)).astype(mdt)
            return _mm(hi, avg) + _mm(lo, avg)

        d = o - head_mean(o)
        var = head_mean(d * d)
        gate = g_ref[rows(ci), :].astype(F32)
        y = d * lax.rsqrt(var + GN_EPS) * (gate * jax.nn.sigmoid(gate))
        o_ref[rows(ci), :] = y.astype(o_ref.dtype)

    @pl.when(jnp.logical_and(ph == 1, t == 0))
    def _forward_context():
        o = intra(0, scores(0)) + cross(0, 0)
        state_update(0, 0, 0)
        finish(0, o)
        for ci in range(1, per):
            o_ref[rows(ci), :] = jnp.zeros((c, w), o_ref.dtype)

    @pl.when(jnp.logical_and(ph == 1, t > 0))
    def _forward_latent():
        base = 1 + per * (t - 1)
        sc = [scores(ci) for ci in range(per)]
        outs = [intra(ci, sc[ci]) for ci in range(per)]
        for ci in range(per):
            outs[ci] = outs[ci] + cross(ci, base + ci)
            state_update(0, 0, ci)
        for ci in range(per):
            finish(ci, outs[ci])


def _retention(proj, log_gamma):
    rows = proj.shape[0]
    c = RET_CHUNK
    tm = ROW_TILE
    ntile = rows // tm
    nchunk = 1 + (rows - CTX_PAD) // c
    dt = proj.dtype
    cq, ckk, cvv, cg = (COL_RQ // RET_WIDTH, COL_RK // RET_WIDTH, COL_RV // RET_WIDTH, COL_RG // RET_WIDTH)

    def kv_blk(ph, t):
        return jnp.where(ph == 0, jnp.where(t == 0, 0, ntile - t), t)

    def fw_blk(ph, t):
        return jnp.where(ph == 0, 0, t)

    return pl.pallas_call(
        _ret_kernel,
        grid=(2, ntile),
        in_specs=[
            pl.BlockSpec(memory_space=pltpu.SMEM),
            pl.BlockSpec((tm, RET_WIDTH), lambda ph, t: (fw_blk(ph, t), cq)),
            pl.BlockSpec((tm, RET_WIDTH), lambda ph, t: (kv_blk(ph, t), ckk)),
            pl.BlockSpec((tm, RET_WIDTH), lambda ph, t: (kv_blk(ph, t), cvv)),
            pl.BlockSpec((tm, RET_WIDTH), lambda ph, t: (fw_blk(ph, t), cg)),
        ],
        out_specs=pl.BlockSpec((tm, RET_WIDTH), lambda ph, t: (fw_blk(ph, t), 0)),
        out_shape=jax.ShapeDtypeStruct((rows, RET_WIDTH), dt),
        scratch_shapes=[
            pltpu.VMEM((nchunk, RET_WIDTH, RET_WIDTH), dt),
            pltpu.VMEM((RET_WIDTH, RET_WIDTH), F32),
            pltpu.VMEM((RET_HEADS, c, c), F32),
            pltpu.VMEM((4, c, RET_WIDTH), F32),
            pltpu.VMEM((3, RET_WIDTH, RET_WIDTH), F32),
        ],
        compiler_params=_params(("arbitrary", "arbitrary")),
        name="retention",
    )(log_gamma, proj, proj, proj, proj)


def _s5_weights(lam_re, lam_im, b_re, b_im, c_re, c_im, log_dt, d_skip):
    hp = lax.Precision.HIGHEST
    tt, g, n, p, a = S5_T, S5_GROUPS, S5_STATE, S5_CH, S5_PAIRS
    lam = lax.complex(lam_re.astype(F32), lam_im.astype(F32))
    dtv = jnp.exp(log_dt.astype(F32))[..., None]
    lam_bar = jnp.exp(lam * dtv)
    bbar = ((lam_bar - 1.0) / lam)[..., None] * lax.complex(b_re.astype(F32), b_im.astype(F32))
    cmat = lax.complex(c_re.astype(F32), c_im.astype(F32))
    pw = [jnp.ones_like(lam_bar)]
    for _ in range(tt):
        pw.append(pw[-1] * lam_bar)
    pw = jnp.stack(pw, axis=1)
    eye2 = jnp.eye(2, dtype=F32)
    ri = lambda z, axis: jnp.stack([jnp.real(z), jnp.imag(z)], axis=axis)

    pw_l = pw.reshape(2, tt + 1, a, 2 * n)
    bbt = jnp.einsum('dahpn,gh->dagphn', jnp.swapaxes(bbar, -1, -2).reshape(2, a, 2, p, n), eye2)
    bbt = bbt.reshape(2, a, 2 * p, 2 * n)
    cct = jnp.einsum('dahpn,gh->dagphn', cmat.reshape(2, a, 2, p, n), eye2).reshape(2, a, 2 * p, 2 * n)
    pw_k = ri(pw_l, 1).transpose(3, 0, 1, 2, 4)
    b_k = ri(bbt, 2).transpose(1, 0, 2, 3, 4)
    c_k = ri(cct, 2).transpose(1, 0, 2, 3, 4)
    decay = [pw_l[:, tt]]
    for _ in range(SCAN_ROWS - 1):
        decay.append(decay[-1] * decay[0])
    decay = jnp.stack(decay, axis=0)
    rows8 = lambda z: jnp.broadcast_to(z[None], (SCAN_ROWS,) + z.shape)
    carry_w = jnp.stack([decay[:, 0], decay[::-1, 1]], axis=1)
    scan_tab = jnp.stack([rows8(decay[0]), rows8(decay[1]), rows8(decay[3]), carry_w], axis=0)
    scan_tab = ri(scan_tab, 0).transpose(4, 3, 1, 0, 2, 5)
    skip = jnp.tile(d_skip.astype(F32).reshape(a, 1, 2 * p), (1, tt, 1)).reshape(a, 1, tt * 2 * p)
    return pw_k, b_k, c_k, scan_tab, skip


def _pair_spec(layer, *shape):
    return pl.BlockSpec((None, None) + shape, lambda i: (layer, i) + (0,) * len(shape))


def _s5_drive_kernel(u_ref, pw_ref, b_ref, o_ref, w_ref):
    rows, half = S5_PAIR_W, LANES
    for d in range(2):
        br, bi = b_ref[d, 0], b_ref[d, 1]
        for j in range(S5_T):
            e = S5_T - 1 - j if d == 0 else j
            pr, pi = pw_ref[d, 0, e:e + 1, :], pw_ref[d, 1, e:e + 1, :]
            w_ref[rows * j:rows * (j + 1), 2 * half * d:2 * half * d + half] = (pr * br - pi * bi).astype(w_ref.dtype)
            w_ref[rows * j:rows * (j + 1), 2 * half * d + half:2 * half * (d + 1)] = (
                pr * bi + pi * br).astype(w_ref.dtype)
    o_ref[0] = _mm(u_ref[0], w_ref[...])


def _s5_drive(u_pairs, pw_k, b_k, layer):
    a, nch, wd = u_pairs.shape
    return pl.pallas_call(
        _s5_drive_kernel,
        grid=(a,),
        in_specs=[pl.BlockSpec((1, nch, wd), lambda i: (i, 0, 0)),
                  _pair_spec(layer, *pw_k.shape[2:]), _pair_spec(layer, *b_k.shape[2:])],
        out_specs=pl.BlockSpec((1, nch, 2 * S5_STATE_W), lambda i: (i, 0, 0)),
        out_shape=jax.ShapeDtypeStruct((a, nch, 2 * S5_STATE_W), F32),
        scratch_shapes=[pltpu.VMEM((wd, 2 * S5_STATE_W), u_pairs.dtype)],
        compiler_params=_params(("arbitrary",)),
        name="s5_drive",
    )(u_pairs, pw_k, b_k)


def _s5_scan_kernel(ef_ref, eb_ref, tab_ref, sf_ref, sb_ref, st_ref):
    t = pl.program_id(0)
    hw = LANES
    sub = SCAN_ROWS
    npair = tab_ref.shape[0]
    row = lax.broadcasted_iota(jnp.int32, (sub, hw), 0)

    @pl.when(t == 0)
    def _reset():
        st_ref[...] = jnp.zeros_like(st_ref)

    def shift(x, k, reverse):
        if reverse:
            return jnp.where(row < sub - k, pltpu.roll(x, sub - k, 0), 0.0)
        return jnp.where(row >= k, pltpu.roll(x, k, 0), 0.0)

    def scan_group(e_ref, o_ref, a, d, r0, cr, ci):
        reverse = d == 1
        xr = e_ref[a, pl.ds(r0, sub), 0:hw]
        xi = e_ref[a, pl.ds(r0, sub), hw:2 * hw]
        for step, k in enumerate((1, 2, 4)):
            ar, ai = tab_ref[a, d, step, 0], tab_ref[a, d, step, 1]
            sr, si = shift(xr, k, reverse), shift(xi, k, reverse)
            xr, xi = xr + ar * sr - ai * si, xi + ar * si + ai * sr
        wr, wi = tab_ref[a, d, 3, 0], tab_ref[a, d, 3, 1]
        fr = xr + wr * cr - wi * ci
        fi = xi + wr * ci + wi * cr
        edge = sub - 1 if reverse else 0
        o_ref[a, pl.ds(r0, sub), 0:hw] = jnp.where(row == edge, cr, shift(fr, 1, reverse))
        o_ref[a, pl.ds(r0, sub), hw:2 * hw] = jnp.where(row == edge, ci, shift(fi, 1, reverse))
        last = 0 if reverse else sub - 1
        return (jnp.broadcast_to(fr[last:last + 1], (sub, hw)), jnp.broadcast_to(fi[last:last + 1], (sub, hw)))

    def run(nrows):
        ngroups = nrows // sub

        def body(gi, carry):
            r0 = pl.multiple_of(gi * sub, sub)
            rb0 = pl.multiple_of((ngroups - 1 - gi) * sub, sub)
            new = []
            for a in range(npair):
                fr, fi, br, bi = carry[a]
                fr, fi = scan_group(ef_ref, sf_ref, a, 0, r0, fr, fi)
                br, bi = scan_group(eb_ref, sb_ref, a, 1, rb0, br, bi)
                new.append((fr, fi, br, bi))
            return tuple(new)

        init = tuple(tuple(st_ref[a, k] for k in range(4)) for a in range(npair))
        final = lax.fori_loop(0, ngroups, body, init)
        for a in range(npair):
            for k in range(4):
                st_ref[a, k] = final[a][k]

    @pl.when(t == 0)
    def _context():
        sf_ref[...] = jnp.zeros_like(sf_ref)
        sb_ref[...] = jnp.zeros_like(sb_ref)
        run(CTX_LEN // S5_T)

    @pl.when(t > 0)
    def _latent():
        run(S5_TILE)


def _s5_scan(drive, scan_tab, layer):
    a, nch, wd2 = drive.shape
    wd = wd2 // 2
    nt = nch // S5_TILE

    def bwd(t):
        return jnp.where(t == 0, 0, nt - t)

    return pl.pallas_call(
        _s5_scan_kernel,
        grid=(nt,),
        in_specs=[pl.BlockSpec((a, S5_TILE, wd), lambda t: (0, t, 0)),
                  pl.BlockSpec((a, S5_TILE, wd), lambda t: (0, bwd(t), 1)),
                  pl.BlockSpec((None,) + scan_tab.shape[1:], lambda t: (layer,) + (0,) * (scan_tab.ndim - 1))],
        out_specs=[pl.BlockSpec((a, S5_TILE, wd), lambda t: (0, t, 0)),
                   pl.BlockSpec((a, S5_TILE, wd), lambda t: (0, bwd(t), 0))],
        out_shape=[jax.ShapeDtypeStruct((a, nch, wd), F32), jax.ShapeDtypeStruct((a, nch, wd), F32)],
        scratch_shapes=[pltpu.VMEM((a, 4, SCAN_ROWS, LANES), F32)],
        compiler_params=_params(("arbitrary",)),
        name="s5_scan",
    )(drive, drive, scan_tab)


def _lane_window(x, start, width):
    cols = []
    for v in range(width // LANES):
        k0, off = divmod(start + LANES * v, LANES)
        lo = x[:, LANES * k0:LANES * (k0 + 1)]
        if off:
            hi = x[:, LANES * (k0 + 1):LANES * (k0 + 2)]
            lane = lax.broadcasted_iota(jnp.int32, lo.shape, 1)
            lo = jnp.where(lane < LANES - off, pltpu.roll(lo, LANES - off, 1), pltpu.roll(hi, LANES - off, 1))
        cols.append(lo)
    return jnp.concatenate(cols, axis=1)


def _s5_read_kernel(u_ref, sf_ref, sb_ref, pw_ref, b_ref, c_ref, skip_ref, o_ref, wt_ref, wi_ref, lag_ref):
    u = u_ref[0]
    mdt = u.dtype
    rows, half = S5_PAIR_W, LANES
    for d in range(2):
        cr, ci = c_ref[d, 0], c_ref[d, 1]
        for i in range(S5_T):
            e = i + 1 if d == 0 else S5_T - i
            pr, pi = pw_ref[d, 0, e:e + 1, :], pw_ref[d, 1, e:e + 1, :]
            wt_ref[d, rows * i:rows * (i + 1), 0:half] = (pr * cr - pi * ci).astype(mdt)
            wt_ref[d, rows * i:rows * (i + 1), half:2 * half] = (-(pr * ci + pi * cr)).astype(mdt)
    nlag = 2 * S5_T - 1
    zero = jnp.zeros((rows, half), F32)
    for l in range(nlag + 1):
        lag = l - (S5_T - 1)
        for d, active in ((1, lag <= 0), (0, 0 <= lag < S5_T)):
            col = 2 * half * (1 - d)
            if active:
                cr, ci = c_ref[d, 0], c_ref[d, 1]
                pr, pi = pw_ref[d, 0, abs(lag):abs(lag) + 1, :], pw_ref[d, 1, abs(lag):abs(lag) + 1, :]
                lag_ref[rows * l:rows * (l + 1), col:col + half] = pr * cr - pi * ci
                lag_ref[rows * l:rows * (l + 1), col + half:col + 2 * half] = pr * ci + pi * cr
            else:
                lag_ref[rows * l:rows * (l + 1), col:col + half] = zero
                lag_ref[rows * l:rows * (l + 1), col + half:col + 2 * half] = zero
    lhs = jnp.concatenate([b_ref[1, 0], -b_ref[1, 1], b_ref[0, 0], -b_ref[0, 1]], axis=1)
    kall = lax.dot_general(lhs, lag_ref[...], (((1,), (1,)), ((), ())), preferred_element_type=F32,
                           precision=lax.Precision.HIGHEST)
    for j in range(S5_T):
        wi_ref[rows * j:rows * (j + 1), :] = _lane_window(kall, rows * (S5_T - 1 - j), S5_T * rows).astype(mdt)
    y = _mm(u, wi_ref[...])
    y = y + _mm_nt(sf_ref[0].astype(mdt), wt_ref[0])
    y = y + _mm_nt(sb_ref[0].astype(mdt), wt_ref[1])
    o_ref[0] = y + u.astype(F32) * skip_ref[...]


def _s5_read(u_pairs, sf, sb, pw_k, b_k, c_k, skip, layer):
    a, nch, wd = u_pairs.shape
    blk = lambda *shape: pl.BlockSpec((1,) + shape, lambda i: (i, 0, 0))
    return pl.pallas_call(
        _s5_read_kernel,
        grid=(a,),
        in_specs=[blk(nch, wd), blk(nch, S5_STATE_W), blk(nch, S5_STATE_W), _pair_spec(layer, *pw_k.shape[2:]),
                  _pair_spec(layer, *b_k.shape[2:]), _pair_spec(layer, *c_k.shape[2:]),
                  _pair_spec(layer, *skip.shape[2:])],
        out_specs=blk(nch, wd),
        out_shape=jax.ShapeDtypeStruct((a, nch, wd), F32),
        scratch_shapes=[pltpu.VMEM((2, wd, S5_STATE_W), u_pairs.dtype), pltpu.VMEM((wd, wd), u_pairs.dtype),
                        pltpu.VMEM((2 * wd, 2 * S5_STATE_W), F32)],
        compiler_params=_params(("arbitrary",)),
        name="s5_read",
    )(u_pairs, sf, sb, pw_k, b_k, c_k, skip)


def _s5_mixer(u_pairs, weights, layer):
    pw_k, b_k, c_k, scan_tab, skip = weights
    drive = _s5_drive(u_pairs, pw_k, b_k, layer)
    sf, sb = _s5_scan(drive, scan_tab, layer)
    return _s5_read(u_pairs, sf, sb, pw_k, b_k, c_k, skip, layer)


def _layer_norm(x, g, b):
    mu = jnp.mean(x, axis=-1, keepdims=True)
    d = x - mu
    var = jnp.mean(d * d, axis=-1, keepdims=True)
    return d * lax.rsqrt(var + LN_EPS) * g + b


def _residual_rows(refs):
    if len(refs) == 1:
        return lambda rows: refs[0][rows, :]
    head_ref, body_ref = refs
    is_head = pl.program_id(0) == 0
    return lambda rows: jnp.where(is_head, head_ref[rows, :], body_ref[rows, :])


def _post_kernel(*refs, split):
    nres = 2 if split else 1
    load_x = _residual_rows(refs[:nres])
    (att_ref, ret_ref, s5_ref, mod_ref, permt_ref, wglu_ref, bglu_ref, wo_ref, g1_ref, b1_ref, w1_ref, w2_ref,
     g2_ref, b2_ref, o_ref) = refs[nres:]
    woa_ref = wo_ref.at[0:ATT_WIDTH]
    wor_ref = wo_ref.at[ATT_WIDTH:ATT_WIDTH + RET_WIDTH]
    wos_ref = wo_ref.at[ATT_WIDTH + RET_WIDTH:ATT_WIDTH + RET_WIDTH + S5_WIDTH]
    mdt = w1_ref.dtype
    sub = o_ref.shape[0] // POST_SPLIT
    csub = sub // S5_T
    nff = D_FF // FF_CHUNK

    def mix(part):
        rows = slice(sub * part, sub * (part + 1))
        zrows = []
        for i in range(S5_T):
            src_vreg, src_blk = divmod(i * S5_PAIR_W, LANES)
            src_blk //= S5_PAIR_W
            cols = [_lane_block_shuffle(
                lambda a: s5_ref[a, csub * part:csub * (part + 1), src_vreg * LANES:(src_vreg + 1) * LANES],
                src_blk, w) for w in range(S5_WIDTH // LANES)]
            zrows.append(jnp.concatenate(cols, axis=1))
        z = jnp.concatenate(zrows, axis=0)
        z_hi = z.astype(mdt)
        z_lo = (z - z_hi.astype(F32)).astype(mdt)
        ys = _mm(permt_ref[...], z_hi) + _mm(permt_ref[...], z_lo)
        hs = jax.nn.gelu(ys)
        gate = jax.nn.sigmoid(_mm(hs.astype(mdt), wglu_ref[...]) + bglu_ref[...])
        s5 = (hs * gate).astype(mdt)
        return _mm(att_ref[rows, :], woa_ref[...]) + _mm(ret_ref[rows, :], wor_ref[...]) + _mm(s5, wos_ref[...])

    def norm1(part, ox):
        rows = slice(sub * part, sub * (part + 1))
        x1 = _layer_norm(DEEPNORM_ALPHA * load_x(rows) + mod_ref[2:3, :] * ox, g1_ref[...], b1_ref[...])
        return x1, (x1 * (1.0 + mod_ref[4:5, :]) + mod_ref[3:4, :]).astype(mdt)

    def ff(h, c):
        a = _mm(h, w1_ref[:, FF_CHUNK * c:FF_CHUNK * (c + 1)])
        a = jnp.square(jnp.maximum(a, 0.0)).astype(mdt)
        return _mm(a, w2_ref[FF_CHUNK * c:FF_CHUNK * (c + 1), :])

    def norm2(part, x1, acc):
        rows = slice(sub * part, sub * (part + 1))
        o_ref[rows, :] = _layer_norm(DEEPNORM_ALPHA * x1 + mod_ref[5:6, :] * acc, g2_ref[...], b2_ref[...])

    nstage = nff + 3
    state = [dict() for _ in range(POST_SPLIT)]
    for part, stage in POST_PROGRAM:
        st = state[part]
        if stage == 0:
            st["ox"] = mix(part)
        elif stage == 1:
            st["x1"], st["h"] = norm1(part, st.pop("ox"))
        elif stage < nstage - 1:
            term = ff(st["h"], stage - 2)
            st["acc"] = term if stage == 2 else st["acc"] + term
        else:
            norm2(part, st["x1"], st["acc"])


def _residual_specs(residual):
    tm = ROW_TILE
    if len(residual) == 1:
        return [pl.BlockSpec((tm, D_MODEL), lambda i: (i, 0))]
    head_tiles = CTX_PAD // tm
    return [pl.BlockSpec((tm, D_MODEL), lambda i: (jnp.minimum(i, head_tiles - 1), 0)),
            pl.BlockSpec((tm, D_MODEL), lambda i: (jnp.maximum(i - head_tiles, 0), 0))]


def _post(residual, att, ret, s5_pairs, mods, layer, permt, wglu, bglu, wo, g1, b1, w1, w2, g2, b2, skip_context):
    rows = att.shape[0]
    tm = ROW_TILE
    off = CTX_PAD // tm if skip_context else 0
    assert not (skip_context and len(residual) > 1)
    row_blk = lambda width: pl.BlockSpec((tm, width), lambda i: (i + off, 0))
    full = lambda arr: pl.BlockSpec(arr.shape, lambda i: (0,) * arr.ndim)
    vec = lambda v: v.reshape(1, -1).astype(F32)
    small = [permt, wglu, vec(bglu), wo, vec(g1), vec(b1), w1, w2, vec(g2), vec(b2)]
    res_specs = [row_blk(D_MODEL)] if skip_context else _residual_specs(residual)
    return pl.pallas_call(
        functools.partial(_post_kernel, split=len(residual) > 1),
        grid=(rows // tm - off,),
        in_specs=res_specs + [row_blk(ATT_WIDTH), row_blk(RET_WIDTH),
                              pl.BlockSpec((S5_PAIRS, tm // S5_T, S5_T * S5_PAIR_W), lambda i: (0, i + off, 0)),
                              pl.BlockSpec((None, None, N_ADA, D_MODEL),
                                           lambda i: (layer, jnp.where(i + off == 0, 1, 0), 0, 0))]
                 + [full(arr) for arr in small],
        out_specs=pl.BlockSpec((tm, D_MODEL), lambda i: (i, 0)),
        out_shape=jax.ShapeDtypeStruct((rows - off * tm, D_MODEL), F32),
        compiler_params=_params(("arbitrary",)),
        name="post",
    )(*residual, att, ret, s5_pairs, mods, *small)


def kernel(x, c, ctx, c_ctx, w_ada, b_ada, w_in, att_sink, ret_decay_logit, s5_lambda_re, s5_lambda_im, s5_b_re,
           s5_b_im, s5_c_re, s5_c_im, s5_log_dt, s5_d, w_glu, b_glu, w_out, ln1_g, ln1_b, w_ff1, w_ff2, ln2_g,
           ln2_b):
    assert x.shape[0] == 1 and x.shape[2] == D_MODEL and ctx.shape[1] == CTX_LEN
    seq = x.shape[1]
    assert seq % ROW_TILE == 0
    residual = (jnp.pad(ctx[0], ((0, CTX_PAD - CTX_LEN), (0, 0))), x[0])
    cond =jnp.zeros((8, D_MODEL), F32).at[0].set(c[0]).at[1].set(c_ctx)
    mods = _modulation(cond, w_ada, b_ada).reshape(DEPTH, 8, N_ADA, D_MODEL)
    tabs = _rope_tables(seq)
    masks = _attention_masks()
    perm = _chunk_perm(ROW_TILE // POST_SPLIT, MXU_DTYPE)
    permt = perm.T
    col_scale = jnp.ones((IN_WIDTH,), F32).at[COL_AQ:COL_AK].set(HEAD_DIM ** -0.5 * LOG2E)
    col_scale = col_scale.at[COL_RQ:COL_RK].set(HEAD_DIM ** -0.5)
    s5w = jax.vmap(_s5_weights)(s5_lambda_re, s5_lambda_im, s5_b_re, s5_b_im, s5_c_re, s5_c_im, s5_log_dt, s5_d)
    log_gamma = jax.nn.log_sigmoid(ret_decay_logit.astype(F32))
    for l in range(DEPTH):
        proj, u_pairs = _in_proj(residual, mods, l, (w_in[l] * col_scale).astype(MXU_DTYPE), tabs, perm)
        att = _attention(proj, att_sink[l].astype(F32) * LOG2E, masks)
        ret = _retention(proj, log_gamma[l])
        s5 = _s5_mixer(u_pairs, s5w, l)
        stream = _post(residual, att, ret, s5, mods, l, permt, w_glu[l].astype(MXU_DTYPE), b_glu[l],
                       w_out[l].astype(MXU_DTYPE), ln1_g[l], ln1_b[l], w_ff1[l].astype(MXU_DTYPE),
                       w_ff2[l].astype(MXU_DTYPE), ln2_g[l], ln2_b[l], skip_context=(l == DEPTH - 1))
        residual = (stream,)
    return stream[None]
```

```python
import functools
import math

import jax
import jax.numpy as jnp
from jax import lax
from jax.experimental import pallas as pl
from jax.experimental.pallas import tpu as pltpu

F32 = jnp.float32
MXU_DTYPE = jnp.bfloat16

D_MODEL = 1024
DEPTH = 4
GRID_W = 64
CTX_LEN = 256
CTX_PAD = 512
HEAD_DIM = 64
ATT_HEADS = 8
ATT_KV_HEADS = 2
ATT_BLOCK = 128
ATT_LOOKAHEAD = 2
ROPE_BASE = 10000.0
RET_HEADS = 4
RET_CHUNK = 256
S5_CH = 16
S5_GROUPS = 16
S5_STATE = 64
S5_T = 16
S5_PAIRS = S5_GROUPS // 2
S5_PAIR_W = 2 * S5_CH
S5_TILE = CTX_PAD // S5_T
SCAN_ROWS = 8
S5_STATE_W = 2 * 2 * S5_STATE
ATT_WIDTH = ATT_HEADS * HEAD_DIM
KV_WIDTH = ATT_KV_HEADS * HEAD_DIM
RET_WIDTH = RET_HEADS * HEAD_DIM
S5_WIDTH = S5_GROUPS * S5_CH
IN_WIDTH = ATT_WIDTH + 2 * KV_WIDTH + 4 * RET_WIDTH + S5_WIDTH
D_FF = 4 * D_MODEL
FF_CHUNK = 1024
N_ADA = 6
LN_EPS = 1e-5
GN_EPS = 1e-5
DEEPNORM_ALPHA = (2 * DEPTH) ** 0.25
ROW_TILE = 512
POST_SPLIT = 2
_NFF = D_FF // FF_CHUNK
POST_PROGRAM = (((0, 0), (0, 1), (1, 0), (0, 2), (1, 1)) + tuple((0, 2 + c) for c in range(1, _NFF))
                + ((1, 2), (0, 2 + _NFF)) + tuple((1, 2 + c) for c in range(1, _NFF)) + ((1, 2 + _NFF),))
NEG_BIG = -1e30
LOG2E = math.log2(math.e)
LANES = 128
VMEM_LIMIT = 56 * 1024 * 1024

COL_AQ, COL_AK, COL_AV = 0, ATT_WIDTH, ATT_WIDTH + KV_WIDTH
COL_RQ = ATT_WIDTH + 2 * KV_WIDTH
COL_RK, COL_RV, COL_RG = COL_RQ + RET_WIDTH, COL_RQ + 2 * RET_WIDTH, COL_RQ + 3 * RET_WIDTH
COL_S5 = COL_RQ + 4 * RET_WIDTH


def _mm(a, b):
    return jnp.dot(a, b, preferred_element_type=F32)


def _mm_nt(a, b):
    return lax.dot_general(a, b, (((1,), (1,)), ((), ())), preferred_element_type=F32)


def _mm_tn(a, b):
    return lax.dot_general(a, b, (((0,), (0,)), ((), ())), preferred_element_type=F32)


def _params(sem):
    return pltpu.CompilerParams(dimension_semantics=sem, vmem_limit_bytes=VMEM_LIMIT)


def _mod_kernel(cond_ref, w_ref, b_ref, o_ref):
    c = cond_ref[...]
    s = c * jax.nn.sigmoid(c)
    o_ref[0] = jnp.dot(s, w_ref[0], preferred_element_type=F32, precision=lax.Precision.HIGHEST) + b_ref[0]


def _modulation(cond, w_ada, b_ada):
    tn = 1536
    n = N_ADA * D_MODEL
    return pl.pallas_call(
        _mod_kernel,
        grid=(DEPTH, n // tn),
        in_specs=[
            pl.BlockSpec((8, D_MODEL), lambda l, j: (0, 0)),
            pl.BlockSpec((1, D_MODEL, tn), lambda l, j: (l, 0, j)),
            pl.BlockSpec((1, 1, tn), lambda l, j: (l, 0, j)),
        ],
        out_specs=pl.BlockSpec((1, 8, tn), lambda l, j: (l, 0, j)),
        out_shape=jax.ShapeDtypeStruct((DEPTH, 8, n), F32),
        compiler_params=_params(("arbitrary", "arbitrary")),
        name="modulation",
    )(cond, w_ada, b_ada.reshape(DEPTH, 1, n))


def _lane_block_shuffle(src_rows, src_lane_blk, out_vreg):
    acc = None
    for q in range(LANES // S5_PAIR_W):
        piece = src_rows(out_vreg * (LANES // S5_PAIR_W) + q)
        shift = (S5_PAIR_W * (q - src_lane_blk)) % LANES
        if shift:
            piece = pltpu.roll(piece, shift, 1)
        if acc is None:
            acc = piece
        else:
            lane_blk = lax.broadcasted_iota(jnp.int32, piece.shape, 1) // S5_PAIR_W
            acc = jnp.where(lane_blk == q, piece, acc)
    return acc


def _in_proj_kernel(*refs, split):
    nres = 2 if split else 1
    mod_ref, w_ref, ca_ref, sa_ref, cr_ref, sr_ref, perm_ref, o_ref, u_ref = refs[nres:]
    x = _residual_rows(refs[:nres])(slice(None))
    h = (x * (1.0 + mod_ref[1:2, :]) + mod_ref[0:1, :]).astype(w_ref.dtype)
    lane = lax.broadcasted_iota(jnp.int32, (x.shape[0], LANES), 1)
    first_att = (lane & 31) < 16
    first_ret = (lane & 63) < 32

    def proj(c0, c1):
        return _mm(h, w_ref[:, c0:c1])

    def rope_store(p, c0, width, cos, sin, first, half):
        for b in range(width // LANES):
            blk = p[:, LANES * b:LANES * (b + 1)]
            rot = jnp.where(first, pltpu.roll(blk, LANES - half, 1), pltpu.roll(blk, half, 1))
            o_ref[:, c0 + LANES * b:c0 + LANES * (b + 1)] = (blk * cos + rot * sin).astype(o_ref.dtype)

    def plain_store(p, c0, width):
        o_ref[:, c0:c0 + width] = p.astype(o_ref.dtype)

    u = proj(COL_S5, IN_WIDTH).astype(w_ref.dtype)
    sub = perm_ref.shape[0]
    nchunk = sub // S5_T
    for part in range(x.shape[0] // sub):
        g = _mm(perm_ref[...], u[sub * part:sub * (part + 1)])
        for a in range(S5_PAIRS):
            vreg_col, lane_blk = divmod(a * S5_PAIR_W, LANES)
            lane_blk //= S5_PAIR_W
            for v in range(S5_T * S5_PAIR_W // LANES):
                slab = _lane_block_shuffle(
                    lambda j: g[nchunk * j:nchunk * (j + 1), vreg_col * LANES:(vreg_col + 1) * LANES], lane_blk, v)
                u_ref[a, nchunk * part:nchunk * (part + 1), LANES * v:LANES * (v + 1)] = slab.astype(u_ref.dtype)

    ca, sa, cr, sr = ca_ref[...], sa_ref[...], cr_ref[...], sr_ref[...]
    att_rope = (ca, sa, first_att, HEAD_DIM // 4)
    ret_rope = (cr, sr, first_ret, HEAD_DIM // 2)
    groups = [(COL_AQ, ATT_WIDTH, att_rope), (COL_AK, KV_WIDTH, att_rope), (COL_RQ, RET_WIDTH, ret_rope),
              (COL_RK, RET_WIDTH, ret_rope), (COL_AV, KV_WIDTH, None), (COL_RV, COL_S5 - COL_RV, None)]
    pending = None
    for group in groups + [None]:
        nxt = None if group is None else (proj(group[0], group[0] + group[1]),) + group
        if pending is not None:
            p, pc0, pwidth, prope = pending
            if prope is None:
                plain_store(p, pc0, pwidth)
            else:
                rope_store(p, pc0, pwidth, *prope)
        pending = nxt


def _chunk_perm(tile_rows, dtype):
    nchunk = tile_rows // S5_T
    r = jnp.arange(tile_rows)
    src = S5_T * (r % nchunk) + r // nchunk
    return (src[:, None] == jnp.arange(tile_rows)[None, :]).astype(dtype)


def _in_proj(residual, mods, layer, w_in, tabs, perm):
    rows = tabs[0].shape[0]
    tm = ROW_TILE
    tab_spec = pl.BlockSpec((tm, LANES), lambda i: (i, 0))
    nch = rows // S5_T
    return pl.pallas_call(
        functools.partial(_in_proj_kernel, split=len(residual) > 1),
        grid=(rows // tm,),
        in_specs=_residual_specs(residual) + [
            pl.BlockSpec((None, None, N_ADA, D_MODEL), lambda i: (layer, jnp.where(i == 0, 1, 0), 0, 0)),
            pl.BlockSpec((D_MODEL, IN_WIDTH), lambda i: (0, 0)),
            tab_spec, tab_spec, tab_spec, tab_spec,
            pl.BlockSpec(perm.shape, lambda i: (0, 0)),
        ],
        out_specs=[pl.BlockSpec((tm, COL_S5), lambda i: (i, 0)),
                   pl.BlockSpec((S5_PAIRS, tm // S5_T, S5_T * S5_PAIR_W), lambda i: (0, i, 0))],
        out_shape=[jax.ShapeDtypeStruct((rows, COL_S5), MXU_DTYPE),
                   jax.ShapeDtypeStruct((S5_PAIRS, nch, S5_T * S5_PAIR_W), MXU_DTYPE)],
        compiler_params=_params(("arbitrary",)),
        name="in_proj",
    )(*residual, mods, w_in, *tabs, perm)


def _rope_tables(seq):
    half_a = HEAD_DIM // 4
    half_r = HEAD_DIM // 2
    nrow = seq // GRID_W
    inv_a = ROPE_BASE ** (-jnp.arange(half_a, dtype=F32) / half_a)
    inv_r = ROPE_BASE ** (-jnp.arange(half_r, dtype=F32) / half_r)
    ang_r = jnp.arange(nrow, dtype=F32)[:, None] * inv_a[None, :]
    ang_c = jnp.arange(GRID_W, dtype=F32)[:, None] * inv_a[None, :]
    ang_t = jnp.arange(seq, dtype=F32)[:, None] * inv_r[None, :]
    hp = lax.Precision.HIGHEST
    lane = jnp.arange(LANES)
    within = lane % HEAD_DIM
    pick_a = (within % half_a)[None, :] == jnp.arange(half_a)[:, None]
    exp_row = (pick_a & (within < 2 * half_a)[None, :]).astype(F32)
    exp_col = (pick_a & (within >= 2 * half_a)[None, :]).astype(F32)
    exp_t = ((lane % half_r)[None, :] == jnp.arange(half_r)[:, None]).astype(F32)
    sign_a = jnp.where(within % (2 * half_a) < half_a, -1.0, 1.0).astype(F32)
    sign_r = jnp.where(within < half_r, -1.0, 1.0).astype(F32)

    def att_table(fn):
        by_row = jnp.dot(fn(ang_r), exp_row, precision=hp)
        by_col = jnp.dot(fn(ang_c), exp_col, precision=hp)
        return (by_row[:, None, :] + by_col[None, :, :]).reshape(seq, LANES)

    cos_a = att_table(jnp.cos)
    sin_a = att_table(jnp.sin) * sign_a
    cos_r = jnp.dot(jnp.cos(ang_t), exp_t, precision=hp)
    sin_r = jnp.dot(jnp.sin(ang_t), exp_t, precision=hp) * sign_r
    pad = lambda tab, ident: jnp.pad(tab, ((CTX_PAD, 0), (0, 0)), constant_values=ident)
    return pad(cos_a, 1.0), pad(sin_a, 0.0), pad(cos_r, 1.0), pad(sin_r, 0.0)


def _swap_halves(x):
    if x.dtype.itemsize == 4:
        return pltpu.roll(x, 64, 1)
    packed = pltpu.bitcast(x, jnp.uint32)
    return pltpu.bitcast(pltpu.roll(packed, 64, 1), x.dtype)


def _dup_heads(x):
    sw = _swap_halves(x)
    lo = lax.broadcasted_iota(jnp.int32, x.shape, 1) < HEAD_DIM
    return jnp.where(lo, x, sw), jnp.where(lo, sw, x)


def _attn_kernel(sink_ref, q_ref, km_ref, kp_ref, kn_ref, vm_ref, vp_ref, vn_ref, kc_ref, vc_ref, mask_ref,
                 o_ref, k2_ref, v2_ref, kc2_ref, vc2_ref):
    i = pl.program_id(0)
    last_blk = pl.num_programs(0) * (ROW_TILE // ATT_BLOCK) - 1
    blk = ATT_BLOCK
    def spread(src, ones_upper):
        x = src[...]
        a, b = _dup_heads(x)
        if ones_upper:
            upper = lax.broadcasted_iota(jnp.int32, x.shape, 1) >= HEAD_DIM
            a = jnp.where(upper, jnp.ones_like(a), a)
            b = jnp.where(upper, jnp.ones_like(b), b)
        return a, b

    for dst, parts, is_v in ((k2_ref, (kp_ref, km_ref, kn_ref), False), (v2_ref, (vp_ref, vm_ref, vn_ref), True)):
        row = 0
        for part in parts:
            a, b = spread(part, is_v)
            n = part.shape[0]
            dst[0, row:row + n, :] = a
            dst[1, row:row + n, :] = b
            row += n
    for dst, src, is_v in ((kc2_ref, kc_ref, False), (vc2_ref, vc_ref, True)):
        a, b = spread(src, is_v)
        dst[0] = a
        dst[1] = b

    lo = lax.broadcasted_iota(jnp.int32, (blk, LANES), 1) < HEAD_DIM
    group = ATT_HEADS // ATT_KV_HEADS

    nloc = 3 * blk

    def scores(j, kv):
        r0 = j * blk
        qt = q_ref[r0:r0 + blk, group * HEAD_DIM * kv:group * HEAD_DIM * (kv + 1)]
        parts = []
        for g in range(group):
            qc = qt[:, LANES * (g // 2):LANES * (g // 2 + 1)]
            keep = lo if g % 2 == 0 else jnp.logical_not(lo)
            parts.append(jnp.where(keep, qc, jnp.zeros_like(qc)))
        qs = jnp.concatenate(parts, axis=0)
        return _mm_nt(qs, k2_ref[kv, r0:r0 + nloc, :]), _mm_nt(qs, kc2_ref[kv])

    def finish(j, kv, s_loc, s_ctx):
        r0 = j * blk
        gblk = i * (ROW_TILE // blk) + j
        sel = jnp.where(i == 0, 3, jnp.where(gblk == CTX_PAD // blk, 0, jnp.where(gblk == last_blk, 2, 1)))
        bias = mask_ref[sel]
        probs, sink_w = [], []
        for g in range(group):
            s = jnp.concatenate([s_loc[blk * g:blk * (g + 1)] + bias, s_ctx[blk * g:blk * (g + 1)]], axis=1)
            sk = sink_ref[group * kv + g]
            m = jnp.maximum(jnp.max(s, axis=-1, keepdims=True), sk)
            probs.append(jnp.exp2(s - m).astype(o_ref.dtype))
            sink_w.append(jnp.exp2(sk - m))
        p = jnp.concatenate(probs, axis=0)
        o = _mm(p[:, :nloc], v2_ref[kv, r0:r0 + nloc, :]) + _mm(p[:, nloc:], vc2_ref[kv])
        for half in range(group // 2):
            even, odd = 2 * half, 2 * half + 1
            oe = o[blk * even:blk * (even + 1)]
            oo = o[blk * odd:blk * (odd + 1)]
            y_even = oe * (1.0 / (pltpu.roll(oe, HEAD_DIM, 1) + sink_w[even]))
            y_odd = pltpu.roll(oo, HEAD_DIM, 1) * (1.0 / (oo + sink_w[odd]))
            c0 = group * HEAD_DIM * kv + LANES * half
            o_ref[r0:r0 + blk, c0:c0 + LANES] = jnp.where(lo, y_even, y_odd).astype(o_ref.dtype)

    items = [(j, kv) for j in range(ROW_TILE // blk) for kv in range(ATT_KV_HEADS)]
    pending = {}
    for t in range(len(items) + ATT_LOOKAHEAD):
        if t < len(items):
            pending[t] = scores(*items[t])
        if t >= ATT_LOOKAHEAD:
            finish(*items[t - ATT_LOOKAHEAD], *pending.pop(t - ATT_LOOKAHEAD))


def _attention_masks():
    qi = jnp.arange(ATT_BLOCK)[:, None]
    kj = jnp.arange(3 * ATT_BLOCK)[None, :]
    band = jnp.abs(kj - ATT_BLOCK - qi) <= ATT_BLOCK
    first = band & (kj >= ATT_BLOCK)
    last = band & (kj < 2 * ATT_BLOCK)
    none = jnp.zeros_like(band)
    masks = jnp.stack([first, band, last, none])
    return jnp.where(masks, 0.0, NEG_BIG).astype(F32)


def _attention(proj, sink, masks):
    rows = proj.shape[0]
    tm, blk = ROW_TILE, ATT_BLOCK
    per = tm // blk
    nblk = rows // blk
    ck, cv = COL_AK // KV_WIDTH, COL_AV // KV_WIDTH
    dt = proj.dtype
    return pl.pallas_call(
        _attn_kernel,
        grid=(rows // tm,),
        in_specs=[
            pl.BlockSpec(memory_space=pltpu.SMEM),
            pl.BlockSpec((tm, ATT_WIDTH), lambda i: (i, 0)),
            pl.BlockSpec((tm, KV_WIDTH), lambda i: (i, ck)),
            pl.BlockSpec((blk, KV_WIDTH), lambda i: (jnp.maximum(i * per - 1, 0), ck)),
            pl.BlockSpec((blk, KV_WIDTH), lambda i: (jnp.minimum((i + 1) * per, nblk - 1), ck)),
            pl.BlockSpec((tm, KV_WIDTH), lambda i: (i, cv)),
            pl.BlockSpec((blk, KV_WIDTH), lambda i: (jnp.maximum(i * per - 1, 0), cv)),
            pl.BlockSpec((blk, KV_WIDTH), lambda i: (jnp.minimum((i + 1) * per, nblk - 1), cv)),
            pl.BlockSpec((CTX_LEN, KV_WIDTH), lambda i: (0, ck)),
            pl.BlockSpec((CTX_LEN, KV_WIDTH), lambda i: (0, cv)),
            pl.BlockSpec((4, blk, 3 * blk), lambda i: (0, 0, 0)),
        ],
        out_specs=pl.BlockSpec((tm, ATT_WIDTH), lambda i: (i, 0)),
        out_shape=jax.ShapeDtypeStruct((rows, ATT_WIDTH), dt),
        scratch_shapes=[
            pltpu.VMEM((2, tm + 2 * blk, KV_WIDTH), dt),
            pltpu.VMEM((2, tm + 2 * blk, KV_WIDTH), dt),
            pltpu.VMEM((2, CTX_LEN, KV_WIDTH), dt),
            pltpu.VMEM((2, CTX_LEN, KV_WIDTH), dt),
        ],
        compiler_params=_params(("arbitrary",)),
        name="attention",
    )(sink, proj, proj, proj, proj, proj, proj, proj, proj, proj, masks)


def _ret_kernel(lg_ref, q_ref, k_ref, v_ref, g_ref, o_ref, sb_ref, s_ref, dm_ref, tab_ref, gbd_ref):
    ph = pl.program_id(0)
    t = pl.program_id(1)
    ntile = pl.num_programs(1)
    c = RET_CHUNK
    w = RET_WIDTH
    per = q_ref.shape[0] // c
    mdt = sb_ref.dtype
    rows = lambda ci: slice(c * ci, c * (ci + 1))

    def lane_vec(direction, shape, axis):
        head = lax.broadcasted_iota(jnp.int32, shape, axis) // HEAD_DIM
        out = jnp.full(shape, lg_ref[direction, RET_HEADS - 1], F32)
        for h in range(RET_HEADS - 2, -1, -1):
            out = jnp.where(head == h, lg_ref[direction, h], out)
        return out

    @pl.when(jnp.logical_and(ph == 0, t == 0))
    def _init_tables():
        diff = (lax.broadcasted_iota(jnp.int32, (c, c), 0) - lax.broadcasted_iota(jnp.int32, (c, c), 1)).astype(F32)
        for h in range(RET_HEADS):
            dm_ref[h] = jnp.exp(jnp.where(diff >= 0, diff * lg_ref[0, h], -diff * lg_ref[1, h]))
        pos = lax.broadcasted_iota(jnp.int32, (c, w), 0).astype(F32)
        lgf = lane_vec(0, (c, w), 1)
        lgb = lane_vec(1, (c, w), 1)
        tab_ref[0] = jnp.exp((c - 1.0 - pos) * lgf)
        tab_ref[1] = jnp.exp((pos + 1.0) * lgf)
        tab_ref[2] = jnp.exp(pos * lgb)
        tab_ref[3] = jnp.exp((c - pos) * lgb)
        same = (lax.broadcasted_iota(jnp.int32, (w, w), 0) // HEAD_DIM
                == lax.broadcasted_iota(jnp.int32, (w, w), 1) // HEAD_DIM)
        bd = jnp.where(same, 1.0, 0.0)
        gbd_ref[0] = bd * jnp.exp(c * lane_vec(0, (w, w), 0))
        gbd_ref[1] = bd * jnp.exp(c * lane_vec(1, (w, w), 0))
        gbd_ref[2] = bd

    @pl.when(t == 0)
    def _reset_state():
        s_ref[...] = jnp.zeros_like(s_ref)

    def state_update(direction, key_tab, ci):
        kw = (k_ref[rows(ci), :].astype(F32) * tab_ref[key_tab]).astype(mdt)
        u = _mm_tn(kw, v_ref[rows(ci), :])
        s_ref[...] = gbd_ref[direction] * s_ref[...] + gbd_ref[2] * u

    @pl.when(jnp.logical_and(ph == 0, t == 0))
    def _backward_context():
        sb_ref[0] = s_ref[...].astype(mdt)
        state_update(1, 2, 0)

    @pl.when(jnp.logical_and(ph == 0, t > 0))
    def _backward_latent():
        base = 1 + per * (ntile - t - 1)
        for ci in range(per - 1, -1, -1):
            sb_ref[base + ci] = s_ref[...].astype(mdt)
            state_update(1, 2, ci)

    head = lax.broadcasted_iota(jnp.int32, (c, w), 1) // HEAD_DIM

    def scores(ci):
        q = q_ref[rows(ci), :]
        qs = jnp.concatenate([jnp.where(head == h, q, jnp.zeros_like(q)) for h in range(RET_HEADS)], axis=0)
        return _mm_nt(qs, k_ref[rows(ci), :])

    def intra(ci, sc):
        scd = jnp.concatenate([sc[c * h:c * (h + 1)] * dm_ref[h] for h in range(RET_HEADS)], axis=0).astype(mdt)
        oi = _mm(scd, v_ref[rows(ci), :])
        o = jnp.where(head == 0, oi[0:c], 0.0)
        for h in range(1, RET_HEADS):
            o = o + jnp.where(head == h, oi[c * h:c * (h + 1)], 0.0)
        return o

    def cross(ci, idx):
        qf = q_ref[rows(ci), :].astype(F32)
        return (_mm((qf * tab_ref[1]).astype(mdt), s_ref[...].astype(mdt))
                + _mm((qf * tab_ref[3]).astype(mdt), sb_ref[idx]))

    def finish(ci, o):
        avg = (gbd_ref[2] * (1.0 / HEAD_DIM)).astype(mdt)
        o_hi = o.astype(mdt)
        d = o - (_mm(o_hi, avg) + _mm((o - o_hi.astype(F32)).astype(mdt), avg))
        var = _mm((d * d).astype(mdt), avg)
        gate = g_ref[rows(ci), :].astype(F32)
        y = d * lax.rsqrt(var + GN_EPS) * (gate * jax.nn.sigmoid(gate))
        o_ref[rows(ci), :] = y.astype(o_ref.dtype)

    @pl.when(jnp.logical_and(ph == 1, t == 0))
    def _forward_context():
        o = intra(0, scores(0)) + cross(0, 0)
        state_update(0, 0, 0)
        finish(0, o)
        for ci in range(1, per):
            o_ref[rows(ci), :] = jnp.zeros((c, w), o_ref.dtype)

    @pl.when(jnp.logical_and(ph == 1, t > 0))
    def _forward_latent():
        base = 1 + per * (t - 1)
        sc = [scores(ci) for ci in range(per)]
        outs = [intra(ci, sc[ci]) for ci in range(per)]
        for ci in range(per):
            outs[ci] = outs[ci] + cross(ci, base + ci)
            state_update(0, 0, ci)
        for ci in range(per):
            finish(ci, outs[ci])


def _retention(proj, log_gamma):
    rows = proj.shape[0]
    c = RET_CHUNK
    tm = ROW_TILE
    ntile = rows // tm
    nchunk = 1 + (rows - CTX_PAD) // c
    dt = proj.dtype
    cq, ckk, cvv, cg = (COL_RQ // RET_WIDTH, COL_RK // RET_WIDTH, COL_RV // RET_WIDTH, COL_RG // RET_WIDTH)

    def kv_blk(ph, t):
        return jnp.where(ph == 0, jnp.where(t == 0, 0, ntile - t), t)

    def fw_blk(ph, t):
        return jnp.where(ph == 0, 0, t)

    return pl.pallas_call(
        _ret_kernel,
        grid=(2, ntile),
        in_specs=[
            pl.BlockSpec(memory_space=pltpu.SMEM),
            pl.BlockSpec((tm, RET_WIDTH), lambda ph, t: (fw_blk(ph, t), cq)),
            pl.BlockSpec((tm, RET_WIDTH), lambda ph, t: (kv_blk(ph, t), ckk)),
            pl.BlockSpec((tm, RET_WIDTH), lambda ph, t: (kv_blk(ph, t), cvv)),
            pl.BlockSpec((tm, RET_WIDTH), lambda ph, t: (fw_blk(ph, t), cg)),
        ],
        out_specs=pl.BlockSpec((tm, RET_WIDTH), lambda ph, t: (fw_blk(ph, t), 0)),
        out_shape=jax.ShapeDtypeStruct((rows, RET_WIDTH), dt),
        scratch_shapes=[
            pltpu.VMEM((nchunk, RET_WIDTH, RET_WIDTH), dt),
            pltpu.VMEM((RET_WIDTH, RET_WIDTH), F32),
            pltpu.VMEM((RET_HEADS, c, c), F32),
            pltpu.VMEM((4, c, RET_WIDTH), F32),
            pltpu.VMEM((3, RET_WIDTH, RET_WIDTH), F32),
        ],
        compiler_params=_params(("arbitrary", "arbitrary")),
        name="retention",
    )(log_gamma, proj, proj, proj, proj)


def _s5_weights(lam_re, lam_im, b_re, b_im, c_re, c_im, log_dt, d_skip):
    hp = lax.Precision.HIGHEST
    tt, g, n, p, a = S5_T, S5_GROUPS, S5_STATE, S5_CH, S5_PAIRS
    lam = lax.complex(lam_re.astype(F32), lam_im.astype(F32))
    dtv = jnp.exp(log_dt.astype(F32))[..., None]
    lam_bar = jnp.exp(lam * dtv)
    bbar = ((lam_bar - 1.0) / lam)[..., None] * lax.complex(b_re.astype(F32), b_im.astype(F32))
    cmat = lax.complex(c_re.astype(F32), c_im.astype(F32))
    pw = [jnp.ones_like(lam_bar)]
    for _ in range(tt):
        pw.append(pw[-1] * lam_bar)
    pw = jnp.stack(pw, axis=1)
    eye2 = jnp.eye(2, dtype=F32)
    ri = lambda z, axis: jnp.stack([jnp.real(z), jnp.imag(z)], axis=axis)

    pw_l = pw.reshape(2, tt + 1, a, 2 * n)
    bbt = jnp.einsum('dahpn,gh->dagphn', jnp.swapaxes(bbar, -1, -2).reshape(2, a, 2, p, n), eye2)
    bbt = bbt.reshape(2, a, 2 * p, 2 * n)
    cct = jnp.einsum('dahpn,gh->dagphn', cmat.reshape(2, a, 2, p, n), eye2).reshape(2, a, 2 * p, 2 * n)
    pw_k = ri(pw_l, 1).transpose(3, 0, 1, 2, 4)
    b_k = ri(bbt, 2).transpose(1, 0, 2, 3, 4)
    c_k = ri(cct, 2).transpose(1, 0, 2, 3, 4)
    decay = [pw_l[:, tt]]
    for _ in range(SCAN_ROWS - 1):
        decay.append(decay[-1] * decay[0])
    decay = jnp.stack(decay, axis=0)
    rows8 = lambda z: jnp.broadcast_to(z[None], (SCAN_ROWS,) + z.shape)
    carry_w = jnp.stack([decay[:, 0], decay[::-1, 1]], axis=1)
    scan_tab = jnp.stack([rows8(decay[0]), rows8(decay[1]), rows8(decay[3]), carry_w], axis=0)
    scan_tab = ri(scan_tab, 0).transpose(4, 3, 1, 0, 2, 5)
    skip = jnp.tile(d_skip.astype(F32).reshape(a, 1, 2 * p), (1, tt, 1)).reshape(a, 1, tt * 2 * p)
    return pw_k, b_k, c_k, scan_tab, skip


def _pair_spec(layer, *shape):
    return pl.BlockSpec((None, None) + shape, lambda i: (layer, i) + (0,) * len(shape))


def _s5_drive_kernel(u_ref, pw_ref, b_ref, o_ref, w_ref):
    rows, half = S5_PAIR_W, LANES
    for d in range(2):
        br, bi = b_ref[d, 0], b_ref[d, 1]
        for j in range(S5_T):
            e = S5_T - 1 - j if d == 0 else j
            pr, pi = pw_ref[d, 0, e:e + 1, :], pw_ref[d, 1, e:e + 1, :]
            w_ref[rows * j:rows * (j + 1), 2 * half * d:2 * half * d + half] = (pr * br - pi * bi).astype(w_ref.dtype)
            w_ref[rows * j:rows * (j + 1), 2 * half * d + half:2 * half * (d + 1)] = (
                pr * bi + pi * br).astype(w_ref.dtype)
    o_ref[0] = _mm(u_ref[0], w_ref[...])


def _s5_drive(u_pairs, pw_k, b_k, layer):
    a, nch, wd = u_pairs.shape
    return pl.pallas_call(
        _s5_drive_kernel,
        grid=(a,),
        in_specs=[pl.BlockSpec((1, nch, wd), lambda i: (i, 0, 0)),
                  _pair_spec(layer, *pw_k.shape[2:]), _pair_spec(layer, *b_k.shape[2:])],
        out_specs=pl.BlockSpec((1, nch, 2 * S5_STATE_W), lambda i: (i, 0, 0)),
        out_shape=jax.ShapeDtypeStruct((a, nch, 2 * S5_STATE_W), F32),
        scratch_shapes=[pltpu.VMEM((wd, 2 * S5_STATE_W), u_pairs.dtype)],
        compiler_params=_params(("arbitrary",)),
        name="s5_drive",
    )(u_pairs, pw_k, b_k)


def _s5_scan_kernel(ef_ref, eb_ref, tab_ref, sf_ref, sb_ref, st_ref):
    t = pl.program_id(0)
    hw = LANES
    sub = SCAN_ROWS
    npair = tab_ref.shape[0]
    row = lax.broadcasted_iota(jnp.int32, (sub, hw), 0)

    @pl.when(t == 0)
    def _reset():
        st_ref[...] = jnp.zeros_like(st_ref)

    def shift(x, k, reverse):
        if reverse:
            return jnp.where(row < sub - k, pltpu.roll(x, sub - k, 0), 0.0)
        return jnp.where(row >= k, pltpu.roll(x, k, 0), 0.0)

    def scan_group(e_ref, o_ref, a, d, r0, cr, ci):
        reverse = d == 1
        xr = e_ref[a, pl.ds(r0, sub), 0:hw]
        xi = e_ref[a, pl.ds(r0, sub), hw:2 * hw]
        for step, k in enumerate((1, 2, 4)):
            ar, ai = tab_ref[a, d, step, 0], tab_ref[a, d, step, 1]
            sr, si = shift(xr, k, reverse), shift(xi, k, reverse)
            xr, xi = xr + ar * sr - ai * si, xi + ar * si + ai * sr
        wr, wi = tab_ref[a, d, 3, 0], tab_ref[a, d, 3, 1]
        fr = xr + wr * cr - wi * ci
        fi = xi + wr * ci + wi * cr
        edge = sub - 1 if reverse else 0
        o_ref[a, pl.ds(r0, sub), 0:hw] = jnp.where(row == edge, cr, shift(fr, 1, reverse))
        o_ref[a, pl.ds(r0, sub), hw:2 * hw] = jnp.where(row == edge, ci, shift(fi, 1, reverse))
        last = 0 if reverse else sub - 1
        return (jnp.broadcast_to(fr[last:last + 1], (sub, hw)), jnp.broadcast_to(fi[last:last + 1], (sub, hw)))

    def run(nrows):
        ngroups = nrows // sub

        def body(gi, carry):
            r0 = pl.multiple_of(gi * sub, sub)
            rb0 = pl.multiple_of((ngroups - 1 - gi) * sub, sub)
            new = []
            for a in range(npair):
                fr, fi, br, bi = carry[a]
                fr, fi = scan_group(ef_ref, sf_ref, a, 0, r0, fr, fi)
                br, bi = scan_group(eb_ref, sb_ref, a, 1, rb0, br, bi)
                new.append((fr, fi, br, bi))
            return tuple(new)

        init = tuple(tuple(st_ref[a, k] for k in range(4)) for a in range(npair))
        final = lax.fori_loop(0, ngroups, body, init)
        for a in range(npair):
            for k in range(4):
                st_ref[a, k] = final[a][k]

    @pl.when(t == 0)
    def _context():
        sf_ref[...] = jnp.zeros_like(sf_ref)
        sb_ref[...] = jnp.zeros_like(sb_ref)
        run(CTX_LEN // S5_T)

    @pl.when(t > 0)
    def _latent():
        run(S5_TILE)


def _s5_scan(drive, scan_tab, layer):
    a, nch, wd2 = drive.shape
    wd = wd2 // 2
    nt = nch // S5_TILE

    def bwd(t):
        return jnp.where(t == 0, 0, nt - t)

    return pl.pallas_call(
        _s5_scan_kernel,
        grid=(nt,),
        in_specs=[pl.BlockSpec((a, S5_TILE, wd), lambda t: (0, t, 0)),
                  pl.BlockSpec((a, S5_TILE, wd), lambda t: (0, bwd(t), 1)),
                  pl.BlockSpec((None,) + scan_tab.shape[1:], lambda t: (layer,) + (0,) * (scan_tab.ndim - 1))],
        out_specs=[pl.BlockSpec((a, S5_TILE, wd), lambda t: (0, t, 0)),
                   pl.BlockSpec((a, S5_TILE, wd), lambda t: (0, bwd(t), 0))],
        out_shape=[jax.ShapeDtypeStruct((a, nch, wd), F32), jax.ShapeDtypeStruct((a, nch, wd), F32)],
        scratch_shapes=[pltpu.VMEM((a, 4, SCAN_ROWS, LANES), F32)],
        compiler_params=_params(("arbitrary",)),
        name="s5_scan",
    )(drive, drive, scan_tab)


def _lane_window(x, start, width):
    cols = []
    for v in range(width // LANES):
        k0, off = divmod(start + LANES * v, LANES)
        lo = x[:, LANES * k0:LANES * (k0 + 1)]
        if off:
            hi = x[:, LANES * (k0 + 1):LANES * (k0 + 2)]
            lane = lax.broadcasted_iota(jnp.int32, lo.shape, 1)
            lo = jnp.where(lane < LANES - off, pltpu.roll(lo, LANES - off, 1), pltpu.roll(hi, LANES - off, 1))
        cols.append(lo)
    return jnp.concatenate(cols, axis=1)


def _s5_read_kernel(u_ref, sf_ref, sb_ref, pw_ref, b_ref, c_ref, skip_ref, o_ref, wt_ref, wi_ref, lag_ref):
    u = u_ref[0]
    mdt = u.dtype
    rows, half = S5_PAIR_W, LANES
    for d in range(2):
        cr, ci = c_ref[d, 0], c_ref[d, 1]
        for i in range(S5_T):
            e = i + 1 if d == 0 else S5_T - i
            pr, pi = pw_ref[d, 0, e:e + 1, :], pw_ref[d, 1, e:e + 1, :]
            wt_ref[d, rows * i:rows * (i + 1), 0:half] = (pr * cr - pi * ci).astype(mdt)
            wt_ref[d, rows * i:rows * (i + 1), half:2 * half] = (-(pr * ci + pi * cr)).astype(mdt)
    nlag = 2 * S5_T - 1
    zero = jnp.zeros((rows, half), F32)
    for l in range(nlag + 1):
        lag = l - (S5_T - 1)
        for d, active in ((1, lag <= 0), (0, 0 <= lag < S5_T)):
            col = 2 * half * (1 - d)
            if active:
                cr, ci = c_ref[d, 0], c_ref[d, 1]
                pr, pi = pw_ref[d, 0, abs(lag):abs(lag) + 1, :], pw_ref[d, 1, abs(lag):abs(lag) + 1, :]
                lag_ref[rows * l:rows * (l + 1), col:col + half] = pr * cr - pi * ci
                lag_ref[rows * l:rows * (l + 1), col + half:col + 2 * half] = pr * ci + pi * cr
            else:
                lag_ref[rows * l:rows * (l + 1), col:col + half] = zero
                lag_ref[rows * l:rows * (l + 1), col + half:col + 2 * half] = zero
    lhs = jnp.concatenate([b_ref[1, 0], -b_ref[1, 1], b_ref[0, 0], -b_ref[0, 1]], axis=1)
    kall = lax.dot_general(lhs, lag_ref[...], (((1,), (1,)), ((), ())), preferred_element_type=F32,
                           precision=lax.Precision.HIGHEST)
    for j in range(S5_T):
        wi_ref[rows * j:rows * (j + 1), :] = _lane_window(kall, rows * (S5_T - 1 - j), S5_T * rows).astype(mdt)
    y = _mm(u, wi_ref[...])
    y = y + _mm_nt(sf_ref[0].astype(mdt), wt_ref[0])
    y = y + _mm_nt(sb_ref[0].astype(mdt), wt_ref[1])
    o_ref[0] = y + u.astype(F32) * skip_ref[...]


def _s5_read(u_pairs, sf, sb, pw_k, b_k, c_k, skip, layer):
    a, nch, wd = u_pairs.shape
    blk = lambda *shape: pl.BlockSpec((1,) + shape, lambda i: (i, 0, 0))
    return pl.pallas_call(
        _s5_read_kernel,
        grid=(a,),
        in_specs=[blk(nch, wd), blk(nch, S5_STATE_W), blk(nch, S5_STATE_W), _pair_spec(layer, *pw_k.shape[2:]),
                  _pair_spec(layer, *b_k.shape[2:]), _pair_spec(layer, *c_k.shape[2:]),
                  _pair_spec(layer, *skip.shape[2:])],
        out_specs=blk(nch, wd),
        out_shape=jax.ShapeDtypeStruct((a, nch, wd), F32),
        scratch_shapes=[pltpu.VMEM((2, wd, S5_STATE_W), u_pairs.dtype), pltpu.VMEM((wd, wd), u_pairs.dtype),
                        pltpu.VMEM((2 * wd, 2 * S5_STATE_W), F32)],
        compiler_params=_params(("arbitrary",)),
        name="s5_read",
    )(u_pairs, sf, sb, pw_k, b_k, c_k, skip)


def _s5_mixer(u_pairs, weights, layer):
    pw_k, b_k, c_k, scan_tab, skip = weights
    drive = _s5_drive(u_pairs, pw_k, b_k, layer)
    sf, sb = _s5_scan(drive, scan_tab, layer)
    return _s5_read(u_pairs, sf, sb, pw_k, b_k, c_k, skip, layer)


def _layer_norm(x, g, b):
    mu = jnp.mean(x, axis=-1, keepdims=True)
    d = x - mu
    var = jnp.mean(d * d, axis=-1, keepdims=True)
    return d * lax.rsqrt(var + LN_EPS) * g + b


def _residual_rows(refs):
    if len(refs) == 1:
        return lambda rows: refs[0][rows, :]
    head_ref, body_ref = refs
    is_head = pl.program_id(0) == 0
    return lambda rows: jnp.where(is_head, head_ref[rows, :], body_ref[rows, :])


def _post_kernel(*refs, split):
    nres = 2 if split else 1
    load_x = _residual_rows(refs[:nres])
    (att_ref, ret_ref, s5_ref, mod_ref, permt_ref, wglu_ref, bglu_ref, wo_ref, g1_ref, b1_ref, w1_ref, w2_ref,
     g2_ref, b2_ref, o_ref) = refs[nres:]
    woa_ref = wo_ref.at[0:ATT_WIDTH]
    wor_ref = wo_ref.at[ATT_WIDTH:ATT_WIDTH + RET_WIDTH]
    wos_ref = wo_ref.at[ATT_WIDTH + RET_WIDTH:ATT_WIDTH + RET_WIDTH + S5_WIDTH]
    mdt = w1_ref.dtype
    sub = o_ref.shape[0] // POST_SPLIT
    csub = sub // S5_T
    nff = D_FF // FF_CHUNK

    def mix(part):
        rows = slice(sub * part, sub * (part + 1))
        zrows = []
        for i in range(S5_T):
            src_vreg, src_blk = divmod(i * S5_PAIR_W, LANES)
            src_blk //= S5_PAIR_W
            cols = [_lane_block_shuffle(
                lambda a: s5_ref[a, csub * part:csub * (part + 1), src_vreg * LANES:(src_vreg + 1) * LANES],
                src_blk, w) for w in range(S5_WIDTH // LANES)]
            zrows.append(jnp.concatenate(cols, axis=1))
        hs = jax.nn.gelu(jnp.concatenate(zrows, axis=0))
        gate = jax.nn.sigmoid(_mm(hs.astype(mdt), wglu_ref[...]) + bglu_ref[...])
        s5 = _mm(permt_ref[...], (hs * gate).astype(mdt)).astype(mdt)
        return _mm(att_ref[rows, :], woa_ref[...]) + _mm(ret_ref[rows, :], wor_ref[...]) + _mm(s5, wos_ref[...])

    def norm1(part, ox):
        rows = slice(sub * part, sub * (part + 1))
        x1 = _layer_norm(DEEPNORM_ALPHA * load_x(rows) + mod_ref[2:3, :] * ox, g1_ref[...], b1_ref[...])
        return x1, (x1 * (1.0 + mod_ref[4:5, :]) + mod_ref[3:4, :]).astype(mdt)

    def ff(h, c):
        a = _mm(h, w1_ref[:, FF_CHUNK * c:FF_CHUNK * (c + 1)])
        a = jnp.square(jnp.maximum(a, 0.0)).astype(mdt)
        return _mm(a, w2_ref[FF_CHUNK * c:FF_CHUNK * (c + 1), :])

    def norm2(part, x1, acc):
        rows = slice(sub * part, sub * (part + 1))
        o_ref[rows, :] = _layer_norm(DEEPNORM_ALPHA * x1 + mod_ref[5:6, :] * acc, g2_ref[...], b2_ref[...])

    nstage = nff + 3
    state = [dict() for _ in range(POST_SPLIT)]
    for part, stage in POST_PROGRAM:
        st = state[part]
        if stage == 0:
            st["ox"] = mix(part)
        elif stage == 1:
            st["x1"], st["h"] = norm1(part, st.pop("ox"))
        elif stage < nstage - 1:
            term = ff(st["h"], stage - 2)
            st["acc"] = term if stage == 2 else st["acc"] + term
        else:
            norm2(part, st["x1"], st["acc"])


def _residual_specs(residual):
    tm = ROW_TILE
    if len(residual) == 1:
        return [pl.BlockSpec((tm, D_MODEL), lambda i: (i, 0))]
    head_tiles = CTX_PAD // tm
    return [pl.BlockSpec((tm, D_MODEL), lambda i: (jnp.minimum(i, head_tiles - 1), 0)),
            pl.BlockSpec((tm, D_MODEL), lambda i: (jnp.maximum(i - head_tiles, 0), 0))]


def _post(residual, att, ret, s5_pairs, mods, layer, permt, wglu, bglu, wo, g1, b1, w1, w2, g2, b2, skip_context):
    rows = att.shape[0]
    tm = ROW_TILE
    off = CTX_PAD // tm if skip_context else 0
    assert not (skip_context and len(residual) > 1)
    row_blk = lambda width: pl.BlockSpec((tm, width), lambda i: (i + off, 0))
    full = lambda arr: pl.BlockSpec(arr.shape, lambda i: (0,) * arr.ndim)
    vec = lambda v: v.reshape(1, -1).astype(F32)
    small = [permt, wglu, vec(bglu), wo, vec(g1), vec(b1), w1, w2, vec(g2), vec(b2)]
    res_specs = [row_blk(D_MODEL)] if skip_context else _residual_specs(residual)
    return pl.pallas_call(
        functools.partial(_post_kernel, split=len(residual) > 1),
        grid=(rows // tm - off,),
        in_specs=res_specs + [row_blk(ATT_WIDTH), row_blk(RET_WIDTH),
                              pl.BlockSpec((S5_PAIRS, tm // S5_T, S5_T * S5_PAIR_W), lambda i: (0, i + off, 0)),
                              pl.BlockSpec((None, None, N_ADA, D_MODEL),
                                           lambda i: (layer, jnp.where(i + off == 0, 1, 0), 0, 0))]
                 + [full(arr) for arr in small],
        out_specs=pl.BlockSpec((tm, D_MODEL), lambda i: (i, 0)),
        out_shape=jax.ShapeDtypeStruct((rows - off * tm, D_MODEL), F32),
        compiler_params=_params(("arbitrary",)),
        name="post",
    )(*residual, att, ret, s5_pairs, mods, *small)


def kernel(x, c, ctx, c_ctx, w_ada, b_ada, w_in, att_sink, ret_decay_logit, s5_lambda_re, s5_lambda_im, s5_b_re,
           s5_b_im, s5_c_re, s5_c_im, s5_log_dt, s5_d, w_glu, b_glu, w_out, ln1_g, ln1_b, w_ff1, w_ff2, ln2_g,
           ln2_b):
    assert x.shape[0] == 1 and x.shape[2] == D_MODEL and ctx.shape[1] == CTX_LEN
    seq = x.shape[1]
    assert seq % ROW_TILE == 0
    residual = (jnp.pad(ctx[0], ((0, CTX_PAD - CTX_LEN), (0, 0))), x[0])
    cond =jnp.zeros((8, D_MODEL), F32).at[0].set(c[0]).at[1].set(c_ctx)
    mods = _modulation(cond, w_ada, b_ada).reshape(DEPTH, 8, N_ADA, D_MODEL)
    tabs = _rope_tables(seq)
    masks = _attention_masks()
    perm = _chunk_perm(ROW_TILE // POST_SPLIT, MXU_DTYPE)
    permt = perm.T
    col_scale = jnp.ones((IN_WIDTH,), F32).at[COL_AQ:COL_AK].set(HEAD_DIM ** -0.5 * LOG2E)
    col_scale = col_scale.at[COL_RQ:COL_RK].set(HEAD_DIM ** -0.5)
    s5w = jax.vmap(_s5_weights)(s5_lambda_re, s5_lambda_im, s5_b_re, s5_b_im, s5_c_re, s5_c_im, s5_log_dt, s5_d)
    log_gamma = jax.nn.log_sigmoid(ret_decay_logit.astype(F32))
    for l in range(DEPTH):
        proj, u_pairs = _in_proj(residual, mods, l, (w_in[l] * col_scale).astype(MXU_DTYPE), tabs, perm)
        att = _attention(proj, att_sink[l].astype(F32) * LOG2E, masks)
        ret = _retention(proj, log_gamma[l])
        s5 = _s5_mixer(u_pairs, s5w, l)
        stream = _post(residual, att, ret, s5, mods, l, permt, w_glu[l].astype(MXU_DTYPE), b_glu[l],
                       w_out[l].astype(MXU_DTYPE), ln1_g[l], ln1_b[l], w_ff1[l].astype(MXU_DTYPE),
                       w_ff2[l].astype(MXU_DTYPE), ln2_g[l], ln2_b[l], skip_context=(l == DEPTH - 1))
        residual = (stream,)
    return stream[None]
```

```python
import functools
import math

import jax
import jax.numpy as jnp
from jax import lax
from jax.experimental import pallas as pl
from jax.experimental.pallas import tpu as pltpu

F32 = jnp.float32
MXU_DTYPE = jnp.bfloat16

D_MODEL = 1024
DEPTH = 4
GRID_W = 64
CTX_LEN = 256
CTX_PAD = 512
HEAD_DIM = 64
ATT_HEADS = 8
ATT_KV_HEADS = 2
ATT_BLOCK = 128
ATT_LOOKAHEAD = 2
ROPE_BASE = 10000.0
RET_HEADS = 4
RET_CHUNK = 256
S5_CH = 16
S5_GROUPS = 16
S5_STATE = 64
S5_T = 16
S5_PAIRS = S5_GROUPS // 2
S5_PAIR_W = 2 * S5_CH
S5_TILE = CTX_PAD // S5_T
SCAN_ROWS = 8
S5_STATE_W = 2 * 2 * S5_STATE
ATT_WIDTH = ATT_HEADS * HEAD_DIM
KV_WIDTH = ATT_KV_HEADS * HEAD_DIM
RET_WIDTH = RET_HEADS * HEAD_DIM
S5_WIDTH = S5_GROUPS * S5_CH
IN_WIDTH = ATT_WIDTH + 2 * KV_WIDTH + 4 * RET_WIDTH + S5_WIDTH
D_FF = 4 * D_MODEL
FF_CHUNK = 1024
N_ADA = 6
LN_EPS = 1e-5
GN_EPS = 1e-5
DEEPNORM_ALPHA = (2 * DEPTH) ** 0.25
ROW_TILE = 512
POST_SPLIT = 2
_NFF = D_FF // FF_CHUNK
POST_PROGRAM = (((0, 0), (0, 1), (1, 0), (0, 2), (1, 1)) + tuple((0, 2 + c) for c in range(1, _NFF))
                + ((1, 2), (0, 2 + _NFF)) + tuple((1, 2 + c) for c in range(1, _NFF)) + ((1, 2 + _NFF),))
NEG_BIG = -1e30
LOG2E = math.log2(math.e)
LANES = 128
VMEM_LIMIT = 56 * 1024 * 1024
STAGE_BYTES = 2 * 1024 * 1024

COL_AQ, COL_AK, COL_AV = 0, ATT_WIDTH, ATT_WIDTH + KV_WIDTH
COL_RQ = ATT_WIDTH + 2 * KV_WIDTH
COL_RK, COL_RV, COL_RG = COL_RQ + RET_WIDTH, COL_RQ + 2 * RET_WIDTH, COL_RQ + 3 * RET_WIDTH
COL_S5 = COL_RQ + 4 * RET_WIDTH


def _mm(a, b):
    return jnp.dot(a, b, preferred_element_type=F32)


def _mm_nt(a, b):
    return lax.dot_general(a, b, (((1,), (1,)), ((), ())), preferred_element_type=F32)


def _mm_tn(a, b):
    return lax.dot_general(a, b, (((0,), (0,)), ((), ())), preferred_element_type=F32)


def _params(sem):
    return pltpu.CompilerParams(dimension_semantics=sem, vmem_limit_bytes=VMEM_LIMIT)


def _mod_kernel(cond_ref, w_ref, b_ref, o_ref):
    c = cond_ref[...]
    s = c * jax.nn.sigmoid(c)
    o_ref[0] = jnp.dot(s, w_ref[0], preferred_element_type=F32, precision=lax.Precision.HIGHEST) + b_ref[0]


def _modulation(cond, w_ada, b_ada):
    tn = 1536
    n = N_ADA * D_MODEL
    return pl.pallas_call(
        _mod_kernel,
        grid=(DEPTH, n // tn),
        in_specs=[
            pl.BlockSpec((8, D_MODEL), lambda l, j: (0, 0)),
            pl.BlockSpec((1, D_MODEL, tn), lambda l, j: (l, 0, j)),
            pl.BlockSpec((1, 1, tn), lambda l, j: (l, 0, j)),
        ],
        out_specs=pl.BlockSpec((1, 8, tn), lambda l, j: (l, 0, j)),
        out_shape=jax.ShapeDtypeStruct((DEPTH, 8, n), F32),
        compiler_params=_params(("arbitrary", "arbitrary")),
        name="modulation",
    )(cond, w_ada, b_ada.reshape(DEPTH, 1, n))


def _lane_block_shuffle(src_rows, src_lane_blk, out_vreg):
    acc = None
    for q in range(LANES // S5_PAIR_W):
        piece = src_rows(out_vreg * (LANES // S5_PAIR_W) + q)
        shift = (S5_PAIR_W * (q - src_lane_blk)) % LANES
        if shift:
            piece = pltpu.roll(piece, shift, 1)
        if acc is None:
            acc = piece
        else:
            lane_blk = lax.broadcasted_iota(jnp.int32, piece.shape, 1) // S5_PAIR_W
            acc = jnp.where(lane_blk == q, piece, acc)
    return acc


def _in_proj_kernel(*refs, split, layer):
    nres = 2 if split else 1
    (mod_ref, w_hbm, scale_ref, ca_ref, sa_ref, cr_ref, sr_ref, perm_ref, o_ref, u_ref, w_ref, stage_ref,
     sem_ref) = refs[nres:]

    @pl.when(pl.program_id(0) == 0)
    def _stage_weights():
        _load_cast(w_hbm.at[layer], w_ref, stage_ref, sem_ref, col_scale=scale_ref[...])

    x = _residual_rows(refs[:nres])(slice(None))
    h = (x * (1.0 + mod_ref[1:2, :]) + mod_ref[0:1, :]).astype(w_ref.dtype)
    lane = lax.broadcasted_iota(jnp.int32, (x.shape[0], LANES), 1)
    first_att = (lane & 31) < 16
    first_ret = (lane & 63) < 32

    def proj(c0, c1):
        return _mm(h, w_ref[:, c0:c1])

    def rope_store(p, c0, width, cos, sin, first, half):
        for b in range(width // LANES):
            blk = p[:, LANES * b:LANES * (b + 1)]
            rot = jnp.where(first, pltpu.roll(blk, LANES - half, 1), pltpu.roll(blk, half, 1))
            o_ref[:, c0 + LANES * b:c0 + LANES * (b + 1)] = (blk * cos + rot * sin).astype(o_ref.dtype)

    def plain_store(p, c0, width):
        o_ref[:, c0:c0 + width] = p.astype(o_ref.dtype)

    u = proj(COL_S5, IN_WIDTH).astype(w_ref.dtype)
    sub = perm_ref.shape[0]
    nchunk = sub // S5_T
    for part in range(x.shape[0] // sub):
        g = _mm(perm_ref[...], u[sub * part:sub * (part + 1)])
        for a in range(S5_PAIRS):
            vreg_col, lane_blk = divmod(a * S5_PAIR_W, LANES)
            lane_blk //= S5_PAIR_W
            for v in range(S5_T * S5_PAIR_W // LANES):
                slab = _lane_block_shuffle(
                    lambda j: g[nchunk * j:nchunk * (j + 1), vreg_col * LANES:(vreg_col + 1) * LANES], lane_blk, v)
                u_ref[a, nchunk * part:nchunk * (part + 1), LANES * v:LANES * (v + 1)] = slab.astype(u_ref.dtype)

    ca, sa, cr, sr = ca_ref[...], sa_ref[...], cr_ref[...], sr_ref[...]
    att_rope = (ca, sa, first_att, HEAD_DIM // 4)
    ret_rope = (cr, sr, first_ret, HEAD_DIM // 2)
    groups = [(COL_AQ, ATT_WIDTH, att_rope), (COL_AK, KV_WIDTH, att_rope), (COL_RQ, RET_WIDTH, ret_rope),
              (COL_RK, RET_WIDTH, ret_rope), (COL_AV, KV_WIDTH, None), (COL_RV, COL_S5 - COL_RV, None)]
    pending = None
    for group in groups + [None]:
        nxt = None if group is None else (proj(group[0], group[0] + group[1]),) + group
        if pending is not None:
            p, pc0, pwidth, prope = pending
            if prope is None:
                plain_store(p, pc0, pwidth)
            else:
                rope_store(p, pc0, pwidth, *prope)
        pending = nxt


def _chunk_perm(tile_rows, dtype):
    nchunk = tile_rows // S5_T
    r = jnp.arange(tile_rows)
    src = S5_T * (r % nchunk) + r // nchunk
    return (src[:, None] == jnp.arange(tile_rows)[None, :]).astype(dtype)


def _in_proj(residual, mods, layer, w_in, col_scale, tabs, perm):
    rows = tabs[0].shape[0]
    tm = ROW_TILE
    tab_spec = pl.BlockSpec((tm, LANES), lambda i: (i, 0))
    nch = rows // S5_T
    return pl.pallas_call(
        functools.partial(_in_proj_kernel, split=len(residual) > 1, layer=layer),
        grid=(rows // tm,),
        in_specs=_residual_specs(residual) + [
            pl.BlockSpec((None, None, N_ADA, D_MODEL), lambda i: (layer, jnp.where(i == 0, 1, 0), 0, 0)),
            pl.BlockSpec(memory_space=pl.ANY),
            pl.BlockSpec((1, IN_WIDTH), lambda i: (0, 0)),
            tab_spec, tab_spec, tab_spec, tab_spec,
            pl.BlockSpec(perm.shape, lambda i: (0, 0)),
        ],
        out_specs=[pl.BlockSpec((tm, COL_S5), lambda i: (i, 0)),
                   pl.BlockSpec((S5_PAIRS, tm // S5_T, S5_T * S5_PAIR_W), lambda i: (0, i, 0))],
        out_shape=[jax.ShapeDtypeStruct((rows, COL_S5), MXU_DTYPE),
                   jax.ShapeDtypeStruct((S5_PAIRS, nch, S5_T * S5_PAIR_W), MXU_DTYPE)],
        scratch_shapes=[pltpu.VMEM((D_MODEL, IN_WIDTH), MXU_DTYPE),
                        pltpu.VMEM((2, STAGE_BYTES // (4 * IN_WIDTH), IN_WIDTH), F32),
                        pltpu.SemaphoreType.DMA((2,))],
        compiler_params=_params(("arbitrary",)),
        name="in_proj",
    )(*residual, mods, w_in, col_scale, *tabs, perm)


def _rope_tables(seq):
    half_a = HEAD_DIM // 4
    half_r = HEAD_DIM // 2
    nrow = seq // GRID_W
    inv_a = ROPE_BASE ** (-jnp.arange(half_a, dtype=F32) / half_a)
    inv_r = ROPE_BASE ** (-jnp.arange(half_r, dtype=F32) / half_r)
    ang_r = jnp.arange(nrow, dtype=F32)[:, None] * inv_a[None, :]
    ang_c = jnp.arange(GRID_W, dtype=F32)[:, None] * inv_a[None, :]
    ang_t = jnp.arange(seq, dtype=F32)[:, None] * inv_r[None, :]
    hp = lax.Precision.HIGHEST
    lane = jnp.arange(LANES)
    within = lane % HEAD_DIM
    pick_a = (within % half_a)[None, :] == jnp.arange(half_a)[:, None]
    exp_row = (pick_a & (within < 2 * half_a)[None, :]).astype(F32)
    exp_col = (pick_a & (within >= 2 * half_a)[None, :]).astype(F32)
    exp_t = ((lane % half_r)[None, :] == jnp.arange(half_r)[:, None]).astype(F32)
    sign_a = jnp.where(within % (2 * half_a) < half_a, -1.0, 1.0).astype(F32)
    sign_r = jnp.where(within < half_r, -1.0, 1.0).astype(F32)

    def att_table(fn):
        by_row = jnp.dot(fn(ang_r), exp_row, precision=hp)
        by_col = jnp.dot(fn(ang_c), exp_col, precision=hp)
        return (by_row[:, None, :] + by_col[None, :, :]).reshape(seq, LANES)

    cos_a = att_table(jnp.cos)
    sin_a = att_table(jnp.sin) * sign_a
    cos_r = jnp.dot(jnp.cos(ang_t), exp_t, precision=hp)
    sin_r = jnp.dot(jnp.sin(ang_t), exp_t, precision=hp) * sign_r
    pad = lambda tab, ident: jnp.pad(tab, ((CTX_PAD, 0), (0, 0)), constant_values=ident)
    return pad(cos_a, 1.0), pad(sin_a, 0.0), pad(cos_r, 1.0), pad(sin_r, 0.0)


def _swap_halves(x):
    if x.dtype.itemsize == 4:
        return pltpu.roll(x, 64, 1)
    packed = pltpu.bitcast(x, jnp.uint32)
    return pltpu.bitcast(pltpu.roll(packed, 64, 1), x.dtype)


def _dup_heads(x):
    sw = _swap_halves(x)
    lo = lax.broadcasted_iota(jnp.int32, x.shape, 1) < HEAD_DIM
    return jnp.where(lo, x, sw), jnp.where(lo, sw, x)


def _attn_kernel(sink_ref, q_ref, km_ref, kp_ref, kn_ref, vm_ref, vp_ref, vn_ref, kc_ref, vc_ref, mask_ref,
                 o_ref, k2_ref, v2_ref, kc2_ref, vc2_ref):
    i = pl.program_id(0)
    last_blk = pl.num_programs(0) * (ROW_TILE // ATT_BLOCK) - 1
    blk = ATT_BLOCK
    def spread(src, ones_upper):
        x = src[...]
        a, b = _dup_heads(x)
        if ones_upper:
            upper = lax.broadcasted_iota(jnp.int32, x.shape, 1) >= HEAD_DIM
            a = jnp.where(upper, jnp.ones_like(a), a)
            b = jnp.where(upper, jnp.ones_like(b), b)
        return a, b

    for dst, parts, is_v in ((k2_ref, (kp_ref, km_ref, kn_ref), False), (v2_ref, (vp_ref, vm_ref, vn_ref), True)):
        row = 0
        for part in parts:
            a, b = spread(part, is_v)
            n = part.shape[0]
            dst[0, row:row + n, :] = a
            dst[1, row:row + n, :] = b
            row += n
    for dst, src, is_v in ((kc2_ref, kc_ref, False), (vc2_ref, vc_ref, True)):
        a, b = spread(src, is_v)
        dst[0] = a
        dst[1] = b

    lo = lax.broadcasted_iota(jnp.int32, (blk, LANES), 1) < HEAD_DIM
    group = ATT_HEADS // ATT_KV_HEADS

    nloc = 3 * blk

    def scores(j, kv):
        r0 = j * blk
        qt = q_ref[r0:r0 + blk, group * HEAD_DIM * kv:group * HEAD_DIM * (kv + 1)]
        parts = []
        for g in range(group):
            qc = qt[:, LANES * (g // 2):LANES * (g // 2 + 1)]
            keep = lo if g % 2 == 0 else jnp.logical_not(lo)
            parts.append(jnp.where(keep, qc, jnp.zeros_like(qc)))
        qs = jnp.concatenate(parts, axis=0)
        return _mm_nt(qs, k2_ref[kv, r0:r0 + nloc, :]), _mm_nt(qs, kc2_ref[kv])

    def finish(j, kv, s_loc, s_ctx):
        r0 = j * blk
        gblk = i * (ROW_TILE // blk) + j
        sel = jnp.where(i == 0, 3, jnp.where(gblk == CTX_PAD // blk, 0, jnp.where(gblk == last_blk, 2, 1)))
        bias = mask_ref[sel]
        probs, sink_w = [], []
        for g in range(group):
            s = jnp.concatenate([s_loc[blk * g:blk * (g + 1)] + bias, s_ctx[blk * g:blk * (g + 1)]], axis=1)
            sk = sink_ref[group * kv + g]
            m = jnp.maximum(jnp.max(s, axis=-1, keepdims=True), sk)
            probs.append(jnp.exp2(s - m).astype(o_ref.dtype))
            sink_w.append(jnp.exp2(sk - m))
        p = jnp.concatenate(probs, axis=0)
        o = _mm(p[:, :nloc], v2_ref[kv, r0:r0 + nloc, :]) + _mm(p[:, nloc:], vc2_ref[kv])
        for half in range(group // 2):
            even, odd = 2 * half, 2 * half + 1
            oe = o[blk * even:blk * (even + 1)]
            oo = o[blk * odd:blk * (odd + 1)]
            y_even = oe * (1.0 / (pltpu.roll(oe, HEAD_DIM, 1) + sink_w[even]))
            y_odd = pltpu.roll(oo, HEAD_DIM, 1) * (1.0 / (oo + sink_w[odd]))
            c0 = group * HEAD_DIM * kv + LANES * half
            o_ref[r0:r0 + blk, c0:c0 + LANES] = jnp.where(lo, y_even, y_odd).astype(o_ref.dtype)

    items = [(j, kv) for j in range(ROW_TILE // blk) for kv in range(ATT_KV_HEADS)]
    pending = {}
    for t in range(len(items) + ATT_LOOKAHEAD):
        if t < len(items):
            pending[t] = scores(*items[t])
        if t >= ATT_LOOKAHEAD:
            finish(*items[t - ATT_LOOKAHEAD], *pending.pop(t - ATT_LOOKAHEAD))


def _attention_masks():
    qi = jnp.arange(ATT_BLOCK)[:, None]
    kj = jnp.arange(3 * ATT_BLOCK)[None, :]
    band = jnp.abs(kj - ATT_BLOCK - qi) <= ATT_BLOCK
    first = band & (kj >= ATT_BLOCK)
    last = band & (kj < 2 * ATT_BLOCK)
    none = jnp.zeros_like(band)
    masks = jnp.stack([first, band, last, none])
    return jnp.where(masks, 0.0, NEG_BIG).astype(F32)


def _attention(proj, sink, masks):
    rows = proj.shape[0]
    tm, blk = ROW_TILE, ATT_BLOCK
    per = tm // blk
    nblk = rows // blk
    ck, cv = COL_AK // KV_WIDTH, COL_AV // KV_WIDTH
    dt = proj.dtype
    return pl.pallas_call(
        _attn_kernel,
        grid=(rows // tm,),
        in_specs=[
            pl.BlockSpec(memory_space=pltpu.SMEM),
            pl.BlockSpec((tm, ATT_WIDTH), lambda i: (i, 0)),
            pl.BlockSpec((tm, KV_WIDTH), lambda i: (i, ck)),
            pl.BlockSpec((blk, KV_WIDTH), lambda i: (jnp.maximum(i * per - 1, 0), ck)),
            pl.BlockSpec((blk, KV_WIDTH), lambda i: (jnp.minimum((i + 1) * per, nblk - 1), ck)),
            pl.BlockSpec((tm, KV_WIDTH), lambda i: (i, cv)),
            pl.BlockSpec((blk, KV_WIDTH), lambda i: (jnp.maximum(i * per - 1, 0), cv)),
            pl.BlockSpec((blk, KV_WIDTH), lambda i: (jnp.minimum((i + 1) * per, nblk - 1), cv)),
            pl.BlockSpec((CTX_LEN, KV_WIDTH), lambda i: (0, ck)),
            pl.BlockSpec((CTX_LEN, KV_WIDTH), lambda i: (0, cv)),
            pl.BlockSpec((4, blk, 3 * blk), lambda i: (0, 0, 0)),
        ],
        out_specs=pl.BlockSpec((tm, ATT_WIDTH), lambda i: (i, 0)),
        out_shape=jax.ShapeDtypeStruct((rows, ATT_WIDTH), dt),
        scratch_shapes=[
            pltpu.VMEM((2, tm + 2 * blk, KV_WIDTH), dt),
            pltpu.VMEM((2, tm + 2 * blk, KV_WIDTH), dt),
            pltpu.VMEM((2, CTX_LEN, KV_WIDTH), dt),
            pltpu.VMEM((2, CTX_LEN, KV_WIDTH), dt),
        ],
        compiler_params=_params(("arbitrary",)),
        name="attention",
    )(sink, proj, proj, proj, proj, proj, proj, proj, proj, proj, masks)


def _ret_kernel(lg_ref, q_ref, k_ref, v_ref, g_ref, o_ref, sb_ref, s_ref, dm_ref, tab_ref, gbd_ref):
    ph = pl.program_id(0)
    t = pl.program_id(1)
    ntile = pl.num_programs(1)
    c = RET_CHUNK
    w = RET_WIDTH
    per = q_ref.shape[0] // c
    mdt = sb_ref.dtype
    rows = lambda ci: slice(c * ci, c * (ci + 1))

    def lane_vec(direction, shape, axis):
        head = lax.broadcasted_iota(jnp.int32, shape, axis) // HEAD_DIM
        out = jnp.full(shape, lg_ref[direction, RET_HEADS - 1], F32)
        for h in range(RET_HEADS - 2, -1, -1):
            out = jnp.where(head == h, lg_ref[direction, h], out)
        return out

    @pl.when(jnp.logical_and(ph == 0, t == 0))
    def _init_tables():
        diff = (lax.broadcasted_iota(jnp.int32, (c, c), 0) - lax.broadcasted_iota(jnp.int32, (c, c), 1)).astype(F32)
        for h in range(RET_HEADS):
            dm_ref[h] = jnp.exp(jnp.where(diff >= 0, diff * lg_ref[0, h], -diff * lg_ref[1, h]))
        pos = lax.broadcasted_iota(jnp.int32, (c, w), 0).astype(F32)
        lgf = lane_vec(0, (c, w), 1)
        lgb = lane_vec(1, (c, w), 1)
        tab_ref[0] = jnp.exp((c - 1.0 - pos) * lgf)
        tab_ref[1] = jnp.exp((pos + 1.0) * lgf)
        tab_ref[2] = jnp.exp(pos * lgb)
        tab_ref[3] = jnp.exp((c - pos) * lgb)
        same = (lax.broadcasted_iota(jnp.int32, (w, w), 0) // HEAD_DIM
                == lax.broadcasted_iota(jnp.int32, (w, w), 1) // HEAD_DIM)
        bd = jnp.where(same, 1.0, 0.0)
        gbd_ref[0] = bd * jnp.exp(c * lane_vec(0, (w, w), 0))
        gbd_ref[1] = bd * jnp.exp(c * lane_vec(1, (w, w), 0))
        gbd_ref[2] = bd

    @pl.when(t == 0)
    def _reset_state():
        s_ref[...] = jnp.zeros_like(s_ref)

    def state_update(direction, key_tab, ci):
        kw = (k_ref[rows(ci), :].astype(F32) * tab_ref[key_tab]).astype(mdt)
        u = _mm_tn(kw, v_ref[rows(ci), :])
        s_ref[...] = gbd_ref[direction] * s_ref[...] + gbd_ref[2] * u

    @pl.when(jnp.logical_and(ph == 0, t == 0))
    def _backward_context():
        sb_ref[0] = s_ref[...].astype(mdt)
        state_update(1, 2, 0)

    @pl.when(jnp.logical_and(ph == 0, t > 0))
    def _backward_latent():
        base = 1 + per * (ntile - t - 1)
        for ci in range(per - 1, -1, -1):
            sb_ref[base + ci] = s_ref[...].astype(mdt)
            state_update(1, 2, ci)

    head = lax.broadcasted_iota(jnp.int32, (c, w), 1) // HEAD_DIM

    def scores(ci):
        q = q_ref[rows(ci), :]
        qs = jnp.concatenate([jnp.where(head == h, q, jnp.zeros_like(q)) for h in range(RET_HEADS)], axis=0)
        return _mm_nt(qs, k_ref[rows(ci), :])

    def intra(ci, sc):
        scd = jnp.concatenate([sc[c * h:c * (h + 1)] * dm_ref[h] for h in range(RET_HEADS)], axis=0).astype(mdt)
        oi = _mm(scd, v_ref[rows(ci), :])
        o = jnp.where(head == 0, oi[0:c], 0.0)
        for h in range(1, RET_HEADS):
            o = o + jnp.where(head == h, oi[c * h:c * (h + 1)], 0.0)
        return o

    def cross(ci, idx):
        qf = q_ref[rows(ci), :].astype(F32)
        return (_mm((qf * tab_ref[1]).astype(mdt), s_ref[...].astype(mdt))
                + _mm((qf * tab_ref[3]).astype(mdt), sb_ref[idx]))

    def finish(ci, o):
        avg = (gbd_ref[2] * (1.0 / HEAD_DIM)).astype(mdt)
        o_hi = o.astype(mdt)
        d = o - (_mm(o_hi, avg) + _mm((o - o_hi.astype(F32)).astype(mdt), avg))
        var = _mm((d * d).astype(mdt), avg)
        gate = g_ref[rows(ci), :].astype(F32)
        y = d * lax.rsqrt(var + GN_EPS) * (gate * jax.nn.sigmoid(gate))
        o_ref[rows(ci), :] = y.astype(o_ref.dtype)

    @pl.when(jnp.logical_and(ph == 1, t == 0))
    def _forward_context():
        o = intra(0, scores(0)) + cross(0, 0)
        state_update(0, 0, 0)
        finish(0, o)
        for ci in range(1, per):
            o_ref[rows(ci), :] = jnp.zeros((c, w), o_ref.dtype)

    @pl.when(jnp.logical_and(ph == 1, t > 0))
    def _forward_latent():
        base = 1 + per * (t - 1)
        sc = [scores(ci) for ci in range(per)]
        outs = [intra(ci, sc[ci]) for ci in range(per)]
        for ci in range(per):
            outs[ci] = outs[ci] + cross(ci, base + ci)
            state_update(0, 0, ci)
        for ci in range(per):
            finish(ci, outs[ci])


def _retention(proj, log_gamma):
    rows = proj.shape[0]
    c = RET_CHUNK
    tm = ROW_TILE
    ntile = rows // tm
    nchunk = 1 + (rows - CTX_PAD) // c
    dt = proj.dtype
    cq, ckk, cvv, cg = (COL_RQ // RET_WIDTH, COL_RK // RET_WIDTH, COL_RV // RET_WIDTH, COL_RG // RET_WIDTH)

    def kv_blk(ph, t):
        return jnp.where(ph == 0, jnp.where(t == 0, 0, ntile - t), t)

    def fw_blk(ph, t):
        return jnp.where(ph == 0, 0, t)

    return pl.pallas_call(
        _ret_kernel,
        grid=(2, ntile),
        in_specs=[
            pl.BlockSpec(memory_space=pltpu.SMEM),
            pl.BlockSpec((tm, RET_WIDTH), lambda ph, t: (fw_blk(ph, t), cq)),
            pl.BlockSpec((tm, RET_WIDTH), lambda ph, t: (kv_blk(ph, t), ckk)),
            pl.BlockSpec((tm, RET_WIDTH), lambda ph, t: (kv_blk(ph, t), cvv)),
            pl.BlockSpec((tm, RET_WIDTH), lambda ph, t: (fw_blk(ph, t), cg)),
        ],
        out_specs=pl.BlockSpec((tm, RET_WIDTH), lambda ph, t: (fw_blk(ph, t), 0)),
        out_shape=jax.ShapeDtypeStruct((rows, RET_WIDTH), dt),
        scratch_shapes=[
            pltpu.VMEM((nchunk, RET_WIDTH, RET_WIDTH), dt),
            pltpu.VMEM((RET_WIDTH, RET_WIDTH), F32),
            pltpu.VMEM((RET_HEADS, c, c), F32),
            pltpu.VMEM((4, c, RET_WIDTH), F32),
            pltpu.VMEM((3, RET_WIDTH, RET_WIDTH), F32),
        ],
        compiler_params=_params(("arbitrary", "arbitrary")),
        name="retention",
    )(log_gamma, proj, proj, proj, proj)


def _s5_weights(lam_re, lam_im, b_re, b_im, c_re, c_im, log_dt, d_skip):
    hp = lax.Precision.HIGHEST
    tt, g, n, p, a = S5_T, S5_GROUPS, S5_STATE, S5_CH, S5_PAIRS
    lam = lax.complex(lam_re.astype(F32), lam_im.astype(F32))
    dtv = jnp.exp(log_dt.astype(F32))[..., None]
    lam_bar = jnp.exp(lam * dtv)
    bbar = ((lam_bar - 1.0) / lam)[..., None] * lax.complex(b_re.astype(F32), b_im.astype(F32))
    cmat = lax.complex(c_re.astype(F32), c_im.astype(F32))
    pw = [jnp.ones_like(lam_bar)]
    for _ in range(tt):
        pw.append(pw[-1] * lam_bar)
    pw = jnp.stack(pw, axis=1)
    eye2 = jnp.eye(2, dtype=F32)
    ri = lambda z, axis: jnp.stack([jnp.real(z), jnp.imag(z)], axis=axis)

    pw_l = pw.reshape(2, tt + 1, a, 2 * n)
    bbt = jnp.einsum('dahpn,gh->dagphn', jnp.swapaxes(bbar, -1, -2).reshape(2, a, 2, p, n), eye2)
    bbt = bbt.reshape(2, a, 2 * p, 2 * n)
    cct = jnp.einsum('dahpn,gh->dagphn', cmat.reshape(2, a, 2, p, n), eye2).reshape(2, a, 2 * p, 2 * n)
    pw_k = ri(pw_l, 1).transpose(3, 0, 1, 2, 4)
    b_k = ri(bbt, 2).transpose(1, 0, 2, 3, 4)
    c_k = ri(cct, 2).transpose(1, 0, 2, 3, 4)
    decay = [pw_l[:, tt]]
    for _ in range(SCAN_ROWS - 1):
        decay.append(decay[-1] * decay[0])
    decay = jnp.stack(decay, axis=0)
    rows8 = lambda z: jnp.broadcast_to(z[None], (SCAN_ROWS,) + z.shape)
    carry_w = jnp.stack([decay[:, 0], decay[::-1, 1]], axis=1)
    scan_tab = jnp.stack([rows8(decay[0]), rows8(decay[1]), rows8(decay[3]), carry_w], axis=0)
    scan_tab = ri(scan_tab, 0).transpose(4, 3, 1, 0, 2, 5)
    skip = jnp.tile(d_skip.astype(F32).reshape(a, 1, 2 * p), (1, tt, 1)).reshape(a, 1, tt * 2 * p)
    return pw_k, b_k, c_k, scan_tab, skip


def _pair_spec(layer, *shape):
    return pl.BlockSpec((None, None) + shape, lambda i: (layer, i) + (0,) * len(shape))


def _s5_drive_kernel(u_ref, pw_ref, b_ref, o_ref, w_ref):
    rows, half = S5_PAIR_W, LANES
    for d in range(2):
        br, bi = b_ref[d, 0], b_ref[d, 1]
        for j in range(S5_T):
            e = S5_T - 1 - j if d == 0 else j
            pr, pi = pw_ref[d, 0, e:e + 1, :], pw_ref[d, 1, e:e + 1, :]
            w_ref[rows * j:rows * (j + 1), 2 * half * d:2 * half * d + half] = (pr * br - pi * bi).astype(w_ref.dtype)
            w_ref[rows * j:rows * (j + 1), 2 * half * d + half:2 * half * (d + 1)] = (
                pr * bi + pi * br).astype(w_ref.dtype)
    o_ref[0] = _mm(u_ref[0], w_ref[...])


def _s5_drive(u_pairs, pw_k, b_k, layer):
    a, nch, wd = u_pairs.shape
    return pl.pallas_call(
        _s5_drive_kernel,
        grid=(a,),
        in_specs=[pl.BlockSpec((1, nch, wd), lambda i: (i, 0, 0)),
                  _pair_spec(layer, *pw_k.shape[2:]), _pair_spec(layer, *b_k.shape[2:])],
        out_specs=pl.BlockSpec((1, nch, 2 * S5_STATE_W), lambda i: (i, 0, 0)),
        out_shape=jax.ShapeDtypeStruct((a, nch, 2 * S5_STATE_W), F32),
        scratch_shapes=[pltpu.VMEM((wd, 2 * S5_STATE_W), u_pairs.dtype)],
        compiler_params=_params(("arbitrary",)),
        name="s5_drive",
    )(u_pairs, pw_k, b_k)


def _s5_scan_kernel(ef_ref, eb_ref, tab_ref, sf_ref, sb_ref, st_ref):
    t = pl.program_id(0)
    hw = LANES
    sub = SCAN_ROWS
    npair = tab_ref.shape[0]
    row = lax.broadcasted_iota(jnp.int32, (sub, hw), 0)

    @pl.when(t == 0)
    def _reset():
        st_ref[...] = jnp.zeros_like(st_ref)

    def shift(x, k, reverse):
        if reverse:
            return jnp.where(row < sub - k, pltpu.roll(x, sub - k, 0), 0.0)
        return jnp.where(row >= k, pltpu.roll(x, k, 0), 0.0)

    def scan_group(e_ref, o_ref, a, d, r0, cr, ci):
        reverse = d == 1
        xr = e_ref[a, pl.ds(r0, sub), 0:hw]
        xi = e_ref[a, pl.ds(r0, sub), hw:2 * hw]
        for step, k in enumerate((1, 2, 4)):
            ar, ai = tab_ref[a, d, step, 0], tab_ref[a, d, step, 1]
            sr, si = shift(xr, k, reverse), shift(xi, k, reverse)
            xr, xi = xr + ar * sr - ai * si, xi + ar * si + ai * sr
        wr, wi = tab_ref[a, d, 3, 0], tab_ref[a, d, 3, 1]
        fr = xr + wr * cr - wi * ci
        fi = xi + wr * ci + wi * cr
        edge = sub - 1 if reverse else 0
        o_ref[a, pl.ds(r0, sub), 0:hw] = jnp.where(row == edge, cr, shift(fr, 1, reverse))
        o_ref[a, pl.ds(r0, sub), hw:2 * hw] = jnp.where(row == edge, ci, shift(fi, 1, reverse))
        last = 0 if reverse else sub - 1
        return (jnp.broadcast_to(fr[last:last + 1], (sub, hw)), jnp.broadcast_to(fi[last:last + 1], (sub, hw)))

    def run(nrows):
        ngroups = nrows // sub

        def body(gi, carry):
            r0 = pl.multiple_of(gi * sub, sub)
            rb0 = pl.multiple_of((ngroups - 1 - gi) * sub, sub)
            new = []
            for a in range(npair):
                fr, fi, br, bi = carry[a]
                fr, fi = scan_group(ef_ref, sf_ref, a, 0, r0, fr, fi)
                br, bi = scan_group(eb_ref, sb_ref, a, 1, rb0, br, bi)
                new.append((fr, fi, br, bi))
            return tuple(new)

        init = tuple(tuple(st_ref[a, k] for k in range(4)) for a in range(npair))
        final = lax.fori_loop(0, ngroups, body, init)
        for a in range(npair):
            for k in range(4):
                st_ref[a, k] = final[a][k]

    @pl.when(t == 0)
    def _context():
        sf_ref[...] = jnp.zeros_like(sf_ref)
        sb_ref[...] = jnp.zeros_like(sb_ref)
        run(CTX_LEN // S5_T)

    @pl.when(t > 0)
    def _latent():
        run(S5_TILE)


def _s5_scan(drive, scan_tab, layer):
    a, nch, wd2 = drive.shape
    wd = wd2 // 2
    nt = nch // S5_TILE

    def bwd(t):
        return jnp.where(t == 0, 0, nt - t)

    return pl.pallas_call(
        _s5_scan_kernel,
        grid=(nt,),
        in_specs=[pl.BlockSpec((a, S5_TILE, wd), lambda t: (0, t, 0)),
                  pl.BlockSpec((a, S5_TILE, wd), lambda t: (0, bwd(t), 1)),
                  pl.BlockSpec((None,) + scan_tab.shape[1:], lambda t: (layer,) + (0,) * (scan_tab.ndim - 1))],
        out_specs=[pl.BlockSpec((a, S5_TILE, wd), lambda t: (0, t, 0)),
                   pl.BlockSpec((a, S5_TILE, wd), lambda t: (0, bwd(t), 0))],
        out_shape=[jax.ShapeDtypeStruct((a, nch, wd), F32), jax.ShapeDtypeStruct((a, nch, wd), F32)],
        scratch_shapes=[pltpu.VMEM((a, 4, SCAN_ROWS, LANES), F32)],
        compiler_params=_params(("arbitrary",)),
        name="s5_scan",
    )(drive, drive, scan_tab)


def _lane_window(x, start, width):
    cols = []
    for v in range(width // LANES):
        k0, off = divmod(start + LANES * v, LANES)
        lo = x[:, LANES * k0:LANES * (k0 + 1)]
        if off:
            hi = x[:, LANES * (k0 + 1):LANES * (k0 + 2)]
            lane = lax.broadcasted_iota(jnp.int32, lo.shape, 1)
            lo = jnp.where(lane < LANES - off, pltpu.roll(lo, LANES - off, 1), pltpu.roll(hi, LANES - off, 1))
        cols.append(lo)
    return jnp.concatenate(cols, axis=1)


def _s5_read_kernel(u_ref, sf_ref, sb_ref, pw_ref, b_ref, c_ref, skip_ref, o_ref, wt_ref, wi_ref, lag_ref):
    u = u_ref[0]
    mdt = u.dtype
    rows, half = S5_PAIR_W, LANES
    for d in range(2):
        cr, ci = c_ref[d, 0], c_ref[d, 1]
        for i in range(S5_T):
            e = i + 1 if d == 0 else S5_T - i
            pr, pi = pw_ref[d, 0, e:e + 1, :], pw_ref[d, 1, e:e + 1, :]
            wt_ref[d, rows * i:rows * (i + 1), 0:half] = (pr * cr - pi * ci).astype(mdt)
            wt_ref[d, rows * i:rows * (i + 1), half:2 * half] = (-(pr * ci + pi * cr)).astype(mdt)
    nlag = 2 * S5_T - 1
    zero = jnp.zeros((rows, half), F32)
    for l in range(nlag + 1):
        lag = l - (S5_T - 1)
        for d, active in ((1, lag <= 0), (0, 0 <= lag < S5_T)):
            col = 2 * half * (1 - d)
            if active:
                cr, ci = c_ref[d, 0], c_ref[d, 1]
                pr, pi = pw_ref[d, 0, abs(lag):abs(lag) + 1, :], pw_ref[d, 1, abs(lag):abs(lag) + 1, :]
                lag_ref[rows * l:rows * (l + 1), col:col + half] = pr * cr - pi * ci
                lag_ref[rows * l:rows * (l + 1), col + half:col + 2 * half] = pr * ci + pi * cr
            else:
                lag_ref[rows * l:rows * (l + 1), col:col + half] = zero
                lag_ref[rows * l:rows * (l + 1), col + half:col + 2 * half] = zero
    lhs = jnp.concatenate([b_ref[1, 0], -b_ref[1, 1], b_ref[0, 0], -b_ref[0, 1]], axis=1)
    kall = lax.dot_general(lhs, lag_ref[...], (((1,), (1,)), ((), ())), preferred_element_type=F32,
                           precision=lax.Precision.HIGHEST)
    for j in range(S5_T):
        wi_ref[rows * j:rows * (j + 1), :] = _lane_window(kall, rows * (S5_T - 1 - j), S5_T * rows).astype(mdt)
    y = _mm(u, wi_ref[...])
    y = y + _mm_nt(sf_ref[0].astype(mdt), wt_ref[0])
    y = y + _mm_nt(sb_ref[0].astype(mdt), wt_ref[1])
    o_ref[0] = y + u.astype(F32) * skip_ref[...]


def _s5_read(u_pairs, sf, sb, pw_k, b_k, c_k, skip, layer):
    a, nch, wd = u_pairs.shape
    blk = lambda *shape: pl.BlockSpec((1,) + shape, lambda i: (i, 0, 0))
    return pl.pallas_call(
        _s5_read_kernel,
        grid=(a,),
        in_specs=[blk(nch, wd), blk(nch, S5_STATE_W), blk(nch, S5_STATE_W), _pair_spec(layer, *pw_k.shape[2:]),
                  _pair_spec(layer, *b_k.shape[2:]), _pair_spec(layer, *c_k.shape[2:]),
                  _pair_spec(layer, *skip.shape[2:])],
        out_specs=blk(nch, wd),
        out_shape=jax.ShapeDtypeStruct((a, nch, wd), F32),
        scratch_shapes=[pltpu.VMEM((2, wd, S5_STATE_W), u_pairs.dtype), pltpu.VMEM((wd, wd), u_pairs.dtype),
                        pltpu.VMEM((2 * wd, 2 * S5_STATE_W), F32)],
        compiler_params=_params(("arbitrary",)),
        name="s5_read",
    )(u_pairs, sf, sb, pw_k, b_k, c_k, skip)


def _s5_mixer(u_pairs, weights, layer):
    pw_k, b_k, c_k, scan_tab, skip = weights
    drive = _s5_drive(u_pairs, pw_k, b_k, layer)
    sf, sb = _s5_scan(drive, scan_tab, layer)
    return _s5_read(u_pairs, sf, sb, pw_k, b_k, c_k, skip, layer)


def _layer_norm(x, g, b):
    mu = jnp.mean(x, axis=-1, keepdims=True)
    d = x - mu
    var = jnp.mean(d * d, axis=-1, keepdims=True)
    return d * lax.rsqrt(var + LN_EPS) * g + b


def _residual_rows(refs):
    if len(refs) == 1:
        return lambda rows: refs[0][rows, :]
    head_ref, body_ref = refs
    is_head = pl.program_id(0) == 0
    return lambda rows: jnp.where(is_head, head_ref[rows, :], body_ref[rows, :])


def _load_cast(src_hbm, dst_ref, stage_ref, sems, col_scale=None):
    chunk = stage_ref.shape[1]
    nchunk = src_hbm.shape[0] // chunk

    def copy(k):
        slot = k % 2
        return pltpu.make_async_copy(src_hbm.at[pl.ds(k * chunk, chunk), :], stage_ref.at[slot], sems.at[slot])

    copy(0).start()
    for k in range(nchunk):
        if k + 1 < nchunk:
            copy(k + 1).start()
        copy(k).wait()
        vals = stage_ref[k % 2]
        if col_scale is not None:
            vals = vals * col_scale
        dst_ref[k * chunk:(k + 1) * chunk, :] = vals.astype(dst_ref.dtype)


def _post_kernel(*refs, split, layer):
    nres = 2 if split else 1
    load_x = _residual_rows(refs[:nres])
    (att_ref, ret_ref, s5_ref, mod_ref, permt_ref, wglu_ref, bglu_ref, wo_hbm, g1_ref, b1_ref, w1_hbm, w2_hbm,
     g2_ref, b2_ref, o_ref, wo_ref, w1_ref, w2_ref, stage1_ref, stage2_ref, sem_ref) = refs[nres:]
    woa_ref = wo_ref.at[0:ATT_WIDTH]
    wor_ref = wo_ref.at[ATT_WIDTH:ATT_WIDTH + RET_WIDTH]
    wos_ref = wo_ref.at[ATT_WIDTH + RET_WIDTH:ATT_WIDTH + RET_WIDTH + S5_WIDTH]
    mdt = w1_ref.dtype

    @pl.when(pl.program_id(0) == 0)
    def _stage_weights():
        _load_cast(wo_hbm.at[layer], wo_ref, stage2_ref, sem_ref.at[1])
        _load_cast(w1_hbm.at[layer], w1_ref, stage1_ref, sem_ref.at[0])
        _load_cast(w2_hbm.at[layer], w2_ref, stage2_ref, sem_ref.at[1])
    sub = o_ref.shape[0] // POST_SPLIT
    csub = sub // S5_T
    nff = D_FF // FF_CHUNK

    def mix(part):
        rows = slice(sub * part, sub * (part + 1))
        zrows = []
        for i in range(S5_T):
            src_vreg, src_blk = divmod(i * S5_PAIR_W, LANES)
            src_blk //= S5_PAIR_W
            cols = [_lane_block_shuffle(
                lambda a: s5_ref[a, csub * part:csub * (part + 1), src_vreg * LANES:(src_vreg + 1) * LANES],
                src_blk, w) for w in range(S5_WIDTH // LANES)]
            zrows.append(jnp.concatenate(cols, axis=1))
        hs = jax.nn.gelu(jnp.concatenate(zrows, axis=0))
        gate = jax.nn.sigmoid(_mm(hs.astype(mdt), wglu_ref[...]) + bglu_ref[...])
        s5 = _mm(permt_ref[...], (hs * gate).astype(mdt)).astype(mdt)
        return _mm(att_ref[rows, :], woa_ref[...]) + _mm(ret_ref[rows, :], wor_ref[...]) + _mm(s5, wos_ref[...])

    def norm1(part, ox):
        rows = slice(sub * part, sub * (part + 1))
        x1 = _layer_norm(DEEPNORM_ALPHA * load_x(rows) + mod_ref[2:3, :] * ox, g1_ref[...], b1_ref[...])
        return x1, (x1 * (1.0 + mod_ref[4:5, :]) + mod_ref[3:4, :]).astype(mdt)

    def ff(h, c):
        a = _mm(h, w1_ref[:, FF_CHUNK * c:FF_CHUNK * (c + 1)])
        a = jnp.square(jnp.maximum(a, 0.0)).astype(mdt)
        return _mm(a, w2_ref[FF_CHUNK * c:FF_CHUNK * (c + 1), :])

    def norm2(part, x1, acc):
        rows = slice(sub * part, sub * (part + 1))
        o_ref[rows, :] = _layer_norm(DEEPNORM_ALPHA * x1 + mod_ref[5:6, :] * acc, g2_ref[...], b2_ref[...])

    nstage = nff + 3
    state = [dict() for _ in range(POST_SPLIT)]
    for part, stage in POST_PROGRAM:
        st = state[part]
        if stage == 0:
            st["ox"] = mix(part)
        elif stage == 1:
            st["x1"], st["h"] = norm1(part, st.pop("ox"))
        elif stage < nstage - 1:
            term = ff(st["h"], stage - 2)
            st["acc"] = term if stage == 2 else st["acc"] + term
        else:
            norm2(part, st["x1"], st["acc"])


def _residual_specs(residual):
    tm = ROW_TILE
    if len(residual) == 1:
        return [pl.BlockSpec((tm, D_MODEL), lambda i: (i, 0))]
    head_tiles = CTX_PAD // tm
    return [pl.BlockSpec((tm, D_MODEL), lambda i: (jnp.minimum(i, head_tiles - 1), 0)),
            pl.BlockSpec((tm, D_MODEL), lambda i: (jnp.maximum(i - head_tiles, 0), 0))]


def _post(residual, att, ret, s5_pairs, mods, layer, permt, wglu, bglu, wo, g1, b1, w1, w2, g2, b2, skip_context):
    rows = att.shape[0]
    tm = ROW_TILE
    off = CTX_PAD // tm if skip_context else 0
    assert not (skip_context and len(residual) > 1)
    row_blk = lambda width: pl.BlockSpec((tm, width), lambda i: (i + off, 0))
    full = lambda arr: pl.BlockSpec(arr.shape, lambda i: (0,) * arr.ndim)
    hbm = lambda arr: pl.BlockSpec(memory_space=pl.ANY)
    vec = lambda v: v.reshape(1, -1).astype(F32)
    small = [(permt, full), (wglu, full), (vec(bglu), full), (wo, hbm), (vec(g1), full), (vec(b1), full),
             (w1, hbm), (w2, hbm), (vec(g2), full), (vec(b2), full)]
    res_specs = [row_blk(D_MODEL)] if skip_context else _residual_specs(residual)
    return pl.pallas_call(
        functools.partial(_post_kernel, split=len(residual) > 1, layer=layer),
        grid=(rows // tm - off,),
        in_specs=res_specs + [row_blk(ATT_WIDTH), row_blk(RET_WIDTH),
                              pl.BlockSpec((S5_PAIRS, tm // S5_T, S5_T * S5_PAIR_W), lambda i: (0, i + off, 0)),
                              pl.BlockSpec((None, None, N_ADA, D_MODEL),
                                           lambda i: (layer, jnp.where(i + off == 0, 1, 0), 0, 0))]
                 + [spec(arr) for arr, spec in small],
        out_specs=pl.BlockSpec((tm, D_MODEL), lambda i: (i, 0)),
        out_shape=jax.ShapeDtypeStruct((rows - off * tm, D_MODEL), F32),
        scratch_shapes=[pltpu.VMEM((ATT_WIDTH + RET_WIDTH + S5_WIDTH, D_MODEL), MXU_DTYPE),
                        pltpu.VMEM((D_MODEL, D_FF), MXU_DTYPE), pltpu.VMEM((D_FF, D_MODEL), MXU_DTYPE),
                        pltpu.VMEM((2, STAGE_BYTES // (4 * D_FF), D_FF), F32),
                        pltpu.VMEM((2, STAGE_BYTES // (4 * D_MODEL), D_MODEL), F32),
                        pltpu.SemaphoreType.DMA((2, 2))],
        compiler_params=_params(("arbitrary",)),
        name="post",
    )(*residual, att, ret, s5_pairs, mods, *[arr for arr, _ in small])


def kernel(x, c, ctx, c_ctx, w_ada, b_ada, w_in, att_sink, ret_decay_logit, s5_lambda_re, s5_lambda_im, s5_b_re,
           s5_b_im, s5_c_re, s5_c_im, s5_log_dt, s5_d, w_glu, b_glu, w_out, ln1_g, ln1_b, w_ff1, w_ff2, ln2_g,
           ln2_b):
    assert x.shape[0] == 1 and x.shape[2] == D_MODEL and ctx.shape[1] == CTX_LEN
    seq = x.shape[1]
    assert seq % ROW_TILE == 0
    residual = (jnp.pad(ctx[0], ((0, CTX_PAD - CTX_LEN), (0, 0))), x[0])
    cond =jnp.zeros((8, D_MODEL), F32).at[0].set(c[0]).at[1].set(c_ctx)
    mods = _modulation(cond, w_ada, b_ada).reshape(DEPTH, 8, N_ADA, D_MODEL)
    tabs = _rope_tables(seq)
    masks = _attention_masks()
    perm = _chunk_perm(ROW_TILE // POST_SPLIT, MXU_DTYPE)
    permt = perm.T
    col_scale = jnp.ones((IN_WIDTH,), F32).at[COL_AQ:COL_AK].set(HEAD_DIM ** -0.5 * LOG2E)
    col_scale = col_scale.at[COL_RQ:COL_RK].set(HEAD_DIM ** -0.5)
    s5w = jax.vmap(_s5_weights)(s5_lambda_re, s5_lambda_im, s5_b_re, s5_b_im, s5_c_re, s5_c_im, s5_log_dt, s5_d)
    log_gamma = jax.nn.log_sigmoid(ret_decay_logit.astype(F32))
    for l in range(DEPTH):
        proj, u_pairs = _in_proj(residual, mods, l, w_in, col_scale.reshape(1, IN_WIDTH), tabs, perm)
        att = _attention(proj, att_sink[l].astype(F32) * LOG2E, masks)
        ret = _retention(proj, log_gamma[l])
        s5 = _s5_mixer(u_pairs, s5w, l)
        stream = _post(residual, att, ret, s5, mods, l, permt, w_glu[l].astype(MXU_DTYPE), b_glu[l],
                       w_out, ln1_g[l], ln1_b[l], w_ff1, w_ff2, ln2_g[l], ln2_b[l],
                       skip_context=(l == DEPTH - 1))
        residual = (stream,)
    return stream[None]
```

```python
import functools
import math

import jax
import jax.numpy as jnp
from jax import lax
from jax.experimental import pallas as pl
from jax.experimental.pallas import tpu as pltpu

F32 = jnp.float32
MXU_DTYPE = jnp.bfloat16

D_MODEL = 1024
DEPTH = 4
GRID_W = 64
CTX_LEN = 256
CTX_PAD = 512
HEAD_DIM = 64
ATT_HEADS = 8
ATT_KV_HEADS = 2
ATT_BLOCK = 128
ATT_LOOKAHEAD = 2
ROPE_BASE = 10000.0
RET_HEADS = 4
RET_CHUNK = 256
S5_CH = 16
S5_GROUPS = 16
S5_STATE = 64
S5_T = 16
S5_PAIRS = S5_GROUPS // 2
S5_PAIR_W = 2 * S5_CH
S5_TILE = CTX_PAD // S5_T
SCAN_ROWS = 8
S5_STATE_W = 2 * 2 * S5_STATE
ATT_WIDTH = ATT_HEADS * HEAD_DIM
KV_WIDTH = ATT_KV_HEADS * HEAD_DIM
RET_WIDTH = RET_HEADS * HEAD_DIM
S5_WIDTH = S5_GROUPS * S5_CH
IN_WIDTH = ATT_WIDTH + 2 * KV_WIDTH + 4 * RET_WIDTH + S5_WIDTH
D_FF = 4 * D_MODEL
FF_CHUNK = 1024
N_ADA = 6
LN_EPS = 1e-5
GN_EPS = 1e-5
DEEPNORM_ALPHA = (2 * DEPTH) ** 0.25
ROW_TILE = 512
POST_SPLIT = 2
_NFF = D_FF // FF_CHUNK
POST_PROGRAM = (((0, 0), (0, 1), (1, 0), (0, 2), (1, 1)) + tuple((0, 2 + c) for c in range(1, _NFF))
                + ((1, 2), (0, 2 + _NFF)) + tuple((1, 2 + c) for c in range(1, _NFF)) + ((1, 2 + _NFF),))
NEG_BIG = -1e30
LOG2E = math.log2(math.e)
LANES = 128
VMEM_LIMIT = 56 * 1024 * 1024
STAGE_BYTES = 2 * 1024 * 1024

COL_AQ, COL_AK, COL_AV = 0, ATT_WIDTH, ATT_WIDTH + KV_WIDTH
COL_RQ = ATT_WIDTH + 2 * KV_WIDTH
COL_RK, COL_RV, COL_RG = COL_RQ + RET_WIDTH, COL_RQ + 2 * RET_WIDTH, COL_RQ + 3 * RET_WIDTH
COL_S5 = COL_RQ + 4 * RET_WIDTH


def _mm(a, b):
    return jnp.dot(a, b, preferred_element_type=F32)


def _mm_nt(a, b):
    return lax.dot_general(a, b, (((1,), (1,)), ((), ())), preferred_element_type=F32)


def _mm_tn(a, b):
    return lax.dot_general(a, b, (((0,), (0,)), ((), ())), preferred_element_type=F32)


def _params(sem):
    return pltpu.CompilerParams(dimension_semantics=sem, vmem_limit_bytes=VMEM_LIMIT)


def _mod_kernel(cond_ref, w_ref, b_ref, o_ref):
    c = cond_ref[...]
    s = c * jax.nn.sigmoid(c)
    o_ref[0] = jnp.dot(s, w_ref[0], preferred_element_type=F32, precision=lax.Precision.HIGHEST) + b_ref[0]


def _modulation(cond, w_ada, b_ada):
    tn = 1536
    n = N_ADA * D_MODEL
    return pl.pallas_call(
        _mod_kernel,
        grid=(DEPTH, n // tn),
        in_specs=[
            pl.BlockSpec((8, D_MODEL), lambda l, j: (0, 0)),
            pl.BlockSpec((1, D_MODEL, tn), lambda l, j: (l, 0, j)),
            pl.BlockSpec((1, 1, tn), lambda l, j: (l, 0, j)),
        ],
        out_specs=pl.BlockSpec((1, 8, tn), lambda l, j: (l, 0, j)),
        out_shape=jax.ShapeDtypeStruct((DEPTH, 8, n), F32),
        compiler_params=_params(("arbitrary", "arbitrary")),
        name="modulation",
    )(cond, w_ada, b_ada.reshape(DEPTH, 1, n))


def _lane_block_shuffle(src_rows, src_lane_blk, out_vreg):
    acc = None
    for q in range(LANES // S5_PAIR_W):
        piece = src_rows(out_vreg * (LANES // S5_PAIR_W) + q)
        shift = (S5_PAIR_W * (q - src_lane_blk)) % LANES
        if shift:
            piece = pltpu.roll(piece, shift, 1)
        if acc is None:
            acc = piece
        else:
            lane_blk = lax.broadcasted_iota(jnp.int32, piece.shape, 1) // S5_PAIR_W
            acc = jnp.where(lane_blk == q, piece, acc)
    return acc


def _in_proj_kernel(*refs, split, layer):
    nres = 2 if split else 1
    (mod_ref, w_hbm, scale_ref, ca_ref, sa_ref, cr_ref, sr_ref, perm_ref, o_ref, u_ref, w_ref, stage_ref,
     sem_ref) = refs[nres:]

    @pl.when(pl.program_id(0) == 0)
    def _stage_weights():
        _load_cast(w_hbm.at[layer], w_ref, stage_ref, sem_ref, col_scale=scale_ref[...])

    x = _residual_rows(refs[:nres])(slice(None))
    h = (x * (1.0 + mod_ref[1:2, :]) + mod_ref[0:1, :]).astype(w_ref.dtype)
    lane = lax.broadcasted_iota(jnp.int32, (x.shape[0], LANES), 1)
    first_att = (lane & 31) < 16
    first_ret = (lane & 63) < 32

    def proj(c0, c1):
        return _mm(h, w_ref[:, c0:c1])

    def rope_store(p, c0, width, cos, sin, first, half):
        for b in range(width // LANES):
            blk = p[:, LANES * b:LANES * (b + 1)]
            rot = jnp.where(first, pltpu.roll(blk, LANES - half, 1), pltpu.roll(blk, half, 1))
            o_ref[:, c0 + LANES * b:c0 + LANES * (b + 1)] = (blk * cos + rot * sin).astype(o_ref.dtype)

    def plain_store(p, c0, width):
        o_ref[:, c0:c0 + width] = p.astype(o_ref.dtype)

    u = proj(COL_S5, IN_WIDTH).astype(w_ref.dtype)
    sub = perm_ref.shape[0]
    nchunk = sub // S5_T
    for part in range(x.shape[0] // sub):
        g = _mm(perm_ref[...], u[sub * part:sub * (part + 1)])
        for a in range(S5_PAIRS):
            vreg_col, lane_blk = divmod(a * S5_PAIR_W, LANES)
            lane_blk //= S5_PAIR_W
            for v in range(S5_T * S5_PAIR_W // LANES):
                slab = _lane_block_shuffle(
                    lambda j: g[nchunk * j:nchunk * (j + 1), vreg_col * LANES:(vreg_col + 1) * LANES], lane_blk, v)
                u_ref[a, nchunk * part:nchunk * (part + 1), LANES * v:LANES * (v + 1)] = slab.astype(u_ref.dtype)

    ca, sa, cr, sr = ca_ref[...], sa_ref[...], cr_ref[...], sr_ref[...]
    att_rope = (ca, sa, first_att, HEAD_DIM // 4)
    ret_rope = (cr, sr, first_ret, HEAD_DIM // 2)
    groups = [(COL_AQ, ATT_WIDTH, att_rope), (COL_AK, KV_WIDTH, att_rope), (COL_RQ, RET_WIDTH, ret_rope),
              (COL_RK, RET_WIDTH, ret_rope), (COL_AV, KV_WIDTH, None), (COL_RV, COL_S5 - COL_RV, None)]
    pending = None
    for group in groups + [None]:
        nxt = None if group is None else (proj(group[0], group[0] + group[1]),) + group
        if pending is not None:
            p, pc0, pwidth, prope = pending
            if prope is None:
                plain_store(p, pc0, pwidth)
            else:
                rope_store(p, pc0, pwidth, *prope)
        pending = nxt


def _chunk_perm(tile_rows, dtype):
    nchunk = tile_rows // S5_T
    r = jnp.arange(tile_rows)
    src = S5_T * (r % nchunk) + r // nchunk
    return (src[:, None] == jnp.arange(tile_rows)[None, :]).astype(dtype)


def _in_proj(residual, mods, layer, w_in, col_scale, tabs, perm):
    rows = tabs[0].shape[0]
    tm = ROW_TILE
    tab_spec = pl.BlockSpec((tm, LANES), lambda i: (i, 0))
    nch = rows // S5_T
    return pl.pallas_call(
        functools.partial(_in_proj_kernel, split=len(residual) > 1, layer=layer),
        grid=(rows // tm,),
        in_specs=_residual_specs(residual) + [
            pl.BlockSpec((None, None, N_ADA, D_MODEL), lambda i: (layer, jnp.where(i == 0, 1, 0), 0, 0)),
            pl.BlockSpec(memory_space=pl.ANY),
            pl.BlockSpec((1, IN_WIDTH), lambda i: (0, 0)),
            tab_spec, tab_spec, tab_spec, tab_spec,
            pl.BlockSpec(perm.shape, lambda i: (0, 0)),
        ],
        out_specs=[pl.BlockSpec((tm, COL_S5), lambda i: (i, 0)),
                   pl.BlockSpec((S5_PAIRS, tm // S5_T, S5_T * S5_PAIR_W), lambda i: (0, i, 0))],
        out_shape=[jax.ShapeDtypeStruct((rows, COL_S5), MXU_DTYPE),
                   jax.ShapeDtypeStruct((S5_PAIRS, nch, S5_T * S5_PAIR_W), MXU_DTYPE)],
        scratch_shapes=[pltpu.VMEM((D_MODEL, IN_WIDTH), MXU_DTYPE),
                        pltpu.VMEM((2, STAGE_BYTES // (4 * IN_WIDTH), IN_WIDTH), F32),
                        pltpu.SemaphoreType.DMA((2,))],
        compiler_params=_params(("arbitrary",)),
        name="in_proj",
    )(*residual, mods, w_in, col_scale, *tabs, perm)


def _rope_tables(seq):
    half_a = HEAD_DIM // 4
    half_r = HEAD_DIM // 2
    nrow = seq // GRID_W
    inv_a = ROPE_BASE ** (-jnp.arange(half_a, dtype=F32) / half_a)
    inv_r = ROPE_BASE ** (-jnp.arange(half_r, dtype=F32) / half_r)
    ang_r = jnp.arange(nrow, dtype=F32)[:, None] * inv_a[None, :]
    ang_c = jnp.arange(GRID_W, dtype=F32)[:, None] * inv_a[None, :]
    ang_t = jnp.arange(seq, dtype=F32)[:, None] * inv_r[None, :]
    hp = lax.Precision.HIGHEST
    lane = jnp.arange(LANES)
    within = lane % HEAD_DIM
    pick_a = (within % half_a)[None, :] == jnp.arange(half_a)[:, None]
    exp_row = (pick_a & (within < 2 * half_a)[None, :]).astype(F32)
    exp_col = (pick_a & (within >= 2 * half_a)[None, :]).astype(F32)
    exp_t = ((lane % half_r)[None, :] == jnp.arange(half_r)[:, None]).astype(F32)
    sign_a = jnp.where(within % (2 * half_a) < half_a, -1.0, 1.0).astype(F32)
    sign_r = jnp.where(within < half_r, -1.0, 1.0).astype(F32)

    def att_table(fn):
        by_row = jnp.dot(fn(ang_r), exp_row, precision=hp)
        by_col = jnp.dot(fn(ang_c), exp_col, precision=hp)
        return (by_row[:, None, :] + by_col[None, :, :]).reshape(seq, LANES)

    cos_a = att_table(jnp.cos)
    sin_a = att_table(jnp.sin) * sign_a
    cos_r = jnp.dot(jnp.cos(ang_t), exp_t, precision=hp)
    sin_r = jnp.dot(jnp.sin(ang_t), exp_t, precision=hp) * sign_r
    pad = lambda tab, ident: jnp.pad(tab, ((CTX_PAD, 0), (0, 0)), constant_values=ident)
    return pad(cos_a, 1.0), pad(sin_a, 0.0), pad(cos_r, 1.0), pad(sin_r, 0.0)


def _swap_halves(x):
    if x.dtype.itemsize == 4:
        return pltpu.roll(x, 64, 1)
    packed = pltpu.bitcast(x, jnp.uint32)
    return pltpu.bitcast(pltpu.roll(packed, 64, 1), x.dtype)


def _dup_heads(x):
    sw = _swap_halves(x)
    lo = lax.broadcasted_iota(jnp.int32, x.shape, 1) < HEAD_DIM
    return jnp.where(lo, x, sw), jnp.where(lo, sw, x)


def _attn_kernel(sink_ref, q_ref, km_ref, kp_ref, kn_ref, vm_ref, vp_ref, vn_ref, kc_ref, vc_ref, mask_ref,
                 o_ref, k2_ref, v2_ref, kc2_ref, vc2_ref):
    i = pl.program_id(0)
    last_blk = pl.num_programs(0) * (ROW_TILE // ATT_BLOCK) - 1
    blk = ATT_BLOCK
    def spread(src, ones_upper):
        x = src[...]
        a, b = _dup_heads(x)
        if ones_upper:
            upper = lax.broadcasted_iota(jnp.int32, x.shape, 1) >= HEAD_DIM
            a = jnp.where(upper, jnp.ones_like(a), a)
            b = jnp.where(upper, jnp.ones_like(b), b)
        return a, b

    for dst, parts, is_v in ((k2_ref, (kp_ref, km_ref, kn_ref), False), (v2_ref, (vp_ref, vm_ref, vn_ref), True)):
        row = 0
        for part in parts:
            a, b = spread(part, is_v)
            n = part.shape[0]
            dst[0, row:row + n, :] = a
            dst[1, row:row + n, :] = b
            row += n
    for dst, src, is_v in ((kc2_ref, kc_ref, False), (vc2_ref, vc_ref, True)):
        a, b = spread(src, is_v)
        dst[0] = a
        dst[1] = b

    lo = lax.broadcasted_iota(jnp.int32, (blk, LANES), 1) < HEAD_DIM
    group = ATT_HEADS // ATT_KV_HEADS

    nloc = 3 * blk

    def scores(j, kv):
        r0 = j * blk
        qt = q_ref[r0:r0 + blk, group * HEAD_DIM * kv:group * HEAD_DIM * (kv + 1)]
        parts = []
        for g in range(group):
            qc = qt[:, LANES * (g // 2):LANES * (g // 2 + 1)]
            keep = lo if g % 2 == 0 else jnp.logical_not(lo)
            parts.append(jnp.where(keep, qc, jnp.zeros_like(qc)))
        qs = jnp.concatenate(parts, axis=0)
        return _mm_nt(qs, k2_ref[kv, r0:r0 + nloc, :]), _mm_nt(qs, kc2_ref[kv])

    def finish(j, kv, s_loc, s_ctx):
        r0 = j * blk
        gblk = i * (ROW_TILE // blk) + j
        sel = jnp.where(i == 0, 3, jnp.where(gblk == CTX_PAD // blk, 0, jnp.where(gblk == last_blk, 2, 1)))
        bias = mask_ref[sel]
        probs, sink_w = [], []
        for g in range(group):
            s = jnp.concatenate([s_loc[blk * g:blk * (g + 1)] + bias, s_ctx[blk * g:blk * (g + 1)]], axis=1)
            sk = sink_ref[group * kv + g]
            m = jnp.maximum(jnp.max(s, axis=-1, keepdims=True), sk)
            probs.append(jnp.exp2(s - m).astype(o_ref.dtype))
            sink_w.append(jnp.exp2(sk - m))
        p = jnp.concatenate(probs, axis=0)
        o = _mm(p[:, :nloc], v2_ref[kv, r0:r0 + nloc, :]) + _mm(p[:, nloc:], vc2_ref[kv])
        for half in range(group // 2):
            even, odd = 2 * half, 2 * half + 1
            oe = o[blk * even:blk * (even + 1)]
            oo = o[blk * odd:blk * (odd + 1)]
            y_even = oe * (1.0 / (pltpu.roll(oe, HEAD_DIM, 1) + sink_w[even]))
            y_odd = pltpu.roll(oo, HEAD_DIM, 1) * (1.0 / (oo + sink_w[odd]))
            c0 = group * HEAD_DIM * kv + LANES * half
            o_ref[r0:r0 + blk, c0:c0 + LANES] = jnp.where(lo, y_even, y_odd).astype(o_ref.dtype)

    items = [(j, kv) for j in range(ROW_TILE // blk) for kv in range(ATT_KV_HEADS)]
    pending = {}
    for t in range(len(items) + ATT_LOOKAHEAD):
        if t < len(items):
            pending[t] = scores(*items[t])
        if t >= ATT_LOOKAHEAD:
            finish(*items[t - ATT_LOOKAHEAD], *pending.pop(t - ATT_LOOKAHEAD))


def _attention_masks():
    qi = jnp.arange(ATT_BLOCK)[:, None]
    kj = jnp.arange(3 * ATT_BLOCK)[None, :]
    band = jnp.abs(kj - ATT_BLOCK - qi) <= ATT_BLOCK
    first = band & (kj >= ATT_BLOCK)
    last = band & (kj < 2 * ATT_BLOCK)
    none = jnp.zeros_like(band)
    masks = jnp.stack([first, band, last, none])
    return jnp.where(masks, 0.0, NEG_BIG).astype(F32)


def _attention(proj, sink, masks):
    rows = proj.shape[0]
    tm, blk = ROW_TILE, ATT_BLOCK
    per = tm // blk
    nblk = rows // blk
    ck, cv = COL_AK // KV_WIDTH, COL_AV // KV_WIDTH
    dt = proj.dtype
    return pl.pallas_call(
        _attn_kernel,
        grid=(rows // tm,),
        in_specs=[
            pl.BlockSpec(memory_space=pltpu.SMEM),
            pl.BlockSpec((tm, ATT_WIDTH), lambda i: (i, 0)),
            pl.BlockSpec((tm, KV_WIDTH), lambda i: (i, ck)),
            pl.BlockSpec((blk, KV_WIDTH), lambda i: (jnp.maximum(i * per - 1, 0), ck)),
            pl.BlockSpec((blk, KV_WIDTH), lambda i: (jnp.minimum((i + 1) * per, nblk - 1), ck)),
            pl.BlockSpec((tm, KV_WIDTH), lambda i: (i, cv)),
            pl.BlockSpec((blk, KV_WIDTH), lambda i: (jnp.maximum(i * per - 1, 0), cv)),
            pl.BlockSpec((blk, KV_WIDTH), lambda i: (jnp.minimum((i + 1) * per, nblk - 1), cv)),
            pl.BlockSpec((CTX_LEN, KV_WIDTH), lambda i: (0, ck)),
            pl.BlockSpec((CTX_LEN, KV_WIDTH), lambda i: (0, cv)),
            pl.BlockSpec((4, blk, 3 * blk), lambda i: (0, 0, 0)),
        ],
        out_specs=pl.BlockSpec((tm, ATT_WIDTH), lambda i: (i, 0)),
        out_shape=jax.ShapeDtypeStruct((rows, ATT_WIDTH), dt),
        scratch_shapes=[
            pltpu.VMEM((2, tm + 2 * blk, KV_WIDTH), dt),
            pltpu.VMEM((2, tm + 2 * blk, KV_WIDTH), dt),
            pltpu.VMEM((2, CTX_LEN, KV_WIDTH), dt),
            pltpu.VMEM((2, CTX_LEN, KV_WIDTH), dt),
        ],
        compiler_params=_params(("arbitrary",)),
        name="attention",
    )(sink, proj, proj, proj, proj, proj, proj, proj, proj, proj, masks)


def _ret_kernel(*refs, sup, ntile):
    lg_ref, q_ref = refs[:2]
    k_refs, v_refs = refs[2:2 + sup], refs[2 + sup:2 + 2 * sup]
    g_ref, o_ref, sb_ref, s_ref, dm_ref, tab_ref, gbd_ref = refs[2 + 2 * sup:]
    k_ref, v_ref = k_refs[0], v_refs[0]
    step = pl.program_id(0)
    nsup = (ntile - 1) // sup
    ph = jnp.where(step <= nsup, 0, 1)
    t = jnp.where(step <= nsup, step, step - (nsup + 1))
    c = RET_CHUNK
    w = RET_WIDTH
    per = q_ref.shape[0] // c
    mdt = sb_ref.dtype
    rows = lambda ci: slice(c * ci, c * (ci + 1))

    def lane_vec(direction, shape, axis):
        head = lax.broadcasted_iota(jnp.int32, shape, axis) // HEAD_DIM
        out = jnp.full(shape, lg_ref[direction, RET_HEADS - 1], F32)
        for h in range(RET_HEADS - 2, -1, -1):
            out = jnp.where(head == h, lg_ref[direction, h], out)
        return out

    @pl.when(jnp.logical_and(ph == 0, t == 0))
    def _init_tables():
        diff = (lax.broadcasted_iota(jnp.int32, (c, c), 0) - lax.broadcasted_iota(jnp.int32, (c, c), 1)).astype(F32)
        for h in range(RET_HEADS):
            dm_ref[h] = jnp.exp(jnp.where(diff >= 0, diff * lg_ref[0, h], -diff * lg_ref[1, h]))
        pos = lax.broadcasted_iota(jnp.int32, (c, w), 0).astype(F32)
        lgf = lane_vec(0, (c, w), 1)
        lgb = lane_vec(1, (c, w), 1)
        tab_ref[0] = jnp.exp((c - 1.0 - pos) * lgf)
        tab_ref[1] = jnp.exp((pos + 1.0) * lgf)
        tab_ref[2] = jnp.exp(pos * lgb)
        tab_ref[3] = jnp.exp((c - pos) * lgb)
        same = (lax.broadcasted_iota(jnp.int32, (w, w), 0) // HEAD_DIM
                == lax.broadcasted_iota(jnp.int32, (w, w), 1) // HEAD_DIM)
        bd = jnp.where(same, 1.0, 0.0)
        gbd_ref[0] = bd * jnp.exp(c * lane_vec(0, (w, w), 0))
        gbd_ref[1] = bd * jnp.exp(c * lane_vec(1, (w, w), 0))
        gbd_ref[2] = bd

    @pl.when(t == 0)
    def _reset_state():
        s_ref[...] = jnp.zeros_like(s_ref)

    def state_update(direction, key_tab, ci, kr=k_ref, vr=v_ref):
        kw = (kr[rows(ci), :].astype(F32) * tab_ref[key_tab]).astype(mdt)
        u = _mm_tn(kw, vr[rows(ci), :])
        s_ref[...] = gbd_ref[direction] * s_ref[...] + gbd_ref[2] * u

    @pl.when(jnp.logical_and(ph == 0, t == 0))
    def _backward_context():
        sb_ref[0] = s_ref[...].astype(mdt)
        state_update(1, 2, 0)

    @pl.when(jnp.logical_and(ph == 0, t > 0))
    def _backward_latent():
        base = 1 + per * sup * (nsup - t)
        for qi in range(sup - 1, -1, -1):
            for ci in range(per - 1, -1, -1):
                sb_ref[base + per * qi + ci] = s_ref[...].astype(mdt)
                state_update(1, 2, ci, k_refs[qi], v_refs[qi])

    head = lax.broadcasted_iota(jnp.int32, (c, w), 1) // HEAD_DIM

    def scores(ci):
        q = q_ref[rows(ci), :]
        qs = jnp.concatenate([jnp.where(head == h, q, jnp.zeros_like(q)) for h in range(RET_HEADS)], axis=0)
        return _mm_nt(qs, k_ref[rows(ci), :])

    def intra(ci, sc):
        scd = jnp.concatenate([sc[c * h:c * (h + 1)] * dm_ref[h] for h in range(RET_HEADS)], axis=0).astype(mdt)
        oi = _mm(scd, v_ref[rows(ci), :])
        o = jnp.where(head == 0, oi[0:c], 0.0)
        for h in range(1, RET_HEADS):
            o = o + jnp.where(head == h, oi[c * h:c * (h + 1)], 0.0)
        return o

    def cross(ci, idx):
        qf = q_ref[rows(ci), :].astype(F32)
        return (_mm((qf * tab_ref[1]).astype(mdt), s_ref[...].astype(mdt))
                + _mm((qf * tab_ref[3]).astype(mdt), sb_ref[idx]))

    def finish(ci, o):
        avg = (gbd_ref[2] * (1.0 / HEAD_DIM)).astype(mdt)
        o_hi = o.astype(mdt)
        d = o - (_mm(o_hi, avg) + _mm((o - o_hi.astype(F32)).astype(mdt), avg))
        var = _mm((d * d).astype(mdt), avg)
        gate = g_ref[rows(ci), :].astype(F32)
        y = d * lax.rsqrt(var + GN_EPS) * (gate * jax.nn.sigmoid(gate))
        o_ref[rows(ci), :] = y.astype(o_ref.dtype)

    @pl.when(jnp.logical_and(ph == 1, t == 0))
    def _forward_context():
        o = intra(0, scores(0)) + cross(0, 0)
        state_update(0, 0, 0)
        finish(0, o)
        for ci in range(1, per):
            o_ref[rows(ci), :] = jnp.zeros((c, w), o_ref.dtype)

    @pl.when(jnp.logical_and(ph == 1, t > 0))
    def _forward_latent():
        base = 1 + per * (t - 1)
        sc = [scores(ci) for ci in range(per)]
        outs = [intra(ci, sc[ci]) for ci in range(per)]
        for ci in range(per):
            outs[ci] = outs[ci] + cross(ci, base + ci)
            state_update(0, 0, ci)
        for ci in range(per):
            finish(ci, outs[ci])


def _retention(proj, log_gamma):
    rows = proj.shape[0]
    c = RET_CHUNK
    tm = ROW_TILE
    ntile = rows // tm
    nchunk = 1 + (rows - CTX_PAD) // c
    dt = proj.dtype
    cq, ckk, cvv, cg = (COL_RQ // RET_WIDTH, COL_RK // RET_WIDTH, COL_RV // RET_WIDTH, COL_RG // RET_WIDTH)

    nlat = ntile - 1
    sup = next(s for s in (4, 2, 1) if nlat % s == 0)
    nsup = nlat // sup
    nback = 1 + nsup

    def fw_blk(step):
        return jnp.maximum(step - nback, 0)

    def kv_spec(col, qi):
        def index(step):
            back = jnp.where(step == 0, 0, 1 + sup * (nsup - jnp.minimum(step, nsup)) + qi)
            fwd = step - nback if qi == 0 else 1 + qi
            return jnp.where(step < nback, back, fwd), col
        return pl.BlockSpec((tm, RET_WIDTH), index)

    return pl.pallas_call(
        functools.partial(_ret_kernel, sup=sup, ntile=ntile),
        grid=(nback + ntile,),
        in_specs=[pl.BlockSpec(memory_space=pltpu.SMEM),
                  pl.BlockSpec((tm, RET_WIDTH), lambda step: (fw_blk(step), cq))]
                 + [kv_spec(ckk, qi) for qi in range(sup)] + [kv_spec(cvv, qi) for qi in range(sup)]
                 + [pl.BlockSpec((tm, RET_WIDTH), lambda step: (fw_blk(step), cg))],
        out_specs=pl.BlockSpec((tm, RET_WIDTH), lambda step: (fw_blk(step), 0)),
        out_shape=jax.ShapeDtypeStruct((rows, RET_WIDTH), dt),
        scratch_shapes=[
            pltpu.VMEM((nchunk, RET_WIDTH, RET_WIDTH), dt),
            pltpu.VMEM((RET_WIDTH, RET_WIDTH), F32),
            pltpu.VMEM((RET_HEADS, c, c), F32),
            pltpu.VMEM((4, c, RET_WIDTH), F32),
            pltpu.VMEM((3, RET_WIDTH, RET_WIDTH), F32),
        ],
        compiler_params=_params(("arbitrary",)),
        name="retention",
    )(log_gamma, proj, *([proj] * (2 * sup)), proj)


def _s5_weights(lam_re, lam_im, b_re, b_im, c_re, c_im, log_dt, d_skip):
    hp = lax.Precision.HIGHEST
    tt, g, n, p, a = S5_T, S5_GROUPS, S5_STATE, S5_CH, S5_PAIRS
    lam = lax.complex(lam_re.astype(F32), lam_im.astype(F32))
    dtv = jnp.exp(log_dt.astype(F32))[..., None]
    lam_bar = jnp.exp(lam * dtv)
    bbar = ((lam_bar - 1.0) / lam)[..., None] * lax.complex(b_re.astype(F32), b_im.astype(F32))
    cmat = lax.complex(c_re.astype(F32), c_im.astype(F32))
    pw = [jnp.ones_like(lam_bar)]
    for _ in range(tt):
        pw.append(pw[-1] * lam_bar)
    pw = jnp.stack(pw, axis=1)
    eye2 = jnp.eye(2, dtype=F32)
    ri = lambda z, axis: jnp.stack([jnp.real(z), jnp.imag(z)], axis=axis)

    pw_l = pw.reshape(2, tt + 1, a, 2 * n)
    bbt = jnp.einsum('dahpn,gh->dagphn', jnp.swapaxes(bbar, -1, -2).reshape(2, a, 2, p, n), eye2)
    bbt = bbt.reshape(2, a, 2 * p, 2 * n)
    cct = jnp.einsum('dahpn,gh->dagphn', cmat.reshape(2, a, 2, p, n), eye2).reshape(2, a, 2 * p, 2 * n)
    pw_k = ri(pw_l, 1).transpose(3, 0, 1, 2, 4)
    b_k = ri(bbt, 2).transpose(1, 0, 2, 3, 4)
    c_k = ri(cct, 2).transpose(1, 0, 2, 3, 4)
    decay = [pw_l[:, tt]]
    for _ in range(SCAN_ROWS - 1):
        decay.append(decay[-1] * decay[0])
    decay = jnp.stack(decay, axis=0)
    rows8 = lambda z: jnp.broadcast_to(z[None], (SCAN_ROWS,) + z.shape)
    carry_w = jnp.stack([decay[:, 0], decay[::-1, 1]], axis=1)
    scan_tab = jnp.stack([rows8(decay[0]), rows8(decay[1]), rows8(decay[3]), carry_w], axis=0)
    scan_tab = ri(scan_tab, 0).transpose(4, 3, 1, 0, 2, 5)
    skip = jnp.tile(d_skip.astype(F32).reshape(a, 1, 2 * p), (1, tt, 1)).reshape(a, 1, tt * 2 * p)
    return pw_k, b_k, c_k, scan_tab, skip


def _pair_spec(layer, *shape):
    return pl.BlockSpec((None, None) + shape, lambda i: (layer, i) + (0,) * len(shape))


def _s5_drive_kernel(u_ref, pw_ref, b_ref, o_ref, w_ref):
    rows, half = S5_PAIR_W, LANES
    for d in range(2):
        br, bi = b_ref[d, 0], b_ref[d, 1]
        for j in range(S5_T):
            e = S5_T - 1 - j if d == 0 else j
            pr, pi = pw_ref[d, 0, e:e + 1, :], pw_ref[d, 1, e:e + 1, :]
            w_ref[rows * j:rows * (j + 1), 2 * half * d:2 * half * d + half] = (pr * br - pi * bi).astype(w_ref.dtype)
            w_ref[rows * j:rows * (j + 1), 2 * half * d + half:2 * half * (d + 1)] = (
                pr * bi + pi * br).astype(w_ref.dtype)
    o_ref[0] = _mm(u_ref[0], w_ref[...])


def _s5_drive(u_pairs, pw_k, b_k, layer):
    a, nch, wd = u_pairs.shape
    return pl.pallas_call(
        _s5_drive_kernel,
        grid=(a,),
        in_specs=[pl.BlockSpec((1, nch, wd), lambda i: (i, 0, 0)),
                  _pair_spec(layer, *pw_k.shape[2:]), _pair_spec(layer, *b_k.shape[2:])],
        out_specs=pl.BlockSpec((1, nch, 2 * S5_STATE_W), lambda i: (i, 0, 0)),
        out_shape=jax.ShapeDtypeStruct((a, nch, 2 * S5_STATE_W), F32),
        scratch_shapes=[pltpu.VMEM((wd, 2 * S5_STATE_W), u_pairs.dtype)],
        compiler_params=_params(("arbitrary",)),
        name="s5_drive",
    )(u_pairs, pw_k, b_k)


def _s5_scan_kernel(ef_ref, eb_ref, tab_ref, sf_ref, sb_ref, st_ref):
    t = pl.program_id(0)
    hw = LANES
    sub = SCAN_ROWS
    npair = tab_ref.shape[0]
    row = lax.broadcasted_iota(jnp.int32, (sub, hw), 0)

    @pl.when(t == 0)
    def _reset():
        st_ref[...] = jnp.zeros_like(st_ref)

    def shift(x, k, reverse):
        if reverse:
            return jnp.where(row < sub - k, pltpu.roll(x, sub - k, 0), 0.0)
        return jnp.where(row >= k, pltpu.roll(x, k, 0), 0.0)

    def scan_group(e_ref, o_ref, a, d, r0, cr, ci):
        reverse = d == 1
        xr = e_ref[a, pl.ds(r0, sub), 0:hw]
        xi = e_ref[a, pl.ds(r0, sub), hw:2 * hw]
        for step, k in enumerate((1, 2, 4)):
            ar, ai = tab_ref[a, d, step, 0], tab_ref[a, d, step, 1]
            sr, si = shift(xr, k, reverse), shift(xi, k, reverse)
            xr, xi = xr + ar * sr - ai * si, xi + ar * si + ai * sr
        wr, wi = tab_ref[a, d, 3, 0], tab_ref[a, d, 3, 1]
        fr = xr + wr * cr - wi * ci
        fi = xi + wr * ci + wi * cr
        edge = sub - 1 if reverse else 0
        o_ref[a, pl.ds(r0, sub), 0:hw] = jnp.where(row == edge, cr, shift(fr, 1, reverse))
        o_ref[a, pl.ds(r0, sub), hw:2 * hw] = jnp.where(row == edge, ci, shift(fi, 1, reverse))
        last = 0 if reverse else sub - 1
        return (jnp.broadcast_to(fr[last:last + 1], (sub, hw)), jnp.broadcast_to(fi[last:last + 1], (sub, hw)))

    def run(nrows):
        ngroups = nrows // sub

        def body(gi, carry):
            r0 = pl.multiple_of(gi * sub, sub)
            rb0 = pl.multiple_of((ngroups - 1 - gi) * sub, sub)
            new = []
            for a in range(npair):
                fr, fi, br, bi = carry[a]
                fr, fi = scan_group(ef_ref, sf_ref, a, 0, r0, fr, fi)
                br, bi = scan_group(eb_ref, sb_ref, a, 1, rb0, br, bi)
                new.append((fr, fi, br, bi))
            return tuple(new)

        init = tuple(tuple(st_ref[a, k] for k in range(4)) for a in range(npair))
        final = lax.fori_loop(0, ngroups, body, init)
        for a in range(npair):
            for k in range(4):
                st_ref[a, k] = final[a][k]

    @pl.when(t == 0)
    def _context():
        sf_ref[...] = jnp.zeros_like(sf_ref)
        sb_ref[...] = jnp.zeros_like(sb_ref)
        run(CTX_LEN // S5_T)

    @pl.when(t > 0)
    def _latent():
        run(S5_TILE)


def _s5_scan(drive, scan_tab, layer):
    a, nch, wd2 = drive.shape
    wd = wd2 // 2
    nt = nch // S5_TILE

    def bwd(t):
        return jnp.where(t == 0, 0, nt - t)

    return pl.pallas_call(
        _s5_scan_kernel,
        grid=(nt,),
        in_specs=[pl.BlockSpec((a, S5_TILE, wd), lambda t: (0, t, 0)),
                  pl.BlockSpec((a, S5_TILE, wd), lambda t: (0, bwd(t), 1)),
                  pl.BlockSpec((None,) + scan_tab.shape[1:], lambda t: (layer,) + (0,) * (scan_tab.ndim - 1))],
        out_specs=[pl.BlockSpec((a, S5_TILE, wd), lambda t: (0, t, 0)),
                   pl.BlockSpec((a, S5_TILE, wd), lambda t: (0, bwd(t), 0))],
        out_shape=[jax.ShapeDtypeStruct((a, nch, wd), F32), jax.ShapeDtypeStruct((a, nch, wd), F32)],
        scratch_shapes=[pltpu.VMEM((a, 4, SCAN_ROWS, LANES), F32)],
        compiler_params=_params(("arbitrary",)),
        name="s5_scan",
    )(drive, drive, scan_tab)


def _lane_window(x, start, width):
    cols = []
    for v in range(width // LANES):
        k0, off = divmod(start + LANES * v, LANES)
        lo = x[:, LANES * k0:LANES * (k0 + 1)]
        if off:
            hi = x[:, LANES * (k0 + 1):LANES * (k0 + 2)]
            lane = lax.broadcasted_iota(jnp.int32, lo.shape, 1)
            lo = jnp.where(lane < LANES - off, pltpu.roll(lo, LANES - off, 1), pltpu.roll(hi, LANES - off, 1))
        cols.append(lo)
    return jnp.concatenate(cols, axis=1)


def _s5_read_kernel(u_ref, sf_ref, sb_ref, pw_ref, b_ref, c_ref, skip_ref, o_ref, wt_ref, wi_ref, lag_ref):
    u = u_ref[0]
    mdt = u.dtype
    rows, half = S5_PAIR_W, LANES
    for d in range(2):
        cr, ci = c_ref[d, 0], c_ref[d, 1]
        for i in range(S5_T):
            e = i + 1 if d == 0 else S5_T - i
            pr, pi = pw_ref[d, 0, e:e + 1, :], pw_ref[d, 1, e:e + 1, :]
            wt_ref[d, rows * i:rows * (i + 1), 0:half] = (pr * cr - pi * ci).astype(mdt)
            wt_ref[d, rows * i:rows * (i + 1), half:2 * half] = (-(pr * ci + pi * cr)).astype(mdt)
    nlag = 2 * S5_T - 1
    zero = jnp.zeros((rows, half), F32)
    for l in range(nlag + 1):
        lag = l - (S5_T - 1)
        for d, active in ((1, lag <= 0), (0, 0 <= lag < S5_T)):
            col = 2 * half * (1 - d)
            if active:
                cr, ci = c_ref[d, 0], c_ref[d, 1]
                pr, pi = pw_ref[d, 0, abs(lag):abs(lag) + 1, :], pw_ref[d, 1, abs(lag):abs(lag) + 1, :]
                lag_ref[rows * l:rows * (l + 1), col:col + half] = pr * cr - pi * ci
                lag_ref[rows * l:rows * (l + 1), col + half:col + 2 * half] = pr * ci + pi * cr
            else:
                lag_ref[rows * l:rows * (l + 1), col:col + half] = zero
                lag_ref[rows * l:rows * (l + 1), col + half:col + 2 * half] = zero
    lhs = jnp.concatenate([b_ref[1, 0], -b_ref[1, 1], b_ref[0, 0], -b_ref[0, 1]], axis=1)
    kall = lax.dot_general(lhs, lag_ref[...], (((1,), (1,)), ((), ())), preferred_element_type=F32,
                           precision=lax.Precision.HIGHEST)
    for j in range(S5_T):
        wi_ref[rows * j:rows * (j + 1), :] = _lane_window(kall, rows * (S5_T - 1 - j), S5_T * rows).astype(mdt)
    y = _mm(u, wi_ref[...])
    y = y + _mm_nt(sf_ref[0].astype(mdt), wt_ref[0])
    y = y + _mm_nt(sb_ref[0].astype(mdt), wt_ref[1])
    o_ref[0] = y + u.astype(F32) * skip_ref[...]


def _s5_read(u_pairs, sf, sb, pw_k, b_k, c_k, skip, layer):
    a, nch, wd = u_pairs.shape
    blk = lambda *shape: pl.BlockSpec((1,) + shape, lambda i: (i, 0, 0))
    return pl.pallas_call(
        _s5_read_kernel,
        grid=(a,),
        in_specs=[blk(nch, wd), blk(nch, S5_STATE_W), blk(nch, S5_STATE_W), _pair_spec(layer, *pw_k.shape[2:]),
                  _pair_spec(layer, *b_k.shape[2:]), _pair_spec(layer, *c_k.shape[2:]),
                  _pair_spec(layer, *skip.shape[2:])],
        out_specs=blk(nch, wd),
        out_shape=jax.ShapeDtypeStruct((a, nch, wd), F32),
        scratch_shapes=[pltpu.VMEM((2, wd, S5_STATE_W), u_pairs.dtype), pltpu.VMEM((wd, wd), u_pairs.dtype),
                        pltpu.VMEM((2 * wd, 2 * S5_STATE_W), F32)],
        compiler_params=_params(("arbitrary",)),
        name="s5_read",
    )(u_pairs, sf, sb, pw_k, b_k, c_k, skip)


def _s5_mixer(u_pairs, weights, layer):
    pw_k, b_k, c_k, scan_tab, skip = weights
    drive = _s5_drive(u_pairs, pw_k, b_k, layer)
    sf, sb = _s5_scan(drive, scan_tab, layer)
    return _s5_read(u_pairs, sf, sb, pw_k, b_k, c_k, skip, layer)


def _layer_norm(x, g, b):
    mu = jnp.mean(x, axis=-1, keepdims=True)
    d = x - mu
    var = jnp.mean(d * d, axis=-1, keepdims=True)
    return d * lax.rsqrt(var + LN_EPS) * g + b


def _residual_rows(refs):
    if len(refs) == 1:
        return lambda rows: refs[0][rows, :]
    head_ref, body_ref = refs
    is_head = pl.program_id(0) == 0
    return lambda rows: jnp.where(is_head, head_ref[rows, :], body_ref[rows, :])


def _load_cast(src_hbm, dst_ref, stage_ref, sems, col_scale=None):
    chunk = stage_ref.shape[1]
    nchunk = src_hbm.shape[0] // chunk

    def copy(k):
        slot = k % 2
        return pltpu.make_async_copy(src_hbm.at[pl.ds(k * chunk, chunk), :], stage_ref.at[slot], sems.at[slot])

    copy(0).start()
    for k in range(nchunk):
        if k + 1 < nchunk:
            copy(k + 1).start()
        copy(k).wait()
        vals = stage_ref[k % 2]
        if col_scale is not None:
            vals = vals * col_scale
        dst_ref[k * chunk:(k + 1) * chunk, :] = vals.astype(dst_ref.dtype)


def _post_kernel(*refs, split, layer):
    nres = 2 if split else 1
    load_x = _residual_rows(refs[:nres])
    (att_ref, ret_ref, s5_ref, mod_ref, permt_ref, wglu_ref, bglu_ref, wo_hbm, g1_ref, b1_ref, w1_hbm, w2_hbm,
     g2_ref, b2_ref, o_ref, wo_ref, w1_ref, w2_ref, stage1_ref, stage2_ref, sem_ref) = refs[nres:]
    woa_ref = wo_ref.at[0:ATT_WIDTH]
    wor_ref = wo_ref.at[ATT_WIDTH:ATT_WIDTH + RET_WIDTH]
    wos_ref = wo_ref.at[ATT_WIDTH + RET_WIDTH:ATT_WIDTH + RET_WIDTH + S5_WIDTH]
    mdt = w1_ref.dtype

    @pl.when(pl.program_id(0) == 0)
    def _stage_weights():
        _load_cast(wo_hbm.at[layer], wo_ref, stage2_ref, sem_ref.at[1])
        _load_cast(w1_hbm.at[layer], w1_ref, stage1_ref, sem_ref.at[0])
        _load_cast(w2_hbm.at[layer], w2_ref, stage2_ref, sem_ref.at[1])
    sub = o_ref.shape[0] // POST_SPLIT
    csub = sub // S5_T
    nff = D_FF // FF_CHUNK

    def mix(part):
        rows = slice(sub * part, sub * (part + 1))
        zrows = []
        for i in range(S5_T):
            src_vreg, src_blk = divmod(i * S5_PAIR_W, LANES)
            src_blk //= S5_PAIR_W
            cols = [_lane_block_shuffle(
                lambda a: s5_ref[a, csub * part:csub * (part + 1), src_vreg * LANES:(src_vreg + 1) * LANES],
                src_blk, w) for w in range(S5_WIDTH // LANES)]
            zrows.append(jnp.concatenate(cols, axis=1))
        hs = jax.nn.gelu(jnp.concatenate(zrows, axis=0))
        gate = jax.nn.sigmoid(_mm(hs.astype(mdt), wglu_ref[...]) + bglu_ref[...])
        s5 = _mm(permt_ref[...], (hs * gate).astype(mdt)).astype(mdt)
        return _mm(att_ref[rows, :], woa_ref[...]) + _mm(ret_ref[rows, :], wor_ref[...]) + _mm(s5, wos_ref[...])

    def norm1(part, ox):
        rows = slice(sub * part, sub * (part + 1))
        x1 = _layer_norm(DEEPNORM_ALPHA * load_x(rows) + mod_ref[2:3, :] * ox, g1_ref[...], b1_ref[...])
        return x1, (x1 * (1.0 + mod_ref[4:5, :]) + mod_ref[3:4, :]).astype(mdt)

    def ff(h, c):
        a = _mm(h, w1_ref[:, FF_CHUNK * c:FF_CHUNK * (c + 1)])
        a = jnp.square(jnp.maximum(a, 0.0)).astype(mdt)
        return _mm(a, w2_ref[FF_CHUNK * c:FF_CHUNK * (c + 1), :])

    def norm2(part, x1, acc):
        rows = slice(sub * part, sub * (part + 1))
        o_ref[rows, :] = _layer_norm(DEEPNORM_ALPHA * x1 + mod_ref[5:6, :] * acc, g2_ref[...], b2_ref[...])

    nstage = nff + 3
    state = [dict() for _ in range(POST_SPLIT)]
    for part, stage in POST_PROGRAM:
        st = state[part]
        if stage == 0:
            st["ox"] = mix(part)
        elif stage == 1:
            st["x1"], st["h"] = norm1(part, st.pop("ox"))
        elif stage < nstage - 1:
            term = ff(st["h"], stage - 2)
            st["acc"] = term if stage == 2 else st["acc"] + term
        else:
            norm2(part, st["x1"], st["acc"])


def _residual_specs(residual):
    tm = ROW_TILE
    if len(residual) == 1:
        return [pl.BlockSpec((tm, D_MODEL), lambda i: (i, 0))]
    head_tiles = CTX_PAD // tm
    return [pl.BlockSpec((tm, D_MODEL), lambda i: (jnp.minimum(i, head_tiles - 1), 0)),
            pl.BlockSpec((tm, D_MODEL), lambda i: (jnp.maximum(i - head_tiles, 0), 0))]


def _post(residual, att, ret, s5_pairs, mods, layer, permt, wglu, bglu, wo, g1, b1, w1, w2, g2, b2, skip_context):
    rows = att.shape[0]
    tm = ROW_TILE
    off = CTX_PAD // tm if skip_context else 0
    assert not (skip_context and len(residual) > 1)
    row_blk = lambda width: pl.BlockSpec((tm, width), lambda i: (i + off, 0))
    full = lambda arr: pl.BlockSpec(arr.shape, lambda i: (0,) * arr.ndim)
    hbm = lambda arr: pl.BlockSpec(memory_space=pl.ANY)
    vec = lambda v: v.reshape(1, -1).astype(F32)
    small = [(permt, full), (wglu, full), (vec(bglu), full), (wo, hbm), (vec(g1), full), (vec(b1), full),
             (w1, hbm), (w2, hbm), (vec(g2), full), (vec(b2), full)]
    res_specs = [row_blk(D_MODEL)] if skip_context else _residual_specs(residual)
    return pl.pallas_call(
        functools.partial(_post_kernel, split=len(residual) > 1, layer=layer),
        grid=(rows // tm - off,),
        in_specs=res_specs + [row_blk(ATT_WIDTH), row_blk(RET_WIDTH),
                              pl.BlockSpec((S5_PAIRS, tm // S5_T, S5_T * S5_PAIR_W), lambda i: (0, i + off, 0)),
                              pl.BlockSpec((None, None, N_ADA, D_MODEL),
                                           lambda i: (layer, jnp.where(i + off == 0, 1, 0), 0, 0))]
                 + [spec(arr) for arr, spec in small],
        out_specs=pl.BlockSpec((tm, D_MODEL), lambda i: (i, 0)),
        out_shape=jax.ShapeDtypeStruct((rows - off * tm, D_MODEL), F32),
        scratch_shapes=[pltpu.VMEM((ATT_WIDTH + RET_WIDTH + S5_WIDTH, D_MODEL), MXU_DTYPE),
                        pltpu.VMEM((D_MODEL, D_FF), MXU_DTYPE), pltpu.VMEM((D_FF, D_MODEL), MXU_DTYPE),
                        pltpu.VMEM((2, STAGE_BYTES // (4 * D_FF), D_FF), F32),
                        pltpu.VMEM((2, STAGE_BYTES // (4 * D_MODEL), D_MODEL), F32),
                        pltpu.SemaphoreType.DMA((2, 2))],
        compiler_params=_params(("arbitrary",)),
        name="post",
    )(*residual, att, ret, s5_pairs, mods, *[arr for arr, _ in small])


def kernel(x, c, ctx, c_ctx, w_ada, b_ada, w_in, att_sink, ret_decay_logit, s5_lambda_re, s5_lambda_im, s5_b_re,
           s5_b_im, s5_c_re, s5_c_im, s5_log_dt, s5_d, w_glu, b_glu, w_out, ln1_g, ln1_b, w_ff1, w_ff2, ln2_g,
           ln2_b):
    assert x.shape[0] == 1 and x.shape[2] == D_MODEL and ctx.shape[1] == CTX_LEN
    seq = x.shape[1]
    assert seq % ROW_TILE == 0
    residual = (jnp.pad(ctx[0], ((0, CTX_PAD - CTX_LEN), (0, 0))), x[0])
    cond =jnp.zeros((8, D_MODEL), F32).at[0].set(c[0]).at[1].set(c_ctx)
    mods = _modulation(cond, w_ada, b_ada).reshape(DEPTH, 8, N_ADA, D_MODEL)
    tabs = _rope_tables(seq)
    masks = _attention_masks()
    perm = _chunk_perm(ROW_TILE // POST_SPLIT, MXU_DTYPE)
    permt = perm.T
    col_scale = jnp.ones((IN_WIDTH,), F32).at[COL_AQ:COL_AK].set(HEAD_DIM ** -0.5 * LOG2E)
    col_scale = col_scale.at[COL_RQ:COL_RK].set(HEAD_DIM ** -0.5)
    s5w = jax.vmap(_s5_weights)(s5_lambda_re, s5_lambda_im, s5_b_re, s5_b_im, s5_c_re, s5_c_im, s5_log_dt, s5_d)
    log_gamma = jax.nn.log_sigmoid(ret_decay_logit.astype(F32))
    for l in range(DEPTH):
        proj, u_pairs = _in_proj(residual, mods, l, w_in, col_scale.reshape(1, IN_WIDTH), tabs, perm)
        att = _attention(proj, att_sink[l].astype(F32) * LOG2E, masks)
        ret = _retention(proj, log_gamma[l])
        s5 = _s5_mixer(u_pairs, s5w, l)
        stream = _post(residual, att, ret, s5, mods, l, permt, w_glu[l].astype(MXU_DTYPE), b_glu[l],
                       w_out, ln1_g[l], ln1_b[l], w_ff1, w_ff2, ln2_g[l], ln2_b[l],
                       skip_context=(l == DEPTH - 1))
        residual = (stream,)
    return stream[None]
```

```python
import functools
import math

import jax
import jax.numpy as jnp
from jax import lax
from jax.experimental import pallas as pl
from jax.experimental.pallas import tpu as pltpu

F32 = jnp.float32
MXU_DTYPE = jnp.bfloat16

D_MODEL = 1024
DEPTH = 4
GRID_W = 64
CTX_LEN = 256
CTX_PAD = 512
HEAD_DIM = 64
ATT_HEADS = 8
ATT_KV_HEADS = 2
ATT_BLOCK = 128
ATT_LOOKAHEAD = 2
ROPE_BASE = 10000.0
RET_HEADS = 4
RET_CHUNK = 256
S5_CH = 16
S5_GROUPS = 16
S5_STATE = 64
S5_T = 16
S5_PAIRS = S5_GROUPS // 2
S5_PAIR_W = 2 * S5_CH
S5_TILE = CTX_PAD // S5_T
SCAN_ROWS = 8
S5_STATE_W = 2 * 2 * S5_STATE
ATT_WIDTH = ATT_HEADS * HEAD_DIM
KV_WIDTH = ATT_KV_HEADS * HEAD_DIM
RET_WIDTH = RET_HEADS * HEAD_DIM
S5_WIDTH = S5_GROUPS * S5_CH
IN_WIDTH = ATT_WIDTH + 2 * KV_WIDTH + 4 * RET_WIDTH + S5_WIDTH
D_FF = 4 * D_MODEL
FF_CHUNK = 1024
N_ADA = 6
MOD_ROWS = 2
LN_EPS = 1e-5
GN_EPS = 1e-5
DEEPNORM_ALPHA = (2 * DEPTH) ** 0.25
ROW_TILE = 512
POST_SPLIT = 2
_NFF = D_FF // FF_CHUNK
POST_PROGRAM = (((0, 0), (0, 1), (1, 0), (0, 2), (1, 1)) + tuple((0, 2 + c) for c in range(1, _NFF))
                + ((1, 2), (0, 2 + _NFF)) + tuple((1, 2 + c) for c in range(1, _NFF)) + ((1, 2 + _NFF),))
NEG_BIG = -1e30
LOG2E = math.log2(math.e)
LANES = 128
VMEM_LIMIT = 56 * 1024 * 1024
STAGE_BYTES = 2 * 1024 * 1024

COL_AQ, COL_AK, COL_AV = 0, ATT_WIDTH, ATT_WIDTH + KV_WIDTH
COL_RQ = ATT_WIDTH + 2 * KV_WIDTH
COL_RK, COL_RV, COL_RG = COL_RQ + RET_WIDTH, COL_RQ + 2 * RET_WIDTH, COL_RQ + 3 * RET_WIDTH
COL_S5 = COL_RQ + 4 * RET_WIDTH


def _mm(a, b):
    return jnp.dot(a, b, preferred_element_type=F32)


def _mm_nt(a, b):
    return lax.dot_general(a, b, (((1,), (1,)), ((), ())), preferred_element_type=F32)


def _mm_tn(a, b):
    return lax.dot_general(a, b, (((0,), (0,)), ((), ())), preferred_element_type=F32)


def _params(sem):
    return pltpu.CompilerParams(dimension_semantics=sem, vmem_limit_bytes=VMEM_LIMIT)


def _mod_kernel(cond_ref, w_ref, b_ref, o_ref):
    sub = 8
    tn = w_ref.shape[-1]

    def body(g, acc):
        r0 = pl.multiple_of(g * sub, sub)
        w8 = w_ref[0, pl.ds(r0, sub), :]
        c8 = cond_ref[pl.ds(r0, sub), :]
        s8 = c8 * jax.nn.sigmoid(c8)
        return tuple(acc[r] + s8[:, r:r + 1] * w8 for r in range(MOD_ROWS))

    acc = lax.fori_loop(0, w_ref.shape[1] // sub, body, tuple(jnp.zeros((sub, tn), F32) for _ in range(MOD_ROWS)))
    rows = [jnp.sum(a, axis=0, keepdims=True) + b_ref[0] for a in acc]
    o_ref[0] = jnp.concatenate(rows + [jnp.zeros((8 - MOD_ROWS, tn), F32)], axis=0)


def _modulation(cond_t, w_ada, b_ada):
    tn = 1536
    n = N_ADA * D_MODEL
    return pl.pallas_call(
        _mod_kernel,
        grid=(DEPTH, n // tn),
        in_specs=[
            pl.BlockSpec((D_MODEL, 8), lambda l, j: (0, 0)),
            pl.BlockSpec((1, D_MODEL, tn), lambda l, j: (l, 0, j)),
            pl.BlockSpec((1, 1, tn), lambda l, j: (l, 0, j)),
        ],
        out_specs=pl.BlockSpec((1, 8, tn), lambda l, j: (l, 0, j)),
        out_shape=jax.ShapeDtypeStruct((DEPTH, 8, n), F32),
        compiler_params=_params(("arbitrary", "arbitrary")),
        name="modulation",
    )(cond_t, w_ada, b_ada.reshape(DEPTH, 1, n))


def _lane_block_shuffle(src_rows, src_lane_blk, out_vreg):
    acc = None
    for q in range(LANES // S5_PAIR_W):
        piece = src_rows(out_vreg * (LANES // S5_PAIR_W) + q)
        shift = (S5_PAIR_W * (q - src_lane_blk)) % LANES
        if shift:
            piece = pltpu.roll(piece, shift, 1)
        if acc is None:
            acc = piece
        else:
            lane_blk = lax.broadcasted_iota(jnp.int32, piece.shape, 1) // S5_PAIR_W
            acc = jnp.where(lane_blk == q, piece, acc)
    return acc


def _in_proj_kernel(*refs, split, layer):
    nres = 2 if split else 1
    (mod_ref, w_hbm, scale_ref, ca_ref, sa_ref, cr_ref, sr_ref, perm_ref, o_ref, u_ref, w_ref, stage_ref,
     sem_ref) = refs[nres:]

    @pl.when(pl.program_id(0) == 0)
    def _stage_weights():
        _load_cast(w_hbm.at[layer], w_ref, stage_ref, sem_ref, col_scale=scale_ref[...])

    x = _residual_rows(refs[:nres])(slice(None))
    h = (x * (1.0 + mod_ref[1:2, :]) + mod_ref[0:1, :]).astype(w_ref.dtype)
    lane = lax.broadcasted_iota(jnp.int32, (x.shape[0], LANES), 1)
    first_att = (lane & 31) < 16
    first_ret = (lane & 63) < 32

    def proj(c0, c1):
        return _mm(h, w_ref[:, c0:c1])

    def rope_store(p, c0, width, cos, sin, first, half):
        for b in range(width // LANES):
            blk = p[:, LANES * b:LANES * (b + 1)]
            rot = jnp.where(first, pltpu.roll(blk, LANES - half, 1), pltpu.roll(blk, half, 1))
            o_ref[:, c0 + LANES * b:c0 + LANES * (b + 1)] = (blk * cos + rot * sin).astype(o_ref.dtype)

    def plain_store(p, c0, width):
        o_ref[:, c0:c0 + width] = p.astype(o_ref.dtype)

    u = proj(COL_S5, IN_WIDTH).astype(w_ref.dtype)
    sub = perm_ref.shape[0]
    nchunk = sub // S5_T
    for part in range(x.shape[0] // sub):
        g = _mm(perm_ref[...], u[sub * part:sub * (part + 1)])
        for a in range(S5_PAIRS):
            vreg_col, lane_blk = divmod(a * S5_PAIR_W, LANES)
            lane_blk //= S5_PAIR_W
            for v in range(S5_T * S5_PAIR_W // LANES):
                slab = _lane_block_shuffle(
                    lambda j: g[nchunk * j:nchunk * (j + 1), vreg_col * LANES:(vreg_col + 1) * LANES], lane_blk, v)
                u_ref[a, nchunk * part:nchunk * (part + 1), LANES * v:LANES * (v + 1)] = slab.astype(u_ref.dtype)

    ca, sa, cr, sr = ca_ref[...], sa_ref[...], cr_ref[...], sr_ref[...]
    att_rope = (ca, sa, first_att, HEAD_DIM // 4)
    ret_rope = (cr, sr, first_ret, HEAD_DIM // 2)
    groups = [(COL_AQ, ATT_WIDTH, att_rope), (COL_AK, KV_WIDTH, att_rope), (COL_RQ, RET_WIDTH, ret_rope),
              (COL_RK, RET_WIDTH, ret_rope), (COL_AV, KV_WIDTH, None), (COL_RV, COL_S5 - COL_RV, None)]
    pending = None
    for group in groups + [None]:
        nxt = None if group is None else (proj(group[0], group[0] + group[1]),) + group
        if pending is not None:
            p, pc0, pwidth, prope = pending
            if prope is None:
                plain_store(p, pc0, pwidth)
            else:
                rope_store(p, pc0, pwidth, *prope)
        pending = nxt


def _chunk_perm(tile_rows, dtype):
    nchunk = tile_rows // S5_T
    r = jnp.arange(tile_rows)
    src = S5_T * (r % nchunk) + r // nchunk
    return (src[:, None] == jnp.arange(tile_rows)[None, :]).astype(dtype)


def _in_proj(residual, mods, layer, w_in, col_scale, tabs, perm):
    rows = tabs[0].shape[0]
    tm = ROW_TILE
    tab_spec = pl.BlockSpec((tm, LANES), lambda i: (i, 0))
    nch = rows // S5_T
    return pl.pallas_call(
        functools.partial(_in_proj_kernel, split=len(residual) > 1, layer=layer),
        grid=(rows // tm,),
        in_specs=_residual_specs(residual) + [
            pl.BlockSpec((None, None, N_ADA, D_MODEL), lambda i: (layer, jnp.where(i == 0, 1, 0), 0, 0)),
            pl.BlockSpec(memory_space=pl.ANY),
            pl.BlockSpec((1, IN_WIDTH), lambda i: (0, 0)),
            tab_spec, tab_spec, tab_spec, tab_spec,
            pl.BlockSpec(perm.shape, lambda i: (0, 0)),
        ],
        out_specs=[pl.BlockSpec((tm, COL_S5), lambda i: (i, 0)),
                   pl.BlockSpec((S5_PAIRS, tm // S5_T, S5_T * S5_PAIR_W), lambda i: (0, i, 0))],
        out_shape=[jax.ShapeDtypeStruct((rows, COL_S5), MXU_DTYPE),
                   jax.ShapeDtypeStruct((S5_PAIRS, nch, S5_T * S5_PAIR_W), MXU_DTYPE)],
        scratch_shapes=[pltpu.VMEM((D_MODEL, IN_WIDTH), MXU_DTYPE),
                        pltpu.VMEM((2, STAGE_BYTES // (4 * IN_WIDTH), IN_WIDTH), F32),
                        pltpu.SemaphoreType.DMA((2,))],
        compiler_params=_params(("arbitrary",)),
        name="in_proj",
    )(*residual, mods, w_in, col_scale, *tabs, perm)


def _rope_tables(seq):
    half_a = HEAD_DIM // 4
    half_r = HEAD_DIM // 2
    nrow = seq // GRID_W
    inv_a = ROPE_BASE ** (-jnp.arange(half_a, dtype=F32) / half_a)
    inv_r = ROPE_BASE ** (-jnp.arange(half_r, dtype=F32) / half_r)
    ang_r = jnp.arange(nrow, dtype=F32)[:, None] * inv_a[None, :]
    ang_c = jnp.arange(GRID_W, dtype=F32)[:, None] * inv_a[None, :]
    ang_t = jnp.arange(seq, dtype=F32)[:, None] * inv_r[None, :]
    hp = lax.Precision.HIGHEST
    lane = jnp.arange(LANES)
    within = lane % HEAD_DIM
    pick_a = (within % half_a)[None, :] == jnp.arange(half_a)[:, None]
    exp_row = (pick_a & (within < 2 * half_a)[None, :]).astype(F32)
    exp_col = (pick_a & (within >= 2 * half_a)[None, :]).astype(F32)
    exp_t = ((lane % half_r)[None, :] == jnp.arange(half_r)[:, None]).astype(F32)
    sign_a = jnp.where(within % (2 * half_a) < half_a, -1.0, 1.0).astype(F32)
    sign_r = jnp.where(within < half_r, -1.0, 1.0).astype(F32)

    def att_table(fn):
        by_row = jnp.dot(fn(ang_r), exp_row, precision=hp)
        by_col = jnp.dot(fn(ang_c), exp_col, precision=hp)
        return (by_row[:, None, :] + by_col[None, :, :]).reshape(seq, LANES)

    cos_a = att_table(jnp.cos)
    sin_a = att_table(jnp.sin) * sign_a
    cos_r = jnp.dot(jnp.cos(ang_t), exp_t, precision=hp)
    sin_r = jnp.dot(jnp.sin(ang_t), exp_t, precision=hp) * sign_r
    pad = lambda tab, ident: jnp.pad(tab, ((CTX_PAD, 0), (0, 0)), constant_values=ident)
    return pad(cos_a, 1.0), pad(sin_a, 0.0), pad(cos_r, 1.0), pad(sin_r, 0.0)


def _swap_halves(x):
    if x.dtype.itemsize == 4:
        return pltpu.roll(x, 64, 1)
    packed = pltpu.bitcast(x, jnp.uint32)
    return pltpu.bitcast(pltpu.roll(packed, 64, 1), x.dtype)


def _dup_heads(x):
    sw = _swap_halves(x)
    lo = lax.broadcasted_iota(jnp.int32, x.shape, 1) < HEAD_DIM
    return jnp.where(lo, x, sw), jnp.where(lo, sw, x)


def _attn_kernel(sink_ref, q_ref, km_ref, kp_ref, kn_ref, vm_ref, vp_ref, vn_ref, kc_ref, vc_ref, mask_ref,
                 o_ref, k2_ref, v2_ref, kc2_ref, vc2_ref):
    i = pl.program_id(0)
    last_blk = pl.num_programs(0) * (ROW_TILE // ATT_BLOCK) - 1
    blk = ATT_BLOCK
    def spread(src, ones_upper):
        x = src[...]
        a, b = _dup_heads(x)
        if ones_upper:
            upper = lax.broadcasted_iota(jnp.int32, x.shape, 1) >= HEAD_DIM
            a = jnp.where(upper, jnp.ones_like(a), a)
            b = jnp.where(upper, jnp.ones_like(b), b)
        return a, b

    for dst, parts, is_v in ((k2_ref, (kp_ref, km_ref, kn_ref), False), (v2_ref, (vp_ref, vm_ref, vn_ref), True)):
        row = 0
        for part in parts:
            a, b = spread(part, is_v)
            n = part.shape[0]
            dst[0, row:row + n, :] = a
            dst[1, row:row + n, :] = b
            row += n
    for dst, src, is_v in ((kc2_ref, kc_ref, False), (vc2_ref, vc_ref, True)):
        a, b = spread(src, is_v)
        dst[0] = a
        dst[1] = b

    lo = lax.broadcasted_iota(jnp.int32, (blk, LANES), 1) < HEAD_DIM
    group = ATT_HEADS // ATT_KV_HEADS

    nloc = 3 * blk

    def scores(j, kv):
        r0 = j * blk
        qt = q_ref[r0:r0 + blk, group * HEAD_DIM * kv:group * HEAD_DIM * (kv + 1)]
        parts = []
        for g in range(group):
            qc = qt[:, LANES * (g // 2):LANES * (g // 2 + 1)]
            keep = lo if g % 2 == 0 else jnp.logical_not(lo)
            parts.append(jnp.where(keep, qc, jnp.zeros_like(qc)))
        qs = jnp.concatenate(parts, axis=0)
        return _mm_nt(qs, k2_ref[kv, r0:r0 + nloc, :]), _mm_nt(qs, kc2_ref[kv])

    def finish(j, kv, s_loc, s_ctx):
        r0 = j * blk
        gblk = i * (ROW_TILE // blk) + j
        sel = jnp.where(i == 0, 3, jnp.where(gblk == CTX_PAD // blk, 0, jnp.where(gblk == last_blk, 2, 1)))
        bias = mask_ref[sel]
        probs, sink_w = [], []
        for g in range(group):
            s = jnp.concatenate([s_loc[blk * g:blk * (g + 1)] + bias, s_ctx[blk * g:blk * (g + 1)]], axis=1)
            sk = sink_ref[group * kv + g]
            m = jnp.maximum(jnp.max(s, axis=-1, keepdims=True), sk)
            probs.append(jnp.exp2(s - m).astype(o_ref.dtype))
            sink_w.append(jnp.exp2(sk - m))
        p = jnp.concatenate(probs, axis=0)
        o = _mm(p[:, :nloc], v2_ref[kv, r0:r0 + nloc, :]) + _mm(p[:, nloc:], vc2_ref[kv])
        for half in range(group // 2):
            even, odd = 2 * half, 2 * half + 1
            oe = o[blk * even:blk * (even + 1)]
            oo = o[blk * odd:blk * (odd + 1)]
            y_even = oe * (1.0 / (pltpu.roll(oe, HEAD_DIM, 1) + sink_w[even]))
            y_odd = pltpu.roll(oo, HEAD_DIM, 1) * (1.0 / (oo + sink_w[odd]))
            c0 = group * HEAD_DIM * kv + LANES * half
            o_ref[r0:r0 + blk, c0:c0 + LANES] = jnp.where(lo, y_even, y_odd).astype(o_ref.dtype)

    items = [(j, kv) for j in range(ROW_TILE // blk) for kv in range(ATT_KV_HEADS)]
    pending = {}
    for t in range(len(items) + ATT_LOOKAHEAD):
        if t < len(items):
            pending[t] = scores(*items[t])
        if t >= ATT_LOOKAHEAD:
            finish(*items[t - ATT_LOOKAHEAD], *pending.pop(t - ATT_LOOKAHEAD))


def _attention_masks():
    qi = jnp.arange(ATT_BLOCK)[:, None]
    kj = jnp.arange(3 * ATT_BLOCK)[None, :]
    band = jnp.abs(kj - ATT_BLOCK - qi) <= ATT_BLOCK
    first = band & (kj >= ATT_BLOCK)
    last = band & (kj < 2 * ATT_BLOCK)
    none = jnp.zeros_like(band)
    masks = jnp.stack([first, band, last, none])
    return jnp.where(masks, 0.0, NEG_BIG).astype(F32)


def _attention(proj, sink, masks):
    rows = proj.shape[0]
    tm, blk = ROW_TILE, ATT_BLOCK
    per = tm // blk
    nblk = rows // blk
    ck, cv = COL_AK // KV_WIDTH, COL_AV // KV_WIDTH
    dt = proj.dtype
    return pl.pallas_call(
        _attn_kernel,
        grid=(rows // tm,),
        in_specs=[
            pl.BlockSpec(memory_space=pltpu.SMEM),
            pl.BlockSpec((tm, ATT_WIDTH), lambda i: (i, 0)),
            pl.BlockSpec((tm, KV_WIDTH), lambda i: (i, ck)),
            pl.BlockSpec((blk, KV_WIDTH), lambda i: (jnp.maximum(i * per - 1, 0), ck)),
            pl.BlockSpec((blk, KV_WIDTH), lambda i: (jnp.minimum((i + 1) * per, nblk - 1), ck)),
            pl.BlockSpec((tm, KV_WIDTH), lambda i: (i, cv)),
            pl.BlockSpec((blk, KV_WIDTH), lambda i: (jnp.maximum(i * per - 1, 0), cv)),
            pl.BlockSpec((blk, KV_WIDTH), lambda i: (jnp.minimum((i + 1) * per, nblk - 1), cv)),
            pl.BlockSpec((CTX_LEN, KV_WIDTH), lambda i: (0, ck)),
            pl.BlockSpec((CTX_LEN, KV_WIDTH), lambda i: (0, cv)),
            pl.BlockSpec((4, blk, 3 * blk), lambda i: (0, 0, 0)),
        ],
        out_specs=pl.BlockSpec((tm, ATT_WIDTH), lambda i: (i, 0)),
        out_shape=jax.ShapeDtypeStruct((rows, ATT_WIDTH), dt),
        scratch_shapes=[
            pltpu.VMEM((2, tm + 2 * blk, KV_WIDTH), dt),
            pltpu.VMEM((2, tm + 2 * blk, KV_WIDTH), dt),
            pltpu.VMEM((2, CTX_LEN, KV_WIDTH), dt),
            pltpu.VMEM((2, CTX_LEN, KV_WIDTH), dt),
        ],
        compiler_params=_params(("arbitrary",)),
        name="attention",
    )(sink, proj, proj, proj, proj, proj, proj, proj, proj, proj, masks)


def _ret_kernel(*refs, sup, ntile):
    lg_ref, q_ref = refs[:2]
    k_refs, v_refs = refs[2:2 + sup], refs[2 + sup:2 + 2 * sup]
    g_ref, o_ref, sb_ref, s_ref, dm_ref, tab_ref, gbd_ref = refs[2 + 2 * sup:]
    k_ref, v_ref = k_refs[0], v_refs[0]
    step = pl.program_id(0)
    nsup = (ntile - 1) // sup
    ph = jnp.where(step <= nsup, 0, 1)
    t = jnp.where(step <= nsup, step, step - (nsup + 1))
    c = RET_CHUNK
    w = RET_WIDTH
    per = q_ref.shape[0] // c
    mdt = sb_ref.dtype
    rows = lambda ci: slice(c * ci, c * (ci + 1))

    def lane_vec(direction, shape, axis):
        head = lax.broadcasted_iota(jnp.int32, shape, axis) // HEAD_DIM
        out = jnp.full(shape, lg_ref[direction, RET_HEADS - 1], F32)
        for h in range(RET_HEADS - 2, -1, -1):
            out = jnp.where(head == h, lg_ref[direction, h], out)
        return out

    @pl.when(jnp.logical_and(ph == 0, t == 0))
    def _init_tables():
        diff = (lax.broadcasted_iota(jnp.int32, (c, c), 0) - lax.broadcasted_iota(jnp.int32, (c, c), 1)).astype(F32)
        for h in range(RET_HEADS):
            dm_ref[h] = jnp.exp(jnp.where(diff >= 0, diff * lg_ref[0, h], -diff * lg_ref[1, h]))
        pos = lax.broadcasted_iota(jnp.int32, (c, w), 0).astype(F32)
        lgf = lane_vec(0, (c, w), 1)
        lgb = lane_vec(1, (c, w), 1)
        tab_ref[0] = jnp.exp((c - 1.0 - pos) * lgf)
        tab_ref[1] = jnp.exp((pos + 1.0) * lgf)
        tab_ref[2] = jnp.exp(pos * lgb)
        tab_ref[3] = jnp.exp((c - pos) * lgb)
        same = (lax.broadcasted_iota(jnp.int32, (w, w), 0) // HEAD_DIM
                == lax.broadcasted_iota(jnp.int32, (w, w), 1) // HEAD_DIM)
        bd = jnp.where(same, 1.0, 0.0)
        gbd_ref[0] = bd * jnp.exp(c * lane_vec(0, (w, w), 0))
        gbd_ref[1] = bd * jnp.exp(c * lane_vec(1, (w, w), 0))
        gbd_ref[2] = bd

    @pl.when(t == 0)
    def _reset_state():
        s_ref[...] = jnp.zeros_like(s_ref)

    def state_update(direction, key_tab, ci, kr=k_ref, vr=v_ref):
        kw = (kr[rows(ci), :].astype(F32) * tab_ref[key_tab]).astype(mdt)
        u = _mm_tn(kw, vr[rows(ci), :])
        s_ref[...] = gbd_ref[direction] * s_ref[...] + gbd_ref[2] * u

    @pl.when(jnp.logical_and(ph == 0, t == 0))
    def _backward_context():
        sb_ref[0] = s_ref[...].astype(mdt)
        state_update(1, 2, 0)

    @pl.when(jnp.logical_and(ph == 0, t > 0))
    def _backward_latent():
        base = 1 + per * sup * (nsup - t)
        for qi in range(sup - 1, -1, -1):
            for ci in range(per - 1, -1, -1):
                sb_ref[base + per * qi + ci] = s_ref[...].astype(mdt)
                state_update(1, 2, ci, k_refs[qi], v_refs[qi])

    head = lax.broadcasted_iota(jnp.int32, (c, w), 1) // HEAD_DIM

    def scores(ci):
        q = q_ref[rows(ci), :]
        qs = jnp.concatenate([jnp.where(head == h, q, jnp.zeros_like(q)) for h in range(RET_HEADS)], axis=0)
        return _mm_nt(qs, k_ref[rows(ci), :])

    def intra(ci, sc):
        scd = jnp.concatenate([sc[c * h:c * (h + 1)] * dm_ref[h] for h in range(RET_HEADS)], axis=0).astype(mdt)
        oi = _mm(scd, v_ref[rows(ci), :])
        o = jnp.where(head == 0, oi[0:c], 0.0)
        for h in range(1, RET_HEADS):
            o = o + jnp.where(head == h, oi[c * h:c * (h + 1)], 0.0)
        return o

    def cross(ci, idx):
        qf = q_ref[rows(ci), :].astype(F32)
        return (_mm((qf * tab_ref[1]).astype(mdt), s_ref[...].astype(mdt))
                + _mm((qf * tab_ref[3]).astype(mdt), sb_ref[idx]))

    def finish(ci, o):
        avg = (gbd_ref[2] * (1.0 / HEAD_DIM)).astype(mdt)
        o_hi = o.astype(mdt)
        d = o - (_mm(o_hi, avg) + _mm((o - o_hi.astype(F32)).astype(mdt), avg))
        var = _mm((d * d).astype(mdt), avg)
        gate = g_ref[rows(ci), :].astype(F32)
        y = d * lax.rsqrt(var + GN_EPS) * (gate * jax.nn.sigmoid(gate))
        o_ref[rows(ci), :] = y.astype(o_ref.dtype)

    @pl.when(jnp.logical_and(ph == 1, t == 0))
    def _forward_context():
        o = intra(0, scores(0)) + cross(0, 0)
        state_update(0, 0, 0)
        finish(0, o)
        for ci in range(1, per):
            o_ref[rows(ci), :] = jnp.zeros((c, w), o_ref.dtype)

    @pl.when(jnp.logical_and(ph == 1, t > 0))
    def _forward_latent():
        base = 1 + per * (t - 1)
        sc = [scores(ci) for ci in range(per)]
        outs = [intra(ci, sc[ci]) for ci in range(per)]
        for ci in range(per):
            outs[ci] = outs[ci] + cross(ci, base + ci)
            state_update(0, 0, ci)
        for ci in range(per):
            finish(ci, outs[ci])


def _retention(proj, log_gamma):
    rows = proj.shape[0]
    c = RET_CHUNK
    tm = ROW_TILE
    ntile = rows // tm
    nchunk = 1 + (rows - CTX_PAD) // c
    dt = proj.dtype
    cq, ckk, cvv, cg = (COL_RQ // RET_WIDTH, COL_RK // RET_WIDTH, COL_RV // RET_WIDTH, COL_RG // RET_WIDTH)

    nlat = ntile - 1
    sup = next(s for s in (4, 2, 1) if nlat % s == 0)
    nsup = nlat // sup
    nback = 1 + nsup

    def fw_blk(step):
        return jnp.maximum(step - nback, 0)

    def kv_spec(col, qi):
        def index(step):
            back = jnp.where(step == 0, 0, 1 + sup * (nsup - jnp.minimum(step, nsup)) + qi)
            fwd = step - nback if qi == 0 else 1 + qi
            return jnp.where(step < nback, back, fwd), col
        return pl.BlockSpec((tm, RET_WIDTH), index)

    return pl.pallas_call(
        functools.partial(_ret_kernel, sup=sup, ntile=ntile),
        grid=(nback + ntile,),
        in_specs=[pl.BlockSpec(memory_space=pltpu.SMEM),
                  pl.BlockSpec((tm, RET_WIDTH), lambda step: (fw_blk(step), cq))]
                 + [kv_spec(ckk, qi) for qi in range(sup)] + [kv_spec(cvv, qi) for qi in range(sup)]
                 + [pl.BlockSpec((tm, RET_WIDTH), lambda step: (fw_blk(step), cg))],
        out_specs=pl.BlockSpec((tm, RET_WIDTH), lambda step: (fw_blk(step), 0)),
        out_shape=jax.ShapeDtypeStruct((rows, RET_WIDTH), dt),
        scratch_shapes=[
            pltpu.VMEM((nchunk, RET_WIDTH, RET_WIDTH), dt),
            pltpu.VMEM((RET_WIDTH, RET_WIDTH), F32),
            pltpu.VMEM((RET_HEADS, c, c), F32),
            pltpu.VMEM((4, c, RET_WIDTH), F32),
            pltpu.VMEM((3, RET_WIDTH, RET_WIDTH), F32),
        ],
        compiler_params=_params(("arbitrary",)),
        name="retention",
    )(log_gamma, proj, *([proj] * (2 * sup)), proj)


def _s5_weights(lam_re, lam_im, b_re, b_im, c_re, c_im, log_dt, d_skip):
    hp = lax.Precision.HIGHEST
    tt, g, n, p, a = S5_T, S5_GROUPS, S5_STATE, S5_CH, S5_PAIRS
    lam = lax.complex(lam_re.astype(F32), lam_im.astype(F32))
    dtv = jnp.exp(log_dt.astype(F32))[..., None]
    lam_bar = jnp.exp(lam * dtv)
    bbar = ((lam_bar - 1.0) / lam)[..., None] * lax.complex(b_re.astype(F32), b_im.astype(F32))
    cmat = lax.complex(c_re.astype(F32), c_im.astype(F32))
    pw = [jnp.ones_like(lam_bar)]
    for _ in range(tt):
        pw.append(pw[-1] * lam_bar)
    pw = jnp.stack(pw, axis=1)
    eye2 = jnp.eye(2, dtype=F32)
    ri = lambda z, axis: jnp.stack([jnp.real(z), jnp.imag(z)], axis=axis)

    pw_l = pw.reshape(2, tt + 1, a, 2 * n)
    bbt = jnp.einsum('dahpn,gh->dagphn', jnp.swapaxes(bbar, -1, -2).reshape(2, a, 2, p, n), eye2)
    bbt = bbt.reshape(2, a, 2 * p, 2 * n)
    cct = jnp.einsum('dahpn,gh->dagphn', cmat.reshape(2, a, 2, p, n), eye2).reshape(2, a, 2 * p, 2 * n)
    pw_k = ri(pw_l, 1).transpose(3, 0, 1, 2, 4)
    b_k = ri(bbt, 2).transpose(1, 0, 2, 3, 4)
    c_k = ri(cct, 2).transpose(1, 0, 2, 3, 4)
    decay = [pw_l[:, tt]]
    for _ in range(SCAN_ROWS - 1):
        decay.append(decay[-1] * decay[0])
    decay = jnp.stack(decay, axis=0)
    rows8 = lambda z: jnp.broadcast_to(z[None], (SCAN_ROWS,) + z.shape)
    carry_w = jnp.stack([decay[:, 0], decay[::-1, 1]], axis=1)
    scan_tab = jnp.stack([rows8(decay[0]), rows8(decay[1]), rows8(decay[3]), carry_w], axis=0)
    scan_tab = ri(scan_tab, 0).transpose(4, 3, 1, 0, 2, 5)
    skip = jnp.tile(d_skip.astype(F32).reshape(a, 1, 2 * p), (1, tt, 1)).reshape(a, 1, tt * 2 * p)
    return pw_k, b_k, c_k, scan_tab, skip


def _pair_spec(layer, *shape):
    return pl.BlockSpec((None, None) + shape, lambda i: (layer, i) + (0,) * len(shape))


def _s5_drive_kernel(u_ref, pw_ref, b_ref, o_ref, w_ref):
    rows, half = S5_PAIR_W, LANES
    for d in range(2):
        br, bi = b_ref[d, 0], b_ref[d, 1]
        for j in range(S5_T):
            e = S5_T - 1 - j if d == 0 else j
            pr, pi = pw_ref[d, 0, e:e + 1, :], pw_ref[d, 1, e:e + 1, :]
            w_ref[rows * j:rows * (j + 1), 2 * half * d:2 * half * d + half] = (pr * br - pi * bi).astype(w_ref.dtype)
            w_ref[rows * j:rows * (j + 1), 2 * half * d + half:2 * half * (d + 1)] = (
                pr * bi + pi * br).astype(w_ref.dtype)
    o_ref[0] = _mm(u_ref[0], w_ref[...])


def _s5_drive(u_pairs, pw_k, b_k, layer):
    a, nch, wd = u_pairs.shape
    return pl.pallas_call(
        _s5_drive_kernel,
        grid=(a,),
        in_specs=[pl.BlockSpec((1, nch, wd), lambda i: (i, 0, 0)),
                  _pair_spec(layer, *pw_k.shape[2:]), _pair_spec(layer, *b_k.shape[2:])],
        out_specs=pl.BlockSpec((1, nch, 2 * S5_STATE_W), lambda i: (i, 0, 0)),
        out_shape=jax.ShapeDtypeStruct((a, nch, 2 * S5_STATE_W), F32),
        scratch_shapes=[pltpu.VMEM((wd, 2 * S5_STATE_W), u_pairs.dtype)],
        compiler_params=_params(("arbitrary",)),
        name="s5_drive",
    )(u_pairs, pw_k, b_k)


def _s5_scan_kernel(ef_ref, eb_ref, tab_ref, sf_ref, sb_ref, st_ref):
    t = pl.program_id(0)
    hw = LANES
    sub = SCAN_ROWS
    npair = tab_ref.shape[0]
    row = lax.broadcasted_iota(jnp.int32, (sub, hw), 0)

    @pl.when(t == 0)
    def _reset():
        st_ref[...] = jnp.zeros_like(st_ref)

    def shift(x, k, reverse):
        if reverse:
            return jnp.where(row < sub - k, pltpu.roll(x, sub - k, 0), 0.0)
        return jnp.where(row >= k, pltpu.roll(x, k, 0), 0.0)

    def scan_group(e_ref, o_ref, a, d, r0, cr, ci):
        reverse = d == 1
        xr = e_ref[a, pl.ds(r0, sub), 0:hw]
        xi = e_ref[a, pl.ds(r0, sub), hw:2 * hw]
        for step, k in enumerate((1, 2, 4)):
            ar, ai = tab_ref[a, d, step, 0], tab_ref[a, d, step, 1]
            sr, si = shift(xr, k, reverse), shift(xi, k, reverse)
            xr, xi = xr + ar * sr - ai * si, xi + ar * si + ai * sr
        wr, wi = tab_ref[a, d, 3, 0], tab_ref[a, d, 3, 1]
        fr = xr + wr * cr - wi * ci
        fi = xi + wr * ci + wi * cr
        edge = sub - 1 if reverse else 0
        o_ref[a, pl.ds(r0, sub), 0:hw] = jnp.where(row == edge, cr, shift(fr, 1, reverse))
        o_ref[a, pl.ds(r0, sub), hw:2 * hw] = jnp.where(row == edge, ci, shift(fi, 1, reverse))
        last = 0 if reverse else sub - 1
        return (jnp.broadcast_to(fr[last:last + 1], (sub, hw)), jnp.broadcast_to(fi[last:last + 1], (sub, hw)))

    def run(nrows):
        ngroups = nrows // sub

        def body(gi, carry):
            r0 = pl.multiple_of(gi * sub, sub)
            rb0 = pl.multiple_of((ngroups - 1 - gi) * sub, sub)
            new = []
            for a in range(npair):
                fr, fi, br, bi = carry[a]
                fr, fi = scan_group(ef_ref, sf_ref, a, 0, r0, fr, fi)
                br, bi = scan_group(eb_ref, sb_ref, a, 1, rb0, br, bi)
                new.append((fr, fi, br, bi))
            return tuple(new)

        init = tuple(tuple(st_ref[a, k] for k in range(4)) for a in range(npair))
        final = lax.fori_loop(0, ngroups, body, init)
        for a in range(npair):
            for k in range(4):
                st_ref[a, k] = final[a][k]

    @pl.when(t == 0)
    def _context():
        sf_ref[...] = jnp.zeros_like(sf_ref)
        sb_ref[...] = jnp.zeros_like(sb_ref)
        run(CTX_LEN // S5_T)

    @pl.when(t > 0)
    def _latent():
        run(S5_TILE)


def _s5_scan(drive, scan_tab, layer):
    a, nch, wd2 = drive.shape
    wd = wd2 // 2
    nt = nch // S5_TILE

    def bwd(t):
        return jnp.where(t == 0, 0, nt - t)

    return pl.pallas_call(
        _s5_scan_kernel,
        grid=(nt,),
        in_specs=[pl.BlockSpec((a, S5_TILE, wd), lambda t: (0, t, 0)),
                  pl.BlockSpec((a, S5_TILE, wd), lambda t: (0, bwd(t), 1)),
                  pl.BlockSpec((None,) + scan_tab.shape[1:], lambda t: (layer,) + (0,) * (scan_tab.ndim - 1))],
        out_specs=[pl.BlockSpec((a, S5_TILE, wd), lambda t: (0, t, 0)),
                   pl.BlockSpec((a, S5_TILE, wd), lambda t: (0, bwd(t), 0))],
        out_shape=[jax.ShapeDtypeStruct((a, nch, wd), F32), jax.ShapeDtypeStruct((a, nch, wd), F32)],
        scratch_shapes=[pltpu.VMEM((a, 4, SCAN_ROWS, LANES), F32)],
        compiler_params=_params(("arbitrary",)),
        name="s5_scan",
    )(drive, drive, scan_tab)


def _lane_window(x, start, width):
    cols = []
    for v in range(width // LANES):
        k0, off = divmod(start + LANES * v, LANES)
        lo = x[:, LANES * k0:LANES * (k0 + 1)]
        if off:
            hi = x[:, LANES * (k0 + 1):LANES * (k0 + 2)]
            lane = lax.broadcasted_iota(jnp.int32, lo.shape, 1)
            lo = jnp.where(lane < LANES - off, pltpu.roll(lo, LANES - off, 1), pltpu.roll(hi, LANES - off, 1))
        cols.append(lo)
    return jnp.concatenate(cols, axis=1)


def _s5_read_kernel(u_ref, sf_ref, sb_ref, pw_ref, b_ref, c_ref, skip_ref, o_ref, wt_ref, wi_ref, lag_ref):
    u = u_ref[0]
    mdt = u.dtype
    rows, half = S5_PAIR_W, LANES
    for d in range(2):
        cr, ci = c_ref[d, 0], c_ref[d, 1]
        for i in range(S5_T):
            e = i + 1 if d == 0 else S5_T - i
            pr, pi = pw_ref[d, 0, e:e + 1, :], pw_ref[d, 1, e:e + 1, :]
            wt_ref[d, rows * i:rows * (i + 1), 0:half] = (pr * cr - pi * ci).astype(mdt)
            wt_ref[d, rows * i:rows * (i + 1), half:2 * half] = (-(pr * ci + pi * cr)).astype(mdt)
    nlag = 2 * S5_T - 1
    ldt = lag_ref.dtype
    zero = jnp.zeros((rows, half), ldt)
    for l in range(nlag + 1):
        lag = l - (S5_T - 1)
        for d, active in ((1, lag <= 0), (0, 0 <= lag < S5_T)):
            col = 2 * half * (1 - d)
            if active:
                cr, ci = c_ref[d, 0], c_ref[d, 1]
                pr, pi = pw_ref[d, 0, abs(lag):abs(lag) + 1, :], pw_ref[d, 1, abs(lag):abs(lag) + 1, :]
                lag_ref[rows * l:rows * (l + 1), col:col + half] = (pr * cr - pi * ci).astype(ldt)
                lag_ref[rows * l:rows * (l + 1), col + half:col + 2 * half] = (pr * ci + pi * cr).astype(ldt)
            else:
                lag_ref[rows * l:rows * (l + 1), col:col + half] = zero
                lag_ref[rows * l:rows * (l + 1), col + half:col + 2 * half] = zero
    lhs = jnp.concatenate([b_ref[1, 0], -b_ref[1, 1], b_ref[0, 0], -b_ref[0, 1]], axis=1).astype(ldt)
    kall = _mm_nt(lhs, lag_ref[...])
    for j in range(S5_T):
        wi_ref[rows * j:rows * (j + 1), :] = _lane_window(kall, rows * (S5_T - 1 - j), S5_T * rows).astype(mdt)
    y = _mm(u, wi_ref[...])
    y = y + _mm_nt(sf_ref[0].astype(mdt), wt_ref[0])
    y = y + _mm_nt(sb_ref[0].astype(mdt), wt_ref[1])
    o_ref[0] = y + u.astype(F32) * skip_ref[...]


def _s5_read(u_pairs, sf, sb, pw_k, b_k, c_k, skip, layer):
    a, nch, wd = u_pairs.shape
    blk = lambda *shape: pl.BlockSpec((1,) + shape, lambda i: (i, 0, 0))
    return pl.pallas_call(
        _s5_read_kernel,
        grid=(a,),
        in_specs=[blk(nch, wd), blk(nch, S5_STATE_W), blk(nch, S5_STATE_W), _pair_spec(layer, *pw_k.shape[2:]),
                  _pair_spec(layer, *b_k.shape[2:]), _pair_spec(layer, *c_k.shape[2:]),
                  _pair_spec(layer, *skip.shape[2:])],
        out_specs=blk(nch, wd),
        out_shape=jax.ShapeDtypeStruct((a, nch, wd), F32),
        scratch_shapes=[pltpu.VMEM((2, wd, S5_STATE_W), u_pairs.dtype), pltpu.VMEM((wd, wd), u_pairs.dtype),
                        pltpu.VMEM((2 * wd, 2 * S5_STATE_W), u_pairs.dtype)],
        compiler_params=_params(("arbitrary",)),
        name="s5_read",
    )(u_pairs, sf, sb, pw_k, b_k, c_k, skip)


def _s5_mixer(u_pairs, weights, layer):
    pw_k, b_k, c_k, scan_tab, skip = weights
    drive = _s5_drive(u_pairs, pw_k, b_k, layer)
    sf, sb = _s5_scan(drive, scan_tab, layer)
    return _s5_read(u_pairs, sf, sb, pw_k, b_k, c_k, skip, layer)


def _layer_norm(x, g, b):
    mu = jnp.mean(x, axis=-1, keepdims=True)
    d = x - mu
    var = jnp.mean(d * d, axis=-1, keepdims=True)
    return d * lax.rsqrt(var + LN_EPS) * g + b


def _residual_rows(refs):
    if len(refs) == 1:
        return lambda rows: refs[0][rows, :]
    head_ref, body_ref = refs
    is_head = pl.program_id(0) == 0
    return lambda rows: jnp.where(is_head, head_ref[rows, :], body_ref[rows, :])


def _load_cast(src_hbm, dst_ref, stage_ref, sems, col_scale=None):
    chunk = stage_ref.shape[1]
    nchunk = src_hbm.shape[0] // chunk

    def copy(k):
        slot = k % 2
        return pltpu.make_async_copy(src_hbm.at[pl.ds(k * chunk, chunk), :], stage_ref.at[slot], sems.at[slot])

    copy(0).start()
    for k in range(nchunk):
        if k + 1 < nchunk:
            copy(k + 1).start()
        copy(k).wait()
        vals = stage_ref[k % 2]
        if col_scale is not None:
            vals = vals * col_scale
        dst_ref[k * chunk:(k + 1) * chunk, :] = vals.astype(dst_ref.dtype)


def _post_kernel(*refs, split, layer):
    nres = 2 if split else 1
    load_x = _residual_rows(refs[:nres])
    (att_ref, ret_ref, s5_ref, mod_ref, permt_ref, wglu_ref, bglu_ref, wo_hbm, g1_ref, b1_ref, w1_hbm, w2_hbm,
     g2_ref, b2_ref, o_ref, wo_ref, w1_ref, w2_ref, stage1_ref, stage2_ref, sem_ref) = refs[nres:]
    woa_ref = wo_ref.at[0:ATT_WIDTH]
    wor_ref = wo_ref.at[ATT_WIDTH:ATT_WIDTH + RET_WIDTH]
    wos_ref = wo_ref.at[ATT_WIDTH + RET_WIDTH:ATT_WIDTH + RET_WIDTH + S5_WIDTH]
    mdt = w1_ref.dtype

    @pl.when(pl.program_id(0) == 0)
    def _stage_weights():
        _load_cast(wo_hbm.at[layer], wo_ref, stage2_ref, sem_ref.at[1])
        _load_cast(w1_hbm.at[layer], w1_ref, stage1_ref, sem_ref.at[0])
        _load_cast(w2_hbm.at[layer], w2_ref, stage2_ref, sem_ref.at[1])
    sub = o_ref.shape[0] // POST_SPLIT
    csub = sub // S5_T
    nff = D_FF // FF_CHUNK

    def mix(part):
        rows = slice(sub * part, sub * (part + 1))
        zrows = []
        for i in range(S5_T):
            src_vreg, src_blk = divmod(i * S5_PAIR_W, LANES)
            src_blk //= S5_PAIR_W
            cols = [_lane_block_shuffle(
                lambda a: s5_ref[a, csub * part:csub * (part + 1), src_vreg * LANES:(src_vreg + 1) * LANES],
                src_blk, w) for w in range(S5_WIDTH // LANES)]
            zrows.append(jnp.concatenate(cols, axis=1))
        hs = jax.nn.gelu(jnp.concatenate(zrows, axis=0))
        gate = jax.nn.sigmoid(_mm(hs.astype(mdt), wglu_ref[...]) + bglu_ref[...])
        s5 = _mm(permt_ref[...], (hs * gate).astype(mdt)).astype(mdt)
        return _mm(att_ref[rows, :], woa_ref[...]) + _mm(ret_ref[rows, :], wor_ref[...]) + _mm(s5, wos_ref[...])

    def norm1(part, ox):
        rows = slice(sub * part, sub * (part + 1))
        x1 = _layer_norm(DEEPNORM_ALPHA * load_x(rows) + mod_ref[2:3, :] * ox, g1_ref[...], b1_ref[...])
        return x1, (x1 * (1.0 + mod_ref[4:5, :]) + mod_ref[3:4, :]).astype(mdt)

    def ff(h, c):
        a = _mm(h, w1_ref[:, FF_CHUNK * c:FF_CHUNK * (c + 1)])
        a = jnp.square(jnp.maximum(a, 0.0)).astype(mdt)
        return _mm(a, w2_ref[FF_CHUNK * c:FF_CHUNK * (c + 1), :])

    def norm2(part, x1, acc):
        rows = slice(sub * part, sub * (part + 1))
        o_ref[rows, :] = _layer_norm(DEEPNORM_ALPHA * x1 + mod_ref[5:6, :] * acc, g2_ref[...], b2_ref[...])

    nstage = nff + 3
    state = [dict() for _ in range(POST_SPLIT)]
    for part, stage in POST_PROGRAM:
        st = state[part]
        if stage == 0:
            st["ox"] = mix(part)
        elif stage == 1:
            st["x1"], st["h"] = norm1(part, st.pop("ox"))
        elif stage < nstage - 1:
            term = ff(st["h"], stage - 2)
            st["acc"] = term if stage == 2 else st["acc"] + term
        else:
            norm2(part, st["x1"], st["acc"])


def _residual_specs(residual):
    tm = ROW_TILE
    if len(residual) == 1:
        return [pl.BlockSpec((tm, D_MODEL), lambda i: (i, 0))]
    head_tiles = CTX_PAD // tm
    return [pl.BlockSpec((tm, D_MODEL), lambda i: (jnp.minimum(i, head_tiles - 1), 0)),
            pl.BlockSpec((tm, D_MODEL), lambda i: (jnp.maximum(i - head_tiles, 0), 0))]


def _post(residual, att, ret, s5_pairs, mods, layer, permt, wglu, bglu, wo, g1, b1, w1, w2, g2, b2, skip_context):
    rows = att.shape[0]
    tm = ROW_TILE
    off = CTX_PAD // tm if skip_context else 0
    assert not (skip_context and len(residual) > 1)
    row_blk = lambda width: pl.BlockSpec((tm, width), lambda i: (i + off, 0))
    full = lambda arr: pl.BlockSpec(arr.shape, lambda i: (0,) * arr.ndim)
    hbm = lambda arr: pl.BlockSpec(memory_space=pl.ANY)
    vec = lambda v: v.reshape(1, -1).astype(F32)
    small = [(permt, full), (wglu, full), (vec(bglu), full), (wo, hbm), (vec(g1), full), (vec(b1), full),
             (w1, hbm), (w2, hbm), (vec(g2), full), (vec(b2), full)]
    res_specs = [row_blk(D_MODEL)] if skip_context else _residual_specs(residual)
    return pl.pallas_call(
        functools.partial(_post_kernel, split=len(residual) > 1, layer=layer),
        grid=(rows // tm - off,),
        in_specs=res_specs + [row_blk(ATT_WIDTH), row_blk(RET_WIDTH),
                              pl.BlockSpec((S5_PAIRS, tm // S5_T, S5_T * S5_PAIR_W), lambda i: (0, i + off, 0)),
                              pl.BlockSpec((None, None, N_ADA, D_MODEL),
                                           lambda i: (layer, jnp.where(i + off == 0, 1, 0), 0, 0))]
                 + [spec(arr) for arr, spec in small],
        out_specs=pl.BlockSpec((tm, D_MODEL), lambda i: (i, 0)),
        out_shape=jax.ShapeDtypeStruct((rows - off * tm, D_MODEL), F32),
        scratch_shapes=[pltpu.VMEM((ATT_WIDTH + RET_WIDTH + S5_WIDTH, D_MODEL), MXU_DTYPE),
                        pltpu.VMEM((D_MODEL, D_FF), MXU_DTYPE), pltpu.VMEM((D_FF, D_MODEL), MXU_DTYPE),
                        pltpu.VMEM((2, STAGE_BYTES // (4 * D_FF), D_FF), F32),
                        pltpu.VMEM((2, STAGE_BYTES // (4 * D_MODEL), D_MODEL), F32),
                        pltpu.SemaphoreType.DMA((2, 2))],
        compiler_params=_params(("arbitrary",)),
        name="post",
    )(*residual, att, ret, s5_pairs, mods, *[arr for arr, _ in small])


def kernel(x, c, ctx, c_ctx, w_ada, b_ada, w_in, att_sink, ret_decay_logit, s5_lambda_re, s5_lambda_im, s5_b_re,
           s5_b_im, s5_c_re, s5_c_im, s5_log_dt, s5_d, w_glu, b_glu, w_out, ln1_g, ln1_b, w_ff1, w_ff2, ln2_g,
           ln2_b):
    assert x.shape[0] == 1 and x.shape[2] == D_MODEL and ctx.shape[1] == CTX_LEN
    seq = x.shape[1]
    assert seq % ROW_TILE == 0
    residual = (jnp.pad(ctx[0], ((0, CTX_PAD - CTX_LEN), (0, 0))), x[0])
    cond_t = jnp.zeros((D_MODEL, 8), F32).at[:, 0].set(c[0]).at[:, 1].set(c_ctx)
    mods = _modulation(cond_t, w_ada, b_ada).reshape(DEPTH, 8, N_ADA, D_MODEL)
    tabs = _rope_tables(seq)
    masks = _attention_masks()
    perm = _chunk_perm(ROW_TILE // POST_SPLIT, MXU_DTYPE)
    permt = perm.T
    col_scale = jnp.ones((IN_WIDTH,), F32).at[COL_AQ:COL_AK].set(HEAD_DIM ** -0.5 * LOG2E)
    col_scale = col_scale.at[COL_RQ:COL_RK].set(HEAD_DIM ** -0.5)
    s5w = jax.vmap(_s5_weights)(s5_lambda_re, s5_lambda_im, s5_b_re, s5_b_im, s5_c_re, s5_c_im, s5_log_dt, s5_d)
    log_gamma = jax.nn.log_sigmoid(ret_decay_logit.astype(F32))
    for l in range(DEPTH):
        proj, u_pairs = _in_proj(residual, mods, l, w_in, col_scale.reshape(1, IN_WIDTH), tabs, perm)
        att = _attention(proj, att_sink[l].astype(F32) * LOG2E, masks)
        ret = _retention(proj, log_gamma[l])
        s5 = _s5_mixer(u_pairs, s5w, l)
        stream = _post(residual, att, ret, s5, mods, l, permt, w_glu[l].astype(MXU_DTYPE), b_glu[l],
                       w_out, ln1_g[l], ln1_b[l], w_ff1, w_ff2, ln2_g[l], ln2_b[l],
                       skip_context=(l == DEPTH - 1))
        residual = (stream,)
    return stream[None]
```

```python
import functools
import math

import jax
import jax.numpy as jnp
from jax import lax
from jax.experimental import pallas as pl
from jax.experimental.pallas import tpu as pltpu

F32 = jnp.float32
MXU_DTYPE = jnp.bfloat16

D_MODEL = 1024
DEPTH = 4
GRID_W = 64
CTX_LEN = 256
CTX_PAD = 512
HEAD_DIM = 64
ATT_HEADS = 8
ATT_KV_HEADS = 2
ATT_BLOCK = 128
ATT_LOOKAHEAD = 2
ROPE_BASE = 10000.0
RET_HEADS = 4
RET_CHUNK = 256
S5_CH = 16
S5_GROUPS = 16
S5_STATE = 64
S5_T = 16
S5_PAIRS = S5_GROUPS // 2
S5_PAIR_W = 2 * S5_CH
S5_TILE = CTX_PAD // S5_T
SCAN_ROWS = 8
S5_STATE_W = 2 * 2 * S5_STATE
ATT_WIDTH = ATT_HEADS * HEAD_DIM
KV_WIDTH = ATT_KV_HEADS * HEAD_DIM
RET_WIDTH = RET_HEADS * HEAD_DIM
S5_WIDTH = S5_GROUPS * S5_CH
IN_WIDTH = ATT_WIDTH + 2 * KV_WIDTH + 4 * RET_WIDTH + S5_WIDTH
D_FF = 4 * D_MODEL
FF_CHUNK = 1024
N_ADA = 6
LN_EPS = 1e-5
GN_EPS = 1e-5
DEEPNORM_ALPHA = (2 * DEPTH) ** 0.25
ROW_TILE = 512
POST_SPLIT = 2
_NFF = D_FF // FF_CHUNK
POST_PROGRAM = (((0, 0), (0, 1), (1, 0), (0, 2), (1, 1)) + tuple((0, 2 + c) for c in range(1, _NFF))
                + ((1, 2), (0, 2 + _NFF)) + tuple((1, 2 + c) for c in range(1, _NFF)) + ((1, 2 + _NFF),))
NEG_BIG = -1e30
LOG2E = math.log2(math.e)
LANES = 128
VMEM_LIMIT = 56 * 1024 * 1024
STAGE_BYTES = 2 * 1024 * 1024

COL_AQ, COL_AK, COL_AV = 0, ATT_WIDTH, ATT_WIDTH + KV_WIDTH
COL_RQ = ATT_WIDTH + 2 * KV_WIDTH
COL_RK, COL_RV, COL_RG = COL_RQ + RET_WIDTH, COL_RQ + 2 * RET_WIDTH, COL_RQ + 3 * RET_WIDTH
COL_S5 = COL_RQ + 4 * RET_WIDTH


def _mm(a, b):
    return jnp.dot(a, b, preferred_element_type=F32)


def _mm_nt(a, b):
    return lax.dot_general(a, b, (((1,), (1,)), ((), ())), preferred_element_type=F32)


def _mm_tn(a, b):
    return lax.dot_general(a, b, (((0,), (0,)), ((), ())), preferred_element_type=F32)


def _params(sem):
    return pltpu.CompilerParams(dimension_semantics=sem, vmem_limit_bytes=VMEM_LIMIT)


def _mod_kernel(cond_ref, w_ref, b_ref, o_ref):
    c = cond_ref[...]
    s = c * jax.nn.sigmoid(c)
    o_ref[0] = jnp.dot(s, w_ref[0], preferred_element_type=F32, precision=lax.Precision.HIGHEST) + b_ref[0]


def _modulation(cond, w_ada, b_ada):
    tn = 1536
    n = N_ADA * D_MODEL
    return pl.pallas_call(
        _mod_kernel,
        grid=(DEPTH, n // tn),
        in_specs=[
            pl.BlockSpec((8, D_MODEL), lambda l, j: (0, 0)),
            pl.BlockSpec((1, D_MODEL, tn), lambda l, j: (l, 0, j)),
            pl.BlockSpec((1, 1, tn), lambda l, j: (l, 0, j)),
        ],
        out_specs=pl.BlockSpec((1, 8, tn), lambda l, j: (l, 0, j)),
        out_shape=jax.ShapeDtypeStruct((DEPTH, 8, n), F32),
        compiler_params=_params(("arbitrary", "arbitrary")),
        name="modulation",
    )(cond, w_ada, b_ada.reshape(DEPTH, 1, n))


def _lane_block_shuffle(src_rows, src_lane_blk, out_vreg):
    acc = None
    for q in range(LANES // S5_PAIR_W):
        piece = src_rows(out_vreg * (LANES // S5_PAIR_W) + q)
        shift = (S5_PAIR_W * (q - src_lane_blk)) % LANES
        if shift:
            piece = pltpu.roll(piece, shift, 1)
        if acc is None:
            acc = piece
        else:
            lane_blk = lax.broadcasted_iota(jnp.int32, piece.shape, 1) // S5_PAIR_W
            acc = jnp.where(lane_blk == q, piece, acc)
    return acc


def _in_proj_kernel(*refs, split, layer):
    nres = 2 if split else 1
    (mod_ref, w_hbm, scale_ref, ca_ref, sa_ref, cr_ref, sr_ref, perm_ref, o_ref, u_ref, w_ref, stage_ref,
     sem_ref) = refs[nres:]

    @pl.when(pl.program_id(0) == 0)
    def _stage_weights():
        _load_cast(w_hbm.at[layer], w_ref, stage_ref, sem_ref, col_scale=scale_ref[...])

    x = _residual_rows(refs[:nres])(slice(None))
    h = (x * (1.0 + mod_ref[1:2, :]) + mod_ref[0:1, :]).astype(w_ref.dtype)
    lane = lax.broadcasted_iota(jnp.int32, (x.shape[0], LANES), 1)
    first_att = (lane & 31) < 16
    first_ret = (lane & 63) < 32

    def proj(c0, c1):
        return _mm(h, w_ref[:, c0:c1])

    def rope_store(p, c0, width, cos, sin, first, half):
        for b in range(width // LANES):
            blk = p[:, LANES * b:LANES * (b + 1)]
            rot = jnp.where(first, pltpu.roll(blk, LANES - half, 1), pltpu.roll(blk, half, 1))
            o_ref[:, c0 + LANES * b:c0 + LANES * (b + 1)] = (blk * cos + rot * sin).astype(o_ref.dtype)

    def plain_store(p, c0, width):
        o_ref[:, c0:c0 + width] = p.astype(o_ref.dtype)

    u = proj(COL_S5, IN_WIDTH).astype(w_ref.dtype)
    sub = perm_ref.shape[0]
    nchunk = sub // S5_T
    for part in range(x.shape[0] // sub):
        g = _mm(perm_ref[...], u[sub * part:sub * (part + 1)])
        for a in range(S5_PAIRS):
            vreg_col, lane_blk = divmod(a * S5_PAIR_W, LANES)
            lane_blk //= S5_PAIR_W
            for v in range(S5_T * S5_PAIR_W // LANES):
                slab = _lane_block_shuffle(
                    lambda j: g[nchunk * j:nchunk * (j + 1), vreg_col * LANES:(vreg_col + 1) * LANES], lane_blk, v)
                u_ref[a, nchunk * part:nchunk * (part + 1), LANES * v:LANES * (v + 1)] = slab.astype(u_ref.dtype)

    ca, sa, cr, sr = ca_ref[...], sa_ref[...], cr_ref[...], sr_ref[...]
    att_rope = (ca, sa, first_att, HEAD_DIM // 4)
    ret_rope = (cr, sr, first_ret, HEAD_DIM // 2)
    groups = [(COL_AQ, ATT_WIDTH, att_rope), (COL_AK, KV_WIDTH, att_rope), (COL_RQ, RET_WIDTH, ret_rope),
              (COL_RK, RET_WIDTH, ret_rope), (COL_AV, KV_WIDTH, None), (COL_RV, COL_S5 - COL_RV, None)]
    pending = None
    for group in groups + [None]:
        nxt = None if group is None else (proj(group[0], group[0] + group[1]),) + group
        if pending is not None:
            p, pc0, pwidth, prope = pending
            if prope is None:
                plain_store(p, pc0, pwidth)
            else:
                rope_store(p, pc0, pwidth, *prope)
        pending = nxt


def _chunk_perm(tile_rows, dtype):
    nchunk = tile_rows // S5_T
    r = jnp.arange(tile_rows)
    src = S5_T * (r % nchunk) + r // nchunk
    return (src[:, None] == jnp.arange(tile_rows)[None, :]).astype(dtype)


def _in_proj(residual, mods, layer, w_in, col_scale, tabs, perm):
    rows = tabs[0].shape[0]
    tm = ROW_TILE
    tab_spec = pl.BlockSpec((tm, LANES), lambda i: (i, 0))
    nch = rows // S5_T
    return pl.pallas_call(
        functools.partial(_in_proj_kernel, split=len(residual) > 1, layer=layer),
        grid=(rows // tm,),
        in_specs=_residual_specs(residual) + [
            pl.BlockSpec((None, None, N_ADA, D_MODEL), lambda i: (layer, jnp.where(i == 0, 1, 0), 0, 0)),
            pl.BlockSpec(memory_space=pl.ANY),
            pl.BlockSpec((1, IN_WIDTH), lambda i: (0, 0)),
            tab_spec, tab_spec, tab_spec, tab_spec,
            pl.BlockSpec(perm.shape, lambda i: (0, 0)),
        ],
        out_specs=[pl.BlockSpec((tm, COL_S5), lambda i: (i, 0)),
                   pl.BlockSpec((S5_PAIRS, tm // S5_T, S5_T * S5_PAIR_W), lambda i: (0, i, 0))],
        out_shape=[jax.ShapeDtypeStruct((rows, COL_S5), MXU_DTYPE),
                   jax.ShapeDtypeStruct((S5_PAIRS, nch, S5_T * S5_PAIR_W), MXU_DTYPE)],
        scratch_shapes=[pltpu.VMEM((D_MODEL, IN_WIDTH), MXU_DTYPE),
                        pltpu.VMEM((2, STAGE_BYTES // (4 * IN_WIDTH), IN_WIDTH), F32),
                        pltpu.SemaphoreType.DMA((2,))],
        compiler_params=_params(("arbitrary",)),
        name="in_proj",
    )(*residual, mods, w_in, col_scale, *tabs, perm)


def _rope_tables(seq):
    half_a = HEAD_DIM // 4
    half_r = HEAD_DIM // 2
    nrow = seq // GRID_W
    inv_a = ROPE_BASE ** (-jnp.arange(half_a, dtype=F32) / half_a)
    inv_r = ROPE_BASE ** (-jnp.arange(half_r, dtype=F32) / half_r)
    ang_r = jnp.arange(nrow, dtype=F32)[:, None] * inv_a[None, :]
    ang_c = jnp.arange(GRID_W, dtype=F32)[:, None] * inv_a[None, :]
    ang_t = jnp.arange(seq, dtype=F32)[:, None] * inv_r[None, :]
    hp = lax.Precision.HIGHEST
    lane = jnp.arange(LANES)
    within = lane % HEAD_DIM
    pick_a = (within % half_a)[None, :] == jnp.arange(half_a)[:, None]
    exp_row = (pick_a & (within < 2 * half_a)[None, :]).astype(F32)
    exp_col = (pick_a & (within >= 2 * half_a)[None, :]).astype(F32)
    exp_t = ((lane % half_r)[None, :] == jnp.arange(half_r)[:, None]).astype(F32)
    sign_a = jnp.where(within % (2 * half_a) < half_a, -1.0, 1.0).astype(F32)
    sign_r = jnp.where(within < half_r, -1.0, 1.0).astype(F32)

    def att_table(fn):
        by_row = jnp.dot(fn(ang_r), exp_row, precision=hp)
        by_col = jnp.dot(fn(ang_c), exp_col, precision=hp)
        return (by_row[:, None, :] + by_col[None, :, :]).reshape(seq, LANES)

    cos_a = att_table(jnp.cos)
    sin_a = att_table(jnp.sin) * sign_a
    cos_r = jnp.dot(jnp.cos(ang_t), exp_t, precision=hp)
    sin_r = jnp.dot(jnp.sin(ang_t), exp_t, precision=hp) * sign_r
    pad = lambda tab, ident: jnp.pad(tab, ((CTX_PAD, 0), (0, 0)), constant_values=ident)
    return pad(cos_a, 1.0), pad(sin_a, 0.0), pad(cos_r, 1.0), pad(sin_r, 0.0)


def _swap_halves(x):
    if x.dtype.itemsize == 4:
        return pltpu.roll(x, 64, 1)
    packed = pltpu.bitcast(x, jnp.uint32)
    return pltpu.bitcast(pltpu.roll(packed, 64, 1), x.dtype)


def _dup_heads(x):
    sw = _swap_halves(x)
    lo = lax.broadcasted_iota(jnp.int32, x.shape, 1) < HEAD_DIM
    return jnp.where(lo, x, sw), jnp.where(lo, sw, x)


def _attn_kernel(sink_ref, q_ref, km_ref, kp_ref, kn_ref, vm_ref, vp_ref, vn_ref, kc_ref, vc_ref, mask_ref,
                 o_ref, k2_ref, v2_ref, kc2_ref, vc2_ref):
    i = pl.program_id(0)
    last_blk = pl.num_programs(0) * (ROW_TILE // ATT_BLOCK) - 1
    blk = ATT_BLOCK
    def spread(src, ones_upper):
        x = src[...]
        a, b = _dup_heads(x)
        if ones_upper:
            upper = lax.broadcasted_iota(jnp.int32, x.shape, 1) >= HEAD_DIM
            a = jnp.where(upper, jnp.ones_like(a), a)
            b = jnp.where(upper, jnp.ones_like(b), b)
        return a, b

    for dst, parts, is_v in ((k2_ref, (kp_ref, km_ref, kn_ref), False), (v2_ref, (vp_ref, vm_ref, vn_ref), True)):
        row = 0
        for part in parts:
            a, b = spread(part, is_v)
            n = part.shape[0]
            dst[0, row:row + n, :] = a
            dst[1, row:row + n, :] = b
            row += n
    for dst, src, is_v in ((kc2_ref, kc_ref, False), (vc2_ref, vc_ref, True)):
        a, b = spread(src, is_v)
        dst[0] = a
        dst[1] = b

    lo = lax.broadcasted_iota(jnp.int32, (blk, LANES), 1) < HEAD_DIM
    group = ATT_HEADS // ATT_KV_HEADS

    nloc = 3 * blk

    def scores(j, kv):
        r0 = j * blk
        qt = q_ref[r0:r0 + blk, group * HEAD_DIM * kv:group * HEAD_DIM * (kv + 1)]
        parts = []
        for g in range(group):
            qc = qt[:, LANES * (g // 2):LANES * (g // 2 + 1)]
            keep = lo if g % 2 == 0 else jnp.logical_not(lo)
            parts.append(jnp.where(keep, qc, jnp.zeros_like(qc)))
        qs = jnp.concatenate(parts, axis=0)
        return _mm_nt(qs, k2_ref[kv, r0:r0 + nloc, :]), _mm_nt(qs, kc2_ref[kv])

    def finish(j, kv, s_loc, s_ctx):
        r0 = j * blk
        gblk = i * (ROW_TILE // blk) + j
        sel = jnp.where(i == 0, 3, jnp.where(gblk == CTX_PAD // blk, 0, jnp.where(gblk == last_blk, 2, 1)))
        bias = mask_ref[sel]
        probs, sink_w = [], []
        for g in range(group):
            s = jnp.concatenate([s_loc[blk * g:blk * (g + 1)] + bias, s_ctx[blk * g:blk * (g + 1)]], axis=1)
            sk = sink_ref[group * kv + g]
            m = jnp.maximum(jnp.max(s, axis=-1, keepdims=True), sk)
            probs.append(jnp.exp2(s - m).astype(o_ref.dtype))
            sink_w.append(jnp.exp2(sk - m))
        p = jnp.concatenate(probs, axis=0)
        o = _mm(p[:, :nloc], v2_ref[kv, r0:r0 + nloc, :]) + _mm(p[:, nloc:], vc2_ref[kv])
        for half in range(group // 2):
            even, odd = 2 * half, 2 * half + 1
            oe = o[blk * even:blk * (even + 1)]
            oo = o[blk * odd:blk * (odd + 1)]
            y_even = oe * (1.0 / (pltpu.roll(oe, HEAD_DIM, 1) + sink_w[even]))
            y_odd = pltpu.roll(oo, HEAD_DIM, 1) * (1.0 / (oo + sink_w[odd]))
            c0 = group * HEAD_DIM * kv + LANES * half
            o_ref[r0:r0 + blk, c0:c0 + LANES] = jnp.where(lo, y_even, y_odd).astype(o_ref.dtype)

    items = [(j, kv) for j in range(ROW_TILE // blk) for kv in range(ATT_KV_HEADS)]
    pending = {}
    for t in range(len(items) + ATT_LOOKAHEAD):
        if t < len(items):
            pending[t] = scores(*items[t])
        if t >= ATT_LOOKAHEAD:
            finish(*items[t - ATT_LOOKAHEAD], *pending.pop(t - ATT_LOOKAHEAD))


def _attention_masks():
    qi = jnp.arange(ATT_BLOCK)[:, None]
    kj = jnp.arange(3 * ATT_BLOCK)[None, :]
    band = jnp.abs(kj - ATT_BLOCK - qi) <= ATT_BLOCK
    first = band & (kj >= ATT_BLOCK)
    last = band & (kj < 2 * ATT_BLOCK)
    none = jnp.zeros_like(band)
    masks = jnp.stack([first, band, last, none])
    return jnp.where(masks, 0.0, NEG_BIG).astype(F32)


def _attention(proj, sink, masks):
    rows = proj.shape[0]
    tm, blk = ROW_TILE, ATT_BLOCK
    per = tm // blk
    nblk = rows // blk
    ck, cv = COL_AK // KV_WIDTH, COL_AV // KV_WIDTH
    dt = proj.dtype
    return pl.pallas_call(
        _attn_kernel,
        grid=(rows // tm,),
        in_specs=[
            pl.BlockSpec(memory_space=pltpu.SMEM),
            pl.BlockSpec((tm, ATT_WIDTH), lambda i: (i, 0)),
            pl.BlockSpec((tm, KV_WIDTH), lambda i: (i, ck)),
            pl.BlockSpec((blk, KV_WIDTH), lambda i: (jnp.maximum(i * per - 1, 0), ck)),
            pl.BlockSpec((blk, KV_WIDTH), lambda i: (jnp.minimum((i + 1) * per, nblk - 1), ck)),
            pl.BlockSpec((tm, KV_WIDTH), lambda i: (i, cv)),
            pl.BlockSpec((blk, KV_WIDTH), lambda i: (jnp.maximum(i * per - 1, 0), cv)),
            pl.BlockSpec((blk, KV_WIDTH), lambda i: (jnp.minimum((i + 1) * per, nblk - 1), cv)),
            pl.BlockSpec((CTX_LEN, KV_WIDTH), lambda i: (0, ck)),
            pl.BlockSpec((CTX_LEN, KV_WIDTH), lambda i: (0, cv)),
            pl.BlockSpec((4, blk, 3 * blk), lambda i: (0, 0, 0)),
        ],
        out_specs=pl.BlockSpec((tm, ATT_WIDTH), lambda i: (i, 0)),
        out_shape=jax.ShapeDtypeStruct((rows, ATT_WIDTH), dt),
        scratch_shapes=[
            pltpu.VMEM((2, tm + 2 * blk, KV_WIDTH), dt),
            pltpu.VMEM((2, tm + 2 * blk, KV_WIDTH), dt),
            pltpu.VMEM((2, CTX_LEN, KV_WIDTH), dt),
            pltpu.VMEM((2, CTX_LEN, KV_WIDTH), dt),
        ],
        compiler_params=_params(("arbitrary",)),
        name="attention",
    )(sink, proj, proj, proj, proj, proj, proj, proj, proj, proj, masks)


def _ret_kernel(*refs, sup, ntile):
    lg_ref, q_ref = refs[:2]
    k_refs, v_refs = refs[2:2 + sup], refs[2 + sup:2 + 2 * sup]
    g_ref, o_ref, sb_ref, s_ref, dm_ref, tab_ref, gbd_ref = refs[2 + 2 * sup:]
    k_ref, v_ref = k_refs[0], v_refs[0]
    step = pl.program_id(0)
    nsup = (ntile - 1) // sup
    ph = jnp.where(step <= nsup, 0, 1)
    t = jnp.where(step <= nsup, step, step - (nsup + 1))
    c = RET_CHUNK
    w = RET_WIDTH
    per = q_ref.shape[0] // c
    mdt = sb_ref.dtype
    rows = lambda ci: slice(c * ci, c * (ci + 1))

    def lane_vec(direction, shape, axis):
        head = lax.broadcasted_iota(jnp.int32, shape, axis) // HEAD_DIM
        out = jnp.full(shape, lg_ref[direction, RET_HEADS - 1], F32)
        for h in range(RET_HEADS - 2, -1, -1):
            out = jnp.where(head == h, lg_ref[direction, h], out)
        return out

    @pl.when(jnp.logical_and(ph == 0, t == 0))
    def _init_tables():
        diff = (lax.broadcasted_iota(jnp.int32, (c, c), 0) - lax.broadcasted_iota(jnp.int32, (c, c), 1)).astype(F32)
        for h in range(RET_HEADS):
            dm_ref[h] = jnp.exp(jnp.where(diff >= 0, diff * lg_ref[0, h], -diff * lg_ref[1, h]))
        pos = lax.broadcasted_iota(jnp.int32, (c, w), 0).astype(F32)
        lgf = lane_vec(0, (c, w), 1)
        lgb = lane_vec(1, (c, w), 1)
        tab_ref[0] = jnp.exp((c - 1.0 - pos) * lgf)
        tab_ref[1] = jnp.exp((pos + 1.0) * lgf)
        tab_ref[2] = jnp.exp(pos * lgb)
        tab_ref[3] = jnp.exp((c - pos) * lgb)
        same = (lax.broadcasted_iota(jnp.int32, (w, w), 0) // HEAD_DIM
                == lax.broadcasted_iota(jnp.int32, (w, w), 1) // HEAD_DIM)
        bd = jnp.where(same, 1.0, 0.0)
        gbd_ref[0] = bd * jnp.exp(c * lane_vec(0, (w, w), 0))
        gbd_ref[1] = bd * jnp.exp(c * lane_vec(1, (w, w), 0))
        gbd_ref[2] = bd

    @pl.when(t == 0)
    def _reset_state():
        s_ref[...] = jnp.zeros_like(s_ref)

    def state_update(direction, key_tab, ci, kr=k_ref, vr=v_ref):
        kw = (kr[rows(ci), :].astype(F32) * tab_ref[key_tab]).astype(mdt)
        u = _mm_tn(kw, vr[rows(ci), :])
        s_ref[...] = gbd_ref[direction] * s_ref[...] + gbd_ref[2] * u

    @pl.when(jnp.logical_and(ph == 0, t == 0))
    def _backward_context():
        sb_ref[0] = s_ref[...].astype(mdt)
        state_update(1, 2, 0)

    @pl.when(jnp.logical_and(ph == 0, t > 0))
    def _backward_latent():
        base = 1 + per * sup * (nsup - t)
        for qi in range(sup - 1, -1, -1):
            for ci in range(per - 1, -1, -1):
                sb_ref[base + per * qi + ci] = s_ref[...].astype(mdt)
                state_update(1, 2, ci, k_refs[qi], v_refs[qi])

    head = lax.broadcasted_iota(jnp.int32, (c, w), 1) // HEAD_DIM

    def scores(ci):
        q = q_ref[rows(ci), :]
        qs = jnp.concatenate([jnp.where(head == h, q, jnp.zeros_like(q)) for h in range(RET_HEADS)], axis=0)
        return _mm_nt(qs, k_ref[rows(ci), :])

    def intra(ci, sc):
        scd = jnp.concatenate([sc[c * h:c * (h + 1)] * dm_ref[h] for h in range(RET_HEADS)], axis=0).astype(mdt)
        oi = _mm(scd, v_ref[rows(ci), :])
        o = jnp.where(head == 0, oi[0:c], 0.0)
        for h in range(1, RET_HEADS):
            o = o + jnp.where(head == h, oi[c * h:c * (h + 1)], 0.0)
        return o

    def cross(ci, idx):
        qf = q_ref[rows(ci), :].astype(F32)
        return (_mm((qf * tab_ref[1]).astype(mdt), s_ref[...].astype(mdt))
                + _mm((qf * tab_ref[3]).astype(mdt), sb_ref[idx]))

    def finish(ci, o):
        avg = (gbd_ref[2] * (1.0 / HEAD_DIM)).astype(mdt)
        o_hi = o.astype(mdt)
        d = o - (_mm(o_hi, avg) + _mm((o - o_hi.astype(F32)).astype(mdt), avg))
        var = _mm((d * d).astype(mdt), avg)
        gate = g_ref[rows(ci), :].astype(F32)
        y = d * lax.rsqrt(var + GN_EPS) * (gate * jax.nn.sigmoid(gate))
        o_ref[rows(ci), :] = y.astype(o_ref.dtype)

    @pl.when(jnp.logical_and(ph == 1, t == 0))
    def _forward_context():
        o = intra(0, scores(0)) + cross(0, 0)
        state_update(0, 0, 0)
        finish(0, o)
        for ci in range(1, per):
            o_ref[rows(ci), :] = jnp.zeros((c, w), o_ref.dtype)

    @pl.when(jnp.logical_and(ph == 1, t > 0))
    def _forward_latent():
        base = 1 + per * (t - 1)
        sc = [scores(ci) for ci in range(per)]
        outs = [intra(ci, sc[ci]) for ci in range(per)]
        for ci in range(per):
            outs[ci] = outs[ci] + cross(ci, base + ci)
            state_update(0, 0, ci)
        for ci in range(per):
            finish(ci, outs[ci])


def _retention(proj, log_gamma):
    rows = proj.shape[0]
    c = RET_CHUNK
    tm = ROW_TILE
    ntile = rows // tm
    nchunk = 1 + (rows - CTX_PAD) // c
    dt = proj.dtype
    cq, ckk, cvv, cg = (COL_RQ // RET_WIDTH, COL_RK // RET_WIDTH, COL_RV // RET_WIDTH, COL_RG // RET_WIDTH)

    nlat = ntile - 1
    sup = next(s for s in (4, 2, 1) if nlat % s == 0)
    nsup = nlat // sup
    nback = 1 + nsup

    def fw_blk(step):
        return jnp.maximum(step - nback, 0)

    def kv_spec(col, qi):
        def index(step):
            back = jnp.where(step == 0, 0, 1 + sup * (nsup - jnp.minimum(step, nsup)) + qi)
            fwd = step - nback if qi == 0 else 1 + qi
            return jnp.where(step < nback, back, fwd), col
        return pl.BlockSpec((tm, RET_WIDTH), index)

    return pl.pallas_call(
        functools.partial(_ret_kernel, sup=sup, ntile=ntile),
        grid=(nback + ntile,),
        in_specs=[pl.BlockSpec(memory_space=pltpu.SMEM),
                  pl.BlockSpec((tm, RET_WIDTH), lambda step: (fw_blk(step), cq))]
                 + [kv_spec(ckk, qi) for qi in range(sup)] + [kv_spec(cvv, qi) for qi in range(sup)]
                 + [pl.BlockSpec((tm, RET_WIDTH), lambda step: (fw_blk(step), cg))],
        out_specs=pl.BlockSpec((tm, RET_WIDTH), lambda step: (fw_blk(step), 0)),
        out_shape=jax.ShapeDtypeStruct((rows, RET_WIDTH), dt),
        scratch_shapes=[
            pltpu.VMEM((nchunk, RET_WIDTH, RET_WIDTH), dt),
            pltpu.VMEM((RET_WIDTH, RET_WIDTH), F32),
            pltpu.VMEM((RET_HEADS, c, c), F32),
            pltpu.VMEM((4, c, RET_WIDTH), F32),
            pltpu.VMEM((3, RET_WIDTH, RET_WIDTH), F32),
        ],
        compiler_params=_params(("arbitrary",)),
        name="retention",
    )(log_gamma, proj, *([proj] * (2 * sup)), proj)


def _s5_weights(lam_re, lam_im, b_re, b_im, c_re, c_im, log_dt, d_skip):
    hp = lax.Precision.HIGHEST
    tt, g, n, p, a = S5_T, S5_GROUPS, S5_STATE, S5_CH, S5_PAIRS
    lam = lax.complex(lam_re.astype(F32), lam_im.astype(F32))
    dtv = jnp.exp(log_dt.astype(F32))[..., None]
    lam_bar = jnp.exp(lam * dtv)
    bbar = ((lam_bar - 1.0) / lam)[..., None] * lax.complex(b_re.astype(F32), b_im.astype(F32))
    cmat = lax.complex(c_re.astype(F32), c_im.astype(F32))
    pw = [jnp.ones_like(lam_bar)]
    for _ in range(tt):
        pw.append(pw[-1] * lam_bar)
    pw = jnp.stack(pw, axis=1)
    eye2 = jnp.eye(2, dtype=F32)
    ri = lambda z, axis: jnp.stack([jnp.real(z), jnp.imag(z)], axis=axis)

    pw_l = pw.reshape(2, tt + 1, a, 2 * n)
    bbt = jnp.einsum('dahpn,gh->dagphn', jnp.swapaxes(bbar, -1, -2).reshape(2, a, 2, p, n), eye2)
    bbt = bbt.reshape(2, a, 2 * p, 2 * n)
    cct = jnp.einsum('dahpn,gh->dagphn', cmat.reshape(2, a, 2, p, n), eye2).reshape(2, a, 2 * p, 2 * n)
    pw_k = ri(pw_l, 1).transpose(3, 0, 1, 2, 4)
    b_k = ri(bbt, 2).transpose(1, 0, 2, 3, 4)
    c_k = ri(cct, 2).transpose(1, 0, 2, 3, 4)
    decay = [pw_l[:, tt]]
    for _ in range(SCAN_ROWS - 1):
        decay.append(decay[-1] * decay[0])
    decay = jnp.stack(decay, axis=0)
    rows8 = lambda z: jnp.broadcast_to(z[None], (SCAN_ROWS,) + z.shape)
    carry_w = jnp.stack([decay[:, 0], decay[::-1, 1]], axis=1)
    scan_tab = jnp.stack([rows8(decay[0]), rows8(decay[1]), rows8(decay[3]), carry_w], axis=0)
    scan_tab = ri(scan_tab, 0).transpose(4, 3, 1, 0, 2, 5)
    skip = jnp.tile(d_skip.astype(F32).reshape(a, 1, 2 * p), (1, tt, 1)).reshape(a, 1, tt * 2 * p)
    return pw_k, b_k, c_k, scan_tab, skip


def _pair_spec(layer, *shape):
    return pl.BlockSpec((None, None) + shape, lambda i: (layer, i) + (0,) * len(shape))


def _s5_drive_kernel(u_ref, pw_ref, b_ref, o_ref, w_ref):
    rows, half = S5_PAIR_W, LANES
    for d in range(2):
        br, bi = b_ref[d, 0], b_ref[d, 1]
        for j in range(S5_T):
            e = S5_T - 1 - j if d == 0 else j
            pr, pi = pw_ref[d, 0, e:e + 1, :], pw_ref[d, 1, e:e + 1, :]
            w_ref[rows * j:rows * (j + 1), 2 * half * d:2 * half * d + half] = (pr * br - pi * bi).astype(w_ref.dtype)
            w_ref[rows * j:rows * (j + 1), 2 * half * d + half:2 * half * (d + 1)] = (
                pr * bi + pi * br).astype(w_ref.dtype)
    o_ref[0] = _mm(u_ref[0], w_ref[...])


def _s5_drive(u_pairs, pw_k, b_k, layer):
    a, nch, wd = u_pairs.shape
    return pl.pallas_call(
        _s5_drive_kernel,
        grid=(a,),
        in_specs=[pl.BlockSpec((1, nch, wd), lambda i: (i, 0, 0)),
                  _pair_spec(layer, *pw_k.shape[2:]), _pair_spec(layer, *b_k.shape[2:])],
        out_specs=pl.BlockSpec((1, nch, 2 * S5_STATE_W), lambda i: (i, 0, 0)),
        out_shape=jax.ShapeDtypeStruct((a, nch, 2 * S5_STATE_W), F32),
        scratch_shapes=[pltpu.VMEM((wd, 2 * S5_STATE_W), u_pairs.dtype)],
        compiler_params=_params(("arbitrary",)),
        name="s5_drive",
    )(u_pairs, pw_k, b_k)


def _s5_scan_kernel(ef_ref, eb_ref, tab_ref, sf_ref, sb_ref, st_ref):
    t = pl.program_id(0)
    hw = LANES
    sub = SCAN_ROWS
    npair = tab_ref.shape[0]
    row = lax.broadcasted_iota(jnp.int32, (sub, hw), 0)

    @pl.when(t == 0)
    def _reset():
        st_ref[...] = jnp.zeros_like(st_ref)

    def shift(x, k, reverse):
        if reverse:
            return jnp.where(row < sub - k, pltpu.roll(x, sub - k, 0), 0.0)
        return jnp.where(row >= k, pltpu.roll(x, k, 0), 0.0)

    def scan_group(e_ref, o_ref, a, d, r0, cr, ci):
        reverse = d == 1
        xr = e_ref[a, pl.ds(r0, sub), 0:hw]
        xi = e_ref[a, pl.ds(r0, sub), hw:2 * hw]
        for step, k in enumerate((1, 2, 4)):
            ar, ai = tab_ref[a, d, step, 0], tab_ref[a, d, step, 1]
            sr, si = shift(xr, k, reverse), shift(xi, k, reverse)
            xr, xi = xr + ar * sr - ai * si, xi + ar * si + ai * sr
        wr, wi = tab_ref[a, d, 3, 0], tab_ref[a, d, 3, 1]
        fr = xr + wr * cr - wi * ci
        fi = xi + wr * ci + wi * cr
        edge = sub - 1 if reverse else 0
        o_ref[a, pl.ds(r0, sub), 0:hw] = jnp.where(row == edge, cr, shift(fr, 1, reverse))
        o_ref[a, pl.ds(r0, sub), hw:2 * hw] = jnp.where(row == edge, ci, shift(fi, 1, reverse))
        last = 0 if reverse else sub - 1
        return (jnp.broadcast_to(fr[last:last + 1], (sub, hw)), jnp.broadcast_to(fi[last:last + 1], (sub, hw)))

    def run(nrows):
        ngroups = nrows // sub

        def body(gi, carry):
            r0 = pl.multiple_of(gi * sub, sub)
            rb0 = pl.multiple_of((ngroups - 1 - gi) * sub, sub)
            new = []
            for a in range(npair):
                fr, fi, br, bi = carry[a]
                fr, fi = scan_group(ef_ref, sf_ref, a, 0, r0, fr, fi)
                br, bi = scan_group(eb_ref, sb_ref, a, 1, rb0, br, bi)
                new.append((fr, fi, br, bi))
            return tuple(new)

        init = tuple(tuple(st_ref[a, k] for k in range(4)) for a in range(npair))
        final = lax.fori_loop(0, ngroups, body, init)
        for a in range(npair):
            for k in range(4):
                st_ref[a, k] = final[a][k]

    @pl.when(t == 0)
    def _context():
        sf_ref[...] = jnp.zeros_like(sf_ref)
        sb_ref[...] = jnp.zeros_like(sb_ref)
        run(CTX_LEN // S5_T)

    @pl.when(t > 0)
    def _latent():
        run(S5_TILE)


def _s5_scan(drive, scan_tab, layer):
    a, nch, wd2 = drive.shape
    wd = wd2 // 2
    nt = nch // S5_TILE

    def bwd(t):
        return jnp.where(t == 0, 0, nt - t)

    return pl.pallas_call(
        _s5_scan_kernel,
        grid=(nt,),
        in_specs=[pl.BlockSpec((a, S5_TILE, wd), lambda t: (0, t, 0)),
                  pl.BlockSpec((a, S5_TILE, wd), lambda t: (0, bwd(t), 1)),
                  pl.BlockSpec((None,) + scan_tab.shape[1:], lambda t: (layer,) + (0,) * (scan_tab.ndim - 1))],
        out_specs=[pl.BlockSpec((a, S5_TILE, wd), lambda t: (0, t, 0)),
                   pl.BlockSpec((a, S5_TILE, wd), lambda t: (0, bwd(t), 0))],
        out_shape=[jax.ShapeDtypeStruct((a, nch, wd), F32), jax.ShapeDtypeStruct((a, nch, wd), F32)],
        scratch_shapes=[pltpu.VMEM((a, 4, SCAN_ROWS, LANES), F32)],
        compiler_params=_params(("arbitrary",)),
        name="s5_scan",
    )(drive, drive, scan_tab)


def _lane_window(x, start, width):
    cols = []
    for v in range(width // LANES):
        k0, off = divmod(start + LANES * v, LANES)
        lo = x[:, LANES * k0:LANES * (k0 + 1)]
        if off:
            hi = x[:, LANES * (k0 + 1):LANES * (k0 + 2)]
            lane = lax.broadcasted_iota(jnp.int32, lo.shape, 1)
            lo = jnp.where(lane < LANES - off, pltpu.roll(lo, LANES - off, 1), pltpu.roll(hi, LANES - off, 1))
        cols.append(lo)
    return jnp.concatenate(cols, axis=1)


def _s5_read_kernel(u_ref, sf_ref, sb_ref, pw_ref, b_ref, c_ref, skip_ref, o_ref, wt_ref, wi_ref, lag_ref):
    u = u_ref[0]
    mdt = u.dtype
    rows, half = S5_PAIR_W, LANES
    for d in range(2):
        cr, ci = c_ref[d, 0], c_ref[d, 1]
        for i in range(S5_T):
            e = i + 1 if d == 0 else S5_T - i
            pr, pi = pw_ref[d, 0, e:e + 1, :], pw_ref[d, 1, e:e + 1, :]
            wt_ref[d, rows * i:rows * (i + 1), 0:half] = (pr * cr - pi * ci).astype(mdt)
            wt_ref[d, rows * i:rows * (i + 1), half:2 * half] = (-(pr * ci + pi * cr)).astype(mdt)
    nlag = 2 * S5_T - 1
    ldt = lag_ref.dtype
    zero = jnp.zeros((rows, half), ldt)
    for l in range(nlag + 1):
        lag = l - (S5_T - 1)
        for d, active in ((1, lag <= 0), (0, 0 <= lag < S5_T)):
            col = 2 * half * (1 - d)
            if active:
                cr, ci = c_ref[d, 0], c_ref[d, 1]
                pr, pi = pw_ref[d, 0, abs(lag):abs(lag) + 1, :], pw_ref[d, 1, abs(lag):abs(lag) + 1, :]
                lag_ref[rows * l:rows * (l + 1), col:col + half] = (pr * cr - pi * ci).astype(ldt)
                lag_ref[rows * l:rows * (l + 1), col + half:col + 2 * half] = (pr * ci + pi * cr).astype(ldt)
            else:
                lag_ref[rows * l:rows * (l + 1), col:col + half] = zero
                lag_ref[rows * l:rows * (l + 1), col + half:col + 2 * half] = zero
    lhs = jnp.concatenate([b_ref[1, 0], -b_ref[1, 1], b_ref[0, 0], -b_ref[0, 1]], axis=1).astype(ldt)
    kall = _mm_nt(lhs, lag_ref[...])
    for j in range(S5_T):
        wi_ref[rows * j:rows * (j + 1), :] = _lane_window(kall, rows * (S5_T - 1 - j), S5_T * rows).astype(mdt)
    y = _mm(u, wi_ref[...])
    y = y + _mm_nt(sf_ref[0].astype(mdt), wt_ref[0])
    y = y + _mm_nt(sb_ref[0].astype(mdt), wt_ref[1])
    o_ref[0] = y + u.astype(F32) * skip_ref[...]


def _s5_read(u_pairs, sf, sb, pw_k, b_k, c_k, skip, layer):
    a, nch, wd = u_pairs.shape
    blk = lambda *shape: pl.BlockSpec((1,) + shape, lambda i: (i, 0, 0))
    return pl.pallas_call(
        _s5_read_kernel,
        grid=(a,),
        in_specs=[blk(nch, wd), blk(nch, S5_STATE_W), blk(nch, S5_STATE_W), _pair_spec(layer, *pw_k.shape[2:]),
                  _pair_spec(layer, *b_k.shape[2:]), _pair_spec(layer, *c_k.shape[2:]),
                  _pair_spec(layer, *skip.shape[2:])],
        out_specs=blk(nch, wd),
        out_shape=jax.ShapeDtypeStruct((a, nch, wd), F32),
        scratch_shapes=[pltpu.VMEM((2, wd, S5_STATE_W), u_pairs.dtype), pltpu.VMEM((wd, wd), u_pairs.dtype),
                        pltpu.VMEM((2 * wd, 2 * S5_STATE_W), u_pairs.dtype)],
        compiler_params=_params(("arbitrary",)),
        name="s5_read",
    )(u_pairs, sf, sb, pw_k, b_k, c_k, skip)


def _s5_mixer(u_pairs, weights, layer):
    pw_k, b_k, c_k, scan_tab, skip = weights
    drive = _s5_drive(u_pairs, pw_k, b_k, layer)
    sf, sb = _s5_scan(drive, scan_tab, layer)
    return _s5_read(u_pairs, sf, sb, pw_k, b_k, c_k, skip, layer)


def _layer_norm(x, g, b):
    mu = jnp.mean(x, axis=-1, keepdims=True)
    d = x - mu
    var = jnp.mean(d * d, axis=-1, keepdims=True)
    return d * lax.rsqrt(var + LN_EPS) * g + b


def _residual_rows(refs):
    if len(refs) == 1:
        return lambda rows: refs[0][rows, :]
    head_ref, body_ref = refs
    is_head = pl.program_id(0) == 0
    return lambda rows: jnp.where(is_head, head_ref[rows, :], body_ref[rows, :])


def _load_cast(src_hbm, dst_ref, stage_ref, sems, col_scale=None):
    chunk = stage_ref.shape[1]
    nchunk = src_hbm.shape[0] // chunk

    def copy(k):
        slot = k % 2
        return pltpu.make_async_copy(src_hbm.at[pl.ds(k * chunk, chunk), :], stage_ref.at[slot], sems.at[slot])

    copy(0).start()
    for k in range(nchunk):
        if k + 1 < nchunk:
            copy(k + 1).start()
        copy(k).wait()
        vals = stage_ref[k % 2]
        if col_scale is not None:
            vals = vals * col_scale
        dst_ref[k * chunk:(k + 1) * chunk, :] = vals.astype(dst_ref.dtype)


def _post_kernel(*refs, split, layer):
    nres = 2 if split else 1
    load_x = _residual_rows(refs[:nres])
    (att_ref, ret_ref, s5_ref, mod_ref, permt_ref, wglu_ref, bglu_ref, wo_hbm, g1_ref, b1_ref, w1_hbm, w2_hbm,
     g2_ref, b2_ref, o_ref, wo_ref, w1_ref, w2_ref, stage1_ref, stage2_ref, sem_ref) = refs[nres:]
    woa_ref = wo_ref.at[0:ATT_WIDTH]
    wor_ref = wo_ref.at[ATT_WIDTH:ATT_WIDTH + RET_WIDTH]
    wos_ref = wo_ref.at[ATT_WIDTH + RET_WIDTH:ATT_WIDTH + RET_WIDTH + S5_WIDTH]
    mdt = w1_ref.dtype

    @pl.when(pl.program_id(0) == 0)
    def _stage_weights():
        _load_cast(wo_hbm.at[layer], wo_ref, stage2_ref, sem_ref.at[1])
        _load_cast(w1_hbm.at[layer], w1_ref, stage1_ref, sem_ref.at[0])
        _load_cast(w2_hbm.at[layer], w2_ref, stage2_ref, sem_ref.at[1])
    sub = o_ref.shape[0] // POST_SPLIT
    csub = sub // S5_T
    nff = D_FF // FF_CHUNK

    def mix(part):
        rows = slice(sub * part, sub * (part + 1))
        zrows = []
        for i in range(S5_T):
            src_vreg, src_blk = divmod(i * S5_PAIR_W, LANES)
            src_blk //= S5_PAIR_W
            cols = [_lane_block_shuffle(
                lambda a: s5_ref[a, csub * part:csub * (part + 1), src_vreg * LANES:(src_vreg + 1) * LANES],
                src_blk, w) for w in range(S5_WIDTH // LANES)]
            zrows.append(jnp.concatenate(cols, axis=1))
        hs = jax.nn.gelu(jnp.concatenate(zrows, axis=0))
        gate = jax.nn.sigmoid(_mm(hs.astype(mdt), wglu_ref[...]) + bglu_ref[...])
        s5 = _mm(permt_ref[...], (hs * gate).astype(mdt)).astype(mdt)
        return _mm(att_ref[rows, :], woa_ref[...]) + _mm(ret_ref[rows, :], wor_ref[...]) + _mm(s5, wos_ref[...])

    def norm1(part, ox):
        rows = slice(sub * part, sub * (part + 1))
        x1 = _layer_norm(DEEPNORM_ALPHA * load_x(rows) + mod_ref[2:3, :] * ox, g1_ref[...], b1_ref[...])
        return x1, (x1 * (1.0 + mod_ref[4:5, :]) + mod_ref[3:4, :]).astype(mdt)

    def ff(h, c):
        a = _mm(h, w1_ref[:, FF_CHUNK * c:FF_CHUNK * (c + 1)])
        a = jnp.square(jnp.maximum(a, 0.0)).astype(mdt)
        return _mm(a, w2_ref[FF_CHUNK * c:FF_CHUNK * (c + 1), :])

    def norm2(part, x1, acc):
        rows = slice(sub * part, sub * (part + 1))
        o_ref[rows, :] = _layer_norm(DEEPNORM_ALPHA * x1 + mod_ref[5:6, :] * acc, g2_ref[...], b2_ref[...])

    nstage = nff + 3
    state = [dict() for _ in range(POST_SPLIT)]
    for part, stage in POST_PROGRAM:
        st = state[part]
        if stage == 0:
            st["ox"] = mix(part)
        elif stage == 1:
            st["x1"], st["h"] = norm1(part, st.pop("ox"))
        elif stage < nstage - 1:
            term = ff(st["h"], stage - 2)
            st["acc"] = term if stage == 2 else st["acc"] + term
        else:
            norm2(part, st["x1"], st["acc"])


def _residual_specs(residual):
    tm = ROW_TILE
    if len(residual) == 1:
        return [pl.BlockSpec((tm, D_MODEL), lambda i: (i, 0))]
    head_tiles = CTX_PAD // tm
    return [pl.BlockSpec((tm, D_MODEL), lambda i: (jnp.minimum(i, head_tiles - 1), 0)),
            pl.BlockSpec((tm, D_MODEL), lambda i: (jnp.maximum(i - head_tiles, 0), 0))]


def _post(residual, att, ret, s5_pairs, mods, layer, permt, wglu, bglu, wo, g1, b1, w1, w2, g2, b2, skip_context):
    rows = att.shape[0]
    tm = ROW_TILE
    off = CTX_PAD // tm if skip_context else 0
    assert not (skip_context and len(residual) > 1)
    row_blk = lambda width: pl.BlockSpec((tm, width), lambda i: (i + off, 0))
    full = lambda arr: pl.BlockSpec(arr.shape, lambda i: (0,) * arr.ndim)
    hbm = lambda arr: pl.BlockSpec(memory_space=pl.ANY)
    vec = lambda v: v.reshape(1, -1).astype(F32)
    small = [(permt, full), (wglu, full), (vec(bglu), full), (wo, hbm), (vec(g1), full), (vec(b1), full),
             (w1, hbm), (w2, hbm), (vec(g2), full), (vec(b2), full)]
    res_specs = [row_blk(D_MODEL)] if skip_context else _residual_specs(residual)
    return pl.pallas_call(
        functools.partial(_post_kernel, split=len(residual) > 1, layer=layer),
        grid=(rows // tm - off,),
        in_specs=res_specs + [row_blk(ATT_WIDTH), row_blk(RET_WIDTH),
                              pl.BlockSpec((S5_PAIRS, tm // S5_T, S5_T * S5_PAIR_W), lambda i: (0, i + off, 0)),
                              pl.BlockSpec((None, None, N_ADA, D_MODEL),
                                           lambda i: (layer, jnp.where(i + off == 0, 1, 0), 0, 0))]
                 + [spec(arr) for arr, spec in small],
        out_specs=pl.BlockSpec((tm, D_MODEL), lambda i: (i, 0)),
        out_shape=jax.ShapeDtypeStruct((rows - off * tm, D_MODEL), F32),
        scratch_shapes=[pltpu.VMEM((ATT_WIDTH + RET_WIDTH + S5_WIDTH, D_MODEL), MXU_DTYPE),
                        pltpu.VMEM((D_MODEL, D_FF), MXU_DTYPE), pltpu.VMEM((D_FF, D_MODEL), MXU_DTYPE),
                        pltpu.VMEM((2, STAGE_BYTES // (4 * D_FF), D_FF), F32),
                        pltpu.VMEM((2, STAGE_BYTES // (4 * D_MODEL), D_MODEL), F32),
                        pltpu.SemaphoreType.DMA((2, 2))],
        compiler_params=_params(("arbitrary",)),
        name="post",
    )(*residual, att, ret, s5_pairs, mods, *[arr for arr, _ in small])


def kernel(x, c, ctx, c_ctx, w_ada, b_ada, w_in, att_sink, ret_decay_logit, s5_lambda_re, s5_lambda_im, s5_b_re,
           s5_b_im, s5_c_re, s5_c_im, s5_log_dt, s5_d, w_glu, b_glu, w_out, ln1_g, ln1_b, w_ff1, w_ff2, ln2_g,
           ln2_b):
    assert x.shape[0] == 1 and x.shape[2] == D_MODEL and ctx.shape[1] == CTX_LEN
    seq = x.shape[1]
    assert seq % ROW_TILE == 0
    residual = (jnp.pad(ctx[0], ((0, CTX_PAD - CTX_LEN), (0, 0))), x[0])
    cond = jnp.zeros((8, D_MODEL), F32).at[0].set(c[0]).at[1].set(c_ctx)
    mods = _modulation(cond, w_ada, b_ada).reshape(DEPTH, 8, N_ADA, D_MODEL)
    tabs = _rope_tables(seq)
    masks = _attention_masks()
    perm = _chunk_perm(ROW_TILE // POST_SPLIT, MXU_DTYPE)
    permt = perm.T
    col_scale = jnp.ones((IN_WIDTH,), F32).at[COL_AQ:COL_AK].set(HEAD_DIM ** -0.5 * LOG2E)
    col_scale = col_scale.at[COL_RQ:COL_RK].set(HEAD_DIM ** -0.5)
    s5w = jax.vmap(_s5_weights)(s5_lambda_re, s5_lambda_im, s5_b_re, s5_b_im, s5_c_re, s5_c_im, s5_log_dt, s5_d)
    log_gamma = jax.nn.log_sigmoid(ret_decay_logit.astype(F32))
    for l in range(DEPTH):
        proj, u_pairs = _in_proj(residual, mods, l, w_in, col_scale.reshape(1, IN_WIDTH), tabs, perm)
        att = _attention(proj, att_sink[l].astype(F32) * LOG2E, masks)
        ret = _retention(proj, log_gamma[l])
        s5 = _s5_mixer(u_pairs, s5w, l)
        stream = _post(residual, att, ret, s5, mods, l, permt, w_glu[l].astype(MXU_DTYPE), b_glu[l],
                       w_out, ln1_g[l], ln1_b[l], w_ff1, w_ff2, ln2_g[l], ln2_b[l],
                       skip_context=(l == DEPTH - 1))
        residual = (stream,)
    return stream[None]
```

```python
import functools
import math

import jax
import jax.numpy as jnp
from jax import lax
from jax.experimental import pallas as pl
from jax.experimental.pallas import tpu as pltpu

F32 = jnp.float32
MXU_DTYPE = jnp.bfloat16

D_MODEL = 1024
DEPTH = 4
GRID_W = 64
CTX_LEN = 256
CTX_PAD = 512
HEAD_DIM = 64
ATT_HEADS = 8
ATT_KV_HEADS = 2
ATT_BLOCK = 128
ATT_LOOKAHEAD = 2
ROPE_BASE = 10000.0
RET_HEADS = 4
RET_CHUNK = 256
S5_CH = 16
S5_GROUPS = 16
S5_STATE = 64
S5_T = 16
S5_PAIRS = S5_GROUPS // 2
S5_PAIR_W = 2 * S5_CH
S5_TILE = CTX_PAD // S5_T
SCAN_ROWS = 8
S5_STATE_W = 2 * 2 * S5_STATE
ATT_WIDTH = ATT_HEADS * HEAD_DIM
KV_WIDTH = ATT_KV_HEADS * HEAD_DIM
RET_WIDTH = RET_HEADS * HEAD_DIM
S5_WIDTH = S5_GROUPS * S5_CH
IN_WIDTH = ATT_WIDTH + 2 * KV_WIDTH + 4 * RET_WIDTH + S5_WIDTH
D_FF = 4 * D_MODEL
FF_CHUNK = 1024
N_ADA = 6
LN_EPS = 1e-5
GN_EPS = 1e-5
DEEPNORM_ALPHA = (2 * DEPTH) ** 0.25
ROW_TILE = 512
POST_SPLIT = 2
_NFF = D_FF // FF_CHUNK
POST_PROGRAM = (((0, 0), (0, 1), (1, 0), (0, 2), (1, 1)) + tuple((0, 2 + c) for c in range(1, _NFF))
                + ((1, 2), (0, 2 + _NFF)) + tuple((1, 2 + c) for c in range(1, _NFF)) + ((1, 2 + _NFF),))
NEG_BIG = -1e30
LOG2E = math.log2(math.e)
LANES = 128
VMEM_LIMIT = 56 * 1024 * 1024
STAGE_BYTES = 2 * 1024 * 1024

COL_AQ, COL_AK, COL_AV = 0, ATT_WIDTH, ATT_WIDTH + KV_WIDTH
COL_RQ = ATT_WIDTH + 2 * KV_WIDTH
COL_RK, COL_RV, COL_RG = COL_RQ + RET_WIDTH, COL_RQ + 2 * RET_WIDTH, COL_RQ + 3 * RET_WIDTH
COL_S5 = COL_RQ + 4 * RET_WIDTH


def _mm(a, b):
    return jnp.dot(a, b, preferred_element_type=F32)


def _mm_nt(a, b):
    return lax.dot_general(a, b, (((1,), (1,)), ((), ())), preferred_element_type=F32)


def _mm_tn(a, b):
    return lax.dot_general(a, b, (((0,), (0,)), ((), ())), preferred_element_type=F32)


def _params(sem):
    return pltpu.CompilerParams(dimension_semantics=sem, vmem_limit_bytes=VMEM_LIMIT)


def _mod_kernel(cond_ref, w_ref, b_ref, o_ref):
    c = cond_ref[...]
    s = c * jax.nn.sigmoid(c)
    o_ref[0] = jnp.dot(s, w_ref[0], preferred_element_type=F32, precision=lax.Precision.HIGHEST) + b_ref[0]


def _modulation(cond, w_ada, b_ada):
    tn = 1536
    n = N_ADA * D_MODEL
    return pl.pallas_call(
        _mod_kernel,
        grid=(DEPTH, n // tn),
        in_specs=[
            pl.BlockSpec((8, D_MODEL), lambda l, j: (0, 0)),
            pl.BlockSpec((1, D_MODEL, tn), lambda l, j: (l, 0, j)),
            pl.BlockSpec((1, 1, tn), lambda l, j: (l, 0, j)),
        ],
        out_specs=pl.BlockSpec((1, 8, tn), lambda l, j: (l, 0, j)),
        out_shape=jax.ShapeDtypeStruct((DEPTH, 8, n), F32),
        compiler_params=_params(("arbitrary", "arbitrary")),
        name="modulation",
    )(cond, w_ada, b_ada.reshape(DEPTH, 1, n))


def _lane_block_shuffle(src_rows, src_lane_blk, out_vreg):
    acc = None
    for q in range(LANES // S5_PAIR_W):
        piece = src_rows(out_vreg * (LANES // S5_PAIR_W) + q)
        shift = (S5_PAIR_W * (q - src_lane_blk)) % LANES
        if shift:
            piece = pltpu.roll(piece, shift, 1)
        if acc is None:
            acc = piece
        else:
            lane_blk = lax.broadcasted_iota(jnp.int32, piece.shape, 1) // S5_PAIR_W
            acc = jnp.where(lane_blk == q, piece, acc)
    return acc


def _in_proj_kernel(*refs, split, layer):
    nres = 2 if split else 1
    (mod_ref, w_hbm, scale_ref, ca_ref, sa_ref, cr_ref, sr_ref, perm_ref, o_ref, u_ref, w_ref, stage_ref,
     sem_ref) = refs[nres:]

    @pl.when(pl.program_id(0) == 0)
    def _stage_weights():
        _load_cast(w_hbm.at[layer], w_ref, stage_ref, sem_ref, col_scale=scale_ref[...])

    x = _residual_rows(refs[:nres])(slice(None))
    h = (x * (1.0 + mod_ref[1:2, :]) + mod_ref[0:1, :]).astype(w_ref.dtype)
    lane = lax.broadcasted_iota(jnp.int32, (x.shape[0], LANES), 1)
    first_att = lane % (HEAD_DIM // 2) < HEAD_DIM // 4
    first_ret = lane % HEAD_DIM < HEAD_DIM // 2

    def proj(c0, c1):
        return _mm(h, w_ref[:, c0:c1])

    def rope_store(p, c0, width, cos, sin, first, half):
        for b in range(width // LANES):
            blk = p[:, LANES * b:LANES * (b + 1)]
            rot = jnp.where(first, pltpu.roll(blk, LANES - half, 1), pltpu.roll(blk, half, 1))
            o_ref[:, c0 + LANES * b:c0 + LANES * (b + 1)] = (blk * cos + rot * sin).astype(o_ref.dtype)

    def plain_store(p, c0, width):
        o_ref[:, c0:c0 + width] = p.astype(o_ref.dtype)

    u = proj(COL_S5, IN_WIDTH).astype(w_ref.dtype)
    sub = perm_ref.shape[0]
    nchunk = sub // S5_T
    for part in range(x.shape[0] // sub):
        g = _mm(perm_ref[...], u[sub * part:sub * (part + 1)])
        for a in range(S5_PAIRS):
            vreg_col, lane_blk = divmod(a * S5_PAIR_W, LANES)
            lane_blk //= S5_PAIR_W
            for v in range(S5_T * S5_PAIR_W // LANES):
                slab = _lane_block_shuffle(
                    lambda j: g[nchunk * j:nchunk * (j + 1), vreg_col * LANES:(vreg_col + 1) * LANES], lane_blk, v)
                u_ref[a, nchunk * part:nchunk * (part + 1), LANES * v:LANES * (v + 1)] = slab.astype(u_ref.dtype)

    ca, sa, cr, sr = ca_ref[...], sa_ref[...], cr_ref[...], sr_ref[...]
    att_rope = (ca, sa, first_att, HEAD_DIM // 4)
    ret_rope = (cr, sr, first_ret, HEAD_DIM // 2)
    groups = [(COL_AQ, ATT_WIDTH, att_rope), (COL_AK, KV_WIDTH, att_rope), (COL_RQ, RET_WIDTH, ret_rope),
              (COL_RK, RET_WIDTH, ret_rope), (COL_AV, KV_WIDTH, None), (COL_RV, COL_S5 - COL_RV, None)]
    pending = None
    for group in groups + [None]:
        nxt = None if group is None else (proj(group[0], group[0] + group[1]),) + group
        if pending is not None:
            p, pc0, pwidth, prope = pending
            if prope is None:
                plain_store(p, pc0, pwidth)
            else:
                rope_store(p, pc0, pwidth, *prope)
        pending = nxt


def _chunk_perm(tile_rows, dtype):
    nchunk = tile_rows // S5_T
    r = jnp.arange(tile_rows)
    src = S5_T * (r % nchunk) + r // nchunk
    return (src[:, None] == jnp.arange(tile_rows)[None, :]).astype(dtype)


def _in_proj(residual, mods, layer, w_in, col_scale, tabs, perm):
    rows = tabs[0].shape[0]
    tm = ROW_TILE
    tab_spec = pl.BlockSpec((tm, LANES), lambda i: (i, 0))
    nch = rows // S5_T
    return pl.pallas_call(
        functools.partial(_in_proj_kernel, split=len(residual) > 1, layer=layer),
        grid=(rows // tm,),
        in_specs=_residual_specs(residual) + [
            pl.BlockSpec((None, None, N_ADA, D_MODEL), lambda i: (layer, jnp.where(i == 0, 1, 0), 0, 0)),
            pl.BlockSpec(memory_space=pl.ANY),
            pl.BlockSpec((1, IN_WIDTH), lambda i: (0, 0)),
            tab_spec, tab_spec, tab_spec, tab_spec,
            pl.BlockSpec(perm.shape, lambda i: (0, 0)),
        ],
        out_specs=[pl.BlockSpec((tm, COL_S5), lambda i: (i, 0)),
                   pl.BlockSpec((S5_PAIRS, tm // S5_T, S5_T * S5_PAIR_W), lambda i: (0, i, 0))],
        out_shape=[jax.ShapeDtypeStruct((rows, COL_S5), MXU_DTYPE),
                   jax.ShapeDtypeStruct((S5_PAIRS, nch, S5_T * S5_PAIR_W), MXU_DTYPE)],
        scratch_shapes=[pltpu.VMEM((D_MODEL, IN_WIDTH), MXU_DTYPE),
                        pltpu.VMEM((2, STAGE_BYTES // (4 * IN_WIDTH), IN_WIDTH), F32),
                        pltpu.SemaphoreType.DMA((2,))],
        compiler_params=_params(("arbitrary",)),
        name="in_proj",
    )(*residual, mods, w_in, col_scale, *tabs, perm)


def _rope_tables(seq):
    half_a = HEAD_DIM // 4
    half_r = HEAD_DIM // 2
    nrow = seq // GRID_W
    inv_a = ROPE_BASE ** (-jnp.arange(half_a, dtype=F32) / half_a)
    inv_r = ROPE_BASE ** (-jnp.arange(half_r, dtype=F32) / half_r)
    ang_r = jnp.arange(nrow, dtype=F32)[:, None] * inv_a[None, :]
    ang_c = jnp.arange(GRID_W, dtype=F32)[:, None] * inv_a[None, :]
    ang_t = jnp.arange(seq, dtype=F32)[:, None] * inv_r[None, :]
    hp = lax.Precision.HIGHEST
    lane = jnp.arange(LANES)
    within = lane % HEAD_DIM
    pick_a = (within % half_a)[None, :] == jnp.arange(half_a)[:, None]
    exp_row = (pick_a & (within < 2 * half_a)[None, :]).astype(F32)
    exp_col = (pick_a & (within >= 2 * half_a)[None, :]).astype(F32)
    exp_t = ((lane % half_r)[None, :] == jnp.arange(half_r)[:, None]).astype(F32)
    sign_a = jnp.where(within % (2 * half_a) < half_a, -1.0, 1.0).astype(F32)
    sign_r = jnp.where(within < half_r, -1.0, 1.0).astype(F32)

    def att_table(fn):
        by_row = jnp.dot(fn(ang_r), exp_row, precision=hp)
        by_col = jnp.dot(fn(ang_c), exp_col, precision=hp)
        return (by_row[:, None, :] + by_col[None, :, :]).reshape(seq, LANES)

    cos_a = att_table(jnp.cos)
    sin_a = att_table(jnp.sin) * sign_a
    cos_r = jnp.dot(jnp.cos(ang_t), exp_t, precision=hp)
    sin_r = jnp.dot(jnp.sin(ang_t), exp_t, precision=hp) * sign_r
    pad = lambda tab, ident: jnp.pad(tab, ((CTX_PAD, 0), (0, 0)), constant_values=ident)
    return pad(cos_a, 1.0), pad(sin_a, 0.0), pad(cos_r, 1.0), pad(sin_r, 0.0)


def _swap_halves(x):
    if x.dtype.itemsize == 4:
        return pltpu.roll(x, 64, 1)
    packed = pltpu.bitcast(x, jnp.uint32)
    return pltpu.bitcast(pltpu.roll(packed, 64, 1), x.dtype)


def _dup_heads(x):
    sw = _swap_halves(x)
    lo = lax.broadcasted_iota(jnp.int32, x.shape, 1) < HEAD_DIM
    return jnp.where(lo, x, sw), jnp.where(lo, sw, x)


def _attn_kernel(sink_ref, q_ref, km_ref, kp_ref, kn_ref, vm_ref, vp_ref, vn_ref, kc_ref, vc_ref, mask_ref,
                 o_ref, k2_ref, v2_ref, kc2_ref, vc2_ref):
    i = pl.program_id(0)
    last_blk = pl.num_programs(0) * (ROW_TILE // ATT_BLOCK) - 1
    blk = ATT_BLOCK
    def spread(src, ones_upper):
        x = src[...]
        a, b = _dup_heads(x)
        if ones_upper:
            upper = lax.broadcasted_iota(jnp.int32, x.shape, 1) >= HEAD_DIM
            a = jnp.where(upper, jnp.ones_like(a), a)
            b = jnp.where(upper, jnp.ones_like(b), b)
        return a, b

    for dst, parts, is_v in ((k2_ref, (kp_ref, km_ref, kn_ref), False), (v2_ref, (vp_ref, vm_ref, vn_ref), True)):
        row = 0
        for part in parts:
            a, b = spread(part, is_v)
            n = part.shape[0]
            dst[0, row:row + n, :] = a
            dst[1, row:row + n, :] = b
            row += n
    for dst, src, is_v in ((kc2_ref, kc_ref, False), (vc2_ref, vc_ref, True)):
        a, b = spread(src, is_v)
        dst[0] = a
        dst[1] = b

    lo = lax.broadcasted_iota(jnp.int32, (blk, LANES), 1) < HEAD_DIM
    group = ATT_HEADS // ATT_KV_HEADS

    nloc = 3 * blk

    def scores(j, kv):
        r0 = j * blk
        qt = q_ref[r0:r0 + blk, group * HEAD_DIM * kv:group * HEAD_DIM * (kv + 1)]
        parts = []
        for g in range(group):
            qc = qt[:, LANES * (g // 2):LANES * (g // 2 + 1)]
            keep = lo if g % 2 == 0 else jnp.logical_not(lo)
            parts.append(jnp.where(keep, qc, jnp.zeros_like(qc)))
        qs = jnp.concatenate(parts, axis=0)
        return _mm_nt(qs, k2_ref[kv, r0:r0 + nloc, :]), _mm_nt(qs, kc2_ref[kv])

    def finish(j, kv, s_loc, s_ctx):
        r0 = j * blk
        gblk = i * (ROW_TILE // blk) + j
        sel = jnp.where(i == 0, 3, jnp.where(gblk == CTX_PAD // blk, 0, jnp.where(gblk == last_blk, 2, 1)))
        bias = mask_ref[sel]
        probs, sink_w = [], []
        for g in range(group):
            s = jnp.concatenate([s_loc[blk * g:blk * (g + 1)] + bias, s_ctx[blk * g:blk * (g + 1)]], axis=1)
            sk = sink_ref[group * kv + g]
            m = jnp.maximum(jnp.max(s, axis=-1, keepdims=True), sk)
            probs.append(jnp.exp2(s - m).astype(o_ref.dtype))
            sink_w.append(jnp.exp2(sk - m))
        p = jnp.concatenate(probs, axis=0)
        o = _mm(p[:, :nloc], v2_ref[kv, r0:r0 + nloc, :]) + _mm(p[:, nloc:], vc2_ref[kv])
        for half in range(group // 2):
            even, odd = 2 * half, 2 * half + 1
            oe = o[blk * even:blk * (even + 1)]
            oo = o[blk * odd:blk * (odd + 1)]
            y_even = oe * (1.0 / (pltpu.roll(oe, HEAD_DIM, 1) + sink_w[even]))
            y_odd = pltpu.roll(oo, HEAD_DIM, 1) * (1.0 / (oo + sink_w[odd]))
            c0 = group * HEAD_DIM * kv + LANES * half
            o_ref[r0:r0 + blk, c0:c0 + LANES] = jnp.where(lo, y_even, y_odd).astype(o_ref.dtype)

    items = [(j, kv) for j in range(ROW_TILE // blk) for kv in range(ATT_KV_HEADS)]
    pending = {}
    for t in range(len(items) + ATT_LOOKAHEAD):
        if t < len(items):
            pending[t] = scores(*items[t])
        if t >= ATT_LOOKAHEAD:
            finish(*items[t - ATT_LOOKAHEAD], *pending.pop(t - ATT_LOOKAHEAD))


def _attention_masks():
    qi = jnp.arange(ATT_BLOCK)[:, None]
    kj = jnp.arange(3 * ATT_BLOCK)[None, :]
    band = jnp.abs(kj - ATT_BLOCK - qi) <= ATT_BLOCK
    first = band & (kj >= ATT_BLOCK)
    last = band & (kj < 2 * ATT_BLOCK)
    none = jnp.zeros_like(band)
    masks = jnp.stack([first, band, last, none])
    return jnp.where(masks, 0.0, NEG_BIG).astype(F32)


def _attention(proj, sink, masks):
    rows = proj.shape[0]
    tm, blk = ROW_TILE, ATT_BLOCK
    per = tm // blk
    nblk = rows // blk
    ck, cv = COL_AK // KV_WIDTH, COL_AV // KV_WIDTH
    dt = proj.dtype
    return pl.pallas_call(
        _attn_kernel,
        grid=(rows // tm,),
        in_specs=[
            pl.BlockSpec(memory_space=pltpu.SMEM),
            pl.BlockSpec((tm, ATT_WIDTH), lambda i: (i, 0)),
            pl.BlockSpec((tm, KV_WIDTH), lambda i: (i, ck)),
            pl.BlockSpec((blk, KV_WIDTH), lambda i: (jnp.maximum(i * per - 1, 0), ck)),
            pl.BlockSpec((blk, KV_WIDTH), lambda i: (jnp.minimum((i + 1) * per, nblk - 1), ck)),
            pl.BlockSpec((tm, KV_WIDTH), lambda i: (i, cv)),
            pl.BlockSpec((blk, KV_WIDTH), lambda i: (jnp.maximum(i * per - 1, 0), cv)),
            pl.BlockSpec((blk, KV_WIDTH), lambda i: (jnp.minimum((i + 1) * per, nblk - 1), cv)),
            pl.BlockSpec((CTX_LEN, KV_WIDTH), lambda i: (0, ck)),
            pl.BlockSpec((CTX_LEN, KV_WIDTH), lambda i: (0, cv)),
            pl.BlockSpec((4, blk, 3 * blk), lambda i: (0, 0, 0)),
        ],
        out_specs=pl.BlockSpec((tm, ATT_WIDTH), lambda i: (i, 0)),
        out_shape=jax.ShapeDtypeStruct((rows, ATT_WIDTH), dt),
        scratch_shapes=[
            pltpu.VMEM((2, tm + 2 * blk, KV_WIDTH), dt),
            pltpu.VMEM((2, tm + 2 * blk, KV_WIDTH), dt),
            pltpu.VMEM((2, CTX_LEN, KV_WIDTH), dt),
            pltpu.VMEM((2, CTX_LEN, KV_WIDTH), dt),
        ],
        compiler_params=_params(("arbitrary",)),
        name="attention",
    )(sink, proj, proj, proj, proj, proj, proj, proj, proj, proj, masks)


def _ret_kernel(*refs, sup, ntile):
    lg_ref, q_ref = refs[:2]
    k_refs, v_refs = refs[2:2 + sup], refs[2 + sup:2 + 2 * sup]
    g_ref, o_ref, sb_ref, s_ref, dm_ref, tab_ref, gbd_ref = refs[2 + 2 * sup:]
    k_ref, v_ref = k_refs[0], v_refs[0]
    step = pl.program_id(0)
    nsup = (ntile - 1) // sup
    ph = jnp.where(step <= nsup, 0, 1)
    t = jnp.where(step <= nsup, step, step - (nsup + 1))
    c = RET_CHUNK
    w = RET_WIDTH
    per = q_ref.shape[0] // c
    mdt = sb_ref.dtype
    rows = lambda ci: slice(c * ci, c * (ci + 1))

    def lane_vec(direction, shape, axis):
        head = lax.broadcasted_iota(jnp.int32, shape, axis) // HEAD_DIM
        out = jnp.full(shape, lg_ref[direction, RET_HEADS - 1], F32)
        for h in range(RET_HEADS - 2, -1, -1):
            out = jnp.where(head == h, lg_ref[direction, h], out)
        return out

    @pl.when(jnp.logical_and(ph == 0, t == 0))
    def _init_tables():
        diff = (lax.broadcasted_iota(jnp.int32, (c, c), 0) - lax.broadcasted_iota(jnp.int32, (c, c), 1)).astype(F32)
        for h in range(RET_HEADS):
            dm_ref[h] = jnp.exp(jnp.where(diff >= 0, diff * lg_ref[0, h], -diff * lg_ref[1, h]))
        pos = lax.broadcasted_iota(jnp.int32, (c, w), 0).astype(F32)
        lgf = lane_vec(0, (c, w), 1)
        lgb = lane_vec(1, (c, w), 1)
        tab_ref[0] = jnp.exp((c - 1.0 - pos) * lgf)
        tab_ref[1] = jnp.exp((pos + 1.0) * lgf)
        tab_ref[2] = jnp.exp(pos * lgb)
        tab_ref[3] = jnp.exp((c - pos) * lgb)
        same = (lax.broadcasted_iota(jnp.int32, (w, w), 0) // HEAD_DIM
                == lax.broadcasted_iota(jnp.int32, (w, w), 1) // HEAD_DIM)
        bd = jnp.where(same, 1.0, 0.0)
        gbd_ref[0] = bd * jnp.exp(c * lane_vec(0, (w, w), 0))
        gbd_ref[1] = bd * jnp.exp(c * lane_vec(1, (w, w), 0))
        gbd_ref[2] = bd

    @pl.when(t == 0)
    def _reset_state():
        s_ref[...] = jnp.zeros_like(s_ref)

    def state_update(direction, key_tab, ci, kr=k_ref, vr=v_ref):
        kw = (kr[rows(ci), :].astype(F32) * tab_ref[key_tab]).astype(mdt)
        u = _mm_tn(kw, vr[rows(ci), :])
        s_ref[...] = gbd_ref[direction] * s_ref[...] + gbd_ref[2] * u

    @pl.when(jnp.logical_and(ph == 0, t == 0))
    def _backward_context():
        sb_ref[0] = s_ref[...].astype(mdt)
        state_update(1, 2, 0)

    @pl.when(jnp.logical_and(ph == 0, t > 0))
    def _backward_latent():
        base = 1 + per * sup * (nsup - t)
        for qi in range(sup - 1, -1, -1):
            for ci in range(per - 1, -1, -1):
                sb_ref[base + per * qi + ci] = s_ref[...].astype(mdt)
                state_update(1, 2, ci, k_refs[qi], v_refs[qi])

    head = lax.broadcasted_iota(jnp.int32, (c, w), 1) // HEAD_DIM

    def scores(ci):
        q = q_ref[rows(ci), :]
        qs = jnp.concatenate([jnp.where(head == h, q, jnp.zeros_like(q)) for h in range(RET_HEADS)], axis=0)
        return _mm_nt(qs, k_ref[rows(ci), :])

    def intra(ci, sc):
        scd = jnp.concatenate([sc[c * h:c * (h + 1)] * dm_ref[h] for h in range(RET_HEADS)], axis=0).astype(mdt)
        oi = _mm(scd, v_ref[rows(ci), :])
        o = jnp.where(head == 0, oi[0:c], 0.0)
        for h in range(1, RET_HEADS):
            o = o + jnp.where(head == h, oi[c * h:c * (h + 1)], 0.0)
        return o

    def cross(ci, idx):
        qf = q_ref[rows(ci), :].astype(F32)
        return (_mm((qf * tab_ref[1]).astype(mdt), s_ref[...].astype(mdt))
                + _mm((qf * tab_ref[3]).astype(mdt), sb_ref[idx]))

    def finish(ci, o):
        avg = (gbd_ref[2] * (1.0 / HEAD_DIM)).astype(mdt)
        o_hi = o.astype(mdt)
        d = o - (_mm(o_hi, avg) + _mm((o - o_hi.astype(F32)).astype(mdt), avg))
        var = _mm((d * d).astype(mdt), avg)
        gate = g_ref[rows(ci), :].astype(F32)
        y = d * lax.rsqrt(var + GN_EPS) * (gate * jax.nn.sigmoid(gate))
        o_ref[rows(ci), :] = y.astype(o_ref.dtype)

    @pl.when(jnp.logical_and(ph == 1, t == 0))
    def _forward_context():
        o = intra(0, scores(0)) + cross(0, 0)
        state_update(0, 0, 0)
        finish(0, o)
        for ci in range(1, per):
            o_ref[rows(ci), :] = jnp.zeros((c, w), o_ref.dtype)

    @pl.when(jnp.logical_and(ph == 1, t > 0))
    def _forward_latent():
        base = 1 + per * (t - 1)
        sc = [scores(ci) for ci in range(per)]
        outs = [intra(ci, sc[ci]) for ci in range(per)]
        for ci in range(per):
            outs[ci] = outs[ci] + cross(ci, base + ci)
            state_update(0, 0, ci)
        for ci in range(per):
            finish(ci, outs[ci])


def _retention(proj, log_gamma):
    rows = proj.shape[0]
    c = RET_CHUNK
    tm = ROW_TILE
    ntile = rows // tm
    nchunk = 1 + (rows - CTX_PAD) // c
    dt = proj.dtype
    cq, ckk, cvv, cg = (COL_RQ // RET_WIDTH, COL_RK // RET_WIDTH, COL_RV // RET_WIDTH, COL_RG // RET_WIDTH)

    nlat = ntile - 1
    sup = next(s for s in (4, 2, 1) if nlat % s == 0)
    nsup = nlat // sup
    nback = 1 + nsup

    def fw_blk(step):
        return jnp.maximum(step - nback, 0)

    def kv_spec(col, qi):
        def index(step):
            back = jnp.where(step == 0, 0, 1 + sup * (nsup - jnp.minimum(step, nsup)) + qi)
            fwd = step - nback if qi == 0 else 1 + qi
            return jnp.where(step < nback, back, fwd), col
        return pl.BlockSpec((tm, RET_WIDTH), index)

    return pl.pallas_call(
        functools.partial(_ret_kernel, sup=sup, ntile=ntile),
        grid=(nback + ntile,),
        in_specs=[pl.BlockSpec(memory_space=pltpu.SMEM),
                  pl.BlockSpec((tm, RET_WIDTH), lambda step: (fw_blk(step), cq))]
                 + [kv_spec(ckk, qi) for qi in range(sup)] + [kv_spec(cvv, qi) for qi in range(sup)]
                 + [pl.BlockSpec((tm, RET_WIDTH), lambda step: (fw_blk(step), cg))],
        out_specs=pl.BlockSpec((tm, RET_WIDTH), lambda step: (fw_blk(step), 0)),
        out_shape=jax.ShapeDtypeStruct((rows, RET_WIDTH), dt),
        scratch_shapes=[
            pltpu.VMEM((nchunk, RET_WIDTH, RET_WIDTH), dt),
            pltpu.VMEM((RET_WIDTH, RET_WIDTH), F32),
            pltpu.VMEM((RET_HEADS, c, c), F32),
            pltpu.VMEM((4, c, RET_WIDTH), F32),
            pltpu.VMEM((3, RET_WIDTH, RET_WIDTH), F32),
        ],
        compiler_params=_params(("arbitrary",)),
        name="retention",
    )(log_gamma, proj, *([proj] * (2 * sup)), proj)


def _s5_weights(lam_re, lam_im, b_re, b_im, c_re, c_im, log_dt, d_skip):
    tt, g, n, p, a = S5_T, S5_GROUPS, S5_STATE, S5_CH, S5_PAIRS
    lam = lax.complex(lam_re.astype(F32), lam_im.astype(F32))
    dtv = jnp.exp(log_dt.astype(F32))[..., None]
    lam_bar = jnp.exp(lam * dtv)
    bbar = ((lam_bar - 1.0) / lam)[..., None] * lax.complex(b_re.astype(F32), b_im.astype(F32))
    cmat = lax.complex(c_re.astype(F32), c_im.astype(F32))
    pw = [jnp.ones_like(lam_bar)]
    for _ in range(tt):
        pw.append(pw[-1] * lam_bar)
    pw = jnp.stack(pw, axis=1)
    eye2 = jnp.eye(2, dtype=F32)
    ri = lambda z, axis: jnp.stack([jnp.real(z), jnp.imag(z)], axis=axis)

    pw_l = pw.reshape(2, tt + 1, a, 2 * n)
    bbt = jnp.einsum('dahpn,gh->dagphn', jnp.swapaxes(bbar, -1, -2).reshape(2, a, 2, p, n), eye2)
    bbt = bbt.reshape(2, a, 2 * p, 2 * n)
    cct = jnp.einsum('dahpn,gh->dagphn', cmat.reshape(2, a, 2, p, n), eye2).reshape(2, a, 2 * p, 2 * n)
    pw_k = ri(pw_l, 1).transpose(3, 0, 1, 2, 4)
    b_k = ri(bbt, 2).transpose(1, 0, 2, 3, 4)
    c_k = ri(cct, 2).transpose(1, 0, 2, 3, 4)
    decay = [pw_l[:, tt]]
    for _ in range(SCAN_ROWS - 1):
        decay.append(decay[-1] * decay[0])
    decay = jnp.stack(decay, axis=0)
    rows8 = lambda z: jnp.broadcast_to(z[None], (SCAN_ROWS,) + z.shape)
    carry_w = jnp.stack([decay[:, 0], decay[::-1, 1]], axis=1)
    scan_tab = jnp.stack([rows8(decay[0]), rows8(decay[1]), rows8(decay[3]), carry_w], axis=0)
    scan_tab = ri(scan_tab, 0).transpose(4, 3, 1, 0, 2, 5)
    skip = jnp.tile(d_skip.astype(F32).reshape(a, 1, 2 * p), (1, tt, 1)).reshape(a, 1, tt * 2 * p)
    return pw_k, b_k, c_k, scan_tab, skip


def _pair_spec(layer, *shape):
    return pl.BlockSpec((None, None) + shape, lambda i: (layer, i) + (0,) * len(shape))


def _s5_drive_kernel(u_ref, pw_ref, b_ref, o_ref, w_ref):
    rows, half = S5_PAIR_W, LANES
    for d in range(2):
        br, bi = b_ref[d, 0], b_ref[d, 1]
        for j in range(S5_T):
            e = S5_T - 1 - j if d == 0 else j
            pr, pi = pw_ref[d, 0, e:e + 1, :], pw_ref[d, 1, e:e + 1, :]
            w_ref[rows * j:rows * (j + 1), 2 * half * d:2 * half * d + half] = (pr * br - pi * bi).astype(w_ref.dtype)
            w_ref[rows * j:rows * (j + 1), 2 * half * d + half:2 * half * (d + 1)] = (
                pr * bi + pi * br).astype(w_ref.dtype)
    o_ref[0] = _mm(u_ref[0], w_ref[...])


def _s5_drive(u_pairs, pw_k, b_k, layer):
    a, nch, wd = u_pairs.shape
    return pl.pallas_call(
        _s5_drive_kernel,
        grid=(a,),
        in_specs=[pl.BlockSpec((1, nch, wd), lambda i: (i, 0, 0)),
                  _pair_spec(layer, *pw_k.shape[2:]), _pair_spec(layer, *b_k.shape[2:])],
        out_specs=pl.BlockSpec((1, nch, 2 * S5_STATE_W), lambda i: (i, 0, 0)),
        out_shape=jax.ShapeDtypeStruct((a, nch, 2 * S5_STATE_W), F32),
        scratch_shapes=[pltpu.VMEM((wd, 2 * S5_STATE_W), u_pairs.dtype)],
        compiler_params=_params(("arbitrary",)),
        name="s5_drive",
    )(u_pairs, pw_k, b_k)


def _s5_scan_kernel(ef_ref, eb_ref, tab_ref, sf_ref, sb_ref, st_ref):
    t = pl.program_id(0)
    hw = LANES
    sub = SCAN_ROWS
    npair = tab_ref.shape[0]
    row = lax.broadcasted_iota(jnp.int32, (sub, hw), 0)

    @pl.when(t == 0)
    def _reset():
        st_ref[...] = jnp.zeros_like(st_ref)

    def shift(x, k, reverse):
        if reverse:
            return jnp.where(row < sub - k, pltpu.roll(x, sub - k, 0), 0.0)
        return jnp.where(row >= k, pltpu.roll(x, k, 0), 0.0)

    def scan_group(e_ref, a, d, r0, cr, ci):
        reverse = d == 1
        xr = e_ref[a, pl.ds(r0, sub), 0:hw]
        xi = e_ref[a, pl.ds(r0, sub), hw:2 * hw]
        for step, k in enumerate((1, 2, 4)):
            ar, ai = tab_ref[a, d, step, 0], tab_ref[a, d, step, 1]
            sr, si = shift(xr, k, reverse), shift(xi, k, reverse)
            xr, xi = xr + ar * sr - ai * si, xi + ar * si + ai * sr
        wr, wi = tab_ref[a, d, 3, 0], tab_ref[a, d, 3, 1]
        fr = xr + wr * cr - wi * ci
        fi = xi + wr * ci + wi * cr
        edge = sub - 1 if reverse else 0
        before = (jnp.where(row == edge, cr, shift(fr, 1, reverse)), jnp.where(row == edge, ci, shift(fi, 1, reverse)))
        last = 0 if reverse else sub - 1
        after = (jnp.broadcast_to(fr[last:last + 1], (sub, hw)), jnp.broadcast_to(fi[last:last + 1], (sub, hw)))
        return before, after

    def store(o_ref, a, r0, lower, upper):
        for part in range(2):
            val = jnp.concatenate([lower[part], upper[part]], axis=0)
            o_ref[a, pl.ds(r0, 2 * sub), part * hw:(part + 1) * hw] = val.astype(o_ref.dtype)

    def run(nrows):
        span = 2 * sub
        nspan = nrows // span

        def body(gi, carry):
            r0 = pl.multiple_of(gi * span, span)
            rb0 = pl.multiple_of((nspan - 1 - gi) * span, span)
            new = []
            for a in range(npair):
                fr, fi, br, bi = carry[a]
                f_lo, (fr, fi) = scan_group(ef_ref, a, 0, r0, fr, fi)
                f_hi, (fr, fi) = scan_group(ef_ref, a, 0, r0 + sub, fr, fi)
                store(sf_ref, a, r0, f_lo, f_hi)
                b_hi, (br, bi) = scan_group(eb_ref, a, 1, rb0 + sub, br, bi)
                b_lo, (br, bi) = scan_group(eb_ref, a, 1, rb0, br, bi)
                store(sb_ref, a, rb0, b_lo, b_hi)
                new.append((fr, fi, br, bi))
            return tuple(new)

        init = tuple(tuple(st_ref[a, k] for k in range(4)) for a in range(npair))
        final = lax.fori_loop(0, nspan, body, init)
        for a in range(npair):
            for k in range(4):
                st_ref[a, k] = final[a][k]

    @pl.when(t == 0)
    def _context():
        sf_ref[...] = jnp.zeros_like(sf_ref)
        sb_ref[...] = jnp.zeros_like(sb_ref)
        run(CTX_LEN // S5_T)

    @pl.when(t > 0)
    def _latent():
        run(S5_TILE)


def _s5_scan(drive, scan_tab, layer):
    a, nch, wd2 = drive.shape
    wd = wd2 // 2
    nt = nch // S5_TILE

    def bwd(t):
        return jnp.where(t == 0, 0, nt - t)

    return pl.pallas_call(
        _s5_scan_kernel,
        grid=(nt,),
        in_specs=[pl.BlockSpec((a, S5_TILE, wd), lambda t: (0, t, 0)),
                  pl.BlockSpec((a, S5_TILE, wd), lambda t: (0, bwd(t), 1)),
                  pl.BlockSpec((None,) + scan_tab.shape[1:], lambda t: (layer,) + (0,) * (scan_tab.ndim - 1))],
        out_specs=[pl.BlockSpec((a, S5_TILE, wd), lambda t: (0, t, 0)),
                   pl.BlockSpec((a, S5_TILE, wd), lambda t: (0, bwd(t), 0))],
        out_shape=[jax.ShapeDtypeStruct((a, nch, wd), MXU_DTYPE), jax.ShapeDtypeStruct((a, nch, wd), MXU_DTYPE)],
        scratch_shapes=[pltpu.VMEM((a, 4, SCAN_ROWS, LANES), F32)],
        compiler_params=_params(("arbitrary",)),
        name="s5_scan",
    )(drive, drive, scan_tab)


def _lane_window(x, start, width):
    cols = []
    for v in range(width // LANES):
        k0, off = divmod(start + LANES * v, LANES)
        lo = x[:, LANES * k0:LANES * (k0 + 1)]
        if off:
            hi = x[:, LANES * (k0 + 1):LANES * (k0 + 2)]
            lane = lax.broadcasted_iota(jnp.int32, lo.shape, 1)
            lo = jnp.where(lane < LANES - off, pltpu.roll(lo, LANES - off, 1), pltpu.roll(hi, LANES - off, 1))
        cols.append(lo)
    return jnp.concatenate(cols, axis=1)


def _s5_read_kernel(u_ref, sf_ref, sb_ref, pw_ref, b_ref, c_ref, skip_ref, o_ref, wt_ref, wi_ref, lag_ref):
    u = u_ref[0]
    mdt = u.dtype
    rows, half = S5_PAIR_W, LANES
    for d in range(2):
        cr, ci = c_ref[d, 0], c_ref[d, 1]
        for i in range(S5_T):
            e = i + 1 if d == 0 else S5_T - i
            pr, pi = pw_ref[d, 0, e:e + 1, :], pw_ref[d, 1, e:e + 1, :]
            wt_ref[d, rows * i:rows * (i + 1), 0:half] = (pr * cr - pi * ci).astype(mdt)
            wt_ref[d, rows * i:rows * (i + 1), half:2 * half] = (-(pr * ci + pi * cr)).astype(mdt)
    nlag = 2 * S5_T - 1
    ldt = lag_ref.dtype
    zero = jnp.zeros((rows, half), ldt)
    for l in range(nlag + 1):
        lag = l - (S5_T - 1)
        for d, active in ((1, lag <= 0), (0, 0 <= lag < S5_T)):
            col = 2 * half * (1 - d)
            if active:
                cr, ci = c_ref[d, 0], c_ref[d, 1]
                pr, pi = pw_ref[d, 0, abs(lag):abs(lag) + 1, :], pw_ref[d, 1, abs(lag):abs(lag) + 1, :]
                lag_ref[rows * l:rows * (l + 1), col:col + half] = (pr * cr - pi * ci).astype(ldt)
                lag_ref[rows * l:rows * (l + 1), col + half:col + 2 * half] = (pr * ci + pi * cr).astype(ldt)
            else:
                lag_ref[rows * l:rows * (l + 1), col:col + half] = zero
                lag_ref[rows * l:rows * (l + 1), col + half:col + 2 * half] = zero
    lhs = jnp.concatenate([b_ref[1, 0], -b_ref[1, 1], b_ref[0, 0], -b_ref[0, 1]], axis=1).astype(ldt)
    kall = _mm_nt(lhs, lag_ref[...])
    for j in range(S5_T):
        wi_ref[rows * j:rows * (j + 1), :] = _lane_window(kall, rows * (S5_T - 1 - j), S5_T * rows).astype(mdt)
    y = _mm(u, wi_ref[...])
    y = y + _mm_nt(sf_ref[0].astype(mdt), wt_ref[0])
    y = y + _mm_nt(sb_ref[0].astype(mdt), wt_ref[1])
    o_ref[0] = y + u.astype(F32) * skip_ref[...]


def _s5_read(u_pairs, sf, sb, pw_k, b_k, c_k, skip, layer):
    a, nch, wd = u_pairs.shape
    blk = lambda *shape: pl.BlockSpec((1,) + shape, lambda i: (i, 0, 0))
    return pl.pallas_call(
        _s5_read_kernel,
        grid=(a,),
        in_specs=[blk(nch, wd), blk(nch, S5_STATE_W), blk(nch, S5_STATE_W), _pair_spec(layer, *pw_k.shape[2:]),
                  _pair_spec(layer, *b_k.shape[2:]), _pair_spec(layer, *c_k.shape[2:]),
                  _pair_spec(layer, *skip.shape[2:])],
        out_specs=blk(nch, wd),
        out_shape=jax.ShapeDtypeStruct((a, nch, wd), F32),
        scratch_shapes=[pltpu.VMEM((2, wd, S5_STATE_W), u_pairs.dtype), pltpu.VMEM((wd, wd), u_pairs.dtype),
                        pltpu.VMEM((2 * wd, 2 * S5_STATE_W), u_pairs.dtype)],
        compiler_params=_params(("arbitrary",)),
        name="s5_read",
    )(u_pairs, sf, sb, pw_k, b_k, c_k, skip)


def _s5_mixer(u_pairs, weights, layer):
    pw_k, b_k, c_k, scan_tab, skip = weights
    drive = _s5_drive(u_pairs, pw_k, b_k, layer)
    sf, sb = _s5_scan(drive, scan_tab, layer)
    return _s5_read(u_pairs, sf, sb, pw_k, b_k, c_k, skip, layer)


def _layer_norm(x, g, b):
    mu = jnp.mean(x, axis=-1, keepdims=True)
    d = x - mu
    var = jnp.mean(d * d, axis=-1, keepdims=True)
    return d * lax.rsqrt(var + LN_EPS) * g + b


def _residual_rows(refs):
    if len(refs) == 1:
        return lambda rows: refs[0][rows, :]
    head_ref, body_ref = refs
    is_head = pl.program_id(0) == 0
    return lambda rows: jnp.where(is_head, head_ref[rows, :], body_ref[rows, :])


def _load_cast(src_hbm, dst_ref, stage_ref, sems, col_scale=None):
    chunk = stage_ref.shape[1]
    nchunk = src_hbm.shape[0] // chunk

    def copy(k):
        slot = k % 2
        return pltpu.make_async_copy(src_hbm.at[pl.ds(k * chunk, chunk), :], stage_ref.at[slot], sems.at[slot])

    copy(0).start()
    for k in range(nchunk):
        if k + 1 < nchunk:
            copy(k + 1).start()
        copy(k).wait()
        vals = stage_ref[k % 2]
        if col_scale is not None:
            vals = vals * col_scale
        dst_ref[k * chunk:(k + 1) * chunk, :] = vals.astype(dst_ref.dtype)


def _post_kernel(*refs, split, layer):
    nres = 2 if split else 1
    load_x = _residual_rows(refs[:nres])
    (att_ref, ret_ref, s5_ref, mod_ref, permt_ref, wglu_ref, bglu_ref, wo_hbm, g1_ref, b1_ref, w1_hbm, w2_hbm,
     g2_ref, b2_ref, o_ref, wo_ref, w1_ref, w2_ref, stage1_ref, stage2_ref, sem_ref) = refs[nres:]
    woa_ref = wo_ref.at[0:ATT_WIDTH]
    wor_ref = wo_ref.at[ATT_WIDTH:ATT_WIDTH + RET_WIDTH]
    wos_ref = wo_ref.at[ATT_WIDTH + RET_WIDTH:ATT_WIDTH + RET_WIDTH + S5_WIDTH]
    mdt = w1_ref.dtype

    @pl.when(pl.program_id(0) == 0)
    def _stage_weights():
        _load_cast(wo_hbm.at[layer], wo_ref, stage2_ref, sem_ref.at[1])
        _load_cast(w1_hbm.at[layer], w1_ref, stage1_ref, sem_ref.at[0])
        _load_cast(w2_hbm.at[layer], w2_ref, stage2_ref, sem_ref.at[1])
    sub = o_ref.shape[0] // POST_SPLIT
    csub = sub // S5_T
    nff = D_FF // FF_CHUNK

    def mix(part):
        rows = slice(sub * part, sub * (part + 1))
        zrows = []
        for i in range(S5_T):
            src_vreg, src_blk = divmod(i * S5_PAIR_W, LANES)
            src_blk //= S5_PAIR_W
            cols = [_lane_block_shuffle(
                lambda a: s5_ref[a, csub * part:csub * (part + 1), src_vreg * LANES:(src_vreg + 1) * LANES],
                src_blk, w) for w in range(S5_WIDTH // LANES)]
            zrows.append(jnp.concatenate(cols, axis=1))
        hs = jax.nn.gelu(jnp.concatenate(zrows, axis=0))
        gate = jax.nn.sigmoid(_mm(hs.astype(mdt), wglu_ref[...]) + bglu_ref[...])
        s5 = _mm(permt_ref[...], (hs * gate).astype(mdt)).astype(mdt)
        return _mm(att_ref[rows, :], woa_ref[...]) + _mm(ret_ref[rows, :], wor_ref[...]) + _mm(s5, wos_ref[...])

    def norm1(part, ox):
        rows = slice(sub * part, sub * (part + 1))
        x1 = _layer_norm(DEEPNORM_ALPHA * load_x(rows) + mod_ref[2:3, :] * ox, g1_ref[...], b1_ref[...])
        return x1, (x1 * (1.0 + mod_ref[4:5, :]) + mod_ref[3:4, :]).astype(mdt)

    def ff(h, c):
        a = _mm(h, w1_ref[:, FF_CHUNK * c:FF_CHUNK * (c + 1)])
        a = jnp.square(jnp.maximum(a, 0.0)).astype(mdt)
        return _mm(a, w2_ref[FF_CHUNK * c:FF_CHUNK * (c + 1), :])

    def norm2(part, x1, acc):
        rows = slice(sub * part, sub * (part + 1))
        o_ref[rows, :] = _layer_norm(DEEPNORM_ALPHA * x1 + mod_ref[5:6, :] * acc, g2_ref[...], b2_ref[...])

    nstage = nff + 3
    state = [dict() for _ in range(POST_SPLIT)]
    for part, stage in POST_PROGRAM:
        st = state[part]
        if stage == 0:
            st["ox"] = mix(part)
        elif stage == 1:
            st["x1"], st["h"] = norm1(part, st.pop("ox"))
        elif stage < nstage - 1:
            term = ff(st["h"], stage - 2)
            st["acc"] = term if stage == 2 else st["acc"] + term
        else:
            norm2(part, st["x1"], st["acc"])


def _residual_specs(residual):
    tm = ROW_TILE
    if len(residual) == 1:
        return [pl.BlockSpec((tm, D_MODEL), lambda i: (i, 0))]
    head_tiles = CTX_PAD // tm
    return [pl.BlockSpec((tm, D_MODEL), lambda i: (jnp.minimum(i, head_tiles - 1), 0)),
            pl.BlockSpec((tm, D_MODEL), lambda i: (jnp.maximum(i - head_tiles, 0), 0))]


def _post(residual, att, ret, s5_pairs, mods, layer, permt, wglu, bglu, wo, g1, b1, w1, w2, g2, b2, skip_context):
    rows = att.shape[0]
    tm = ROW_TILE
    off = CTX_PAD // tm if skip_context else 0
    assert not (skip_context and len(residual) > 1)
    row_blk = lambda width: pl.BlockSpec((tm, width), lambda i: (i + off, 0))
    full = lambda arr: pl.BlockSpec(arr.shape, lambda i: (0,) * arr.ndim)
    hbm = lambda arr: pl.BlockSpec(memory_space=pl.ANY)
    vec = lambda v: v.reshape(1, -1).astype(F32)
    small = [(permt, full), (wglu, full), (vec(bglu), full), (wo, hbm), (vec(g1), full), (vec(b1), full),
             (w1, hbm), (w2, hbm), (vec(g2), full), (vec(b2), full)]
    res_specs = [row_blk(D_MODEL)] if skip_context else _residual_specs(residual)
    return pl.pallas_call(
        functools.partial(_post_kernel, split=len(residual) > 1, layer=layer),
        grid=(rows // tm - off,),
        in_specs=res_specs + [row_blk(ATT_WIDTH), row_blk(RET_WIDTH),
                              pl.BlockSpec((S5_PAIRS, tm // S5_T, S5_T * S5_PAIR_W), lambda i: (0, i + off, 0)),
                              pl.BlockSpec((None, None, N_ADA, D_MODEL),
                                           lambda i: (layer, jnp.where(i + off == 0, 1, 0), 0, 0))]
                 + [spec(arr) for arr, spec in small],
        out_specs=pl.BlockSpec((tm, D_MODEL), lambda i: (i, 0)),
        out_shape=jax.ShapeDtypeStruct((rows - off * tm, D_MODEL), F32),
        scratch_shapes=[pltpu.VMEM((ATT_WIDTH + RET_WIDTH + S5_WIDTH, D_MODEL), MXU_DTYPE),
                        pltpu.VMEM((D_MODEL, D_FF), MXU_DTYPE), pltpu.VMEM((D_FF, D_MODEL), MXU_DTYPE),
                        pltpu.VMEM((2, STAGE_BYTES // (4 * D_FF), D_FF), F32),
                        pltpu.VMEM((2, STAGE_BYTES // (4 * D_MODEL), D_MODEL), F32),
                        pltpu.SemaphoreType.DMA((2, 2))],
        compiler_params=_params(("arbitrary",)),
        name="post",
    )(*residual, att, ret, s5_pairs, mods, *[arr for arr, _ in small])


def kernel(x, c, ctx, c_ctx, w_ada, b_ada, w_in, att_sink, ret_decay_logit, s5_lambda_re, s5_lambda_im, s5_b_re,
           s5_b_im, s5_c_re, s5_c_im, s5_log_dt, s5_d, w_glu, b_glu, w_out, ln1_g, ln1_b, w_ff1, w_ff2, ln2_g,
           ln2_b):
    assert x.shape[0] == 1 and x.shape[2] == D_MODEL and ctx.shape[1] == CTX_LEN
    seq = x.shape[1]
    assert seq % ROW_TILE == 0
    residual = (jnp.pad(ctx[0], ((0, CTX_PAD - CTX_LEN), (0, 0))), x[0])
    cond = jnp.zeros((8, D_MODEL), F32).at[0].set(c[0]).at[1].set(c_ctx)
    mods = _modulation(cond, w_ada, b_ada).reshape(DEPTH, 8, N_ADA, D_MODEL)
    tabs = _rope_tables(seq)
    masks = _attention_masks()
    perm = _chunk_perm(ROW_TILE // POST_SPLIT, MXU_DTYPE)
    permt = perm.T
    col_scale = jnp.ones((IN_WIDTH,), F32).at[COL_AQ:COL_AK].set(HEAD_DIM ** -0.5 * LOG2E)
    col_scale = col_scale.at[COL_RQ:COL_RK].set(HEAD_DIM ** -0.5)
    s5w = jax.vmap(_s5_weights)(s5_lambda_re, s5_lambda_im, s5_b_re, s5_b_im, s5_c_re, s5_c_im, s5_log_dt, s5_d)
    log_gamma = jax.nn.log_sigmoid(ret_decay_logit.astype(F32))
    for l in range(DEPTH):
        proj, u_pairs = _in_proj(residual, mods, l, w_in, col_scale.reshape(1, IN_WIDTH), tabs, perm)
        att = _attention(proj, att_sink[l].astype(F32) * LOG2E, masks)
        ret = _retention(proj, log_gamma[l])
        s5 = _s5_mixer(u_pairs, s5w, l)
        stream = _post(residual, att, ret, s5, mods, l, permt, w_glu[l].astype(MXU_DTYPE), b_glu[l],
                       w_out, ln1_g[l], ln1_b[l], w_ff1, w_ff2, ln2_g[l], ln2_b[l],
                       skip_context=(l == DEPTH - 1))
        residual = (stream,)
    return stream[None]
```

```python
import functools
import math

import jax
import jax.numpy as jnp
from jax import lax
from jax.experimental import pallas as pl
from jax.experimental.pallas import tpu as pltpu

F32 = jnp.float32
MXU_DTYPE = jnp.bfloat16

D_MODEL = 1024
DEPTH = 4
GRID_W = 64
CTX_LEN = 256
CTX_PAD = 512
HEAD_DIM = 64
ATT_HEADS = 8
ATT_KV_HEADS = 2
ATT_BLOCK = 128
ATT_LOOKAHEAD = 3
ROPE_BASE = 10000.0
RET_HEADS = 4
RET_CHUNK = 256
S5_CH = 16
S5_GROUPS = 16
S5_STATE = 64
S5_T = 16
S5_PAIRS = S5_GROUPS // 2
S5_PAIR_W = 2 * S5_CH
S5_TILE = CTX_PAD // S5_T
SCAN_ROWS = 8
S5_STATE_W = 2 * 2 * S5_STATE
ATT_WIDTH = ATT_HEADS * HEAD_DIM
KV_WIDTH = ATT_KV_HEADS * HEAD_DIM
RET_WIDTH = RET_HEADS * HEAD_DIM
S5_WIDTH = S5_GROUPS * S5_CH
IN_WIDTH = ATT_WIDTH + 2 * KV_WIDTH + 4 * RET_WIDTH + S5_WIDTH
D_FF = 4 * D_MODEL
FF_CHUNK = 1024
N_ADA = 6
LN_EPS = 1e-5
GN_EPS = 1e-5
DEEPNORM_ALPHA = (2 * DEPTH) ** 0.25
ROW_TILE = 512
POST_SPLIT = 2
_NFF = D_FF // FF_CHUNK
POST_PROGRAM = (((0, 0), (0, 1), (1, 0), (0, 2), (1, 1)) + tuple((0, 2 + c) for c in range(1, _NFF))
                + ((1, 2), (0, 2 + _NFF)) + tuple((1, 2 + c) for c in range(1, _NFF)) + ((1, 2 + _NFF),))
NEG_BIG = -1e30
LOG2E = math.log2(math.e)
LANES = 128
VMEM_LIMIT = 56 * 1024 * 1024
STAGE_BYTES = 2 * 1024 * 1024

COL_AQ, COL_AK, COL_AV = 0, ATT_WIDTH, ATT_WIDTH + KV_WIDTH
COL_RQ = ATT_WIDTH + 2 * KV_WIDTH
COL_RK, COL_RV, COL_RG = COL_RQ + RET_WIDTH, COL_RQ + 2 * RET_WIDTH, COL_RQ + 3 * RET_WIDTH
COL_S5 = COL_RQ + 4 * RET_WIDTH


def _mm(a, b):
    return jnp.dot(a, b, preferred_element_type=F32)


def _mm_nt(a, b):
    return lax.dot_general(a, b, (((1,), (1,)), ((), ())), preferred_element_type=F32)


def _mm_tn(a, b):
    return lax.dot_general(a, b, (((0,), (0,)), ((), ())), preferred_element_type=F32)


def _params(sem):
    return pltpu.CompilerParams(dimension_semantics=sem, vmem_limit_bytes=VMEM_LIMIT)


def _mod_kernel(cond_ref, w_ref, b_ref, o_ref):
    c = cond_ref[...]
    s = c * jax.nn.sigmoid(c)
    o_ref[0] = jnp.dot(s, w_ref[0], preferred_element_type=F32, precision=lax.Precision.HIGHEST) + b_ref[0]


def _modulation(cond, w_ada, b_ada):
    tn = 1536
    n = N_ADA * D_MODEL
    return pl.pallas_call(
        _mod_kernel,
        grid=(DEPTH, n // tn),
        in_specs=[
            pl.BlockSpec((8, D_MODEL), lambda l, j: (0, 0)),
            pl.BlockSpec((1, D_MODEL, tn), lambda l, j: (l, 0, j)),
            pl.BlockSpec((1, 1, tn), lambda l, j: (l, 0, j)),
        ],
        out_specs=pl.BlockSpec((1, 8, tn), lambda l, j: (l, 0, j)),
        out_shape=jax.ShapeDtypeStruct((DEPTH, 8, n), F32),
        compiler_params=_params(("arbitrary", "arbitrary")),
        name="modulation",
    )(cond, w_ada, b_ada.reshape(DEPTH, 1, n))


def _lane_block_shuffle(src_rows, src_lane_blk, out_vreg):
    acc = None
    for q in range(LANES // S5_PAIR_W):
        piece = src_rows(out_vreg * (LANES // S5_PAIR_W) + q)
        shift = (S5_PAIR_W * (q - src_lane_blk)) % LANES
        if shift:
            piece = pltpu.roll(piece, shift, 1)
        if acc is None:
            acc = piece
        else:
            lane_blk = lax.broadcasted_iota(jnp.int32, piece.shape, 1) // S5_PAIR_W
            acc = jnp.where(lane_blk == q, piece, acc)
    return acc


def _in_proj_kernel(*refs, split, layer):
    nres = 2 if split else 1
    (mod_ref, w_hbm, scale_ref, ca_ref, sa_ref, cr_ref, sr_ref, perm_ref, o_ref, u_ref, w_ref, stage_ref,
     sem_ref) = refs[nres:]

    @pl.when(pl.program_id(0) == 0)
    def _stage_weights():
        _load_cast(w_hbm.at[layer], w_ref, stage_ref, sem_ref, col_scale=scale_ref[...])

    x = _residual_rows(refs[:nres])(slice(None))
    h = (x * (1.0 + mod_ref[1:2, :]) + mod_ref[0:1, :]).astype(w_ref.dtype)
    lane = lax.broadcasted_iota(jnp.int32, (x.shape[0], LANES), 1)
    first_att = lane % (HEAD_DIM // 2) < HEAD_DIM // 4
    first_ret = lane % HEAD_DIM < HEAD_DIM // 2

    def proj(c0, c1):
        return _mm(h, w_ref[:, c0:c1])

    def rope_store(p, c0, width, cos, sin, first, half):
        for b in range(width // LANES):
            blk = p[:, LANES * b:LANES * (b + 1)]
            rot = jnp.where(first, pltpu.roll(blk, LANES - half, 1), pltpu.roll(blk, half, 1))
            o_ref[:, c0 + LANES * b:c0 + LANES * (b + 1)] = (blk * cos + rot * sin).astype(o_ref.dtype)

    def plain_store(p, c0, width):
        o_ref[:, c0:c0 + width] = p.astype(o_ref.dtype)

    u = proj(COL_S5, IN_WIDTH).astype(w_ref.dtype)
    sub = perm_ref.shape[0]
    nchunk = sub // S5_T
    for part in range(x.shape[0] // sub):
        g = _mm(perm_ref[...], u[sub * part:sub * (part + 1)])
        for a in range(S5_PAIRS):
            vreg_col, lane_blk = divmod(a * S5_PAIR_W, LANES)
            lane_blk //= S5_PAIR_W
            for v in range(S5_T * S5_PAIR_W // LANES):
                slab = _lane_block_shuffle(
                    lambda j: g[nchunk * j:nchunk * (j + 1), vreg_col * LANES:(vreg_col + 1) * LANES], lane_blk, v)
                u_ref[a, nchunk * part:nchunk * (part + 1), LANES * v:LANES * (v + 1)] = slab.astype(u_ref.dtype)

    ca, sa, cr, sr = ca_ref[...], sa_ref[...], cr_ref[...], sr_ref[...]
    att_rope = (ca, sa, first_att, HEAD_DIM // 4)
    ret_rope = (cr, sr, first_ret, HEAD_DIM // 2)
    groups = [(COL_AQ, ATT_WIDTH, att_rope), (COL_AK, KV_WIDTH, att_rope), (COL_RQ, RET_WIDTH, ret_rope),
              (COL_RK, RET_WIDTH, ret_rope), (COL_AV, KV_WIDTH, None), (COL_RV, COL_S5 - COL_RV, None)]
    pending = None
    for group in groups + [None]:
        nxt = None if group is None else (proj(group[0], group[0] + group[1]),) + group
        if pending is not None:
            p, pc0, pwidth, prope = pending
            if prope is None:
                plain_store(p, pc0, pwidth)
            else:
                rope_store(p, pc0, pwidth, *prope)
        pending = nxt


def _chunk_perm(tile_rows, dtype):
    nchunk = tile_rows // S5_T
    r = jnp.arange(tile_rows)
    src = S5_T * (r % nchunk) + r // nchunk
    return (src[:, None] == jnp.arange(tile_rows)[None, :]).astype(dtype)


def _in_proj(residual, mods, layer, w_in, col_scale, tabs, perm):
    rows = tabs[0].shape[0]
    tm = ROW_TILE
    tab_spec = pl.BlockSpec((tm, LANES), lambda i: (i, 0))
    nch = rows // S5_T
    return pl.pallas_call(
        functools.partial(_in_proj_kernel, split=len(residual) > 1, layer=layer),
        grid=(rows // tm,),
        in_specs=_residual_specs(residual) + [
            pl.BlockSpec((None, None, N_ADA, D_MODEL), lambda i: (layer, jnp.where(i == 0, 1, 0), 0, 0)),
            pl.BlockSpec(memory_space=pl.ANY),
            pl.BlockSpec((1, IN_WIDTH), lambda i: (0, 0)),
            tab_spec, tab_spec, tab_spec, tab_spec,
            pl.BlockSpec(perm.shape, lambda i: (0, 0)),
        ],
        out_specs=[pl.BlockSpec((tm, COL_S5), lambda i: (i, 0)),
                   pl.BlockSpec((S5_PAIRS, tm // S5_T, S5_T * S5_PAIR_W), lambda i: (0, i, 0))],
        out_shape=[jax.ShapeDtypeStruct((rows, COL_S5), MXU_DTYPE),
                   jax.ShapeDtypeStruct((S5_PAIRS, nch, S5_T * S5_PAIR_W), MXU_DTYPE)],
        scratch_shapes=[pltpu.VMEM((D_MODEL, IN_WIDTH), MXU_DTYPE),
                        pltpu.VMEM((2, STAGE_BYTES // (4 * IN_WIDTH), IN_WIDTH), F32),
                        pltpu.SemaphoreType.DMA((2,))],
        compiler_params=_params(("arbitrary",)),
        name="in_proj",
    )(*residual, mods, w_in, col_scale, *tabs, perm)


def _rope_tables(seq):
    half_a = HEAD_DIM // 4
    half_r = HEAD_DIM // 2
    nrow = seq // GRID_W
    inv_a = ROPE_BASE ** (-jnp.arange(half_a, dtype=F32) / half_a)
    inv_r = ROPE_BASE ** (-jnp.arange(half_r, dtype=F32) / half_r)
    ang_r = jnp.arange(nrow, dtype=F32)[:, None] * inv_a[None, :]
    ang_c = jnp.arange(GRID_W, dtype=F32)[:, None] * inv_a[None, :]
    ang_t = jnp.arange(seq, dtype=F32)[:, None] * inv_r[None, :]
    hp = lax.Precision.HIGHEST
    lane = jnp.arange(LANES)
    within = lane % HEAD_DIM
    pick_a = (within % half_a)[None, :] == jnp.arange(half_a)[:, None]
    exp_row = (pick_a & (within < 2 * half_a)[None, :]).astype(F32)
    exp_col = (pick_a & (within >= 2 * half_a)[None, :]).astype(F32)
    exp_t = ((lane % half_r)[None, :] == jnp.arange(half_r)[:, None]).astype(F32)
    sign_a = jnp.where(within % (2 * half_a) < half_a, -1.0, 1.0).astype(F32)
    sign_r = jnp.where(within < half_r, -1.0, 1.0).astype(F32)

    def att_table(fn):
        by_row = jnp.dot(fn(ang_r), exp_row, precision=hp)
        by_col = jnp.dot(fn(ang_c), exp_col, precision=hp)
        return (by_row[:, None, :] + by_col[None, :, :]).reshape(seq, LANES)

    cos_a = att_table(jnp.cos)
    sin_a = att_table(jnp.sin) * sign_a
    cos_r = jnp.dot(jnp.cos(ang_t), exp_t, precision=hp)
    sin_r = jnp.dot(jnp.sin(ang_t), exp_t, precision=hp) * sign_r
    pad = lambda tab, ident: jnp.pad(tab, ((CTX_PAD, 0), (0, 0)), constant_values=ident)
    return pad(cos_a, 1.0), pad(sin_a, 0.0), pad(cos_r, 1.0), pad(sin_r, 0.0)


def _swap_halves(x):
    if x.dtype.itemsize == 4:
        return pltpu.roll(x, 64, 1)
    packed = pltpu.bitcast(x, jnp.uint32)
    return pltpu.bitcast(pltpu.roll(packed, 64, 1), x.dtype)


def _attn_kernel(sink_ref, q_ref, km_ref, kp_ref, kn_ref, vm_ref, vp_ref, vn_ref, kc_ref, vc_ref, mask_ref,
                 o_ref, kab_ref, vab_ref, kabc_ref, vabc_ref):
    i = pl.program_id(0)
    last_blk = pl.num_programs(0) * (ROW_TILE // ATT_BLOCK) - 1
    blk = ATT_BLOCK
    pair = 2 * blk

    def placed(x, kv):
        lo_x = lax.broadcasted_iota(jnp.int32, x.shape, 1) < HEAD_DIM
        sw = _swap_halves(x)
        zero = jnp.zeros_like(x)
        if kv == 0:
            return jnp.where(lo_x, x, zero), jnp.where(lo_x, zero, sw)
        return jnp.where(lo_x, sw, zero), jnp.where(lo_x, zero, x)

    def ones_half(rows, dtype, second):
        lane = lax.broadcasted_iota(jnp.int32, (rows, LANES), 1)
        hit = (lane >= HEAD_DIM) if second else (lane < HEAD_DIM)
        return jnp.where(hit, 1.0, 0.0).astype(dtype)

    def fill(k_dst, v_dst, k_src, v_src, row0, interleave):
        n = k_src.shape[0]
        kx, vx = k_src[...], v_src[...]
        for kv in range(ATT_KV_HEADS):
            ka, kb = placed(kx, kv)
            va, vb = placed(vx, kv)
            for b in range(n // blk) if interleave else range(1):
                rows = slice(blk * b, blk * (b + 1)) if interleave else slice(0, n)
                size = blk if interleave else n
                base = row0 + (pair * b if interleave else 0)
                for which, (kpart, vpart) in enumerate(((ka, va), (kb, vb))):
                    dst = slice(base + size * which, base + size * (which + 1))
                    k_dst[kv, dst, :] = kpart[rows]
                    v_dst[kv, dst, 0:LANES] = vpart[rows]
                    v_dst[kv, dst, LANES:2 * LANES] = ones_half(size, v_dst.dtype, which == 1)

    fill(kab_ref, vab_ref, kp_ref, vp_ref, 0, True)
    fill(kab_ref, vab_ref, km_ref, vm_ref, pair, True)
    fill(kab_ref, vab_ref, kn_ref, vn_ref, pair * (1 + ROW_TILE // blk), True)
    fill(kabc_ref, vabc_ref, kc_ref, vc_ref, 0, False)

    lo = lax.broadcasted_iota(jnp.int32, (blk, LANES), 1) < HEAD_DIM
    group = ATT_HEADS // ATT_KV_HEADS
    ncol = group // 2
    nwin = 3
    nctx = CTX_LEN

    def scores(j, kv):
        r0 = j * blk
        qt = q_ref[r0:r0 + blk, group * HEAD_DIM * kv:group * HEAD_DIM * (kv + 1)]
        q2 = jnp.concatenate([qt[:, LANES * c:LANES * (c + 1)] for c in range(ncol)], axis=0)
        return _mm_nt(q2, kab_ref[kv, pair * j:pair * (j + nwin), :]), _mm_nt(q2, kabc_ref[kv])

    def finish(j, kv, s_loc, s_ctx):
        r0 = j * blk
        gblk = i * (ROW_TILE // blk) + j
        sel = jnp.where(i == 0, 3, jnp.where(gblk == CTX_PAD // blk, 0, jnp.where(gblk == last_blk, 2, 1)))
        bias = mask_ref[sel]
        p_loc, p_ctx, sink_w = [], [], []
        for c in range(ncol):
            rows = slice(blk * c, blk * (c + 1))
            e = []
            for par in range(2):
                loc = jnp.concatenate([s_loc[rows, pair * b + blk * par:pair * b + blk * (par + 1)]
                                       for b in range(nwin)], axis=1) + bias
                s = jnp.concatenate([loc, s_ctx[rows, nctx * par:nctx * (par + 1)]], axis=1)
                sk = sink_ref[group * kv + 2 * c + par]
                m = jnp.maximum(jnp.max(s, axis=-1, keepdims=True), sk)
                e.append(jnp.exp2(s - m).astype(o_ref.dtype))
                sink_w.append(jnp.exp2(sk - m))
            p_loc.append(jnp.concatenate([e[par][:, blk * b:blk * (b + 1)] for b in range(nwin) for par in range(2)],
                                         axis=1))
            p_ctx.append(jnp.concatenate([e[par][:, nwin * blk:] for par in range(2)], axis=1))
        o = (_mm(jnp.concatenate(p_loc, axis=0), vab_ref[kv, pair * j:pair * (j + nwin), :])
             + _mm(jnp.concatenate(p_ctx, axis=0), vabc_ref[kv]))
        for c in range(ncol):
            oc = o[blk * c:blk * (c + 1)]
            den = oc[:, LANES:2 * LANES] + jnp.where(lo, sink_w[2 * c], sink_w[2 * c + 1])
            c0 = group * HEAD_DIM * kv + LANES * c
            o_ref[r0:r0 + blk, c0:c0 + LANES] = (oc[:, 0:LANES] * (1.0 / den)).astype(o_ref.dtype)

    items = [(j, kv) for j in range(ROW_TILE // blk) for kv in range(ATT_KV_HEADS)]
    pending = {}
    for t in range(len(items) + ATT_LOOKAHEAD):
        if t < len(items):
            pending[t] = scores(*items[t])
        if t >= ATT_LOOKAHEAD:
            finish(*items[t - ATT_LOOKAHEAD], *pending.pop(t - ATT_LOOKAHEAD))


def _attention_masks():
    qi = jnp.arange(ATT_BLOCK)[:, None]
    kj = jnp.arange(3 * ATT_BLOCK)[None, :]
    band = jnp.abs(kj - ATT_BLOCK - qi) <= ATT_BLOCK
    first = band & (kj >= ATT_BLOCK)
    last = band & (kj < 2 * ATT_BLOCK)
    none = jnp.zeros_like(band)
    masks = jnp.stack([first, band, last, none])
    return jnp.where(masks, 0.0, NEG_BIG).astype(F32)


def _attention(proj, sink, masks):
    rows = proj.shape[0]
    tm, blk = ROW_TILE, ATT_BLOCK
    per = tm // blk
    nblk = rows // blk
    ck, cv = COL_AK // KV_WIDTH, COL_AV // KV_WIDTH
    dt = proj.dtype
    return pl.pallas_call(
        _attn_kernel,
        grid=(rows // tm,),
        in_specs=[
            pl.BlockSpec(memory_space=pltpu.SMEM),
            pl.BlockSpec((tm, ATT_WIDTH), lambda i: (i, 0)),
            pl.BlockSpec((tm, KV_WIDTH), lambda i: (i, ck)),
            pl.BlockSpec((blk, KV_WIDTH), lambda i: (jnp.maximum(i * per - 1, 0), ck)),
            pl.BlockSpec((blk, KV_WIDTH), lambda i: (jnp.minimum((i + 1) * per, nblk - 1), ck)),
            pl.BlockSpec((tm, KV_WIDTH), lambda i: (i, cv)),
            pl.BlockSpec((blk, KV_WIDTH), lambda i: (jnp.maximum(i * per - 1, 0), cv)),
            pl.BlockSpec((blk, KV_WIDTH), lambda i: (jnp.minimum((i + 1) * per, nblk - 1), cv)),
            pl.BlockSpec((CTX_LEN, KV_WIDTH), lambda i: (0, ck)),
            pl.BlockSpec((CTX_LEN, KV_WIDTH), lambda i: (0, cv)),
            pl.BlockSpec((4, blk, 3 * blk), lambda i: (0, 0, 0)),
        ],
        out_specs=pl.BlockSpec((tm, ATT_WIDTH), lambda i: (i, 0)),
        out_shape=jax.ShapeDtypeStruct((rows, ATT_WIDTH), dt),
        scratch_shapes=[
            pltpu.VMEM((ATT_KV_HEADS, 2 * (tm + 2 * blk), LANES), dt),
            pltpu.VMEM((ATT_KV_HEADS, 2 * (tm + 2 * blk), 2 * LANES), dt),
            pltpu.VMEM((ATT_KV_HEADS, 2 * CTX_LEN, LANES), dt),
            pltpu.VMEM((ATT_KV_HEADS, 2 * CTX_LEN, 2 * LANES), dt),
        ],
        compiler_params=_params(("arbitrary",)),
        name="attention",
    )(sink, proj, proj, proj, proj, proj, proj, proj, proj, proj, masks)


def _ret_kernel(*refs, sup, ntile):
    lg_ref, q_ref = refs[:2]
    k_refs, v_refs = refs[2:2 + sup], refs[2 + sup:2 + 2 * sup]
    g_ref, o_ref, sb_ref, s_ref, dm_ref, tab_ref, gbd_ref = refs[2 + 2 * sup:]
    k_ref, v_ref = k_refs[0], v_refs[0]
    step = pl.program_id(0)
    nsup = (ntile - 1) // sup
    ph = jnp.where(step <= nsup, 0, 1)
    t = jnp.where(step <= nsup, step, step - (nsup + 1))
    c = RET_CHUNK
    w = RET_WIDTH
    per = q_ref.shape[0] // c
    mdt = sb_ref.dtype
    rows = lambda ci: slice(c * ci, c * (ci + 1))

    def lane_vec(direction, shape, axis):
        head = lax.broadcasted_iota(jnp.int32, shape, axis) // HEAD_DIM
        out = jnp.full(shape, lg_ref[direction, RET_HEADS - 1], F32)
        for h in range(RET_HEADS - 2, -1, -1):
            out = jnp.where(head == h, lg_ref[direction, h], out)
        return out

    @pl.when(jnp.logical_and(ph == 0, t == 0))
    def _init_tables():
        diff = (lax.broadcasted_iota(jnp.int32, (c, c), 0) - lax.broadcasted_iota(jnp.int32, (c, c), 1)).astype(F32)
        for h in range(RET_HEADS):
            dm_ref[h] = jnp.exp(jnp.where(diff >= 0, diff * lg_ref[0, h], -diff * lg_ref[1, h]))
        pos = lax.broadcasted_iota(jnp.int32, (c, w), 0).astype(F32)
        lgf = lane_vec(0, (c, w), 1)
        lgb = lane_vec(1, (c, w), 1)
        tab_ref[0] = jnp.exp((c - 1.0 - pos) * lgf)
        tab_ref[1] = jnp.exp((pos + 1.0) * lgf)
        tab_ref[2] = jnp.exp(pos * lgb)
        tab_ref[3] = jnp.exp((c - pos) * lgb)
        same = (lax.broadcasted_iota(jnp.int32, (w, w), 0) // HEAD_DIM
                == lax.broadcasted_iota(jnp.int32, (w, w), 1) // HEAD_DIM)
        bd = jnp.where(same, 1.0, 0.0)
        gbd_ref[0] = bd * jnp.exp(c * lane_vec(0, (w, w), 0))
        gbd_ref[1] = bd * jnp.exp(c * lane_vec(1, (w, w), 0))
        gbd_ref[2] = bd

    @pl.when(t == 0)
    def _reset_state():
        s_ref[...] = jnp.zeros_like(s_ref)

    def state_update(direction, key_tab, ci, kr=k_ref, vr=v_ref):
        kw = (kr[rows(ci), :].astype(F32) * tab_ref[key_tab]).astype(mdt)
        u = _mm_tn(kw, vr[rows(ci), :])
        s_ref[...] = gbd_ref[direction] * s_ref[...] + gbd_ref[2] * u

    @pl.when(jnp.logical_and(ph == 0, t == 0))
    def _backward_context():
        sb_ref[0] = s_ref[...].astype(mdt)
        state_update(1, 2, 0)

    @pl.when(jnp.logical_and(ph == 0, t > 0))
    def _backward_latent():
        base = 1 + per * sup * (nsup - t)
        for qi in range(sup - 1, -1, -1):
            for ci in range(per - 1, -1, -1):
                sb_ref[base + per * qi + ci] = s_ref[...].astype(mdt)
                state_update(1, 2, ci, k_refs[qi], v_refs[qi])

    head = lax.broadcasted_iota(jnp.int32, (c, w), 1) // HEAD_DIM

    def scores(ci):
        q = q_ref[rows(ci), :]
        qs = jnp.concatenate([jnp.where(head == h, q, jnp.zeros_like(q)) for h in range(RET_HEADS)], axis=0)
        return _mm_nt(qs, k_ref[rows(ci), :])

    def intra(ci, sc):
        scd = jnp.concatenate([sc[c * h:c * (h + 1)] * dm_ref[h] for h in range(RET_HEADS)], axis=0).astype(mdt)
        oi = _mm(scd, v_ref[rows(ci), :])
        o = oi[c * (RET_HEADS - 1):c * RET_HEADS]
        for h in range(RET_HEADS - 2, -1, -1):
            o = jnp.where(head == h, oi[c * h:c * (h + 1)], o)
        return o

    def cross(ci, idx):
        qf = q_ref[rows(ci), :].astype(F32)
        return (_mm((qf * tab_ref[1]).astype(mdt), s_ref[...].astype(mdt))
                + _mm((qf * tab_ref[3]).astype(mdt), sb_ref[idx]))

    def finish(ci, o):
        avg = (gbd_ref[2] * (1.0 / HEAD_DIM)).astype(mdt)
        o_hi = o.astype(mdt)
        d = o - (_mm(o_hi, avg) + _mm((o - o_hi.astype(F32)).astype(mdt), avg))
        var = _mm((d * d).astype(mdt), avg)
        gate = g_ref[rows(ci), :].astype(F32)
        y = d * lax.rsqrt(var + GN_EPS) * (gate * jax.nn.sigmoid(gate))
        o_ref[rows(ci), :] = y.astype(o_ref.dtype)

    @pl.when(jnp.logical_and(ph == 1, t == 0))
    def _forward_context():
        o = intra(0, scores(0)) + cross(0, 0)
        state_update(0, 0, 0)
        finish(0, o)
        for ci in range(1, per):
            o_ref[rows(ci), :] = jnp.zeros((c, w), o_ref.dtype)

    @pl.when(jnp.logical_and(ph == 1, t > 0))
    def _forward_latent():
        base = 1 + per * (t - 1)
        sc = [scores(ci) for ci in range(per)]
        outs = [intra(ci, sc[ci]) for ci in range(per)]
        for ci in range(per):
            outs[ci] = outs[ci] + cross(ci, base + ci)
            state_update(0, 0, ci)
        for ci in range(per):
            finish(ci, outs[ci])


def _retention(proj, log_gamma):
    rows = proj.shape[0]
    c = RET_CHUNK
    tm = ROW_TILE
    ntile = rows // tm
    nchunk = 1 + (rows - CTX_PAD) // c
    dt = proj.dtype
    cq, ckk, cvv, cg = (COL_RQ // RET_WIDTH, COL_RK // RET_WIDTH, COL_RV // RET_WIDTH, COL_RG // RET_WIDTH)

    nlat = ntile - 1
    sup = next(s for s in (4, 2, 1) if nlat % s == 0)
    nsup = nlat // sup
    nback = 1 + nsup

    def fw_blk(step):
        return jnp.maximum(step - nback, 0)

    def kv_spec(col, qi):
        def index(step):
            back = jnp.where(step == 0, 0, 1 + sup * (nsup - jnp.minimum(step, nsup)) + qi)
            fwd = step - nback if qi == 0 else 1 + qi
            return jnp.where(step < nback, back, fwd), col
        return pl.BlockSpec((tm, RET_WIDTH), index)

    return pl.pallas_call(
        functools.partial(_ret_kernel, sup=sup, ntile=ntile),
        grid=(nback + ntile,),
        in_specs=[pl.BlockSpec(memory_space=pltpu.SMEM),
                  pl.BlockSpec((tm, RET_WIDTH), lambda step: (fw_blk(step), cq))]
                 + [kv_spec(ckk, qi) for qi in range(sup)] + [kv_spec(cvv, qi) for qi in range(sup)]
                 + [pl.BlockSpec((tm, RET_WIDTH), lambda step: (fw_blk(step), cg))],
        out_specs=pl.BlockSpec((tm, RET_WIDTH), lambda step: (fw_blk(step), 0)),
        out_shape=jax.ShapeDtypeStruct((rows, RET_WIDTH), dt),
        scratch_shapes=[
            pltpu.VMEM((nchunk, RET_WIDTH, RET_WIDTH), dt),
            pltpu.VMEM((RET_WIDTH, RET_WIDTH), F32),
            pltpu.VMEM((RET_HEADS, c, c), F32),
            pltpu.VMEM((4, c, RET_WIDTH), F32),
            pltpu.VMEM((3, RET_WIDTH, RET_WIDTH), F32),
        ],
        compiler_params=_params(("arbitrary",)),
        name="retention",
    )(log_gamma, proj, *([proj] * (2 * sup)), proj)


def _s5_weights(lam_re, lam_im, b_re, b_im, c_re, c_im, log_dt, d_skip):
    tt, g, n, p, a = S5_T, S5_GROUPS, S5_STATE, S5_CH, S5_PAIRS
    lam = lax.complex(lam_re.astype(F32), lam_im.astype(F32))
    dtv = jnp.exp(log_dt.astype(F32))[..., None]
    lam_bar = jnp.exp(lam * dtv)
    bbar = ((lam_bar - 1.0) / lam)[..., None] * lax.complex(b_re.astype(F32), b_im.astype(F32))
    cmat = lax.complex(c_re.astype(F32), c_im.astype(F32))
    pw = [jnp.ones_like(lam_bar)]
    for _ in range(tt):
        pw.append(pw[-1] * lam_bar)
    pw = jnp.stack(pw, axis=1)
    eye2 = jnp.eye(2, dtype=F32)
    ri = lambda z, axis: jnp.stack([jnp.real(z), jnp.imag(z)], axis=axis)

    pw_l = pw.reshape(2, tt + 1, a, 2 * n)
    bbt = jnp.einsum('dahpn,gh->dagphn', jnp.swapaxes(bbar, -1, -2).reshape(2, a, 2, p, n), eye2)
    bbt = bbt.reshape(2, a, 2 * p, 2 * n)
    cct = jnp.einsum('dahpn,gh->dagphn', cmat.reshape(2, a, 2, p, n), eye2).reshape(2, a, 2 * p, 2 * n)
    pw_k = ri(pw_l, 1).transpose(3, 0, 1, 2, 4)
    b_k = ri(bbt, 2).transpose(1, 0, 2, 3, 4)
    c_k = ri(cct, 2).transpose(1, 0, 2, 3, 4)
    decay = [pw_l[:, tt]]
    for _ in range(SCAN_ROWS - 1):
        decay.append(decay[-1] * decay[0])
    decay = jnp.stack(decay, axis=0)
    rows8 = lambda z: jnp.broadcast_to(z[None], (SCAN_ROWS,) + z.shape)
    carry_w = jnp.stack([decay[:, 0], decay[::-1, 1]], axis=1)
    scan_tab = jnp.stack([rows8(decay[0]), rows8(decay[1]), rows8(decay[3]), carry_w], axis=0)
    scan_tab = ri(scan_tab, 0).transpose(4, 3, 1, 0, 2, 5)
    skip = jnp.tile(d_skip.astype(F32).reshape(a, 1, 2 * p), (1, tt, 1)).reshape(a, 1, tt * 2 * p)
    return pw_k, b_k, c_k, scan_tab, skip


def _pair_spec(layer, *shape):
    return pl.BlockSpec((None, None) + shape, lambda i: (layer, i) + (0,) * len(shape))


def _s5_drive_kernel(u_ref, pw_ref, b_ref, o_ref, w_ref):
    rows, half = S5_PAIR_W, LANES
    for d in range(2):
        br, bi = b_ref[d, 0], b_ref[d, 1]
        for j in range(S5_T):
            e = S5_T - 1 - j if d == 0 else j
            pr, pi = pw_ref[d, 0, e:e + 1, :], pw_ref[d, 1, e:e + 1, :]
            w_ref[rows * j:rows * (j + 1), 2 * half * d:2 * half * d + half] = (pr * br - pi * bi).astype(w_ref.dtype)
            w_ref[rows * j:rows * (j + 1), 2 * half * d + half:2 * half * (d + 1)] = (
                pr * bi + pi * br).astype(w_ref.dtype)
    o_ref[0] = _mm(u_ref[0], w_ref[...])


def _s5_drive(u_pairs, pw_k, b_k, layer):
    a, nch, wd = u_pairs.shape
    return pl.pallas_call(
        _s5_drive_kernel,
        grid=(a,),
        in_specs=[pl.BlockSpec((1, nch, wd), lambda i: (i, 0, 0)),
                  _pair_spec(layer, *pw_k.shape[2:]), _pair_spec(layer, *b_k.shape[2:])],
        out_specs=pl.BlockSpec((1, nch, 2 * S5_STATE_W), lambda i: (i, 0, 0)),
        out_shape=jax.ShapeDtypeStruct((a, nch, 2 * S5_STATE_W), F32),
        scratch_shapes=[pltpu.VMEM((wd, 2 * S5_STATE_W), u_pairs.dtype)],
        compiler_params=_params(("arbitrary",)),
        name="s5_drive",
    )(u_pairs, pw_k, b_k)


def _s5_scan_kernel(ef_ref, eb_ref, tab_ref, sf_ref, sb_ref, st_ref):
    t = pl.program_id(0)
    hw = LANES
    sub = SCAN_ROWS
    npair = tab_ref.shape[0]
    row = lax.broadcasted_iota(jnp.int32, (sub, hw), 0)

    @pl.when(t == 0)
    def _reset():
        st_ref[...] = jnp.zeros_like(st_ref)

    def shift(x, k, reverse):
        if reverse:
            return jnp.where(row < sub - k, pltpu.roll(x, sub - k, 0), 0.0)
        return jnp.where(row >= k, pltpu.roll(x, k, 0), 0.0)

    def scan_group(e_ref, a, d, r0, cr, ci):
        reverse = d == 1
        xr = e_ref[a, pl.ds(r0, sub), 0:hw]
        xi = e_ref[a, pl.ds(r0, sub), hw:2 * hw]
        for step, k in enumerate((1, 2, 4)):
            ar, ai = tab_ref[a, d, step, 0], tab_ref[a, d, step, 1]
            sr, si = shift(xr, k, reverse), shift(xi, k, reverse)
            xr, xi = xr + ar * sr - ai * si, xi + ar * si + ai * sr
        wr, wi = tab_ref[a, d, 3, 0], tab_ref[a, d, 3, 1]
        fr = xr + wr * cr - wi * ci
        fi = xi + wr * ci + wi * cr
        edge = sub - 1 if reverse else 0
        before = (jnp.where(row == edge, cr, shift(fr, 1, reverse)), jnp.where(row == edge, ci, shift(fi, 1, reverse)))
        last = 0 if reverse else sub - 1
        after = (jnp.broadcast_to(fr[last:last + 1], (sub, hw)), jnp.broadcast_to(fi[last:last + 1], (sub, hw)))
        return before, after

    def store(o_ref, a, r0, lower, upper):
        for part in range(2):
            val = jnp.concatenate([lower[part], upper[part]], axis=0)
            o_ref[a, pl.ds(r0, 2 * sub), part * hw:(part + 1) * hw] = val.astype(o_ref.dtype)

    def run(nrows):
        span = 2 * sub
        nspan = nrows // span

        def body(gi, carry):
            r0 = pl.multiple_of(gi * span, span)
            rb0 = pl.multiple_of((nspan - 1 - gi) * span, span)
            new = []
            for a in range(npair):
                fr, fi, br, bi = carry[a]
                f_lo, (fr, fi) = scan_group(ef_ref, a, 0, r0, fr, fi)
                f_hi, (fr, fi) = scan_group(ef_ref, a, 0, r0 + sub, fr, fi)
                store(sf_ref, a, r0, f_lo, f_hi)
                b_hi, (br, bi) = scan_group(eb_ref, a, 1, rb0 + sub, br, bi)
                b_lo, (br, bi) = scan_group(eb_ref, a, 1, rb0, br, bi)
                store(sb_ref, a, rb0, b_lo, b_hi)
                new.append((fr, fi, br, bi))
            return tuple(new)

        init = tuple(tuple(st_ref[a, k] for k in range(4)) for a in range(npair))
        final = lax.fori_loop(0, nspan, body, init)
        for a in range(npair):
            for k in range(4):
                st_ref[a, k] = final[a][k]

    @pl.when(t == 0)
    def _context():
        sf_ref[...] = jnp.zeros_like(sf_ref)
        sb_ref[...] = jnp.zeros_like(sb_ref)
        run(CTX_LEN // S5_T)

    @pl.when(t > 0)
    def _latent():
        run(S5_TILE)


def _s5_scan(drive, scan_tab, layer):
    a, nch, wd2 = drive.shape
    wd = wd2 // 2
    nt = nch // S5_TILE

    def bwd(t):
        return jnp.where(t == 0, 0, nt - t)

    return pl.pallas_call(
        _s5_scan_kernel,
        grid=(nt,),
        in_specs=[pl.BlockSpec((a, S5_TILE, wd), lambda t: (0, t, 0)),
                  pl.BlockSpec((a, S5_TILE, wd), lambda t: (0, bwd(t), 1)),
                  pl.BlockSpec((None,) + scan_tab.shape[1:], lambda t: (layer,) + (0,) * (scan_tab.ndim - 1))],
        out_specs=[pl.BlockSpec((a, S5_TILE, wd), lambda t: (0, t, 0)),
                   pl.BlockSpec((a, S5_TILE, wd), lambda t: (0, bwd(t), 0))],
        out_shape=[jax.ShapeDtypeStruct((a, nch, wd), MXU_DTYPE), jax.ShapeDtypeStruct((a, nch, wd), MXU_DTYPE)],
        scratch_shapes=[pltpu.VMEM((a, 4, SCAN_ROWS, LANES), F32)],
        compiler_params=_params(("arbitrary",)),
        name="s5_scan",
    )(drive, drive, scan_tab)


def _lane_window(x, start, width):
    cols = []
    for v in range(width // LANES):
        k0, off = divmod(start + LANES * v, LANES)
        lo = x[:, LANES * k0:LANES * (k0 + 1)]
        if off:
            hi = x[:, LANES * (k0 + 1):LANES * (k0 + 2)]
            lane = lax.broadcasted_iota(jnp.int32, lo.shape, 1)
            lo = jnp.where(lane < LANES - off, pltpu.roll(lo, LANES - off, 1), pltpu.roll(hi, LANES - off, 1))
        cols.append(lo)
    return jnp.concatenate(cols, axis=1)


def _s5_read_kernel(u_ref, sf_ref, sb_ref, pw_ref, b_ref, c_ref, skip_ref, o_ref, wt_ref, wi_ref, lag_ref):
    u = u_ref[0]
    mdt = u.dtype
    rows, half = S5_PAIR_W, LANES
    for d in range(2):
        cr, ci = c_ref[d, 0], c_ref[d, 1]
        for i in range(S5_T):
            e = i + 1 if d == 0 else S5_T - i
            pr, pi = pw_ref[d, 0, e:e + 1, :], pw_ref[d, 1, e:e + 1, :]
            wt_ref[d, rows * i:rows * (i + 1), 0:half] = (pr * cr - pi * ci).astype(mdt)
            wt_ref[d, rows * i:rows * (i + 1), half:2 * half] = (-(pr * ci + pi * cr)).astype(mdt)
    nlag = 2 * S5_T - 1
    ldt = lag_ref.dtype
    zero = jnp.zeros((rows, half), ldt)
    for l in range(nlag + 1):
        lag = l - (S5_T - 1)
        for d, active in ((1, lag <= 0), (0, 0 <= lag < S5_T)):
            col = 2 * half * (1 - d)
            if active:
                cr, ci = c_ref[d, 0], c_ref[d, 1]
                pr, pi = pw_ref[d, 0, abs(lag):abs(lag) + 1, :], pw_ref[d, 1, abs(lag):abs(lag) + 1, :]
                lag_ref[rows * l:rows * (l + 1), col:col + half] = (pr * cr - pi * ci).astype(ldt)
                lag_ref[rows * l:rows * (l + 1), col + half:col + 2 * half] = (pr * ci + pi * cr).astype(ldt)
            else:
                lag_ref[rows * l:rows * (l + 1), col:col + half] = zero
                lag_ref[rows * l:rows * (l + 1), col + half:col + 2 * half] = zero
    lhs = jnp.concatenate([b_ref[1, 0], -b_ref[1, 1], b_ref[0, 0], -b_ref[0, 1]], axis=1).astype(ldt)
    kall = _mm_nt(lhs, lag_ref[...])
    for j in range(S5_T):
        wi_ref[rows * j:rows * (j + 1), :] = _lane_window(kall, rows * (S5_T - 1 - j), S5_T * rows).astype(mdt)
    y = _mm(u, wi_ref[...])
    y = y + _mm_nt(sf_ref[0].astype(mdt), wt_ref[0])
    y = y + _mm_nt(sb_ref[0].astype(mdt), wt_ref[1])
    o_ref[0] = y + u.astype(F32) * skip_ref[...]


def _s5_read(u_pairs, sf, sb, pw_k, b_k, c_k, skip, layer):
    a, nch, wd = u_pairs.shape
    blk = lambda *shape: pl.BlockSpec((1,) + shape, lambda i: (i, 0, 0))
    return pl.pallas_call(
        _s5_read_kernel,
        grid=(a,),
        in_specs=[blk(nch, wd), blk(nch, S5_STATE_W), blk(nch, S5_STATE_W), _pair_spec(layer, *pw_k.shape[2:]),
                  _pair_spec(layer, *b_k.shape[2:]), _pair_spec(layer, *c_k.shape[2:]),
                  _pair_spec(layer, *skip.shape[2:])],
        out_specs=blk(nch, wd),
        out_shape=jax.ShapeDtypeStruct((a, nch, wd), F32),
        scratch_shapes=[pltpu.VMEM((2, wd, S5_STATE_W), u_pairs.dtype), pltpu.VMEM((wd, wd), u_pairs.dtype),
                        pltpu.VMEM((2 * wd, 2 * S5_STATE_W), u_pairs.dtype)],
        compiler_params=_params(("arbitrary",)),
        name="s5_read",
    )(u_pairs, sf, sb, pw_k, b_k, c_k, skip)


def _s5_mixer(u_pairs, weights, layer):
    pw_k, b_k, c_k, scan_tab, skip = weights
    drive = _s5_drive(u_pairs, pw_k, b_k, layer)
    sf, sb = _s5_scan(drive, scan_tab, layer)
    return _s5_read(u_pairs, sf, sb, pw_k, b_k, c_k, skip, layer)


def _layer_norm(x, g, b):
    mu = jnp.mean(x, axis=-1, keepdims=True)
    d = x - mu
    var = jnp.mean(d * d, axis=-1, keepdims=True)
    return d * lax.rsqrt(var + LN_EPS) * g + b


def _residual_rows(refs):
    if len(refs) == 1:
        return lambda rows: refs[0][rows, :]
    head_ref, body_ref = refs
    is_head = pl.program_id(0) == 0
    return lambda rows: jnp.where(is_head, head_ref[rows, :], body_ref[rows, :])


def _load_cast(src_hbm, dst_ref, stage_ref, sems, col_scale=None):
    chunk = stage_ref.shape[1]
    nchunk = src_hbm.shape[0] // chunk

    def copy(k):
        slot = k % 2
        return pltpu.make_async_copy(src_hbm.at[pl.ds(k * chunk, chunk), :], stage_ref.at[slot], sems.at[slot])

    copy(0).start()
    for k in range(nchunk):
        if k + 1 < nchunk:
            copy(k + 1).start()
        copy(k).wait()
        vals = stage_ref[k % 2]
        if col_scale is not None:
            vals = vals * col_scale
        dst_ref[k * chunk:(k + 1) * chunk, :] = vals.astype(dst_ref.dtype)


def _post_kernel(*refs, split, layer):
    nres = 2 if split else 1
    load_x = _residual_rows(refs[:nres])
    (att_ref, ret_ref, s5_ref, mod_ref, permt_ref, wglu_ref, bglu_ref, wo_hbm, g1_ref, b1_ref, w1_hbm, w2_hbm,
     g2_ref, b2_ref, o_ref, wo_ref, w1_ref, w2_ref, stage1_ref, stage2_ref, sem_ref) = refs[nres:]
    woa_ref = wo_ref.at[0:ATT_WIDTH]
    wor_ref = wo_ref.at[ATT_WIDTH:ATT_WIDTH + RET_WIDTH]
    wos_ref = wo_ref.at[ATT_WIDTH + RET_WIDTH:ATT_WIDTH + RET_WIDTH + S5_WIDTH]
    mdt = w1_ref.dtype

    @pl.when(pl.program_id(0) == 0)
    def _stage_weights():
        _load_cast(wo_hbm.at[layer], wo_ref, stage2_ref, sem_ref.at[1])
        _load_cast(w1_hbm.at[layer], w1_ref, stage1_ref, sem_ref.at[0])
        _load_cast(w2_hbm.at[layer], w2_ref, stage2_ref, sem_ref.at[1])
    sub = o_ref.shape[0] // POST_SPLIT
    csub = sub // S5_T
    nff = D_FF // FF_CHUNK

    def mix(part):
        rows = slice(sub * part, sub * (part + 1))
        zrows = []
        for i in range(S5_T):
            src_vreg, src_blk = divmod(i * S5_PAIR_W, LANES)
            src_blk //= S5_PAIR_W
            cols = [_lane_block_shuffle(
                lambda a: s5_ref[a, csub * part:csub * (part + 1), src_vreg * LANES:(src_vreg + 1) * LANES],
                src_blk, w) for w in range(S5_WIDTH // LANES)]
            zrows.append(jnp.concatenate(cols, axis=1))
        hs = jax.nn.gelu(jnp.concatenate(zrows, axis=0))
        gate = jax.nn.sigmoid(_mm(hs.astype(mdt), wglu_ref[...]) + bglu_ref[...])
        s5 = _mm(permt_ref[...], (hs * gate).astype(mdt)).astype(mdt)
        return _mm(att_ref[rows, :], woa_ref[...]) + _mm(ret_ref[rows, :], wor_ref[...]) + _mm(s5, wos_ref[...])

    def norm1(part, ox):
        rows = slice(sub * part, sub * (part + 1))
        x1 = _layer_norm(DEEPNORM_ALPHA * load_x(rows) + mod_ref[2:3, :] * ox, g1_ref[...], b1_ref[...])
        return x1, (x1 * (1.0 + mod_ref[4:5, :]) + mod_ref[3:4, :]).astype(mdt)

    def ff(h, c):
        a = _mm(h, w1_ref[:, FF_CHUNK * c:FF_CHUNK * (c + 1)])
        a = jnp.square(jnp.maximum(a, 0.0)).astype(mdt)
        return _mm(a, w2_ref[FF_CHUNK * c:FF_CHUNK * (c + 1), :])

    def norm2(part, x1, acc):
        rows = slice(sub * part, sub * (part + 1))
        o_ref[rows, :] = _layer_norm(DEEPNORM_ALPHA * x1 + mod_ref[5:6, :] * acc, g2_ref[...], b2_ref[...])

    nstage = nff + 3
    state = [dict() for _ in range(POST_SPLIT)]
    for part, stage in POST_PROGRAM:
        st = state[part]
        if stage == 0:
            st["ox"] = mix(part)
        elif stage == 1:
            st["x1"], st["h"] = norm1(part, st.pop("ox"))
        elif stage < nstage - 1:
            term = ff(st["h"], stage - 2)
            st["acc"] = term if stage == 2 else st["acc"] + term
        else:
            norm2(part, st["x1"], st["acc"])


def _residual_specs(residual):
    tm = ROW_TILE
    if len(residual) == 1:
        return [pl.BlockSpec((tm, D_MODEL), lambda i: (i, 0))]
    head_tiles = CTX_PAD // tm
    return [pl.BlockSpec((tm, D_MODEL), lambda i: (jnp.minimum(i, head_tiles - 1), 0)),
            pl.BlockSpec((tm, D_MODEL), lambda i: (jnp.maximum(i - head_tiles, 0), 0))]


def _post(residual, att, ret, s5_pairs, mods, layer, permt, wglu, bglu, wo, g1, b1, w1, w2, g2, b2, skip_context):
    rows = att.shape[0]
    tm = ROW_TILE
    off = CTX_PAD // tm if skip_context else 0
    assert not (skip_context and len(residual) > 1)
    row_blk = lambda width: pl.BlockSpec((tm, width), lambda i: (i + off, 0))
    full = lambda arr: pl.BlockSpec(arr.shape, lambda i: (0,) * arr.ndim)
    hbm = lambda arr: pl.BlockSpec(memory_space=pl.ANY)
    vec = lambda v: v.reshape(1, -1).astype(F32)
    small = [(permt, full), (wglu, full), (vec(bglu), full), (wo, hbm), (vec(g1), full), (vec(b1), full),
             (w1, hbm), (w2, hbm), (vec(g2), full), (vec(b2), full)]
    res_specs = [row_blk(D_MODEL)] if skip_context else _residual_specs(residual)
    return pl.pallas_call(
        functools.partial(_post_kernel, split=len(residual) > 1, layer=layer),
        grid=(rows // tm - off,),
        in_specs=res_specs + [row_blk(ATT_WIDTH), row_blk(RET_WIDTH),
                              pl.BlockSpec((S5_PAIRS, tm // S5_T, S5_T * S5_PAIR_W), lambda i: (0, i + off, 0)),
                              pl.BlockSpec((None, None, N_ADA, D_MODEL),
                                           lambda i: (layer, jnp.where(i + off == 0, 1, 0), 0, 0))]
                 + [spec(arr) for arr, spec in small],
        out_specs=pl.BlockSpec((tm, D_MODEL), lambda i: (i, 0)),
        out_shape=jax.ShapeDtypeStruct((rows - off * tm, D_MODEL), F32),
        scratch_shapes=[pltpu.VMEM((ATT_WIDTH + RET_WIDTH + S5_WIDTH, D_MODEL), MXU_DTYPE),
                        pltpu.VMEM((D_MODEL, D_FF), MXU_DTYPE), pltpu.VMEM((D_FF, D_MODEL), MXU_DTYPE),
                        pltpu.VMEM((2, STAGE_BYTES // (4 * D_FF), D_FF), F32),
                        pltpu.VMEM((2, STAGE_BYTES // (4 * D_MODEL), D_MODEL), F32),
                        pltpu.SemaphoreType.DMA((2, 2))],
        compiler_params=_params(("arbitrary",)),
        name="post",
    )(*residual, att, ret, s5_pairs, mods, *[arr for arr, _ in small])


def kernel(x, c, ctx, c_ctx, w_ada, b_ada, w_in, att_sink, ret_decay_logit, s5_lambda_re, s5_lambda_im, s5_b_re,
           s5_b_im, s5_c_re, s5_c_im, s5_log_dt, s5_d, w_glu, b_glu, w_out, ln1_g, ln1_b, w_ff1, w_ff2, ln2_g,
           ln2_b):
    assert x.shape[0] == 1 and x.shape[2] == D_MODEL and ctx.shape[1] == CTX_LEN
    seq = x.shape[1]
    assert seq % ROW_TILE == 0
    residual = (jnp.pad(ctx[0], ((0, CTX_PAD - CTX_LEN), (0, 0))), x[0])
    cond = jnp.zeros((8, D_MODEL), F32).at[0].set(c[0]).at[1].set(c_ctx)
    mods = _modulation(cond, w_ada, b_ada).reshape(DEPTH, 8, N_ADA, D_MODEL)
    tabs = _rope_tables(seq)
    masks = _attention_masks()
    perm = _chunk_perm(ROW_TILE // POST_SPLIT, MXU_DTYPE)
    permt = perm.T
    col_scale = jnp.ones((IN_WIDTH,), F32).at[COL_AQ:COL_AK].set(HEAD_DIM ** -0.5 * LOG2E)
    col_scale = col_scale.at[COL_RQ:COL_RK].set(HEAD_DIM ** -0.5)
    s5w = jax.vmap(_s5_weights)(s5_lambda_re, s5_lambda_im, s5_b_re, s5_b_im, s5_c_re, s5_c_im, s5_log_dt, s5_d)
    log_gamma = jax.nn.log_sigmoid(ret_decay_logit.astype(F32))
    for l in range(DEPTH):
        proj, u_pairs = _in_proj(residual, mods, l, w_in, col_scale.reshape(1, IN_WIDTH), tabs, perm)
        att = _attention(proj, att_sink[l].astype(F32) * LOG2E, masks)
        ret = _retention(proj, log_gamma[l])
        s5 = _s5_mixer(u_pairs, s5w, l)
        stream = _post(residual, att, ret, s5, mods, l, permt, w_glu[l].astype(MXU_DTYPE), b_glu[l],
                       w_out, ln1_g[l], ln1_b[l], w_ff1, w_ff2, ln2_g[l], ln2_b[l],
                       skip_context=(l == DEPTH - 1))
        residual = (stream,)
    return stream[None]
```

```python
import functools
import math

import jax
import jax.numpy as jnp
from jax import lax
from jax.experimental import pallas as pl
from jax.experimental.pallas import tpu as pltpu

F32 = jnp.float32
MXU_DTYPE = jnp.bfloat16

D_MODEL = 1024
DEPTH = 4
GRID_W = 64
CTX_LEN = 256
CTX_PAD = 512
HEAD_DIM = 64
ATT_HEADS = 8
ATT_KV_HEADS = 2
ATT_BLOCK = 128
ATT_LOOKAHEAD = 2
ROPE_BASE = 10000.0
RET_HEADS = 4
RET_CHUNK = 256
S5_CH = 16
S5_GROUPS = 16
S5_STATE = 64
S5_T = 16
S5_PAIRS = S5_GROUPS // 2
S5_PAIR_W = 2 * S5_CH
S5_TILE = CTX_PAD // S5_T
SCAN_ROWS = 8
S5_STATE_W = 2 * 2 * S5_STATE
ATT_WIDTH = ATT_HEADS * HEAD_DIM
KV_WIDTH = ATT_KV_HEADS * HEAD_DIM
RET_WIDTH = RET_HEADS * HEAD_DIM
S5_WIDTH = S5_GROUPS * S5_CH
IN_WIDTH = ATT_WIDTH + 2 * KV_WIDTH + 4 * RET_WIDTH + S5_WIDTH
D_FF = 4 * D_MODEL
FF_CHUNK = 1024
N_ADA = 6
LN_EPS = 1e-5
GN_EPS = 1e-5
DEEPNORM_ALPHA = (2 * DEPTH) ** 0.25
ROW_TILE = 512
POST_SPLIT = 2
_NFF = D_FF // FF_CHUNK
POST_PROGRAM = (((0, 0), (0, 1), (1, 0), (0, 2), (1, 1)) + tuple((0, 2 + c) for c in range(1, _NFF))
                + ((1, 2), (0, 2 + _NFF)) + tuple((1, 2 + c) for c in range(1, _NFF)) + ((1, 2 + _NFF),))
NEG_BIG = -1e30
LOG2E = math.log2(math.e)
LANES = 128
VMEM_LIMIT = 56 * 1024 * 1024
STAGE_BYTES = 2 * 1024 * 1024

COL_AQ, COL_AK, COL_AV = 0, ATT_WIDTH, ATT_WIDTH + KV_WIDTH
COL_RQ = ATT_WIDTH + 2 * KV_WIDTH
COL_RK, COL_RV, COL_RG = COL_RQ + RET_WIDTH, COL_RQ + 2 * RET_WIDTH, COL_RQ + 3 * RET_WIDTH
COL_S5 = COL_RQ + 4 * RET_WIDTH


def _mm(a, b):
    return jnp.dot(a, b, preferred_element_type=F32)


def _mm_nt(a, b):
    return lax.dot_general(a, b, (((1,), (1,)), ((), ())), preferred_element_type=F32)


def _mm_tn(a, b):
    return lax.dot_general(a, b, (((0,), (0,)), ((), ())), preferred_element_type=F32)


def _params(sem):
    return pltpu.CompilerParams(dimension_semantics=sem, vmem_limit_bytes=VMEM_LIMIT)


def _mod_kernel(cond_ref, w_ref, b_ref, o_ref):
    c = cond_ref[...]
    s = c * jax.nn.sigmoid(c)
    w = w_ref[0]
    split = lambda v: (v.astype(jnp.bfloat16), (v - v.astype(jnp.bfloat16).astype(F32)).astype(jnp.bfloat16))
    s_hi, s_lo = split(s)
    w_hi, w_lo = split(w)
    o_ref[0] = _mm(s_hi, w_hi) + (_mm(s_lo, w_hi) + _mm(s_hi, w_lo)) + b_ref[0]


def _modulation(cond, w_ada, b_ada):
    tn = 1536
    n = N_ADA * D_MODEL
    return pl.pallas_call(
        _mod_kernel,
        grid=(DEPTH, n // tn),
        in_specs=[
            pl.BlockSpec((8, D_MODEL), lambda l, j: (0, 0)),
            pl.BlockSpec((1, D_MODEL, tn), lambda l, j: (l, 0, j)),
            pl.BlockSpec((1, 1, tn), lambda l, j: (l, 0, j)),
        ],
        out_specs=pl.BlockSpec((1, 8, tn), lambda l, j: (l, 0, j)),
        out_shape=jax.ShapeDtypeStruct((DEPTH, 8, n), F32),
        compiler_params=_params(("arbitrary", "arbitrary")),
        name="modulation",
    )(cond, w_ada, b_ada.reshape(DEPTH, 1, n))


def _lane_block_shuffle(src_rows, src_lane_blk, out_vreg):
    acc = None
    for q in range(LANES // S5_PAIR_W):
        piece = src_rows(out_vreg * (LANES // S5_PAIR_W) + q)
        shift = (S5_PAIR_W * (q - src_lane_blk)) % LANES
        if shift:
            piece = pltpu.roll(piece, shift, 1)
        if acc is None:
            acc = piece
        else:
            lane_blk = lax.broadcasted_iota(jnp.int32, piece.shape, 1) // S5_PAIR_W
            acc = jnp.where(lane_blk == q, piece, acc)
    return acc


def _in_proj_kernel(*refs, split, layer):
    nres = 2 if split else 1
    (mod_ref, w_hbm, scale_ref, ca_ref, sa_ref, cr_ref, sr_ref, perm_ref, o_ref, u_ref, w_ref, stage_ref,
     sem_ref) = refs[nres:]

    @pl.when(pl.program_id(0) == 0)
    def _stage_weights():
        _load_cast(w_hbm.at[layer], w_ref, stage_ref, sem_ref, col_scale=scale_ref[...])

    x = _residual_rows(refs[:nres])(slice(None))
    h = (x * (1.0 + mod_ref[1:2, :]) + mod_ref[0:1, :]).astype(w_ref.dtype)
    lane = lax.broadcasted_iota(jnp.int32, (x.shape[0], LANES), 1)
    first_att = lane % (HEAD_DIM // 2) < HEAD_DIM // 4
    first_ret = lane % HEAD_DIM < HEAD_DIM // 2

    def proj(c0, c1):
        return _mm(h, w_ref[:, c0:c1])

    def rope_store(p, c0, width, cos, sin, first, half):
        for b in range(width // LANES):
            blk = p[:, LANES * b:LANES * (b + 1)]
            rot = jnp.where(first, pltpu.roll(blk, LANES - half, 1), pltpu.roll(blk, half, 1))
            o_ref[:, c0 + LANES * b:c0 + LANES * (b + 1)] = (blk * cos + rot * sin).astype(o_ref.dtype)

    def plain_store(p, c0, width):
        o_ref[:, c0:c0 + width] = p.astype(o_ref.dtype)

    u = proj(COL_S5, IN_WIDTH).astype(w_ref.dtype)
    sub = perm_ref.shape[0]
    nchunk = sub // S5_T
    for part in range(x.shape[0] // sub):
        g = _mm(perm_ref[...], u[sub * part:sub * (part + 1)])
        for a in range(S5_PAIRS):
            vreg_col, lane_blk = divmod(a * S5_PAIR_W, LANES)
            lane_blk //= S5_PAIR_W
            for v in range(S5_T * S5_PAIR_W // LANES):
                slab = _lane_block_shuffle(
                    lambda j: g[nchunk * j:nchunk * (j + 1), vreg_col * LANES:(vreg_col + 1) * LANES], lane_blk, v)
                u_ref[a, nchunk * part:nchunk * (part + 1), LANES * v:LANES * (v + 1)] = slab.astype(u_ref.dtype)

    ca, sa, cr, sr = ca_ref[...], sa_ref[...], cr_ref[...], sr_ref[...]
    att_rope = (ca, sa, first_att, HEAD_DIM // 4)
    ret_rope = (cr, sr, first_ret, HEAD_DIM // 2)
    groups = [(COL_AQ, ATT_WIDTH, att_rope), (COL_AK, KV_WIDTH, att_rope), (COL_RQ, RET_WIDTH, ret_rope),
              (COL_RK, RET_WIDTH, ret_rope), (COL_AV, KV_WIDTH, None), (COL_RV, COL_S5 - COL_RV, None)]
    pending = None
    for group in groups + [None]:
        nxt = None if group is None else (proj(group[0], group[0] + group[1]),) + group
        if pending is not None:
            p, pc0, pwidth, prope = pending
            if prope is None:
                plain_store(p, pc0, pwidth)
            else:
                rope_store(p, pc0, pwidth, *prope)
        pending = nxt


def _chunk_perm(tile_rows, dtype):
    nchunk = tile_rows // S5_T
    r = jnp.arange(tile_rows)
    src = S5_T * (r % nchunk) + r // nchunk
    return (src[:, None] == jnp.arange(tile_rows)[None, :]).astype(dtype)


def _in_proj(residual, mods, layer, w_in, col_scale, tabs, perm):
    rows = tabs[0].shape[0]
    tm = ROW_TILE
    tab_spec = pl.BlockSpec((tm, LANES), lambda i: (i, 0))
    nch = rows // S5_T
    return pl.pallas_call(
        functools.partial(_in_proj_kernel, split=len(residual) > 1, layer=layer),
        grid=(rows // tm,),
        in_specs=_residual_specs(residual) + [
            pl.BlockSpec((None, None, N_ADA, D_MODEL), lambda i: (layer, jnp.where(i == 0, 1, 0), 0, 0)),
            pl.BlockSpec(memory_space=pl.ANY),
            pl.BlockSpec((1, IN_WIDTH), lambda i: (0, 0)),
            tab_spec, tab_spec, tab_spec, tab_spec,
            pl.BlockSpec(perm.shape, lambda i: (0, 0)),
        ],
        out_specs=[pl.BlockSpec((tm, COL_S5), lambda i: (i, 0)),
                   pl.BlockSpec((S5_PAIRS, tm // S5_T, S5_T * S5_PAIR_W), lambda i: (0, i, 0))],
        out_shape=[jax.ShapeDtypeStruct((rows, COL_S5), MXU_DTYPE),
                   jax.ShapeDtypeStruct((S5_PAIRS, nch, S5_T * S5_PAIR_W), MXU_DTYPE)],
        scratch_shapes=[pltpu.VMEM((D_MODEL, IN_WIDTH), MXU_DTYPE),
                        pltpu.VMEM((2, STAGE_BYTES // (4 * IN_WIDTH), IN_WIDTH), F32),
                        pltpu.SemaphoreType.DMA((2,))],
        compiler_params=_params(("arbitrary",)),
        name="in_proj",
    )(*residual, mods, w_in, col_scale, *tabs, perm)


def _rope_tables(seq):
    half_a = HEAD_DIM // 4
    half_r = HEAD_DIM // 2
    nrow = seq // GRID_W
    inv_a = ROPE_BASE ** (-jnp.arange(half_a, dtype=F32) / half_a)
    inv_r = ROPE_BASE ** (-jnp.arange(half_r, dtype=F32) / half_r)
    ang_r = jnp.arange(nrow, dtype=F32)[:, None] * inv_a[None, :]
    ang_c = jnp.arange(GRID_W, dtype=F32)[:, None] * inv_a[None, :]
    ang_t = jnp.arange(seq, dtype=F32)[:, None] * inv_r[None, :]
    hp = lax.Precision.HIGHEST
    lane = jnp.arange(LANES)
    within = lane % HEAD_DIM
    pick_a = (within % half_a)[None, :] == jnp.arange(half_a)[:, None]
    exp_row = (pick_a & (within < 2 * half_a)[None, :]).astype(F32)
    exp_col = (pick_a & (within >= 2 * half_a)[None, :]).astype(F32)
    exp_t = ((lane % half_r)[None, :] == jnp.arange(half_r)[:, None]).astype(F32)
    sign_a = jnp.where(within % (2 * half_a) < half_a, -1.0, 1.0).astype(F32)
    sign_r = jnp.where(within < half_r, -1.0, 1.0).astype(F32)

    def att_table(fn):
        by_row = jnp.dot(fn(ang_r), exp_row, precision=hp)
        by_col = jnp.dot(fn(ang_c), exp_col, precision=hp)
        return (by_row[:, None, :] + by_col[None, :, :]).reshape(seq, LANES)

    cos_a = att_table(jnp.cos)
    sin_a = att_table(jnp.sin) * sign_a
    cos_r = jnp.dot(jnp.cos(ang_t), exp_t, precision=hp)
    sin_r = jnp.dot(jnp.sin(ang_t), exp_t, precision=hp) * sign_r
    pad = lambda tab, ident: jnp.pad(tab, ((CTX_PAD, 0), (0, 0)), constant_values=ident)
    return pad(cos_a, 1.0), pad(sin_a, 0.0), pad(cos_r, 1.0), pad(sin_r, 0.0)


def _swap_halves(x):
    if x.dtype.itemsize == 4:
        return pltpu.roll(x, 64, 1)
    packed = pltpu.bitcast(x, jnp.uint32)
    return pltpu.bitcast(pltpu.roll(packed, 64, 1), x.dtype)


def _dup_heads(x):
    sw = _swap_halves(x)
    lo = lax.broadcasted_iota(jnp.int32, x.shape, 1) < HEAD_DIM
    return jnp.where(lo, x, sw), jnp.where(lo, sw, x)


def _attn_kernel(sink_ref, q_ref, km_ref, kp_ref, kn_ref, vm_ref, vp_ref, vn_ref, kc_ref, vc_ref, mask_ref,
                 o_ref, k2_ref, v2_ref, kc2_ref, vc2_ref):
    i = pl.program_id(0)
    last_blk = pl.num_programs(0) * (ROW_TILE // ATT_BLOCK) - 1
    blk = ATT_BLOCK
    def spread(src, ones_upper):
        x = src[...]
        a, b = _dup_heads(x)
        if ones_upper:
            upper = lax.broadcasted_iota(jnp.int32, x.shape, 1) >= HEAD_DIM
            a = jnp.where(upper, jnp.ones_like(a), a)
            b = jnp.where(upper, jnp.ones_like(b), b)
        return a, b

    for dst, parts, is_v in ((k2_ref, (kp_ref, km_ref, kn_ref), False), (v2_ref, (vp_ref, vm_ref, vn_ref), True)):
        row = 0
        for part in parts:
            a, b = spread(part, is_v)
            n = part.shape[0]
            dst[0, row:row + n, :] = a
            dst[1, row:row + n, :] = b
            row += n
    for dst, src, is_v in ((kc2_ref, kc_ref, False), (vc2_ref, vc_ref, True)):
        a, b = spread(src, is_v)
        dst[0] = a
        dst[1] = b

    lo = lax.broadcasted_iota(jnp.int32, (blk, LANES), 1) < HEAD_DIM
    group = ATT_HEADS // ATT_KV_HEADS

    nloc = 3 * blk

    def scores(j, kv):
        r0 = j * blk
        qt = q_ref[r0:r0 + blk, group * HEAD_DIM * kv:group * HEAD_DIM * (kv + 1)]
        parts = []
        for g in range(group):
            qc = qt[:, LANES * (g // 2):LANES * (g // 2 + 1)]
            keep = lo if g % 2 == 0 else jnp.logical_not(lo)
            parts.append(jnp.where(keep, qc, jnp.zeros_like(qc)))
        qs = jnp.concatenate(parts, axis=0)
        return _mm_nt(qs, k2_ref[kv, r0:r0 + nloc, :]), _mm_nt(qs, kc2_ref[kv])

    def finish(j, kv, s_loc, s_ctx):
        r0 = j * blk
        gblk = i * (ROW_TILE // blk) + j
        sel = jnp.where(i == 0, 3, jnp.where(gblk == CTX_PAD // blk, 0, jnp.where(gblk == last_blk, 2, 1)))
        bias = mask_ref[sel]
        probs, sink_w = [], []
        for g in range(group):
            s = jnp.concatenate([s_loc[blk * g:blk * (g + 1)] + bias, s_ctx[blk * g:blk * (g + 1)]], axis=1)
            sk = sink_ref[group * kv + g]
            m = jnp.maximum(jnp.max(s, axis=-1, keepdims=True), sk)
            probs.append(jnp.exp2(s - m).astype(o_ref.dtype))
            sink_w.append(jnp.exp2(sk - m))
        p = jnp.concatenate(probs, axis=0)
        o = _mm(p[:, :nloc], v2_ref[kv, r0:r0 + nloc, :]) + _mm(p[:, nloc:], vc2_ref[kv])
        for half in range(group // 2):
            even, odd = 2 * half, 2 * half + 1
            oe = o[blk * even:blk * (even + 1)]
            oo = o[blk * odd:blk * (odd + 1)]
            y_even = oe * (1.0 / (pltpu.roll(oe, HEAD_DIM, 1) + sink_w[even]))
            y_odd = pltpu.roll(oo, HEAD_DIM, 1) * (1.0 / (oo + sink_w[odd]))
            c0 = group * HEAD_DIM * kv + LANES * half
            o_ref[r0:r0 + blk, c0:c0 + LANES] = jnp.where(lo, y_even, y_odd).astype(o_ref.dtype)

    items = [(j, kv) for j in range(ROW_TILE // blk) for kv in range(ATT_KV_HEADS)]
    pending = {}
    for t in range(len(items) + ATT_LOOKAHEAD):
        if t < len(items):
            pending[t] = scores(*items[t])
        if t >= ATT_LOOKAHEAD:
            finish(*items[t - ATT_LOOKAHEAD], *pending.pop(t - ATT_LOOKAHEAD))


def _attention_masks():
    qi = jnp.arange(ATT_BLOCK)[:, None]
    kj = jnp.arange(3 * ATT_BLOCK)[None, :]
    band = jnp.abs(kj - ATT_BLOCK - qi) <= ATT_BLOCK
    first = band & (kj >= ATT_BLOCK)
    last = band & (kj < 2 * ATT_BLOCK)
    none = jnp.zeros_like(band)
    masks = jnp.stack([first, band, last, none])
    return jnp.where(masks, 0.0, NEG_BIG).astype(F32)


def _attention(proj, sink, masks):
    rows = proj.shape[0]
    tm, blk = ROW_TILE, ATT_BLOCK
    per = tm // blk
    nblk = rows // blk
    ck, cv = COL_AK // KV_WIDTH, COL_AV // KV_WIDTH
    dt = proj.dtype
    return pl.pallas_call(
        _attn_kernel,
        grid=(rows // tm,),
        in_specs=[
            pl.BlockSpec(memory_space=pltpu.SMEM),
            pl.BlockSpec((tm, ATT_WIDTH), lambda i: (i, 0)),
            pl.BlockSpec((tm, KV_WIDTH), lambda i: (i, ck)),
            pl.BlockSpec((blk, KV_WIDTH), lambda i: (jnp.maximum(i * per - 1, 0), ck)),
            pl.BlockSpec((blk, KV_WIDTH), lambda i: (jnp.minimum((i + 1) * per, nblk - 1), ck)),
            pl.BlockSpec((tm, KV_WIDTH), lambda i: (i, cv)),
            pl.BlockSpec((blk, KV_WIDTH), lambda i: (jnp.maximum(i * per - 1, 0), cv)),
            pl.BlockSpec((blk, KV_WIDTH), lambda i: (jnp.minimum((i + 1) * per, nblk - 1), cv)),
            pl.BlockSpec((CTX_LEN, KV_WIDTH), lambda i: (0, ck)),
            pl.BlockSpec((CTX_LEN, KV_WIDTH), lambda i: (0, cv)),
            pl.BlockSpec((4, blk, 3 * blk), lambda i: (0, 0, 0)),
        ],
        out_specs=pl.BlockSpec((tm, ATT_WIDTH), lambda i: (i, 0)),
        out_shape=jax.ShapeDtypeStruct((rows, ATT_WIDTH), dt),
        scratch_shapes=[
            pltpu.VMEM((2, tm + 2 * blk, KV_WIDTH), dt),
            pltpu.VMEM((2, tm + 2 * blk, KV_WIDTH), dt),
            pltpu.VMEM((2, CTX_LEN, KV_WIDTH), dt),
            pltpu.VMEM((2, CTX_LEN, KV_WIDTH), dt),
        ],
        compiler_params=_params(("arbitrary",)),
        name="attention",
    )(sink, proj, proj, proj, proj, proj, proj, proj, proj, proj, masks)


def _ret_kernel(*refs, sup, ntile):
    lg_ref, q_ref = refs[:2]
    k_refs, v_refs = refs[2:2 + sup], refs[2 + sup:2 + 2 * sup]
    g_ref, o_ref, sb_ref, s_ref, dm_ref, tab_ref, gbd_ref = refs[2 + 2 * sup:]
    k_ref, v_ref = k_refs[0], v_refs[0]
    step = pl.program_id(0)
    nsup = (ntile - 1) // sup
    ph = jnp.where(step <= nsup, 0, 1)
    t = jnp.where(step <= nsup, step, step - (nsup + 1))
    c = RET_CHUNK
    w = RET_WIDTH
    per = q_ref.shape[0] // c
    mdt = sb_ref.dtype
    rows = lambda ci: slice(c * ci, c * (ci + 1))

    def lane_vec(direction, shape, axis):
        head = lax.broadcasted_iota(jnp.int32, shape, axis) // HEAD_DIM
        out = jnp.full(shape, lg_ref[direction, RET_HEADS - 1], F32)
        for h in range(RET_HEADS - 2, -1, -1):
            out = jnp.where(head == h, lg_ref[direction, h], out)
        return out

    @pl.when(jnp.logical_and(ph == 0, t == 0))
    def _init_tables():
        diff = (lax.broadcasted_iota(jnp.int32, (c, c), 0) - lax.broadcasted_iota(jnp.int32, (c, c), 1)).astype(F32)
        for h in range(RET_HEADS):
            dm_ref[h] = jnp.exp(jnp.where(diff >= 0, diff * lg_ref[0, h], -diff * lg_ref[1, h]))
        pos = lax.broadcasted_iota(jnp.int32, (c, w), 0).astype(F32)
        lgf = lane_vec(0, (c, w), 1)
        lgb = lane_vec(1, (c, w), 1)
        tab_ref[0] = jnp.exp((c - 1.0 - pos) * lgf)
        tab_ref[1] = jnp.exp((pos + 1.0) * lgf)
        tab_ref[2] = jnp.exp(pos * lgb)
        tab_ref[3] = jnp.exp((c - pos) * lgb)
        same = (lax.broadcasted_iota(jnp.int32, (w, w), 0) // HEAD_DIM
                == lax.broadcasted_iota(jnp.int32, (w, w), 1) // HEAD_DIM)
        bd = jnp.where(same, 1.0, 0.0)
        gbd_ref[0] = bd * jnp.exp(c * lane_vec(0, (w, w), 0))
        gbd_ref[1] = bd * jnp.exp(c * lane_vec(1, (w, w), 0))
        gbd_ref[2] = bd

    @pl.when(t == 0)
    def _reset_state():
        s_ref[...] = jnp.zeros_like(s_ref)

    def state_update(direction, key_tab, ci, kr=k_ref, vr=v_ref):
        kw = (kr[rows(ci), :].astype(F32) * tab_ref[key_tab]).astype(mdt)
        u = _mm_tn(kw, vr[rows(ci), :])
        s_ref[...] = gbd_ref[direction] * s_ref[...] + gbd_ref[2] * u

    @pl.when(jnp.logical_and(ph == 0, t == 0))
    def _backward_context():
        sb_ref[0] = s_ref[...].astype(mdt)
        state_update(1, 2, 0)

    @pl.when(jnp.logical_and(ph == 0, t > 0))
    def _backward_latent():
        base = 1 + per * sup * (nsup - t)
        for qi in range(sup - 1, -1, -1):
            for ci in range(per - 1, -1, -1):
                sb_ref[base + per * qi + ci] = s_ref[...].astype(mdt)
                state_update(1, 2, ci, k_refs[qi], v_refs[qi])

    head = lax.broadcasted_iota(jnp.int32, (c, w), 1) // HEAD_DIM

    def scores(ci):
        q = q_ref[rows(ci), :]
        qs = jnp.concatenate([jnp.where(head == h, q, jnp.zeros_like(q)) for h in range(RET_HEADS)], axis=0)
        return _mm_nt(qs, k_ref[rows(ci), :])

    def intra(ci, sc):
        scd = jnp.concatenate([sc[c * h:c * (h + 1)] * dm_ref[h] for h in range(RET_HEADS)], axis=0).astype(mdt)
        oi = _mm(scd, v_ref[rows(ci), :])
        o = jnp.where(head == 0, oi[0:c], 0.0)
        for h in range(1, RET_HEADS):
            o = o + jnp.where(head == h, oi[c * h:c * (h + 1)], 0.0)
        return o

    def cross(ci, idx):
        qf = q_ref[rows(ci), :].astype(F32)
        return (_mm((qf * tab_ref[1]).astype(mdt), s_ref[...].astype(mdt))
                + _mm((qf * tab_ref[3]).astype(mdt), sb_ref[idx]))

    def finish(ci, o):
        avg = (gbd_ref[2] * (1.0 / HEAD_DIM)).astype(mdt)
        o_hi = o.astype(mdt)
        d = o - (_mm(o_hi, avg) + _mm((o - o_hi.astype(F32)).astype(mdt), avg))
        var = _mm((d * d).astype(mdt), avg)
        gate = g_ref[rows(ci), :].astype(F32)
        y = d * lax.rsqrt(var + GN_EPS) * (gate * jax.nn.sigmoid(gate))
        o_ref[rows(ci), :] = y.astype(o_ref.dtype)

    @pl.when(jnp.logical_and(ph == 1, t == 0))
    def _forward_context():
        o = intra(0, scores(0)) + cross(0, 0)
        state_update(0, 0, 0)
        finish(0, o)
        for ci in range(1, per):
            o_ref[rows(ci), :] = jnp.zeros((c, w), o_ref.dtype)

    @pl.when(jnp.logical_and(ph == 1, t > 0))
    def _forward_latent():
        base = 1 + per * (t - 1)
        sc = [scores(ci) for ci in range(per)]
        outs = [intra(ci, sc[ci]) for ci in range(per)]
        for ci in range(per):
            outs[ci] = outs[ci] + cross(ci, base + ci)
            state_update(0, 0, ci)
        for ci in range(per):
            finish(ci, outs[ci])


def _retention(proj, log_gamma):
    rows = proj.shape[0]
    c = RET_CHUNK
    tm = ROW_TILE
    ntile = rows // tm
    nchunk = 1 + (rows - CTX_PAD) // c
    dt = proj.dtype
    cq, ckk, cvv, cg = (COL_RQ // RET_WIDTH, COL_RK // RET_WIDTH, COL_RV // RET_WIDTH, COL_RG // RET_WIDTH)

    nlat = ntile - 1
    sup = next(s for s in (4, 2, 1) if nlat % s == 0)
    nsup = nlat // sup
    nback = 1 + nsup

    def fw_blk(step):
        return jnp.maximum(step - nback, 0)

    def kv_spec(col, qi):
        def index(step):
            back = jnp.where(step == 0, 0, 1 + sup * (nsup - jnp.minimum(step, nsup)) + qi)
            fwd = step - nback if qi == 0 else 1 + qi
            return jnp.where(step < nback, back, fwd), col
        return pl.BlockSpec((tm, RET_WIDTH), index)

    return pl.pallas_call(
        functools.partial(_ret_kernel, sup=sup, ntile=ntile),
        grid=(nback + ntile,),
        in_specs=[pl.BlockSpec(memory_space=pltpu.SMEM),
                  pl.BlockSpec((tm, RET_WIDTH), lambda step: (fw_blk(step), cq))]
                 + [kv_spec(ckk, qi) for qi in range(sup)] + [kv_spec(cvv, qi) for qi in range(sup)]
                 + [pl.BlockSpec((tm, RET_WIDTH), lambda step: (fw_blk(step), cg))],
        out_specs=pl.BlockSpec((tm, RET_WIDTH), lambda step: (fw_blk(step), 0)),
        out_shape=jax.ShapeDtypeStruct((rows, RET_WIDTH), dt),
        scratch_shapes=[
            pltpu.VMEM((nchunk, RET_WIDTH, RET_WIDTH), dt),
            pltpu.VMEM((RET_WIDTH, RET_WIDTH), F32),
            pltpu.VMEM((RET_HEADS, c, c), F32),
            pltpu.VMEM((4, c, RET_WIDTH), F32),
            pltpu.VMEM((3, RET_WIDTH, RET_WIDTH), F32),
        ],
        compiler_params=_params(("arbitrary",)),
        name="retention",
    )(log_gamma, proj, *([proj] * (2 * sup)), proj)


def _s5_weights(lam_re, lam_im, b_re, b_im, c_re, c_im, log_dt, d_skip):
    tt, g, n, p, a = S5_T, S5_GROUPS, S5_STATE, S5_CH, S5_PAIRS
    lam = lax.complex(lam_re.astype(F32), lam_im.astype(F32))
    dtv = jnp.exp(log_dt.astype(F32))[..., None]
    lam_bar = jnp.exp(lam * dtv)
    bbar = ((lam_bar - 1.0) / lam)[..., None] * lax.complex(b_re.astype(F32), b_im.astype(F32))
    cmat = lax.complex(c_re.astype(F32), c_im.astype(F32))
    pw = [jnp.ones_like(lam_bar)]
    for _ in range(tt):
        pw.append(pw[-1] * lam_bar)
    pw = jnp.stack(pw, axis=1)
    eye2 = jnp.eye(2, dtype=F32)
    ri = lambda z, axis: jnp.stack([jnp.real(z), jnp.imag(z)], axis=axis)

    pw_l = pw.reshape(2, tt + 1, a, 2 * n)
    bbt = jnp.einsum('dahpn,gh->dagphn', jnp.swapaxes(bbar, -1, -2).reshape(2, a, 2, p, n), eye2)
    bbt = bbt.reshape(2, a, 2 * p, 2 * n)
    cct = jnp.einsum('dahpn,gh->dagphn', cmat.reshape(2, a, 2, p, n), eye2).reshape(2, a, 2 * p, 2 * n)
    pw_k = ri(pw_l, 1).transpose(3, 0, 1, 2, 4)
    b_k = ri(bbt, 2).transpose(1, 0, 2, 3, 4)
    c_k = ri(cct, 2).transpose(1, 0, 2, 3, 4)
    decay = [pw_l[:, tt]]
    for _ in range(SCAN_ROWS - 1):
        decay.append(decay[-1] * decay[0])
    decay = jnp.stack(decay, axis=0)
    rows8 = lambda z: jnp.broadcast_to(z[None], (SCAN_ROWS,) + z.shape)
    carry_w = jnp.stack([decay[:, 0], decay[::-1, 1]], axis=1)
    scan_tab = jnp.stack([rows8(decay[0]), rows8(decay[1]), rows8(decay[3]), carry_w], axis=0)
    scan_tab = ri(scan_tab, 0).transpose(4, 3, 1, 0, 2, 5)
    skip = jnp.tile(d_skip.astype(F32).reshape(a, 1, 2 * p), (1, tt, 1)).reshape(a, 1, tt * 2 * p)
    return pw_k, b_k, c_k, scan_tab, skip


def _pair_spec(layer, *shape):
    return pl.BlockSpec((None, None) + shape, lambda i: (layer, i) + (0,) * len(shape))


def _s5_drive_kernel(u_ref, pw_ref, b_ref, o_ref, w_ref):
    rows, half = S5_PAIR_W, LANES
    for d in range(2):
        br, bi = b_ref[d, 0], b_ref[d, 1]
        for j in range(S5_T):
            e = S5_T - 1 - j if d == 0 else j
            pr, pi = pw_ref[d, 0, e:e + 1, :], pw_ref[d, 1, e:e + 1, :]
            w_ref[rows * j:rows * (j + 1), 2 * half * d:2 * half * d + half] = (pr * br - pi * bi).astype(w_ref.dtype)
            w_ref[rows * j:rows * (j + 1), 2 * half * d + half:2 * half * (d + 1)] = (
                pr * bi + pi * br).astype(w_ref.dtype)
    o_ref[0] = _mm(u_ref[0], w_ref[...])


def _s5_drive(u_pairs, pw_k, b_k, layer):
    a, nch, wd = u_pairs.shape
    return pl.pallas_call(
        _s5_drive_kernel,
        grid=(a,),
        in_specs=[pl.BlockSpec((1, nch, wd), lambda i: (i, 0, 0)),
                  _pair_spec(layer, *pw_k.shape[2:]), _pair_spec(layer, *b_k.shape[2:])],
        out_specs=pl.BlockSpec((1, nch, 2 * S5_STATE_W), lambda i: (i, 0, 0)),
        out_shape=jax.ShapeDtypeStruct((a, nch, 2 * S5_STATE_W), F32),
        scratch_shapes=[pltpu.VMEM((wd, 2 * S5_STATE_W), u_pairs.dtype)],
        compiler_params=_params(("arbitrary",)),
        name="s5_drive",
    )(u_pairs, pw_k, b_k)


def _s5_scan_kernel(ef_ref, eb_ref, tab_ref, sf_ref, sb_ref, st_ref):
    t = pl.program_id(0)
    hw = LANES
    sub = SCAN_ROWS
    npair = tab_ref.shape[0]
    row = lax.broadcasted_iota(jnp.int32, (sub, hw), 0)

    @pl.when(t == 0)
    def _reset():
        st_ref[...] = jnp.zeros_like(st_ref)

    def shift(x, k, reverse):
        if reverse:
            return jnp.where(row < sub - k, pltpu.roll(x, sub - k, 0), 0.0)
        return jnp.where(row >= k, pltpu.roll(x, k, 0), 0.0)

    def scan_group(e_ref, a, d, r0, cr, ci):
        reverse = d == 1
        xr = e_ref[a, pl.ds(r0, sub), 0:hw]
        xi = e_ref[a, pl.ds(r0, sub), hw:2 * hw]
        for step, k in enumerate((1, 2, 4)):
            ar, ai = tab_ref[a, d, step, 0], tab_ref[a, d, step, 1]
            sr, si = shift(xr, k, reverse), shift(xi, k, reverse)
            xr, xi = xr + ar * sr - ai * si, xi + ar * si + ai * sr
        wr, wi = tab_ref[a, d, 3, 0], tab_ref[a, d, 3, 1]
        fr = xr + wr * cr - wi * ci
        fi = xi + wr * ci + wi * cr
        edge = sub - 1 if reverse else 0
        before = (jnp.where(row == edge, cr, shift(fr, 1, reverse)), jnp.where(row == edge, ci, shift(fi, 1, reverse)))
        last = 0 if reverse else sub - 1
        after = (jnp.broadcast_to(fr[last:last + 1], (sub, hw)), jnp.broadcast_to(fi[last:last + 1], (sub, hw)))
        return before, after

    def store(o_ref, a, r0, lower, upper):
        for part in range(2):
            val = jnp.concatenate([lower[part], upper[part]], axis=0)
            o_ref[a, pl.ds(r0, 2 * sub), part * hw:(part + 1) * hw] = val.astype(o_ref.dtype)

    def run(nrows):
        span = 2 * sub
        nspan = nrows // span

        def body(gi, carry):
            r0 = pl.multiple_of(gi * span, span)
            rb0 = pl.multiple_of((nspan - 1 - gi) * span, span)
            new = []
            for a in range(npair):
                fr, fi, br, bi = carry[a]
                f_lo, (fr, fi) = scan_group(ef_ref, a, 0, r0, fr, fi)
                f_hi, (fr, fi) = scan_group(ef_ref, a, 0, r0 + sub, fr, fi)
                store(sf_ref, a, r0, f_lo, f_hi)
                b_hi, (br, bi) = scan_group(eb_ref, a, 1, rb0 + sub, br, bi)
                b_lo, (br, bi) = scan_group(eb_ref, a, 1, rb0, br, bi)
                store(sb_ref, a, rb0, b_lo, b_hi)
                new.append((fr, fi, br, bi))
            return tuple(new)

        init = tuple(tuple(st_ref[a, k] for k in range(4)) for a in range(npair))
        final = lax.fori_loop(0, nspan, body, init)
        for a in range(npair):
            for k in range(4):
                st_ref[a, k] = final[a][k]

    @pl.when(t == 0)
    def _context():
        sf_ref[...] = jnp.zeros_like(sf_ref)
        sb_ref[...] = jnp.zeros_like(sb_ref)
        run(CTX_LEN // S5_T)

    @pl.when(t > 0)
    def _latent():
        run(S5_TILE)


def _s5_scan(drive, scan_tab, layer):
    a, nch, wd2 = drive.shape
    wd = wd2 // 2
    nt = nch // S5_TILE

    def bwd(t):
        return jnp.where(t == 0, 0, nt - t)

    return pl.pallas_call(
        _s5_scan_kernel,
        grid=(nt,),
        in_specs=[pl.BlockSpec((a, S5_TILE, wd), lambda t: (0, t, 0)),
                  pl.BlockSpec((a, S5_TILE, wd), lambda t: (0, bwd(t), 1)),
                  pl.BlockSpec((None,) + scan_tab.shape[1:], lambda t: (layer,) + (0,) * (scan_tab.ndim - 1))],
        out_specs=[pl.BlockSpec((a, S5_TILE, wd), lambda t: (0, t, 0)),
                   pl.BlockSpec((a, S5_TILE, wd), lambda t: (0, bwd(t), 0))],
        out_shape=[jax.ShapeDtypeStruct((a, nch, wd), MXU_DTYPE), jax.ShapeDtypeStruct((a, nch, wd), MXU_DTYPE)],
        scratch_shapes=[pltpu.VMEM((a, 4, SCAN_ROWS, LANES), F32)],
        compiler_params=_params(("arbitrary",)),
        name="s5_scan",
    )(drive, drive, scan_tab)


def _lane_window(x, start, width):
    cols = []
    for v in range(width // LANES):
        k0, off = divmod(start + LANES * v, LANES)
        lo = x[:, LANES * k0:LANES * (k0 + 1)]
        if off:
            hi = x[:, LANES * (k0 + 1):LANES * (k0 + 2)]
            lane = lax.broadcasted_iota(jnp.int32, lo.shape, 1)
            lo = jnp.where(lane < LANES - off, pltpu.roll(lo, LANES - off, 1), pltpu.roll(hi, LANES - off, 1))
        cols.append(lo)
    return jnp.concatenate(cols, axis=1)


def _s5_read_kernel(u_ref, sf_ref, sb_ref, pw_ref, b_ref, c_ref, skip_ref, o_ref, wt_ref, wi_ref, lag_ref):
    u = u_ref[0]
    mdt = u.dtype
    rows, half = S5_PAIR_W, LANES
    for d in range(2):
        cr, ci = c_ref[d, 0], c_ref[d, 1]
        for i in range(S5_T):
            e = i + 1 if d == 0 else S5_T - i
            pr, pi = pw_ref[d, 0, e:e + 1, :], pw_ref[d, 1, e:e + 1, :]
            wt_ref[d, rows * i:rows * (i + 1), 0:half] = (pr * cr - pi * ci).astype(mdt)
            wt_ref[d, rows * i:rows * (i + 1), half:2 * half] = (-(pr * ci + pi * cr)).astype(mdt)
    nlag = 2 * S5_T - 1
    ldt = lag_ref.dtype
    zero = jnp.zeros((rows, half), ldt)
    for l in range(nlag + 1):
        lag = l - (S5_T - 1)
        for d, active in ((1, lag <= 0), (0, 0 <= lag < S5_T)):
            col = 2 * half * (1 - d)
            if active:
                cr, ci = c_ref[d, 0], c_ref[d, 1]
                pr, pi = pw_ref[d, 0, abs(lag):abs(lag) + 1, :], pw_ref[d, 1, abs(lag):abs(lag) + 1, :]
                lag_ref[rows * l:rows * (l + 1), col:col + half] = (pr * cr - pi * ci).astype(ldt)
                lag_ref[rows * l:rows * (l + 1), col + half:col + 2 * half] = (pr * ci + pi * cr).astype(ldt)
            else:
                lag_ref[rows * l:rows * (l + 1), col:col + half] = zero
                lag_ref[rows * l:rows * (l + 1), col + half:col + 2 * half] = zero
    lhs = jnp.concatenate([b_ref[1, 0], -b_ref[1, 1], b_ref[0, 0], -b_ref[0, 1]], axis=1).astype(ldt)
    kall = _mm_nt(lhs, lag_ref[...])
    for j in range(S5_T):
        wi_ref[rows * j:rows * (j + 1), :] = _lane_window(kall, rows * (S5_T - 1 - j), S5_T * rows).astype(mdt)
    y = _mm(u, wi_ref[...])
    y = y + _mm_nt(sf_ref[0].astype(mdt), wt_ref[0])
    y = y + _mm_nt(sb_ref[0].astype(mdt), wt_ref[1])
    o_ref[0] = y + u.astype(F32) * skip_ref[...]


def _s5_read(u_pairs, sf, sb, pw_k, b_k, c_k, skip, layer):
    a, nch, wd = u_pairs.shape
    blk = lambda *shape: pl.BlockSpec((1,) + shape, lambda i: (i, 0, 0))
    return pl.pallas_call(
        _s5_read_kernel,
        grid=(a,),
        in_specs=[blk(nch, wd), blk(nch, S5_STATE_W), blk(nch, S5_STATE_W), _pair_spec(layer, *pw_k.shape[2:]),
                  _pair_spec(layer, *b_k.shape[2:]), _pair_spec(layer, *c_k.shape[2:]),
                  _pair_spec(layer, *skip.shape[2:])],
        out_specs=blk(nch, wd),
        out_shape=jax.ShapeDtypeStruct((a, nch, wd), F32),
        scratch_shapes=[pltpu.VMEM((2, wd, S5_STATE_W), u_pairs.dtype), pltpu.VMEM((wd, wd), u_pairs.dtype),
                        pltpu.VMEM((2 * wd, 2 * S5_STATE_W), u_pairs.dtype)],
        compiler_params=_params(("arbitrary",)),
        name="s5_read",
    )(u_pairs, sf, sb, pw_k, b_k, c_k, skip)


def _s5_mixer(u_pairs, weights, layer):
    pw_k, b_k, c_k, scan_tab, skip = weights
    drive = _s5_drive(u_pairs, pw_k, b_k, layer)
    sf, sb = _s5_scan(drive, scan_tab, layer)
    return _s5_read(u_pairs, sf, sb, pw_k, b_k, c_k, skip, layer)


def _layer_norm(x, g, b):
    mu = jnp.mean(x, axis=-1, keepdims=True)
    d = x - mu
    var = jnp.mean(d * d, axis=-1, keepdims=True)
    return d * lax.rsqrt(var + LN_EPS) * g + b


def _residual_rows(refs):
    if len(refs) == 1:
        return lambda rows: refs[0][rows, :]
    head_ref, body_ref = refs
    is_head = pl.program_id(0) == 0
    return lambda rows: jnp.where(is_head, head_ref[rows, :], body_ref[rows, :])


def _load_cast(src_hbm, dst_ref, stage_ref, sems, col_scale=None):
    chunk = stage_ref.shape[1]
    nchunk = src_hbm.shape[0] // chunk

    def copy(k):
        slot = k % 2
        return pltpu.make_async_copy(src_hbm.at[pl.ds(k * chunk, chunk), :], stage_ref.at[slot], sems.at[slot])

    copy(0).start()
    for k in range(nchunk):
        if k + 1 < nchunk:
            copy(k + 1).start()
        copy(k).wait()
        vals = stage_ref[k % 2]
        if col_scale is not None:
            vals = vals * col_scale
        dst_ref[k * chunk:(k + 1) * chunk, :] = vals.astype(dst_ref.dtype)


def _post_kernel(*refs, split, layer):
    nres = 2 if split else 1
    load_x = _residual_rows(refs[:nres])
    (att_ref, ret_ref, s5_ref, mod_ref, permt_ref, wglu_ref, bglu_ref, wo_hbm, g1_ref, b1_ref, w1_hbm, w2_hbm,
     g2_ref, b2_ref, o_ref, wo_ref, w1_ref, w2_ref, stage1_ref, stage2_ref, sem_ref) = refs[nres:]
    woa_ref = wo_ref.at[0:ATT_WIDTH]
    wor_ref = wo_ref.at[ATT_WIDTH:ATT_WIDTH + RET_WIDTH]
    wos_ref = wo_ref.at[ATT_WIDTH + RET_WIDTH:ATT_WIDTH + RET_WIDTH + S5_WIDTH]
    mdt = w1_ref.dtype

    @pl.when(pl.program_id(0) == 0)
    def _stage_weights():
        _load_cast(wo_hbm.at[layer], wo_ref, stage2_ref, sem_ref.at[1])
        _load_cast(w1_hbm.at[layer], w1_ref, stage1_ref, sem_ref.at[0])
        _load_cast(w2_hbm.at[layer], w2_ref, stage2_ref, sem_ref.at[1])
    sub = o_ref.shape[0] // POST_SPLIT
    csub = sub // S5_T
    nff = D_FF // FF_CHUNK

    def mix(part):
        rows = slice(sub * part, sub * (part + 1))
        zrows = []
        for i in range(S5_T):
            src_vreg, src_blk = divmod(i * S5_PAIR_W, LANES)
            src_blk //= S5_PAIR_W
            cols = [_lane_block_shuffle(
                lambda a: s5_ref[a, csub * part:csub * (part + 1), src_vreg * LANES:(src_vreg + 1) * LANES],
                src_blk, w) for w in range(S5_WIDTH // LANES)]
            zrows.append(jnp.concatenate(cols, axis=1))
        hs = jax.nn.gelu(jnp.concatenate(zrows, axis=0))
        gate = jax.nn.sigmoid(_mm(hs.astype(mdt), wglu_ref[...]) + bglu_ref[...])
        s5 = _mm(permt_ref[...], (hs * gate).astype(mdt)).astype(mdt)
        return _mm(att_ref[rows, :], woa_ref[...]) + _mm(ret_ref[rows, :], wor_ref[...]) + _mm(s5, wos_ref[...])

    def norm1(part, ox):
        rows = slice(sub * part, sub * (part + 1))
        x1 = _layer_norm(DEEPNORM_ALPHA * load_x(rows) + mod_ref[2:3, :] * ox, g1_ref[...], b1_ref[...])
        return x1, (x1 * (1.0 + mod_ref[4:5, :]) + mod_ref[3:4, :]).astype(mdt)

    def ff(h, c):
        a = _mm(h, w1_ref[:, FF_CHUNK * c:FF_CHUNK * (c + 1)])
        a = jnp.square(jnp.maximum(a, 0.0)).astype(mdt)
        return _mm(a, w2_ref[FF_CHUNK * c:FF_CHUNK * (c + 1), :])

    def norm2(part, x1, acc):
        rows = slice(sub * part, sub * (part + 1))
        o_ref[rows, :] = _layer_norm(DEEPNORM_ALPHA * x1 + mod_ref[5:6, :] * acc, g2_ref[...], b2_ref[...])

    nstage = nff + 3
    state = [dict() for _ in range(POST_SPLIT)]
    for part, stage in POST_PROGRAM:
        st = state[part]
        if stage == 0:
            st["ox"] = mix(part)
        elif stage == 1:
            st["x1"], st["h"] = norm1(part, st.pop("ox"))
        elif stage < nstage - 1:
            term = ff(st["h"], stage - 2)
            st["acc"] = term if stage == 2 else st["acc"] + term
        else:
            norm2(part, st["x1"], st["acc"])


def _residual_specs(residual):
    tm = ROW_TILE
    if len(residual) == 1:
        return [pl.BlockSpec((tm, D_MODEL), lambda i: (i, 0))]
    head_tiles = CTX_PAD // tm
    return [pl.BlockSpec((tm, D_MODEL), lambda i: (jnp.minimum(i, head_tiles - 1), 0)),
            pl.BlockSpec((tm, D_MODEL), lambda i: (jnp.maximum(i - head_tiles, 0), 0))]


def _post(residual, att, ret, s5_pairs, mods, layer, permt, wglu, bglu, wo, g1, b1, w1, w2, g2, b2, skip_context):
    rows = att.shape[0]
    tm = ROW_TILE
    off = CTX_PAD // tm if skip_context else 0
    assert not (skip_context and len(residual) > 1)
    row_blk = lambda width: pl.BlockSpec((tm, width), lambda i: (i + off, 0))
    full = lambda arr: pl.BlockSpec(arr.shape, lambda i: (0,) * arr.ndim)
    hbm = lambda arr: pl.BlockSpec(memory_space=pl.ANY)
    vec = lambda v: v.reshape(1, -1).astype(F32)
    small = [(permt, full), (wglu, full), (vec(bglu), full), (wo, hbm), (vec(g1), full), (vec(b1), full),
             (w1, hbm), (w2, hbm), (vec(g2), full), (vec(b2), full)]
    res_specs = [row_blk(D_MODEL)] if skip_context else _residual_specs(residual)
    return pl.pallas_call(
        functools.partial(_post_kernel, split=len(residual) > 1, layer=layer),
        grid=(rows // tm - off,),
        in_specs=res_specs + [row_blk(ATT_WIDTH), row_blk(RET_WIDTH),
                              pl.BlockSpec((S5_PAIRS, tm // S5_T, S5_T * S5_PAIR_W), lambda i: (0, i + off, 0)),
                              pl.BlockSpec((None, None, N_ADA, D_MODEL),
                                           lambda i: (layer, jnp.where(i + off == 0, 1, 0), 0, 0))]
                 + [spec(arr) for arr, spec in small],
        out_specs=pl.BlockSpec((tm, D_MODEL), lambda i: (i, 0)),
        out_shape=jax.ShapeDtypeStruct((rows - off * tm, D_MODEL), F32),
        scratch_shapes=[pltpu.VMEM((ATT_WIDTH + RET_WIDTH + S5_WIDTH, D_MODEL), MXU_DTYPE),
                        pltpu.VMEM((D_MODEL, D_FF), MXU_DTYPE), pltpu.VMEM((D_FF, D_MODEL), MXU_DTYPE),
                        pltpu.VMEM((2, STAGE_BYTES // (4 * D_FF), D_FF), F32),
                        pltpu.VMEM((2, STAGE_BYTES // (4 * D_MODEL), D_MODEL), F32),
                        pltpu.SemaphoreType.DMA((2, 2))],
        compiler_params=_params(("arbitrary",)),
        name="post",
    )(*residual, att, ret, s5_pairs, mods, *[arr for arr, _ in small])


def kernel(x, c, ctx, c_ctx, w_ada, b_ada, w_in, att_sink, ret_decay_logit, s5_lambda_re, s5_lambda_im, s5_b_re,
           s5_b_im, s5_c_re, s5_c_im, s5_log_dt, s5_d, w_glu, b_glu, w_out, ln1_g, ln1_b, w_ff1, w_ff2, ln2_g,
           ln2_b):
    assert x.shape[0] == 1 and x.shape[2] == D_MODEL and ctx.shape[1] == CTX_LEN
    seq = x.shape[1]
    assert seq % ROW_TILE == 0
    residual = (jnp.pad(ctx[0], ((0, CTX_PAD - CTX_LEN), (0, 0))), x[0])
    cond = jnp.zeros((8, D_MODEL), F32).at[0].set(c[0]).at[1].set(c_ctx)
    mods = _modulation(cond, w_ada, b_ada).reshape(DEPTH, 8, N_ADA, D_MODEL)
    tabs = _rope_tables(seq)
    masks = _attention_masks()
    perm = _chunk_perm(ROW_TILE // POST_SPLIT, MXU_DTYPE)
    permt = perm.T
    col_scale = jnp.ones((IN_WIDTH,), F32).at[COL_AQ:COL_AK].set(HEAD_DIM ** -0.5 * LOG2E)
    col_scale = col_scale.at[COL_RQ:COL_RK].set(HEAD_DIM ** -0.5)
    s5w = jax.vmap(_s5_weights)(s5_lambda_re, s5_lambda_im, s5_b_re, s5_b_im, s5_c_re, s5_c_im, s5_log_dt, s5_d)
    log_gamma = jax.nn.log_sigmoid(ret_decay_logit.astype(F32))
    for l in range(DEPTH):
        proj, u_pairs = _in_proj(residual, mods, l, w_in, col_scale.reshape(1, IN_WIDTH), tabs, perm)
        att = _attention(proj, att_sink[l].astype(F32) * LOG2E, masks)
        ret = _retention(proj, log_gamma[l])
        s5 = _s5_mixer(u_pairs, s5w, l)
        stream = _post(residual, att, ret, s5, mods, l, permt, w_glu[l].astype(MXU_DTYPE), b_glu[l],
                       w_out, ln1_g[l], ln1_b[l], w_ff1, w_ff2, ln2_g[l], ln2_b[l],
                       skip_context=(l == DEPTH - 1))
        residual = (stream,)
    return stream[None]
```

```python
import functools
import math

import jax
import jax.numpy as jnp
from jax import lax
from jax.experimental import pallas as pl
from jax.experimental.pallas import tpu as pltpu

F32 = jnp.float32
MXU_DTYPE = jnp.bfloat16

D_MODEL = 1024
DEPTH = 4
GRID_W = 64
CTX_LEN = 256
CTX_PAD = 512
HEAD_DIM = 64
ATT_HEADS = 8
ATT_KV_HEADS = 2
ATT_BLOCK = 128
ATT_LOOKAHEAD = 2
ROPE_BASE = 10000.0
RET_HEADS = 4
RET_CHUNK = 256
S5_CH = 16
S5_GROUPS = 16
S5_STATE = 64
S5_T = 16
S5_PAIRS = S5_GROUPS // 2
S5_PAIR_W = 2 * S5_CH
S5_TILE = CTX_PAD // S5_T
SCAN_ROWS = 8
S5_STATE_W = 2 * 2 * S5_STATE
ATT_WIDTH = ATT_HEADS * HEAD_DIM
KV_WIDTH = ATT_KV_HEADS * HEAD_DIM
RET_WIDTH = RET_HEADS * HEAD_DIM
S5_WIDTH = S5_GROUPS * S5_CH
IN_WIDTH = ATT_WIDTH + 2 * KV_WIDTH + 4 * RET_WIDTH + S5_WIDTH
D_FF = 4 * D_MODEL
FF_CHUNK = 1024
N_ADA = 6
LN_EPS = 1e-5
GN_EPS = 1e-5
DEEPNORM_ALPHA = (2 * DEPTH) ** 0.25
ROW_TILE = 512
POST_SPLIT = 2
_NFF = D_FF // FF_CHUNK
POST_PROGRAM = (((0, 0), (0, 1), (1, 0), (0, 2), (1, 1)) + tuple((0, 2 + c) for c in range(1, _NFF))
                + ((1, 2), (0, 2 + _NFF)) + tuple((1, 2 + c) for c in range(1, _NFF)) + ((1, 2 + _NFF),))
NEG_BIG = -1e30
LOG2E = math.log2(math.e)
LANES = 128
VMEM_LIMIT = 56 * 1024 * 1024
STAGE_BYTES = 2 * 1024 * 1024

COL_AQ, COL_AK, COL_AV = 0, ATT_WIDTH, ATT_WIDTH + KV_WIDTH
COL_RQ = ATT_WIDTH + 2 * KV_WIDTH
COL_RK, COL_RV, COL_RG = COL_RQ + RET_WIDTH, COL_RQ + 2 * RET_WIDTH, COL_RQ + 3 * RET_WIDTH
COL_S5 = COL_RQ + 4 * RET_WIDTH


def _mm(a, b):
    return jnp.dot(a, b, preferred_element_type=F32)


def _mm_nt(a, b):
    return lax.dot_general(a, b, (((1,), (1,)), ((), ())), preferred_element_type=F32)


def _mm_tn(a, b):
    return lax.dot_general(a, b, (((0,), (0,)), ((), ())), preferred_element_type=F32)


def _params(sem):
    return pltpu.CompilerParams(dimension_semantics=sem, vmem_limit_bytes=VMEM_LIMIT)


def _mod_kernel(cond_ref, w_ref, b_ref, o_ref):
    c = cond_ref[...]
    s = c * jax.nn.sigmoid(c)
    w = w_ref[0]
    split = lambda v: (v.astype(jnp.bfloat16), (v - v.astype(jnp.bfloat16).astype(F32)).astype(jnp.bfloat16))
    s_hi, s_lo = split(s)
    w_hi, w_lo = split(w)
    o_ref[0] = _mm(s_hi, w_hi) + (_mm(s_lo, w_hi) + _mm(s_hi, w_lo)) + b_ref[0]


def _modulation(cond, w_ada, b_ada):
    tn = 1536
    n = N_ADA * D_MODEL
    return pl.pallas_call(
        _mod_kernel,
        grid=(DEPTH, n // tn),
        in_specs=[
            pl.BlockSpec((8, D_MODEL), lambda l, j: (0, 0)),
            pl.BlockSpec((1, D_MODEL, tn), lambda l, j: (l, 0, j)),
            pl.BlockSpec((1, 1, tn), lambda l, j: (l, 0, j)),
        ],
        out_specs=pl.BlockSpec((1, 8, tn), lambda l, j: (l, 0, j)),
        out_shape=jax.ShapeDtypeStruct((DEPTH, 8, n), F32),
        compiler_params=_params(("arbitrary", "arbitrary")),
        name="modulation",
    )(cond, w_ada, b_ada.reshape(DEPTH, 1, n))


def _lane_block_shuffle(src_rows, src_lane_blk, out_vreg):
    acc = None
    for q in range(LANES // S5_PAIR_W):
        piece = src_rows(out_vreg * (LANES // S5_PAIR_W) + q)
        shift = (S5_PAIR_W * (q - src_lane_blk)) % LANES
        if shift:
            piece = pltpu.roll(piece, shift, 1)
        if acc is None:
            acc = piece
        else:
            lane_blk = lax.broadcasted_iota(jnp.int32, piece.shape, 1) // S5_PAIR_W
            acc = jnp.where(lane_blk == q, piece, acc)
    return acc


def _in_proj_kernel(*refs, split, layer):
    nres = 2 if split else 1
    (mod_ref, w_hbm, scale_ref, ca_ref, sa_ref, cr_ref, sr_ref, perm_ref, o_ref, u_ref, w_ref, stage_ref,
     sem_ref) = refs[nres:]

    @pl.when(pl.program_id(0) == 0)
    def _stage_weights():
        _load_cast(w_hbm.at[layer], w_ref, stage_ref, sem_ref, col_scale=scale_ref[...])

    x = _residual_rows(refs[:nres])(slice(None))
    h = (x * (1.0 + mod_ref[1:2, :]) + mod_ref[0:1, :]).astype(w_ref.dtype)
    lane = lax.broadcasted_iota(jnp.int32, (x.shape[0], LANES), 1)
    first_att = lane % (HEAD_DIM // 2) < HEAD_DIM // 4
    first_ret = lane % HEAD_DIM < HEAD_DIM // 2

    def proj(c0, c1):
        return _mm(h, w_ref[:, c0:c1])

    def rope_store(p, c0, width, cos, sin, first, half):
        for b in range(width // LANES):
            blk = p[:, LANES * b:LANES * (b + 1)]
            rot = jnp.where(first, pltpu.roll(blk, LANES - half, 1), pltpu.roll(blk, half, 1))
            o_ref[:, c0 + LANES * b:c0 + LANES * (b + 1)] = (blk * cos + rot * sin).astype(o_ref.dtype)

    def plain_store(p, c0, width):
        o_ref[:, c0:c0 + width] = p.astype(o_ref.dtype)

    u = proj(COL_S5, IN_WIDTH).astype(w_ref.dtype)
    sub = perm_ref.shape[0]
    nchunk = sub // S5_T
    for part in range(x.shape[0] // sub):
        g = _mm(perm_ref[...], u[sub * part:sub * (part + 1)])
        for a in range(S5_PAIRS):
            vreg_col, lane_blk = divmod(a * S5_PAIR_W, LANES)
            lane_blk //= S5_PAIR_W
            for v in range(S5_T * S5_PAIR_W // LANES):
                slab = _lane_block_shuffle(
                    lambda j: g[nchunk * j:nchunk * (j + 1), vreg_col * LANES:(vreg_col + 1) * LANES], lane_blk, v)
                u_ref[a, nchunk * part:nchunk * (part + 1), LANES * v:LANES * (v + 1)] = slab.astype(u_ref.dtype)

    ca, sa, cr, sr = ca_ref[...], sa_ref[...], cr_ref[...], sr_ref[...]
    att_rope = (ca, sa, first_att, HEAD_DIM // 4)
    ret_rope = (cr, sr, first_ret, HEAD_DIM // 2)
    groups = [(COL_AQ, ATT_WIDTH, att_rope), (COL_AK, KV_WIDTH, att_rope), (COL_RQ, RET_WIDTH, ret_rope),
              (COL_RK, RET_WIDTH, ret_rope), (COL_AV, KV_WIDTH, None), (COL_RV, COL_S5 - COL_RV, None)]
    pending = None
    for group in groups + [None]:
        nxt = None if group is None else (proj(group[0], group[0] + group[1]),) + group
        if pending is not None:
            p, pc0, pwidth, prope = pending
            if prope is None:
                plain_store(p, pc0, pwidth)
            else:
                rope_store(p, pc0, pwidth, *prope)
        pending = nxt


def _chunk_perm(tile_rows, dtype):
    nchunk = tile_rows // S5_T
    r = jnp.arange(tile_rows)
    src = S5_T * (r % nchunk) + r // nchunk
    return (src[:, None] == jnp.arange(tile_rows)[None, :]).astype(dtype)


def _in_proj(residual, mods, layer, w_in, col_scale, tabs, perm):
    rows = tabs[0].shape[0]
    tm = ROW_TILE
    tab_spec = pl.BlockSpec((tm, LANES), lambda i: (i, 0))
    nch = rows // S5_T
    return pl.pallas_call(
        functools.partial(_in_proj_kernel, split=len(residual) > 1, layer=layer),
        grid=(rows // tm,),
        in_specs=_residual_specs(residual) + [
            pl.BlockSpec((None, None, N_ADA, D_MODEL), lambda i: (layer, jnp.where(i == 0, 1, 0), 0, 0)),
            pl.BlockSpec(memory_space=pl.ANY),
            pl.BlockSpec((1, IN_WIDTH), lambda i: (0, 0)),
            tab_spec, tab_spec, tab_spec, tab_spec,
            pl.BlockSpec(perm.shape, lambda i: (0, 0)),
        ],
        out_specs=[pl.BlockSpec((tm, COL_S5), lambda i: (i, 0)),
                   pl.BlockSpec((S5_PAIRS, tm // S5_T, S5_T * S5_PAIR_W), lambda i: (0, i, 0))],
        out_shape=[jax.ShapeDtypeStruct((rows, COL_S5), MXU_DTYPE),
                   jax.ShapeDtypeStruct((S5_PAIRS, nch, S5_T * S5_PAIR_W), MXU_DTYPE)],
        scratch_shapes=[pltpu.VMEM((D_MODEL, IN_WIDTH), MXU_DTYPE),
                        pltpu.VMEM((2, STAGE_BYTES // (4 * IN_WIDTH), IN_WIDTH), F32),
                        pltpu.SemaphoreType.DMA((2,))],
        compiler_params=_params(("arbitrary",)),
        name="in_proj",
    )(*residual, mods, w_in, col_scale, *tabs, perm)


def _rope_tables(seq):
    half_a = HEAD_DIM // 4
    half_r = HEAD_DIM // 2
    nrow = seq // GRID_W
    inv_a = ROPE_BASE ** (-jnp.arange(half_a, dtype=F32) / half_a)
    inv_r = ROPE_BASE ** (-jnp.arange(half_r, dtype=F32) / half_r)
    ang_r = jnp.arange(nrow, dtype=F32)[:, None] * inv_a[None, :]
    ang_c = jnp.arange(GRID_W, dtype=F32)[:, None] * inv_a[None, :]
    ang_t = jnp.arange(seq, dtype=F32)[:, None] * inv_r[None, :]
    hp = lax.Precision.HIGHEST
    lane = jnp.arange(LANES)
    within = lane % HEAD_DIM
    pick_a = (within % half_a)[None, :] == jnp.arange(half_a)[:, None]
    exp_row = (pick_a & (within < 2 * half_a)[None, :]).astype(F32)
    exp_col = (pick_a & (within >= 2 * half_a)[None, :]).astype(F32)
    exp_t = ((lane % half_r)[None, :] == jnp.arange(half_r)[:, None]).astype(F32)
    sign_a = jnp.where(within % (2 * half_a) < half_a, -1.0, 1.0).astype(F32)
    sign_r = jnp.where(within < half_r, -1.0, 1.0).astype(F32)

    def att_table(fn):
        by_row = jnp.dot(fn(ang_r), exp_row, precision=hp)
        by_col = jnp.dot(fn(ang_c), exp_col, precision=hp)
        return (by_row[:, None, :] + by_col[None, :, :]).reshape(seq, LANES)

    cos_a = att_table(jnp.cos)
    sin_a = att_table(jnp.sin) * sign_a
    cos_r = jnp.dot(jnp.cos(ang_t), exp_t, precision=hp)
    sin_r = jnp.dot(jnp.sin(ang_t), exp_t, precision=hp) * sign_r
    pad = lambda tab, ident: jnp.pad(tab, ((CTX_PAD, 0), (0, 0)), constant_values=ident)
    return pad(cos_a, 1.0), pad(sin_a, 0.0), pad(cos_r, 1.0), pad(sin_r, 0.0)


def _swap_halves(x):
    if x.dtype.itemsize == 4:
        return pltpu.roll(x, 64, 1)
    packed = pltpu.bitcast(x, jnp.uint32)
    return pltpu.bitcast(pltpu.roll(packed, 64, 1), x.dtype)


def _dup_heads(x):
    sw = _swap_halves(x)
    lo = lax.broadcasted_iota(jnp.int32, x.shape, 1) < HEAD_DIM
    return jnp.where(lo, x, sw), jnp.where(lo, sw, x)


def _attn_kernel(sink_ref, q_ref, km_ref, kp_ref, kn_ref, vm_ref, vp_ref, vn_ref, kc_ref, vc_ref, mask_ref,
                 o_ref, k2_ref, v2_ref, kc2_ref, vc2_ref):
    i = pl.program_id(0)
    last_blk = pl.num_programs(0) * (ROW_TILE // ATT_BLOCK) - 1
    blk = ATT_BLOCK
    def spread(src, ones_upper):
        x = src[...]
        a, b = _dup_heads(x)
        if ones_upper:
            upper = lax.broadcasted_iota(jnp.int32, x.shape, 1) >= HEAD_DIM
            a = jnp.where(upper, jnp.ones_like(a), a)
            b = jnp.where(upper, jnp.ones_like(b), b)
        return a, b

    for dst, parts, is_v in ((k2_ref, (kp_ref, km_ref, kn_ref), False), (v2_ref, (vp_ref, vm_ref, vn_ref), True)):
        row = 0
        for part in parts:
            a, b = spread(part, is_v)
            n = part.shape[0]
            dst[0, row:row + n, :] = a
            dst[1, row:row + n, :] = b
            row += n
    for dst, src, is_v in ((kc2_ref, kc_ref, False), (vc2_ref, vc_ref, True)):
        a, b = spread(src, is_v)
        dst[0] = a
        dst[1] = b

    lo = lax.broadcasted_iota(jnp.int32, (blk, LANES), 1) < HEAD_DIM
    group = ATT_HEADS // ATT_KV_HEADS

    nloc = 3 * blk

    def scores(j, kv):
        r0 = j * blk
        qt = q_ref[r0:r0 + blk, group * HEAD_DIM * kv:group * HEAD_DIM * (kv + 1)]
        parts = []
        for g in range(group):
            qc = qt[:, LANES * (g // 2):LANES * (g // 2 + 1)]
            keep = lo if g % 2 == 0 else jnp.logical_not(lo)
            parts.append(jnp.where(keep, qc, jnp.zeros_like(qc)))
        qs = jnp.concatenate(parts, axis=0)
        return _mm_nt(qs, k2_ref[kv, r0:r0 + nloc, :]), _mm_nt(qs, kc2_ref[kv])

    def finish(j, kv, s_loc, s_ctx):
        r0 = j * blk
        gblk = i * (ROW_TILE // blk) + j
        sel = jnp.where(i == 0, 3, jnp.where(gblk == CTX_PAD // blk, 0, jnp.where(gblk == last_blk, 2, 1)))
        bias = mask_ref[sel]
        probs, sink_w = [], []
        for g in range(group):
            s = jnp.concatenate([s_loc[blk * g:blk * (g + 1)] + bias, s_ctx[blk * g:blk * (g + 1)]], axis=1)
            sk = sink_ref[group * kv + g]
            m = jnp.maximum(jnp.max(s, axis=-1, keepdims=True), sk)
            probs.append(jnp.exp2(s - m).astype(o_ref.dtype))
            sink_w.append(jnp.exp2(sk - m))
        p = jnp.concatenate(probs, axis=0)
        o = _mm(p[:, :nloc], v2_ref[kv, r0:r0 + nloc, :]) + _mm(p[:, nloc:], vc2_ref[kv])
        for half in range(group // 2):
            even, odd = 2 * half, 2 * half + 1
            oe = o[blk * even:blk * (even + 1)]
            oo = o[blk * odd:blk * (odd + 1)]
            y_even = oe * (1.0 / (pltpu.roll(oe, HEAD_DIM, 1) + sink_w[even]))
            y_odd = pltpu.roll(oo, HEAD_DIM, 1) * (1.0 / (oo + sink_w[odd]))
            c0 = group * HEAD_DIM * kv + LANES * half
            o_ref[r0:r0 + blk, c0:c0 + LANES] = jnp.where(lo, y_even, y_odd).astype(o_ref.dtype)

    items = [(j, kv) for j in range(ROW_TILE // blk) for kv in range(ATT_KV_HEADS)]
    pending = {}
    for t in range(len(items) + ATT_LOOKAHEAD):
        if t < len(items):
            pending[t] = scores(*items[t])
        if t >= ATT_LOOKAHEAD:
            finish(*items[t - ATT_LOOKAHEAD], *pending.pop(t - ATT_LOOKAHEAD))


def _attention_masks():
    qi = jnp.arange(ATT_BLOCK)[:, None]
    kj = jnp.arange(3 * ATT_BLOCK)[None, :]
    band = jnp.abs(kj - ATT_BLOCK - qi) <= ATT_BLOCK
    first = band & (kj >= ATT_BLOCK)
    last = band & (kj < 2 * ATT_BLOCK)
    none = jnp.zeros_like(band)
    masks = jnp.stack([first, band, last, none])
    return jnp.where(masks, 0.0, NEG_BIG).astype(F32)


def _attention(proj, sink, masks):
    rows = proj.shape[0]
    tm, blk = ROW_TILE, ATT_BLOCK
    per = tm // blk
    nblk = rows // blk
    ck, cv = COL_AK // KV_WIDTH, COL_AV // KV_WIDTH
    dt = proj.dtype
    return pl.pallas_call(
        _attn_kernel,
        grid=(rows // tm,),
        in_specs=[
            pl.BlockSpec(memory_space=pltpu.SMEM),
            pl.BlockSpec((tm, ATT_WIDTH), lambda i: (i, 0)),
            pl.BlockSpec((tm, KV_WIDTH), lambda i: (i, ck)),
            pl.BlockSpec((blk, KV_WIDTH), lambda i: (jnp.maximum(i * per - 1, 0), ck)),
            pl.BlockSpec((blk, KV_WIDTH), lambda i: (jnp.minimum((i + 1) * per, nblk - 1), ck)),
            pl.BlockSpec((tm, KV_WIDTH), lambda i: (i, cv)),
            pl.BlockSpec((blk, KV_WIDTH), lambda i: (jnp.maximum(i * per - 1, 0), cv)),
            pl.BlockSpec((blk, KV_WIDTH), lambda i: (jnp.minimum((i + 1) * per, nblk - 1), cv)),
            pl.BlockSpec((CTX_LEN, KV_WIDTH), lambda i: (0, ck)),
            pl.BlockSpec((CTX_LEN, KV_WIDTH), lambda i: (0, cv)),
            pl.BlockSpec((4, blk, 3 * blk), lambda i: (0, 0, 0)),
        ],
        out_specs=pl.BlockSpec((tm, ATT_WIDTH), lambda i: (i, 0)),
        out_shape=jax.ShapeDtypeStruct((rows, ATT_WIDTH), dt),
        scratch_shapes=[
            pltpu.VMEM((2, tm + 2 * blk, KV_WIDTH), dt),
            pltpu.VMEM((2, tm + 2 * blk, KV_WIDTH), dt),
            pltpu.VMEM((2, CTX_LEN, KV_WIDTH), dt),
            pltpu.VMEM((2, CTX_LEN, KV_WIDTH), dt),
        ],
        compiler_params=_params(("arbitrary",)),
        name="attention",
    )(sink, proj, proj, proj, proj, proj, proj, proj, proj, proj, masks)


def _ret_kernel(*refs, sup, ntile):
    lg_ref, q_ref = refs[:2]
    k_refs, v_refs = refs[2:2 + sup], refs[2 + sup:2 + 2 * sup]
    g_ref, o_ref, sb_ref, s_ref, dm_ref, tab_ref, gbd_ref = refs[2 + 2 * sup:]
    k_ref, v_ref = k_refs[0], v_refs[0]
    step = pl.program_id(0)
    nsup = (ntile - 1) // sup
    ph = jnp.where(step <= nsup, 0, 1)
    t = jnp.where(step <= nsup, step, step - (nsup + 1))
    c = RET_CHUNK
    w = RET_WIDTH
    per = q_ref.shape[0] // c
    mdt = sb_ref.dtype
    rows = lambda ci: slice(c * ci, c * (ci + 1))

    def lane_vec(direction, shape, axis):
        head = lax.broadcasted_iota(jnp.int32, shape, axis) // HEAD_DIM
        out = jnp.full(shape, lg_ref[direction, RET_HEADS - 1], F32)
        for h in range(RET_HEADS - 2, -1, -1):
            out = jnp.where(head == h, lg_ref[direction, h], out)
        return out

    @pl.when(jnp.logical_and(ph == 0, t == 0))
    def _init_tables():
        diff = (lax.broadcasted_iota(jnp.int32, (c, c), 0) - lax.broadcasted_iota(jnp.int32, (c, c), 1)).astype(F32)
        for h in range(RET_HEADS):
            dm_ref[h] = jnp.exp(jnp.where(diff >= 0, diff * lg_ref[0, h], -diff * lg_ref[1, h]))
        pos = lax.broadcasted_iota(jnp.int32, (c, w), 0).astype(F32)
        lgf = lane_vec(0, (c, w), 1)
        lgb = lane_vec(1, (c, w), 1)
        tab_ref[0] = jnp.exp((c - 1.0 - pos) * lgf)
        tab_ref[1] = jnp.exp((pos + 1.0) * lgf)
        tab_ref[2] = jnp.exp(pos * lgb)
        tab_ref[3] = jnp.exp((c - pos) * lgb)
        same = (lax.broadcasted_iota(jnp.int32, (w, w), 0) // HEAD_DIM
                == lax.broadcasted_iota(jnp.int32, (w, w), 1) // HEAD_DIM)
        bd = jnp.where(same, 1.0, 0.0)
        gbd_ref[0] = bd * jnp.exp(c * lane_vec(0, (w, w), 0))
        gbd_ref[1] = bd * jnp.exp(c * lane_vec(1, (w, w), 0))
        gbd_ref[2] = bd

    @pl.when(t == 0)
    def _reset_state():
        s_ref[...] = jnp.zeros_like(s_ref)

    def state_update(direction, key_tab, ci, kr=k_ref, vr=v_ref):
        kw = (kr[rows(ci), :].astype(F32) * tab_ref[key_tab]).astype(mdt)
        u = _mm_tn(kw, vr[rows(ci), :])
        s_ref[...] = gbd_ref[direction] * s_ref[...] + gbd_ref[2] * u

    @pl.when(jnp.logical_and(ph == 0, t == 0))
    def _backward_context():
        sb_ref[0] = s_ref[...].astype(mdt)
        state_update(1, 2, 0)

    @pl.when(jnp.logical_and(ph == 0, t > 0))
    def _backward_latent():
        base = 1 + per * sup * (nsup - t)
        for qi in range(sup - 1, -1, -1):
            for ci in range(per - 1, -1, -1):
                sb_ref[base + per * qi + ci] = s_ref[...].astype(mdt)
                state_update(1, 2, ci, k_refs[qi], v_refs[qi])

    head = lax.broadcasted_iota(jnp.int32, (c, w), 1) // HEAD_DIM

    def scores(ci):
        q = q_ref[rows(ci), :]
        qs = jnp.concatenate([jnp.where(head == h, q, jnp.zeros_like(q)) for h in range(RET_HEADS)], axis=0)
        return _mm_nt(qs, k_ref[rows(ci), :])

    def intra(ci, sc):
        scd = jnp.concatenate([sc[c * h:c * (h + 1)] * dm_ref[h] for h in range(RET_HEADS)], axis=0).astype(mdt)
        oi = _mm(scd, v_ref[rows(ci), :])
        o = jnp.where(head == 0, oi[0:c], 0.0)
        for h in range(1, RET_HEADS):
            o = o + jnp.where(head == h, oi[c * h:c * (h + 1)], 0.0)
        return o

    def cross(ci, idx):
        qf = q_ref[rows(ci), :].astype(F32)
        return (_mm((qf * tab_ref[1]).astype(mdt), s_ref[...].astype(mdt))
                + _mm((qf * tab_ref[3]).astype(mdt), sb_ref[idx]))

    def finish(ci, o):
        avg = (gbd_ref[2] * (1.0 / HEAD_DIM)).astype(mdt)
        o_hi = o.astype(mdt)
        d = o - (_mm(o_hi, avg) + _mm((o - o_hi.astype(F32)).astype(mdt), avg))
        var = _mm((d * d).astype(mdt), avg)
        gate = g_ref[rows(ci), :].astype(F32)
        y = d * lax.rsqrt(var + GN_EPS) * (gate * jax.nn.sigmoid(gate))
        o_ref[rows(ci), :] = y.astype(o_ref.dtype)

    @pl.when(jnp.logical_and(ph == 1, t == 0))
    def _forward_context():
        o = intra(0, scores(0)) + cross(0, 0)
        state_update(0, 0, 0)
        finish(0, o)
        for ci in range(1, per):
            o_ref[rows(ci), :] = jnp.zeros((c, w), o_ref.dtype)

    @pl.when(jnp.logical_and(ph == 1, t > 0))
    def _forward_latent():
        base = 1 + per * (t - 1)
        sc = [scores(ci) for ci in range(per)]
        outs = [intra(ci, sc[ci]) for ci in range(per)]
        for ci in range(per):
            outs[ci] = outs[ci] + cross(ci, base + ci)
            state_update(0, 0, ci)
        for ci in range(per):
            finish(ci, outs[ci])


def _retention(proj, log_gamma):
    rows = proj.shape[0]
    c = RET_CHUNK
    tm = ROW_TILE
    ntile = rows // tm
    nchunk = 1 + (rows - CTX_PAD) // c
    dt = proj.dtype
    cq, ckk, cvv, cg = (COL_RQ // RET_WIDTH, COL_RK // RET_WIDTH, COL_RV // RET_WIDTH, COL_RG // RET_WIDTH)

    nlat = ntile - 1
    sup = next(s for s in (4, 2, 1) if nlat % s == 0)
    nsup = nlat // sup
    nback = 1 + nsup

    def fw_blk(step):
        return jnp.maximum(step - nback, 0)

    def kv_spec(col, qi):
        def index(step):
            back = jnp.where(step == 0, 0, 1 + sup * (nsup - jnp.minimum(step, nsup)) + qi)
            fwd = step - nback if qi == 0 else 1 + qi
            return jnp.where(step < nback, back, fwd), col
        return pl.BlockSpec((tm, RET_WIDTH), index)

    return pl.pallas_call(
        functools.partial(_ret_kernel, sup=sup, ntile=ntile),
        grid=(nback + ntile,),
        in_specs=[pl.BlockSpec(memory_space=pltpu.SMEM),
                  pl.BlockSpec((tm, RET_WIDTH), lambda step: (fw_blk(step), cq))]
                 + [kv_spec(ckk, qi) for qi in range(sup)] + [kv_spec(cvv, qi) for qi in range(sup)]
                 + [pl.BlockSpec((tm, RET_WIDTH), lambda step: (fw_blk(step), cg))],
        out_specs=pl.BlockSpec((tm, RET_WIDTH), lambda step: (fw_blk(step), 0)),
        out_shape=jax.ShapeDtypeStruct((rows, RET_WIDTH), dt),
        scratch_shapes=[
            pltpu.VMEM((nchunk, RET_WIDTH, RET_WIDTH), dt),
            pltpu.VMEM((RET_WIDTH, RET_WIDTH), F32),
            pltpu.VMEM((RET_HEADS, c, c), F32),
            pltpu.VMEM((4, c, RET_WIDTH), F32),
            pltpu.VMEM((3, RET_WIDTH, RET_WIDTH), F32),
        ],
        compiler_params=_params(("arbitrary",)),
        name="retention",
    )(log_gamma, proj, *([proj] * (2 * sup)), proj)


def _s5_weights(lam_re, lam_im, b_re, b_im, c_re, c_im, log_dt, d_skip):
    tt, g, n, p, a = S5_T, S5_GROUPS, S5_STATE, S5_CH, S5_PAIRS
    lam = lax.complex(lam_re.astype(F32), lam_im.astype(F32))
    dtv = jnp.exp(log_dt.astype(F32))[..., None]
    lam_bar = jnp.exp(lam * dtv)
    bbar = ((lam_bar - 1.0) / lam)[..., None] * lax.complex(b_re.astype(F32), b_im.astype(F32))
    cmat = lax.complex(c_re.astype(F32), c_im.astype(F32))
    pw = [jnp.ones_like(lam_bar)]
    for _ in range(tt):
        pw.append(pw[-1] * lam_bar)
    pw = jnp.stack(pw, axis=1)
    eye2 = jnp.eye(2, dtype=F32)
    ri = lambda z, axis: jnp.stack([jnp.real(z), jnp.imag(z)], axis=axis)

    pw_l = pw.reshape(2, tt + 1, a, 2 * n)
    bbt = jnp.einsum('dahpn,gh->dagphn', jnp.swapaxes(bbar, -1, -2).reshape(2, a, 2, p, n), eye2)
    bbt = bbt.reshape(2, a, 2 * p, 2 * n)
    cct = jnp.einsum('dahpn,gh->dagphn', cmat.reshape(2, a, 2, p, n), eye2).reshape(2, a, 2 * p, 2 * n)
    pw_k = ri(pw_l, 1).transpose(3, 0, 1, 2, 4)
    b_k = ri(bbt, 2).transpose(1, 0, 2, 3, 4)
    c_k = ri(cct, 2).transpose(1, 0, 2, 3, 4)
    decay = [pw_l[:, tt]]
    for _ in range(SCAN_ROWS - 1):
        decay.append(decay[-1] * decay[0])
    decay = jnp.stack(decay, axis=0)
    rows8 = lambda z: jnp.broadcast_to(z[None], (SCAN_ROWS,) + z.shape)
    carry_w = jnp.stack([decay[:, 0], decay[::-1, 1]], axis=1)
    scan_tab = jnp.stack([rows8(decay[0]), rows8(decay[1]), rows8(decay[3]), carry_w], axis=0)
    scan_tab = ri(scan_tab, 0).transpose(4, 3, 1, 0, 2, 5)
    skip = jnp.tile(d_skip.astype(F32).reshape(a, 1, 2 * p), (1, tt, 1)).reshape(a, 1, tt * 2 * p)
    return pw_k, b_k, c_k, scan_tab, skip


def _pair_spec(layer, *shape):
    return pl.BlockSpec((None, None) + shape, lambda i: (layer, i) + (0,) * len(shape))


def _s5_drive_kernel(u_ref, pw_ref, b_ref, o_ref, w_ref):
    rows, half = S5_PAIR_W, LANES
    for d in range(2):
        br, bi = b_ref[d, 0], b_ref[d, 1]
        for j in range(S5_T):
            e = S5_T - 1 - j if d == 0 else j
            pr, pi = pw_ref[d, 0, e:e + 1, :], pw_ref[d, 1, e:e + 1, :]
            w_ref[rows * j:rows * (j + 1), 2 * half * d:2 * half * d + half] = (pr * br - pi * bi).astype(w_ref.dtype)
            w_ref[rows * j:rows * (j + 1), 2 * half * d + half:2 * half * (d + 1)] = (
                pr * bi + pi * br).astype(w_ref.dtype)
    o_ref[0] = _mm(u_ref[0], w_ref[...])


def _s5_drive(u_pairs, pw_k, b_k, layer):
    a, nch, wd = u_pairs.shape
    return pl.pallas_call(
        _s5_drive_kernel,
        grid=(a,),
        in_specs=[pl.BlockSpec((1, nch, wd), lambda i: (i, 0, 0)),
                  _pair_spec(layer, *pw_k.shape[2:]), _pair_spec(layer, *b_k.shape[2:])],
        out_specs=pl.BlockSpec((1, nch, 2 * S5_STATE_W), lambda i: (i, 0, 0)),
        out_shape=jax.ShapeDtypeStruct((a, nch, 2 * S5_STATE_W), F32),
        scratch_shapes=[pltpu.VMEM((wd, 2 * S5_STATE_W), u_pairs.dtype)],
        compiler_params=_params(("arbitrary",)),
        name="s5_drive",
    )(u_pairs, pw_k, b_k)


def _s5_scan_kernel(ef_ref, eb_ref, tab_ref, sf_ref, sb_ref, st_ref):
    t = pl.program_id(0)
    hw = LANES
    sub = SCAN_ROWS
    npair = tab_ref.shape[0]
    row = lax.broadcasted_iota(jnp.int32, (sub, hw), 0)

    @pl.when(t == 0)
    def _reset():
        st_ref[...] = jnp.zeros_like(st_ref)

    def shift(x, k, reverse):
        if reverse:
            return jnp.where(row < sub - k, pltpu.roll(x, sub - k, 0), 0.0)
        return jnp.where(row >= k, pltpu.roll(x, k, 0), 0.0)

    def scan_group(e_ref, a, d, r0, cr, ci):
        reverse = d == 1
        xr = e_ref[a, pl.ds(r0, sub), 0:hw]
        xi = e_ref[a, pl.ds(r0, sub), hw:2 * hw]
        for step, k in enumerate((1, 2, 4)):
            ar, ai = tab_ref[a, d, step, 0], tab_ref[a, d, step, 1]
            sr, si = shift(xr, k, reverse), shift(xi, k, reverse)
            xr, xi = xr + ar * sr - ai * si, xi + ar * si + ai * sr
        wr, wi = tab_ref[a, d, 3, 0], tab_ref[a, d, 3, 1]
        fr = xr + wr * cr - wi * ci
        fi = xi + wr * ci + wi * cr
        edge = sub - 1 if reverse else 0
        before = (jnp.where(row == edge, cr, shift(fr, 1, reverse)), jnp.where(row == edge, ci, shift(fi, 1, reverse)))
        last = 0 if reverse else sub - 1
        after = (jnp.broadcast_to(fr[last:last + 1], (sub, hw)), jnp.broadcast_to(fi[last:last + 1], (sub, hw)))
        return before, after

    def store(o_ref, a, r0, lower, upper):
        for part in range(2):
            val = jnp.concatenate([lower[part], upper[part]], axis=0)
            o_ref[a, pl.ds(r0, 2 * sub), part * hw:(part + 1) * hw] = val.astype(o_ref.dtype)

    def run(nrows):
        span = 2 * sub
        nspan = nrows // span

        def body(gi, carry):
            r0 = pl.multiple_of(gi * span, span)
            rb0 = pl.multiple_of((nspan - 1 - gi) * span, span)
            new = []
            for a in range(npair):
                fr, fi, br, bi = carry[a]
                f_lo, (fr, fi) = scan_group(ef_ref, a, 0, r0, fr, fi)
                f_hi, (fr, fi) = scan_group(ef_ref, a, 0, r0 + sub, fr, fi)
                store(sf_ref, a, r0, f_lo, f_hi)
                b_hi, (br, bi) = scan_group(eb_ref, a, 1, rb0 + sub, br, bi)
                b_lo, (br, bi) = scan_group(eb_ref, a, 1, rb0, br, bi)
                store(sb_ref, a, rb0, b_lo, b_hi)
                new.append((fr, fi, br, bi))
            return tuple(new)

        init = tuple(tuple(st_ref[a, k] for k in range(4)) for a in range(npair))
        final = lax.fori_loop(0, nspan, body, init)
        for a in range(npair):
            for k in range(4):
                st_ref[a, k] = final[a][k]

    @pl.when(t == 0)
    def _context():
        sf_ref[...] = jnp.zeros_like(sf_ref)
        sb_ref[...] = jnp.zeros_like(sb_ref)
        run(CTX_LEN // S5_T)

    @pl.when(t > 0)
    def _latent():
        run(S5_TILE)


def _s5_scan(drive, scan_tab, layer):
    a, nch, wd2 = drive.shape
    wd = wd2 // 2
    nt = nch // S5_TILE

    def bwd(t):
        return jnp.where(t == 0, 0, nt - t)

    return pl.pallas_call(
        _s5_scan_kernel,
        grid=(nt,),
        in_specs=[pl.BlockSpec((a, S5_TILE, wd), lambda t: (0, t, 0)),
                  pl.BlockSpec((a, S5_TILE, wd), lambda t: (0, bwd(t), 1)),
                  pl.BlockSpec((None,) + scan_tab.shape[1:], lambda t: (layer,) + (0,) * (scan_tab.ndim - 1))],
        out_specs=[pl.BlockSpec((a, S5_TILE, wd), lambda t: (0, t, 0)),
                   pl.BlockSpec((a, S5_TILE, wd), lambda t: (0, bwd(t), 0))],
        out_shape=[jax.ShapeDtypeStruct((a, nch, wd), MXU_DTYPE), jax.ShapeDtypeStruct((a, nch, wd), MXU_DTYPE)],
        scratch_shapes=[pltpu.VMEM((a, 4, SCAN_ROWS, LANES), F32)],
        compiler_params=_params(("arbitrary",)),
        name="s5_scan",
    )(drive, drive, scan_tab)


def _lane_window(x, start, width):
    cols = []
    for v in range(width // LANES):
        k0, off = divmod(start + LANES * v, LANES)
        lo = x[:, LANES * k0:LANES * (k0 + 1)]
        if off:
            hi = x[:, LANES * (k0 + 1):LANES * (k0 + 2)]
            lane = lax.broadcasted_iota(jnp.int32, lo.shape, 1)
            lo = jnp.where(lane < LANES - off, pltpu.roll(lo, LANES - off, 1), pltpu.roll(hi, LANES - off, 1))
        cols.append(lo)
    return jnp.concatenate(cols, axis=1)


def _s5_read_kernel(u_ref, sf_ref, sb_ref, pw_ref, b_ref, c_ref, skip_ref, o_ref, wt_ref, wi_ref, lag_ref):
    u = u_ref[0]
    mdt = u.dtype
    rows, half = S5_PAIR_W, LANES
    for d in range(2):
        cr, ci = c_ref[d, 0], c_ref[d, 1]
        for i in range(S5_T):
            e = i + 1 if d == 0 else S5_T - i
            pr, pi = pw_ref[d, 0, e:e + 1, :], pw_ref[d, 1, e:e + 1, :]
            wt_ref[d, rows * i:rows * (i + 1), 0:half] = (pr * cr - pi * ci).astype(mdt)
            wt_ref[d, rows * i:rows * (i + 1), half:2 * half] = (-(pr * ci + pi * cr)).astype(mdt)
    nlag = 2 * S5_T - 1
    ldt = lag_ref.dtype
    zero = jnp.zeros((rows, half), ldt)
    for l in range(nlag + 1):
        lag = l - (S5_T - 1)
        for d, active in ((1, lag <= 0), (0, 0 <= lag < S5_T)):
            col = 2 * half * (1 - d)
            if active:
                cr, ci = c_ref[d, 0], c_ref[d, 1]
                pr, pi = pw_ref[d, 0, abs(lag):abs(lag) + 1, :], pw_ref[d, 1, abs(lag):abs(lag) + 1, :]
                lag_ref[rows * l:rows * (l + 1), col:col + half] = (pr * cr - pi * ci).astype(ldt)
                lag_ref[rows * l:rows * (l + 1), col + half:col + 2 * half] = (pr * ci + pi * cr).astype(ldt)
            else:
                lag_ref[rows * l:rows * (l + 1), col:col + half] = zero
                lag_ref[rows * l:rows * (l + 1), col + half:col + 2 * half] = zero
    lhs = jnp.concatenate([b_ref[1, 0], -b_ref[1, 1], b_ref[0, 0], -b_ref[0, 1]], axis=1).astype(ldt)
    kall = _mm_nt(lhs, lag_ref[...])
    for j in range(S5_T):
        wi_ref[rows * j:rows * (j + 1), :] = _lane_window(kall, rows * (S5_T - 1 - j), S5_T * rows).astype(mdt)
    y = _mm(u, wi_ref[...])
    y = y + _mm_nt(sf_ref[0].astype(mdt), wt_ref[0])
    y = y + _mm_nt(sb_ref[0].astype(mdt), wt_ref[1])
    o_ref[0] = y + u.astype(F32) * skip_ref[...]


def _s5_read(u_pairs, sf, sb, pw_k, b_k, c_k, skip, layer):
    a, nch, wd = u_pairs.shape
    blk = lambda *shape: pl.BlockSpec((1,) + shape, lambda i: (i, 0, 0))
    return pl.pallas_call(
        _s5_read_kernel,
        grid=(a,),
        in_specs=[blk(nch, wd), blk(nch, S5_STATE_W), blk(nch, S5_STATE_W), _pair_spec(layer, *pw_k.shape[2:]),
                  _pair_spec(layer, *b_k.shape[2:]), _pair_spec(layer, *c_k.shape[2:]),
                  _pair_spec(layer, *skip.shape[2:])],
        out_specs=blk(nch, wd),
        out_shape=jax.ShapeDtypeStruct((a, nch, wd), F32),
        scratch_shapes=[pltpu.VMEM((2, wd, S5_STATE_W), u_pairs.dtype), pltpu.VMEM((wd, wd), u_pairs.dtype),
                        pltpu.VMEM((2 * wd, 2 * S5_STATE_W), u_pairs.dtype)],
        compiler_params=_params(("arbitrary",)),
        name="s5_read",
    )(u_pairs, sf, sb, pw_k, b_k, c_k, skip)


def _s5_mixer(u_pairs, weights, layer):
    pw_k, b_k, c_k, scan_tab, skip = weights
    drive = _s5_drive(u_pairs, pw_k, b_k, layer)
    sf, sb = _s5_scan(drive, scan_tab, layer)
    return _s5_read(u_pairs, sf, sb, pw_k, b_k, c_k, skip, layer)


def _layer_norm(x, g, b):
    mu = jnp.mean(x, axis=-1, keepdims=True)
    d = x - mu
    var = jnp.mean(d * d, axis=-1, keepdims=True)
    return d * lax.rsqrt(var + LN_EPS) * g + b


def _residual_rows(refs):
    if len(refs) == 1:
        return lambda rows: refs[0][rows, :]
    head_ref, body_ref = refs
    is_head = pl.program_id(0) == 0
    return lambda rows: jnp.where(is_head, head_ref[rows, :], body_ref[rows, :])


def _load_cast(src_hbm, dst_ref, stage_ref, sems, col_scale=None):
    chunk = stage_ref.shape[1]
    nchunk = src_hbm.shape[0] // chunk

    def copy(k):
        slot = k % 2
        return pltpu.make_async_copy(src_hbm.at[pl.ds(k * chunk, chunk), :], stage_ref.at[slot], sems.at[slot])

    copy(0).start()
    for k in range(nchunk):
        if k + 1 < nchunk:
            copy(k + 1).start()
        copy(k).wait()
        vals = stage_ref[k % 2]
        if col_scale is not None:
            vals = vals * col_scale
        dst_ref[k * chunk:(k + 1) * chunk, :] = vals.astype(dst_ref.dtype)


def _post_kernel(*refs, split, layer):
    nres = 2 if split else 1
    load_x = _residual_rows(refs[:nres])
    (att_ref, ret_ref, s5_ref, mod_ref, permt_ref, wglu_ref, bglu_ref, wo_hbm, g1_ref, b1_ref, w1_hbm, w2_hbm,
     g2_ref, b2_ref, o_ref, wo_ref, w1_ref, w2_ref, stage_ref, sem_ref) = refs[nres:]
    woa_ref = wo_ref.at[0:ATT_WIDTH]
    wor_ref = wo_ref.at[ATT_WIDTH:ATT_WIDTH + RET_WIDTH]
    wos_ref = wo_ref.at[ATT_WIDTH + RET_WIDTH:ATT_WIDTH + RET_WIDTH + S5_WIDTH]
    mdt = w1_ref.dtype
    sub = o_ref.shape[0] // POST_SPLIT
    csub = sub // S5_T
    nff = D_FF // FF_CHUNK

    blocks = lambda c: slice(FF_CHUNK * c, FF_CHUNK * (c + 1))
    jobs = [(wo_hbm.at[layer], wo_ref)]
    for c in range(nff):
        jobs.append((w1_hbm.at[layer, :, blocks(c)], w1_ref.at[:, blocks(c)]))
        jobs.append((w2_hbm.at[layer, blocks(c), :], w2_ref.at[blocks(c), :]))

    def job_copy(k):
        return pltpu.make_async_copy(jobs[k][0], stage_ref.at[k % 2], sem_ref.at[k % 2])

    def job_finish(k):
        job_copy(k).wait()
        jobs[k][1][...] = stage_ref[k % 2].astype(mdt)
        if k + 2 < len(jobs):
            job_copy(k + 2).start()

    def mix(part):
        rows = slice(sub * part, sub * (part + 1))
        zrows = []
        for i in range(S5_T):
            src_vreg, src_blk = divmod(i * S5_PAIR_W, LANES)
            src_blk //= S5_PAIR_W
            cols = [_lane_block_shuffle(
                lambda a: s5_ref[a, csub * part:csub * (part + 1), src_vreg * LANES:(src_vreg + 1) * LANES],
                src_blk, w) for w in range(S5_WIDTH // LANES)]
            zrows.append(jnp.concatenate(cols, axis=1))
        hs = jax.nn.gelu(jnp.concatenate(zrows, axis=0))
        gate = jax.nn.sigmoid(_mm(hs.astype(mdt), wglu_ref[...]) + bglu_ref[...])
        s5 = _mm(permt_ref[...], (hs * gate).astype(mdt)).astype(mdt)
        return _mm(att_ref[rows, :], woa_ref[...]) + _mm(ret_ref[rows, :], wor_ref[...]) + _mm(s5, wos_ref[...])

    def norm1(part, ox):
        rows = slice(sub * part, sub * (part + 1))
        x1 = _layer_norm(DEEPNORM_ALPHA * load_x(rows) + mod_ref[2:3, :] * ox, g1_ref[...], b1_ref[...])
        return x1, (x1 * (1.0 + mod_ref[4:5, :]) + mod_ref[3:4, :]).astype(mdt)

    def ff(h, c):
        a = _mm(h, w1_ref[:, FF_CHUNK * c:FF_CHUNK * (c + 1)])
        a = jnp.square(jnp.maximum(a, 0.0)).astype(mdt)
        return _mm(a, w2_ref[FF_CHUNK * c:FF_CHUNK * (c + 1), :])

    def norm2(part, x1, acc):
        rows = slice(sub * part, sub * (part + 1))
        o_ref[rows, :] = _layer_norm(DEEPNORM_ALPHA * x1 + mod_ref[5:6, :] * acc, g2_ref[...], b2_ref[...])

    nstage = nff + 3

    def run(staging):
        state = [dict() for _ in range(POST_SPLIT)]
        done = -1
        if staging:
            job_copy(0).start()
            job_copy(1).start()
        for part, stage in POST_PROGRAM:
            st = state[part]
            if staging:
                need = 0 if stage < 2 else 2 * (stage - 2) + 2 if stage < nstage - 1 else done
                while done < need:
                    done += 1
                    job_finish(done)
            if stage == 0:
                st["ox"] = mix(part)
            elif stage == 1:
                st["x1"], st["h"] = norm1(part, st.pop("ox"))
            elif stage < nstage - 1:
                term = ff(st["h"], stage - 2)
                st["acc"] = term if stage == 2 else st["acc"] + term
            else:
                norm2(part, st["x1"], st["acc"])
        assert not staging or done == len(jobs) - 1

    @pl.when(pl.program_id(0) == 0)
    def _first_step():
        run(staging=True)

    @pl.when(pl.program_id(0) > 0)
    def _other_steps():
        run(staging=False)


def _residual_specs(residual):
    tm = ROW_TILE
    if len(residual) == 1:
        return [pl.BlockSpec((tm, D_MODEL), lambda i: (i, 0))]
    head_tiles = CTX_PAD // tm
    return [pl.BlockSpec((tm, D_MODEL), lambda i: (jnp.minimum(i, head_tiles - 1), 0)),
            pl.BlockSpec((tm, D_MODEL), lambda i: (jnp.maximum(i - head_tiles, 0), 0))]


def _post(residual, att, ret, s5_pairs, mods, layer, permt, wglu, bglu, wo, g1, b1, w1, w2, g2, b2, skip_context):
    rows = att.shape[0]
    tm = ROW_TILE
    off = CTX_PAD // tm if skip_context else 0
    assert not (skip_context and len(residual) > 1)
    assert FF_CHUNK == D_MODEL == wo.shape[1]
    row_blk = lambda width: pl.BlockSpec((tm, width), lambda i: (i + off, 0))
    full = lambda arr: pl.BlockSpec(arr.shape, lambda i: (0,) * arr.ndim)
    hbm = lambda arr: pl.BlockSpec(memory_space=pl.ANY)
    vec = lambda v: v.reshape(1, -1).astype(F32)
    small = [(permt, full), (wglu, full), (vec(bglu), full), (wo, hbm), (vec(g1), full), (vec(b1), full),
             (w1, hbm), (w2, hbm), (vec(g2), full), (vec(b2), full)]
    res_specs = [row_blk(D_MODEL)] if skip_context else _residual_specs(residual)
    return pl.pallas_call(
        functools.partial(_post_kernel, split=len(residual) > 1, layer=layer),
        grid=(rows // tm - off,),
        in_specs=res_specs + [row_blk(ATT_WIDTH), row_blk(RET_WIDTH),
                              pl.BlockSpec((S5_PAIRS, tm // S5_T, S5_T * S5_PAIR_W), lambda i: (0, i + off, 0)),
                              pl.BlockSpec((None, None, N_ADA, D_MODEL),
                                           lambda i: (layer, jnp.where(i + off == 0, 1, 0), 0, 0))]
                 + [spec(arr) for arr, spec in small],
        out_specs=pl.BlockSpec((tm, D_MODEL), lambda i: (i, 0)),
        out_shape=jax.ShapeDtypeStruct((rows - off * tm, D_MODEL), F32),
        scratch_shapes=[pltpu.VMEM((ATT_WIDTH + RET_WIDTH + S5_WIDTH, D_MODEL), MXU_DTYPE),
                        pltpu.VMEM((D_MODEL, D_FF), MXU_DTYPE), pltpu.VMEM((D_FF, D_MODEL), MXU_DTYPE),
                        pltpu.VMEM((2, D_MODEL, FF_CHUNK), F32),
                        pltpu.SemaphoreType.DMA((2,))],
        compiler_params=_params(("arbitrary",)),
        name="post",
    )(*residual, att, ret, s5_pairs, mods, *[arr for arr, _ in small])


def kernel(x, c, ctx, c_ctx, w_ada, b_ada, w_in, att_sink, ret_decay_logit, s5_lambda_re, s5_lambda_im, s5_b_re,
           s5_b_im, s5_c_re, s5_c_im, s5_log_dt, s5_d, w_glu, b_glu, w_out, ln1_g, ln1_b, w_ff1, w_ff2, ln2_g,
           ln2_b):
    assert x.shape[0] == 1 and x.shape[2] == D_MODEL and ctx.shape[1] == CTX_LEN
    seq = x.shape[1]
    assert seq % ROW_TILE == 0
    residual = (jnp.pad(ctx[0], ((0, CTX_PAD - CTX_LEN), (0, 0))), x[0])
    cond = jnp.zeros((8, D_MODEL), F32).at[0].set(c[0]).at[1].set(c_ctx)
    mods = _modulation(cond, w_ada, b_ada).reshape(DEPTH, 8, N_ADA, D_MODEL)
    tabs = _rope_tables(seq)
    masks = _attention_masks()
    perm = _chunk_perm(ROW_TILE // POST_SPLIT, MXU_DTYPE)
    permt = perm.T
    col_scale = jnp.ones((IN_WIDTH,), F32).at[COL_AQ:COL_AK].set(HEAD_DIM ** -0.5 * LOG2E)
    col_scale = col_scale.at[COL_RQ:COL_RK].set(HEAD_DIM ** -0.5)
    s5w = jax.vmap(_s5_weights)(s5_lambda_re, s5_lambda_im, s5_b_re, s5_b_im, s5_c_re, s5_c_im, s5_log_dt, s5_d)
    log_gamma = jax.nn.log_sigmoid(ret_decay_logit.astype(F32))
    for l in range(DEPTH):
        proj, u_pairs = _in_proj(residual, mods, l, w_in, col_scale.reshape(1, IN_WIDTH), tabs, perm)
        att = _attention(proj, att_sink[l].astype(F32) * LOG2E, masks)
        ret = _retention(proj, log_gamma[l])
        s5 = _s5_mixer(u_pairs, s5w, l)
        stream = _post(residual, att, ret, s5, mods, l, permt, w_glu[l].astype(MXU_DTYPE), b_glu[l],
                       w_out, ln1_g[l], ln1_b[l], w_ff1, w_ff2, ln2_g[l], ln2_b[l],
                       skip_context=(l == DEPTH - 1))
        residual = (stream,)
    return stream[None]
```

```python
import functools
import math

import jax
import jax.numpy as jnp
from jax import lax
from jax.experimental import pallas as pl
from jax.experimental.pallas import tpu as pltpu

F32 = jnp.float32
MXU_DTYPE = jnp.bfloat16

D_MODEL = 1024
DEPTH = 4
GRID_W = 64
CTX_LEN = 256
CTX_PAD = 512
HEAD_DIM = 64
ATT_HEADS = 8
ATT_KV_HEADS = 2
ATT_BLOCK = 128
ATT_LOOKAHEAD = 2
ROPE_BASE = 10000.0
RET_HEADS = 4
RET_CHUNK = 256
S5_CH = 16
S5_GROUPS = 16
S5_STATE = 64
S5_T = 16
S5_PAIRS = S5_GROUPS // 2
S5_PAIR_W = 2 * S5_CH
S5_TILE = CTX_PAD // S5_T
SCAN_ROWS = 8
S5_STATE_W = 2 * 2 * S5_STATE
ATT_WIDTH = ATT_HEADS * HEAD_DIM
KV_WIDTH = ATT_KV_HEADS * HEAD_DIM
RET_WIDTH = RET_HEADS * HEAD_DIM
S5_WIDTH = S5_GROUPS * S5_CH
IN_WIDTH = ATT_WIDTH + 2 * KV_WIDTH + 4 * RET_WIDTH + S5_WIDTH
D_FF = 4 * D_MODEL
FF_CHUNK = 1024
N_ADA = 6
LN_EPS = 1e-5
GN_EPS = 1e-5
DEEPNORM_ALPHA = (2 * DEPTH) ** 0.25
ROW_TILE = 512
POST_SPLIT = 2
_NFF = D_FF // FF_CHUNK
POST_PROGRAM = (((0, 0), (0, 1), (1, 0), (0, 2), (1, 1)) + tuple((0, 2 + c) for c in range(1, _NFF))
                + ((1, 2), (0, 2 + _NFF)) + tuple((1, 2 + c) for c in range(1, _NFF)) + ((1, 2 + _NFF),))
NEG_BIG = -1e30
LOG2E = math.log2(math.e)
LANES = 128
VMEM_LIMIT = 56 * 1024 * 1024
STAGE_BYTES = 2 * 1024 * 1024

COL_AQ, COL_AK, COL_AV = 0, ATT_WIDTH, ATT_WIDTH + KV_WIDTH
COL_RQ = ATT_WIDTH + 2 * KV_WIDTH
COL_RK, COL_RV, COL_RG = COL_RQ + RET_WIDTH, COL_RQ + 2 * RET_WIDTH, COL_RQ + 3 * RET_WIDTH
COL_S5 = COL_RQ + 4 * RET_WIDTH


def _mm(a, b):
    return jnp.dot(a, b, preferred_element_type=F32)


def _mm_nt(a, b):
    return lax.dot_general(a, b, (((1,), (1,)), ((), ())), preferred_element_type=F32)


def _mm_tn(a, b):
    return lax.dot_general(a, b, (((0,), (0,)), ((), ())), preferred_element_type=F32)


def _params(sem):
    return pltpu.CompilerParams(dimension_semantics=sem, vmem_limit_bytes=VMEM_LIMIT)


def _mod_kernel(cond_ref, w_ref, b_ref, o_ref):
    c = cond_ref[...]
    s = c * jax.nn.sigmoid(c)
    w = w_ref[0]
    split = lambda v: (v.astype(jnp.bfloat16), (v - v.astype(jnp.bfloat16).astype(F32)).astype(jnp.bfloat16))
    s_hi, s_lo = split(s)
    w_hi, w_lo = split(w)
    o_ref[0] = _mm(s_hi, w_hi) + (_mm(s_lo, w_hi) + _mm(s_hi, w_lo)) + b_ref[0]


def _modulation(cond, w_ada, b_ada):
    tn = 1536
    n = N_ADA * D_MODEL
    return pl.pallas_call(
        _mod_kernel,
        grid=(DEPTH, n // tn),
        in_specs=[
            pl.BlockSpec((8, D_MODEL), lambda l, j: (0, 0)),
            pl.BlockSpec((1, D_MODEL, tn), lambda l, j: (l, 0, j)),
            pl.BlockSpec((1, 1, tn), lambda l, j: (l, 0, j)),
        ],
        out_specs=pl.BlockSpec((1, 8, tn), lambda l, j: (l, 0, j)),
        out_shape=jax.ShapeDtypeStruct((DEPTH, 8, n), F32),
        compiler_params=_params(("arbitrary", "arbitrary")),
        name="modulation",
    )(cond, w_ada, b_ada.reshape(DEPTH, 1, n))


def _lane_block_shuffle(src_rows, src_lane_blk, out_vreg):
    acc = None
    for q in range(LANES // S5_PAIR_W):
        piece = src_rows(out_vreg * (LANES // S5_PAIR_W) + q)
        shift = (S5_PAIR_W * (q - src_lane_blk)) % LANES
        if shift:
            piece = pltpu.roll(piece, shift, 1)
        if acc is None:
            acc = piece
        else:
            lane_blk = lax.broadcasted_iota(jnp.int32, piece.shape, 1) // S5_PAIR_W
            acc = jnp.where(lane_blk == q, piece, acc)
    return acc


def _in_proj_kernel(*refs, split, layer):
    nres = 2 if split else 1
    (mod_ref, w_hbm, scale_ref, ca_ref, sa_ref, cr_ref, sr_ref, perm_ref, o_ref, u_ref, w_ref, stage_ref,
     sem_ref) = refs[nres:]

    @pl.when(pl.program_id(0) == 0)
    def _stage_weights():
        _load_cast(w_hbm.at[layer], w_ref, stage_ref, sem_ref, col_scale=scale_ref[...])

    x = _residual_rows(refs[:nres])(slice(None))
    h = (x * (1.0 + mod_ref[1:2, :]) + mod_ref[0:1, :]).astype(w_ref.dtype)
    lane = lax.broadcasted_iota(jnp.int32, (x.shape[0], LANES), 1)
    first_att = lane % (HEAD_DIM // 2) < HEAD_DIM // 4
    first_ret = lane % HEAD_DIM < HEAD_DIM // 2

    def proj(c0, c1):
        return _mm(h, w_ref[:, c0:c1])

    def rope_store(p, c0, width, cos, sin, first, half):
        for b in range(width // LANES):
            blk = p[:, LANES * b:LANES * (b + 1)]
            rot = jnp.where(first, pltpu.roll(blk, LANES - half, 1), pltpu.roll(blk, half, 1))
            o_ref[:, c0 + LANES * b:c0 + LANES * (b + 1)] = (blk * cos + rot * sin).astype(o_ref.dtype)

    def plain_store(p, c0, width):
        o_ref[:, c0:c0 + width] = p.astype(o_ref.dtype)

    u = proj(COL_S5, IN_WIDTH).astype(w_ref.dtype)
    sub = perm_ref.shape[0]
    nchunk = sub // S5_T
    for part in range(x.shape[0] // sub):
        g = _mm(perm_ref[...], u[sub * part:sub * (part + 1)])
        for a in range(S5_PAIRS):
            vreg_col, lane_blk = divmod(a * S5_PAIR_W, LANES)
            lane_blk //= S5_PAIR_W
            for v in range(S5_T * S5_PAIR_W // LANES):
                slab = _lane_block_shuffle(
                    lambda j: g[nchunk * j:nchunk * (j + 1), vreg_col * LANES:(vreg_col + 1) * LANES], lane_blk, v)
                u_ref[a, nchunk * part:nchunk * (part + 1), LANES * v:LANES * (v + 1)] = slab.astype(u_ref.dtype)

    ca, sa, cr, sr = ca_ref[...], sa_ref[...], cr_ref[...], sr_ref[...]
    att_rope = (ca, sa, first_att, HEAD_DIM // 4)
    ret_rope = (cr, sr, first_ret, HEAD_DIM // 2)
    groups = [(COL_AQ, ATT_WIDTH, att_rope), (COL_AK, KV_WIDTH, att_rope), (COL_RQ, RET_WIDTH, ret_rope),
              (COL_RK, RET_WIDTH, ret_rope), (COL_AV, KV_WIDTH, None), (COL_RV, COL_S5 - COL_RV, None)]
    pending = None
    for group in groups + [None]:
        nxt = None if group is None else (proj(group[0], group[0] + group[1]),) + group
        if pending is not None:
            p, pc0, pwidth, prope = pending
            if prope is None:
                plain_store(p, pc0, pwidth)
            else:
                rope_store(p, pc0, pwidth, *prope)
        pending = nxt


def _chunk_perm(tile_rows, dtype):
    nchunk = tile_rows // S5_T
    r = jnp.arange(tile_rows)
    src = S5_T * (r % nchunk) + r // nchunk
    return (src[:, None] == jnp.arange(tile_rows)[None, :]).astype(dtype)


def _in_proj(residual, mods, layer, w_in, col_scale, tabs, perm):
    rows = tabs[0].shape[0]
    tm = ROW_TILE
    tab_spec = pl.BlockSpec((tm, LANES), lambda i: (i, 0))
    nch = rows // S5_T
    return pl.pallas_call(
        functools.partial(_in_proj_kernel, split=len(residual) > 1, layer=layer),
        grid=(rows // tm,),
        in_specs=_residual_specs(residual) + [
            pl.BlockSpec((None, None, N_ADA, D_MODEL), lambda i: (layer, jnp.where(i == 0, 1, 0), 0, 0)),
            pl.BlockSpec(memory_space=pl.ANY),
            pl.BlockSpec((1, IN_WIDTH), lambda i: (0, 0)),
            tab_spec, tab_spec, tab_spec, tab_spec,
            pl.BlockSpec(perm.shape, lambda i: (0, 0)),
        ],
        out_specs=[pl.BlockSpec((tm, COL_S5), lambda i: (i, 0)),
                   pl.BlockSpec((S5_PAIRS, tm // S5_T, S5_T * S5_PAIR_W), lambda i: (0, i, 0))],
        out_shape=[jax.ShapeDtypeStruct((rows, COL_S5), MXU_DTYPE),
                   jax.ShapeDtypeStruct((S5_PAIRS, nch, S5_T * S5_PAIR_W), MXU_DTYPE)],
        scratch_shapes=[pltpu.VMEM((D_MODEL, IN_WIDTH), MXU_DTYPE),
                        pltpu.VMEM((2, STAGE_BYTES // (4 * IN_WIDTH), IN_WIDTH), F32),
                        pltpu.SemaphoreType.DMA((2,))],
        compiler_params=_params(("arbitrary",)),
        name="in_proj",
    )(*residual, mods, w_in, col_scale, *tabs, perm)


def _rope_tables(seq):
    half_a = HEAD_DIM // 4
    half_r = HEAD_DIM // 2
    nrow = seq // GRID_W
    inv_a = ROPE_BASE ** (-jnp.arange(half_a, dtype=F32) / half_a)
    inv_r = ROPE_BASE ** (-jnp.arange(half_r, dtype=F32) / half_r)
    ang_r = jnp.arange(nrow, dtype=F32)[:, None] * inv_a[None, :]
    ang_c = jnp.arange(GRID_W, dtype=F32)[:, None] * inv_a[None, :]
    ang_t = jnp.arange(seq, dtype=F32)[:, None] * inv_r[None, :]
    hp = lax.Precision.HIGHEST
    lane = jnp.arange(LANES)
    within = lane % HEAD_DIM
    pick_a = (within % half_a)[None, :] == jnp.arange(half_a)[:, None]
    exp_row = (pick_a & (within < 2 * half_a)[None, :]).astype(F32)
    exp_col = (pick_a & (within >= 2 * half_a)[None, :]).astype(F32)
    exp_t = ((lane % half_r)[None, :] == jnp.arange(half_r)[:, None]).astype(F32)
    sign_a = jnp.where(within % (2 * half_a) < half_a, -1.0, 1.0).astype(F32)
    sign_r = jnp.where(within < half_r, -1.0, 1.0).astype(F32)

    def att_table(fn):
        by_row = jnp.dot(fn(ang_r), exp_row, precision=hp)
        by_col = jnp.dot(fn(ang_c), exp_col, precision=hp)
        return (by_row[:, None, :] + by_col[None, :, :]).reshape(seq, LANES)

    cos_a = att_table(jnp.cos)
    sin_a = att_table(jnp.sin) * sign_a
    cos_r = jnp.dot(jnp.cos(ang_t), exp_t, precision=hp)
    sin_r = jnp.dot(jnp.sin(ang_t), exp_t, precision=hp) * sign_r
    pad = lambda tab, ident: jnp.pad(tab, ((CTX_PAD, 0), (0, 0)), constant_values=ident)
    return pad(cos_a, 1.0), pad(sin_a, 0.0), pad(cos_r, 1.0), pad(sin_r, 0.0)


def _swap_halves(x):
    if x.dtype.itemsize == 4:
        return pltpu.roll(x, 64, 1)
    packed = pltpu.bitcast(x, jnp.uint32)
    return pltpu.bitcast(pltpu.roll(packed, 64, 1), x.dtype)


def _dup_heads(x):
    sw = _swap_halves(x)
    lo = lax.broadcasted_iota(jnp.int32, x.shape, 1) < HEAD_DIM
    return jnp.where(lo, x, sw), jnp.where(lo, sw, x)


def _attn_kernel(sink_ref, q_ref, km_ref, kp_ref, kn_ref, vm_ref, vp_ref, vn_ref, kc_ref, vc_ref, mask_ref,
                 o_ref, k2_ref, v2_ref, kc2_ref, vc2_ref):
    i = pl.program_id(0)
    last_blk = pl.num_programs(0) * (ROW_TILE // ATT_BLOCK) - 1
    blk = ATT_BLOCK
    def spread(src, ones_upper):
        x = src[...]
        a, b = _dup_heads(x)
        if ones_upper:
            upper = lax.broadcasted_iota(jnp.int32, x.shape, 1) >= HEAD_DIM
            a = jnp.where(upper, jnp.ones_like(a), a)
            b = jnp.where(upper, jnp.ones_like(b), b)
        return a, b

    for dst, parts, is_v in ((k2_ref, (kp_ref, km_ref, kn_ref), False), (v2_ref, (vp_ref, vm_ref, vn_ref), True)):
        row = 0
        for part in parts:
            a, b = spread(part, is_v)
            n = part.shape[0]
            dst[0, row:row + n, :] = a
            dst[1, row:row + n, :] = b
            row += n
    for dst, src, is_v in ((kc2_ref, kc_ref, False), (vc2_ref, vc_ref, True)):
        a, b = spread(src, is_v)
        dst[0] = a
        dst[1] = b

    lo = lax.broadcasted_iota(jnp.int32, (blk, LANES), 1) < HEAD_DIM
    group = ATT_HEADS // ATT_KV_HEADS

    nloc = 3 * blk

    def scores(j, kv):
        r0 = j * blk
        qt = q_ref[r0:r0 + blk, group * HEAD_DIM * kv:group * HEAD_DIM * (kv + 1)]
        parts = []
        for g in range(group):
            qc = qt[:, LANES * (g // 2):LANES * (g // 2 + 1)]
            keep = lo if g % 2 == 0 else jnp.logical_not(lo)
            parts.append(jnp.where(keep, qc, jnp.zeros_like(qc)))
        qs = jnp.concatenate(parts, axis=0)
        return _mm_nt(qs, k2_ref[kv, r0:r0 + nloc, :]), _mm_nt(qs, kc2_ref[kv])

    def finish(j, kv, s_loc, s_ctx):
        r0 = j * blk
        gblk = i * (ROW_TILE // blk) + j
        sel = jnp.where(i == 0, 3, jnp.where(gblk == CTX_PAD // blk, 0, jnp.where(gblk == last_blk, 2, 1)))
        bias = mask_ref[sel]
        probs, sink_w = [], []
        for g in range(group):
            s = jnp.concatenate([s_loc[blk * g:blk * (g + 1)] + bias, s_ctx[blk * g:blk * (g + 1)]], axis=1)
            sk = sink_ref[group * kv + g]
            m = jnp.maximum(jnp.max(s, axis=-1, keepdims=True), sk)
            probs.append(jnp.exp2(s - m).astype(o_ref.dtype))
            sink_w.append(jnp.exp2(sk - m))
        p = jnp.concatenate(probs, axis=0)
        o = _mm(p[:, :nloc], v2_ref[kv, r0:r0 + nloc, :]) + _mm(p[:, nloc:], vc2_ref[kv])
        for half in range(group // 2):
            even, odd = 2 * half, 2 * half + 1
            oe = o[blk * even:blk * (even + 1)]
            oo = o[blk * odd:blk * (odd + 1)]
            y_even = oe * (1.0 / (pltpu.roll(oe, HEAD_DIM, 1) + sink_w[even]))
            y_odd = pltpu.roll(oo, HEAD_DIM, 1) * (1.0 / (oo + sink_w[odd]))
            c0 = group * HEAD_DIM * kv + LANES * half
            o_ref[r0:r0 + blk, c0:c0 + LANES] = jnp.where(lo, y_even, y_odd).astype(o_ref.dtype)

    items = [(j, kv) for j in range(ROW_TILE // blk) for kv in range(ATT_KV_HEADS)]
    pending = {}
    for t in range(len(items) + ATT_LOOKAHEAD):
        if t < len(items):
            pending[t] = scores(*items[t])
        if t >= ATT_LOOKAHEAD:
            finish(*items[t - ATT_LOOKAHEAD], *pending.pop(t - ATT_LOOKAHEAD))


def _attention_masks():
    qi = jnp.arange(ATT_BLOCK)[:, None]
    kj = jnp.arange(3 * ATT_BLOCK)[None, :]
    band = jnp.abs(kj - ATT_BLOCK - qi) <= ATT_BLOCK
    first = band & (kj >= ATT_BLOCK)
    last = band & (kj < 2 * ATT_BLOCK)
    none = jnp.zeros_like(band)
    masks = jnp.stack([first, band, last, none])
    return jnp.where(masks, 0.0, NEG_BIG).astype(F32)


def _attention(proj, sink, masks):
    rows = proj.shape[0]
    tm, blk = ROW_TILE, ATT_BLOCK
    per = tm // blk
    nblk = rows // blk
    ck, cv = COL_AK // KV_WIDTH, COL_AV // KV_WIDTH
    dt = proj.dtype
    return pl.pallas_call(
        _attn_kernel,
        grid=(rows // tm,),
        in_specs=[
            pl.BlockSpec(memory_space=pltpu.SMEM),
            pl.BlockSpec((tm, ATT_WIDTH), lambda i: (i, 0)),
            pl.BlockSpec((tm, KV_WIDTH), lambda i: (i, ck)),
            pl.BlockSpec((blk, KV_WIDTH), lambda i: (jnp.maximum(i * per - 1, 0), ck)),
            pl.BlockSpec((blk, KV_WIDTH), lambda i: (jnp.minimum((i + 1) * per, nblk - 1), ck)),
            pl.BlockSpec((tm, KV_WIDTH), lambda i: (i, cv)),
            pl.BlockSpec((blk, KV_WIDTH), lambda i: (jnp.maximum(i * per - 1, 0), cv)),
            pl.BlockSpec((blk, KV_WIDTH), lambda i: (jnp.minimum((i + 1) * per, nblk - 1), cv)),
            pl.BlockSpec((CTX_LEN, KV_WIDTH), lambda i: (0, ck)),
            pl.BlockSpec((CTX_LEN, KV_WIDTH), lambda i: (0, cv)),
            pl.BlockSpec((4, blk, 3 * blk), lambda i: (0, 0, 0)),
        ],
        out_specs=pl.BlockSpec((tm, ATT_WIDTH), lambda i: (i, 0)),
        out_shape=jax.ShapeDtypeStruct((rows, ATT_WIDTH), dt),
        scratch_shapes=[
            pltpu.VMEM((2, tm + 2 * blk, KV_WIDTH), dt),
            pltpu.VMEM((2, tm + 2 * blk, KV_WIDTH), dt),
            pltpu.VMEM((2, CTX_LEN, KV_WIDTH), dt),
            pltpu.VMEM((2, CTX_LEN, KV_WIDTH), dt),
        ],
        compiler_params=_params(("arbitrary",)),
        name="attention",
    )(sink, proj, proj, proj, proj, proj, proj, proj, proj, proj, masks)


def _ret_kernel(*refs, sup, ntile):
    lg_ref, q_ref = refs[:2]
    k_refs, v_refs = refs[2:2 + sup], refs[2 + sup:2 + 2 * sup]
    g_ref, o_ref, sb_ref, s_ref, dm_ref, tab_ref, gbd_ref = refs[2 + 2 * sup:]
    k_ref, v_ref = k_refs[0], v_refs[0]
    step = pl.program_id(0)
    nsup = (ntile - 1) // sup
    ph = jnp.where(step <= nsup, 0, 1)
    t = jnp.where(step <= nsup, step, step - (nsup + 1))
    c = RET_CHUNK
    w = RET_WIDTH
    per = q_ref.shape[0] // c
    mdt = sb_ref.dtype
    rows = lambda ci: slice(c * ci, c * (ci + 1))

    def lane_vec(direction, shape, axis):
        head = lax.broadcasted_iota(jnp.int32, shape, axis) // HEAD_DIM
        out = jnp.full(shape, lg_ref[direction, RET_HEADS - 1], F32)
        for h in range(RET_HEADS - 2, -1, -1):
            out = jnp.where(head == h, lg_ref[direction, h], out)
        return out

    @pl.when(jnp.logical_and(ph == 0, t == 0))
    def _init_tables():
        diff = (lax.broadcasted_iota(jnp.int32, (c, c), 0) - lax.broadcasted_iota(jnp.int32, (c, c), 1)).astype(F32)
        for h in range(RET_HEADS):
            dm_ref[h] = jnp.exp(jnp.where(diff >= 0, diff * lg_ref[0, h], -diff * lg_ref[1, h]))
        pos = lax.broadcasted_iota(jnp.int32, (c, w), 0).astype(F32)
        lgf = lane_vec(0, (c, w), 1)
        lgb = lane_vec(1, (c, w), 1)
        tab_ref[0] = jnp.exp((c - 1.0 - pos) * lgf)
        tab_ref[1] = jnp.exp((pos + 1.0) * lgf)
        tab_ref[2] = jnp.exp(pos * lgb)
        tab_ref[3] = jnp.exp((c - pos) * lgb)
        same = (lax.broadcasted_iota(jnp.int32, (w, w), 0) // HEAD_DIM
                == lax.broadcasted_iota(jnp.int32, (w, w), 1) // HEAD_DIM)
        bd = jnp.where(same, 1.0, 0.0)
        gbd_ref[0] = bd * jnp.exp(c * lane_vec(0, (w, w), 0))
        gbd_ref[1] = bd * jnp.exp(c * lane_vec(1, (w, w), 0))
        gbd_ref[2] = bd

    @pl.when(t == 0)
    def _reset_state():
        s_ref[...] = jnp.zeros_like(s_ref)

    def state_update(direction, key_tab, ci, kr=k_ref, vr=v_ref):
        kw = (kr[rows(ci), :].astype(F32) * tab_ref[key_tab]).astype(mdt)
        u = _mm_tn(kw, vr[rows(ci), :])
        s_ref[...] = gbd_ref[direction] * s_ref[...] + gbd_ref[2] * u

    @pl.when(jnp.logical_and(ph == 0, t == 0))
    def _backward_context():
        sb_ref[0] = s_ref[...].astype(mdt)
        state_update(1, 2, 0)

    @pl.when(jnp.logical_and(ph == 0, t > 0))
    def _backward_latent():
        base = 1 + per * sup * (nsup - t)
        for qi in range(sup - 1, -1, -1):
            for ci in range(per - 1, -1, -1):
                sb_ref[base + per * qi + ci] = s_ref[...].astype(mdt)
                state_update(1, 2, ci, k_refs[qi], v_refs[qi])

    head = lax.broadcasted_iota(jnp.int32, (c, w), 1) // HEAD_DIM

    def scores(ci):
        q = q_ref[rows(ci), :]
        qs = jnp.concatenate([jnp.where(head == h, q, jnp.zeros_like(q)) for h in range(RET_HEADS)], axis=0)
        return _mm_nt(qs, k_ref[rows(ci), :])

    def intra(ci, sc):
        scd = jnp.concatenate([sc[c * h:c * (h + 1)] * dm_ref[h] for h in range(RET_HEADS)], axis=0).astype(mdt)
        oi = _mm(scd, v_ref[rows(ci), :])
        o = jnp.where(head == 0, oi[0:c], 0.0)
        for h in range(1, RET_HEADS):
            o = o + jnp.where(head == h, oi[c * h:c * (h + 1)], 0.0)
        return o

    def cross(ci, idx):
        qf = q_ref[rows(ci), :].astype(F32)
        return (_mm((qf * tab_ref[1]).astype(mdt), s_ref[...].astype(mdt))
                + _mm((qf * tab_ref[3]).astype(mdt), sb_ref[idx]))

    def finish(ci, o):
        avg = (gbd_ref[2] * (1.0 / HEAD_DIM)).astype(mdt)
        o_hi = o.astype(mdt)
        d = o - (_mm(o_hi, avg) + _mm((o - o_hi.astype(F32)).astype(mdt), avg))
        var = _mm((d * d).astype(mdt), avg)
        gate = g_ref[rows(ci), :].astype(F32)
        y = d * lax.rsqrt(var + GN_EPS) * (gate * jax.nn.sigmoid(gate))
        o_ref[rows(ci), :] = y.astype(o_ref.dtype)

    @pl.when(jnp.logical_and(ph == 1, t == 0))
    def _forward_context():
        o = intra(0, scores(0)) + cross(0, 0)
        state_update(0, 0, 0)
        finish(0, o)
        for ci in range(1, per):
            o_ref[rows(ci), :] = jnp.zeros((c, w), o_ref.dtype)

    @pl.when(jnp.logical_and(ph == 1, t > 0))
    def _forward_latent():
        base = 1 + per * (t - 1)
        sc = [scores(ci) for ci in range(per)]
        outs = [intra(ci, sc[ci]) for ci in range(per)]
        for ci in range(per):
            outs[ci] = outs[ci] + cross(ci, base + ci)
            state_update(0, 0, ci)
        for ci in range(per):
            finish(ci, outs[ci])


def _retention(proj, log_gamma):
    rows = proj.shape[0]
    c = RET_CHUNK
    tm = ROW_TILE
    ntile = rows // tm
    nchunk = 1 + (rows - CTX_PAD) // c
    dt = proj.dtype
    cq, ckk, cvv, cg = (COL_RQ // RET_WIDTH, COL_RK // RET_WIDTH, COL_RV // RET_WIDTH, COL_RG // RET_WIDTH)

    nlat = ntile - 1
    sup = next(s for s in (4, 2, 1) if nlat % s == 0)
    nsup = nlat // sup
    nback = 1 + nsup

    def fw_blk(step):
        return jnp.maximum(step - nback, 0)

    def kv_spec(col, qi):
        def index(step):
            back = jnp.where(step == 0, 0, 1 + sup * (nsup - jnp.minimum(step, nsup)) + qi)
            fwd = step - nback if qi == 0 else 1 + qi
            return jnp.where(step < nback, back, fwd), col
        return pl.BlockSpec((tm, RET_WIDTH), index)

    return pl.pallas_call(
        functools.partial(_ret_kernel, sup=sup, ntile=ntile),
        grid=(nback + ntile,),
        in_specs=[pl.BlockSpec(memory_space=pltpu.SMEM),
                  pl.BlockSpec((tm, RET_WIDTH), lambda step: (fw_blk(step), cq))]
                 + [kv_spec(ckk, qi) for qi in range(sup)] + [kv_spec(cvv, qi) for qi in range(sup)]
                 + [pl.BlockSpec((tm, RET_WIDTH), lambda step: (fw_blk(step), cg))],
        out_specs=pl.BlockSpec((tm, RET_WIDTH), lambda step: (fw_blk(step), 0)),
        out_shape=jax.ShapeDtypeStruct((rows, RET_WIDTH), dt),
        scratch_shapes=[
            pltpu.VMEM((nchunk, RET_WIDTH, RET_WIDTH), dt),
            pltpu.VMEM((RET_WIDTH, RET_WIDTH), F32),
            pltpu.VMEM((RET_HEADS, c, c), F32),
            pltpu.VMEM((4, c, RET_WIDTH), F32),
            pltpu.VMEM((3, RET_WIDTH, RET_WIDTH), F32),
        ],
        compiler_params=_params(("arbitrary",)),
        name="retention",
    )(log_gamma, proj, *([proj] * (2 * sup)), proj)


def _s5_weights(lam_re, lam_im, b_re, b_im, c_re, c_im, log_dt, d_skip):
    tt, g, n, p, a = S5_T, S5_GROUPS, S5_STATE, S5_CH, S5_PAIRS
    lam = lax.complex(lam_re.astype(F32), lam_im.astype(F32))
    dtv = jnp.exp(log_dt.astype(F32))[..., None]
    lam_bar = jnp.exp(lam * dtv)
    bbar = ((lam_bar - 1.0) / lam)[..., None] * lax.complex(b_re.astype(F32), b_im.astype(F32))
    cmat = lax.complex(c_re.astype(F32), c_im.astype(F32))
    pw = [jnp.ones_like(lam_bar)]
    for _ in range(tt):
        pw.append(pw[-1] * lam_bar)
    pw = jnp.stack(pw, axis=1)
    eye2 = jnp.eye(2, dtype=F32)
    ri = lambda z, axis: jnp.stack([jnp.real(z), jnp.imag(z)], axis=axis)

    pw_l = pw.reshape(2, tt + 1, a, 2 * n)
    bbt = jnp.einsum('dahpn,gh->dagphn', jnp.swapaxes(bbar, -1, -2).reshape(2, a, 2, p, n), eye2)
    bbt = bbt.reshape(2, a, 2 * p, 2 * n)
    cct = jnp.einsum('dahpn,gh->dagphn', cmat.reshape(2, a, 2, p, n), eye2).reshape(2, a, 2 * p, 2 * n)
    pw_k = ri(pw_l, 1).transpose(3, 0, 1, 2, 4)
    b_k = ri(bbt, 2).transpose(1, 0, 2, 3, 4)
    c_k = ri(cct, 2).transpose(1, 0, 2, 3, 4)
    decay = [pw_l[:, tt]]
    for _ in range(SCAN_ROWS - 1):
        decay.append(decay[-1] * decay[0])
    decay = jnp.stack(decay, axis=0)
    rows8 = lambda z: jnp.broadcast_to(z[None], (SCAN_ROWS,) + z.shape)
    carry_w = jnp.stack([decay[:, 0], decay[::-1, 1]], axis=1)
    scan_tab = jnp.stack([rows8(decay[0]), rows8(decay[1]), rows8(decay[3]), carry_w], axis=0)
    scan_tab = ri(scan_tab, 0).transpose(4, 3, 1, 0, 2, 5)
    skip = jnp.tile(d_skip.astype(F32).reshape(a, 1, 2 * p), (1, tt, 1)).reshape(a, 1, tt * 2 * p)
    return pw_k, b_k, c_k, scan_tab, skip


def _pair_spec(layer, *shape):
    return pl.BlockSpec((None, None) + shape, lambda i: (layer, i) + (0,) * len(shape))


def _s5_drive_kernel(u_ref, pw_ref, b_ref, o_ref, w_ref):
    rows, half = S5_PAIR_W, LANES
    for d in range(2):
        br, bi = b_ref[d, 0], b_ref[d, 1]
        for j in range(S5_T):
            e = S5_T - 1 - j if d == 0 else j
            pr, pi = pw_ref[d, 0, e:e + 1, :], pw_ref[d, 1, e:e + 1, :]
            w_ref[rows * j:rows * (j + 1), 2 * half * d:2 * half * d + half] = (pr * br - pi * bi).astype(w_ref.dtype)
            w_ref[rows * j:rows * (j + 1), 2 * half * d + half:2 * half * (d + 1)] = (
                pr * bi + pi * br).astype(w_ref.dtype)
    o_ref[0] = _mm(u_ref[0], w_ref[...])


def _s5_drive(u_pairs, pw_k, b_k, layer):
    a, nch, wd = u_pairs.shape
    return pl.pallas_call(
        _s5_drive_kernel,
        grid=(a,),
        in_specs=[pl.BlockSpec((1, nch, wd), lambda i: (i, 0, 0)),
                  _pair_spec(layer, *pw_k.shape[2:]), _pair_spec(layer, *b_k.shape[2:])],
        out_specs=pl.BlockSpec((1, nch, 2 * S5_STATE_W), lambda i: (i, 0, 0)),
        out_shape=jax.ShapeDtypeStruct((a, nch, 2 * S5_STATE_W), F32),
        scratch_shapes=[pltpu.VMEM((wd, 2 * S5_STATE_W), u_pairs.dtype)],
        compiler_params=_params(("arbitrary",)),
        name="s5_drive",
    )(u_pairs, pw_k, b_k)


def _s5_scan_kernel(ef_ref, eb_ref, tab_ref, sf_ref, sb_ref, st_ref):
    t = pl.program_id(0)
    hw = LANES
    sub = SCAN_ROWS
    npair = tab_ref.shape[0]
    row = lax.broadcasted_iota(jnp.int32, (sub, hw), 0)

    @pl.when(t == 0)
    def _reset():
        st_ref[...] = jnp.zeros_like(st_ref)

    def shift(x, k, reverse):
        if reverse:
            return jnp.where(row < sub - k, pltpu.roll(x, sub - k, 0), 0.0)
        return jnp.where(row >= k, pltpu.roll(x, k, 0), 0.0)

    def scan_group(e_ref, a, d, r0, cr, ci):
        reverse = d == 1
        xr = e_ref[a, pl.ds(r0, sub), 0:hw]
        xi = e_ref[a, pl.ds(r0, sub), hw:2 * hw]
        for step, k in enumerate((1, 2, 4)):
            ar, ai = tab_ref[a, d, step, 0], tab_ref[a, d, step, 1]
            sr, si = shift(xr, k, reverse), shift(xi, k, reverse)
            xr, xi = xr + ar * sr - ai * si, xi + ar * si + ai * sr
        wr, wi = tab_ref[a, d, 3, 0], tab_ref[a, d, 3, 1]
        fr = xr + wr * cr - wi * ci
        fi = xi + wr * ci + wi * cr
        edge = sub - 1 if reverse else 0
        before = (jnp.where(row == edge, cr, shift(fr, 1, reverse)), jnp.where(row == edge, ci, shift(fi, 1, reverse)))
        last = 0 if reverse else sub - 1
        after = (jnp.broadcast_to(fr[last:last + 1], (sub, hw)), jnp.broadcast_to(fi[last:last + 1], (sub, hw)))
        return before, after

    def store(o_ref, a, r0, lower, upper):
        for part in range(2):
            val = jnp.concatenate([lower[part], upper[part]], axis=0)
            o_ref[a, pl.ds(r0, 2 * sub), part * hw:(part + 1) * hw] = val.astype(o_ref.dtype)

    def run(nrows):
        span = 2 * sub
        nspan = nrows // span

        def body(gi, carry):
            r0 = pl.multiple_of(gi * span, span)
            rb0 = pl.multiple_of((nspan - 1 - gi) * span, span)
            new = []
            for a in range(npair):
                fr, fi, br, bi = carry[a]
                f_lo, (fr, fi) = scan_group(ef_ref, a, 0, r0, fr, fi)
                f_hi, (fr, fi) = scan_group(ef_ref, a, 0, r0 + sub, fr, fi)
                store(sf_ref, a, r0, f_lo, f_hi)
                b_hi, (br, bi) = scan_group(eb_ref, a, 1, rb0 + sub, br, bi)
                b_lo, (br, bi) = scan_group(eb_ref, a, 1, rb0, br, bi)
                store(sb_ref, a, rb0, b_lo, b_hi)
                new.append((fr, fi, br, bi))
            return tuple(new)

        init = tuple(tuple(st_ref[a, k] for k in range(4)) for a in range(npair))
        final = lax.fori_loop(0, nspan, body, init)
        for a in range(npair):
            for k in range(4):
                st_ref[a, k] = final[a][k]

    @pl.when(t == 0)
    def _context():
        sf_ref[...] = jnp.zeros_like(sf_ref)
        sb_ref[...] = jnp.zeros_like(sb_ref)
        run(CTX_LEN // S5_T)

    @pl.when(t > 0)
    def _latent():
        run(S5_TILE)


def _s5_scan(drive, scan_tab, layer):
    a, nch, wd2 = drive.shape
    wd = wd2 // 2
    nt = nch // S5_TILE

    def bwd(t):
        return jnp.where(t == 0, 0, nt - t)

    return pl.pallas_call(
        _s5_scan_kernel,
        grid=(nt,),
        in_specs=[pl.BlockSpec((a, S5_TILE, wd), lambda t: (0, t, 0)),
                  pl.BlockSpec((a, S5_TILE, wd), lambda t: (0, bwd(t), 1)),
                  pl.BlockSpec((None,) + scan_tab.shape[1:], lambda t: (layer,) + (0,) * (scan_tab.ndim - 1))],
        out_specs=[pl.BlockSpec((a, S5_TILE, wd), lambda t: (0, t, 0)),
                   pl.BlockSpec((a, S5_TILE, wd), lambda t: (0, bwd(t), 0))],
        out_shape=[jax.ShapeDtypeStruct((a, nch, wd), MXU_DTYPE), jax.ShapeDtypeStruct((a, nch, wd), MXU_DTYPE)],
        scratch_shapes=[pltpu.VMEM((a, 4, SCAN_ROWS, LANES), F32)],
        compiler_params=_params(("arbitrary",)),
        name="s5_scan",
    )(drive, drive, scan_tab)


def _lane_window(x, start, width):
    cols = []
    for v in range(width // LANES):
        k0, off = divmod(start + LANES * v, LANES)
        lo = x[:, LANES * k0:LANES * (k0 + 1)]
        if off:
            hi = x[:, LANES * (k0 + 1):LANES * (k0 + 2)]
            lane = lax.broadcasted_iota(jnp.int32, lo.shape, 1)
            lo = jnp.where(lane < LANES - off, pltpu.roll(lo, LANES - off, 1), pltpu.roll(hi, LANES - off, 1))
        cols.append(lo)
    return jnp.concatenate(cols, axis=1)


def _s5_read_kernel(u_ref, sf_ref, sb_ref, pw_ref, b_ref, c_ref, skip_ref, o_ref, wt_ref, wi_ref, lag_ref):
    u = u_ref[0]
    mdt = u.dtype
    rows, half = S5_PAIR_W, LANES
    for d in range(2):
        cr, ci = c_ref[d, 0], c_ref[d, 1]
        for i in range(S5_T):
            e = i + 1 if d == 0 else S5_T - i
            pr, pi = pw_ref[d, 0, e:e + 1, :], pw_ref[d, 1, e:e + 1, :]
            wt_ref[d, rows * i:rows * (i + 1), 0:half] = (pr * cr - pi * ci).astype(mdt)
            wt_ref[d, rows * i:rows * (i + 1), half:2 * half] = (-(pr * ci + pi * cr)).astype(mdt)
    nlag = 2 * S5_T - 1
    ldt = lag_ref.dtype
    zero = jnp.zeros((rows, half), ldt)
    for l in range(nlag + 1):
        lag = l - (S5_T - 1)
        for d, active in ((1, lag <= 0), (0, 0 <= lag < S5_T)):
            col = 2 * half * (1 - d)
            if active:
                cr, ci = c_ref[d, 0], c_ref[d, 1]
                pr, pi = pw_ref[d, 0, abs(lag):abs(lag) + 1, :], pw_ref[d, 1, abs(lag):abs(lag) + 1, :]
                lag_ref[rows * l:rows * (l + 1), col:col + half] = (pr * cr - pi * ci).astype(ldt)
                lag_ref[rows * l:rows * (l + 1), col + half:col + 2 * half] = (pr * ci + pi * cr).astype(ldt)
            else:
                lag_ref[rows * l:rows * (l + 1), col:col + half] = zero
                lag_ref[rows * l:rows * (l + 1), col + half:col + 2 * half] = zero
    lhs = jnp.concatenate([b_ref[1, 0], -b_ref[1, 1], b_ref[0, 0], -b_ref[0, 1]], axis=1).astype(ldt)
    kall = _mm_nt(lhs, lag_ref[...])
    for j in range(S5_T):
        wi_ref[rows * j:rows * (j + 1), :] = _lane_window(kall, rows * (S5_T - 1 - j), S5_T * rows).astype(mdt)
    y = _mm(u, wi_ref[...])
    y = y + _mm_nt(sf_ref[0].astype(mdt), wt_ref[0])
    y = y + _mm_nt(sb_ref[0].astype(mdt), wt_ref[1])
    o_ref[0] = y + u.astype(F32) * skip_ref[...]


def _s5_read(u_pairs, sf, sb, pw_k, b_k, c_k, skip, layer):
    a, nch, wd = u_pairs.shape
    blk = lambda *shape: pl.BlockSpec((1,) + shape, lambda i: (i, 0, 0))
    return pl.pallas_call(
        _s5_read_kernel,
        grid=(a,),
        in_specs=[blk(nch, wd), blk(nch, S5_STATE_W), blk(nch, S5_STATE_W), _pair_spec(layer, *pw_k.shape[2:]),
                  _pair_spec(layer, *b_k.shape[2:]), _pair_spec(layer, *c_k.shape[2:]),
                  _pair_spec(layer, *skip.shape[2:])],
        out_specs=blk(nch, wd),
        out_shape=jax.ShapeDtypeStruct((a, nch, wd), F32),
        scratch_shapes=[pltpu.VMEM((2, wd, S5_STATE_W), u_pairs.dtype), pltpu.VMEM((wd, wd), u_pairs.dtype),
                        pltpu.VMEM((2 * wd, 2 * S5_STATE_W), u_pairs.dtype)],
        compiler_params=_params(("arbitrary",)),
        name="s5_read",
    )(u_pairs, sf, sb, pw_k, b_k, c_k, skip)


def _s5_fused_kernel(u_ref, pw_ref, b_ref, c_ref, tab_ref, skip_ref, o_ref, e_sc, sf_sc, sb_sc, w_sc, wt_sc, wi_sc,
                     lag_sc):
    step = pl.program_id(0)
    npair = S5_PAIRS
    nch = u_ref.shape[1]
    hw, sub = LANES, SCAN_ROWS
    span = 2 * sub
    ctx_rows = CTX_LEN // S5_T

    @pl.when(step < npair)
    def _drive():
        _s5_drive_kernel(u_ref, pw_ref, b_ref, e_sc.at[pl.ds(step, 1)], w_sc)

    @pl.when(step == npair)
    def _scan():
        row = lax.broadcasted_iota(jnp.int32, (sub, hw), 0)

        def shift(x, k, reverse):
            if reverse:
                return jnp.where(row < sub - k, pltpu.roll(x, sub - k, 0), 0.0)
            return jnp.where(row >= k, pltpu.roll(x, k, 0), 0.0)

        def scan_group(a, d, r0, cr, ci):
            reverse = d == 1
            xr = e_sc[a, pl.ds(r0, sub), 2 * hw * d:2 * hw * d + hw]
            xi = e_sc[a, pl.ds(r0, sub), 2 * hw * d + hw:2 * hw * (d + 1)]
            for k_i, k in enumerate((1, 2, 4)):
                ar, ai = tab_ref[a, d, k_i, 0], tab_ref[a, d, k_i, 1]
                sr, si = shift(xr, k, reverse), shift(xi, k, reverse)
                xr, xi = xr + ar * sr - ai * si, xi + ar * si + ai * sr
            wr, wi = tab_ref[a, d, 3, 0], tab_ref[a, d, 3, 1]
            fr = xr + wr * cr - wi * ci
            fi = xi + wr * ci + wi * cr
            edge = sub - 1 if reverse else 0
            before = (jnp.where(row == edge, cr, shift(fr, 1, reverse)),
                      jnp.where(row == edge, ci, shift(fi, 1, reverse)))
            last = 0 if reverse else sub - 1
            after = (jnp.broadcast_to(fr[last:last + 1], (sub, hw)), jnp.broadcast_to(fi[last:last + 1], (sub, hw)))
            return before, after

        def store(dst, a, r0, lower, upper):
            for part in range(2):
                val = jnp.concatenate([lower[part], upper[part]], axis=0)
                dst[a, pl.ds(r0, span), part * hw:(part + 1) * hw] = val.astype(dst.dtype)

        def do_span(carry, rf0, rb0):
            new = []
            for a in range(npair):
                fr, fi, br, bi = carry[a]
                f_lo, (fr, fi) = scan_group(a, 0, rf0, fr, fi)
                f_hi, (fr, fi) = scan_group(a, 0, rf0 + sub, fr, fi)
                store(sf_sc, a, rf0, f_lo, f_hi)
                b_hi, (br, bi) = scan_group(a, 1, rb0 + sub, br, bi)
                b_lo, (br, bi) = scan_group(a, 1, rb0, br, bi)
                store(sb_sc, a, rb0, b_lo, b_hi)
                new.append((fr, fi, br, bi))
            return tuple(new)

        sf_sc[:, ctx_rows:S5_TILE, :] = jnp.zeros((npair, S5_TILE - ctx_rows, 2 * hw), sf_sc.dtype)
        sb_sc[:, ctx_rows:S5_TILE, :] = jnp.zeros((npair, S5_TILE - ctx_rows, 2 * hw), sb_sc.dtype)
        zero = jnp.zeros((sub, hw), F32)
        carry = tuple((zero, zero, zero, zero) for _ in range(npair))
        for g in range(ctx_rows // span):
            carry = do_span(carry, span * g, ctx_rows - span * (g + 1))

        def body(gi, carry):
            rf0 = pl.multiple_of(S5_TILE + gi * span, span)
            rb0 = pl.multiple_of(nch - span - gi * span, span)
            return do_span(carry, rf0, rb0)

        lax.fori_loop(0, (nch - S5_TILE) // span, body, carry)

    @pl.when(step > npair)
    def _read():
        pair = step - npair - 1
        _s5_read_kernel(u_ref, sf_sc.at[pl.ds(pair, 1)], sb_sc.at[pl.ds(pair, 1)], pw_ref, b_ref, c_ref, skip_ref,
                        o_ref, wt_sc, wi_sc, lag_sc)


def _s5_mixer(u_pairs, weights, layer):
    pw_k, b_k, c_k, scan_tab, skip = weights
    a, nch, wd = u_pairs.shape
    dt = u_pairs.dtype

    def in_pair(step):
        return jnp.where(step < a, step, jnp.where(step == a, a - 1, step - a - 1))

    pair_blk = lambda arr: pl.BlockSpec((None, None) + arr.shape[2:],
                                        lambda s: (layer, in_pair(s)) + (0,) * (arr.ndim - 2))
    return pl.pallas_call(
        _s5_fused_kernel,
        grid=(2 * a + 1,),
        in_specs=[pl.BlockSpec((1, nch, wd), lambda s: (in_pair(s), 0, 0)),
                  pair_blk(pw_k), pair_blk(b_k), pair_blk(c_k),
                  pl.BlockSpec((None,) + scan_tab.shape[1:], lambda s: (layer,) + (0,) * (scan_tab.ndim - 1)),
                  pair_blk(skip)],
        out_specs=pl.BlockSpec((1, nch, wd), lambda s: (jnp.maximum(s - a - 1, 0), 0, 0)),
        out_shape=jax.ShapeDtypeStruct((a, nch, wd), F32),
        scratch_shapes=[pltpu.VMEM((a, nch, 2 * S5_STATE_W), F32),
                        pltpu.VMEM((a, nch, S5_STATE_W), dt), pltpu.VMEM((a, nch, S5_STATE_W), dt),
                        pltpu.VMEM((wd, 2 * S5_STATE_W), dt),
                        pltpu.VMEM((2, wd, S5_STATE_W), dt), pltpu.VMEM((wd, wd), dt),
                        pltpu.VMEM((2 * wd, 2 * S5_STATE_W), dt)],
        compiler_params=_params(("arbitrary",)),
        name="s5",
    )(u_pairs, pw_k, b_k, c_k, scan_tab, skip)


def _layer_norm(x, g, b):
    mu = jnp.mean(x, axis=-1, keepdims=True)
    d = x - mu
    var = jnp.mean(d * d, axis=-1, keepdims=True)
    return d * lax.rsqrt(var + LN_EPS) * g + b


def _residual_rows(refs):
    if len(refs) == 1:
        return lambda rows: refs[0][rows, :]
    head_ref, body_ref = refs
    is_head = pl.program_id(0) == 0
    return lambda rows: jnp.where(is_head, head_ref[rows, :], body_ref[rows, :])


def _load_cast(src_hbm, dst_ref, stage_ref, sems, col_scale=None):
    chunk = stage_ref.shape[1]
    nchunk = src_hbm.shape[0] // chunk

    def copy(k):
        slot = k % 2
        return pltpu.make_async_copy(src_hbm.at[pl.ds(k * chunk, chunk), :], stage_ref.at[slot], sems.at[slot])

    copy(0).start()
    for k in range(nchunk):
        if k + 1 < nchunk:
            copy(k + 1).start()
        copy(k).wait()
        vals = stage_ref[k % 2]
        if col_scale is not None:
            vals = vals * col_scale
        dst_ref[k * chunk:(k + 1) * chunk, :] = vals.astype(dst_ref.dtype)


def _post_kernel(*refs, split, layer):
    nres = 2 if split else 1
    load_x = _residual_rows(refs[:nres])
    (att_ref, ret_ref, s5_ref, mod_ref, permt_ref, wglu_ref, bglu_ref, wo_hbm, g1_ref, b1_ref, w1_hbm, w2_hbm,
     g2_ref, b2_ref, o_ref, wo_ref, w1_ref, w2_ref, stage_ref, sem_ref) = refs[nres:]
    woa_ref = wo_ref.at[0:ATT_WIDTH]
    wor_ref = wo_ref.at[ATT_WIDTH:ATT_WIDTH + RET_WIDTH]
    wos_ref = wo_ref.at[ATT_WIDTH + RET_WIDTH:ATT_WIDTH + RET_WIDTH + S5_WIDTH]
    mdt = w1_ref.dtype
    sub = o_ref.shape[0] // POST_SPLIT
    csub = sub // S5_T
    nff = D_FF // FF_CHUNK

    blocks = lambda c: slice(FF_CHUNK * c, FF_CHUNK * (c + 1))
    jobs = [(wo_hbm.at[layer], wo_ref)]
    for c in range(nff):
        jobs.append((w1_hbm.at[layer, :, blocks(c)], w1_ref.at[:, blocks(c)]))
        jobs.append((w2_hbm.at[layer, blocks(c), :], w2_ref.at[blocks(c), :]))

    def job_copy(k):
        return pltpu.make_async_copy(jobs[k][0], stage_ref.at[k % 2], sem_ref.at[k % 2])

    def job_finish(k):
        job_copy(k).wait()
        jobs[k][1][...] = stage_ref[k % 2].astype(mdt)
        if k + 2 < len(jobs):
            job_copy(k + 2).start()

    def mix(part):
        rows = slice(sub * part, sub * (part + 1))
        zrows = []
        for i in range(S5_T):
            src_vreg, src_blk = divmod(i * S5_PAIR_W, LANES)
            src_blk //= S5_PAIR_W
            cols = [_lane_block_shuffle(
                lambda a: s5_ref[a, csub * part:csub * (part + 1), src_vreg * LANES:(src_vreg + 1) * LANES],
                src_blk, w) for w in range(S5_WIDTH // LANES)]
            zrows.append(jnp.concatenate(cols, axis=1))
        hs = jax.nn.gelu(jnp.concatenate(zrows, axis=0))
        gate = jax.nn.sigmoid(_mm(hs.astype(mdt), wglu_ref[...]) + bglu_ref[...])
        s5 = _mm(permt_ref[...], (hs * gate).astype(mdt)).astype(mdt)
        return _mm(att_ref[rows, :], woa_ref[...]) + _mm(ret_ref[rows, :], wor_ref[...]) + _mm(s5, wos_ref[...])

    def norm1(part, ox):
        rows = slice(sub * part, sub * (part + 1))
        x1 = _layer_norm(DEEPNORM_ALPHA * load_x(rows) + mod_ref[2:3, :] * ox, g1_ref[...], b1_ref[...])
        return x1, (x1 * (1.0 + mod_ref[4:5, :]) + mod_ref[3:4, :]).astype(mdt)

    def ff(h, c):
        a = _mm(h, w1_ref[:, FF_CHUNK * c:FF_CHUNK * (c + 1)])
        a = jnp.square(jnp.maximum(a, 0.0)).astype(mdt)
        return _mm(a, w2_ref[FF_CHUNK * c:FF_CHUNK * (c + 1), :])

    def norm2(part, x1, acc):
        rows = slice(sub * part, sub * (part + 1))
        o_ref[rows, :] = _layer_norm(DEEPNORM_ALPHA * x1 + mod_ref[5:6, :] * acc, g2_ref[...], b2_ref[...])

    nstage = nff + 3

    def run(staging):
        state = [dict() for _ in range(POST_SPLIT)]
        done = -1
        if staging:
            job_copy(0).start()
            job_copy(1).start()
        for part, stage in POST_PROGRAM:
            st = state[part]
            if staging:
                need = 0 if stage < 2 else 2 * (stage - 2) + 2 if stage < nstage - 1 else done
                while done < need:
                    done += 1
                    job_finish(done)
            if stage == 0:
                st["ox"] = mix(part)
            elif stage == 1:
                st["x1"], st["h"] = norm1(part, st.pop("ox"))
            elif stage < nstage - 1:
                term = ff(st["h"], stage - 2)
                st["acc"] = term if stage == 2 else st["acc"] + term
            else:
                norm2(part, st["x1"], st["acc"])
        assert not staging or done == len(jobs) - 1

    @pl.when(pl.program_id(0) == 0)
    def _first_step():
        run(staging=True)

    @pl.when(pl.program_id(0) > 0)
    def _other_steps():
        run(staging=False)


def _residual_specs(residual):
    tm = ROW_TILE
    if len(residual) == 1:
        return [pl.BlockSpec((tm, D_MODEL), lambda i: (i, 0))]
    head_tiles = CTX_PAD // tm
    return [pl.BlockSpec((tm, D_MODEL), lambda i: (jnp.minimum(i, head_tiles - 1), 0)),
            pl.BlockSpec((tm, D_MODEL), lambda i: (jnp.maximum(i - head_tiles, 0), 0))]


def _post(residual, att, ret, s5_pairs, mods, layer, permt, wglu, bglu, wo, g1, b1, w1, w2, g2, b2, skip_context):
    rows = att.shape[0]
    tm = ROW_TILE
    off = CTX_PAD // tm if skip_context else 0
    assert not (skip_context and len(residual) > 1)
    assert FF_CHUNK == D_MODEL == wo.shape[1]
    row_blk = lambda width: pl.BlockSpec((tm, width), lambda i: (i + off, 0))
    full = lambda arr: pl.BlockSpec(arr.shape, lambda i: (0,) * arr.ndim)
    hbm = lambda arr: pl.BlockSpec(memory_space=pl.ANY)
    vec = lambda v: v.reshape(1, -1).astype(F32)
    small = [(permt, full), (wglu, full), (vec(bglu), full), (wo, hbm), (vec(g1), full), (vec(b1), full),
             (w1, hbm), (w2, hbm), (vec(g2), full), (vec(b2), full)]
    res_specs = [row_blk(D_MODEL)] if skip_context else _residual_specs(residual)
    return pl.pallas_call(
        functools.partial(_post_kernel, split=len(residual) > 1, layer=layer),
        grid=(rows // tm - off,),
        in_specs=res_specs + [row_blk(ATT_WIDTH), row_blk(RET_WIDTH),
                              pl.BlockSpec((S5_PAIRS, tm // S5_T, S5_T * S5_PAIR_W), lambda i: (0, i + off, 0)),
                              pl.BlockSpec((None, None, N_ADA, D_MODEL),
                                           lambda i: (layer, jnp.where(i + off == 0, 1, 0), 0, 0))]
                 + [spec(arr) for arr, spec in small],
        out_specs=pl.BlockSpec((tm, D_MODEL), lambda i: (i, 0)),
        out_shape=jax.ShapeDtypeStruct((rows - off * tm, D_MODEL), F32),
        scratch_shapes=[pltpu.VMEM((ATT_WIDTH + RET_WIDTH + S5_WIDTH, D_MODEL), MXU_DTYPE),
                        pltpu.VMEM((D_MODEL, D_FF), MXU_DTYPE), pltpu.VMEM((D_FF, D_MODEL), MXU_DTYPE),
                        pltpu.VMEM((2, D_MODEL, FF_CHUNK), F32),
                        pltpu.SemaphoreType.DMA((2,))],
        compiler_params=_params(("arbitrary",)),
        name="post",
    )(*residual, att, ret, s5_pairs, mods, *[arr for arr, _ in small])


def kernel(x, c, ctx, c_ctx, w_ada, b_ada, w_in, att_sink, ret_decay_logit, s5_lambda_re, s5_lambda_im, s5_b_re,
           s5_b_im, s5_c_re, s5_c_im, s5_log_dt, s5_d, w_glu, b_glu, w_out, ln1_g, ln1_b, w_ff1, w_ff2, ln2_g,
           ln2_b):
    assert x.shape[0] == 1 and x.shape[2] == D_MODEL and ctx.shape[1] == CTX_LEN
    seq = x.shape[1]
    assert seq % ROW_TILE == 0
    residual = (jnp.pad(ctx[0], ((0, CTX_PAD - CTX_LEN), (0, 0))), x[0])
    cond = jnp.zeros((8, D_MODEL), F32).at[0].set(c[0]).at[1].set(c_ctx)
    mods = _modulation(cond, w_ada, b_ada).reshape(DEPTH, 8, N_ADA, D_MODEL)
    tabs = _rope_tables(seq)
    masks = _attention_masks()
    perm = _chunk_perm(ROW_TILE // POST_SPLIT, MXU_DTYPE)
    permt = perm.T
    col_scale = jnp.ones((IN_WIDTH,), F32).at[COL_AQ:COL_AK].set(HEAD_DIM ** -0.5 * LOG2E)
    col_scale = col_scale.at[COL_RQ:COL_RK].set(HEAD_DIM ** -0.5)
    s5w = jax.vmap(_s5_weights)(s5_lambda_re, s5_lambda_im, s5_b_re, s5_b_im, s5_c_re, s5_c_im, s5_log_dt, s5_d)
    log_gamma = jax.nn.log_sigmoid(ret_decay_logit.astype(F32))
    for l in range(DEPTH):
        proj, u_pairs = _in_proj(residual, mods, l, w_in, col_scale.reshape(1, IN_WIDTH), tabs, perm)
        att = _attention(proj, att_sink[l].astype(F32) * LOG2E, masks)
        ret = _retention(proj, log_gamma[l])
        s5 = _s5_mixer(u_pairs, s5w, l)
        stream = _post(residual, att, ret, s5, mods, l, permt, w_glu[l].astype(MXU_DTYPE), b_glu[l],
                       w_out, ln1_g[l], ln1_b[l], w_ff1, w_ff2, ln2_g[l], ln2_b[l],
                       skip_context=(l == DEPTH - 1))
        residual = (stream,)
    return stream[None]
```

```python
import functools
import math

import jax
import jax.numpy as jnp
from jax import lax
from jax.experimental import pallas as pl
from jax.experimental.pallas import tpu as pltpu

F32 = jnp.float32
MXU_DTYPE = jnp.bfloat16

D_MODEL = 1024
DEPTH = 4
GRID_W = 64
CTX_LEN = 256
CTX_PAD = 512
HEAD_DIM = 64
ATT_HEADS = 8
ATT_KV_HEADS = 2
ATT_BLOCK = 128
ATT_LOOKAHEAD = 2
ROPE_BASE = 10000.0
RET_HEADS = 4
RET_CHUNK = 256
S5_CH = 16
S5_GROUPS = 16
S5_STATE = 64
S5_T = 16
S5_PAIRS = S5_GROUPS // 2
S5_PAIR_W = 2 * S5_CH
S5_TILE = CTX_PAD // S5_T
SCAN_ROWS = 8
S5_STATE_W = 2 * 2 * S5_STATE
ATT_WIDTH = ATT_HEADS * HEAD_DIM
KV_WIDTH = ATT_KV_HEADS * HEAD_DIM
RET_WIDTH = RET_HEADS * HEAD_DIM
S5_WIDTH = S5_GROUPS * S5_CH
IN_WIDTH = ATT_WIDTH + 2 * KV_WIDTH + 4 * RET_WIDTH + S5_WIDTH
D_FF = 4 * D_MODEL
FF_CHUNK = 1024
N_ADA = 6
LN_EPS = 1e-5
GN_EPS = 1e-5
DEEPNORM_ALPHA = (2 * DEPTH) ** 0.25
ROW_TILE = 512
POST_SPLIT = 2
_NFF = D_FF // FF_CHUNK
POST_PROGRAM = (((0, 0), (0, 1), (1, 0), (0, 2), (1, 1)) + tuple((0, 2 + c) for c in range(1, _NFF))
                + ((1, 2), (0, 2 + _NFF)) + tuple((1, 2 + c) for c in range(1, _NFF)) + ((1, 2 + _NFF),))
NEG_BIG = -1e30
LOG2E = math.log2(math.e)
LANES = 128
VMEM_LIMIT = 56 * 1024 * 1024
STAGE_BYTES = 2 * 1024 * 1024

COL_AQ, COL_AK, COL_AV = 0, ATT_WIDTH, ATT_WIDTH + KV_WIDTH
COL_RQ = ATT_WIDTH + 2 * KV_WIDTH
COL_RK, COL_RV, COL_RG = COL_RQ + RET_WIDTH, COL_RQ + 2 * RET_WIDTH, COL_RQ + 3 * RET_WIDTH
COL_S5 = COL_RQ + 4 * RET_WIDTH


def _mm(a, b):
    return jnp.dot(a, b, preferred_element_type=F32)


def _mm_nt(a, b):
    return lax.dot_general(a, b, (((1,), (1,)), ((), ())), preferred_element_type=F32)


def _mm_tn(a, b):
    return lax.dot_general(a, b, (((0,), (0,)), ((), ())), preferred_element_type=F32)


def _params(sem):
    return pltpu.CompilerParams(dimension_semantics=sem, vmem_limit_bytes=VMEM_LIMIT)


def _mod_kernel(cond_ref, w_ref, b_ref, o_ref):
    c = cond_ref[...]
    s = c * jax.nn.sigmoid(c)
    w = w_ref[0]
    split = lambda v: (v.astype(jnp.bfloat16), (v - v.astype(jnp.bfloat16).astype(F32)).astype(jnp.bfloat16))
    s_hi, s_lo = split(s)
    w_hi, w_lo = split(w)
    o_ref[0] = _mm(s_hi, w_hi) + (_mm(s_lo, w_hi) + _mm(s_hi, w_lo)) + b_ref[0]


def _modulation(cond, w_ada, b_ada):
    tn = 1536
    n = N_ADA * D_MODEL
    return pl.pallas_call(
        _mod_kernel,
        grid=(DEPTH, n // tn),
        in_specs=[
            pl.BlockSpec((8, D_MODEL), lambda l, j: (0, 0)),
            pl.BlockSpec((1, D_MODEL, tn), lambda l, j: (l, 0, j)),
            pl.BlockSpec((1, 1, tn), lambda l, j: (l, 0, j)),
        ],
        out_specs=pl.BlockSpec((1, 8, tn), lambda l, j: (l, 0, j)),
        out_shape=jax.ShapeDtypeStruct((DEPTH, 8, n), F32),
        compiler_params=_params(("arbitrary", "arbitrary")),
        name="modulation",
    )(cond, w_ada, b_ada.reshape(DEPTH, 1, n))


def _lane_block_shuffle(src_rows, src_lane_blk, out_vreg):
    acc = None
    for q in range(LANES // S5_PAIR_W):
        piece = src_rows(out_vreg * (LANES // S5_PAIR_W) + q)
        shift = (S5_PAIR_W * (q - src_lane_blk)) % LANES
        if shift:
            piece = pltpu.roll(piece, shift, 1)
        if acc is None:
            acc = piece
        else:
            lane_blk = lax.broadcasted_iota(jnp.int32, piece.shape, 1) // S5_PAIR_W
            acc = jnp.where(lane_blk == q, piece, acc)
    return acc


def _in_proj_kernel(*refs, split, layer):
    nres = 2 if split else 1
    (mod_ref, w_hbm, scale_ref, ca_ref, sa_ref, cr_ref, sr_ref, perm_ref, o_ref, u_ref, w_ref, stage_ref,
     sem_ref) = refs[nres:]

    @pl.when(pl.program_id(0) == 0)
    def _stage_weights():
        _load_cast(w_hbm.at[layer], w_ref, stage_ref, sem_ref, col_scale=scale_ref[...])

    x = _residual_rows(refs[:nres])(slice(None))
    h = (x * (1.0 + mod_ref[1:2, :]) + mod_ref[0:1, :]).astype(w_ref.dtype)
    lane = lax.broadcasted_iota(jnp.int32, (x.shape[0], LANES), 1)
    first_att = lane % (HEAD_DIM // 2) < HEAD_DIM // 4
    first_ret = lane % HEAD_DIM < HEAD_DIM // 2

    def proj(c0, c1):
        return _mm(h, w_ref[:, c0:c1])

    def rope_store(p, c0, width, cos, sin, first, half):
        for b in range(width // LANES):
            blk = p[:, LANES * b:LANES * (b + 1)]
            rot = jnp.where(first, pltpu.roll(blk, LANES - half, 1), pltpu.roll(blk, half, 1))
            o_ref[:, c0 + LANES * b:c0 + LANES * (b + 1)] = (blk * cos + rot * sin).astype(o_ref.dtype)

    def plain_store(p, c0, width):
        o_ref[:, c0:c0 + width] = p.astype(o_ref.dtype)

    u = proj(COL_S5, IN_WIDTH).astype(w_ref.dtype)
    sub = perm_ref.shape[0]
    nchunk = sub // S5_T
    for part in range(x.shape[0] // sub):
        g = _mm(perm_ref[...], u[sub * part:sub * (part + 1)])
        for a in range(S5_PAIRS):
            vreg_col, lane_blk = divmod(a * S5_PAIR_W, LANES)
            lane_blk //= S5_PAIR_W
            for v in range(S5_T * S5_PAIR_W // LANES):
                slab = _lane_block_shuffle(
                    lambda j: g[nchunk * j:nchunk * (j + 1), vreg_col * LANES:(vreg_col + 1) * LANES], lane_blk, v)
                u_ref[a, nchunk * part:nchunk * (part + 1), LANES * v:LANES * (v + 1)] = slab.astype(u_ref.dtype)

    ca, sa, cr, sr = ca_ref[...], sa_ref[...], cr_ref[...], sr_ref[...]
    att_rope = (ca, sa, first_att, HEAD_DIM // 4)
    ret_rope = (cr, sr, first_ret, HEAD_DIM // 2)
    groups = [(COL_AQ, ATT_WIDTH, att_rope), (COL_AK, KV_WIDTH, att_rope), (COL_RQ, RET_WIDTH, ret_rope),
              (COL_RK, RET_WIDTH, ret_rope), (COL_AV, KV_WIDTH, None), (COL_RV, COL_S5 - COL_RV, None)]
    pending = None
    for group in groups + [None]:
        nxt = None if group is None else (proj(group[0], group[0] + group[1]),) + group
        if pending is not None:
            p, pc0, pwidth, prope = pending
            if prope is None:
                plain_store(p, pc0, pwidth)
            else:
                rope_store(p, pc0, pwidth, *prope)
        pending = nxt


def _chunk_perm(tile_rows, dtype):
    nchunk = tile_rows // S5_T
    r = jnp.arange(tile_rows)
    src = S5_T * (r % nchunk) + r // nchunk
    return (src[:, None] == jnp.arange(tile_rows)[None, :]).astype(dtype)


def _in_proj(residual, mods, layer, w_in, col_scale, tabs, perm):
    rows = tabs[0].shape[0]
    tm = ROW_TILE
    tab_spec = pl.BlockSpec((tm, LANES), lambda i: (i, 0))
    nch = rows // S5_T
    return pl.pallas_call(
        functools.partial(_in_proj_kernel, split=len(residual) > 1, layer=layer),
        grid=(rows // tm,),
        in_specs=_residual_specs(residual) + [
            pl.BlockSpec((None, None, N_ADA, D_MODEL), lambda i: (layer, jnp.where(i == 0, 1, 0), 0, 0)),
            pl.BlockSpec(memory_space=pl.ANY),
            pl.BlockSpec((1, IN_WIDTH), lambda i: (0, 0)),
            tab_spec, tab_spec, tab_spec, tab_spec,
            pl.BlockSpec(perm.shape, lambda i: (0, 0)),
        ],
        out_specs=[pl.BlockSpec((tm, COL_S5), lambda i: (i, 0)),
                   pl.BlockSpec((S5_PAIRS, tm // S5_T, S5_T * S5_PAIR_W), lambda i: (0, i, 0))],
        out_shape=[jax.ShapeDtypeStruct((rows, COL_S5), MXU_DTYPE),
                   jax.ShapeDtypeStruct((S5_PAIRS, nch, S5_T * S5_PAIR_W), MXU_DTYPE)],
        scratch_shapes=[pltpu.VMEM((D_MODEL, IN_WIDTH), MXU_DTYPE),
                        pltpu.VMEM((2, STAGE_BYTES // (4 * IN_WIDTH), IN_WIDTH), F32),
                        pltpu.SemaphoreType.DMA((2,))],
        compiler_params=_params(("arbitrary",)),
        name="in_proj",
    )(*residual, mods, w_in, col_scale, *tabs, perm)


def _rope_tables(seq):
    half_a = HEAD_DIM // 4
    half_r = HEAD_DIM // 2
    nrow = seq // GRID_W
    inv_a = ROPE_BASE ** (-jnp.arange(half_a, dtype=F32) / half_a)
    inv_r = ROPE_BASE ** (-jnp.arange(half_r, dtype=F32) / half_r)
    ang_r = jnp.arange(nrow, dtype=F32)[:, None] * inv_a[None, :]
    ang_c = jnp.arange(GRID_W, dtype=F32)[:, None] * inv_a[None, :]
    ang_t = jnp.arange(seq, dtype=F32)[:, None] * inv_r[None, :]
    hp = lax.Precision.HIGHEST
    lane = jnp.arange(LANES)
    within = lane % HEAD_DIM
    pick_a = (within % half_a)[None, :] == jnp.arange(half_a)[:, None]
    exp_row = (pick_a & (within < 2 * half_a)[None, :]).astype(F32)
    exp_col = (pick_a & (within >= 2 * half_a)[None, :]).astype(F32)
    exp_t = ((lane % half_r)[None, :] == jnp.arange(half_r)[:, None]).astype(F32)
    sign_a = jnp.where(within % (2 * half_a) < half_a, -1.0, 1.0).astype(F32)
    sign_r = jnp.where(within < half_r, -1.0, 1.0).astype(F32)

    def att_table(fn):
        by_row = jnp.dot(fn(ang_r), exp_row, precision=hp)
        by_col = jnp.dot(fn(ang_c), exp_col, precision=hp)
        return (by_row[:, None, :] + by_col[None, :, :]).reshape(seq, LANES)

    cos_a = att_table(jnp.cos)
    sin_a = att_table(jnp.sin) * sign_a
    cos_r = jnp.dot(jnp.cos(ang_t), exp_t, precision=hp)
    sin_r = jnp.dot(jnp.sin(ang_t), exp_t, precision=hp) * sign_r
    pad = lambda tab, ident: jnp.pad(tab, ((CTX_PAD, 0), (0, 0)), constant_values=ident)
    return pad(cos_a, 1.0), pad(sin_a, 0.0), pad(cos_r, 1.0), pad(sin_r, 0.0)


def _swap_halves(x):
    if x.dtype.itemsize == 4:
        return pltpu.roll(x, 64, 1)
    packed = pltpu.bitcast(x, jnp.uint32)
    return pltpu.bitcast(pltpu.roll(packed, 64, 1), x.dtype)


def _dup_heads(x):
    sw = _swap_halves(x)
    lo = lax.broadcasted_iota(jnp.int32, x.shape, 1) < HEAD_DIM
    return jnp.where(lo, x, sw), jnp.where(lo, sw, x)


def _attn_kernel(sink_ref, q_ref, km_ref, kp_ref, kn_ref, vm_ref, vp_ref, vn_ref, kc_ref, vc_ref, mask_ref,
                 o_ref, k2_ref, v2_ref, kc2_ref, vc2_ref):
    i = pl.program_id(0)
    last_blk = pl.num_programs(0) * (ROW_TILE // ATT_BLOCK) - 1
    blk = ATT_BLOCK
    def spread(src, ones_upper):
        x = src[...]
        a, b = _dup_heads(x)
        if ones_upper:
            upper = lax.broadcasted_iota(jnp.int32, x.shape, 1) >= HEAD_DIM
            a = jnp.where(upper, jnp.ones_like(a), a)
            b = jnp.where(upper, jnp.ones_like(b), b)
        return a, b

    for dst, parts, is_v in ((k2_ref, (kp_ref, km_ref, kn_ref), False), (v2_ref, (vp_ref, vm_ref, vn_ref), True)):
        row = 0
        for part in parts:
            a, b = spread(part, is_v)
            n = part.shape[0]
            dst[0, row:row + n, :] = a
            dst[1, row:row + n, :] = b
            row += n
    for dst, src, is_v in ((kc2_ref, kc_ref, False), (vc2_ref, vc_ref, True)):
        a, b = spread(src, is_v)
        dst[0] = a
        dst[1] = b

    lo = lax.broadcasted_iota(jnp.int32, (blk, LANES), 1) < HEAD_DIM
    group = ATT_HEADS // ATT_KV_HEADS

    nloc = 3 * blk

    def scores(j, kv):
        r0 = j * blk
        qt = q_ref[r0:r0 + blk, group * HEAD_DIM * kv:group * HEAD_DIM * (kv + 1)]
        parts = []
        for g in range(group):
            qc = qt[:, LANES * (g // 2):LANES * (g // 2 + 1)]
            keep = lo if g % 2 == 0 else jnp.logical_not(lo)
            parts.append(jnp.where(keep, qc, jnp.zeros_like(qc)))
        qs = jnp.concatenate(parts, axis=0)
        return _mm_nt(qs, k2_ref[kv, r0:r0 + nloc, :]), _mm_nt(qs, kc2_ref[kv])

    def finish(j, kv, s_loc, s_ctx):
        r0 = j * blk
        gblk = i * (ROW_TILE // blk) + j
        sel = jnp.where(i == 0, 3, jnp.where(gblk == CTX_PAD // blk, 0, jnp.where(gblk == last_blk, 2, 1)))
        bias = mask_ref[sel]
        probs, sink_w = [], []
        for g in range(group):
            s = jnp.concatenate([s_loc[blk * g:blk * (g + 1)] + bias, s_ctx[blk * g:blk * (g + 1)]], axis=1)
            sk = sink_ref[group * kv + g]
            m = jnp.maximum(jnp.max(s, axis=-1, keepdims=True), sk)
            probs.append(jnp.exp2(s - m).astype(o_ref.dtype))
            sink_w.append(jnp.exp2(sk - m))
        p = jnp.concatenate(probs, axis=0)
        o = _mm(p[:, :nloc], v2_ref[kv, r0:r0 + nloc, :]) + _mm(p[:, nloc:], vc2_ref[kv])
        for half in range(group // 2):
            even, odd = 2 * half, 2 * half + 1
            oe = o[blk * even:blk * (even + 1)]
            oo = o[blk * odd:blk * (odd + 1)]
            y_even = oe * (1.0 / (pltpu.roll(oe, HEAD_DIM, 1) + sink_w[even]))
            y_odd = pltpu.roll(oo, HEAD_DIM, 1) * (1.0 / (oo + sink_w[odd]))
            c0 = group * HEAD_DIM * kv + LANES * half
            o_ref[r0:r0 + blk, c0:c0 + LANES] = jnp.where(lo, y_even, y_odd).astype(o_ref.dtype)

    items = [(j, kv) for j in range(ROW_TILE // blk) for kv in range(ATT_KV_HEADS)]
    pending = {}
    for t in range(len(items) + ATT_LOOKAHEAD):
        if t < len(items):
            pending[t] = scores(*items[t])
        if t >= ATT_LOOKAHEAD:
            finish(*items[t - ATT_LOOKAHEAD], *pending.pop(t - ATT_LOOKAHEAD))


def _attention_masks():
    qi = jnp.arange(ATT_BLOCK)[:, None]
    kj = jnp.arange(3 * ATT_BLOCK)[None, :]
    band = jnp.abs(kj - ATT_BLOCK - qi) <= ATT_BLOCK
    first = band & (kj >= ATT_BLOCK)
    last = band & (kj < 2 * ATT_BLOCK)
    none = jnp.zeros_like(band)
    masks = jnp.stack([first, band, last, none])
    return jnp.where(masks, 0.0, NEG_BIG).astype(F32)


def _attention(proj, sink, masks):
    rows = proj.shape[0]
    tm, blk = ROW_TILE, ATT_BLOCK
    per = tm // blk
    nblk = rows // blk
    ck, cv = COL_AK // KV_WIDTH, COL_AV // KV_WIDTH
    dt = proj.dtype
    return pl.pallas_call(
        _attn_kernel,
        grid=(rows // tm,),
        in_specs=[
            pl.BlockSpec(memory_space=pltpu.SMEM),
            pl.BlockSpec((tm, ATT_WIDTH), lambda i: (i, 0)),
            pl.BlockSpec((tm, KV_WIDTH), lambda i: (i, ck)),
            pl.BlockSpec((blk, KV_WIDTH), lambda i: (jnp.maximum(i * per - 1, 0), ck)),
            pl.BlockSpec((blk, KV_WIDTH), lambda i: (jnp.minimum((i + 1) * per, nblk - 1), ck)),
            pl.BlockSpec((tm, KV_WIDTH), lambda i: (i, cv)),
            pl.BlockSpec((blk, KV_WIDTH), lambda i: (jnp.maximum(i * per - 1, 0), cv)),
            pl.BlockSpec((blk, KV_WIDTH), lambda i: (jnp.minimum((i + 1) * per, nblk - 1), cv)),
            pl.BlockSpec((CTX_LEN, KV_WIDTH), lambda i: (0, ck)),
            pl.BlockSpec((CTX_LEN, KV_WIDTH), lambda i: (0, cv)),
            pl.BlockSpec((4, blk, 3 * blk), lambda i: (0, 0, 0)),
        ],
        out_specs=pl.BlockSpec((tm, ATT_WIDTH), lambda i: (i, 0)),
        out_shape=jax.ShapeDtypeStruct((rows, ATT_WIDTH), dt),
        scratch_shapes=[
            pltpu.VMEM((2, tm + 2 * blk, KV_WIDTH), dt),
            pltpu.VMEM((2, tm + 2 * blk, KV_WIDTH), dt),
            pltpu.VMEM((2, CTX_LEN, KV_WIDTH), dt),
            pltpu.VMEM((2, CTX_LEN, KV_WIDTH), dt),
        ],
        compiler_params=_params(("arbitrary",)),
        name="attention",
    )(sink, proj, proj, proj, proj, proj, proj, proj, proj, proj, masks)


def _ret_kernel(*refs, sup, ntile):
    lg_ref, q_ref = refs[:2]
    k_refs, v_refs = refs[2:2 + sup], refs[2 + sup:2 + 2 * sup]
    g_ref, o_ref, sb_ref, s_ref, dm_ref, tab_ref, gbd_ref = refs[2 + 2 * sup:]
    k_ref, v_ref = k_refs[0], v_refs[0]
    step = pl.program_id(0)
    nsup = (ntile - 1) // sup
    ph = jnp.where(step <= nsup, 0, 1)
    t = jnp.where(step <= nsup, step, step - (nsup + 1))
    c = RET_CHUNK
    w = RET_WIDTH
    per = q_ref.shape[0] // c
    mdt = sb_ref.dtype
    rows = lambda ci: slice(c * ci, c * (ci + 1))

    def lane_vec(direction, shape, axis):
        head = lax.broadcasted_iota(jnp.int32, shape, axis) // HEAD_DIM
        out = jnp.full(shape, lg_ref[direction, RET_HEADS - 1], F32)
        for h in range(RET_HEADS - 2, -1, -1):
            out = jnp.where(head == h, lg_ref[direction, h], out)
        return out

    @pl.when(jnp.logical_and(ph == 0, t == 0))
    def _init_tables():
        diff = (lax.broadcasted_iota(jnp.int32, (c, c), 0) - lax.broadcasted_iota(jnp.int32, (c, c), 1)).astype(F32)
        for h in range(RET_HEADS):
            dm_ref[h] = jnp.exp(jnp.where(diff >= 0, diff * lg_ref[0, h], -diff * lg_ref[1, h]))
        pos = lax.broadcasted_iota(jnp.int32, (c, w), 0).astype(F32)
        lgf = lane_vec(0, (c, w), 1)
        lgb = lane_vec(1, (c, w), 1)
        tab_ref[0] = jnp.exp((c - 1.0 - pos) * lgf)
        tab_ref[1] = jnp.exp((pos + 1.0) * lgf)
        tab_ref[2] = jnp.exp(pos * lgb)
        tab_ref[3] = jnp.exp((c - pos) * lgb)
        same = (lax.broadcasted_iota(jnp.int32, (w, w), 0) // HEAD_DIM
                == lax.broadcasted_iota(jnp.int32, (w, w), 1) // HEAD_DIM)
        bd = jnp.where(same, 1.0, 0.0)
        gbd_ref[0] = bd * jnp.exp(c * lane_vec(0, (w, w), 0))
        gbd_ref[1] = bd * jnp.exp(c * lane_vec(1, (w, w), 0))
        gbd_ref[2] = bd

    @pl.when(t == 0)
    def _reset_state():
        s_ref[...] = jnp.zeros_like(s_ref)

    def state_update(direction, key_tab, ci, kr=k_ref, vr=v_ref):
        kw = (kr[rows(ci), :].astype(F32) * tab_ref[key_tab]).astype(mdt)
        u = _mm_tn(kw, vr[rows(ci), :])
        s_ref[...] = gbd_ref[direction] * s_ref[...] + gbd_ref[2] * u

    @pl.when(jnp.logical_and(ph == 0, t == 0))
    def _backward_context():
        sb_ref[0] = s_ref[...].astype(mdt)
        state_update(1, 2, 0)

    @pl.when(jnp.logical_and(ph == 0, t > 0))
    def _backward_latent():
        base = 1 + per * sup * (nsup - t)
        for qi in range(sup - 1, -1, -1):
            for ci in range(per - 1, -1, -1):
                sb_ref[base + per * qi + ci] = s_ref[...].astype(mdt)
                state_update(1, 2, ci, k_refs[qi], v_refs[qi])

    head = lax.broadcasted_iota(jnp.int32, (c, w), 1) // HEAD_DIM

    def scores(ci):
        q = q_ref[rows(ci), :]
        qs = jnp.concatenate([jnp.where(head == h, q, jnp.zeros_like(q)) for h in range(RET_HEADS)], axis=0)
        return _mm_nt(qs, k_ref[rows(ci), :])

    def intra(ci, sc):
        scd = jnp.concatenate([sc[c * h:c * (h + 1)] * dm_ref[h] for h in range(RET_HEADS)], axis=0).astype(mdt)
        oi = _mm(scd, v_ref[rows(ci), :])
        o = jnp.where(head == 0, oi[0:c], 0.0)
        for h in range(1, RET_HEADS):
            o = o + jnp.where(head == h, oi[c * h:c * (h + 1)], 0.0)
        return o

    def cross(ci, idx):
        qf = q_ref[rows(ci), :].astype(F32)
        return (_mm((qf * tab_ref[1]).astype(mdt), s_ref[...].astype(mdt))
                + _mm((qf * tab_ref[3]).astype(mdt), sb_ref[idx]))

    def finish(ci, o):
        avg = (gbd_ref[2] * (1.0 / HEAD_DIM)).astype(mdt)
        o_hi = o.astype(mdt)
        d = o - (_mm(o_hi, avg) + _mm((o - o_hi.astype(F32)).astype(mdt), avg))
        var = _mm((d * d).astype(mdt), avg)
        gate = g_ref[rows(ci), :].astype(F32)
        y = d * lax.rsqrt(var + GN_EPS) * (gate * jax.nn.sigmoid(gate))
        o_ref[rows(ci), :] = y.astype(o_ref.dtype)

    @pl.when(jnp.logical_and(ph == 1, t == 0))
    def _forward_context():
        o = intra(0, scores(0)) + cross(0, 0)
        state_update(0, 0, 0)
        finish(0, o)
        for ci in range(1, per):
            o_ref[rows(ci), :] = jnp.zeros((c, w), o_ref.dtype)

    @pl.when(jnp.logical_and(ph == 1, t > 0))
    def _forward_latent():
        base = 1 + per * (t - 1)
        sc = [scores(ci) for ci in range(per)]
        outs = [intra(ci, sc[ci]) for ci in range(per)]
        for ci in range(per):
            outs[ci] = outs[ci] + cross(ci, base + ci)
            state_update(0, 0, ci)
        for ci in range(per):
            finish(ci, outs[ci])


def _retention(proj, log_gamma):
    rows = proj.shape[0]
    c = RET_CHUNK
    tm = ROW_TILE
    ntile = rows // tm
    nchunk = 1 + (rows - CTX_PAD) // c
    dt = proj.dtype
    cq, ckk, cvv, cg = (COL_RQ // RET_WIDTH, COL_RK // RET_WIDTH, COL_RV // RET_WIDTH, COL_RG // RET_WIDTH)

    nlat = ntile - 1
    sup = next(s for s in (4, 2, 1) if nlat % s == 0)
    nsup = nlat // sup
    nback = 1 + nsup

    def fw_blk(step):
        return jnp.maximum(step - nback, 0)

    def kv_spec(col, qi):
        def index(step):
            back = jnp.where(step == 0, 0, 1 + sup * (nsup - jnp.minimum(step, nsup)) + qi)
            fwd = step - nback if qi == 0 else 1 + qi
            return jnp.where(step < nback, back, fwd), col
        return pl.BlockSpec((tm, RET_WIDTH), index)

    return pl.pallas_call(
        functools.partial(_ret_kernel, sup=sup, ntile=ntile),
        grid=(nback + ntile,),
        in_specs=[pl.BlockSpec(memory_space=pltpu.SMEM),
                  pl.BlockSpec((tm, RET_WIDTH), lambda step: (fw_blk(step), cq))]
                 + [kv_spec(ckk, qi) for qi in range(sup)] + [kv_spec(cvv, qi) for qi in range(sup)]
                 + [pl.BlockSpec((tm, RET_WIDTH), lambda step: (fw_blk(step), cg))],
        out_specs=pl.BlockSpec((tm, RET_WIDTH), lambda step: (fw_blk(step), 0)),
        out_shape=jax.ShapeDtypeStruct((rows, RET_WIDTH), dt),
        scratch_shapes=[
            pltpu.VMEM((nchunk, RET_WIDTH, RET_WIDTH), dt),
            pltpu.VMEM((RET_WIDTH, RET_WIDTH), F32),
            pltpu.VMEM((RET_HEADS, c, c), F32),
            pltpu.VMEM((4, c, RET_WIDTH), F32),
            pltpu.VMEM((3, RET_WIDTH, RET_WIDTH), F32),
        ],
        compiler_params=_params(("arbitrary",)),
        name="retention",
    )(log_gamma, proj, *([proj] * (2 * sup)), proj)


def _s5_weights(lam_re, lam_im, b_re, b_im, c_re, c_im, log_dt, d_skip):
    tt, g, n, p, a = S5_T, S5_GROUPS, S5_STATE, S5_CH, S5_PAIRS
    lam = lax.complex(lam_re.astype(F32), lam_im.astype(F32))
    dtv = jnp.exp(log_dt.astype(F32))[..., None]
    lam_bar = jnp.exp(lam * dtv)
    bbar = ((lam_bar - 1.0) / lam)[..., None] * lax.complex(b_re.astype(F32), b_im.astype(F32))
    cmat = lax.complex(c_re.astype(F32), c_im.astype(F32))
    pw = [jnp.ones_like(lam_bar)]
    for _ in range(tt):
        pw.append(pw[-1] * lam_bar)
    pw = jnp.stack(pw, axis=1)
    eye2 = jnp.eye(2, dtype=F32)
    ri = lambda z, axis: jnp.stack([jnp.real(z), jnp.imag(z)], axis=axis)

    pw_l = pw.reshape(2, tt + 1, a, 2 * n)
    bbt = jnp.einsum('dahpn,gh->dagphn', jnp.swapaxes(bbar, -1, -2).reshape(2, a, 2, p, n), eye2)
    bbt = bbt.reshape(2, a, 2 * p, 2 * n)
    cct = jnp.einsum('dahpn,gh->dagphn', cmat.reshape(2, a, 2, p, n), eye2).reshape(2, a, 2 * p, 2 * n)
    pw_k = ri(pw_l, 1).transpose(3, 0, 1, 2, 4)
    b_k = ri(bbt, 2).transpose(1, 0, 2, 3, 4)
    c_k = ri(cct, 2).transpose(1, 0, 2, 3, 4)
    decay = [pw_l[:, tt]]
    for _ in range(SCAN_ROWS - 1):
        decay.append(decay[-1] * decay[0])
    decay = jnp.stack(decay, axis=0)
    rows8 = lambda z: jnp.broadcast_to(z[None], (SCAN_ROWS,) + z.shape)
    carry_w = jnp.stack([decay[:, 0], decay[::-1, 1]], axis=1)
    scan_tab = jnp.stack([rows8(decay[0]), rows8(decay[1]), rows8(decay[3]), carry_w], axis=0)
    scan_tab = ri(scan_tab, 0).transpose(4, 3, 1, 0, 2, 5)
    skip = jnp.tile(d_skip.astype(F32).reshape(a, 1, 2 * p), (1, tt, 1)).reshape(a, 1, tt * 2 * p)
    return pw_k, b_k, c_k, scan_tab, skip


def _s5_drive_kernel(u_ref, pw_ref, b_ref, o_ref, w_ref):
    rows, half = S5_PAIR_W, LANES
    for d in range(2):
        br, bi = b_ref[d, 0], b_ref[d, 1]
        for j in range(S5_T):
            e = S5_T - 1 - j if d == 0 else j
            pr, pi = pw_ref[d, 0, e:e + 1, :], pw_ref[d, 1, e:e + 1, :]
            w_ref[rows * j:rows * (j + 1), 2 * half * d:2 * half * d + half] = (pr * br - pi * bi).astype(w_ref.dtype)
            w_ref[rows * j:rows * (j + 1), 2 * half * d + half:2 * half * (d + 1)] = (
                pr * bi + pi * br).astype(w_ref.dtype)
    o_ref[0] = _mm(u_ref[0], w_ref[...])


def _lane_window(x, start, width):
    cols = []
    for v in range(width // LANES):
        k0, off = divmod(start + LANES * v, LANES)
        lo = x[:, LANES * k0:LANES * (k0 + 1)]
        if off:
            hi = x[:, LANES * (k0 + 1):LANES * (k0 + 2)]
            lane = lax.broadcasted_iota(jnp.int32, lo.shape, 1)
            lo = jnp.where(lane < LANES - off, pltpu.roll(lo, LANES - off, 1), pltpu.roll(hi, LANES - off, 1))
        cols.append(lo)
    return jnp.concatenate(cols, axis=1)


def _s5_read_kernel(u_ref, sf_ref, sb_ref, pw_ref, b_ref, c_ref, skip_ref, o_ref, wt_ref, wi_ref, lag_ref):
    u = u_ref[0]
    mdt = u.dtype
    rows, half = S5_PAIR_W, LANES
    for d in range(2):
        cr, ci = c_ref[d, 0], c_ref[d, 1]
        for i in range(S5_T):
            e = i + 1 if d == 0 else S5_T - i
            pr, pi = pw_ref[d, 0, e:e + 1, :], pw_ref[d, 1, e:e + 1, :]
            wt_ref[d, rows * i:rows * (i + 1), 0:half] = (pr * cr - pi * ci).astype(mdt)
            wt_ref[d, rows * i:rows * (i + 1), half:2 * half] = (-(pr * ci + pi * cr)).astype(mdt)
    nlag = 2 * S5_T - 1
    ldt = lag_ref.dtype
    zero = jnp.zeros((rows, half), ldt)
    for l in range(nlag + 1):
        lag = l - (S5_T - 1)
        for d, active in ((1, lag <= 0), (0, 0 <= lag < S5_T)):
            col = 2 * half * (1 - d)
            if active:
                cr, ci = c_ref[d, 0], c_ref[d, 1]
                pr, pi = pw_ref[d, 0, abs(lag):abs(lag) + 1, :], pw_ref[d, 1, abs(lag):abs(lag) + 1, :]
                lag_ref[rows * l:rows * (l + 1), col:col + half] = (pr * cr - pi * ci).astype(ldt)
                lag_ref[rows * l:rows * (l + 1), col + half:col + 2 * half] = (pr * ci + pi * cr).astype(ldt)
            else:
                lag_ref[rows * l:rows * (l + 1), col:col + half] = zero
                lag_ref[rows * l:rows * (l + 1), col + half:col + 2 * half] = zero
    lhs = jnp.concatenate([b_ref[1, 0], -b_ref[1, 1], b_ref[0, 0], -b_ref[0, 1]], axis=1).astype(ldt)
    kall = _mm_nt(lhs, lag_ref[...])
    for j in range(S5_T):
        wi_ref[rows * j:rows * (j + 1), :] = _lane_window(kall, rows * (S5_T - 1 - j), S5_T * rows).astype(mdt)
    y = _mm(u, wi_ref[...])
    y = y + _mm_nt(sf_ref[0].astype(mdt), wt_ref[0])
    y = y + _mm_nt(sb_ref[0].astype(mdt), wt_ref[1])
    o_ref[0] = y + u.astype(F32) * skip_ref[...]


def _s5_fused_kernel(u_ref, pw_ref, b_ref, c_ref, tab_ref, skip_ref, o_ref, e_sc, sf_sc, sb_sc, w_sc, wt_sc, wi_sc,
                     lag_sc):
    step = pl.program_id(0)
    npair = S5_PAIRS
    nch = u_ref.shape[1]
    hw, sub = LANES, SCAN_ROWS
    span = 2 * sub
    ctx_rows = CTX_LEN // S5_T

    @pl.when(step < npair)
    def _drive():
        _s5_drive_kernel(u_ref, pw_ref, b_ref, e_sc.at[pl.ds(step, 1)], w_sc)

    @pl.when(step == npair)
    def _scan():
        row = lax.broadcasted_iota(jnp.int32, (sub, hw), 0)

        def shift(x, k, reverse):
            if reverse:
                return jnp.where(row < sub - k, pltpu.roll(x, sub - k, 0), 0.0)
            return jnp.where(row >= k, pltpu.roll(x, k, 0), 0.0)

        def scan_group(a, d, r0, cr, ci):
            reverse = d == 1
            xr = e_sc[a, pl.ds(r0, sub), 2 * hw * d:2 * hw * d + hw]
            xi = e_sc[a, pl.ds(r0, sub), 2 * hw * d + hw:2 * hw * (d + 1)]
            for k_i, k in enumerate((1, 2, 4)):
                ar, ai = tab_ref[a, d, k_i, 0], tab_ref[a, d, k_i, 1]
                sr, si = shift(xr, k, reverse), shift(xi, k, reverse)
                xr, xi = xr + ar * sr - ai * si, xi + ar * si + ai * sr
            wr, wi = tab_ref[a, d, 3, 0], tab_ref[a, d, 3, 1]
            fr = xr + wr * cr - wi * ci
            fi = xi + wr * ci + wi * cr
            edge = sub - 1 if reverse else 0
            before = (jnp.where(row == edge, cr, shift(fr, 1, reverse)),
                      jnp.where(row == edge, ci, shift(fi, 1, reverse)))
            last = 0 if reverse else sub - 1
            after = (jnp.broadcast_to(fr[last:last + 1], (sub, hw)), jnp.broadcast_to(fi[last:last + 1], (sub, hw)))
            return before, after

        def store(dst, a, r0, lower, upper):
            for part in range(2):
                val = jnp.concatenate([lower[part], upper[part]], axis=0)
                dst[a, pl.ds(r0, span), part * hw:(part + 1) * hw] = val.astype(dst.dtype)

        def do_span(carry, rf0, rb0):
            new = []
            for a in range(npair):
                fr, fi, br, bi = carry[a]
                f_lo, (fr, fi) = scan_group(a, 0, rf0, fr, fi)
                f_hi, (fr, fi) = scan_group(a, 0, rf0 + sub, fr, fi)
                store(sf_sc, a, rf0, f_lo, f_hi)
                b_hi, (br, bi) = scan_group(a, 1, rb0 + sub, br, bi)
                b_lo, (br, bi) = scan_group(a, 1, rb0, br, bi)
                store(sb_sc, a, rb0, b_lo, b_hi)
                new.append((fr, fi, br, bi))
            return tuple(new)

        sf_sc[:, ctx_rows:S5_TILE, :] = jnp.zeros((npair, S5_TILE - ctx_rows, 2 * hw), sf_sc.dtype)
        sb_sc[:, ctx_rows:S5_TILE, :] = jnp.zeros((npair, S5_TILE - ctx_rows, 2 * hw), sb_sc.dtype)
        zero = jnp.zeros((sub, hw), F32)
        carry = tuple((zero, zero, zero, zero) for _ in range(npair))
        for g in range(ctx_rows // span):
            carry = do_span(carry, span * g, ctx_rows - span * (g + 1))

        def body(gi, carry):
            rf0 = pl.multiple_of(S5_TILE + gi * span, span)
            rb0 = pl.multiple_of(nch - span - gi * span, span)
            return do_span(carry, rf0, rb0)

        lax.fori_loop(0, (nch - S5_TILE) // span, body, carry)

    @pl.when(step > npair)
    def _read():
        pair = step - npair - 1
        _s5_read_kernel(u_ref, sf_sc.at[pl.ds(pair, 1)], sb_sc.at[pl.ds(pair, 1)], pw_ref, b_ref, c_ref, skip_ref,
                        o_ref, wt_sc, wi_sc, lag_sc)


def _s5_mixer(u_pairs, weights, layer):
    pw_k, b_k, c_k, scan_tab, skip = weights
    a, nch, wd = u_pairs.shape
    dt = u_pairs.dtype

    def in_pair(step):
        return jnp.where(step < a, step, jnp.where(step == a, a - 1, step - a - 1))

    pair_blk = lambda arr: pl.BlockSpec((None, None) + arr.shape[2:],
                                        lambda s: (layer, in_pair(s)) + (0,) * (arr.ndim - 2))
    return pl.pallas_call(
        _s5_fused_kernel,
        grid=(2 * a + 1,),
        in_specs=[pl.BlockSpec((1, nch, wd), lambda s: (in_pair(s), 0, 0)),
                  pair_blk(pw_k), pair_blk(b_k), pair_blk(c_k),
                  pl.BlockSpec((None,) + scan_tab.shape[1:], lambda s: (layer,) + (0,) * (scan_tab.ndim - 1)),
                  pair_blk(skip)],
        out_specs=pl.BlockSpec((1, nch, wd), lambda s: (jnp.maximum(s - a - 1, 0), 0, 0)),
        out_shape=jax.ShapeDtypeStruct((a, nch, wd), F32),
        scratch_shapes=[pltpu.VMEM((a, nch, 2 * S5_STATE_W), F32),
                        pltpu.VMEM((a, nch, S5_STATE_W), dt), pltpu.VMEM((a, nch, S5_STATE_W), dt),
                        pltpu.VMEM((wd, 2 * S5_STATE_W), dt),
                        pltpu.VMEM((2, wd, S5_STATE_W), dt), pltpu.VMEM((wd, wd), dt),
                        pltpu.VMEM((2 * wd, 2 * S5_STATE_W), dt)],
        compiler_params=_params(("arbitrary",)),
        name="s5",
    )(u_pairs, pw_k, b_k, c_k, scan_tab, skip)


def _layer_norm(x, g, b):
    mu = jnp.mean(x, axis=-1, keepdims=True)
    d = x - mu
    var = jnp.mean(d * d, axis=-1, keepdims=True)
    return d * lax.rsqrt(var + LN_EPS) * g + b


def _residual_rows(refs):
    if len(refs) == 1:
        return lambda rows: refs[0][rows, :]
    head_ref, body_ref = refs
    is_head = pl.program_id(0) == 0
    return lambda rows: jnp.where(is_head, head_ref[rows, :], body_ref[rows, :])


def _load_cast(src_hbm, dst_ref, stage_ref, sems, col_scale=None):
    chunk = stage_ref.shape[1]
    nchunk = src_hbm.shape[0] // chunk

    def copy(k):
        slot = k % 2
        return pltpu.make_async_copy(src_hbm.at[pl.ds(k * chunk, chunk), :], stage_ref.at[slot], sems.at[slot])

    copy(0).start()
    for k in range(nchunk):
        if k + 1 < nchunk:
            copy(k + 1).start()
        copy(k).wait()
        vals = stage_ref[k % 2]
        if col_scale is not None:
            vals = vals * col_scale
        dst_ref[k * chunk:(k + 1) * chunk, :] = vals.astype(dst_ref.dtype)


def _post_kernel(*refs, split, layer):
    nres = 2 if split else 1
    load_x = _residual_rows(refs[:nres])
    (att_ref, ret_ref, s5_ref, mod_ref, permt_ref, wglu_ref, bglu_ref, wo_hbm, g1_ref, b1_ref, w1_hbm, w2_hbm,
     g2_ref, b2_ref, o_ref, wo_ref, w1_ref, w2_ref, stage_ref, sem_ref) = refs[nres:]
    woa_ref = wo_ref.at[0:ATT_WIDTH]
    wor_ref = wo_ref.at[ATT_WIDTH:ATT_WIDTH + RET_WIDTH]
    wos_ref = wo_ref.at[ATT_WIDTH + RET_WIDTH:ATT_WIDTH + RET_WIDTH + S5_WIDTH]
    mdt = w1_ref.dtype
    sub = o_ref.shape[0] // POST_SPLIT
    csub = sub // S5_T
    nff = D_FF // FF_CHUNK

    blocks = lambda c: slice(FF_CHUNK * c, FF_CHUNK * (c + 1))
    jobs = [(wo_hbm.at[layer], wo_ref)]
    for c in range(nff):
        jobs.append((w1_hbm.at[layer, :, blocks(c)], w1_ref.at[:, blocks(c)]))
        jobs.append((w2_hbm.at[layer, blocks(c), :], w2_ref.at[blocks(c), :]))

    def job_copy(k):
        return pltpu.make_async_copy(jobs[k][0], stage_ref.at[k % 2], sem_ref.at[k % 2])

    def job_finish(k):
        job_copy(k).wait()
        jobs[k][1][...] = stage_ref[k % 2].astype(mdt)
        if k + 2 < len(jobs):
            job_copy(k + 2).start()

    def mix(part):
        rows = slice(sub * part, sub * (part + 1))
        zrows = []
        for i in range(S5_T):
            src_vreg, src_blk = divmod(i * S5_PAIR_W, LANES)
            src_blk //= S5_PAIR_W
            cols = [_lane_block_shuffle(
                lambda a: s5_ref[a, csub * part:csub * (part + 1), src_vreg * LANES:(src_vreg + 1) * LANES],
                src_blk, w) for w in range(S5_WIDTH // LANES)]
            zrows.append(jnp.concatenate(cols, axis=1))
        hs = jax.nn.gelu(jnp.concatenate(zrows, axis=0))
        gate = jax.nn.sigmoid(_mm(hs.astype(mdt), wglu_ref[...]) + bglu_ref[...])
        s5 = _mm(permt_ref[...], (hs * gate).astype(mdt)).astype(mdt)
        return _mm(att_ref[rows, :], woa_ref[...]) + _mm(ret_ref[rows, :], wor_ref[...]) + _mm(s5, wos_ref[...])

    def norm1(part, ox):
        rows = slice(sub * part, sub * (part + 1))
        x1 = _layer_norm(DEEPNORM_ALPHA * load_x(rows) + mod_ref[2:3, :] * ox, g1_ref[...], b1_ref[...])
        return x1, (x1 * (1.0 + mod_ref[4:5, :]) + mod_ref[3:4, :]).astype(mdt)

    def ff(h, c):
        a = _mm(h, w1_ref[:, FF_CHUNK * c:FF_CHUNK * (c + 1)])
        a = jnp.square(jnp.maximum(a, 0.0)).astype(mdt)
        return _mm(a, w2_ref[FF_CHUNK * c:FF_CHUNK * (c + 1), :])

    def norm2(part, x1, acc):
        rows = slice(sub * part, sub * (part + 1))
        o_ref[rows, :] = _layer_norm(DEEPNORM_ALPHA * x1 + mod_ref[5:6, :] * acc, g2_ref[...], b2_ref[...])

    nstage = nff + 3

    def run(staging):
        state = [dict() for _ in range(POST_SPLIT)]
        done = -1
        if staging:
            job_copy(0).start()
            job_copy(1).start()
        for part, stage in POST_PROGRAM:
            st = state[part]
            if staging:
                need = 0 if stage < 2 else 2 * (stage - 2) + 2 if stage < nstage - 1 else done
                while done < need:
                    done += 1
                    job_finish(done)
            if stage == 0:
                st["ox"] = mix(part)
            elif stage == 1:
                st["x1"], st["h"] = norm1(part, st.pop("ox"))
            elif stage < nstage - 1:
                term = ff(st["h"], stage - 2)
                st["acc"] = term if stage == 2 else st["acc"] + term
            else:
                norm2(part, st["x1"], st["acc"])
        assert not staging or done == len(jobs) - 1

    @pl.when(pl.program_id(0) == 0)
    def _first_step():
        run(staging=True)

    @pl.when(pl.program_id(0) > 0)
    def _other_steps():
        run(staging=False)


def _residual_specs(residual):
    tm = ROW_TILE
    if len(residual) == 1:
        return [pl.BlockSpec((tm, D_MODEL), lambda i: (i, 0))]
    head_tiles = CTX_PAD // tm
    return [pl.BlockSpec((tm, D_MODEL), lambda i: (jnp.minimum(i, head_tiles - 1), 0)),
            pl.BlockSpec((tm, D_MODEL), lambda i: (jnp.maximum(i - head_tiles, 0), 0))]


def _post(residual, att, ret, s5_pairs, mods, layer, permt, wglu, bglu, wo, g1, b1, w1, w2, g2, b2, skip_context):
    rows = att.shape[0]
    tm = ROW_TILE
    off = CTX_PAD // tm if skip_context else 0
    assert not (skip_context and len(residual) > 1)
    assert FF_CHUNK == D_MODEL == wo.shape[1]
    row_blk = lambda width: pl.BlockSpec((tm, width), lambda i: (i + off, 0))
    full = lambda arr: pl.BlockSpec(arr.shape, lambda i: (0,) * arr.ndim)
    hbm = lambda arr: pl.BlockSpec(memory_space=pl.ANY)
    vec = lambda v: v.reshape(1, -1).astype(F32)
    small = [(permt, full), (wglu, full), (vec(bglu), full), (wo, hbm), (vec(g1), full), (vec(b1), full),
             (w1, hbm), (w2, hbm), (vec(g2), full), (vec(b2), full)]
    res_specs = [row_blk(D_MODEL)] if skip_context else _residual_specs(residual)
    return pl.pallas_call(
        functools.partial(_post_kernel, split=len(residual) > 1, layer=layer),
        grid=(rows // tm - off,),
        in_specs=res_specs + [row_blk(ATT_WIDTH), row_blk(RET_WIDTH),
                              pl.BlockSpec((S5_PAIRS, tm // S5_T, S5_T * S5_PAIR_W), lambda i: (0, i + off, 0)),
                              pl.BlockSpec((None, None, N_ADA, D_MODEL),
                                           lambda i: (layer, jnp.where(i + off == 0, 1, 0), 0, 0))]
                 + [spec(arr) for arr, spec in small],
        out_specs=pl.BlockSpec((tm, D_MODEL), lambda i: (i, 0)),
        out_shape=jax.ShapeDtypeStruct((rows - off * tm, D_MODEL), F32),
        scratch_shapes=[pltpu.VMEM((ATT_WIDTH + RET_WIDTH + S5_WIDTH, D_MODEL), MXU_DTYPE),
                        pltpu.VMEM((D_MODEL, D_FF), MXU_DTYPE), pltpu.VMEM((D_FF, D_MODEL), MXU_DTYPE),
                        pltpu.VMEM((2, D_MODEL, FF_CHUNK), F32),
                        pltpu.SemaphoreType.DMA((2,))],
        compiler_params=_params(("arbitrary",)),
        name="post",
    )(*residual, att, ret, s5_pairs, mods, *[arr for arr, _ in small])


def kernel(x, c, ctx, c_ctx, w_ada, b_ada, w_in, att_sink, ret_decay_logit, s5_lambda_re, s5_lambda_im, s5_b_re,
           s5_b_im, s5_c_re, s5_c_im, s5_log_dt, s5_d, w_glu, b_glu, w_out, ln1_g, ln1_b, w_ff1, w_ff2, ln2_g,
           ln2_b):
    assert x.shape[0] == 1 and x.shape[2] == D_MODEL and ctx.shape[1] == CTX_LEN
    seq = x.shape[1]
    assert seq % ROW_TILE == 0
    residual = (jnp.pad(ctx[0], ((0, CTX_PAD - CTX_LEN), (0, 0))), x[0])
    cond = jnp.zeros((8, D_MODEL), F32).at[0].set(c[0]).at[1].set(c_ctx)
    mods = _modulation(cond, w_ada, b_ada).reshape(DEPTH, 8, N_ADA, D_MODEL)
    tabs = _rope_tables(seq)
    masks = _attention_masks()
    perm = _chunk_perm(ROW_TILE // POST_SPLIT, MXU_DTYPE)
    permt = perm.T
    col_scale = jnp.ones((IN_WIDTH,), F32).at[COL_AQ:COL_AK].set(HEAD_DIM ** -0.5 * LOG2E)
    col_scale = col_scale.at[COL_RQ:COL_RK].set(HEAD_DIM ** -0.5)
    s5w = jax.vmap(_s5_weights)(s5_lambda_re, s5_lambda_im, s5_b_re, s5_b_im, s5_c_re, s5_c_im, s5_log_dt, s5_d)
    log_gamma = jax.nn.log_sigmoid(ret_decay_logit.astype(F32))
    for l in range(DEPTH):
        proj, u_pairs = _in_proj(residual, mods, l, w_in, col_scale.reshape(1, IN_WIDTH), tabs, perm)
        att = _attention(proj, att_sink[l].astype(F32) * LOG2E, masks)
        ret = _retention(proj, log_gamma[l])
        s5 = _s5_mixer(u_pairs, s5w, l)
        stream = _post(residual, att, ret, s5, mods, l, permt, w_glu[l].astype(MXU_DTYPE), b_glu[l],
                       w_out, ln1_g[l], ln1_b[l], w_ff1, w_ff2, ln2_g[l], ln2_b[l],
                       skip_context=(l == DEPTH - 1))
        residual = (stream,)
    return stream[None]
```

```python
import functools
import math

import jax
import jax.numpy as jnp
from jax import lax
from jax.experimental import pallas as pl
from jax.experimental.pallas import tpu as pltpu

F32 = jnp.float32
MXU_DTYPE = jnp.bfloat16

D_MODEL = 1024
DEPTH = 4
GRID_W = 64
CTX_LEN = 256
CTX_PAD = 512
HEAD_DIM = 64
ATT_HEADS = 8
ATT_KV_HEADS = 2
ATT_BLOCK = 128
ATT_LOOKAHEAD = 2
ROPE_BASE = 10000.0
RET_HEADS = 4
RET_CHUNK = 256
S5_CH = 16
S5_GROUPS = 16
S5_STATE = 64
S5_T = 16
S5_PAIRS = S5_GROUPS // 2
S5_PAIR_W = 2 * S5_CH
S5_TILE = CTX_PAD // S5_T
SCAN_ROWS = 8
S5_STATE_W = 2 * 2 * S5_STATE
ATT_WIDTH = ATT_HEADS * HEAD_DIM
KV_WIDTH = ATT_KV_HEADS * HEAD_DIM
RET_WIDTH = RET_HEADS * HEAD_DIM
S5_WIDTH = S5_GROUPS * S5_CH
IN_WIDTH = ATT_WIDTH + 2 * KV_WIDTH + 4 * RET_WIDTH + S5_WIDTH
D_FF = 4 * D_MODEL
FF_CHUNK = 1024
N_ADA = 6
LN_EPS = 1e-5
GN_EPS = 1e-5
DEEPNORM_ALPHA = (2 * DEPTH) ** 0.25
ROW_TILE = 512
POST_SPLIT = 2
_NFF = D_FF // FF_CHUNK
POST_PROGRAM = (((0, 0), (0, 1), (1, 0), (0, 2), (1, 1)) + tuple((0, 2 + c) for c in range(1, _NFF))
                + ((1, 2), (0, 2 + _NFF)) + tuple((1, 2 + c) for c in range(1, _NFF)) + ((1, 2 + _NFF),))
NEG_BIG = -1e30
LOG2E = math.log2(math.e)
LANES = 128
VMEM_LIMIT = 56 * 1024 * 1024
STAGE_BYTES = 2 * 1024 * 1024

COL_AQ, COL_AK, COL_AV = 0, ATT_WIDTH, ATT_WIDTH + KV_WIDTH
COL_RQ = ATT_WIDTH + 2 * KV_WIDTH
COL_RK, COL_RV, COL_RG = COL_RQ + RET_WIDTH, COL_RQ + 2 * RET_WIDTH, COL_RQ + 3 * RET_WIDTH
COL_S5 = COL_RQ + 4 * RET_WIDTH


def _mm(a, b):
    return jnp.dot(a, b, preferred_element_type=F32)


def _mm_nt(a, b):
    return lax.dot_general(a, b, (((1,), (1,)), ((), ())), preferred_element_type=F32)


def _mm_tn(a, b):
    return lax.dot_general(a, b, (((0,), (0,)), ((), ())), preferred_element_type=F32)


def _params(sem):
    return pltpu.CompilerParams(dimension_semantics=sem, vmem_limit_bytes=VMEM_LIMIT)


def _mod_kernel(cond_ref, w_ref, b_ref, o_ref):
    c = cond_ref[...]
    s = c * jax.nn.sigmoid(c)
    w = w_ref[0]
    split = lambda v: (v.astype(jnp.bfloat16), (v - v.astype(jnp.bfloat16).astype(F32)).astype(jnp.bfloat16))
    s_hi, s_lo = split(s)
    w_hi, w_lo = split(w)
    o_ref[0] = _mm(s_hi, w_hi) + (_mm(s_lo, w_hi) + _mm(s_hi, w_lo)) + b_ref[0]


def _modulation(cond, w_ada, b_ada):
    tn = 1536
    n = N_ADA * D_MODEL
    return pl.pallas_call(
        _mod_kernel,
        grid=(DEPTH, n // tn),
        in_specs=[
            pl.BlockSpec((8, D_MODEL), lambda l, j: (0, 0)),
            pl.BlockSpec((1, D_MODEL, tn), lambda l, j: (l, 0, j)),
            pl.BlockSpec((1, 1, tn), lambda l, j: (l, 0, j)),
        ],
        out_specs=pl.BlockSpec((1, 8, tn), lambda l, j: (l, 0, j)),
        out_shape=jax.ShapeDtypeStruct((DEPTH, 8, n), F32),
        compiler_params=_params(("arbitrary", "arbitrary")),
        name="modulation",
    )(cond, w_ada, b_ada.reshape(DEPTH, 1, n))


def _lane_block_shuffle(src_rows, src_lane_blk, out_vreg):
    acc = None
    for q in range(LANES // S5_PAIR_W):
        piece = src_rows(out_vreg * (LANES // S5_PAIR_W) + q)
        shift = (S5_PAIR_W * (q - src_lane_blk)) % LANES
        if shift:
            piece = pltpu.roll(piece, shift, 1)
        if acc is None:
            acc = piece
        else:
            lane_blk = lax.broadcasted_iota(jnp.int32, piece.shape, 1) // S5_PAIR_W
            acc = jnp.where(lane_blk == q, piece, acc)
    return acc


def _in_proj_kernel(*refs, split, layer):
    nres = 2 if split else 1
    (mod_ref, w_hbm, scale_ref, ca_ref, sa_ref, cr_ref, sr_ref, perm_ref, o_ref, u_ref, w_ref, stage_ref,
     sem_ref) = refs[nres:]

    @pl.when(pl.program_id(0) == 0)
    def _stage_weights():
        _load_cast(w_hbm.at[layer], w_ref, stage_ref, sem_ref, col_scale=scale_ref[...])

    x = _residual_rows(refs[:nres])(slice(None))
    h = (x * (1.0 + mod_ref[1:2, :]) + mod_ref[0:1, :]).astype(w_ref.dtype)
    lane = lax.broadcasted_iota(jnp.int32, (x.shape[0], LANES), 1)
    first_att = lane % (HEAD_DIM // 2) < HEAD_DIM // 4
    first_ret = lane % HEAD_DIM < HEAD_DIM // 2

    def proj(c0, c1):
        return _mm(h, w_ref[:, c0:c1])

    def rope_store(p, c0, width, cos, sin, first, half):
        for b in range(width // LANES):
            blk = p[:, LANES * b:LANES * (b + 1)]
            rot = jnp.where(first, pltpu.roll(blk, LANES - half, 1), pltpu.roll(blk, half, 1))
            o_ref[:, c0 + LANES * b:c0 + LANES * (b + 1)] = (blk * cos + rot * sin).astype(o_ref.dtype)

    def plain_store(p, c0, width):
        o_ref[:, c0:c0 + width] = p.astype(o_ref.dtype)

    u = proj(COL_S5, IN_WIDTH).astype(w_ref.dtype)
    sub = perm_ref.shape[0]
    nchunk = sub // S5_T
    for part in range(x.shape[0] // sub):
        g = _mm(perm_ref[...], u[sub * part:sub * (part + 1)])
        for a in range(S5_PAIRS):
            vreg_col, lane_blk = divmod(a * S5_PAIR_W, LANES)
            lane_blk //= S5_PAIR_W
            for v in range(S5_T * S5_PAIR_W // LANES):
                slab = _lane_block_shuffle(
                    lambda j: g[nchunk * j:nchunk * (j + 1), vreg_col * LANES:(vreg_col + 1) * LANES], lane_blk, v)
                u_ref[a, nchunk * part:nchunk * (part + 1), LANES * v:LANES * (v + 1)] = slab.astype(u_ref.dtype)

    ca, sa, cr, sr = ca_ref[...], sa_ref[...], cr_ref[...], sr_ref[...]
    att_rope = (ca, sa, first_att, HEAD_DIM // 4)
    ret_rope = (cr, sr, first_ret, HEAD_DIM // 2)
    groups = [(COL_AQ, ATT_WIDTH, att_rope), (COL_AK, KV_WIDTH, att_rope), (COL_RQ, RET_WIDTH, ret_rope),
              (COL_RK, RET_WIDTH, ret_rope), (COL_AV, KV_WIDTH, None), (COL_RV, COL_S5 - COL_RV, None)]
    pending = None
    for group in groups + [None]:
        nxt = None if group is None else (proj(group[0], group[0] + group[1]),) + group
        if pending is not None:
            p, pc0, pwidth, prope = pending
            if prope is None:
                plain_store(p, pc0, pwidth)
            else:
                rope_store(p, pc0, pwidth, *prope)
        pending = nxt


def _chunk_perm(tile_rows, dtype):
    nchunk = tile_rows // S5_T
    r = jnp.arange(tile_rows)
    src = S5_T * (r % nchunk) + r // nchunk
    return (src[:, None] == jnp.arange(tile_rows)[None, :]).astype(dtype)


def _in_proj(residual, mods, layer, w_in, col_scale, tabs, perm):
    rows = tabs[0].shape[0]
    tm = ROW_TILE
    tab_spec = pl.BlockSpec((tm, LANES), lambda i: (i, 0))
    nch = rows // S5_T
    return pl.pallas_call(
        functools.partial(_in_proj_kernel, split=len(residual) > 1, layer=layer),
        grid=(rows // tm,),
        in_specs=_residual_specs(residual) + [
            pl.BlockSpec((None, None, N_ADA, D_MODEL), lambda i: (layer, jnp.where(i == 0, 1, 0), 0, 0)),
            pl.BlockSpec(memory_space=pl.ANY),
            pl.BlockSpec((1, IN_WIDTH), lambda i: (0, 0)),
            tab_spec, tab_spec, tab_spec, tab_spec,
            pl.BlockSpec(perm.shape, lambda i: (0, 0)),
        ],
        out_specs=[pl.BlockSpec((tm, COL_S5), lambda i: (i, 0)),
                   pl.BlockSpec((S5_PAIRS, tm // S5_T, S5_T * S5_PAIR_W), lambda i: (0, i, 0))],
        out_shape=[jax.ShapeDtypeStruct((rows, COL_S5), MXU_DTYPE),
                   jax.ShapeDtypeStruct((S5_PAIRS, nch, S5_T * S5_PAIR_W), MXU_DTYPE)],
        scratch_shapes=[pltpu.VMEM((D_MODEL, IN_WIDTH), MXU_DTYPE),
                        pltpu.VMEM((2, STAGE_BYTES // (4 * IN_WIDTH), IN_WIDTH), F32),
                        pltpu.SemaphoreType.DMA((2,))],
        compiler_params=_params(("arbitrary",)),
        name="in_proj",
    )(*residual, mods, w_in, col_scale, *tabs, perm)


def _rope_tables(seq):
    half_a = HEAD_DIM // 4
    half_r = HEAD_DIM // 2
    nrow = seq // GRID_W
    inv_a = ROPE_BASE ** (-jnp.arange(half_a, dtype=F32) / half_a)
    inv_r = ROPE_BASE ** (-jnp.arange(half_r, dtype=F32) / half_r)
    ang_r = jnp.arange(nrow, dtype=F32)[:, None] * inv_a[None, :]
    ang_c = jnp.arange(GRID_W, dtype=F32)[:, None] * inv_a[None, :]
    ang_t = jnp.arange(seq, dtype=F32)[:, None] * inv_r[None, :]
    hp = lax.Precision.HIGHEST
    lane = jnp.arange(LANES)
    within = lane % HEAD_DIM
    pick_a = (within % half_a)[None, :] == jnp.arange(half_a)[:, None]
    exp_row = (pick_a & (within < 2 * half_a)[None, :]).astype(F32)
    exp_col = (pick_a & (within >= 2 * half_a)[None, :]).astype(F32)
    exp_t = ((lane % half_r)[None, :] == jnp.arange(half_r)[:, None]).astype(F32)
    sign_a = jnp.where(within % (2 * half_a) < half_a, -1.0, 1.0).astype(F32)
    sign_r = jnp.where(within < half_r, -1.0, 1.0).astype(F32)

    def att_table(fn):
        by_row = jnp.dot(fn(ang_r), exp_row, precision=hp)
        by_col = jnp.dot(fn(ang_c), exp_col, precision=hp)
        return (by_row[:, None, :] + by_col[None, :, :]).reshape(seq, LANES)

    cos_a = att_table(jnp.cos)
    sin_a = att_table(jnp.sin) * sign_a
    cos_r = jnp.dot(jnp.cos(ang_t), exp_t, precision=hp)
    sin_r = jnp.dot(jnp.sin(ang_t), exp_t, precision=hp) * sign_r
    pad = lambda tab, ident: jnp.pad(tab, ((CTX_PAD, 0), (0, 0)), constant_values=ident)
    return pad(cos_a, 1.0), pad(sin_a, 0.0), pad(cos_r, 1.0), pad(sin_r, 0.0)


def _swap_halves(x):
    if x.dtype.itemsize == 4:
        return pltpu.roll(x, 64, 1)
    packed = pltpu.bitcast(x, jnp.uint32)
    return pltpu.bitcast(pltpu.roll(packed, 64, 1), x.dtype)


def _dup_heads(x):
    sw = _swap_halves(x)
    lo = lax.broadcasted_iota(jnp.int32, x.shape, 1) < HEAD_DIM
    return jnp.where(lo, x, sw), jnp.where(lo, sw, x)


def _attn_kernel(sink_ref, q_ref, km_ref, kp_ref, kn_ref, vm_ref, vp_ref, vn_ref, kc_ref, vc_ref, mask_ref,
                 o_ref, k2_ref, v2_ref, kc2_ref, vc2_ref):
    i = pl.program_id(0)
    last_blk = pl.num_programs(0) * (ROW_TILE // ATT_BLOCK) - 1
    blk = ATT_BLOCK
    def spread(src, ones_upper):
        x = src[...]
        a, b = _dup_heads(x)
        if ones_upper:
            upper = lax.broadcasted_iota(jnp.int32, x.shape, 1) >= HEAD_DIM
            a = jnp.where(upper, jnp.ones_like(a), a)
            b = jnp.where(upper, jnp.ones_like(b), b)
        return a, b

    for dst, parts, is_v in ((k2_ref, (kp_ref, km_ref, kn_ref), False), (v2_ref, (vp_ref, vm_ref, vn_ref), True)):
        row = 0
        for part in parts:
            a, b = spread(part, is_v)
            n = part.shape[0]
            dst[0, row:row + n, :] = a
            dst[1, row:row + n, :] = b
            row += n
    for dst, src, is_v in ((kc2_ref, kc_ref, False), (vc2_ref, vc_ref, True)):
        a, b = spread(src, is_v)
        dst[0] = a
        dst[1] = b

    lo = lax.broadcasted_iota(jnp.int32, (blk, LANES), 1) < HEAD_DIM
    group = ATT_HEADS // ATT_KV_HEADS

    nloc = 3 * blk

    def scores(j, kv):
        r0 = j * blk
        qt = q_ref[r0:r0 + blk, group * HEAD_DIM * kv:group * HEAD_DIM * (kv + 1)]
        parts = []
        for g in range(group):
            qc = qt[:, LANES * (g // 2):LANES * (g // 2 + 1)]
            keep = lo if g % 2 == 0 else jnp.logical_not(lo)
            parts.append(jnp.where(keep, qc, jnp.zeros_like(qc)))
        qs = jnp.concatenate(parts, axis=0)
        return _mm_nt(qs, k2_ref[kv, r0:r0 + nloc, :]), _mm_nt(qs, kc2_ref[kv])

    def finish(j, kv, s_loc, s_ctx):
        r0 = j * blk
        gblk = i * (ROW_TILE // blk) + j
        sel = jnp.where(i == 0, 3, jnp.where(gblk == CTX_PAD // blk, 0, jnp.where(gblk == last_blk, 2, 1)))
        bias = mask_ref[sel]
        probs, sink_w = [], []
        for g in range(group):
            s = jnp.concatenate([s_loc[blk * g:blk * (g + 1)] + bias, s_ctx[blk * g:blk * (g + 1)]], axis=1)
            sk = sink_ref[group * kv + g]
            m = jnp.maximum(jnp.max(s, axis=-1, keepdims=True), sk)
            probs.append(jnp.exp2(s - m).astype(o_ref.dtype))
            sink_w.append(jnp.exp2(sk - m))
        p = jnp.concatenate(probs, axis=0)
        o = _mm(p[:, :nloc], v2_ref[kv, r0:r0 + nloc, :]) + _mm(p[:, nloc:], vc2_ref[kv])
        for half in range(group // 2):
            even, odd = 2 * half, 2 * half + 1
            oe = o[blk * even:blk * (even + 1)]
            oo = o[blk * odd:blk * (odd + 1)]
            y_even = oe * (1.0 / (pltpu.roll(oe, HEAD_DIM, 1) + sink_w[even]))
            y_odd = pltpu.roll(oo, HEAD_DIM, 1) * (1.0 / (oo + sink_w[odd]))
            c0 = group * HEAD_DIM * kv + LANES * half
            o_ref[r0:r0 + blk, c0:c0 + LANES] = jnp.where(lo, y_even, y_odd).astype(o_ref.dtype)

    items = [(j, kv) for j in range(ROW_TILE // blk) for kv in range(ATT_KV_HEADS)]
    pending = {}
    for t in range(len(items) + ATT_LOOKAHEAD):
        if t < len(items):
            pending[t] = scores(*items[t])
        if t >= ATT_LOOKAHEAD:
            finish(*items[t - ATT_LOOKAHEAD], *pending.pop(t - ATT_LOOKAHEAD))


def _attention_masks():
    qi = jnp.arange(ATT_BLOCK)[:, None]
    kj = jnp.arange(3 * ATT_BLOCK)[None, :]
    band = jnp.abs(kj - ATT_BLOCK - qi) <= ATT_BLOCK
    first = band & (kj >= ATT_BLOCK)
    last = band & (kj < 2 * ATT_BLOCK)
    none = jnp.zeros_like(band)
    masks = jnp.stack([first, band, last, none])
    return jnp.where(masks, 0.0, NEG_BIG).astype(F32)


def _attention(proj, sink, masks):
    rows = proj.shape[0]
    tm, blk = ROW_TILE, ATT_BLOCK
    per = tm // blk
    nblk = rows // blk
    ck, cv = COL_AK // KV_WIDTH, COL_AV // KV_WIDTH
    dt = proj.dtype
    return pl.pallas_call(
        _attn_kernel,
        grid=(rows // tm,),
        in_specs=[
            pl.BlockSpec(memory_space=pltpu.SMEM),
            pl.BlockSpec((tm, ATT_WIDTH), lambda i: (i, 0)),
            pl.BlockSpec((tm, KV_WIDTH), lambda i: (i, ck)),
            pl.BlockSpec((blk, KV_WIDTH), lambda i: (jnp.maximum(i * per - 1, 0), ck)),
            pl.BlockSpec((blk, KV_WIDTH), lambda i: (jnp.minimum((i + 1) * per, nblk - 1), ck)),
            pl.BlockSpec((tm, KV_WIDTH), lambda i: (i, cv)),
            pl.BlockSpec((blk, KV_WIDTH), lambda i: (jnp.maximum(i * per - 1, 0), cv)),
            pl.BlockSpec((blk, KV_WIDTH), lambda i: (jnp.minimum((i + 1) * per, nblk - 1), cv)),
            pl.BlockSpec((CTX_LEN, KV_WIDTH), lambda i: (0, ck)),
            pl.BlockSpec((CTX_LEN, KV_WIDTH), lambda i: (0, cv)),
            pl.BlockSpec((4, blk, 3 * blk), lambda i: (0, 0, 0)),
        ],
        out_specs=pl.BlockSpec((tm, ATT_WIDTH), lambda i: (i, 0)),
        out_shape=jax.ShapeDtypeStruct((rows, ATT_WIDTH), dt),
        scratch_shapes=[
            pltpu.VMEM((2, tm + 2 * blk, KV_WIDTH), dt),
            pltpu.VMEM((2, tm + 2 * blk, KV_WIDTH), dt),
            pltpu.VMEM((2, CTX_LEN, KV_WIDTH), dt),
            pltpu.VMEM((2, CTX_LEN, KV_WIDTH), dt),
        ],
        compiler_params=_params(("arbitrary",)),
        name="attention",
    )(sink, proj, proj, proj, proj, proj, proj, proj, proj, proj, masks)


def _ret_kernel(*refs, sup, ntile):
    lg_ref, q_ref = refs[:2]
    k_refs, v_refs = refs[2:2 + sup], refs[2 + sup:2 + 2 * sup]
    g_ref, o_ref, sb_ref, s_ref, dm_ref, tab_ref, gbd_ref = refs[2 + 2 * sup:]
    k_ref, v_ref = k_refs[0], v_refs[0]
    step = pl.program_id(0)
    nsup = (ntile - 1) // sup
    ph = jnp.where(step <= nsup, 0, 1)
    t = jnp.where(step <= nsup, step, step - (nsup + 1))
    c = RET_CHUNK
    w = RET_WIDTH
    per = q_ref.shape[0] // c
    mdt = sb_ref.dtype
    rows = lambda ci: slice(c * ci, c * (ci + 1))

    def lane_vec(direction, shape, axis):
        head = lax.broadcasted_iota(jnp.int32, shape, axis) // HEAD_DIM
        out = jnp.full(shape, lg_ref[direction, RET_HEADS - 1], F32)
        for h in range(RET_HEADS - 2, -1, -1):
            out = jnp.where(head == h, lg_ref[direction, h], out)
        return out

    @pl.when(jnp.logical_and(ph == 0, t == 0))
    def _init_tables():
        diff = (lax.broadcasted_iota(jnp.int32, (c, c), 0) - lax.broadcasted_iota(jnp.int32, (c, c), 1)).astype(F32)
        for h in range(RET_HEADS):
            dm_ref[h] = jnp.exp(jnp.where(diff >= 0, diff * lg_ref[0, h], -diff * lg_ref[1, h]))
        pos = lax.broadcasted_iota(jnp.int32, (c, w), 0).astype(F32)
        lgf = lane_vec(0, (c, w), 1)
        lgb = lane_vec(1, (c, w), 1)
        tab_ref[0] = jnp.exp((c - 1.0 - pos) * lgf)
        tab_ref[1] = jnp.exp((pos + 1.0) * lgf)
        tab_ref[2] = jnp.exp(pos * lgb)
        tab_ref[3] = jnp.exp((c - pos) * lgb)
        same = (lax.broadcasted_iota(jnp.int32, (w, w), 0) // HEAD_DIM
                == lax.broadcasted_iota(jnp.int32, (w, w), 1) // HEAD_DIM)
        bd = jnp.where(same, 1.0, 0.0)
        gbd_ref[0] = bd * jnp.exp(c * lane_vec(0, (w, w), 0))
        gbd_ref[1] = bd * jnp.exp(c * lane_vec(1, (w, w), 0))
        gbd_ref[2] = bd

    @pl.when(t == 0)
    def _reset_state():
        s_ref[...] = jnp.zeros_like(s_ref)

    def state_update(direction, key_tab, ci, kr=k_ref, vr=v_ref):
        kw = (kr[rows(ci), :].astype(F32) * tab_ref[key_tab]).astype(mdt)
        u = _mm_tn(kw, vr[rows(ci), :])
        s_ref[...] = gbd_ref[direction] * s_ref[...] + gbd_ref[2] * u

    @pl.when(jnp.logical_and(ph == 0, t == 0))
    def _backward_context():
        sb_ref[0] = s_ref[...].astype(mdt)
        state_update(1, 2, 0)

    @pl.when(jnp.logical_and(ph == 0, t > 0))
    def _backward_latent():
        base = 1 + per * sup * (nsup - t)
        for qi in range(sup - 1, -1, -1):
            for ci in range(per - 1, -1, -1):
                sb_ref[base + per * qi + ci] = s_ref[...].astype(mdt)
                state_update(1, 2, ci, k_refs[qi], v_refs[qi])

    head = lax.broadcasted_iota(jnp.int32, (c, w), 1) // HEAD_DIM

    def scores(ci):
        q = q_ref[rows(ci), :]
        qs = jnp.concatenate([jnp.where(head == h, q, jnp.zeros_like(q)) for h in range(RET_HEADS)], axis=0)
        return _mm_nt(qs, k_ref[rows(ci), :])

    def intra(ci, sc):
        scd = jnp.concatenate([sc[c * h:c * (h + 1)] * dm_ref[h] for h in range(RET_HEADS)], axis=0).astype(mdt)
        oi = _mm(scd, v_ref[rows(ci), :])
        o = jnp.where(head == 0, oi[0:c], 0.0)
        for h in range(1, RET_HEADS):
            o = o + jnp.where(head == h, oi[c * h:c * (h + 1)], 0.0)
        return o

    def cross(ci, idx):
        qf = q_ref[rows(ci), :].astype(F32)
        return (_mm((qf * tab_ref[1]).astype(mdt), s_ref[...].astype(mdt))
                + _mm((qf * tab_ref[3]).astype(mdt), sb_ref[idx]))

    def finish(ci, o):
        avg = (gbd_ref[2] * (1.0 / HEAD_DIM)).astype(mdt)
        o_hi = o.astype(mdt)
        d = o - (_mm(o_hi, avg) + _mm((o - o_hi.astype(F32)).astype(mdt), avg))
        var = _mm((d * d).astype(mdt), avg)
        gate = g_ref[rows(ci), :].astype(F32)
        y = d * lax.rsqrt(var + GN_EPS) * (gate * jax.nn.sigmoid(gate))
        o_ref[rows(ci), :] = y.astype(o_ref.dtype)

    @pl.when(jnp.logical_and(ph == 1, t == 0))
    def _forward_context():
        o = intra(0, scores(0)) + cross(0, 0)
        state_update(0, 0, 0)
        finish(0, o)
        for ci in range(1, per):
            o_ref[rows(ci), :] = jnp.zeros((c, w), o_ref.dtype)

    @pl.when(jnp.logical_and(ph == 1, t > 0))
    def _forward_latent():
        base = 1 + per * (t - 1)
        sc = [scores(ci) for ci in range(per)]
        outs = [intra(ci, sc[ci]) for ci in range(per)]
        for ci in range(per):
            outs[ci] = outs[ci] + cross(ci, base + ci)
            state_update(0, 0, ci)
        for ci in range(per):
            finish(ci, outs[ci])


def _retention(proj, log_gamma):
    rows = proj.shape[0]
    c = RET_CHUNK
    tm = ROW_TILE
    ntile = rows // tm
    nchunk = 1 + (rows - CTX_PAD) // c
    dt = proj.dtype
    cq, ckk, cvv, cg = (COL_RQ // RET_WIDTH, COL_RK // RET_WIDTH, COL_RV // RET_WIDTH, COL_RG // RET_WIDTH)

    nlat = ntile - 1
    sup = next(s for s in (4, 2, 1) if nlat % s == 0)
    nsup = nlat // sup
    nback = 1 + nsup

    def fw_blk(step):
        return jnp.maximum(step - nback, 0)

    def kv_spec(col, qi):
        def index(step):
            back = jnp.where(step == 0, 0, 1 + sup * (nsup - jnp.minimum(step, nsup)) + qi)
            fwd = step - nback if qi == 0 else 1 + qi
            return jnp.where(step < nback, back, fwd), col
        return pl.BlockSpec((tm, RET_WIDTH), index)

    return pl.pallas_call(
        functools.partial(_ret_kernel, sup=sup, ntile=ntile),
        grid=(nback + ntile,),
        in_specs=[pl.BlockSpec(memory_space=pltpu.SMEM),
                  pl.BlockSpec((tm, RET_WIDTH), lambda step: (fw_blk(step), cq))]
                 + [kv_spec(ckk, qi) for qi in range(sup)] + [kv_spec(cvv, qi) for qi in range(sup)]
                 + [pl.BlockSpec((tm, RET_WIDTH), lambda step: (fw_blk(step), cg))],
        out_specs=pl.BlockSpec((tm, RET_WIDTH), lambda step: (fw_blk(step), 0)),
        out_shape=jax.ShapeDtypeStruct((rows, RET_WIDTH), dt),
        scratch_shapes=[
            pltpu.VMEM((nchunk, RET_WIDTH, RET_WIDTH), dt),
            pltpu.VMEM((RET_WIDTH, RET_WIDTH), F32),
            pltpu.VMEM((RET_HEADS, c, c), F32),
            pltpu.VMEM((4, c, RET_WIDTH), F32),
            pltpu.VMEM((3, RET_WIDTH, RET_WIDTH), F32),
        ],
        compiler_params=_params(("arbitrary",)),
        name="retention",
    )(log_gamma, proj, *([proj] * (2 * sup)), proj)


def _s5_weights(lam_re, lam_im, b_re, b_im, c_re, c_im, log_dt, d_skip):
    tt, g, n, p, a = S5_T, S5_GROUPS, S5_STATE, S5_CH, S5_PAIRS
    lam = lax.complex(lam_re.astype(F32), lam_im.astype(F32))
    dtv = jnp.exp(log_dt.astype(F32))[..., None]
    lam_bar = jnp.exp(lam * dtv)
    bbar = ((lam_bar - 1.0) / lam)[..., None] * lax.complex(b_re.astype(F32), b_im.astype(F32))
    cmat = lax.complex(c_re.astype(F32), c_im.astype(F32))
    pw = [jnp.ones_like(lam_bar)]
    for _ in range(tt):
        pw.append(pw[-1] * lam_bar)
    pw = jnp.stack(pw, axis=1)
    eye2 = jnp.eye(2, dtype=F32)
    ri = lambda z, axis: jnp.stack([jnp.real(z), jnp.imag(z)], axis=axis)

    pw_l = pw.reshape(2, tt + 1, a, 2 * n)
    bbt = jnp.einsum('dahpn,gh->dagphn', jnp.swapaxes(bbar, -1, -2).reshape(2, a, 2, p, n), eye2)
    bbt = bbt.reshape(2, a, 2 * p, 2 * n)
    cct = jnp.einsum('dahpn,gh->dagphn', cmat.reshape(2, a, 2, p, n), eye2).reshape(2, a, 2 * p, 2 * n)
    pw_k = ri(pw_l, 1).transpose(3, 0, 1, 2, 4)
    b_k = ri(bbt, 2).transpose(1, 0, 2, 3, 4)
    c_k = ri(cct, 2).transpose(1, 0, 2, 3, 4)
    decay = [pw_l[:, tt]]
    for _ in range(SCAN_ROWS - 1):
        decay.append(decay[-1] * decay[0])
    decay = jnp.stack(decay, axis=0)
    rows8 = lambda z: jnp.broadcast_to(z[None], (SCAN_ROWS,) + z.shape)
    carry_w = jnp.stack([decay[:, 0], decay[::-1, 1]], axis=1)
    r8 = jnp.arange(SCAN_ROWS)
    keep = lambda k: jnp.stack([r8 >= k, r8 < SCAN_ROWS - k], axis=1)[:, :, None, None]
    scan_tab = jnp.stack([jnp.where(keep(k), rows8(decay[k - 1]), 0.0) for k in (1, 2, 4)] + [carry_w],
                         axis=0)
    scan_tab = ri(scan_tab, 0).transpose(4, 3, 1, 0, 2, 5)
    skip = jnp.tile(d_skip.astype(F32).reshape(a, 1, 2 * p), (1, tt, 1)).reshape(a, 1, tt * 2 * p)
    return pw_k, b_k, c_k, scan_tab, skip


def _s5_drive_kernel(u_ref, pw_ref, b_ref, o_ref, w_ref):
    rows, half = S5_PAIR_W, LANES
    for d in range(2):
        br, bi = b_ref[d, 0], b_ref[d, 1]
        for j in range(S5_T):
            e = S5_T - 1 - j if d == 0 else j
            pr, pi = pw_ref[d, 0, e:e + 1, :], pw_ref[d, 1, e:e + 1, :]
            w_ref[rows * j:rows * (j + 1), 2 * half * d:2 * half * d + half] = (pr * br - pi * bi).astype(w_ref.dtype)
            w_ref[rows * j:rows * (j + 1), 2 * half * d + half:2 * half * (d + 1)] = (
                pr * bi + pi * br).astype(w_ref.dtype)
    o_ref[0] = _mm(u_ref[0], w_ref[...])


def _lane_window(x, start, width):
    cols = []
    for v in range(width // LANES):
        k0, off = divmod(start + LANES * v, LANES)
        lo = x[:, LANES * k0:LANES * (k0 + 1)]
        if off:
            hi = x[:, LANES * (k0 + 1):LANES * (k0 + 2)]
            lane = lax.broadcasted_iota(jnp.int32, lo.shape, 1)
            lo = jnp.where(lane < LANES - off, pltpu.roll(lo, LANES - off, 1), pltpu.roll(hi, LANES - off, 1))
        cols.append(lo)
    return jnp.concatenate(cols, axis=1)


def _s5_read_kernel(u_ref, sf_ref, sb_ref, pw_ref, b_ref, c_ref, skip_ref, o_ref, wt_ref, wi_ref, lag_ref):
    u = u_ref[0]
    mdt = u.dtype
    rows, half = S5_PAIR_W, LANES
    for d in range(2):
        cr, ci = c_ref[d, 0], c_ref[d, 1]
        for i in range(S5_T):
            e = i + 1 if d == 0 else S5_T - i
            pr, pi = pw_ref[d, 0, e:e + 1, :], pw_ref[d, 1, e:e + 1, :]
            wt_ref[d, rows * i:rows * (i + 1), 0:half] = (pr * cr - pi * ci).astype(mdt)
            wt_ref[d, rows * i:rows * (i + 1), half:2 * half] = (-(pr * ci + pi * cr)).astype(mdt)
    nlag = 2 * S5_T - 1
    ldt = lag_ref.dtype
    zero = jnp.zeros((rows, half), ldt)
    for l in range(nlag + 1):
        lag = l - (S5_T - 1)
        for d, active in ((1, lag <= 0), (0, 0 <= lag < S5_T)):
            col = 2 * half * (1 - d)
            if active:
                cr, ci = c_ref[d, 0], c_ref[d, 1]
                pr, pi = pw_ref[d, 0, abs(lag):abs(lag) + 1, :], pw_ref[d, 1, abs(lag):abs(lag) + 1, :]
                lag_ref[rows * l:rows * (l + 1), col:col + half] = (pr * cr - pi * ci).astype(ldt)
                lag_ref[rows * l:rows * (l + 1), col + half:col + 2 * half] = (pr * ci + pi * cr).astype(ldt)
            else:
                lag_ref[rows * l:rows * (l + 1), col:col + half] = zero
                lag_ref[rows * l:rows * (l + 1), col + half:col + 2 * half] = zero
    lhs = jnp.concatenate([b_ref[1, 0], -b_ref[1, 1], b_ref[0, 0], -b_ref[0, 1]], axis=1).astype(ldt)
    kall = _mm_nt(lhs, lag_ref[...])
    for j in range(S5_T):
        wi_ref[rows * j:rows * (j + 1), :] = _lane_window(kall, rows * (S5_T - 1 - j), S5_T * rows).astype(mdt)
    y = _mm(u, wi_ref[...])
    y = y + _mm_nt(sf_ref[0].astype(mdt), wt_ref[0])
    y = y + _mm_nt(sb_ref[0].astype(mdt), wt_ref[1])
    o_ref[0] = y + u.astype(F32) * skip_ref[...]


def _s5_fused_kernel(u_ref, pw_ref, b_ref, c_ref, tab_ref, skip_ref, o_ref, e_sc, sf_sc, sb_sc, w_sc, wt_sc, wi_sc,
                     lag_sc):
    step = pl.program_id(0)
    npair = S5_PAIRS
    nch = u_ref.shape[1]
    hw, sub = LANES, SCAN_ROWS
    span = 2 * sub
    ctx_rows = CTX_LEN // S5_T

    @pl.when(step < npair)
    def _drive():
        _s5_drive_kernel(u_ref, pw_ref, b_ref, e_sc.at[pl.ds(step, 1)], w_sc)

    @pl.when(step == npair)
    def _scan():
        row = lax.broadcasted_iota(jnp.int32, (sub, hw), 0)

        def shift(x, k, reverse):
            if reverse:
                return jnp.where(row < sub - k, pltpu.roll(x, sub - k, 0), 0.0)
            return jnp.where(row >= k, pltpu.roll(x, k, 0), 0.0)

        def scan_group(a, d, r0, cr, ci):
            reverse = d == 1
            xr = e_sc[a, pl.ds(r0, sub), 2 * hw * d:2 * hw * d + hw]
            xi = e_sc[a, pl.ds(r0, sub), 2 * hw * d + hw:2 * hw * (d + 1)]
            for k_i, k in enumerate((1, 2, 4)):
                ar, ai = tab_ref[a, d, k_i, 0], tab_ref[a, d, k_i, 1]
                amount = sub - k if reverse else k
                sr, si = pltpu.roll(xr, amount, 0), pltpu.roll(xi, amount, 0)
                xr, xi = xr + ar * sr - ai * si, xi + ar * si + ai * sr
            wr, wi = tab_ref[a, d, 3, 0], tab_ref[a, d, 3, 1]
            fr = xr + wr * cr - wi * ci
            fi = xi + wr * ci + wi * cr
            edge = sub - 1 if reverse else 0
            before = (jnp.where(row == edge, cr, shift(fr, 1, reverse)),
                      jnp.where(row == edge, ci, shift(fi, 1, reverse)))
            last = 0 if reverse else sub - 1
            after = (jnp.broadcast_to(fr[last:last + 1], (sub, hw)), jnp.broadcast_to(fi[last:last + 1], (sub, hw)))
            return before, after

        def store(dst, a, r0, lower, upper):
            for part in range(2):
                val = jnp.concatenate([lower[part], upper[part]], axis=0)
                dst[a, pl.ds(r0, span), part * hw:(part + 1) * hw] = val.astype(dst.dtype)

        def do_span(carry, rf0, rb0):
            new = []
            for a in range(npair):
                fr, fi, br, bi = carry[a]
                f_lo, (fr, fi) = scan_group(a, 0, rf0, fr, fi)
                f_hi, (fr, fi) = scan_group(a, 0, rf0 + sub, fr, fi)
                store(sf_sc, a, rf0, f_lo, f_hi)
                b_hi, (br, bi) = scan_group(a, 1, rb0 + sub, br, bi)
                b_lo, (br, bi) = scan_group(a, 1, rb0, br, bi)
                store(sb_sc, a, rb0, b_lo, b_hi)
                new.append((fr, fi, br, bi))
            return tuple(new)

        sf_sc[:, ctx_rows:S5_TILE, :] = jnp.zeros((npair, S5_TILE - ctx_rows, 2 * hw), sf_sc.dtype)
        sb_sc[:, ctx_rows:S5_TILE, :] = jnp.zeros((npair, S5_TILE - ctx_rows, 2 * hw), sb_sc.dtype)
        zero = jnp.zeros((sub, hw), F32)
        carry = tuple((zero, zero, zero, zero) for _ in range(npair))
        for g in range(ctx_rows // span):
            carry = do_span(carry, span * g, ctx_rows - span * (g + 1))

        def body(gi, carry):
            rf0 = pl.multiple_of(S5_TILE + gi * span, span)
            rb0 = pl.multiple_of(nch - span - gi * span, span)
            return do_span(carry, rf0, rb0)

        lax.fori_loop(0, (nch - S5_TILE) // span, body, carry)

    @pl.when(step > npair)
    def _read():
        pair = step - npair - 1
        _s5_read_kernel(u_ref, sf_sc.at[pl.ds(pair, 1)], sb_sc.at[pl.ds(pair, 1)], pw_ref, b_ref, c_ref, skip_ref,
                        o_ref, wt_sc, wi_sc, lag_sc)


def _s5_mixer(u_pairs, weights, layer):
    pw_k, b_k, c_k, scan_tab, skip = weights
    a, nch, wd = u_pairs.shape
    dt = u_pairs.dtype

    def in_pair(step):
        return jnp.where(step < a, step, jnp.where(step == a, a - 1, step - a - 1))

    pair_blk = lambda arr: pl.BlockSpec((None, None) + arr.shape[2:],
                                        lambda s: (layer, in_pair(s)) + (0,) * (arr.ndim - 2))
    return pl.pallas_call(
        _s5_fused_kernel,
        grid=(2 * a + 1,),
        in_specs=[pl.BlockSpec((1, nch, wd), lambda s: (in_pair(s), 0, 0)),
                  pair_blk(pw_k), pair_blk(b_k), pair_blk(c_k),
                  pl.BlockSpec((None,) + scan_tab.shape[1:], lambda s: (layer,) + (0,) * (scan_tab.ndim - 1)),
                  pair_blk(skip)],
        out_specs=pl.BlockSpec((1, nch, wd), lambda s: (jnp.maximum(s - a - 1, 0), 0, 0)),
        out_shape=jax.ShapeDtypeStruct((a, nch, wd), F32),
        scratch_shapes=[pltpu.VMEM((a, nch, 2 * S5_STATE_W), F32),
                        pltpu.VMEM((a, nch, S5_STATE_W), dt), pltpu.VMEM((a, nch, S5_STATE_W), dt),
                        pltpu.VMEM((wd, 2 * S5_STATE_W), dt),
                        pltpu.VMEM((2, wd, S5_STATE_W), dt), pltpu.VMEM((wd, wd), dt),
                        pltpu.VMEM((2 * wd, 2 * S5_STATE_W), dt)],
        compiler_params=_params(("arbitrary",)),
        name="s5",
    )(u_pairs, pw_k, b_k, c_k, scan_tab, skip)


def _layer_norm(x, g, b):
    mu = jnp.mean(x, axis=-1, keepdims=True)
    d = x - mu
    var = jnp.mean(d * d, axis=-1, keepdims=True)
    return d * lax.rsqrt(var + LN_EPS) * g + b


def _residual_rows(refs):
    if len(refs) == 1:
        return lambda rows: refs[0][rows, :]
    head_ref, body_ref = refs
    is_head = pl.program_id(0) == 0
    return lambda rows: jnp.where(is_head, head_ref[rows, :], body_ref[rows, :])


def _load_cast(src_hbm, dst_ref, stage_ref, sems, col_scale=None):
    chunk = stage_ref.shape[1]
    nchunk = src_hbm.shape[0] // chunk

    def copy(k):
        slot = k % 2
        return pltpu.make_async_copy(src_hbm.at[pl.ds(k * chunk, chunk), :], stage_ref.at[slot], sems.at[slot])

    copy(0).start()
    for k in range(nchunk):
        if k + 1 < nchunk:
            copy(k + 1).start()
        copy(k).wait()
        vals = stage_ref[k % 2]
        if col_scale is not None:
            vals = vals * col_scale
        dst_ref[k * chunk:(k + 1) * chunk, :] = vals.astype(dst_ref.dtype)


def _post_kernel(*refs, split, layer):
    nres = 2 if split else 1
    load_x = _residual_rows(refs[:nres])
    (att_ref, ret_ref, s5_ref, mod_ref, permt_ref, wglu_ref, bglu_ref, wo_hbm, g1_ref, b1_ref, w1_hbm, w2_hbm,
     g2_ref, b2_ref, o_ref, wo_ref, w1_ref, w2_ref, stage_ref, sem_ref) = refs[nres:]
    woa_ref = wo_ref.at[0:ATT_WIDTH]
    wor_ref = wo_ref.at[ATT_WIDTH:ATT_WIDTH + RET_WIDTH]
    wos_ref = wo_ref.at[ATT_WIDTH + RET_WIDTH:ATT_WIDTH + RET_WIDTH + S5_WIDTH]
    mdt = w1_ref.dtype
    sub = o_ref.shape[0] // POST_SPLIT
    csub = sub // S5_T
    nff = D_FF // FF_CHUNK

    blocks = lambda c: slice(FF_CHUNK * c, FF_CHUNK * (c + 1))
    jobs = [(wo_hbm.at[layer], wo_ref)]
    for c in range(nff):
        jobs.append((w1_hbm.at[layer, :, blocks(c)], w1_ref.at[:, blocks(c)]))
        jobs.append((w2_hbm.at[layer, blocks(c), :], w2_ref.at[blocks(c), :]))

    def job_copy(k):
        return pltpu.make_async_copy(jobs[k][0], stage_ref.at[k % 2], sem_ref.at[k % 2])

    def job_finish(k):
        job_copy(k).wait()
        jobs[k][1][...] = stage_ref[k % 2].astype(mdt)
        if k + 2 < len(jobs):
            job_copy(k + 2).start()

    def mix(part):
        rows = slice(sub * part, sub * (part + 1))
        zrows = []
        for i in range(S5_T):
            src_vreg, src_blk = divmod(i * S5_PAIR_W, LANES)
            src_blk //= S5_PAIR_W
            cols = [_lane_block_shuffle(
                lambda a: s5_ref[a, csub * part:csub * (part + 1), src_vreg * LANES:(src_vreg + 1) * LANES],
                src_blk, w) for w in range(S5_WIDTH // LANES)]
            zrows.append(jnp.concatenate(cols, axis=1))
        hs = jax.nn.gelu(jnp.concatenate(zrows, axis=0))
        gate = jax.nn.sigmoid(_mm(hs.astype(mdt), wglu_ref[...]) + bglu_ref[...])
        s5 = _mm(permt_ref[...], (hs * gate).astype(mdt)).astype(mdt)
        return _mm(att_ref[rows, :], woa_ref[...]) + _mm(ret_ref[rows, :], wor_ref[...]) + _mm(s5, wos_ref[...])

    def norm1(part, ox):
        rows = slice(sub * part, sub * (part + 1))
        x1 = _layer_norm(DEEPNORM_ALPHA * load_x(rows) + mod_ref[2:3, :] * ox, g1_ref[...], b1_ref[...])
        return x1, (x1 * (1.0 + mod_ref[4:5, :]) + mod_ref[3:4, :]).astype(mdt)

    def ff(h, c):
        a = _mm(h, w1_ref[:, FF_CHUNK * c:FF_CHUNK * (c + 1)])
        a = jnp.square(jnp.maximum(a, 0.0)).astype(mdt)
        return _mm(a, w2_ref[FF_CHUNK * c:FF_CHUNK * (c + 1), :])

    def norm2(part, x1, acc):
        rows = slice(sub * part, sub * (part + 1))
        o_ref[rows, :] = _layer_norm(DEEPNORM_ALPHA * x1 + mod_ref[5:6, :] * acc, g2_ref[...], b2_ref[...])

    nstage = nff + 3

    def run(staging):
        state = [dict() for _ in range(POST_SPLIT)]
        done = -1
        if staging:
            job_copy(0).start()
            job_copy(1).start()
        for part, stage in POST_PROGRAM:
            st = state[part]
            if staging:
                need = 0 if stage < 2 else 2 * (stage - 2) + 2 if stage < nstage - 1 else done
                while done < need:
                    done += 1
                    job_finish(done)
            if stage == 0:
                st["ox"] = mix(part)
            elif stage == 1:
                st["x1"], st["h"] = norm1(part, st.pop("ox"))
            elif stage < nstage - 1:
                term = ff(st["h"], stage - 2)
                st["acc"] = term if stage == 2 else st["acc"] + term
            else:
                norm2(part, st["x1"], st["acc"])
        assert not staging or done == len(jobs) - 1

    @pl.when(pl.program_id(0) == 0)
    def _first_step():
        run(staging=True)

    @pl.when(pl.program_id(0) > 0)
    def _other_steps():
        run(staging=False)


def _residual_specs(residual):
    tm = ROW_TILE
    if len(residual) == 1:
        return [pl.BlockSpec((tm, D_MODEL), lambda i: (i, 0))]
    head_tiles = CTX_PAD // tm
    return [pl.BlockSpec((tm, D_MODEL), lambda i: (jnp.minimum(i, head_tiles - 1), 0)),
            pl.BlockSpec((tm, D_MODEL), lambda i: (jnp.maximum(i - head_tiles, 0), 0))]


def _post(residual, att, ret, s5_pairs, mods, layer, permt, wglu, bglu, wo, g1, b1, w1, w2, g2, b2, skip_context):
    rows = att.shape[0]
    tm = ROW_TILE
    off = CTX_PAD // tm if skip_context else 0
    assert not (skip_context and len(residual) > 1)
    assert FF_CHUNK == D_MODEL == wo.shape[1]
    row_blk = lambda width: pl.BlockSpec((tm, width), lambda i: (i + off, 0))
    full = lambda arr: pl.BlockSpec(arr.shape, lambda i: (0,) * arr.ndim)
    hbm = lambda arr: pl.BlockSpec(memory_space=pl.ANY)
    vec = lambda v: v.reshape(1, -1).astype(F32)
    small = [(permt, full), (wglu, full), (vec(bglu), full), (wo, hbm), (vec(g1), full), (vec(b1), full),
             (w1, hbm), (w2, hbm), (vec(g2), full), (vec(b2), full)]
    res_specs = [row_blk(D_MODEL)] if skip_context else _residual_specs(residual)
    return pl.pallas_call(
        functools.partial(_post_kernel, split=len(residual) > 1, layer=layer),
        grid=(rows // tm - off,),
        in_specs=res_specs + [row_blk(ATT_WIDTH), row_blk(RET_WIDTH),
                              pl.BlockSpec((S5_PAIRS, tm // S5_T, S5_T * S5_PAIR_W), lambda i: (0, i + off, 0)),
                              pl.BlockSpec((None, None, N_ADA, D_MODEL),
                                           lambda i: (layer, jnp.where(i + off == 0, 1, 0), 0, 0))]
                 + [spec(arr) for arr, spec in small],
        out_specs=pl.BlockSpec((tm, D_MODEL), lambda i: (i, 0)),
        out_shape=jax.ShapeDtypeStruct((rows - off * tm, D_MODEL), F32),
        scratch_shapes=[pltpu.VMEM((ATT_WIDTH + RET_WIDTH + S5_WIDTH, D_MODEL), MXU_DTYPE),
                        pltpu.VMEM((D_MODEL, D_FF), MXU_DTYPE), pltpu.VMEM((D_FF, D_MODEL), MXU_DTYPE),
                        pltpu.VMEM((2, D_MODEL, FF_CHUNK), F32),
                        pltpu.SemaphoreType.DMA((2,))],
        compiler_params=_params(("arbitrary",)),
        name="post",
    )(*residual, att, ret, s5_pairs, mods, *[arr for arr, _ in small])


def kernel(x, c, ctx, c_ctx, w_ada, b_ada, w_in, att_sink, ret_decay_logit, s5_lambda_re, s5_lambda_im, s5_b_re,
           s5_b_im, s5_c_re, s5_c_im, s5_log_dt, s5_d, w_glu, b_glu, w_out, ln1_g, ln1_b, w_ff1, w_ff2, ln2_g,
           ln2_b):
    assert x.shape[0] == 1 and x.shape[2] == D_MODEL and ctx.shape[1] == CTX_LEN
    seq = x.shape[1]
    assert seq % ROW_TILE == 0
    residual = (jnp.pad(ctx[0], ((0, CTX_PAD - CTX_LEN), (0, 0))), x[0])
    cond = jnp.zeros((8, D_MODEL), F32).at[0].set(c[0]).at[1].set(c_ctx)
    mods = _modulation(cond, w_ada, b_ada).reshape(DEPTH, 8, N_ADA, D_MODEL)
    tabs = _rope_tables(seq)
    masks = _attention_masks()
    perm = _chunk_perm(ROW_TILE // POST_SPLIT, MXU_DTYPE)
    permt = perm.T
    col_scale = jnp.ones((IN_WIDTH,), F32).at[COL_AQ:COL_AK].set(HEAD_DIM ** -0.5 * LOG2E)
    col_scale = col_scale.at[COL_RQ:COL_RK].set(HEAD_DIM ** -0.5)
    s5w = jax.vmap(_s5_weights)(s5_lambda_re, s5_lambda_im, s5_b_re, s5_b_im, s5_c_re, s5_c_im, s5_log_dt, s5_d)
    log_gamma = jax.nn.log_sigmoid(ret_decay_logit.astype(F32))
    for l in range(DEPTH):
        proj, u_pairs = _in_proj(residual, mods, l, w_in, col_scale.reshape(1, IN_WIDTH), tabs, perm)
        att = _attention(proj, att_sink[l].astype(F32) * LOG2E, masks)
        ret = _retention(proj, log_gamma[l])
        s5 = _s5_mixer(u_pairs, s5w, l)
        stream = _post(residual, att, ret, s5, mods, l, permt, w_glu[l].astype(MXU_DTYPE), b_glu[l],
                       w_out, ln1_g[l], ln1_b[l], w_ff1, w_ff2, ln2_g[l], ln2_b[l],
                       skip_context=(l == DEPTH - 1))
        residual = (stream,)
    return stream[None]
```

```python
import functools
import math

import jax
import jax.numpy as jnp
from jax import lax
from jax.experimental import pallas as pl
from jax.experimental.pallas import tpu as pltpu

F32 = jnp.float32
MXU_DTYPE = jnp.bfloat16

D_MODEL = 1024
DEPTH = 4
GRID_W = 64
CTX_LEN = 256
CTX_PAD = 512
HEAD_DIM = 64
ATT_HEADS = 8
ATT_KV_HEADS = 2
ATT_BLOCK = 128
ATT_LOOKAHEAD = 2
ROPE_BASE = 10000.0
RET_HEADS = 4
RET_CHUNK = 256
S5_CH = 16
S5_GROUPS = 16
S5_STATE = 64
S5_T = 16
S5_PAIRS = S5_GROUPS // 2
S5_PAIR_W = 2 * S5_CH
S5_TILE = CTX_PAD // S5_T
SCAN_ROWS = 8
S5_STATE_W = 2 * 2 * S5_STATE
ATT_WIDTH = ATT_HEADS * HEAD_DIM
KV_WIDTH = ATT_KV_HEADS * HEAD_DIM
RET_WIDTH = RET_HEADS * HEAD_DIM
S5_WIDTH = S5_GROUPS * S5_CH
IN_WIDTH = ATT_WIDTH + 2 * KV_WIDTH + 4 * RET_WIDTH + S5_WIDTH
D_FF = 4 * D_MODEL
FF_CHUNK = 1024
N_ADA = 6
LN_EPS = 1e-5
GN_EPS = 1e-5
DEEPNORM_ALPHA = (2 * DEPTH) ** 0.25
ROW_TILE = 512
POST_SPLIT = 2
_NFF = D_FF // FF_CHUNK
POST_PROGRAM = (((0, 0), (0, 1), (1, 0), (0, 2), (1, 1)) + tuple((0, 2 + c) for c in range(1, _NFF))
                + ((1, 2), (0, 2 + _NFF)) + tuple((1, 2 + c) for c in range(1, _NFF)) + ((1, 2 + _NFF),))
NEG_BIG = -1e30
LOG2E = math.log2(math.e)
LANES = 128
VMEM_LIMIT = 56 * 1024 * 1024
STAGE_BYTES = 2 * 1024 * 1024

COL_AQ, COL_AK, COL_AV = 0, ATT_WIDTH, ATT_WIDTH + KV_WIDTH
COL_RQ = ATT_WIDTH + 2 * KV_WIDTH
COL_RK, COL_RV, COL_RG = COL_RQ + RET_WIDTH, COL_RQ + 2 * RET_WIDTH, COL_RQ + 3 * RET_WIDTH
COL_S5 = COL_RQ + 4 * RET_WIDTH


def _mm(a, b):
    return jnp.dot(a, b, preferred_element_type=F32)


def _mm_nt(a, b):
    return lax.dot_general(a, b, (((1,), (1,)), ((), ())), preferred_element_type=F32)


def _mm_tn(a, b):
    return lax.dot_general(a, b, (((0,), (0,)), ((), ())), preferred_element_type=F32)


def _params(sem):
    return pltpu.CompilerParams(dimension_semantics=sem, vmem_limit_bytes=VMEM_LIMIT)


def _mod_kernel(cond_ref, w_ref, b_ref, o_ref):
    c = cond_ref[...]
    s = c * jax.nn.sigmoid(c)
    w = w_ref[0]
    split = lambda v: (v.astype(jnp.bfloat16), (v - v.astype(jnp.bfloat16).astype(F32)).astype(jnp.bfloat16))
    s_hi, s_lo = split(s)
    w_hi, w_lo = split(w)
    o_ref[0] = _mm(s_hi, w_hi) + (_mm(s_lo, w_hi) + _mm(s_hi, w_lo)) + b_ref[0]


def _modulation(cond, w_ada, b_ada):
    tn = 1536
    n = N_ADA * D_MODEL
    return pl.pallas_call(
        _mod_kernel,
        grid=(DEPTH, n // tn),
        in_specs=[
            pl.BlockSpec((8, D_MODEL), lambda l, j: (0, 0)),
            pl.BlockSpec((1, D_MODEL, tn), lambda l, j: (l, 0, j)),
            pl.BlockSpec((1, 1, tn), lambda l, j: (l, 0, j)),
        ],
        out_specs=pl.BlockSpec((1, 8, tn), lambda l, j: (l, 0, j)),
        out_shape=jax.ShapeDtypeStruct((DEPTH, 8, n), F32),
        compiler_params=_params(("arbitrary", "arbitrary")),
        name="modulation",
    )(cond, w_ada, b_ada.reshape(DEPTH, 1, n))


def _lane_block_shuffle(src_rows, src_lane_blk, out_vreg):
    acc = None
    for q in range(LANES // S5_PAIR_W):
        piece = src_rows(out_vreg * (LANES // S5_PAIR_W) + q)
        shift = (S5_PAIR_W * (q - src_lane_blk)) % LANES
        if shift:
            piece = pltpu.roll(piece, shift, 1)
        if acc is None:
            acc = piece
        else:
            lane_blk = lax.broadcasted_iota(jnp.int32, piece.shape, 1) // S5_PAIR_W
            acc = jnp.where(lane_blk == q, piece, acc)
    return acc


def _in_proj_kernel(*refs, split, layer):
    nres = 2 if split else 1
    (mod_ref, w_hbm, scale_ref, ca_ref, sa_ref, cr_ref, sr_ref, perm_ref, o_ref, u_ref, w_ref, stage_ref,
     sem_ref) = refs[nres:]

    @pl.when(pl.program_id(0) == 0)
    def _stage_weights():
        _load_cast(w_hbm.at[layer], w_ref, stage_ref, sem_ref, col_scale=scale_ref[...])

    x = _residual_rows(refs[:nres])(slice(None))
    h = (x * (1.0 + mod_ref[1:2, :]) + mod_ref[0:1, :]).astype(w_ref.dtype)
    lane = lax.broadcasted_iota(jnp.int32, (x.shape[0], LANES), 1)
    first_att = lane % (HEAD_DIM // 2) < HEAD_DIM // 4
    first_ret = lane % HEAD_DIM < HEAD_DIM // 2

    def proj(c0, c1):
        return _mm(h, w_ref[:, c0:c1])

    def rope_store(p, c0, width, cos, sin, first, half):
        for b in range(width // LANES):
            blk = p[:, LANES * b:LANES * (b + 1)]
            rot = jnp.where(first, pltpu.roll(blk, LANES - half, 1), pltpu.roll(blk, half, 1))
            o_ref[:, c0 + LANES * b:c0 + LANES * (b + 1)] = (blk * cos + rot * sin).astype(o_ref.dtype)

    def plain_store(p, c0, width):
        o_ref[:, c0:c0 + width] = p.astype(o_ref.dtype)

    u = proj(COL_S5, IN_WIDTH).astype(w_ref.dtype)
    sub = perm_ref.shape[0]
    nchunk = sub // S5_T
    for part in range(x.shape[0] // sub):
        g = _mm(perm_ref[...], u[sub * part:sub * (part + 1)])
        for a in range(S5_PAIRS):
            vreg_col, lane_blk = divmod(a * S5_PAIR_W, LANES)
            lane_blk //= S5_PAIR_W
            for v in range(S5_T * S5_PAIR_W // LANES):
                slab = _lane_block_shuffle(
                    lambda j: g[nchunk * j:nchunk * (j + 1), vreg_col * LANES:(vreg_col + 1) * LANES], lane_blk, v)
                u_ref[a, nchunk * part:nchunk * (part + 1), LANES * v:LANES * (v + 1)] = slab.astype(u_ref.dtype)

    ca, sa, cr, sr = ca_ref[...], sa_ref[...], cr_ref[...], sr_ref[...]
    att_rope = (ca, sa, first_att, HEAD_DIM // 4)
    ret_rope = (cr, sr, first_ret, HEAD_DIM // 2)
    groups = [(COL_AQ, ATT_WIDTH, att_rope), (COL_AK, KV_WIDTH, att_rope), (COL_RQ, RET_WIDTH, ret_rope),
              (COL_RK, RET_WIDTH, ret_rope), (COL_AV, KV_WIDTH, None), (COL_RV, COL_S5 - COL_RV, None)]
    pending = None
    for group in groups + [None]:
        nxt = None if group is None else (proj(group[0], group[0] + group[1]),) + group
        if pending is not None:
            p, pc0, pwidth, prope = pending
            if prope is None:
                plain_store(p, pc0, pwidth)
            else:
                rope_store(p, pc0, pwidth, *prope)
        pending = nxt


def _chunk_perm(tile_rows, dtype):
    nchunk = tile_rows // S5_T
    r = jnp.arange(tile_rows)
    src = S5_T * (r % nchunk) + r // nchunk
    return (src[:, None] == jnp.arange(tile_rows)[None, :]).astype(dtype)


def _in_proj(residual, mods, layer, w_in, col_scale, tabs, perm):
    rows = tabs[0].shape[0]
    tm = ROW_TILE
    tab_spec = pl.BlockSpec((tm, LANES), lambda i: (i, 0))
    nch = rows // S5_T
    return pl.pallas_call(
        functools.partial(_in_proj_kernel, split=len(residual) > 1, layer=layer),
        grid=(rows // tm,),
        in_specs=_residual_specs(residual) + [
            pl.BlockSpec((None, None, N_ADA, D_MODEL), lambda i: (layer, jnp.where(i == 0, 1, 0), 0, 0)),
            pl.BlockSpec(memory_space=pl.ANY),
            pl.BlockSpec((1, IN_WIDTH), lambda i: (0, 0)),
            tab_spec, tab_spec, tab_spec, tab_spec,
            pl.BlockSpec(perm.shape, lambda i: (0, 0)),
        ],
        out_specs=[pl.BlockSpec((tm, COL_S5), lambda i: (i, 0)),
                   pl.BlockSpec((S5_PAIRS, tm // S5_T, S5_T * S5_PAIR_W), lambda i: (0, i, 0))],
        out_shape=[jax.ShapeDtypeStruct((rows, COL_S5), MXU_DTYPE),
                   jax.ShapeDtypeStruct((S5_PAIRS, nch, S5_T * S5_PAIR_W), MXU_DTYPE)],
        scratch_shapes=[pltpu.VMEM((D_MODEL, IN_WIDTH), MXU_DTYPE),
                        pltpu.VMEM((2, STAGE_BYTES // (4 * IN_WIDTH), IN_WIDTH), F32),
                        pltpu.SemaphoreType.DMA((2,))],
        compiler_params=_params(("arbitrary",)),
        name="in_proj",
    )(*residual, mods, w_in, col_scale, *tabs, perm)


def _rope_tables(seq):
    half_a = HEAD_DIM // 4
    half_r = HEAD_DIM // 2
    nrow = seq // GRID_W
    inv_a = ROPE_BASE ** (-jnp.arange(half_a, dtype=F32) / half_a)
    inv_r = ROPE_BASE ** (-jnp.arange(half_r, dtype=F32) / half_r)
    ang_r = jnp.arange(nrow, dtype=F32)[:, None] * inv_a[None, :]
    ang_c = jnp.arange(GRID_W, dtype=F32)[:, None] * inv_a[None, :]
    ang_t = jnp.arange(seq, dtype=F32)[:, None] * inv_r[None, :]
    hp = lax.Precision.HIGHEST
    lane = jnp.arange(LANES)
    within = lane % HEAD_DIM
    pick_a = (within % half_a)[None, :] == jnp.arange(half_a)[:, None]
    exp_row = (pick_a & (within < 2 * half_a)[None, :]).astype(F32)
    exp_col = (pick_a & (within >= 2 * half_a)[None, :]).astype(F32)
    exp_t = ((lane % half_r)[None, :] == jnp.arange(half_r)[:, None]).astype(F32)
    sign_a = jnp.where(within % (2 * half_a) < half_a, -1.0, 1.0).astype(F32)
    sign_r = jnp.where(within < half_r, -1.0, 1.0).astype(F32)

    def att_table(fn):
        by_row = jnp.dot(fn(ang_r), exp_row, precision=hp)
        by_col = jnp.dot(fn(ang_c), exp_col, precision=hp)
        return (by_row[:, None, :] + by_col[None, :, :]).reshape(seq, LANES)

    cos_a = att_table(jnp.cos)
    sin_a = att_table(jnp.sin) * sign_a
    cos_r = jnp.dot(jnp.cos(ang_t), exp_t, precision=hp)
    sin_r = jnp.dot(jnp.sin(ang_t), exp_t, precision=hp) * sign_r
    pad = lambda tab, ident: jnp.pad(tab, ((CTX_PAD, 0), (0, 0)), constant_values=ident)
    return pad(cos_a, 1.0), pad(sin_a, 0.0), pad(cos_r, 1.0), pad(sin_r, 0.0)


def _swap_halves(x):
    if x.dtype.itemsize == 4:
        return pltpu.roll(x, 64, 1)
    packed = pltpu.bitcast(x, jnp.uint32)
    return pltpu.bitcast(pltpu.roll(packed, 64, 1), x.dtype)


def _dup_heads(x):
    sw = _swap_halves(x)
    lo = lax.broadcasted_iota(jnp.int32, x.shape, 1) < HEAD_DIM
    return jnp.where(lo, x, sw), jnp.where(lo, sw, x)


def _attn_kernel(sink_ref, q_ref, km_ref, kp_ref, kn_ref, vm_ref, vp_ref, vn_ref, kc_ref, vc_ref, mask_ref,
                 o_ref, k2_ref, v2_ref, kc2_ref, vc2_ref):
    i = pl.program_id(0)
    last_blk = pl.num_programs(0) * (ROW_TILE // ATT_BLOCK) - 1
    blk = ATT_BLOCK
    def spread(src, ones_upper):
        x = src[...]
        a, b = _dup_heads(x)
        if ones_upper:
            upper = lax.broadcasted_iota(jnp.int32, x.shape, 1) >= HEAD_DIM
            a = jnp.where(upper, jnp.ones_like(a), a)
            b = jnp.where(upper, jnp.ones_like(b), b)
        return a, b

    for dst, parts, is_v in ((k2_ref, (kp_ref, km_ref, kn_ref), False), (v2_ref, (vp_ref, vm_ref, vn_ref), True)):
        row = 0
        for part in parts:
            a, b = spread(part, is_v)
            n = part.shape[0]
            dst[0, row:row + n, :] = a
            dst[1, row:row + n, :] = b
            row += n
    for dst, src, is_v in ((kc2_ref, kc_ref, False), (vc2_ref, vc_ref, True)):
        a, b = spread(src, is_v)
        dst[0] = a
        dst[1] = b

    lo = lax.broadcasted_iota(jnp.int32, (blk, LANES), 1) < HEAD_DIM
    group = ATT_HEADS // ATT_KV_HEADS

    nloc = 3 * blk

    def scores(j, kv):
        r0 = j * blk
        qt = q_ref[r0:r0 + blk, group * HEAD_DIM * kv:group * HEAD_DIM * (kv + 1)]
        parts = []
        for g in range(group):
            qc = qt[:, LANES * (g // 2):LANES * (g // 2 + 1)]
            keep = lo if g % 2 == 0 else jnp.logical_not(lo)
            parts.append(jnp.where(keep, qc, jnp.zeros_like(qc)))
        qs = jnp.concatenate(parts, axis=0)
        return _mm_nt(qs, k2_ref[kv, r0:r0 + nloc, :]), _mm_nt(qs, kc2_ref[kv])

    def finish(j, kv, s_loc, s_ctx):
        r0 = j * blk
        gblk = i * (ROW_TILE // blk) + j
        sel = jnp.where(i == 0, 3, jnp.where(gblk == CTX_PAD // blk, 0, jnp.where(gblk == last_blk, 2, 1)))
        bias = mask_ref[sel]
        probs, sink_w = [], []
        for g in range(group):
            s = jnp.concatenate([s_loc[blk * g:blk * (g + 1)] + bias, s_ctx[blk * g:blk * (g + 1)]], axis=1)
            sk = sink_ref[group * kv + g]
            m = jnp.maximum(jnp.max(s, axis=-1, keepdims=True), sk)
            probs.append(jnp.exp2(s - m).astype(o_ref.dtype))
            sink_w.append(jnp.exp2(sk - m))
        p = jnp.concatenate(probs, axis=0)
        o = _mm(p[:, :nloc], v2_ref[kv, r0:r0 + nloc, :]) + _mm(p[:, nloc:], vc2_ref[kv])
        for half in range(group // 2):
            even, odd = 2 * half, 2 * half + 1
            oe = o[blk * even:blk * (even + 1)]
            oo = o[blk * odd:blk * (odd + 1)]
            y_even = oe * (1.0 / (pltpu.roll(oe, HEAD_DIM, 1) + sink_w[even]))
            y_odd = pltpu.roll(oo, HEAD_DIM, 1) * (1.0 / (oo + sink_w[odd]))
            c0 = group * HEAD_DIM * kv + LANES * half
            o_ref[r0:r0 + blk, c0:c0 + LANES] = jnp.where(lo, y_even, y_odd).astype(o_ref.dtype)

    items = [(j, kv) for j in range(ROW_TILE // blk) for kv in range(ATT_KV_HEADS)]
    pending = {}
    for t in range(len(items) + ATT_LOOKAHEAD):
        if t < len(items):
            pending[t] = scores(*items[t])
        if t >= ATT_LOOKAHEAD:
            finish(*items[t - ATT_LOOKAHEAD], *pending.pop(t - ATT_LOOKAHEAD))


def _attention_masks():
    qi = jnp.arange(ATT_BLOCK)[:, None]
    kj = jnp.arange(3 * ATT_BLOCK)[None, :]
    band = jnp.abs(kj - ATT_BLOCK - qi) <= ATT_BLOCK
    first = band & (kj >= ATT_BLOCK)
    last = band & (kj < 2 * ATT_BLOCK)
    none = jnp.zeros_like(band)
    masks = jnp.stack([first, band, last, none])
    return jnp.where(masks, 0.0, NEG_BIG).astype(F32)


def _attention(proj, sink, masks):
    rows = proj.shape[0]
    tm, blk = ROW_TILE, ATT_BLOCK
    per = tm // blk
    nblk = rows // blk
    ck, cv = COL_AK // KV_WIDTH, COL_AV // KV_WIDTH
    dt = proj.dtype
    return pl.pallas_call(
        _attn_kernel,
        grid=(rows // tm,),
        in_specs=[
            pl.BlockSpec(memory_space=pltpu.SMEM),
            pl.BlockSpec((tm, ATT_WIDTH), lambda i: (i, 0)),
            pl.BlockSpec((tm, KV_WIDTH), lambda i: (i, ck)),
            pl.BlockSpec((blk, KV_WIDTH), lambda i: (jnp.maximum(i * per - 1, 0), ck)),
            pl.BlockSpec((blk, KV_WIDTH), lambda i: (jnp.minimum((i + 1) * per, nblk - 1), ck)),
            pl.BlockSpec((tm, KV_WIDTH), lambda i: (i, cv)),
            pl.BlockSpec((blk, KV_WIDTH), lambda i: (jnp.maximum(i * per - 1, 0), cv)),
            pl.BlockSpec((blk, KV_WIDTH), lambda i: (jnp.minimum((i + 1) * per, nblk - 1), cv)),
            pl.BlockSpec((CTX_LEN, KV_WIDTH), lambda i: (0, ck)),
            pl.BlockSpec((CTX_LEN, KV_WIDTH), lambda i: (0, cv)),
            pl.BlockSpec((4, blk, 3 * blk), lambda i: (0, 0, 0)),
        ],
        out_specs=pl.BlockSpec((tm, ATT_WIDTH), lambda i: (i, 0)),
        out_shape=jax.ShapeDtypeStruct((rows, ATT_WIDTH), dt),
        scratch_shapes=[
            pltpu.VMEM((2, tm + 2 * blk, KV_WIDTH), dt),
            pltpu.VMEM((2, tm + 2 * blk, KV_WIDTH), dt),
            pltpu.VMEM((2, CTX_LEN, KV_WIDTH), dt),
            pltpu.VMEM((2, CTX_LEN, KV_WIDTH), dt),
        ],
        compiler_params=_params(("arbitrary",)),
        name="attention",
    )(sink, proj, proj, proj, proj, proj, proj, proj, proj, proj, masks)


def _ret_kernel(*refs, sup, ntile):
    lg_ref, q_ref = refs[:2]
    k_refs, v_refs = refs[2:2 + sup], refs[2 + sup:2 + 2 * sup]
    g_ref, o_ref, sb_ref, s_ref, dm_ref, tab_ref, gbd_ref = refs[2 + 2 * sup:]
    k_ref, v_ref = k_refs[0], v_refs[0]
    step = pl.program_id(0)
    nsup = (ntile - 1) // sup
    ph = jnp.where(step <= nsup, 0, 1)
    t = jnp.where(step <= nsup, step, step - (nsup + 1))
    c = RET_CHUNK
    w = RET_WIDTH
    per = q_ref.shape[0] // c
    mdt = sb_ref.dtype
    rows = lambda ci: slice(c * ci, c * (ci + 1))

    def lane_vec(direction, shape, axis):
        head = lax.broadcasted_iota(jnp.int32, shape, axis) // HEAD_DIM
        out = jnp.full(shape, lg_ref[direction, RET_HEADS - 1], F32)
        for h in range(RET_HEADS - 2, -1, -1):
            out = jnp.where(head == h, lg_ref[direction, h], out)
        return out

    @pl.when(jnp.logical_and(ph == 0, t == 0))
    def _init_tables():
        diff = (lax.broadcasted_iota(jnp.int32, (c, c), 0) - lax.broadcasted_iota(jnp.int32, (c, c), 1)).astype(F32)
        for h in range(RET_HEADS):
            dm_ref[h] = jnp.exp(jnp.where(diff >= 0, diff * lg_ref[0, h], -diff * lg_ref[1, h]))
        pos = lax.broadcasted_iota(jnp.int32, (c, w), 0).astype(F32)
        lgf = lane_vec(0, (c, w), 1)
        lgb = lane_vec(1, (c, w), 1)
        tab_ref[0] = jnp.exp((c - 1.0 - pos) * lgf)
        tab_ref[1] = jnp.exp((pos + 1.0) * lgf)
        tab_ref[2] = jnp.exp(pos * lgb)
        tab_ref[3] = jnp.exp((c - pos) * lgb)
        same = (lax.broadcasted_iota(jnp.int32, (w, w), 0) // HEAD_DIM
                == lax.broadcasted_iota(jnp.int32, (w, w), 1) // HEAD_DIM)
        bd = jnp.where(same, 1.0, 0.0)
        gbd_ref[0] = bd * jnp.exp(c * lane_vec(0, (w, w), 0))
        gbd_ref[1] = bd * jnp.exp(c * lane_vec(1, (w, w), 0))
        gbd_ref[2] = bd

    @pl.when(t == 0)
    def _reset_state():
        s_ref[...] = jnp.zeros_like(s_ref)

    def state_update(direction, key_tab, ci, kr=k_ref, vr=v_ref):
        kw = (kr[rows(ci), :].astype(F32) * tab_ref[key_tab]).astype(mdt)
        u = _mm_tn(kw, vr[rows(ci), :])
        s_ref[...] = gbd_ref[direction] * s_ref[...] + gbd_ref[2] * u

    @pl.when(jnp.logical_and(ph == 0, t == 0))
    def _backward_context():
        sb_ref[0] = s_ref[...].astype(mdt)
        state_update(1, 2, 0)

    @pl.when(jnp.logical_and(ph == 0, t > 0))
    def _backward_latent():
        base = 1 + per * sup * (nsup - t)
        for qi in range(sup - 1, -1, -1):
            for ci in range(per - 1, -1, -1):
                sb_ref[base + per * qi + ci] = s_ref[...].astype(mdt)
                state_update(1, 2, ci, k_refs[qi], v_refs[qi])

    head = lax.broadcasted_iota(jnp.int32, (c, w), 1) // HEAD_DIM

    def scores(ci):
        q = q_ref[rows(ci), :]
        qs = jnp.concatenate([jnp.where(head == h, q, jnp.zeros_like(q)) for h in range(RET_HEADS)], axis=0)
        return _mm_nt(qs, k_ref[rows(ci), :])

    def intra(ci, sc):
        scd = jnp.concatenate([sc[c * h:c * (h + 1)] * dm_ref[h] for h in range(RET_HEADS)], axis=0).astype(mdt)
        oi = _mm(scd, v_ref[rows(ci), :])
        o = jnp.where(head == 0, oi[0:c], 0.0)
        for h in range(1, RET_HEADS):
            o = o + jnp.where(head == h, oi[c * h:c * (h + 1)], 0.0)
        return o

    def cross(ci, idx):
        qf = q_ref[rows(ci), :].astype(F32)
        return (_mm((qf * tab_ref[1]).astype(mdt), s_ref[...].astype(mdt))
                + _mm((qf * tab_ref[3]).astype(mdt), sb_ref[idx]))

    def finish(ci, o):
        avg = (gbd_ref[2] * (1.0 / HEAD_DIM)).astype(mdt)
        o_hi = o.astype(mdt)
        d = o - (_mm(o_hi, avg) + _mm((o - o_hi.astype(F32)).astype(mdt), avg))
        var = _mm((d * d).astype(mdt), avg)
        gate = g_ref[rows(ci), :].astype(F32)
        y = d * lax.rsqrt(var + GN_EPS) * (gate * jax.nn.sigmoid(gate))
        o_ref[rows(ci), :] = y.astype(o_ref.dtype)

    @pl.when(jnp.logical_and(ph == 1, t == 0))
    def _forward_context():
        o = intra(0, scores(0)) + cross(0, 0)
        state_update(0, 0, 0)
        finish(0, o)
        for ci in range(1, per):
            o_ref[rows(ci), :] = jnp.zeros((c, w), o_ref.dtype)

    @pl.when(jnp.logical_and(ph == 1, t > 0))
    def _forward_latent():
        base = 1 + per * (t - 1)
        sc = [scores(ci) for ci in range(per)]
        outs = [intra(ci, sc[ci]) for ci in range(per)]
        for ci in range(per):
            outs[ci] = outs[ci] + cross(ci, base + ci)
            state_update(0, 0, ci)
        for ci in range(per):
            finish(ci, outs[ci])


def _retention(proj, log_gamma):
    rows = proj.shape[0]
    c = RET_CHUNK
    tm = ROW_TILE
    ntile = rows // tm
    nchunk = 1 + (rows - CTX_PAD) // c
    dt = proj.dtype
    cq, ckk, cvv, cg = (COL_RQ // RET_WIDTH, COL_RK // RET_WIDTH, COL_RV // RET_WIDTH, COL_RG // RET_WIDTH)

    nlat = ntile - 1
    sup = next(s for s in (4, 2, 1) if nlat % s == 0)
    nsup = nlat // sup
    nback = 1 + nsup

    def fw_blk(step):
        return jnp.maximum(step - nback, 0)

    def kv_spec(col, qi):
        def index(step):
            back = jnp.where(step == 0, 0, 1 + sup * (nsup - jnp.minimum(step, nsup)) + qi)
            fwd = step - nback if qi == 0 else 1 + qi
            return jnp.where(step < nback, back, fwd), col
        return pl.BlockSpec((tm, RET_WIDTH), index)

    return pl.pallas_call(
        functools.partial(_ret_kernel, sup=sup, ntile=ntile),
        grid=(nback + ntile,),
        in_specs=[pl.BlockSpec(memory_space=pltpu.SMEM),
                  pl.BlockSpec((tm, RET_WIDTH), lambda step: (fw_blk(step), cq))]
                 + [kv_spec(ckk, qi) for qi in range(sup)] + [kv_spec(cvv, qi) for qi in range(sup)]
                 + [pl.BlockSpec((tm, RET_WIDTH), lambda step: (fw_blk(step), cg))],
        out_specs=pl.BlockSpec((tm, RET_WIDTH), lambda step: (fw_blk(step), 0)),
        out_shape=jax.ShapeDtypeStruct((rows, RET_WIDTH), dt),
        scratch_shapes=[
            pltpu.VMEM((nchunk, RET_WIDTH, RET_WIDTH), dt),
            pltpu.VMEM((RET_WIDTH, RET_WIDTH), F32),
            pltpu.VMEM((RET_HEADS, c, c), F32),
            pltpu.VMEM((4, c, RET_WIDTH), F32),
            pltpu.VMEM((3, RET_WIDTH, RET_WIDTH), F32),
        ],
        compiler_params=_params(("arbitrary",)),
        name="retention",
    )(log_gamma, proj, *([proj] * (2 * sup)), proj)


def _s5_weights(lam_re, lam_im, b_re, b_im, c_re, c_im, log_dt, d_skip):
    tt, g, n, p, a = S5_T, S5_GROUPS, S5_STATE, S5_CH, S5_PAIRS
    lam = lax.complex(lam_re.astype(F32), lam_im.astype(F32))
    dtv = jnp.exp(log_dt.astype(F32))[..., None]
    lam_bar = jnp.exp(lam * dtv)
    bbar = ((lam_bar - 1.0) / lam)[..., None] * lax.complex(b_re.astype(F32), b_im.astype(F32))
    cmat = lax.complex(c_re.astype(F32), c_im.astype(F32))
    pw = [jnp.ones_like(lam_bar)]
    for _ in range(tt):
        pw.append(pw[-1] * lam_bar)
    pw = jnp.stack(pw, axis=1)
    eye2 = jnp.eye(2, dtype=F32)
    ri = lambda z, axis: jnp.stack([jnp.real(z), jnp.imag(z)], axis=axis)

    pw_l = pw.reshape(2, tt + 1, a, 2 * n)
    bbt = jnp.einsum('dahpn,gh->dagphn', jnp.swapaxes(bbar, -1, -2).reshape(2, a, 2, p, n), eye2)
    bbt = bbt.reshape(2, a, 2 * p, 2 * n)
    cct = jnp.einsum('dahpn,gh->dagphn', cmat.reshape(2, a, 2, p, n), eye2).reshape(2, a, 2 * p, 2 * n)
    pw_k = ri(pw_l, 1).transpose(3, 0, 1, 2, 4)
    b_k = ri(bbt, 2).transpose(1, 0, 2, 3, 4)
    c_k = ri(cct, 2).transpose(1, 0, 2, 3, 4)
    decay = [pw_l[:, tt]]
    for _ in range(SCAN_ROWS - 1):
        decay.append(decay[-1] * decay[0])
    decay = jnp.stack(decay, axis=0)
    rows8 = lambda z: jnp.broadcast_to(z[None], (SCAN_ROWS,) + z.shape)
    carry_w = jnp.stack([decay[:, 0], decay[::-1, 1]], axis=1)
    r8 = jnp.arange(SCAN_ROWS)
    keep = lambda k: jnp.stack([r8 >= k, r8 < SCAN_ROWS - k], axis=1)[:, :, None, None]
    scan_tab = jnp.stack([jnp.where(keep(k), rows8(decay[k - 1]), 0.0) for k in (1, 2, 4)] + [carry_w],
                         axis=0)
    scan_tab = ri(scan_tab, 0).transpose(4, 3, 1, 0, 2, 5)
    skip = jnp.tile(d_skip.astype(F32).reshape(a, 1, 2 * p), (1, tt, 1)).reshape(a, 1, tt * 2 * p)
    return pw_k, b_k, c_k, scan_tab, skip


def _s5_drive_kernel(u_ref, pw_ref, b_ref, o_ref, w_ref):
    rows, half = S5_PAIR_W, LANES
    for d in range(2):
        br, bi = b_ref[d, 0], b_ref[d, 1]
        for j in range(S5_T):
            e = S5_T - 1 - j if d == 0 else j
            pr, pi = pw_ref[d, 0, e:e + 1, :], pw_ref[d, 1, e:e + 1, :]
            w_ref[rows * j:rows * (j + 1), 2 * half * d:2 * half * d + half] = (pr * br - pi * bi).astype(w_ref.dtype)
            w_ref[rows * j:rows * (j + 1), 2 * half * d + half:2 * half * (d + 1)] = (
                pr * bi + pi * br).astype(w_ref.dtype)
    o_ref[0] = _mm(u_ref[0], w_ref[...])


def _lane_window(x, start, width):
    cols = []
    for v in range(width // LANES):
        k0, off = divmod(start + LANES * v, LANES)
        lo = x[:, LANES * k0:LANES * (k0 + 1)]
        if off:
            hi = x[:, LANES * (k0 + 1):LANES * (k0 + 2)]
            lane = lax.broadcasted_iota(jnp.int32, lo.shape, 1)
            lo = jnp.where(lane < LANES - off, pltpu.roll(lo, LANES - off, 1), pltpu.roll(hi, LANES - off, 1))
        cols.append(lo)
    return jnp.concatenate(cols, axis=1)


def _s5_read_kernel(u_ref, sf_ref, sb_ref, pw_ref, b_ref, c_ref, skip_ref, o_ref, wt_ref, wi_ref, lag_ref):
    u = u_ref[0]
    mdt = u.dtype
    rows, half = S5_PAIR_W, LANES
    for d in range(2):
        cr, ci = c_ref[d, 0], c_ref[d, 1]
        for i in range(S5_T):
            e = i + 1 if d == 0 else S5_T - i
            pr, pi = pw_ref[d, 0, e:e + 1, :], pw_ref[d, 1, e:e + 1, :]
            wt_ref[d, rows * i:rows * (i + 1), 0:half] = (pr * cr - pi * ci).astype(mdt)
            wt_ref[d, rows * i:rows * (i + 1), half:2 * half] = (-(pr * ci + pi * cr)).astype(mdt)
    nlag = 2 * S5_T - 1
    ldt = lag_ref.dtype
    zero = jnp.zeros((rows, half), ldt)
    for l in range(nlag + 1):
        lag = l - (S5_T - 1)
        for d, active in ((1, lag <= 0), (0, 0 <= lag < S5_T)):
            col = 2 * half * (1 - d)
            if active:
                cr, ci = c_ref[d, 0], c_ref[d, 1]
                pr, pi = pw_ref[d, 0, abs(lag):abs(lag) + 1, :], pw_ref[d, 1, abs(lag):abs(lag) + 1, :]
                lag_ref[rows * l:rows * (l + 1), col:col + half] = (pr * cr - pi * ci).astype(ldt)
                lag_ref[rows * l:rows * (l + 1), col + half:col + 2 * half] = (pr * ci + pi * cr).astype(ldt)
            else:
                lag_ref[rows * l:rows * (l + 1), col:col + half] = zero
                lag_ref[rows * l:rows * (l + 1), col + half:col + 2 * half] = zero
    lhs = jnp.concatenate([b_ref[1, 0], -b_ref[1, 1], b_ref[0, 0], -b_ref[0, 1]], axis=1).astype(ldt)
    kall = _mm_nt(lhs, lag_ref[...])
    for j in range(S5_T):
        wi_ref[rows * j:rows * (j + 1), :] = _lane_window(kall, rows * (S5_T - 1 - j), S5_T * rows).astype(mdt)
    y = _mm(u, wi_ref[...])
    y = y + _mm_nt(sf_ref[0].astype(mdt), wt_ref[0])
    y = y + _mm_nt(sb_ref[0].astype(mdt), wt_ref[1])
    o_ref[0] = y + u.astype(F32) * skip_ref[...]


def _s5_fused_kernel(u_ref, pw_ref, b_ref, c_ref, tab_ref, skip_ref, o_ref, e_sc, sf_sc, sb_sc, w_sc, wt_sc, wi_sc,
                     lag_sc):
    step = pl.program_id(0)
    npair = S5_PAIRS
    nch = u_ref.shape[1]
    hw, sub = LANES, SCAN_ROWS
    span = 2 * sub
    ctx_rows = CTX_LEN // S5_T

    @pl.when(step < npair)
    def _drive():
        _s5_drive_kernel(u_ref, pw_ref, b_ref, e_sc.at[pl.ds(step, 1)], w_sc)

    @pl.when(step == npair)
    def _scan():
        row = lax.broadcasted_iota(jnp.int32, (sub, hw), 0)

        def scan_group(a, d, r0, cr, ci):
            reverse = d == 1
            xr = e_sc[a, pl.ds(r0, sub), 2 * hw * d:2 * hw * d + hw]
            xi = e_sc[a, pl.ds(r0, sub), 2 * hw * d + hw:2 * hw * (d + 1)]
            for k_i, k in enumerate((1, 2, 4)):
                ar, ai = tab_ref[a, d, k_i, 0], tab_ref[a, d, k_i, 1]
                amount = sub - k if reverse else k
                sr, si = pltpu.roll(xr, amount, 0), pltpu.roll(xi, amount, 0)
                xr, xi = xr + ar * sr - ai * si, xi + ar * si + ai * sr
            wr, wi = tab_ref[a, d, 3, 0], tab_ref[a, d, 3, 1]
            fr = xr + wr * cr - wi * ci
            fi = xi + wr * ci + wi * cr
            edge = sub - 1 if reverse else 0
            one = sub - 1 if reverse else 1
            before = (jnp.where(row == edge, cr, pltpu.roll(fr, one, 0)),
                      jnp.where(row == edge, ci, pltpu.roll(fi, one, 0)))
            last = 0 if reverse else sub - 1
            after = (jnp.broadcast_to(fr[last:last + 1], (sub, hw)), jnp.broadcast_to(fi[last:last + 1], (sub, hw)))
            return before, after

        def store(dst, a, r0, lower, upper):
            for part in range(2):
                val = jnp.concatenate([lower[part], upper[part]], axis=0)
                dst[a, pl.ds(r0, span), part * hw:(part + 1) * hw] = val.astype(dst.dtype)

        def do_span(carry, rf0, rb0):
            new = []
            for a in range(npair):
                fr, fi, br, bi = carry[a]
                f_lo, (fr, fi) = scan_group(a, 0, rf0, fr, fi)
                f_hi, (fr, fi) = scan_group(a, 0, rf0 + sub, fr, fi)
                store(sf_sc, a, rf0, f_lo, f_hi)
                b_hi, (br, bi) = scan_group(a, 1, rb0 + sub, br, bi)
                b_lo, (br, bi) = scan_group(a, 1, rb0, br, bi)
                store(sb_sc, a, rb0, b_lo, b_hi)
                new.append((fr, fi, br, bi))
            return tuple(new)

        sf_sc[:, ctx_rows:S5_TILE, :] = jnp.zeros((npair, S5_TILE - ctx_rows, 2 * hw), sf_sc.dtype)
        sb_sc[:, ctx_rows:S5_TILE, :] = jnp.zeros((npair, S5_TILE - ctx_rows, 2 * hw), sb_sc.dtype)
        zero = jnp.zeros((sub, hw), F32)
        carry = tuple((zero, zero, zero, zero) for _ in range(npair))
        for g in range(ctx_rows // span):
            carry = do_span(carry, span * g, ctx_rows - span * (g + 1))

        def body(gi, carry):
            rf0 = pl.multiple_of(S5_TILE + gi * span, span)
            rb0 = pl.multiple_of(nch - span - gi * span, span)
            return do_span(carry, rf0, rb0)

        lax.fori_loop(0, (nch - S5_TILE) // span, body, carry)

    @pl.when(step > npair)
    def _read():
        pair = step - npair - 1
        _s5_read_kernel(u_ref, sf_sc.at[pl.ds(pair, 1)], sb_sc.at[pl.ds(pair, 1)], pw_ref, b_ref, c_ref, skip_ref,
                        o_ref, wt_sc, wi_sc, lag_sc)


def _s5_mixer(u_pairs, weights, layer):
    pw_k, b_k, c_k, scan_tab, skip = weights
    a, nch, wd = u_pairs.shape
    dt = u_pairs.dtype

    def in_pair(step):
        return jnp.where(step < a, step, jnp.where(step == a, a - 1, step - a - 1))

    pair_blk = lambda arr: pl.BlockSpec((None, None) + arr.shape[2:],
                                        lambda s: (layer, in_pair(s)) + (0,) * (arr.ndim - 2))
    return pl.pallas_call(
        _s5_fused_kernel,
        grid=(2 * a + 1,),
        in_specs=[pl.BlockSpec((1, nch, wd), lambda s: (in_pair(s), 0, 0)),
                  pair_blk(pw_k), pair_blk(b_k), pair_blk(c_k),
                  pl.BlockSpec((None,) + scan_tab.shape[1:], lambda s: (layer,) + (0,) * (scan_tab.ndim - 1)),
                  pair_blk(skip)],
        out_specs=pl.BlockSpec((1, nch, wd), lambda s: (jnp.maximum(s - a - 1, 0), 0, 0)),
        out_shape=jax.ShapeDtypeStruct((a, nch, wd), F32),
        scratch_shapes=[pltpu.VMEM((a, nch, 2 * S5_STATE_W), F32),
                        pltpu.VMEM((a, nch, S5_STATE_W), dt), pltpu.VMEM((a, nch, S5_STATE_W), dt),
                        pltpu.VMEM((wd, 2 * S5_STATE_W), dt),
                        pltpu.VMEM((2, wd, S5_STATE_W), dt), pltpu.VMEM((wd, wd), dt),
                        pltpu.VMEM((2 * wd, 2 * S5_STATE_W), dt)],
        compiler_params=_params(("arbitrary",)),
        name="s5",
    )(u_pairs, pw_k, b_k, c_k, scan_tab, skip)


def _layer_norm(x, g, b):
    mu = jnp.mean(x, axis=-1, keepdims=True)
    d = x - mu
    var = jnp.mean(d * d, axis=-1, keepdims=True)
    return d * lax.rsqrt(var + LN_EPS) * g + b


def _residual_rows(refs):
    if len(refs) == 1:
        return lambda rows: refs[0][rows, :]
    head_ref, body_ref = refs
    is_head = pl.program_id(0) == 0
    return lambda rows: jnp.where(is_head, head_ref[rows, :], body_ref[rows, :])


def _load_cast(src_hbm, dst_ref, stage_ref, sems, col_scale=None):
    chunk = stage_ref.shape[1]
    nchunk = src_hbm.shape[0] // chunk

    def copy(k):
        slot = k % 2
        return pltpu.make_async_copy(src_hbm.at[pl.ds(k * chunk, chunk), :], stage_ref.at[slot], sems.at[slot])

    copy(0).start()
    for k in range(nchunk):
        if k + 1 < nchunk:
            copy(k + 1).start()
        copy(k).wait()
        vals = stage_ref[k % 2]
        if col_scale is not None:
            vals = vals * col_scale
        dst_ref[k * chunk:(k + 1) * chunk, :] = vals.astype(dst_ref.dtype)


def _post_kernel(*refs, split, layer):
    nres = 2 if split else 1
    load_x = _residual_rows(refs[:nres])
    (att_ref, ret_ref, s5_ref, mod_ref, permt_ref, wglu_ref, bglu_ref, wo_hbm, g1_ref, b1_ref, w1_hbm, w2_hbm,
     g2_ref, b2_ref, o_ref, wo_ref, w1_ref, w2_ref, stage_ref, sem_ref) = refs[nres:]
    woa_ref = wo_ref.at[0:ATT_WIDTH]
    wor_ref = wo_ref.at[ATT_WIDTH:ATT_WIDTH + RET_WIDTH]
    wos_ref = wo_ref.at[ATT_WIDTH + RET_WIDTH:ATT_WIDTH + RET_WIDTH + S5_WIDTH]
    mdt = w1_ref.dtype
    sub = o_ref.shape[0] // POST_SPLIT
    csub = sub // S5_T
    nff = D_FF // FF_CHUNK

    blocks = lambda c: slice(FF_CHUNK * c, FF_CHUNK * (c + 1))
    jobs = [(wo_hbm.at[layer], wo_ref)]
    for c in range(nff):
        jobs.append((w1_hbm.at[layer, :, blocks(c)], w1_ref.at[:, blocks(c)]))
        jobs.append((w2_hbm.at[layer, blocks(c), :], w2_ref.at[blocks(c), :]))

    def job_copy(k):
        return pltpu.make_async_copy(jobs[k][0], stage_ref.at[k % 2], sem_ref.at[k % 2])

    def job_finish(k):
        job_copy(k).wait()
        jobs[k][1][...] = stage_ref[k % 2].astype(mdt)
        if k + 2 < len(jobs):
            job_copy(k + 2).start()

    def mix(part):
        rows = slice(sub * part, sub * (part + 1))
        zrows = []
        for i in range(S5_T):
            src_vreg, src_blk = divmod(i * S5_PAIR_W, LANES)
            src_blk //= S5_PAIR_W
            cols = [_lane_block_shuffle(
                lambda a: s5_ref[a, csub * part:csub * (part + 1), src_vreg * LANES:(src_vreg + 1) * LANES],
                src_blk, w) for w in range(S5_WIDTH // LANES)]
            zrows.append(jnp.concatenate(cols, axis=1))
        hs = jax.nn.gelu(jnp.concatenate(zrows, axis=0))
        gate = jax.nn.sigmoid(_mm(hs.astype(mdt), wglu_ref[...]) + bglu_ref[...])
        s5 = _mm(permt_ref[...], (hs * gate).astype(mdt)).astype(mdt)
        return _mm(att_ref[rows, :], woa_ref[...]) + _mm(ret_ref[rows, :], wor_ref[...]) + _mm(s5, wos_ref[...])

    def norm1(part, ox):
        rows = slice(sub * part, sub * (part + 1))
        x1 = _layer_norm(DEEPNORM_ALPHA * load_x(rows) + mod_ref[2:3, :] * ox, g1_ref[...], b1_ref[...])
        return x1, (x1 * (1.0 + mod_ref[4:5, :]) + mod_ref[3:4, :]).astype(mdt)

    def ff(h, c):
        a = _mm(h, w1_ref[:, FF_CHUNK * c:FF_CHUNK * (c + 1)])
        a = jnp.square(jnp.maximum(a, 0.0)).astype(mdt)
        return _mm(a, w2_ref[FF_CHUNK * c:FF_CHUNK * (c + 1), :])

    def norm2(part, x1, acc):
        rows = slice(sub * part, sub * (part + 1))
        o_ref[rows, :] = _layer_norm(DEEPNORM_ALPHA * x1 + mod_ref[5:6, :] * acc, g2_ref[...], b2_ref[...])

    nstage = nff + 3

    def run(staging):
        state = [dict() for _ in range(POST_SPLIT)]
        done = -1
        if staging:
            job_copy(0).start()
            job_copy(1).start()
        for part, stage in POST_PROGRAM:
            st = state[part]
            if staging:
                need = 0 if stage < 2 else 2 * (stage - 2) + 2 if stage < nstage - 1 else done
                while done < need:
                    done += 1
                    job_finish(done)
            if stage == 0:
                st["ox"] = mix(part)
            elif stage == 1:
                st["x1"], st["h"] = norm1(part, st.pop("ox"))
            elif stage < nstage - 1:
                term = ff(st["h"], stage - 2)
                st["acc"] = term if stage == 2 else st["acc"] + term
            else:
                norm2(part, st["x1"], st["acc"])
        assert not staging or done == len(jobs) - 1

    @pl.when(pl.program_id(0) == 0)
    def _first_step():
        run(staging=True)

    @pl.when(pl.program_id(0) > 0)
    def _other_steps():
        run(staging=False)


def _residual_specs(residual):
    tm = ROW_TILE
    if len(residual) == 1:
        return [pl.BlockSpec((tm, D_MODEL), lambda i: (i, 0))]
    head_tiles = CTX_PAD // tm
    return [pl.BlockSpec((tm, D_MODEL), lambda i: (jnp.minimum(i, head_tiles - 1), 0)),
            pl.BlockSpec((tm, D_MODEL), lambda i: (jnp.maximum(i - head_tiles, 0), 0))]


def _post(residual, att, ret, s5_pairs, mods, layer, permt, wglu, bglu, wo, g1, b1, w1, w2, g2, b2, skip_context):
    rows = att.shape[0]
    tm = ROW_TILE
    off = CTX_PAD // tm if skip_context else 0
    assert not (skip_context and len(residual) > 1)
    assert FF_CHUNK == D_MODEL == wo.shape[1]
    row_blk = lambda width: pl.BlockSpec((tm, width), lambda i: (i + off, 0))
    full = lambda arr: pl.BlockSpec(arr.shape, lambda i: (0,) * arr.ndim)
    hbm = lambda arr: pl.BlockSpec(memory_space=pl.ANY)
    vec = lambda v: v.reshape(1, -1).astype(F32)
    small = [(permt, full), (wglu, full), (vec(bglu), full), (wo, hbm), (vec(g1), full), (vec(b1), full),
             (w1, hbm), (w2, hbm), (vec(g2), full), (vec(b2), full)]
    res_specs = [row_blk(D_MODEL)] if skip_context else _residual_specs(residual)
    return pl.pallas_call(
        functools.partial(_post_kernel, split=len(residual) > 1, layer=layer),
        grid=(rows // tm - off,),
        in_specs=res_specs + [row_blk(ATT_WIDTH), row_blk(RET_WIDTH),
                              pl.BlockSpec((S5_PAIRS, tm // S5_T, S5_T * S5_PAIR_W), lambda i: (0, i + off, 0)),
                              pl.BlockSpec((None, None, N_ADA, D_MODEL),
                                           lambda i: (layer, jnp.where(i + off == 0, 1, 0), 0, 0))]
                 + [spec(arr) for arr, spec in small],
        out_specs=pl.BlockSpec((tm, D_MODEL), lambda i: (i, 0)),
        out_shape=jax.ShapeDtypeStruct((rows - off * tm, D_MODEL), F32),
        scratch_shapes=[pltpu.VMEM((ATT_WIDTH + RET_WIDTH + S5_WIDTH, D_MODEL), MXU_DTYPE),
                        pltpu.VMEM((D_MODEL, D_FF), MXU_DTYPE), pltpu.VMEM((D_FF, D_MODEL), MXU_DTYPE),
                        pltpu.VMEM((2, D_MODEL, FF_CHUNK), F32),
                        pltpu.SemaphoreType.DMA((2,))],
        compiler_params=_params(("arbitrary",)),
        name="post",
    )(*residual, att, ret, s5_pairs, mods, *[arr for arr, _ in small])


def kernel(x, c, ctx, c_ctx, w_ada, b_ada, w_in, att_sink, ret_decay_logit, s5_lambda_re, s5_lambda_im, s5_b_re,
           s5_b_im, s5_c_re, s5_c_im, s5_log_dt, s5_d, w_glu, b_glu, w_out, ln1_g, ln1_b, w_ff1, w_ff2, ln2_g,
           ln2_b):
    assert x.shape[0] == 1 and x.shape[2] == D_MODEL and ctx.shape[1] == CTX_LEN
    seq = x.shape[1]
    assert seq % ROW_TILE == 0
    residual = (jnp.pad(ctx[0], ((0, CTX_PAD - CTX_LEN), (0, 0))), x[0])
    cond = jnp.zeros((8, D_MODEL), F32).at[0].set(c[0]).at[1].set(c_ctx)
    mods = _modulation(cond, w_ada, b_ada).reshape(DEPTH, 8, N_ADA, D_MODEL)
    tabs = _rope_tables(seq)
    masks = _attention_masks()
    perm = _chunk_perm(ROW_TILE // POST_SPLIT, MXU_DTYPE)
    permt = perm.T
    col_scale = jnp.ones((IN_WIDTH,), F32).at[COL_AQ:COL_AK].set(HEAD_DIM ** -0.5 * LOG2E)
    col_scale = col_scale.at[COL_RQ:COL_RK].set(HEAD_DIM ** -0.5)
    s5w = jax.vmap(_s5_weights)(s5_lambda_re, s5_lambda_im, s5_b_re, s5_b_im, s5_c_re, s5_c_im, s5_log_dt, s5_d)
    log_gamma = jax.nn.log_sigmoid(ret_decay_logit.astype(F32))
    for l in range(DEPTH):
        proj, u_pairs = _in_proj(residual, mods, l, w_in, col_scale.reshape(1, IN_WIDTH), tabs, perm)
        att = _attention(proj, att_sink[l].astype(F32) * LOG2E, masks)
        ret = _retention(proj, log_gamma[l])
        s5 = _s5_mixer(u_pairs, s5w, l)
        stream = _post(residual, att, ret, s5, mods, l, permt, w_glu[l].astype(MXU_DTYPE), b_glu[l],
                       w_out, ln1_g[l], ln1_b[l], w_ff1, w_ff2, ln2_g[l], ln2_b[l],
                       skip_context=(l == DEPTH - 1))
        residual = (stream,)
    return stream[None]
```
